```python
import math
import jax
import jax.numpy as jnp
from jax import lax
import numpy as np

D_MODEL = 1024
BATCH = 8
SEQ = 2048
DEPTH = 2
DEC_BATCH = 128
DEC_SEQ = 8
PAST_LEN = 16384
PAGE_SIZE = 128

HEAD_DIM = 64
D_MIX = D_MODEL
N_MIX_HEADS = D_MIX // HEAD_DIM
H_B = N_MIX_HEADS // 4
H_A = (N_MIX_HEADS - H_B) // 2
H_C = N_MIX_HEADS - H_A - H_B
W_A = H_A * HEAD_DIM
W_B = H_B * HEAD_DIM
W_C = H_C * HEAD_DIM
DECAY_LORA = 64
AAA_LORA = 64
GATE_LORA = 128
COLS_A = 3 * W_A + DECAY_LORA + AAA_LORA + GATE_LORA
COLS_B = 4 * W_B
N_STATE = 128
N_GROUPS = 2
CONV_W = 4
CONV_DIM = W_C + 2 * N_GROUPS * N_STATE
COLS_C = W_C + CONV_DIM + H_C
IN_COLS = COLS_A + COLS_B + COLS_C
CHUNK = 64
ROPE_BASE = 10000.0
D_FF = ((8 * D_MODEL // 3 + 127) // 128) * 128
N_EXPERTS = 8
TOP_K = 2
D_FF_EXPERT = D_FF // TOP_K
N_DENSE = (DEPTH + 1) // 2
N_MOE = DEPTH // 2
RMS_EPS = 1e-6
GN_EPS = 64e-5
GATED_NORM_EPS = 1e-5

kernel_name = 'hybrid_rwkv7_retnet_mamba2_step'


def rmsnorm(x, g, eps=RMS_EPS):
    xf = x.astype(jnp.float32)
    y = xf * lax.rsqrt(jnp.mean(xf * xf, axis=-1, keepdims=True) + eps)
    return (y * g.astype(jnp.float32)).astype(x.dtype)


def chunk_len(T):
    return CHUNK if T % CHUNK == 0 else T


def to_chunks(t, L):
    b, T = t.shape[0], t.shape[1]
    return jnp.moveaxis(t.reshape((b, T // L, L) + t.shape[2:]), 1, 0)


def from_chunks(t):
    t = jnp.moveaxis(t, 0, 1)
    return t.reshape((t.shape[0], t.shape[1] * t.shape[2]) + t.shape[3:])


def rwkv7_recurrence(r, decay, k, v, kk, a, S0):
    def step(S, inp):
        r_t, w_t, k_t, v_t, kk_t, a_t = inp
        sa = jnp.einsum('bhvk,bhk->bhv', S, -kk_t)
        S = (S * w_t[:, :, None, :] + sa[..., None] * (kk_t * a_t)[:, :, None, :]
             + v_t[..., None] * k_t[:, :, None, :])
        return S, jnp.einsum('bhvk,bhk->bhv', S, r_t)
    xs = tuple(jnp.moveaxis(t, 1, 0) for t in (r, decay, k, v, kk, a))
    S, ys = lax.scan(step, S0, xs)
    return jnp.moveaxis(ys, 0, 1), S


def rwkv7_mixer(c, c_prev, S0, mu, w0, w_up, a0, a_up, g_up, k_k, k_a, r_k, ln_w, ln_b):
    f32 = jnp.float32
    b, T, _ = c.shape
    prev = jnp.concatenate([c_prev[:, None, :].astype(c.dtype), c[:, :-1]], axis=1)
    xm = c + (prev - c) * mu
    r = xm[..., :W_A]
    k = xm[..., W_A:2 * W_A]
    v = xm[..., 2 * W_A:3 * W_A]
    o = 3 * W_A
    wd = xm[..., o:o + DECAY_LORA]
    o += DECAY_LORA
    ad = xm[..., o:o + AAA_LORA]
    o += AAA_LORA
    gd = xm[..., o:o + GATE_LORA]
    w = (w0 + jnp.tanh(wd) @ w_up).astype(f32)
    decay = jnp.exp(-jnp.exp(-jax.nn.softplus(-w) - 0.5))
    a = jax.nn.sigmoid((a0 + ad @ a_up).astype(f32))
    g = (jax.nn.sigmoid(gd) @ g_up).astype(f32)
    heads = lambda t: t.astype(f32).reshape(b, T, H_A, HEAD_DIM)
    r, k, v, a, decay = heads(r), heads(k), heads(v), heads(a), heads(decay)
    kk = k * k_k.astype(f32).reshape(H_A, HEAD_DIM)
    kk = kk / jnp.maximum(jnp.sqrt(jnp.sum(kk * kk, axis=-1, keepdims=True)), 1e-12)
    k = k * (1.0 + (a - 1.0) * k_a.astype(f32).reshape(H_A, HEAD_DIM))
    y, S = rwkv7_recurrence(r, decay, k, v, kk, a, S0.astype(f32))
    mean = jnp.mean(y, axis=-1, keepdims=True)
    var = jnp.mean(jnp.square(y - mean), axis=-1, keepdims=True)
    y = ((y - mean) * lax.rsqrt(var + GN_EPS)).reshape(b, T, W_A)
    y = y * ln_w.astype(f32) + ln_b.astype(f32)
    bonus = jnp.sum(r * k * r_k.astype(f32), axis=-1, keepdims=True) * v
    y = (y + bonus.reshape(b, T, W_A)) * g
    return y.astype(c.dtype), c[:, -1], S


def rotate(t, cos, sin):
    t1, t2 = t[..., :HEAD_DIM // 2], t[..., HEAD_DIM // 2:]
    return jnp.concatenate([t1 * cos - t2 * sin, t1 * sin + t2 * cos], axis=-1)


def retention_chunked(q, k, v, S0):
    T, H = q.shape[1], q.shape[2]
    L = chunk_len(T)
    log_g = jnp.log(1.0 - 2.0 ** (-5.0 - jnp.arange(H, dtype=jnp.float32)))
    idx = jnp.arange(L, dtype=jnp.float32)
    diff = idx[:, None] - idx[None, :]
    intra = jnp.where(diff >= 0, jnp.exp(log_g[:, None, None] * jnp.maximum(diff, 0.0)), 0.0)
    q_decay = jnp.exp(log_g[:, None] * (idx[None, :] + 1.0))
    k_decay = jnp.exp(log_g[:, None] * (L - 1.0 - idx[None, :]))
    c_decay = jnp.exp(log_g * L)

    def step(S, inp):
        qc, kc, vc = inp
        scores = jnp.einsum('blhd,bmhd->bhlm', qc, kc) * intra
        out = jnp.einsum('bhlm,bmhd->blhd', scores, vc)
        out = out + jnp.einsum('blhk,bhkv,hl->blhv', qc, S, q_decay)
        S = S * c_decay[None, :, None, None] + jnp.einsum('blhk,blhv,hl->bhkv', kc, vc, k_decay)
        return S, out

    S, outs = lax.scan(step, S0, (to_chunks(q, L), to_chunks(k, L), to_chunks(v, L)))
    return from_chunks(outs), S


def retention_mixer(c, S0, pos0, norm_w):
    f32 = jnp.float32
    b, T, _ = c.shape
    heads = lambda t: t.astype(f32).reshape(b, T, H_B, HEAD_DIM)
    q = heads(c[..., :W_B])
    k = heads(c[..., W_B:2 * W_B])
    v = heads(c[..., 2 * W_B:3 * W_B])
    g = c[..., 3 * W_B:].astype(f32)
    pos = pos0 + jnp.arange(T, dtype=f32)
    theta = 1.0 / (ROPE_BASE ** jnp.linspace(0.0, 1.0, HEAD_DIM // 2, dtype=f32))
    ang = pos[:, None] * theta[None, :]
    cos, sin = jnp.cos(ang)[:, None, :], jnp.sin(ang)[:, None, :]
    q = rotate(q, cos, sin)
    k = rotate(k, cos, sin) * (HEAD_DIM ** -0.5)
    out, S = retention_chunked(q, k, v, S0.astype(f32))
    out = out * lax.rsqrt(jnp.mean(out * out, axis=-1, keepdims=True) + RMS_EPS)
    out = out.reshape(b, T, W_B) * norm_w.astype(f32)
    out = jax.nn.silu(g) * out
    return out.astype(c.dtype), S


def ssd_chunked(x, dt, A, Bm, Cm, h0):
    b, T, H, P = x.shape
    G, N = Bm.shape[2], Bm.shape[3]
    R = H // G
    L = chunk_len(T)
    x = x.reshape(b, T, G, R, P)
    dt = dt.reshape(b, T, G, R)
    A = A.reshape(G, R)
    h0 = h0.reshape(b, G, R, P, N)
    mask = jnp.tril(jnp.ones((L, L), dtype=bool))[None, :, :, None, None]

    def step(h, inp):
        xc, dtc, Bc, Cc = inp
        cum = jnp.cumsum(dtc * A, axis=1)
        seg = cum[:, :, None] - cum[:, None, :]
        decay = jnp.exp(jnp.where(mask, seg, -jnp.inf))
        cb = jnp.einsum('blgn,bmgn->blmg', Cc, Bc)
        wts = cb[..., None] * decay * dtc[:, None]
        y = jnp.einsum('blmgr,bmgrp->blgrp', wts, xc)
        y = y + jnp.einsum('blgn,bgrpn,blgr->blgrp', Cc, h, jnp.exp(cum))
        last = cum[:, -1]
        wk = jnp.exp(last[:, None] - cum) * dtc
        h = h * jnp.exp(last)[..., None, None] + jnp.einsum('blgr,blgrp,blgn->bgrpn', wk, xc, Bc)
        return h, y

    h, ys = lax.scan(step, h0, (to_chunks(x, L), to_chunks(dt, L), to_chunks(Bm, L), to_chunks(Cm, L)))
    return from_chunks(ys).reshape(b, T, H, P), h.reshape(b, H, P, N)


def mamba2_mixer(c, conv_prev, h0, conv_w, conv_b, dt_bias, a_log, d_skip, norm_w):
    f32 = jnp.float32
    b, T, _ = c.shape
    z = c[..., :W_C].astype(f32)
    xbc = c[..., W_C:W_C + CONV_DIM]
    dt_raw = c[..., W_C + CONV_DIM:].astype(f32)
    xpad = jnp.concatenate([conv_prev.astype(c.dtype), xbc], axis=1)
    conv = lax.conv_general_dilated(xpad, conv_w[:, None, :].astype(c.dtype), window_strides=(1,),
                                    padding='VALID', dimension_numbers=('NWC', 'WIO', 'NWC'),
                                    feature_group_count=CONV_DIM)
    xbc = jax.nn.silu(conv.astype(f32) + conv_b.astype(f32))
    xs = xbc[..., :W_C].reshape(b, T, H_C, HEAD_DIM)
    Bm = xbc[..., W_C:W_C + N_GROUPS * N_STATE].reshape(b, T, N_GROUPS, N_STATE)
    Cm = xbc[..., W_C + N_GROUPS * N_STATE:].reshape(b, T, N_GROUPS, N_STATE)
    dt = jax.nn.softplus(dt_raw + dt_bias.astype(f32))
    A = -jnp.exp(a_log.astype(f32))
    y, h = ssd_chunked(xs, dt, A, Bm, Cm, h0.astype(f32))
    y = y + d_skip.astype(f32)[:, None] * xs
    y = y.reshape(b, T, W_C) * jax.nn.silu(z)
    y = y * lax.rsqrt(jnp.mean(y * y, axis=-1, keepdims=True) + GATED_NORM_EPS) * norm_w.astype(f32)
    return y.astype(c.dtype), xpad[:, -(CONV_W - 1):], h


def swiglu(h, wg, wu, wd):
    return (jax.nn.silu(h @ wg) * (h @ wu)) @ wd


def moe_swiglu(h, router, wg, wu, wd):
    probs = jax.nn.softmax((h @ router).astype(jnp.float32), axis=-1)
    top_p, top_i = lax.top_k(probs, TOP_K)
    top_p = top_p / jnp.sum(top_p, axis=-1, keepdims=True)
    gates = jnp.sum(jax.nn.one_hot(top_i, N_EXPERTS, dtype=jnp.float32) * top_p[..., None], axis=-2)
    out = jnp.zeros_like(h)
    for e in range(N_EXPERTS):
        out = out + gates[..., e:e + 1].astype(h.dtype) * swiglu(h, wg[e], wu[e], wd[e])
    return out


def trunk(x, s_rwkv, s_shift, s_ret, s_ssm, s_conv, pos0, W):
    n_rwkv, n_shift, n_ret, n_ssm, n_conv = [], [], [], [], []
    for i in range(DEPTH):
        h = rmsnorm(x, W['norm_mix'][i])
        cp = h @ W['w_in'][i]
        ca = cp[..., :COLS_A]
        cb = cp[..., COLS_A:COLS_A + COLS_B]
        cc = cp[..., COLS_A + COLS_B:]
        ya, sh, Sa = rwkv7_mixer(ca, s_shift[i], s_rwkv[i], W['rwkv_mu'][i], W['rwkv_w0'][i],
                                 W['rwkv_w_up'][i], W['rwkv_a0'][i], W['rwkv_a_up'][i],
                                 W['rwkv_g_up'][i], W['rwkv_k_k'][i], W['rwkv_k_a'][i],
                                 W['rwkv_r_k'][i], W['rwkv_ln_w'][i], W['rwkv_ln_b'][i])
        yb, Sb = retention_mixer(cb, s_ret[i], pos0, W['ret_norm'][i])
        yc, cv, hc = mamba2_mixer(cc, s_conv[i], s_ssm[i], W['ssm_conv_w'][i], W['ssm_conv_b'][i],
                                  W['ssm_dt_bias'][i], W['ssm_a_log'][i], W['ssm_d'][i],
                                  W['ssm_norm'][i])
        x = x + jnp.concatenate([ya, yb, yc], axis=-1) @ W['w_out'][i]
        h = rmsnorm(x, W['norm_ffn'][i])
        j = i // 2
        if i % 2 == 0:
            x = x + swiglu(h, W['ffn_w_gate'][j], W['ffn_w_up'][j], W['ffn_w_down'][j])
        else:
            x = x + moe_swiglu(h, W['moe_router'][j], W['moe_w_gate'][j], W['moe_w_up'][j],
                               W['moe_w_down'][j])
        n_rwkv.append(Sa)
        n_shift.append(sh)
        n_ret.append(Sb)
        n_ssm.append(hc)
        n_conv.append(cv)
    y = rmsnorm(x, W['norm_final'])
    return (y, jnp.stack(n_rwkv), jnp.stack(n_shift), jnp.stack(n_ret), jnp.stack(n_ssm),
            jnp.stack(n_conv))


def setup_inputs(seed: int = 0) -> dict:
    key = jax.random.key(seed)
    ks = iter(jax.random.split(key, 48))
    f32 = jnp.float32

    def nrm(shape, scale):
        return scale * jax.random.normal(next(ks), shape, f32)

    def gain(shape):
        return 1.0 + nrm(shape, 0.02)

    x_prompt = nrm((BATCH, SEQ, D_MODEL), 1.0)
    x_sample = nrm((DEC_BATCH, DEC_SEQ, D_MODEL), 1.0)
    state_rwkv = nrm((DEPTH, DEC_BATCH, H_A, HEAD_DIM, HEAD_DIM), 0.5)
    state_shift = nrm((DEPTH, DEC_BATCH, COLS_A), 1.0)
    state_ret = nrm((DEPTH, DEC_BATCH, H_B, HEAD_DIM, HEAD_DIM), 0.5)
    state_ssm = nrm((DEPTH, DEC_BATCH, H_C, HEAD_DIM, N_STATE), 0.5)
    state_conv = nrm((DEPTH, DEC_BATCH, CONV_W - 1, CONV_DIM), 1.0)
    norm_mix = gain((DEPTH, D_MODEL))
    w_in = nrm((DEPTH, D_MODEL, IN_COLS), D_MODEL ** -0.5)
    rwkv_mu = jax.random.uniform(next(ks), (DEPTH, COLS_A), f32)
    rwkv_w0 = -2.0 + nrm((DEPTH, W_A), 1.0)
    rwkv_w_up = nrm((DEPTH, DECAY_LORA, W_A), 0.1 * DECAY_LORA ** -0.5)
    rwkv_a0 = nrm((DEPTH, W_A), 0.1)
    rwkv_a_up = nrm((DEPTH, AAA_LORA, W_A), 0.1 * AAA_LORA ** -0.5)
    rwkv_g_up = nrm((DEPTH, GATE_LORA, W_A), GATE_LORA ** -0.5)
    rwkv_k_k = 0.85 + nrm((DEPTH, W_A), 0.05)
    rwkv_k_a = 1.0 + nrm((DEPTH, W_A), 0.05)
    rwkv_r_k = nrm((DEPTH, H_A, HEAD_DIM), 0.1)
    rwkv_ln_w = gain((DEPTH, W_A))
    rwkv_ln_b = nrm((DEPTH, W_A), 0.02)
    ret_norm = gain((DEPTH, W_B))
    ssm_conv_w = nrm((DEPTH, CONV_W, CONV_DIM), CONV_W ** -0.5)
    ssm_conv_b = nrm((DEPTH, CONV_DIM), 0.02)
    dt0 = jnp.exp(jax.random.uniform(next(ks), (DEPTH, H_C), f32,
                                     minval=math.log(1e-3), maxval=math.log(1e-1)))
    ssm_dt_bias = dt0 + jnp.log(-jnp.expm1(-dt0))
    ssm_a_log = jnp.log(jax.random.uniform(next(ks), (DEPTH, H_C), f32, minval=1.0, maxval=16.0))
    ssm_d = 1.0 + nrm((DEPTH, H_C), 0.1)
    ssm_norm = gain((DEPTH, W_C))
    w_out = nrm((DEPTH, D_MIX, D_MODEL), D_MIX ** -0.5)
    norm_ffn = gain((DEPTH, D_MODEL))
    ffn_w_gate = nrm((N_DENSE, D_MODEL, D_FF), D_MODEL ** -0.5)
    ffn_w_up = nrm((N_DENSE, D_MODEL, D_FF), D_MODEL ** -0.5)
    ffn_w_down = nrm((N_DENSE, D_FF, D_MODEL), D_FF ** -0.5)
    moe_router = nrm((N_MOE, D_MODEL, N_EXPERTS), D_MODEL ** -0.5)
    moe_w_gate = nrm((N_MOE, N_EXPERTS, D_MODEL, D_FF_EXPERT), D_MODEL ** -0.5)
    moe_w_up = nrm((N_MOE, N_EXPERTS, D_MODEL, D_FF_EXPERT), D_MODEL ** -0.5)
    moe_w_down = nrm((N_MOE, N_EXPERTS, D_FF_EXPERT, D_MODEL), D_FF_EXPERT ** -0.5)
    norm_final = gain((D_MODEL,))
    return {'x_prompt': x_prompt, 'x_sample': x_sample, 'state_rwkv': state_rwkv,
            'state_shift': state_shift, 'state_ret': state_ret, 'state_ssm': state_ssm,
            'state_conv': state_conv, 'norm_mix': norm_mix, 'w_in': w_in, 'rwkv_mu': rwkv_mu,
            'rwkv_w0': rwkv_w0, 'rwkv_w_up': rwkv_w_up, 'rwkv_a0': rwkv_a0,
            'rwkv_a_up': rwkv_a_up, 'rwkv_g_up': rwkv_g_up, 'rwkv_k_k': rwkv_k_k,
            'rwkv_k_a': rwkv_k_a, 'rwkv_r_k': rwkv_r_k, 'rwkv_ln_w': rwkv_ln_w,
            'rwkv_ln_b': rwkv_ln_b, 'ret_norm': ret_norm, 'ssm_conv_w': ssm_conv_w,
            'ssm_conv_b': ssm_conv_b, 'ssm_dt_bias': ssm_dt_bias, 'ssm_a_log': ssm_a_log,
            'ssm_d': ssm_d, 'ssm_norm': ssm_norm, 'w_out': w_out, 'norm_ffn': norm_ffn,
            'ffn_w_gate': ffn_w_gate, 'ffn_w_up': ffn_w_up, 'ffn_w_down': ffn_w_down,
            'moe_router': moe_router, 'moe_w_gate': moe_w_gate, 'moe_w_up': moe_w_up,
            'moe_w_down': moe_w_down, 'norm_final': norm_final}


def reference(x_prompt, x_sample, state_rwkv, state_shift, state_ret, state_ssm, state_conv,
              norm_mix, w_in, rwkv_mu, rwkv_w0, rwkv_w_up, rwkv_a0, rwkv_a_up, rwkv_g_up,
              rwkv_k_k, rwkv_k_a, rwkv_r_k, rwkv_ln_w, rwkv_ln_b, ret_norm, ssm_conv_w,
              ssm_conv_b, ssm_dt_bias, ssm_a_log, ssm_d, ssm_norm, w_out, norm_ffn,
              ffn_w_gate, ffn_w_up, ffn_w_down, moe_router, moe_w_gate, moe_w_up, moe_w_down,
              norm_final):
    W = dict(norm_mix=norm_mix, w_in=w_in, rwkv_mu=rwkv_mu, rwkv_w0=rwkv_w0,
             rwkv_w_up=rwkv_w_up, rwkv_a0=rwkv_a0, rwkv_a_up=rwkv_a_up, rwkv_g_up=rwkv_g_up,
             rwkv_k_k=rwkv_k_k, rwkv_k_a=rwkv_k_a, rwkv_r_k=rwkv_r_k, rwkv_ln_w=rwkv_ln_w,
             rwkv_ln_b=rwkv_ln_b, ret_norm=ret_norm, ssm_conv_w=ssm_conv_w,
             ssm_conv_b=ssm_conv_b, ssm_dt_bias=ssm_dt_bias, ssm_a_log=ssm_a_log, ssm_d=ssm_d,
             ssm_norm=ssm_norm, w_out=w_out, norm_ffn=norm_ffn, ffn_w_gate=ffn_w_gate,
             ffn_w_up=ffn_w_up, ffn_w_down=ffn_w_down, moe_router=moe_router,
             moe_w_gate=moe_w_gate, moe_w_up=moe_w_up, moe_w_down=moe_w_down,
             norm_final=norm_final)
    f32 = jnp.float32
    bp = x_prompt.shape[0]
    y_prompt, p_rwkv, p_shift, p_ret, p_ssm, p_conv = trunk(
        x_prompt,
        jnp.zeros((DEPTH, bp, H_A, HEAD_DIM, HEAD_DIM), f32),
        jnp.zeros((DEPTH, bp, COLS_A), x_prompt.dtype),
        jnp.zeros((DEPTH, bp, H_B, HEAD_DIM, HEAD_DIM), f32),
        jnp.zeros((DEPTH, bp, H_C, HEAD_DIM, N_STATE), f32),
        jnp.zeros((DEPTH, bp, CONV_W - 1, CONV_DIM), x_prompt.dtype),
        0, W)
    y_sample, s_rwkv, s_shift, s_ret, s_ssm, s_conv = trunk(
        x_sample, state_rwkv, state_shift, state_ret, state_ssm, state_conv, PAST_LEN, W)
    return (y_prompt, y_sample,
            p_rwkv.astype(state_rwkv.dtype), p_shift.astype(state_shift.dtype),
            p_ret.astype(state_ret.dtype), p_ssm.astype(state_ssm.dtype),
            p_conv.astype(state_conv.dtype),
            s_rwkv.astype(state_rwkv.dtype), s_shift.astype(state_shift.dtype),
            s_ret.astype(state_ret.dtype), s_ssm.astype(state_ssm.dtype),
            s_conv.astype(state_conv.dtype))
```

```python
import functools
import math

import numpy as np
import jax
import jax.numpy as jnp
from jax import lax
from jax.experimental import pallas as pl
from jax.experimental.pallas import tpu as pltpu

F32 = jnp.float32
BF16 = jnp.bfloat16
HIGHEST = lax.Precision.HIGHEST

LANES = 128
VMEM_LIMIT = 56 * 1024 * 1024

D_MODEL = 1024
HEAD_DIM = 64
H_A, H_B, H_C = 6, 4, 6
W_A, W_B, W_C = H_A * HEAD_DIM, H_B * HEAD_DIM, H_C * HEAD_DIM
DECAY_LORA, AAA_LORA, GATE_LORA = 64, 64, 128
COLS_A = 3 * W_A + DECAY_LORA + AAA_LORA + GATE_LORA
COLS_B = 4 * W_B
N_STATE, N_GROUPS, CONV_W = 128, 2, 4
CONV_DIM = W_C + 2 * N_GROUPS * N_STATE
COLS_C = W_C + CONV_DIM + H_C
COLS_C_PAD = 1408
ROPE_BASE = 10000.0
RMS_EPS = 1e-6
GN_EPS = 64e-5
GATED_NORM_EPS = 1e-5
N_EXPERTS = 8
PAST_LEN = 16384

TM = 512


def _dot(a, b):
    return jnp.dot(a.astype(BF16), b.astype(BF16), preferred_element_type=F32)


def _dot_hi(a, b):
    return jnp.dot(a, b, precision=HIGHEST, preferred_element_type=F32)


def _sigmoid(x):
    return 1.0 / (1.0 + jnp.exp(-x))


def _softplus(x):
    return jnp.maximum(x, 0.0) + jnp.log1p(jnp.exp(-jnp.abs(x)))


def _rmsnorm(x, g, eps):
    return x * lax.rsqrt(jnp.mean(x * x, axis=-1, keepdims=True) + eps) * g


def _params(sem):
    return pltpu.CompilerParams(dimension_semantics=sem, vmem_limit_bytes=VMEM_LIMIT)


def _row_spec(width, col=0):
    return pl.BlockSpec((TM, width), lambda i, c=col: (i, c))


def _full_spec(shape):
    nd = len(shape)
    return pl.BlockSpec(shape, lambda i, n=nd: (0,) * n)


def _in_proj_kernel(x_ref, g_ref, w_ref, oa_ref, ob_ref, oc_ref):
    h = _rmsnorm(x_ref[...], g_ref[...], RMS_EPS).astype(BF16)
    oa_ref[...] = jnp.dot(h, w_ref[:, 0:COLS_A], preferred_element_type=F32)
    ob_ref[...] = jnp.dot(h, w_ref[:, COLS_A:COLS_A + COLS_B], preferred_element_type=F32)
    oc_ref[...] = jnp.dot(h, w_ref[:, COLS_A + COLS_B:], preferred_element_type=F32)


def _in_proj(x, g, w):
    n = x.shape[0]
    wtot = w.shape[1]
    return pl.pallas_call(
        _in_proj_kernel,
        grid=(n // TM,),
        in_specs=[_row_spec(D_MODEL), _full_spec((1, D_MODEL)), _full_spec((D_MODEL, wtot))],
        out_specs=[_row_spec(COLS_A), _row_spec(COLS_B), _row_spec(COLS_C_PAD)],
        out_shape=[jax.ShapeDtypeStruct((n, COLS_A), F32), jax.ShapeDtypeStruct((n, COLS_B), F32),
                   jax.ShapeDtypeStruct((n, COLS_C_PAD), F32)],
        compiler_params=_params(("parallel",)),
        name="in_proj",
    )(x, g, w)


def _rwkv_prep_kernel(c_ref, p_ref, mu_ref, w0_ref, a0_ref, kk_ref, ka_ref, rk_ref, wup_ref, aup_ref,
                      gup_ref, bd_ref, r_o, d_o, k_o, v_o, n_o, b_o, g_o, bonus_o):
    c = c_ref[...]
    xm = c + (p_ref[...] - c) * mu_ref[...]
    r = xm[:, 0:W_A]
    k = xm[:, W_A:2 * W_A]
    v = xm[:, 2 * W_A:3 * W_A]
    lora = xm[:, 3 * W_A:3 * W_A + DECAY_LORA + AAA_LORA]
    gd = xm[:, 3 * W_A + DECAY_LORA + AAA_LORA:]
    w = w0_ref[...] + _dot(jnp.tanh(lora), wup_ref[...])
    decay = jnp.exp(-math.exp(-0.5) * _sigmoid(w))
    a = _sigmoid(a0_ref[...] + _dot(lora, aup_ref[...]))
    g = _dot(_sigmoid(gd), gup_ref[...])
    bd = bd_ref[...]
    kk = k * kk_ref[...]
    kk = kk / jnp.maximum(jnp.sqrt(_dot_hi(kk * kk, bd)), 1e-12)
    k2 = k * (1.0 + (a - 1.0) * ka_ref[...])
    r_o[...] = r
    d_o[...] = decay
    k_o[...] = k2
    v_o[...] = v
    n_o[...] = -kk
    b_o[...] = kk * a
    g_o[...] = g
    bonus_o[...] = _dot_hi(r * k2 * rk_ref[...], bd) * v


def _rwkv_prep(ca, prev, mu, w0, a0, k_k, k_a, r_k, wup, aup, gup, bd):
    n = ca.shape[0]
    vec = _full_spec((1, W_A))
    lora_spec = _full_spec((DECAY_LORA + AAA_LORA, W_A))
    return pl.pallas_call(
        _rwkv_prep_kernel,
        grid=(n // TM,),
        in_specs=[_row_spec(COLS_A), _row_spec(COLS_A), _full_spec((1, COLS_A)), vec, vec, vec, vec, vec,
                  lora_spec, lora_spec, _full_spec((GATE_LORA, W_A)), _full_spec((W_A, W_A))],
        out_specs=[_row_spec(W_A)] * 8,
        out_shape=[jax.ShapeDtypeStruct((n, W_A), F32)] * 8,
        compiler_params=_params(("parallel",)),
        name="rwkv_prep",
    )(ca, prev, mu, w0, a0, k_k, k_a, r_k, wup, aup, gup, bd)


def _ret_prep_kernel(q_ref, k_ref, cos_ref, sin_ref, q_o, k_o):
    cos = cos_ref[...]
    sin = sin_ref[...]
    lane = lax.broadcasted_iota(jnp.int32, (TM, W_B), 1)
    first_half = (lane % HEAD_DIM) < (HEAD_DIM // 2)

    def rope(x):
        partner = jnp.where(first_half, pltpu.roll(x, W_B - HEAD_DIM // 2, 1), pltpu.roll(x, HEAD_DIM // 2, 1))
        return x * cos + partner * sin

    q_o[...] = rope(q_ref[...])
    k_o[...] = rope(k_ref[...]) * (HEAD_DIM ** -0.5)


def _ret_prep(cb, cos, sin, n_prompt_tiles, n_table_tiles):
    n = cb.shape[0]
    tab = lambda i: (jnp.where(i < n_prompt_tiles, i % n_table_tiles, n_table_tiles), 0)
    return pl.pallas_call(
        _ret_prep_kernel,
        grid=(n // TM,),
        in_specs=[_row_spec(W_B, 0), _row_spec(W_B, 1), pl.BlockSpec((TM, W_B), tab), pl.BlockSpec((TM, W_B), tab)],
        out_specs=[_row_spec(W_B)] * 2,
        out_shape=[jax.ShapeDtypeStruct((n, W_B), F32)] * 2,
        compiler_params=_params(("parallel",)),
        name="ret_prep",
    )(cb, cb, cos, sin)


def _ssm_prep_kernel(cc_ref, x1_ref, x2_ref, x3_ref, cw_ref, cb_ref, dtb_ref, alog_ref, ex_ref,
                     xdt_o, b_o, c_o, dec_o, xs_o):
    cc = cc_ref[...]
    x0 = cc[:, W_C:W_C + CONV_DIM]
    cw = cw_ref[...]
    conv = (x0 * cw[3:4, :] + x1_ref[...] * cw[2:3, :] + x2_ref[...] * cw[1:2, :] + x3_ref[...] * cw[0:1, :]
            + cb_ref[...])
    act = conv * _sigmoid(conv)
    xs = act[:, 0:W_C]
    dt = _softplus(cc[:, W_C + CONV_DIM:] + dtb_ref[...])
    dec_o[...] = jnp.exp(dt * (-jnp.exp(alog_ref[...])))
    xdt_o[...] = xs * _dot_hi(dt, ex_ref[...])
    b_o[...] = act[:, W_C:W_C + N_GROUPS * N_STATE]
    c_o[...] = act[:, W_C + N_GROUPS * N_STATE:]
    xs_o[...] = xs


def _ssm_prep(cc, x1, x2, x3, cw, cb, dtb, alog, ex):
    n = cc.shape[0]
    gn = N_GROUPS * N_STATE
    return pl.pallas_call(
        _ssm_prep_kernel,
        grid=(n // TM,),
        in_specs=[_row_spec(COLS_C_PAD), _row_spec(CONV_DIM), _row_spec(CONV_DIM), _row_spec(CONV_DIM),
                  _full_spec((CONV_W, CONV_DIM)), _full_spec((1, CONV_DIM)), _full_spec((1, LANES)),
                  _full_spec((1, LANES)), _full_spec((LANES, W_C))],
        out_specs=[_row_spec(W_C), _row_spec(gn), _row_spec(gn), _row_spec(LANES), _row_spec(W_C)],
        out_shape=[jax.ShapeDtypeStruct((n, W_C), F32), jax.ShapeDtypeStruct((n, gn), F32),
                   jax.ShapeDtypeStruct((n, gn), F32), jax.ShapeDtypeStruct((n, LANES), F32),
                   jax.ShapeDtypeStruct((n, W_C), F32)],
        compiler_params=_params(("parallel",)),
        name="ssm_prep",
    )(cc, x1, x2, x3, cw, cb, dtb, alog, ex)


def _scan_kernel(*refs, mode, ni, tt_len, n_tt):
    if mode == "rwkv":
        q_ref, d_ref, a_ref, b_ref, n_ref, a2_ref, s0_ref, y_ref, so_ref, st = refs
    else:
        q_ref, d_ref, a_ref, b_ref, s0_ref, y_ref, so_ref, st = refs
    tt = pl.program_id(1)

    @pl.when(tt == 0)
    def _():
        st[...] = s0_ref[...]

    def row(ref, t, i):
        return ref[t, pl.ds(i, 1), :]

    def step(t, carry):
        bv = b_ref[t]
        if mode == "rwkv":
            sa_parts = [jnp.zeros_like(bv), jnp.zeros_like(bv)]
            for i in range(ni):
                sa_parts[i % 2] = sa_parts[i % 2] + st[i] * row(n_ref, t, i)
            sa = sa_parts[0] + sa_parts[1]
        elif mode == "ssd":
            d = d_ref[t]
        else:
            d = d_ref[0]
        y_parts = [jnp.zeros_like(bv), jnp.zeros_like(bv)]
        for i in range(ni):
            if mode == "rwkv":
                s = st[i] * row(d_ref, t, i) + row(a_ref, t, i) * bv + row(a2_ref, t, i) * sa
            else:
                s = st[i] * d + row(a_ref, t, i) * bv
            st[i] = s
            y_parts[i % 2] = y_parts[i % 2] + s * row(q_ref, t, i)
        y_ref[t] = y_parts[0] + y_parts[1]
        return carry

    lax.fori_loop(0, tt_len, step, 0)

    @pl.when(tt == n_tt - 1)
    def _():
        so_ref[...] = st[...]


def _scan(mode, q, d, a, b, s0, n=None, a2=None, tt_len=32):
    t_len, ni, lanes = q.shape
    nj = b.shape[1]
    tt_len = min(tt_len, t_len)
    n_tt = t_len // tt_len
    groups = lanes // LANES
    i_spec = pl.BlockSpec((tt_len, ni, LANES), lambda g, t: (t, 0, g))
    j_spec = pl.BlockSpec((tt_len, nj, LANES), lambda g, t: (t, 0, g))
    s_spec = pl.BlockSpec((ni, nj, LANES), lambda g, t: (0, 0, g))
    if mode == "rwkv":
        d_spec = i_spec
    elif mode == "ssd":
        d_spec = pl.BlockSpec((tt_len, 1, LANES), lambda g, t: (t, 0, g))
    else:
        d_spec = pl.BlockSpec((1, 1, LANES), lambda g, t: (0, 0, g))
    args = [q, d, a, b]
    specs = [i_spec, d_spec, i_spec, j_spec]
    if mode == "rwkv":
        args += [n, a2]
        specs += [i_spec, i_spec]
    args.append(s0)
    specs.append(s_spec)
    return pl.pallas_call(
        functools.partial(_scan_kernel, mode=mode, ni=ni, tt_len=tt_len, n_tt=n_tt),
        grid=(groups, n_tt),
        in_specs=specs,
        out_specs=[j_spec, s_spec],
        out_shape=[jax.ShapeDtypeStruct((t_len, nj, lanes), F32), jax.ShapeDtypeStruct((ni, nj, lanes), F32)],
        scratch_shapes=[pltpu.VMEM((ni, nj, LANES), F32)],
        compiler_params=_params(("parallel", "arbitrary")),
        name="scan_" + mode,
    )(*args)


def _post_kernel(x_ref, ya_ref, bonus_ref, ga_ref, ob_ref, gb_ref, yc_ref, xs_ref, z_ref, lnw_ref, lnb_ref,
                 rn_ref, dsk_ref, sn_ref, bda_ref, bdb_ref, wo_ref, out_ref):
    inv_hd = 1.0 / HEAD_DIM
    y = ya_ref[...]
    bda = bda_ref[...]
    mean = _dot_hi(y, bda) * inv_hd
    yd = y - mean
    var = _dot_hi(yd * yd, bda) * inv_hd
    ya = (yd * lax.rsqrt(var + GN_EPS) * lnw_ref[...] + lnb_ref[...] + bonus_ref[...]) * ga_ref[...]
    o = ob_ref[...]
    ms = _dot_hi(o * o, bdb_ref[...]) * inv_hd
    gb = gb_ref[...]
    yb = (gb * _sigmoid(gb)) * (o * lax.rsqrt(ms + RMS_EPS) * rn_ref[...])
    z = z_ref[...]
    yc = (yc_ref[...] + dsk_ref[...] * xs_ref[...]) * (z * _sigmoid(z))
    yc = _rmsnorm(yc, sn_ref[...], GATED_NORM_EPS)
    out_ref[...] = (x_ref[...] + _dot(ya, wo_ref[0:W_A, :]) + _dot(yb, wo_ref[W_A:W_A + W_B, :])
                    + _dot(yc, wo_ref[W_A + W_B:, :]))


def _post(x, ya, bonus, ga, ob, cb, yc, xs, cc, lnw, lnb, rn, dsk, sn, bda, bdb, wo):
    n = x.shape[0]
    va = _full_spec((1, W_A))
    return pl.pallas_call(
        _post_kernel,
        grid=(n // TM,),
        in_specs=[_row_spec(D_MODEL), _row_spec(W_A), _row_spec(W_A), _row_spec(W_A), _row_spec(W_B),
                  _row_spec(W_B, 3), _row_spec(W_C), _row_spec(W_C), _row_spec(W_C, 0), va, va,
                  _full_spec((1, W_B)), va, va, _full_spec((W_A, W_A)), _full_spec((W_B, W_B)),
                  _full_spec((D_MODEL, D_MODEL))],
        out_specs=_row_spec(D_MODEL),
        out_shape=jax.ShapeDtypeStruct((n, D_MODEL), F32),
        compiler_params=_params(("parallel",)),
        name="post_outproj",
    )(x, ya, bonus, ga, ob, cb, yc, xs, cc, lnw, lnb, rn, dsk, sn, bda, bdb, wo)


def _ffn_kernel(*refs, gated, final_norm, n_e):
    refs = list(refs)
    x_ref, nf_ref = refs[:2]
    pos = 2
    if gated:
        router_ref = refs[pos]
        pos += 1
    wg_ref, wu_ref, wd_ref = refs[pos:pos + 3]
    pos += 3
    if final_norm:
        nfin_ref = refs[pos]
        pos += 1
    out_ref = refs[pos]
    hb_s, acc_s = refs[pos + 1:pos + 3]
    if gated:
        gate_s = refs[pos + 3]
    e = pl.program_id(1)

    @pl.when(e == 0)
    def _():
        h = _rmsnorm(x_ref[...], nf_ref[...], RMS_EPS)
        hb_s[...] = h.astype(BF16)
        acc_s[...] = jnp.zeros_like(acc_s)
        if gated:
            lane = lax.broadcasted_iota(jnp.int32, (TM, LANES), 1)
            logits = jnp.where(lane < N_EXPERTS, _dot_hi(h, router_ref[...]), -jnp.inf)
            p = jnp.exp(logits - jnp.max(logits, axis=-1, keepdims=True))
            p = p / jnp.sum(p, axis=-1, keepdims=True)
            p1 = jnp.max(p, axis=-1, keepdims=True)
            i1 = jnp.min(jnp.where(p == p1, lane, LANES), axis=-1, keepdims=True)
            rest = jnp.where(lane == i1, -1.0, p)
            p2 = jnp.max(rest, axis=-1, keepdims=True)
            i2 = jnp.min(jnp.where(rest == p2, lane, LANES), axis=-1, keepdims=True)
            gate_s[...] = jnp.where(lane == i1, p1, jnp.where(lane == i2, p2, 0.0)) / (p1 + p2)

    hb = hb_s[...]
    g = jnp.dot(hb, wg_ref[0], preferred_element_type=F32)
    u = jnp.dot(hb, wu_ref[0], preferred_element_type=F32)
    o = _dot(g * _sigmoid(g) * u, wd_ref[0])
    if gated:
        lane = lax.broadcasted_iota(jnp.int32, (TM, LANES), 1)
        o = o * jnp.sum(jnp.where(lane == e, gate_s[...], 0.0), axis=-1, keepdims=True)
    acc_s[...] += o

    @pl.when(e == n_e - 1)
    def _():
        y = x_ref[...] + acc_s[...]
        if final_norm:
            y = _rmsnorm(y, nfin_ref[...], RMS_EPS)
        out_ref[...] = y


def _ffn(x, nf, wg, wu, wd, router=None, nfin=None):
    n = x.shape[0]
    n_e, _, dff = wg.shape
    gated = router is not None
    final_norm = nfin is not None
    vec = pl.BlockSpec((1, D_MODEL), lambda i, e: (0, 0))
    args = [x, nf]
    specs = [pl.BlockSpec((TM, D_MODEL), lambda i, e: (i, 0)), vec]
    if gated:
        args.append(router)
        specs.append(pl.BlockSpec((D_MODEL, LANES), lambda i, e: (0, 0)))
    args += [wg, wu, wd]
    specs += [pl.BlockSpec((1, D_MODEL, dff), lambda i, e: (e, 0, 0)),
              pl.BlockSpec((1, D_MODEL, dff), lambda i, e: (e, 0, 0)),
              pl.BlockSpec((1, dff, D_MODEL), lambda i, e: (e, 0, 0))]
    if final_norm:
        args.append(nfin)
        specs.append(vec)
    scratch = [pltpu.VMEM((TM, D_MODEL), BF16), pltpu.VMEM((TM, D_MODEL), F32)]
    if gated:
        scratch.append(pltpu.VMEM((TM, LANES), F32))
    return pl.pallas_call(
        functools.partial(_ffn_kernel, gated=gated, final_norm=final_norm, n_e=n_e),
        grid=(n // TM, n_e),
        in_specs=specs,
        out_specs=pl.BlockSpec((TM, D_MODEL), lambda i, e: (i, 0)),
        out_shape=jax.ShapeDtypeStruct((n, D_MODEL), F32),
        scratch_shapes=scratch,
        compiler_params=_params(("parallel", "arbitrary")),
        name="moe" if gated else "ffn",
    )(*args)


def _pad_lanes(y):
    pad = (-y.shape[-1]) % LANES
    return jnp.pad(y, [(0, 0)] * (y.ndim - 1) + [(0, pad)]) if pad else y


def _lanes_i(x, nsplit):
    b, t, h, f = x.shape
    y = jnp.transpose(x, (1, 3, 0, 2)).reshape(t, f, b * h)
    return _pad_lanes(jnp.tile(y, (1, 1, nsplit)))


def _lanes_j(x, nsplit):
    b, t, h, f = x.shape
    y = x.reshape(b, t, h, nsplit, f // nsplit)
    return _pad_lanes(jnp.transpose(y, (1, 4, 3, 0, 2)).reshape(t, f // nsplit, nsplit * b * h))


def _from_lanes_j(y, b, h, nsplit):
    t, j, _ = y.shape
    y = y[:, :, :nsplit * b * h].reshape(t, j, nsplit, b, h)
    return jnp.transpose(y, (3, 0, 4, 2, 1)).reshape(b, t, h * nsplit * j)


def _state_from_lanes(s, b, h, nsplit, j_first):
    ni, nj, _ = s.shape
    s = s[:, :, :nsplit * b * h].reshape(ni, nj, nsplit, b, h)
    if j_first:
        return jnp.transpose(s, (3, 4, 2, 1, 0)).reshape(b, h, nsplit * nj, ni)
    return jnp.transpose(s, (3, 4, 0, 2, 1)).reshape(b, h, ni, nsplit * nj)


def _state_to_lanes(s, nsplit, j_first):
    b, h = s.shape[:2]
    y = jnp.transpose(s, (3, 2, 0, 1) if j_first else (2, 3, 0, 1))
    ni, njf = y.shape[:2]
    y = y.reshape(ni, nsplit, njf // nsplit, b * h)
    return _pad_lanes(jnp.transpose(y, (0, 2, 1, 3)).reshape(ni, njf // nsplit, nsplit * b * h))


def _block_diag_ones(width):
    idx = np.arange(width) // HEAD_DIM
    return jnp.asarray((idx[:, None] == idx[None, :]).astype(np.float32))


def _rope_kernel(ang_ref, cos_o, sin_o):
    ang = ang_ref[...]
    lane = lax.broadcasted_iota(jnp.int32, ang.shape, 1)
    cos_o[...] = jnp.cos(ang)
    sin_o[...] = jnp.where((lane % HEAD_DIM) < (HEAD_DIM // 2), -jnp.sin(ang), jnp.sin(ang))


def _rope_tables(t_prompt, t_sample):
    theta = 1.0 / (ROPE_BASE ** jnp.linspace(0.0, 1.0, HEAD_DIM // 2, dtype=F32))
    pos = jnp.concatenate([jnp.arange(t_prompt, dtype=F32), PAST_LEN + (jnp.arange(TM) % t_sample).astype(F32)])
    ang = jnp.tile(pos[:, None] * theta[None, :], (1, W_B // (HEAD_DIM // 2)))
    n = ang.shape[0]
    return pl.pallas_call(
        _rope_kernel,
        grid=(n // TM,),
        in_specs=[_row_spec(W_B)],
        out_specs=[_row_spec(W_B)] * 2,
        out_shape=[jax.ShapeDtypeStruct((n, W_B), F32)] * 2,
        compiler_params=_params(("parallel",)),
        name="rope_tables",
    )(ang)


def _shifted(x, prev_rows, shift):
    t = x.shape[1]
    p = prev_rows.shape[1]
    full = jnp.concatenate([prev_rows, x], axis=1)
    return full[:, p - shift:p - shift + t]


def _run_scans(group, feats, states, nsplits):
    b, t = group
    (r, dec, k2, v, nkk, beta, qh, kh, vb, xdt, bm, cm, dssm) = feats
    s_rwkv, s_ret, s_ssm = states
    ns_a, ns_b, ns_c = nsplits

    def heads(x, h):
        return x.reshape(b, t, h, x.shape[-1] // h)

    y, s = _scan("rwkv", _lanes_i(heads(r, H_A), ns_a), _lanes_i(heads(dec, H_A), ns_a),
                 _lanes_i(heads(k2, H_A), ns_a), _lanes_j(heads(v, H_A), ns_a),
                 _state_to_lanes(s_rwkv, ns_a, True),
                 n=_lanes_i(heads(nkk, H_A), ns_a), a2=_lanes_i(heads(beta, H_A), ns_a))
    ya = _from_lanes_j(y, b, H_A, ns_a)
    sa_new = _state_from_lanes(s, b, H_A, ns_a, True)

    gam = 1.0 - 2.0 ** (-5.0 - np.arange(H_B, dtype=np.float64))
    n_lanes = ns_b * b * H_B
    gam_l = np.ones((1, 1, -(-n_lanes // LANES) * LANES), np.float32)
    gam_l[0, 0, :n_lanes] = np.tile(gam, ns_b * b)
    y, s = _scan("ret", _lanes_i(heads(qh, H_B), ns_b), jnp.asarray(gam_l), _lanes_i(heads(kh, H_B), ns_b),
                 _lanes_j(heads(vb, H_B), ns_b), _state_to_lanes(s_ret, ns_b, False))
    yb = _from_lanes_j(y, b, H_B, ns_b)
    sb_new = _state_from_lanes(s, b, H_B, ns_b, False)

    rep = H_C // N_GROUPS
    bh = jnp.repeat(heads(bm, N_GROUPS), rep, axis=2)
    ch = jnp.repeat(heads(cm, N_GROUPS), rep, axis=2)
    y, s = _scan("ssd", _lanes_i(ch, ns_c), _lanes_i(dssm[:, :, :H_C, None], ns_c), _lanes_i(bh, ns_c),
                 _lanes_j(heads(xdt, H_C), ns_c), _state_to_lanes(s_ssm, ns_c, True))
    yc = _from_lanes_j(y, b, H_C, ns_c)
    sc_new = _state_from_lanes(s, b, H_C, ns_c, True)
    return (ya, yb, yc), (sa_new, sb_new, sc_new)


def kernel(x_prompt, x_sample, state_rwkv, state_shift, state_ret, state_ssm, state_conv, norm_mix, w_in, rwkv_mu, rwkv_w0, rwkv_w_up, rwkv_a0, rwkv_a_up, rwkv_g_up, rwkv_k_k, rwkv_k_a, rwkv_r_k, rwkv_ln_w, rwkv_ln_b, ret_norm, ssm_conv_w, ssm_conv_b, ssm_dt_bias, ssm_a_log, ssm_d, ssm_norm, w_out, norm_ffn, ffn_w_gate, ffn_w_up, ffn_w_down, moe_router, moe_w_gate, moe_w_up, moe_w_down, norm_final):
    bp, tp, _ = x_prompt.shape
    bs, ts, _ = x_sample.shape
    depth = w_in.shape[0]
    np_tok, ns_tok = bp * tp, bs * ts
    assert np_tok % TM == 0 and ns_tok % TM == 0 and tp % TM == 0 and TM % ts == 0

    x = jnp.concatenate([x_prompt.reshape(np_tok, D_MODEL), x_sample.reshape(ns_tok, D_MODEL)], axis=0)
    cos_t, sin_t = _rope_tables(tp, ts)
    bda, bdb = _block_diag_ones(W_A), _block_diag_ones(W_B)
    expand = np.zeros((LANES, W_C), np.float32)
    expand[np.arange(W_C) // HEAD_DIM, np.arange(W_C)] = 1.0
    expand = jnp.asarray(expand)
    zeros = lambda *s: jnp.zeros(s, F32)

    def split(a):
        return a[:np_tok].reshape(bp, tp, a.shape[-1]), a[np_tok:].reshape(bs, ts, a.shape[-1])

    def merge(p, s):
        return jnp.concatenate([p.reshape(np_tok, p.shape[-1]), s.reshape(ns_tok, s.shape[-1])], axis=0)

    outs_p = [[] for _ in range(5)]
    outs_s = [[] for _ in range(5)]
    for i in range(depth):
        w = jnp.pad(w_in[i], ((0, 0), (0, COLS_C_PAD - COLS_C))).astype(BF16)
        ca, cb, cc = _in_proj(x, norm_mix[i][None, :], w)
        ca_p, ca_s = split(ca)
        prev = merge(_shifted(ca_p, zeros(bp, 1, COLS_A), 1), _shifted(ca_s, state_shift[i][:, None, :], 1))

        pad_lora = lambda m, top: jnp.pad(m, ((0, AAA_LORA), (0, 0)) if top else ((DECAY_LORA, 0), (0, 0))).astype(BF16)
        row = lambda v: v.reshape(1, -1)
        r, dec, k2, v, nkk, beta, ga, bonus = _rwkv_prep(
            ca, prev, row(rwkv_mu[i]), row(rwkv_w0[i]), row(rwkv_a0[i]), row(rwkv_k_k[i]), row(rwkv_k_a[i]),
            row(rwkv_r_k[i]), pad_lora(rwkv_w_up[i], True), pad_lora(rwkv_a_up[i], False),
            rwkv_g_up[i].astype(BF16), bda)

        qh, kh = _ret_prep(cb, cos_t, sin_t, np_tok // TM, tp // TM)
        vb = cb[:, 2 * W_B:3 * W_B]

        xbc_p, xbc_s = split(cc[:, W_C:W_C + CONV_DIM])
        cprev_p, cprev_s = zeros(bp, CONV_W - 1, CONV_DIM), state_conv[i]
        xsh = [merge(_shifted(xbc_p, cprev_p, j), _shifted(xbc_s, cprev_s, j)) for j in (1, 2, 3)]
        pad_l = lambda v: jnp.pad(v, (0, LANES - v.shape[0])).reshape(1, LANES)
        xdt, bm, cm, dssm, xs = _ssm_prep(cc, xsh[0], xsh[1], xsh[2], ssm_conv_w[i], row(ssm_conv_b[i]),
                                          pad_l(ssm_dt_bias[i]), pad_l(ssm_a_log[i]), expand)

        feats = (r, dec, k2, v, nkk, beta, qh, kh, vb, xdt, bm, cm, dssm)
        feats_p, feats_s = zip(*[split(f) for f in feats])
        st_p = (zeros(bp, H_A, HEAD_DIM, HEAD_DIM), zeros(bp, H_B, HEAD_DIM, HEAD_DIM),
                zeros(bp, H_C, HEAD_DIM, N_STATE))
        st_s = (state_rwkv[i], state_ret[i], state_ssm[i])
        (ya_p, yb_p, yc_p), new_p = _run_scans((bp, tp), feats_p, st_p, (2, 4, 2))
        (ya_s, yb_s, yc_s), new_s = _run_scans((bs, ts), feats_s, st_s, (1, 1, 1))

        x2 = _post(x, merge(ya_p, ya_s), bonus, ga, merge(yb_p, yb_s), cb, merge(yc_p, yc_s), xs, cc,
                   row(rwkv_ln_w[i]), row(rwkv_ln_b[i]), row(ret_norm[i]),
                   row(jnp.repeat(ssm_d[i], HEAD_DIM)), row(ssm_norm[i]), bda, bdb, w_out[i].astype(BF16))

        nfin = row(norm_final) if i == depth - 1 else None
        j = i // 2
        if i % 2 == 0:
            dff = ffn_w_gate.shape[-1] // 2
            wg = ffn_w_gate[j].reshape(D_MODEL, 2, dff).transpose(1, 0, 2).astype(BF16)
            wu = ffn_w_up[j].reshape(D_MODEL, 2, dff).transpose(1, 0, 2).astype(BF16)
            wd = ffn_w_down[j].reshape(2, dff, D_MODEL).astype(BF16)
            x = _ffn(x2, row(norm_ffn[i]), wg, wu, wd, nfin=nfin)
        else:
            router = jnp.pad(moe_router[j], ((0, 0), (0, LANES - N_EXPERTS)))
            x = _ffn(x2, row(norm_ffn[i]), moe_w_gate[j].astype(BF16), moe_w_up[j].astype(BF16),
                     moe_w_down[j].astype(BF16), router=router, nfin=nfin)

        for outs, ca_g, xbc_g, cprev_g, new in ((outs_p, ca_p, xbc_p, cprev_p, new_p),
                                                 (outs_s, ca_s, xbc_s, cprev_s, new_s)):
            outs[0].append(new[0])
            outs[1].append(ca_g[:, -1])
            outs[2].append(new[1])
            outs[3].append(new[2])
            outs[4].append(jnp.concatenate([cprev_g, xbc_g], axis=1)[:, -(CONV_W - 1):])

    y_p, y_s = split(x)
    return (y_p, y_s) + tuple(jnp.stack(o) for o in outs_p) + tuple(jnp.stack(o) for o in outs_s)
```

```python
import functools
import math

import numpy as np
import jax
import jax.numpy as jnp
from jax import lax
from jax.experimental import pallas as pl
from jax.experimental.pallas import tpu as pltpu

F32 = jnp.float32
BF16 = jnp.bfloat16
HIGHEST = lax.Precision.HIGHEST

LANES = 128
VMEM_LIMIT = 56 * 1024 * 1024

D_MODEL = 1024
HEAD_DIM = 64
H_A, H_B, H_C = 6, 4, 6
W_A, W_B, W_C = H_A * HEAD_DIM, H_B * HEAD_DIM, H_C * HEAD_DIM
DECAY_LORA, AAA_LORA, GATE_LORA = 64, 64, 128
COLS_A = 3 * W_A + DECAY_LORA + AAA_LORA + GATE_LORA
COLS_B = 4 * W_B
N_STATE, N_GROUPS, CONV_W = 128, 2, 4
CONV_DIM = W_C + 2 * N_GROUPS * N_STATE
COLS_C = W_C + CONV_DIM + H_C
COLS_C_PAD = 1408
ROPE_BASE = 10000.0
RMS_EPS = 1e-6
GN_EPS = 64e-5
GATED_NORM_EPS = 1e-5
N_EXPERTS = 8
PAST_LEN = 16384

TM = 512


def _dot(a, b):
    return jnp.dot(a.astype(BF16), b.astype(BF16), preferred_element_type=F32)


def _dot_hi(a, b):
    return jnp.dot(a, b, precision=HIGHEST, preferred_element_type=F32)


def _sigmoid(x):
    return 1.0 / (1.0 + jnp.exp(-x))


def _softplus(x):
    return jnp.maximum(x, 0.0) + jnp.log1p(jnp.exp(-jnp.abs(x)))


def _rmsnorm(x, g, eps):
    return x * lax.rsqrt(jnp.mean(x * x, axis=-1, keepdims=True) + eps) * g


def _params(sem):
    return pltpu.CompilerParams(dimension_semantics=sem, vmem_limit_bytes=VMEM_LIMIT)


def _row_spec(width, col=0):
    return pl.BlockSpec((TM, width), lambda i, c=col: (i, c))


def _full_spec(shape):
    nd = len(shape)
    return pl.BlockSpec(shape, lambda i, n=nd: (0,) * n)


def _in_proj_kernel(x_ref, g_ref, w_ref, oa_ref, ob_ref, oc_ref):
    h = _rmsnorm(x_ref[...], g_ref[...], RMS_EPS).astype(BF16)
    oa_ref[...] = jnp.dot(h, w_ref[:, 0:COLS_A], preferred_element_type=F32)
    ob_ref[...] = jnp.dot(h, w_ref[:, COLS_A:COLS_A + COLS_B], preferred_element_type=F32)
    oc_ref[...] = jnp.dot(h, w_ref[:, COLS_A + COLS_B:], preferred_element_type=F32)


def _in_proj(x, g, w):
    n = x.shape[0]
    wtot = w.shape[1]
    return pl.pallas_call(
        _in_proj_kernel,
        grid=(n // TM,),
        in_specs=[_row_spec(D_MODEL), _full_spec((1, D_MODEL)), _full_spec((D_MODEL, wtot))],
        out_specs=[_row_spec(COLS_A), _row_spec(COLS_B), _row_spec(COLS_C_PAD)],
        out_shape=[jax.ShapeDtypeStruct((n, COLS_A), F32), jax.ShapeDtypeStruct((n, COLS_B), F32),
                   jax.ShapeDtypeStruct((n, COLS_C_PAD), F32)],
        compiler_params=_params(("parallel",)),
        name="in_proj",
    )(x, g, w)


def _rwkv_prep_kernel(c_ref, p_ref, mu_ref, w0_ref, a0_ref, kk_ref, ka_ref, rk_ref, wup_ref, aup_ref,
                      gup_ref, bd_ref, r_o, d_o, k_o, v_o, n_o, b_o, g_o, bonus_o):
    c = c_ref[...]
    xm = c + (p_ref[...] - c) * mu_ref[...]
    r = xm[:, 0:W_A]
    k = xm[:, W_A:2 * W_A]
    v = xm[:, 2 * W_A:3 * W_A]
    lora = xm[:, 3 * W_A:3 * W_A + DECAY_LORA + AAA_LORA]
    gd = xm[:, 3 * W_A + DECAY_LORA + AAA_LORA:]
    w = w0_ref[...] + _dot(jnp.tanh(lora), wup_ref[...])
    decay = jnp.exp(-math.exp(-0.5) * _sigmoid(w))
    a = _sigmoid(a0_ref[...] + _dot(lora, aup_ref[...]))
    g = _dot(_sigmoid(gd), gup_ref[...])
    bd = bd_ref[...]
    kk = k * kk_ref[...]
    kk = kk / jnp.maximum(jnp.sqrt(_dot_hi(kk * kk, bd)), 1e-12)
    k2 = k * (1.0 + (a - 1.0) * ka_ref[...])
    r_o[...] = r
    d_o[...] = decay
    k_o[...] = k2
    v_o[...] = v
    n_o[...] = -kk
    b_o[...] = kk * a
    g_o[...] = g
    bonus_o[...] = _dot_hi(r * k2 * rk_ref[...], bd) * v


def _rwkv_prep(ca, prev, mu, w0, a0, k_k, k_a, r_k, wup, aup, gup, bd):
    n = ca.shape[0]
    vec = _full_spec((1, W_A))
    lora_spec = _full_spec((DECAY_LORA + AAA_LORA, W_A))
    return pl.pallas_call(
        _rwkv_prep_kernel,
        grid=(n // TM,),
        in_specs=[_row_spec(COLS_A), _row_spec(COLS_A), _full_spec((1, COLS_A)), vec, vec, vec, vec, vec,
                  lora_spec, lora_spec, _full_spec((GATE_LORA, W_A)), _full_spec((W_A, W_A))],
        out_specs=[_row_spec(W_A)] * 8,
        out_shape=[jax.ShapeDtypeStruct((n, W_A), F32)] * 8,
        compiler_params=_params(("parallel",)),
        name="rwkv_prep",
    )(ca, prev, mu, w0, a0, k_k, k_a, r_k, wup, aup, gup, bd)


def _ret_prep_kernel(q_ref, k_ref, cos_ref, sin_ref, q_o, k_o):
    cos = cos_ref[...]
    sin = sin_ref[...]
    lane = lax.broadcasted_iota(jnp.int32, (TM, W_B), 1)
    first_half = (lane % HEAD_DIM) < (HEAD_DIM // 2)

    def rope(x):
        partner = jnp.where(first_half, pltpu.roll(x, W_B - HEAD_DIM // 2, 1), pltpu.roll(x, HEAD_DIM // 2, 1))
        return x * cos + partner * sin

    q_o[...] = rope(q_ref[...])
    k_o[...] = rope(k_ref[...]) * (HEAD_DIM ** -0.5)


def _ret_prep(cb, cos, sin, n_prompt_tiles, n_table_tiles):
    n = cb.shape[0]
    tab = lambda i: (jnp.where(i < n_prompt_tiles, i % n_table_tiles, n_table_tiles), 0)
    return pl.pallas_call(
        _ret_prep_kernel,
        grid=(n // TM,),
        in_specs=[_row_spec(W_B, 0), _row_spec(W_B, 1), pl.BlockSpec((TM, W_B), tab), pl.BlockSpec((TM, W_B), tab)],
        out_specs=[_row_spec(W_B)] * 2,
        out_shape=[jax.ShapeDtypeStruct((n, W_B), F32)] * 2,
        compiler_params=_params(("parallel",)),
        name="ret_prep",
    )(cb, cb, cos, sin)


def _ssm_prep_kernel(cc_ref, x1_ref, x2_ref, x3_ref, cw_ref, cb_ref, dtb_ref, alog_ref, ex_ref,
                     xdt_o, b_o, c_o, dec_o, xs_o, da_o):
    cc = cc_ref[...]
    x0 = cc[:, W_C:W_C + CONV_DIM]
    cw = cw_ref[...]
    conv = (x0 * cw[3:4, :] + x1_ref[...] * cw[2:3, :] + x2_ref[...] * cw[1:2, :] + x3_ref[...] * cw[0:1, :]
            + cb_ref[...])
    act = conv * _sigmoid(conv)
    xs = act[:, 0:W_C]
    dt = _softplus(cc[:, W_C + CONV_DIM:] + dtb_ref[...])
    da = dt * (-jnp.exp(alog_ref[...]))
    da_o[...] = da
    dec_o[...] = jnp.exp(da)
    xdt_o[...] = xs * _dot_hi(dt, ex_ref[...])
    b_o[...] = act[:, W_C:W_C + N_GROUPS * N_STATE]
    c_o[...] = act[:, W_C + N_GROUPS * N_STATE:]
    xs_o[...] = xs


def _ssm_prep(cc, x1, x2, x3, cw, cb, dtb, alog, ex):
    n = cc.shape[0]
    gn = N_GROUPS * N_STATE
    return pl.pallas_call(
        _ssm_prep_kernel,
        grid=(n // TM,),
        in_specs=[_row_spec(COLS_C_PAD), _row_spec(CONV_DIM), _row_spec(CONV_DIM), _row_spec(CONV_DIM),
                  _full_spec((CONV_W, CONV_DIM)), _full_spec((1, CONV_DIM)), _full_spec((1, LANES)),
                  _full_spec((1, LANES)), _full_spec((LANES, W_C))],
        out_specs=[_row_spec(W_C), _row_spec(gn), _row_spec(gn), _row_spec(LANES), _row_spec(W_C),
                   _row_spec(LANES)],
        out_shape=[jax.ShapeDtypeStruct((n, W_C), F32), jax.ShapeDtypeStruct((n, gn), F32),
                   jax.ShapeDtypeStruct((n, gn), F32), jax.ShapeDtypeStruct((n, LANES), F32),
                   jax.ShapeDtypeStruct((n, W_C), F32), jax.ShapeDtypeStruct((n, LANES), F32)],
        compiler_params=_params(("parallel",)),
        name="ssm_prep",
    )(cc, x1, x2, x3, cw, cb, dtb, alog, ex)


def _scan_kernel(*refs, mode, ni, tt_len, n_tt):
    if mode == "rwkv":
        q_ref, d_ref, a_ref, b_ref, n_ref, a2_ref, s0_ref, y_ref, so_ref, st = refs
    else:
        q_ref, d_ref, a_ref, b_ref, s0_ref, y_ref, so_ref, st = refs
    tt = pl.program_id(1)

    @pl.when(tt == 0)
    def _():
        st[...] = s0_ref[...]

    def row(ref, t, i):
        return ref[t, pl.ds(i, 1), :]

    def step(t, carry):
        bv = b_ref[t]
        if mode == "rwkv":
            sa_parts = [jnp.zeros_like(bv), jnp.zeros_like(bv)]
            for i in range(ni):
                sa_parts[i % 2] = sa_parts[i % 2] + st[i] * row(n_ref, t, i)
            sa = sa_parts[0] + sa_parts[1]
        elif mode == "ssd":
            d = d_ref[t]
        else:
            d = d_ref[0]
        y_parts = [jnp.zeros_like(bv), jnp.zeros_like(bv)]
        for i in range(ni):
            if mode == "rwkv":
                s = st[i] * row(d_ref, t, i) + row(a_ref, t, i) * bv + row(a2_ref, t, i) * sa
            else:
                s = st[i] * d + row(a_ref, t, i) * bv
            st[i] = s
            y_parts[i % 2] = y_parts[i % 2] + s * row(q_ref, t, i)
        y_ref[t] = y_parts[0] + y_parts[1]
        return carry

    lax.fori_loop(0, tt_len, step, 0)

    @pl.when(tt == n_tt - 1)
    def _():
        so_ref[...] = st[...]


def _scan(mode, q, d, a, b, s0, n=None, a2=None, tt_len=32):
    t_len, ni, lanes = q.shape
    nj = b.shape[1]
    tt_len = min(tt_len, t_len)
    n_tt = t_len // tt_len
    groups = lanes // LANES
    i_spec = pl.BlockSpec((tt_len, ni, LANES), lambda g, t: (t, 0, g))
    j_spec = pl.BlockSpec((tt_len, nj, LANES), lambda g, t: (t, 0, g))
    s_spec = pl.BlockSpec((ni, nj, LANES), lambda g, t: (0, 0, g))
    if mode == "rwkv":
        d_spec = i_spec
    elif mode == "ssd":
        d_spec = pl.BlockSpec((tt_len, 1, LANES), lambda g, t: (t, 0, g))
    else:
        d_spec = pl.BlockSpec((1, 1, LANES), lambda g, t: (0, 0, g))
    args = [q, d, a, b]
    specs = [i_spec, d_spec, i_spec, j_spec]
    if mode == "rwkv":
        args += [n, a2]
        specs += [i_spec, i_spec]
    args.append(s0)
    specs.append(s_spec)
    return pl.pallas_call(
        functools.partial(_scan_kernel, mode=mode, ni=ni, tt_len=tt_len, n_tt=n_tt),
        grid=(groups, n_tt),
        in_specs=specs,
        out_specs=[j_spec, s_spec],
        out_shape=[jax.ShapeDtypeStruct((t_len, nj, lanes), F32), jax.ShapeDtypeStruct((ni, nj, lanes), F32)],
        scratch_shapes=[pltpu.VMEM((ni, nj, LANES), F32)],
        compiler_params=_params(("parallel", "arbitrary")),
        name="scan_" + mode,
    )(*args)


LC = 256


def _ret_chunk_kernel(q_ref, k_ref, v_ref, y_ref, s_ref, st):
    c = pl.program_id(1)

    @pl.when(c == 0)
    def _():
        st[...] = jnp.zeros_like(st)

    q = q_ref[...]
    k = k_ref[...]
    vb = v_ref[...].astype(BF16)
    kb = k.astype(BF16)
    row = lax.broadcasted_iota(jnp.int32, (LC, W_B), 0)
    head = lax.broadcasted_iota(jnp.int32, (LC, W_B), 1) // HEAD_DIM
    log_g = [math.log(1.0 - 2.0 ** (-5.0 - h)) for h in range(H_B)]
    lg = jnp.full((LC, W_B), log_g[0], F32)
    for h in range(1, H_B):
        lg = jnp.where(head == h, log_g[h], lg)
    rowf = row.astype(F32)
    diff = (lax.broadcasted_iota(jnp.int32, (LC, LC), 0) - lax.broadcasted_iota(jnp.int32, (LC, LC), 1))
    causal = diff >= 0
    difff = jnp.maximum(diff, 0).astype(F32)

    out = jnp.dot((q * jnp.exp(lg * (rowf + 1.0))).astype(BF16), st[...].astype(BF16), preferred_element_type=F32)
    for h in range(H_B):
        qm = jnp.where(head == h, q, 0.0).astype(BF16)
        s = lax.dot_general(qm, kb, (((1,), (1,)), ((), ())), preferred_element_type=F32)
        p = jnp.where(causal, s * jnp.exp(log_g[h] * difff), 0.0).astype(BF16)
        out = out + jnp.where(head == h, jnp.dot(p, vb, preferred_element_type=F32), 0.0)
    y_ref[...] = out

    kt = (k * jnp.exp(lg * (LC - 1.0 - rowf))).T.astype(BF16)
    kv = jnp.dot(kt, vb, preferred_element_type=F32)
    r2 = lax.broadcasted_iota(jnp.int32, (W_B, W_B), 0) // HEAD_DIM
    c2 = lax.broadcasted_iota(jnp.int32, (W_B, W_B), 1) // HEAD_DIM
    cdec = jnp.full((W_B, W_B), math.exp(log_g[0] * LC), F32)
    for h in range(1, H_B):
        cdec = jnp.where(r2 == h, math.exp(log_g[h] * LC), cdec)
    st[...] = st[...] * cdec + jnp.where(r2 == c2, kv, 0.0)
    s_ref[0] = st[...]


def _ret_chunk(q, k, cb, n_seq, t_len):
    n_c = t_len // LC
    rows = lambda col: pl.BlockSpec((LC, W_B), lambda b, c, col=col: (b * n_c + c, col))
    return pl.pallas_call(
        _ret_chunk_kernel,
        grid=(n_seq, n_c),
        in_specs=[rows(0), rows(0), rows(2)],
        out_specs=[rows(0), pl.BlockSpec((1, W_B, W_B), lambda b, c: (b, 0, 0))],
        out_shape=[jax.ShapeDtypeStruct((n_seq * t_len, W_B), F32), jax.ShapeDtypeStruct((n_seq, W_B, W_B), F32)],
        scratch_shapes=[pltpu.VMEM((W_B, W_B), F32)],
        compiler_params=_params(("parallel", "arbitrary")),
        name="ret_chunk",
    )(q, k, cb)


def _ssd_chunk_kernel(x_ref, b_ref, c_ref, da_ref, ex_ref, y_ref, h_ref, st):
    ci = pl.program_id(1)

    @pl.when(ci == 0)
    def _():
        st[...] = jnp.zeros_like(st)

    gn = N_STATE
    half = W_C // N_GROUPS
    ex = ex_ref[...]
    x = x_ref[...]
    xb = x.astype(BF16)
    ri = lax.broadcasted_iota(jnp.int32, (LC, LC), 0)
    cj = lax.broadcasted_iota(jnp.int32, (LC, LC), 1)
    causal = ri >= cj
    cum = _dot_hi(causal.astype(F32), da_ref[...])
    cum_e = _dot_hi(cum, ex)
    last_e = cum_e[LC - 1:LC, :]
    cum_t = cum.T
    head = lax.broadcasted_iota(jnp.int32, (LC, W_C), 1) // HEAD_DIM
    bg = [b_ref[:, g * gn:(g + 1) * gn].astype(BF16) for g in range(N_GROUPS)]
    cg = [c_ref[:, g * gn:(g + 1) * gn].astype(BF16) for g in range(N_GROUPS)]
    cb = [lax.dot_general(cg[g], bg[g], (((1,), (1,)), ((), ())), preferred_element_type=F32)
          for g in range(N_GROUPS)]

    hb = st[...].astype(BF16)
    ys = [lax.dot_general(cg[g], hb, (((1,), (1,)), ((), ())), preferred_element_type=F32)
          for g in range(N_GROUPS)]
    y = jnp.where(head < H_C // N_GROUPS, ys[0], ys[1]) * jnp.exp(cum_e)
    for h in range(H_C):
        seg = cum[:, h:h + 1] - cum_t[h:h + 1, :]
        p = (cb[h // (H_C // N_GROUPS)] * jnp.exp(jnp.where(causal, seg, -jnp.inf))).astype(BF16)
        y = y + jnp.where(head == h, jnp.dot(p, xb, preferred_element_type=F32), 0.0)
    y_ref[...] = y

    xt = (x * jnp.exp(last_e - cum_e)).T.astype(BF16)
    upd = [jnp.dot(xt, bg[g], preferred_element_type=F32) for g in range(N_GROUPS)]
    rowi = lax.broadcasted_iota(jnp.int32, (W_C, gn), 0)
    sel = lax.broadcasted_iota(jnp.int32, (W_C, LANES), 1) == lax.broadcasted_iota(jnp.int32, (W_C, LANES), 0) // HEAD_DIM
    tot = jnp.sum(jnp.where(sel, cum[LC - 1:LC, :], 0.0), axis=1, keepdims=True)
    st[...] = st[...] * jnp.exp(tot) + jnp.where(rowi < half, upd[0], upd[1])
    h_ref[0] = st[...]


def _ssd_chunk(xdt, bm, cm, da, ex, n_seq, t_len):
    n_c = t_len // LC
    gn = N_GROUPS * N_STATE
    rows = lambda w: pl.BlockSpec((LC, w), lambda b, c: (b * n_c + c, 0))
    return pl.pallas_call(
        _ssd_chunk_kernel,
        grid=(n_seq, n_c),
        in_specs=[rows(W_C), rows(gn), rows(gn), rows(LANES), pl.BlockSpec((LANES, W_C), lambda b, c: (0, 0))],
        out_specs=[rows(W_C), pl.BlockSpec((1, W_C, N_STATE), lambda b, c: (b, 0, 0))],
        out_shape=[jax.ShapeDtypeStruct((n_seq * t_len, W_C), F32),
                   jax.ShapeDtypeStruct((n_seq, W_C, N_STATE), F32)],
        scratch_shapes=[pltpu.VMEM((W_C, N_STATE), F32)],
        compiler_params=_params(("parallel", "arbitrary")),
        name="ssd_chunk",
    )(xdt, bm, cm, da, ex)


def _post_kernel(x_ref, ya_ref, bonus_ref, ga_ref, ob_ref, gb_ref, yc_ref, xs_ref, z_ref, lnw_ref, lnb_ref,
                 rn_ref, dsk_ref, sn_ref, bda_ref, bdb_ref, wo_ref, out_ref):
    inv_hd = 1.0 / HEAD_DIM
    y = ya_ref[...]
    bda = bda_ref[...]
    mean = _dot_hi(y, bda) * inv_hd
    yd = y - mean
    var = _dot_hi(yd * yd, bda) * inv_hd
    ya = (yd * lax.rsqrt(var + GN_EPS) * lnw_ref[...] + lnb_ref[...] + bonus_ref[...]) * ga_ref[...]
    o = ob_ref[...]
    ms = _dot_hi(o * o, bdb_ref[...]) * inv_hd
    gb = gb_ref[...]
    yb = (gb * _sigmoid(gb)) * (o * lax.rsqrt(ms + RMS_EPS) * rn_ref[...])
    z = z_ref[...]
    yc = (yc_ref[...] + dsk_ref[...] * xs_ref[...]) * (z * _sigmoid(z))
    yc = _rmsnorm(yc, sn_ref[...], GATED_NORM_EPS)
    out_ref[...] = (x_ref[...] + _dot(ya, wo_ref[0:W_A, :]) + _dot(yb, wo_ref[W_A:W_A + W_B, :])
                    + _dot(yc, wo_ref[W_A + W_B:, :]))


def _post(x, ya, bonus, ga, ob, cb, yc, xs, cc, lnw, lnb, rn, dsk, sn, bda, bdb, wo):
    n = x.shape[0]
    va = _full_spec((1, W_A))
    return pl.pallas_call(
        _post_kernel,
        grid=(n // TM,),
        in_specs=[_row_spec(D_MODEL), _row_spec(W_A), _row_spec(W_A), _row_spec(W_A), _row_spec(W_B),
                  _row_spec(W_B, 3), _row_spec(W_C), _row_spec(W_C), _row_spec(W_C, 0), va, va,
                  _full_spec((1, W_B)), va, va, _full_spec((W_A, W_A)), _full_spec((W_B, W_B)),
                  _full_spec((D_MODEL, D_MODEL))],
        out_specs=_row_spec(D_MODEL),
        out_shape=jax.ShapeDtypeStruct((n, D_MODEL), F32),
        compiler_params=_params(("parallel",)),
        name="post_outproj",
    )(x, ya, bonus, ga, ob, cb, yc, xs, cc, lnw, lnb, rn, dsk, sn, bda, bdb, wo)


def _ffn_kernel(*refs, gated, final_norm, n_e):
    refs = list(refs)
    x_ref, nf_ref = refs[:2]
    pos = 2
    if gated:
        router_ref = refs[pos]
        pos += 1
    wg_ref, wu_ref, wd_ref = refs[pos:pos + 3]
    pos += 3
    if final_norm:
        nfin_ref = refs[pos]
        pos += 1
    out_ref = refs[pos]
    hb_s, acc_s = refs[pos + 1:pos + 3]
    if gated:
        gate_s = refs[pos + 3]
    e = pl.program_id(1)

    @pl.when(e == 0)
    def _():
        h = _rmsnorm(x_ref[...], nf_ref[...], RMS_EPS)
        hb_s[...] = h.astype(BF16)
        acc_s[...] = jnp.zeros_like(acc_s)
        if gated:
            lane = lax.broadcasted_iota(jnp.int32, (TM, LANES), 1)
            logits = jnp.where(lane < N_EXPERTS, _dot_hi(h, router_ref[...]), -jnp.inf)
            p = jnp.exp(logits - jnp.max(logits, axis=-1, keepdims=True))
            p = p / jnp.sum(p, axis=-1, keepdims=True)
            p1 = jnp.max(p, axis=-1, keepdims=True)
            i1 = jnp.min(jnp.where(p == p1, lane, LANES), axis=-1, keepdims=True)
            rest = jnp.where(lane == i1, -1.0, p)
            p2 = jnp.max(rest, axis=-1, keepdims=True)
            i2 = jnp.min(jnp.where(rest == p2, lane, LANES), axis=-1, keepdims=True)
            gate_s[...] = jnp.where(lane == i1, p1, jnp.where(lane == i2, p2, 0.0)) / (p1 + p2)

    hb = hb_s[...]
    g = jnp.dot(hb, wg_ref[0], preferred_element_type=F32)
    u = jnp.dot(hb, wu_ref[0], preferred_element_type=F32)
    o = _dot(g * _sigmoid(g) * u, wd_ref[0])
    if gated:
        lane = lax.broadcasted_iota(jnp.int32, (TM, LANES), 1)
        o = o * jnp.sum(jnp.where(lane == e, gate_s[...], 0.0), axis=-1, keepdims=True)
    acc_s[...] += o

    @pl.when(e == n_e - 1)
    def _():
        y = x_ref[...] + acc_s[...]
        if final_norm:
            y = _rmsnorm(y, nfin_ref[...], RMS_EPS)
        out_ref[...] = y


def _ffn(x, nf, wg, wu, wd, router=None, nfin=None):
    n = x.shape[0]
    n_e, _, dff = wg.shape
    gated = router is not None
    final_norm = nfin is not None
    vec = pl.BlockSpec((1, D_MODEL), lambda i, e: (0, 0))
    args = [x, nf]
    specs = [pl.BlockSpec((TM, D_MODEL), lambda i, e: (i, 0)), vec]
    if gated:
        args.append(router)
        specs.append(pl.BlockSpec((D_MODEL, LANES), lambda i, e: (0, 0)))
    args += [wg, wu, wd]
    specs += [pl.BlockSpec((1, D_MODEL, dff), lambda i, e: (e, 0, 0)),
              pl.BlockSpec((1, D_MODEL, dff), lambda i, e: (e, 0, 0)),
              pl.BlockSpec((1, dff, D_MODEL), lambda i, e: (e, 0, 0))]
    if final_norm:
        args.append(nfin)
        specs.append(vec)
    scratch = [pltpu.VMEM((TM, D_MODEL), BF16), pltpu.VMEM((TM, D_MODEL), F32)]
    if gated:
        scratch.append(pltpu.VMEM((TM, LANES), F32))
    return pl.pallas_call(
        functools.partial(_ffn_kernel, gated=gated, final_norm=final_norm, n_e=n_e),
        grid=(n // TM, n_e),
        in_specs=specs,
        out_specs=pl.BlockSpec((TM, D_MODEL), lambda i, e: (i, 0)),
        out_shape=jax.ShapeDtypeStruct((n, D_MODEL), F32),
        scratch_shapes=scratch,
        compiler_params=_params(("parallel", "arbitrary")),
        name="moe" if gated else "ffn",
    )(*args)


def _pad_lanes(y):
    pad = (-y.shape[-1]) % LANES
    return jnp.pad(y, [(0, 0)] * (y.ndim - 1) + [(0, pad)]) if pad else y


def _pad_heads(x, hp):
    h = x.shape[2]
    return jnp.pad(x, ((0, 0), (0, 0), (0, hp - h), (0, 0))) if hp > h else x


def _lanes_i(x, nsplit, hp):
    b, t, _, f = x.shape
    y = jnp.transpose(_pad_heads(x, hp), (1, 3, 0, 2))
    return _pad_lanes(jnp.broadcast_to(y[:, :, None], (t, f, nsplit, b, hp)).reshape(t, f, nsplit * b * hp))


def _lanes_j(x, nsplit, hp):
    b, t, _, f = x.shape
    y = _pad_heads(x, hp).reshape(b, t, hp, nsplit, f // nsplit)
    return _pad_lanes(jnp.transpose(y, (1, 4, 3, 0, 2)).reshape(t, f // nsplit, nsplit * b * hp))


def _from_lanes_j(y, b, h, nsplit, hp):
    t, j, _ = y.shape
    y = y[:, :, :nsplit * b * hp].reshape(t, j, nsplit, b, hp)[..., :h]
    return jnp.transpose(y, (3, 0, 4, 2, 1)).reshape(b, t, h * nsplit * j)


def _state_from_lanes(s, b, h, nsplit, hp, j_first):
    ni, nj, _ = s.shape
    s = s[:, :, :nsplit * b * hp].reshape(ni, nj, nsplit, b, hp)[..., :h]
    if j_first:
        return jnp.transpose(s, (3, 4, 2, 1, 0)).reshape(b, h, nsplit * nj, ni)
    return jnp.transpose(s, (3, 4, 0, 2, 1)).reshape(b, h, ni, nsplit * nj)


def _state_to_lanes(s, j_first):
    b, h = s.shape[:2]
    y = jnp.transpose(s, (3, 2, 0, 1) if j_first else (2, 3, 0, 1))
    return _pad_lanes(y.reshape(y.shape[0], y.shape[1], b * h))


def _block_diag_ones(width):
    idx = np.arange(width) // HEAD_DIM
    return jnp.asarray((idx[:, None] == idx[None, :]).astype(np.float32))


def _rope_kernel(ang_ref, cos_o, sin_o):
    ang = ang_ref[...]
    lane = lax.broadcasted_iota(jnp.int32, ang.shape, 1)
    cos_o[...] = jnp.cos(ang)
    sin_o[...] = jnp.where((lane % HEAD_DIM) < (HEAD_DIM // 2), -jnp.sin(ang), jnp.sin(ang))


def _rope_tables(t_prompt, t_sample):
    theta = 1.0 / (ROPE_BASE ** jnp.linspace(0.0, 1.0, HEAD_DIM // 2, dtype=F32))
    pos = jnp.concatenate([jnp.arange(t_prompt, dtype=F32), PAST_LEN + (jnp.arange(TM) % t_sample).astype(F32)])
    ang = jnp.tile(pos[:, None] * theta[None, :], (1, W_B // (HEAD_DIM // 2)))
    n = ang.shape[0]
    return pl.pallas_call(
        _rope_kernel,
        grid=(n // TM,),
        in_specs=[_row_spec(W_B)],
        out_specs=[_row_spec(W_B)] * 2,
        out_shape=[jax.ShapeDtypeStruct((n, W_B), F32)] * 2,
        compiler_params=_params(("parallel",)),
        name="rope_tables",
    )(ang)


def _shifted(x, prev_rows, shift):
    t = x.shape[1]
    p = prev_rows.shape[1]
    full = jnp.concatenate([prev_rows, x], axis=1)
    return full[:, p - shift:p - shift + t]


def _heads(x, h):
    b, t, w = x.shape
    return x.reshape(b, t, h, w // h)


def _rwkv_lanes(feats, s0, nsplit, hp):
    r, dec, k2, v, nkk, beta = feats
    b = r.shape[0]
    li = lambda x: _lanes_i(_heads(x, H_A), nsplit, hp)
    if s0 is None:
        s0 = jnp.zeros((HEAD_DIM, HEAD_DIM // nsplit, -(-(nsplit * b * hp) // LANES) * LANES), F32)
    else:
        assert nsplit == 1 and hp == H_A
        s0 = _state_to_lanes(s0, True)
    y, s = _scan("rwkv", li(r), li(dec), li(k2), _lanes_j(_heads(v, H_A), nsplit, hp), s0, n=li(nkk), a2=li(beta))
    return _from_lanes_j(y, b, H_A, nsplit, hp), _state_from_lanes(s, b, H_A, nsplit, hp, True)


def _ret_lanes(qh, kh, vb, s0):
    b = qh.shape[0]
    gam = 1.0 - 2.0 ** (-5.0 - np.arange(H_B, dtype=np.float64))
    gam_l = np.ones((1, 1, -(-(b * H_B) // LANES) * LANES), np.float32)
    gam_l[0, 0, :b * H_B] = np.tile(gam, b)
    y, s = _scan("ret", _lanes_i(_heads(qh, H_B), 1, H_B), jnp.asarray(gam_l), _lanes_i(_heads(kh, H_B), 1, H_B),
                 _lanes_j(_heads(vb, H_B), 1, H_B), _state_to_lanes(s0, False))
    return _from_lanes_j(y, b, H_B, 1, H_B), _state_from_lanes(s, b, H_B, 1, H_B, False)


def _ssd_lanes(xdt, bm, cm, dssm, s0):
    b = xdt.shape[0]
    rep = H_C // N_GROUPS
    bh = jnp.repeat(_heads(bm, N_GROUPS), rep, axis=2)
    ch = jnp.repeat(_heads(cm, N_GROUPS), rep, axis=2)
    y, s = _scan("ssd", _lanes_i(ch, 1, H_C), _lanes_i(dssm[:, :, :H_C, None], 1, H_C), _lanes_i(bh, 1, H_C),
                 _lanes_j(_heads(xdt, H_C), 1, H_C), _state_to_lanes(s0, True))
    return _from_lanes_j(y, b, H_C, 1, H_C), _state_from_lanes(s, b, H_C, 1, H_C, True)


def kernel(x_prompt, x_sample, state_rwkv, state_shift, state_ret, state_ssm, state_conv, norm_mix, w_in, rwkv_mu, rwkv_w0, rwkv_w_up, rwkv_a0, rwkv_a_up, rwkv_g_up, rwkv_k_k, rwkv_k_a, rwkv_r_k, rwkv_ln_w, rwkv_ln_b, ret_norm, ssm_conv_w, ssm_conv_b, ssm_dt_bias, ssm_a_log, ssm_d, ssm_norm, w_out, norm_ffn, ffn_w_gate, ffn_w_up, ffn_w_down, moe_router, moe_w_gate, moe_w_up, moe_w_down, norm_final):
    bp, tp, _ = x_prompt.shape
    bs, ts, _ = x_sample.shape
    depth = w_in.shape[0]
    np_tok, ns_tok = bp * tp, bs * ts
    assert np_tok % TM == 0 and ns_tok % TM == 0 and tp % TM == 0 and TM % ts == 0

    x = jnp.concatenate([x_prompt.reshape(np_tok, D_MODEL), x_sample.reshape(ns_tok, D_MODEL)], axis=0)
    cos_t, sin_t = _rope_tables(tp, ts)
    bda, bdb = _block_diag_ones(W_A), _block_diag_ones(W_B)
    expand = np.zeros((LANES, W_C), np.float32)
    expand[np.arange(W_C) // HEAD_DIM, np.arange(W_C)] = 1.0
    expand = jnp.asarray(expand)
    zeros = lambda *s: jnp.zeros(s, F32)

    def split(a):
        return a[:np_tok].reshape(bp, tp, a.shape[-1]), a[np_tok:].reshape(bs, ts, a.shape[-1])

    def merge(p, s):
        return jnp.concatenate([p.reshape(np_tok, p.shape[-1]), s.reshape(ns_tok, s.shape[-1])], axis=0)

    outs_p = [[] for _ in range(5)]
    outs_s = [[] for _ in range(5)]
    for i in range(depth):
        w = jnp.pad(w_in[i], ((0, 0), (0, COLS_C_PAD - COLS_C))).astype(BF16)
        ca, cb, cc = _in_proj(x, norm_mix[i][None, :], w)
        ca_p, ca_s = split(ca)
        prev = merge(_shifted(ca_p, zeros(bp, 1, COLS_A), 1), _shifted(ca_s, state_shift[i][:, None, :], 1))

        pad_lora = lambda m, top: jnp.pad(m, ((0, AAA_LORA), (0, 0)) if top else ((DECAY_LORA, 0), (0, 0))).astype(BF16)
        row = lambda v: v.reshape(1, -1)
        r, dec, k2, v, nkk, beta, ga, bonus = _rwkv_prep(
            ca, prev, row(rwkv_mu[i]), row(rwkv_w0[i]), row(rwkv_a0[i]), row(rwkv_k_k[i]), row(rwkv_k_a[i]),
            row(rwkv_r_k[i]), pad_lora(rwkv_w_up[i], True), pad_lora(rwkv_a_up[i], False),
            rwkv_g_up[i].astype(BF16), bda)

        qh, kh = _ret_prep(cb, cos_t, sin_t, np_tok // TM, tp // TM)
        vb = cb[:, 2 * W_B:3 * W_B]

        xbc_p, xbc_s = split(cc[:, W_C:W_C + CONV_DIM])
        cprev_p, cprev_s = zeros(bp, CONV_W - 1, CONV_DIM), state_conv[i]
        xsh = [merge(_shifted(xbc_p, cprev_p, j), _shifted(xbc_s, cprev_s, j)) for j in (1, 2, 3)]
        pad_l = lambda v: jnp.pad(v, (0, LANES - v.shape[0])).reshape(1, LANES)
        xdt, bm, cm, dssm, xs, da = _ssm_prep(cc, xsh[0], xsh[1], xsh[2], ssm_conv_w[i], row(ssm_conv_b[i]),
                                              pad_l(ssm_dt_bias[i]), pad_l(ssm_a_log[i]), expand)

        rw_p, rw_s = zip(*[split(f) for f in (r, dec, k2, v, nkk, beta)])
        ya_p, sa_p = _rwkv_lanes(rw_p, None, 2, 8)
        yb_p, sret = _ret_chunk(qh, kh, cb, bp, tp)
        sb_p = jnp.stack([sret[:, h * HEAD_DIM:(h + 1) * HEAD_DIM, h * HEAD_DIM:(h + 1) * HEAD_DIM]
                          for h in range(H_B)], axis=1)
        yc_p, sssm = _ssd_chunk(xdt, bm, cm, da, expand, bp, tp)
        sc_p = sssm.reshape(bp, H_C, HEAD_DIM, N_STATE)
        new_p = (sa_p, sb_p, sc_p)
        ya_s, sa_s = _rwkv_lanes(rw_s, state_rwkv[i], 1, H_A)
        yb_s, sb_s = _ret_lanes(split(qh)[1], split(kh)[1], split(vb)[1], state_ret[i])
        yc_s, sc_s = _ssd_lanes(split(xdt)[1], split(bm)[1], split(cm)[1], split(dssm)[1], state_ssm[i])
        new_s = (sa_s, sb_s, sc_s)

        x2 = _post(x, merge(ya_p, ya_s), bonus, ga, merge(yb_p, yb_s), cb, merge(yc_p, yc_s), xs, cc,
                   row(rwkv_ln_w[i]), row(rwkv_ln_b[i]), row(ret_norm[i]),
                   row(jnp.repeat(ssm_d[i], HEAD_DIM)), row(ssm_norm[i]), bda, bdb, w_out[i].astype(BF16))

        nfin = row(norm_final) if i == depth - 1 else None
        j = i // 2
        if i % 2 == 0:
            dff = ffn_w_gate.shape[-1] // 2
            wg = ffn_w_gate[j].reshape(D_MODEL, 2, dff).transpose(1, 0, 2).astype(BF16)
            wu = ffn_w_up[j].reshape(D_MODEL, 2, dff).transpose(1, 0, 2).astype(BF16)
            wd = ffn_w_down[j].reshape(2, dff, D_MODEL).astype(BF16)
            x = _ffn(x2, row(norm_ffn[i]), wg, wu, wd, nfin=nfin)
        else:
            router = jnp.pad(moe_router[j], ((0, 0), (0, LANES - N_EXPERTS)))
            x = _ffn(x2, row(norm_ffn[i]), moe_w_gate[j].astype(BF16), moe_w_up[j].astype(BF16),
                     moe_w_down[j].astype(BF16), router=router, nfin=nfin)

        for outs, ca_g, xbc_g, cprev_g, new in ((outs_p, ca_p, xbc_p, cprev_p, new_p),
                                                 (outs_s, ca_s, xbc_s, cprev_s, new_s)):
            outs[0].append(new[0])
            outs[1].append(ca_g[:, -1])
            outs[2].append(new[1])
            outs[3].append(new[2])
            outs[4].append(jnp.concatenate([cprev_g, xbc_g], axis=1)[:, -(CONV_W - 1):])

    y_p, y_s = split(x)
    return (y_p, y_s) + tuple(jnp.stack(o) for o in outs_p) + tuple(jnp.stack(o) for o in outs_s)
```

```python
import functools
import math

import numpy as np
import jax
import jax.numpy as jnp
from jax import lax
from jax.experimental import pallas as pl
from jax.experimental.pallas import tpu as pltpu

F32 = jnp.float32
BF16 = jnp.bfloat16
HIGHEST = lax.Precision.HIGHEST

LANES = 128
SUBLANES = 8
VMEM_LIMIT = 56 * 1024 * 1024

D_MODEL = 1024
HEAD_DIM = 64
H_A, H_B, H_C = 6, 4, 6
W_A, W_B, W_C = H_A * HEAD_DIM, H_B * HEAD_DIM, H_C * HEAD_DIM
DECAY_LORA, AAA_LORA, GATE_LORA = 64, 64, 128
COLS_A = 3 * W_A + DECAY_LORA + AAA_LORA + GATE_LORA
COLS_B = 4 * W_B
N_STATE, N_GROUPS, CONV_W = 128, 2, 4
CONV_DIM = W_C + 2 * N_GROUPS * N_STATE
COLS_C = W_C + CONV_DIM + H_C
COLS_C_PAD = 1408
ROPE_BASE = 10000.0
RMS_EPS = 1e-6
GN_EPS = 64e-5
GATED_NORM_EPS = 1e-5
N_EXPERTS = 8
PAST_LEN = 16384

TM = 512
LC = 256
H_PAD = 8
K_SPLIT = 2


def _dot(a, b):
    return jnp.dot(a.astype(BF16), b.astype(BF16), preferred_element_type=F32)


def _dot_hi(a, b):
    return jnp.dot(a, b, precision=HIGHEST, preferred_element_type=F32)


def _sigmoid(x):
    return 1.0 / (1.0 + jnp.exp(-x))


def _softplus(x):
    return jnp.maximum(x, 0.0) + jnp.log1p(jnp.exp(-jnp.abs(x)))


def _rmsnorm(x, g, eps):
    return x * lax.rsqrt(jnp.mean(x * x, axis=-1, keepdims=True) + eps) * g


def _params(sem):
    return pltpu.CompilerParams(dimension_semantics=sem, vmem_limit_bytes=VMEM_LIMIT)


def _row_spec(width, col=0):
    return pl.BlockSpec((TM, width), lambda i, c=col: (i, c))


def _before_spec(width):
    return pl.BlockSpec((SUBLANES, width), lambda i: (jnp.maximum(i * (TM // SUBLANES) - 1, 0), 0))


def _full_spec(shape):
    nd = len(shape)
    return pl.BlockSpec(shape, lambda i, n=nd: (0,) * n)


def _shift_rows(x, before, j, first):
    rolled = pltpu.roll(x, j, 0)
    prev = jnp.where(first, 0.0, pltpu.roll(before, j, 0))
    row = lax.broadcasted_iota(jnp.int32, prev.shape, 0)
    top = jnp.where(row < j, prev, rolled[0:SUBLANES])
    return jnp.concatenate([top, rolled[SUBLANES:]], axis=0)


def _in_proj_kernel(x_ref, g_ref, w_ref, oa_ref, ob_ref, oc_ref):
    h = _rmsnorm(x_ref[...], g_ref[...], RMS_EPS).astype(BF16)
    oa_ref[...] = jnp.dot(h, w_ref[:, 0:COLS_A], preferred_element_type=F32)
    ob_ref[...] = jnp.dot(h, w_ref[:, COLS_A:COLS_A + COLS_B], preferred_element_type=F32)
    oc_ref[...] = jnp.dot(h, w_ref[:, COLS_A + COLS_B:], preferred_element_type=F32)


def _in_proj(x, g, w):
    n = x.shape[0]
    wtot = w.shape[1]
    return pl.pallas_call(
        _in_proj_kernel,
        grid=(n // TM,),
        in_specs=[_row_spec(D_MODEL), _full_spec((1, D_MODEL)), _full_spec((D_MODEL, wtot))],
        out_specs=[_row_spec(COLS_A), _row_spec(COLS_B), _row_spec(COLS_C_PAD)],
        out_shape=[jax.ShapeDtypeStruct((n, COLS_A), F32), jax.ShapeDtypeStruct((n, COLS_B), F32),
                   jax.ShapeDtypeStruct((n, COLS_C_PAD), F32)],
        compiler_params=_params(("parallel",)),
        name="in_proj",
    )(x, g, w)


def _rwkv_prep_kernel(c_ref, p_ref, mu_ref, w0_ref, a0_ref, kk_ref, ka_ref, rk_ref, wup_ref, aup_ref,
                      gup_ref, bd_ref, r_o, d_o, k_o, v_o, n_o, b_o, g_o, bonus_o, *, tiles_per_seq):
    c = c_ref[...]
    if tiles_per_seq:
        prev = _shift_rows(c, p_ref[...], 1, pl.program_id(0) % tiles_per_seq == 0)
    else:
        prev = p_ref[...]
    xm = c + (prev - c) * mu_ref[...]
    r = xm[:, 0:W_A]
    k = xm[:, W_A:2 * W_A]
    v = xm[:, 2 * W_A:3 * W_A]
    lora = xm[:, 3 * W_A:3 * W_A + DECAY_LORA + AAA_LORA]
    gd = xm[:, 3 * W_A + DECAY_LORA + AAA_LORA:]
    w = w0_ref[...] + _dot(jnp.tanh(lora), wup_ref[...])
    decay = jnp.exp(-math.exp(-0.5) * _sigmoid(w))
    a = _sigmoid(a0_ref[...] + _dot(lora, aup_ref[...]))
    g = _dot(_sigmoid(gd), gup_ref[...])
    bd = bd_ref[...]
    kk = k * kk_ref[...]
    kk = kk / jnp.maximum(jnp.sqrt(_dot_hi(kk * kk, bd)), 1e-12)
    k2 = k * (1.0 + (a - 1.0) * ka_ref[...])
    r_o[...] = r
    d_o[...] = decay
    k_o[...] = k2
    v_o[...] = v
    n_o[...] = -kk
    b_o[...] = kk * a
    g_o[...] = g
    bonus_o[...] = _dot_hi(r * k2 * rk_ref[...], bd) * v


def _rwkv_prep(ca, prev, p, tiles_per_seq):
    n = ca.shape[0]
    vec = _full_spec((1, W_A))
    lora_spec = _full_spec((DECAY_LORA + AAA_LORA, W_A))
    p_spec = _before_spec(COLS_A) if tiles_per_seq else _row_spec(COLS_A)
    return pl.pallas_call(
        functools.partial(_rwkv_prep_kernel, tiles_per_seq=tiles_per_seq),
        grid=(n // TM,),
        in_specs=[_row_spec(COLS_A), p_spec, _full_spec((1, COLS_A)), vec, vec, vec, vec, vec,
                  lora_spec, lora_spec, _full_spec((GATE_LORA, W_A)), _full_spec((W_A, W_A))],
        out_specs=[_row_spec(W_A)] * 8,
        out_shape=[jax.ShapeDtypeStruct((n, W_A), F32)] * 8,
        compiler_params=_params(("parallel",)),
        name="rwkv_prep",
    )(ca, ca if tiles_per_seq else prev, p["mu"], p["w0"], p["a0"], p["k_k"], p["k_a"], p["r_k"],
      p["w_up"], p["a_up"], p["g_up"], p["bd_a"])


def _rope_kernel(ang_ref, cos_o, sin_o):
    ang = ang_ref[...]
    lane = lax.broadcasted_iota(jnp.int32, ang.shape, 1)
    cos_o[...] = jnp.cos(ang)
    sin_o[...] = jnp.where((lane % HEAD_DIM) < (HEAD_DIM // 2), -jnp.sin(ang), jnp.sin(ang))


def _rope_tables(t_prompt, t_sample):
    theta = 1.0 / (ROPE_BASE ** jnp.linspace(0.0, 1.0, HEAD_DIM // 2, dtype=F32))
    pos = jnp.concatenate([jnp.arange(t_prompt, dtype=F32), PAST_LEN + (jnp.arange(TM) % t_sample).astype(F32)])
    ang = jnp.tile(pos[:, None] * theta[None, :], (1, W_B // (HEAD_DIM // 2)))
    n = ang.shape[0]
    return pl.pallas_call(
        _rope_kernel,
        grid=(n // TM,),
        in_specs=[_row_spec(W_B)],
        out_specs=[_row_spec(W_B)] * 2,
        out_shape=[jax.ShapeDtypeStruct((n, W_B), F32)] * 2,
        compiler_params=_params(("parallel",)),
        name="rope_tables",
    )(ang)


def _ret_prep_kernel(q_ref, k_ref, cos_ref, sin_ref, q_o, k_o):
    cos = cos_ref[...]
    sin = sin_ref[...]
    lane = lax.broadcasted_iota(jnp.int32, (TM, W_B), 1)
    first_half = (lane % HEAD_DIM) < (HEAD_DIM // 2)

    def rope(x):
        partner = jnp.where(first_half, pltpu.roll(x, W_B - HEAD_DIM // 2, 1), pltpu.roll(x, HEAD_DIM // 2, 1))
        return x * cos + partner * sin

    q_o[...] = rope(q_ref[...])
    k_o[...] = rope(k_ref[...]) * (HEAD_DIM ** -0.5)


def _ret_prep(cb, cos, sin, table_block):
    n = cb.shape[0]
    tab = pl.BlockSpec((TM, W_B), lambda i: (table_block(i), 0))
    return pl.pallas_call(
        _ret_prep_kernel,
        grid=(n // TM,),
        in_specs=[_row_spec(W_B, 0), _row_spec(W_B, 1), tab, tab],
        out_specs=[_row_spec(W_B)] * 2,
        out_shape=[jax.ShapeDtypeStruct((n, W_B), F32)] * 2,
        compiler_params=_params(("parallel",)),
        name="ret_prep",
    )(cb, cb, cos, sin)


def _ssm_prep_kernel(*refs, tiles_per_seq):
    n_shift_refs = 1 if tiles_per_seq else CONV_W - 1
    cc_ref = refs[0]
    shift_refs = refs[1:1 + n_shift_refs]
    cw_ref, cb_ref, dtb_ref, alog_ref, ex_ref, xdt_o, b_o, c_o, dec_o, xs_o, da_o = refs[1 + n_shift_refs:]
    cc = cc_ref[...]
    x0 = cc[:, W_C:W_C + CONV_DIM]
    if tiles_per_seq:
        before = shift_refs[0][:, W_C:W_C + CONV_DIM]
        first = pl.program_id(0) % tiles_per_seq == 0
        xs_prev = [_shift_rows(x0, before, j, first) for j in range(1, CONV_W)]
    else:
        xs_prev = [r[...] for r in shift_refs]
    cw = cw_ref[...]
    conv = x0 * cw[CONV_W - 1:CONV_W, :] + cb_ref[...]
    for j in range(1, CONV_W):
        conv = conv + xs_prev[j - 1] * cw[CONV_W - 1 - j:CONV_W - j, :]
    act = conv * _sigmoid(conv)
    xs = act[:, 0:W_C]
    dt = _softplus(cc[:, W_C + CONV_DIM:] + dtb_ref[...])
    da = dt * (-jnp.exp(alog_ref[...]))
    da_o[...] = da
    dec_o[...] = jnp.exp(da)
    xdt_o[...] = xs * _dot_hi(dt, ex_ref[...])
    b_o[...] = act[:, W_C:W_C + N_GROUPS * N_STATE]
    c_o[...] = act[:, W_C + N_GROUPS * N_STATE:]
    xs_o[...] = xs


def _ssm_prep(cc, shifted, p, tiles_per_seq):
    n = cc.shape[0]
    gn = N_GROUPS * N_STATE
    if tiles_per_seq:
        shift_args, shift_specs = [cc], [_before_spec(COLS_C_PAD)]
    else:
        shift_args, shift_specs = list(shifted), [_row_spec(CONV_DIM)] * (CONV_W - 1)
    return pl.pallas_call(
        functools.partial(_ssm_prep_kernel, tiles_per_seq=tiles_per_seq),
        grid=(n // TM,),
        in_specs=[_row_spec(COLS_C_PAD)] + shift_specs + [
            _full_spec((CONV_W, CONV_DIM)), _full_spec((1, CONV_DIM)), _full_spec((1, LANES)),
            _full_spec((1, LANES)), _full_spec((LANES, W_C))],
        out_specs=[_row_spec(W_C), _row_spec(gn), _row_spec(gn), _row_spec(LANES), _row_spec(W_C),
                   _row_spec(LANES)],
        out_shape=[jax.ShapeDtypeStruct((n, W_C), F32), jax.ShapeDtypeStruct((n, gn), F32),
                   jax.ShapeDtypeStruct((n, gn), F32), jax.ShapeDtypeStruct((n, LANES), F32),
                   jax.ShapeDtypeStruct((n, W_C), F32), jax.ShapeDtypeStruct((n, LANES), F32)],
        compiler_params=_params(("parallel",)),
        name="ssm_prep",
    )(cc, *shift_args, p["conv_w"], p["conv_b"], p["dt_bias"], p["a_log"], p["expand"])


def _scan_kernel(*refs, mode, ni, tt_len, n_tt, fold_halves):
    if mode == "rwkv":
        q_ref, d_ref, a_ref, b_ref, n_ref, a2_ref, s0_ref, y_ref, so_ref, st = refs
    else:
        q_ref, d_ref, a_ref, b_ref, s0_ref, y_ref, so_ref, st = refs
    tt = pl.program_id(1)

    @pl.when(tt == 0)
    def _():
        st[...] = s0_ref[...]

    def row(ref, t, i):
        return ref[t, pl.ds(i, 1), :]

    def total(parts):
        x = parts[0] + parts[1]
        return x + pltpu.roll(x, LANES // 2, 1) if fold_halves else x

    def step(t, carry):
        bv = b_ref[t]
        if mode == "rwkv":
            sa_parts = [jnp.zeros_like(bv), jnp.zeros_like(bv)]
            for i in range(ni):
                sa_parts[i % 2] = sa_parts[i % 2] + st[i] * row(n_ref, t, i)
            sa = total(sa_parts)
        elif mode == "ssd":
            d = d_ref[t]
        else:
            d = d_ref[0]
        y_parts = [jnp.zeros_like(bv), jnp.zeros_like(bv)]
        for i in range(ni):
            if mode == "rwkv":
                s = st[i] * row(d_ref, t, i) + row(a_ref, t, i) * bv + row(a2_ref, t, i) * sa
            else:
                s = st[i] * d + row(a_ref, t, i) * bv
            st[i] = s
            y_parts[i % 2] = y_parts[i % 2] + s * row(q_ref, t, i)
        y_ref[t] = total(y_parts)
        return carry

    lax.fori_loop(0, tt_len, step, 0)

    @pl.when(tt == n_tt - 1)
    def _():
        so_ref[...] = st[...]


def _scan(mode, q, d, a, b, s0, n=None, a2=None, tt_len=32, fold_halves=False):
    t_len, ni, lanes = q.shape
    nj = b.shape[1]
    tt_len = min(tt_len, t_len)
    n_tt = t_len // tt_len
    groups = lanes // LANES
    i_spec = pl.BlockSpec((tt_len, ni, LANES), lambda g, t: (t, 0, g))
    j_spec = pl.BlockSpec((tt_len, nj, LANES), lambda g, t: (t, 0, g))
    s_spec = pl.BlockSpec((ni, nj, LANES), lambda g, t: (0, 0, g))
    if mode == "rwkv":
        d_spec = i_spec
    elif mode == "ssd":
        d_spec = pl.BlockSpec((tt_len, 1, LANES), lambda g, t: (t, 0, g))
    else:
        d_spec = pl.BlockSpec((1, 1, LANES), lambda g, t: (0, 0, g))
    args = [q, d, a, b]
    specs = [i_spec, d_spec, i_spec, j_spec]
    if mode == "rwkv":
        args += [n, a2]
        specs += [i_spec, i_spec]
    args.append(s0)
    specs.append(s_spec)
    return pl.pallas_call(
        functools.partial(_scan_kernel, mode=mode, ni=ni, tt_len=tt_len, n_tt=n_tt, fold_halves=fold_halves),
        grid=(groups, n_tt),
        in_specs=specs,
        out_specs=[j_spec, s_spec],
        out_shape=[jax.ShapeDtypeStruct((t_len, nj, lanes), F32), jax.ShapeDtypeStruct((ni, nj, lanes), F32)],
        scratch_shapes=[pltpu.VMEM((ni, nj, LANES), F32)],
        compiler_params=_params(("parallel", "arbitrary")),
        name="scan_" + mode,
    )(*args)


def _ret_chunk_kernel(q_ref, k_ref, v_ref, y_ref, s_ref, st):
    c = pl.program_id(1)

    @pl.when(c == 0)
    def _():
        st[...] = jnp.zeros_like(st)

    q = q_ref[...]
    k = k_ref[...]
    vb = v_ref[...].astype(BF16)
    kb = k.astype(BF16)
    row = lax.broadcasted_iota(jnp.int32, (LC, W_B), 0)
    head = lax.broadcasted_iota(jnp.int32, (LC, W_B), 1) // HEAD_DIM
    log_g = [math.log(1.0 - 2.0 ** (-5.0 - h)) for h in range(H_B)]
    lg = jnp.full((LC, W_B), log_g[0], F32)
    for h in range(1, H_B):
        lg = jnp.where(head == h, log_g[h], lg)
    rowf = row.astype(F32)
    diff = (lax.broadcasted_iota(jnp.int32, (LC, LC), 0) - lax.broadcasted_iota(jnp.int32, (LC, LC), 1))
    causal = diff >= 0
    difff = jnp.maximum(diff, 0).astype(F32)

    out = jnp.dot((q * jnp.exp(lg * (rowf + 1.0))).astype(BF16), st[...].astype(BF16), preferred_element_type=F32)
    for h in range(H_B):
        qm = jnp.where(head == h, q, 0.0).astype(BF16)
        s = lax.dot_general(qm, kb, (((1,), (1,)), ((), ())), preferred_element_type=F32)
        p = jnp.where(causal, s * jnp.exp(log_g[h] * difff), 0.0).astype(BF16)
        out = out + jnp.where(head == h, jnp.dot(p, vb, preferred_element_type=F32), 0.0)
    y_ref[...] = out

    kt = (k * jnp.exp(lg * (LC - 1.0 - rowf))).T.astype(BF16)
    kv = jnp.dot(kt, vb, preferred_element_type=F32)
    r2 = lax.broadcasted_iota(jnp.int32, (W_B, W_B), 0) // HEAD_DIM
    c2 = lax.broadcasted_iota(jnp.int32, (W_B, W_B), 1) // HEAD_DIM
    cdec = jnp.full((W_B, W_B), math.exp(log_g[0] * LC), F32)
    for h in range(1, H_B):
        cdec = jnp.where(r2 == h, math.exp(log_g[h] * LC), cdec)
    st[...] = st[...] * cdec + jnp.where(r2 == c2, kv, 0.0)
    s_ref[0] = st[...]


def _ret_chunk(q, k, cb, n_seq, t_len):
    n_c = t_len // LC
    rows = lambda col: pl.BlockSpec((LC, W_B), lambda b, c, col=col: (b * n_c + c, col))
    return pl.pallas_call(
        _ret_chunk_kernel,
        grid=(n_seq, n_c),
        in_specs=[rows(0), rows(0), rows(2)],
        out_specs=[rows(0), pl.BlockSpec((1, W_B, W_B), lambda b, c: (b, 0, 0))],
        out_shape=[jax.ShapeDtypeStruct((n_seq * t_len, W_B), F32), jax.ShapeDtypeStruct((n_seq, W_B, W_B), F32)],
        scratch_shapes=[pltpu.VMEM((W_B, W_B), F32)],
        compiler_params=_params(("parallel", "arbitrary")),
        name="ret_chunk",
    )(q, k, cb)


def _ssd_chunk_kernel(x_ref, b_ref, c_ref, da_ref, ex_ref, y_ref, h_ref, st):
    ci = pl.program_id(1)

    @pl.when(ci == 0)
    def _():
        st[...] = jnp.zeros_like(st)

    gn = N_STATE
    half = W_C // N_GROUPS
    ex = ex_ref[...]
    x = x_ref[...]
    xb = x.astype(BF16)
    ri = lax.broadcasted_iota(jnp.int32, (LC, LC), 0)
    cj = lax.broadcasted_iota(jnp.int32, (LC, LC), 1)
    causal = ri >= cj
    cum = _dot_hi(causal.astype(F32), da_ref[...])
    cum_e = _dot_hi(cum, ex)
    last_e = cum_e[LC - 1:LC, :]
    cum_t = cum.T
    head = lax.broadcasted_iota(jnp.int32, (LC, W_C), 1) // HEAD_DIM
    bg = [b_ref[:, g * gn:(g + 1) * gn].astype(BF16) for g in range(N_GROUPS)]
    cg = [c_ref[:, g * gn:(g + 1) * gn].astype(BF16) for g in range(N_GROUPS)]
    cb = [lax.dot_general(cg[g], bg[g], (((1,), (1,)), ((), ())), preferred_element_type=F32)
          for g in range(N_GROUPS)]

    hb = st[...].astype(BF16)
    ys = [lax.dot_general(cg[g], hb, (((1,), (1,)), ((), ())), preferred_element_type=F32)
          for g in range(N_GROUPS)]
    y = jnp.where(head < H_C // N_GROUPS, ys[0], ys[1]) * jnp.exp(cum_e)
    for h in range(H_C):
        seg = cum[:, h:h + 1] - cum_t[h:h + 1, :]
        p = (cb[h // (H_C // N_GROUPS)] * jnp.exp(jnp.where(causal, seg, -jnp.inf))).astype(BF16)
        y = y + jnp.where(head == h, jnp.dot(p, xb, preferred_element_type=F32), 0.0)
    y_ref[...] = y

    xt = (x * jnp.exp(last_e - cum_e)).T.astype(BF16)
    upd = [jnp.dot(xt, bg[g], preferred_element_type=F32) for g in range(N_GROUPS)]
    rowi = lax.broadcasted_iota(jnp.int32, (W_C, gn), 0)
    sel = lax.broadcasted_iota(jnp.int32, (W_C, LANES), 1) == lax.broadcasted_iota(jnp.int32, (W_C, LANES), 0) // HEAD_DIM
    tot = jnp.sum(jnp.where(sel, cum[LC - 1:LC, :], 0.0), axis=1, keepdims=True)
    st[...] = st[...] * jnp.exp(tot) + jnp.where(rowi < half, upd[0], upd[1])
    h_ref[0] = st[...]


def _ssd_chunk(xdt, bm, cm, da, ex, n_seq, t_len):
    n_c = t_len // LC
    gn = N_GROUPS * N_STATE
    rows = lambda w: pl.BlockSpec((LC, w), lambda b, c: (b * n_c + c, 0))
    return pl.pallas_call(
        _ssd_chunk_kernel,
        grid=(n_seq, n_c),
        in_specs=[rows(W_C), rows(gn), rows(gn), rows(LANES), pl.BlockSpec((LANES, W_C), lambda b, c: (0, 0))],
        out_specs=[rows(W_C), pl.BlockSpec((1, W_C, N_STATE), lambda b, c: (b, 0, 0))],
        out_shape=[jax.ShapeDtypeStruct((n_seq * t_len, W_C), F32),
                   jax.ShapeDtypeStruct((n_seq, W_C, N_STATE), F32)],
        scratch_shapes=[pltpu.VMEM((W_C, N_STATE), F32)],
        compiler_params=_params(("parallel", "arbitrary")),
        name="ssd_chunk",
    )(xdt, bm, cm, da, ex)


def _post_kernel(x_ref, ya_ref, bonus_ref, ga_ref, ob_ref, gb_ref, yc_ref, xs_ref, z_ref, lnw_ref, lnb_ref,
                 rn_ref, dsk_ref, sn_ref, bda_ref, bdb_ref, wo_ref, out_ref):
    inv_hd = 1.0 / HEAD_DIM
    y = ya_ref[...]
    bda = bda_ref[...]
    mean = _dot_hi(y, bda) * inv_hd
    yd = y - mean
    var = _dot_hi(yd * yd, bda) * inv_hd
    ya = (yd * lax.rsqrt(var + GN_EPS) * lnw_ref[...] + lnb_ref[...] + bonus_ref[...]) * ga_ref[...]
    o = ob_ref[...]
    ms = _dot_hi(o * o, bdb_ref[...]) * inv_hd
    gb = gb_ref[...]
    yb = (gb * _sigmoid(gb)) * (o * lax.rsqrt(ms + RMS_EPS) * rn_ref[...])
    z = z_ref[...]
    yc = (yc_ref[...] + dsk_ref[...] * xs_ref[...]) * (z * _sigmoid(z))
    yc = _rmsnorm(yc, sn_ref[...], GATED_NORM_EPS)
    out_ref[...] = (x_ref[...] + _dot(ya, wo_ref[0:W_A, :]) + _dot(yb, wo_ref[W_A:W_A + W_B, :])
                    + _dot(yc, wo_ref[W_A + W_B:, :]))


def _post(x, ya, bonus, ga, ob, cb, yc, xs, cc, p):
    n = x.shape[0]
    va = _full_spec((1, W_A))
    return pl.pallas_call(
        _post_kernel,
        grid=(n // TM,),
        in_specs=[_row_spec(D_MODEL), _row_spec(W_A), _row_spec(W_A), _row_spec(W_A), _row_spec(W_B),
                  _row_spec(W_B, 3), _row_spec(W_C), _row_spec(W_C), _row_spec(W_C, 0), va, va,
                  _full_spec((1, W_B)), va, va, _full_spec((W_A, W_A)), _full_spec((W_B, W_B)),
                  _full_spec((D_MODEL, D_MODEL))],
        out_specs=_row_spec(D_MODEL),
        out_shape=jax.ShapeDtypeStruct((n, D_MODEL), F32),
        compiler_params=_params(("parallel",)),
        name="post_outproj",
    )(x, ya, bonus, ga, ob, cb, yc, xs, cc, p["ln_w"], p["ln_b"], p["ret_norm"], p["d_skip"], p["ssm_norm"],
      p["bd_a"], p["bd_b"], p["w_out"])


def _ffn_kernel(*refs, gated, final_norm, n_e):
    refs = list(refs)
    x_ref, nf_ref = refs[:2]
    pos = 2
    if gated:
        router_ref = refs[pos]
        pos += 1
    wg_ref, wu_ref, wd_ref = refs[pos:pos + 3]
    pos += 3
    if final_norm:
        nfin_ref = refs[pos]
        pos += 1
    out_ref = refs[pos]
    hb_s, acc_s = refs[pos + 1:pos + 3]
    if gated:
        gate_s = refs[pos + 3]
    e = pl.program_id(1)

    @pl.when(e == 0)
    def _():
        h = _rmsnorm(x_ref[...], nf_ref[...], RMS_EPS)
        hb_s[...] = h.astype(BF16)
        acc_s[...] = jnp.zeros_like(acc_s)
        if gated:
            lane = lax.broadcasted_iota(jnp.int32, (TM, LANES), 1)
            logits = jnp.where(lane < N_EXPERTS, _dot_hi(h, router_ref[...]), -jnp.inf)
            p = jnp.exp(logits - jnp.max(logits, axis=-1, keepdims=True))
            p = p / jnp.sum(p, axis=-1, keepdims=True)
            p1 = jnp.max(p, axis=-1, keepdims=True)
            i1 = jnp.min(jnp.where(p == p1, lane, LANES), axis=-1, keepdims=True)
            rest = jnp.where(lane == i1, -1.0, p)
            p2 = jnp.max(rest, axis=-1, keepdims=True)
            i2 = jnp.min(jnp.where(rest == p2, lane, LANES), axis=-1, keepdims=True)
            gate_s[...] = jnp.where(lane == i1, p1, jnp.where(lane == i2, p2, 0.0)) / (p1 + p2)

    hb = hb_s[...]
    g = jnp.dot(hb, wg_ref[0], preferred_element_type=F32)
    u = jnp.dot(hb, wu_ref[0], preferred_element_type=F32)
    o = _dot(g * _sigmoid(g) * u, wd_ref[0])
    if gated:
        lane = lax.broadcasted_iota(jnp.int32, (TM, LANES), 1)
        o = o * jnp.sum(jnp.where(lane == e, gate_s[...], 0.0), axis=-1, keepdims=True)
    acc_s[...] += o

    @pl.when(e == n_e - 1)
    def _():
        y = x_ref[...] + acc_s[...]
        if final_norm:
            y = _rmsnorm(y, nfin_ref[...], RMS_EPS)
        out_ref[...] = y


def _ffn(x, nf, wg, wu, wd, router=None, nfin=None):
    n = x.shape[0]
    n_e, _, dff = wg.shape
    gated = router is not None
    final_norm = nfin is not None
    vec = pl.BlockSpec((1, D_MODEL), lambda i, e: (0, 0))
    args = [x, nf]
    specs = [pl.BlockSpec((TM, D_MODEL), lambda i, e: (i, 0)), vec]
    if gated:
        args.append(router)
        specs.append(pl.BlockSpec((D_MODEL, LANES), lambda i, e: (0, 0)))
    args += [wg, wu, wd]
    specs += [pl.BlockSpec((1, D_MODEL, dff), lambda i, e: (e, 0, 0)),
              pl.BlockSpec((1, D_MODEL, dff), lambda i, e: (e, 0, 0)),
              pl.BlockSpec((1, dff, D_MODEL), lambda i, e: (e, 0, 0))]
    if final_norm:
        args.append(nfin)
        specs.append(vec)
    scratch = [pltpu.VMEM((TM, D_MODEL), BF16), pltpu.VMEM((TM, D_MODEL), F32)]
    if gated:
        scratch.append(pltpu.VMEM((TM, LANES), F32))
    return pl.pallas_call(
        functools.partial(_ffn_kernel, gated=gated, final_norm=final_norm, n_e=n_e),
        grid=(n // TM, n_e),
        in_specs=specs,
        out_specs=pl.BlockSpec((TM, D_MODEL), lambda i, e: (i, 0)),
        out_shape=jax.ShapeDtypeStruct((n, D_MODEL), F32),
        scratch_shapes=scratch,
        compiler_params=_params(("parallel", "arbitrary")),
        name="moe" if gated else "ffn",
    )(*args)


def _heads(x, b, h):
    n, w = x.shape
    return x.reshape(b, n // b, h, w // h)


def _lanes(x):
    b, t, h, f = x.shape
    return jnp.transpose(x, (1, 3, 0, 2)).reshape(t, f, b * h)


def _from_lanes(y, b, h):
    t, j, _ = y.shape
    return jnp.transpose(y.reshape(t, j, b, h), (2, 0, 3, 1)).reshape(b * t, h * j)


def _state_to_lanes(s, j_first):
    b, h = s.shape[:2]
    y = jnp.transpose(s, (3, 2, 0, 1) if j_first else (2, 3, 0, 1))
    return y.reshape(y.shape[0], y.shape[1], b * h)


def _state_from_lanes(s, b, h, j_first):
    ni, nj, _ = s.shape
    return jnp.transpose(s.reshape(ni, nj, b, h), (2, 3, 1, 0) if j_first else (2, 3, 0, 1))


def _pad_heads(x, axis):
    pad = [(0, 0)] * x.ndim
    pad[axis] = (0, H_PAD - x.shape[axis])
    return jnp.pad(x, pad)


def _rwkv_long(feats, b):
    r, dec, k2, v, nkk, beta = feats
    t = r.shape[0] // b
    kl = HEAD_DIM // K_SPLIT
    assert K_SPLIT * b * H_PAD == LANES

    def key_rows(x):
        y = _pad_heads(x.reshape(b, t, H_A, K_SPLIT, kl), 2)
        return jnp.transpose(y, (1, 4, 3, 0, 2)).reshape(t, kl, LANES)

    vv = jnp.transpose(_pad_heads(v.reshape(b, t, H_A, HEAD_DIM), 2), (1, 3, 0, 2))
    vv = jnp.broadcast_to(vv[:, :, None], (t, HEAD_DIM, K_SPLIT, b, H_PAD)).reshape(t, HEAD_DIM, LANES)
    y, s = _scan("rwkv", key_rows(r), key_rows(dec), key_rows(k2), vv, jnp.zeros((kl, HEAD_DIM, LANES), F32),
                 n=key_rows(nkk), a2=key_rows(beta), tt_len=64, fold_halves=True)
    y = y[:, :, :b * H_PAD].reshape(t, HEAD_DIM, b, H_PAD)[..., :H_A]
    ya = jnp.transpose(y, (2, 0, 3, 1)).reshape(b * t, W_A)
    s = s.reshape(kl, HEAD_DIM, K_SPLIT, b, H_PAD)[..., :H_A]
    return ya, jnp.transpose(s, (3, 4, 1, 2, 0)).reshape(b, H_A, HEAD_DIM, HEAD_DIM)


def _rwkv_short(feats, b, s0):
    r, dec, k2, v, nkk, beta = [_lanes(_heads(f, b, H_A)) for f in feats]
    y, s = _scan("rwkv", r, dec, k2, v, _state_to_lanes(s0, True), n=nkk, a2=beta)
    return _from_lanes(y, b, H_A), _state_from_lanes(s, b, H_A, True)


def _ret_short(qh, kh, vb, b, s0):
    gam = np.tile(1.0 - 2.0 ** (-5.0 - np.arange(H_B, dtype=np.float64)), b).astype(np.float32)
    y, s = _scan("ret", _lanes(_heads(qh, b, H_B)), jnp.asarray(gam.reshape(1, 1, -1)), _lanes(_heads(kh, b, H_B)),
                 _lanes(_heads(vb, b, H_B)), _state_to_lanes(s0, False))
    return _from_lanes(y, b, H_B), _state_from_lanes(s, b, H_B, False)


def _ssd_short(xdt, bm, cm, dssm, b, s0):
    rep = H_C // N_GROUPS
    bh = jnp.repeat(_heads(bm, b, N_GROUPS), rep, axis=2)
    ch = jnp.repeat(_heads(cm, b, N_GROUPS), rep, axis=2)
    dec = _heads(dssm, b, 1)[:, :, 0, :H_C, None]
    y, s = _scan("ssd", _lanes(ch), _lanes(dec), _lanes(bh), _lanes(_heads(xdt, b, H_C)), _state_to_lanes(s0, True))
    return _from_lanes(y, b, H_C), _state_from_lanes(s, b, H_C, True)


def _block_diag_ones(width):
    idx = np.arange(width) // HEAD_DIM
    return jnp.asarray((idx[:, None] == idx[None, :]).astype(np.float32))


def _shifted(x, prev_rows, shift):
    b, t, c = x.shape
    p = prev_rows.shape[1]
    full = jnp.concatenate([prev_rows, x], axis=1)
    return full[:, p - shift:p - shift + t].reshape(b * t, c)


def _layer(x, b, p, st, rope, table_block):
    n = x.shape[0]
    t = n // b
    fresh = st is None
    tiles_per_seq = t // TM if fresh else 0
    ca, cb, cc = _in_proj(x, p["norm_mix"], p["w_in"])
    ca3 = ca.reshape(b, t, COLS_A)
    xbc_tail = cc.reshape(b, t, COLS_C_PAD)[:, -(CONV_W - 1):, W_C:W_C + CONV_DIM]
    if fresh:
        prev, shifted = None, None
        conv_new = xbc_tail
    else:
        prev = _shifted(ca3, st["shift"][:, None, :], 1)
        xbc = cc[:, W_C:W_C + CONV_DIM].reshape(b, t, CONV_DIM)
        shifted = [_shifted(xbc, st["conv"], j) for j in range(1, CONV_W)]
        conv_new = jnp.concatenate([st["conv"], xbc_tail], axis=1)[:, -(CONV_W - 1):]

    r, dec, k2, v, nkk, beta, ga, bonus = _rwkv_prep(ca, prev, p, tiles_per_seq)
    qh, kh = _ret_prep(cb, rope[0], rope[1], table_block)
    xdt, bm, cm, dssm, xs, da = _ssm_prep(cc, shifted, p, tiles_per_seq)

    feats = (r, dec, k2, v, nkk, beta)
    if fresh:
        ya, s_rwkv = _rwkv_long(feats, b)
        yb, s_full = _ret_chunk(qh, kh, cb, b, t)
        s_ret = jnp.stack([s_full[:, h * HEAD_DIM:(h + 1) * HEAD_DIM, h * HEAD_DIM:(h + 1) * HEAD_DIM]
                           for h in range(H_B)], axis=1)
        yc, s_ssm = _ssd_chunk(xdt, bm, cm, da, p["expand"], b, t)
        s_ssm = s_ssm.reshape(b, H_C, HEAD_DIM, N_STATE)
    else:
        ya, s_rwkv = _rwkv_short(feats, b, st["rwkv"])
        yb, s_ret = _ret_short(qh, kh, cb[:, 2 * W_B:3 * W_B], b, st["ret"])
        yc, s_ssm = _ssd_short(xdt, bm, cm, dssm, b, st["ssm"])

    x2 = _post(x, ya, bonus, ga, yb, cb, yc, xs, cc, p)
    x = _ffn(x2, p["norm_ffn"], p["wg"], p["wu"], p["wd"], router=p["router"], nfin=p["norm_final"])
    return x, (s_rwkv, ca3[:, -1], s_ret, s_ssm, conv_new)


def kernel(x_prompt, x_sample, state_rwkv, state_shift, state_ret, state_ssm, state_conv, norm_mix, w_in, rwkv_mu, rwkv_w0, rwkv_w_up, rwkv_a0, rwkv_a_up, rwkv_g_up, rwkv_k_k, rwkv_k_a, rwkv_r_k, rwkv_ln_w, rwkv_ln_b, ret_norm, ssm_conv_w, ssm_conv_b, ssm_dt_bias, ssm_a_log, ssm_d, ssm_norm, w_out, norm_ffn, ffn_w_gate, ffn_w_up, ffn_w_down, moe_router, moe_w_gate, moe_w_up, moe_w_down, norm_final):
    bp, tp, _ = x_prompt.shape
    bs, ts, _ = x_sample.shape
    depth = w_in.shape[0]
    assert tp % TM == 0 and tp % LC == 0 and (bs * ts) % TM == 0 and TM % ts == 0

    rope = _rope_tables(tp, ts)
    bd_a, bd_b = _block_diag_ones(W_A), _block_diag_ones(W_B)
    expand = np.zeros((LANES, W_C), np.float32)
    expand[np.arange(W_C) // HEAD_DIM, np.arange(W_C)] = 1.0
    expand = jnp.asarray(expand)
    row = lambda v: v.reshape(1, -1)
    pad_l = lambda v: jnp.pad(v, (0, LANES - v.shape[0])).reshape(1, LANES)

    xp = x_prompt.reshape(bp * tp, D_MODEL)
    xs = x_sample.reshape(bs * ts, D_MODEL)
    new_p, new_s = [], []
    for i in range(depth):
        j = i // 2
        p = dict(
            norm_mix=row(norm_mix[i]),
            w_in=jnp.pad(w_in[i], ((0, 0), (0, COLS_C_PAD - COLS_C))).astype(BF16),
            mu=row(rwkv_mu[i]), w0=row(rwkv_w0[i]), a0=row(rwkv_a0[i]), k_k=row(rwkv_k_k[i]),
            k_a=row(rwkv_k_a[i]), r_k=row(rwkv_r_k[i]),
            w_up=jnp.pad(rwkv_w_up[i], ((0, AAA_LORA), (0, 0))).astype(BF16),
            a_up=jnp.pad(rwkv_a_up[i], ((DECAY_LORA, 0), (0, 0))).astype(BF16),
            g_up=rwkv_g_up[i].astype(BF16), bd_a=bd_a, bd_b=bd_b,
            conv_w=ssm_conv_w[i], conv_b=row(ssm_conv_b[i]), dt_bias=pad_l(ssm_dt_bias[i]),
            a_log=pad_l(ssm_a_log[i]), expand=expand,
            ln_w=row(rwkv_ln_w[i]), ln_b=row(rwkv_ln_b[i]), ret_norm=row(ret_norm[i]),
            d_skip=row(jnp.repeat(ssm_d[i], HEAD_DIM)), ssm_norm=row(ssm_norm[i]), w_out=w_out[i].astype(BF16),
            norm_ffn=row(norm_ffn[i]), norm_final=row(norm_final) if i == depth - 1 else None)
        if i % 2 == 0:
            dff = ffn_w_gate.shape[-1] // 2
            p.update(router=None,
                     wg=ffn_w_gate[j].reshape(D_MODEL, 2, dff).transpose(1, 0, 2).astype(BF16),
                     wu=ffn_w_up[j].reshape(D_MODEL, 2, dff).transpose(1, 0, 2).astype(BF16),
                     wd=ffn_w_down[j].reshape(2, dff, D_MODEL).astype(BF16))
        else:
            p.update(router=jnp.pad(moe_router[j], ((0, 0), (0, LANES - N_EXPERTS))),
                     wg=moe_w_gate[j].astype(BF16), wu=moe_w_up[j].astype(BF16), wd=moe_w_down[j].astype(BF16))

        xp, st_p = _layer(xp, bp, p, None, rope, lambda t: t % (tp // TM))
        st = dict(rwkv=state_rwkv[i], shift=state_shift[i], ret=state_ret[i], ssm=state_ssm[i],
                  conv=state_conv[i])
        xs, st_s = _layer(xs, bs, p, st, rope, lambda t: tp // TM)
        new_p.append(st_p)
        new_s.append(st_s)

    stack = lambda sts: tuple(jnp.stack(s) for s in zip(*sts))
    return (xp.reshape(bp, tp, D_MODEL), xs.reshape(bs, ts, D_MODEL)) + stack(new_p) + stack(new_s)
```

```python
import functools
import math

import numpy as np
import jax
import jax.numpy as jnp
from jax import lax
from jax.experimental import pallas as pl
from jax.experimental.pallas import tpu as pltpu

F32 = jnp.float32
BF16 = jnp.bfloat16
HIGHEST = lax.Precision.HIGHEST

LANES = 128
SUBLANES = 8
VMEM_LIMIT = 56 * 1024 * 1024

D_MODEL = 1024
HEAD_DIM = 64
H_A, H_B, H_C = 6, 4, 6
W_A, W_B, W_C = H_A * HEAD_DIM, H_B * HEAD_DIM, H_C * HEAD_DIM
DECAY_LORA, AAA_LORA, GATE_LORA = 64, 64, 128
COLS_A = 3 * W_A + DECAY_LORA + AAA_LORA + GATE_LORA
COLS_B = 4 * W_B
N_STATE, N_GROUPS, CONV_W = 128, 2, 4
CONV_DIM = W_C + 2 * N_GROUPS * N_STATE
COLS_C = W_C + CONV_DIM + H_C
COLS_C_PAD = 1408
ROPE_BASE = 10000.0
RMS_EPS = 1e-6
GN_EPS = 64e-5
GATED_NORM_EPS = 1e-5
N_EXPERTS = 8
PAST_LEN = 16384

TM = 512
LC = 256
H_PAD = 8
K_SPLIT = 2


def _dot(a, b):
    return jnp.dot(a.astype(BF16), b.astype(BF16), preferred_element_type=F32)


def _dot_hi(a, b):
    return jnp.dot(a, b, precision=HIGHEST, preferred_element_type=F32)


def _sigmoid(x):
    return 1.0 / (1.0 + jnp.exp(-x))


def _softplus(x):
    return jnp.maximum(x, 0.0) + jnp.log1p(jnp.exp(-jnp.abs(x)))


def _rmsnorm(x, g, eps):
    return x * lax.rsqrt(jnp.mean(x * x, axis=-1, keepdims=True) + eps) * g


def _params(sem):
    return pltpu.CompilerParams(dimension_semantics=sem, vmem_limit_bytes=VMEM_LIMIT)


def _row_spec(width, col=0):
    return pl.BlockSpec((TM, width), lambda i, c=col: (i, c))


def _before_spec(width):
    return pl.BlockSpec((SUBLANES, width), lambda i: (jnp.maximum(i * (TM // SUBLANES) - 1, 0), 0))


def _full_spec(shape):
    nd = len(shape)
    return pl.BlockSpec(shape, lambda i, n=nd: (0,) * n)


def _shift_rows(x, before, j, first):
    rolled = pltpu.roll(x, j, 0)
    prev = jnp.where(first, 0.0, pltpu.roll(before, j, 0))
    row = lax.broadcasted_iota(jnp.int32, prev.shape, 0)
    top = jnp.where(row < j, prev, rolled[0:SUBLANES])
    return jnp.concatenate([top, rolled[SUBLANES:]], axis=0)


def _in_proj_kernel(x_ref, g_ref, w_ref, oa_ref, ob_ref, oc_ref):
    h = _rmsnorm(x_ref[...], g_ref[...], RMS_EPS).astype(BF16)
    oa_ref[...] = jnp.dot(h, w_ref[:, 0:COLS_A], preferred_element_type=F32)
    ob_ref[...] = jnp.dot(h, w_ref[:, COLS_A:COLS_A + COLS_B], preferred_element_type=F32)
    oc_ref[...] = jnp.dot(h, w_ref[:, COLS_A + COLS_B:], preferred_element_type=F32)


def _in_proj(x, g, w):
    n = x.shape[0]
    wtot = w.shape[1]
    return pl.pallas_call(
        _in_proj_kernel,
        grid=(n // TM,),
        in_specs=[_row_spec(D_MODEL), _full_spec((1, D_MODEL)), _full_spec((D_MODEL, wtot))],
        out_specs=[_row_spec(COLS_A), _row_spec(COLS_B), _row_spec(COLS_C_PAD)],
        out_shape=[jax.ShapeDtypeStruct((n, COLS_A), F32), jax.ShapeDtypeStruct((n, COLS_B), F32),
                   jax.ShapeDtypeStruct((n, COLS_C_PAD), F32)],
        compiler_params=_params(("parallel",)),
        name="in_proj",
    )(x, g, w)


def _rwkv_prep_kernel(c_ref, p_ref, mu_ref, w0_ref, a0_ref, kk_ref, ka_ref, rk_ref, wup_ref, aup_ref,
                      gup_ref, bd_ref, r_o, d_o, k_o, v_o, n_o, b_o, g_o, bonus_o, *, tiles_per_seq):
    c = c_ref[...]
    if tiles_per_seq:
        prev = _shift_rows(c, p_ref[...], 1, pl.program_id(0) % tiles_per_seq == 0)
    else:
        prev = p_ref[...]
    xm = c + (prev - c) * mu_ref[...]
    r = xm[:, 0:W_A]
    k = xm[:, W_A:2 * W_A]
    v = xm[:, 2 * W_A:3 * W_A]
    lora = xm[:, 3 * W_A:3 * W_A + DECAY_LORA + AAA_LORA]
    gd = xm[:, 3 * W_A + DECAY_LORA + AAA_LORA:]
    w = w0_ref[...] + _dot(jnp.tanh(lora), wup_ref[...])
    decay = jnp.exp(-math.exp(-0.5) * _sigmoid(w))
    a = _sigmoid(a0_ref[...] + _dot(lora, aup_ref[...]))
    g = _dot(_sigmoid(gd), gup_ref[...])
    bd = bd_ref[...]
    kk = k * kk_ref[...]
    kk = kk / jnp.maximum(jnp.sqrt(_dot_hi(kk * kk, bd)), 1e-12)
    k2 = k * (1.0 + (a - 1.0) * ka_ref[...])
    for o_ref, val in ((r_o, r), (d_o, decay), (k_o, k2), (v_o, v), (n_o, -kk), (b_o, kk * a)):
        o_ref[:, 0:W_A] = val
        if o_ref.shape[1] > W_A:
            o_ref[:, W_A:] = jnp.zeros((TM, o_ref.shape[1] - W_A), F32)
    g_o[...] = g
    bonus_o[...] = _dot_hi(r * k2 * rk_ref[...], bd) * v


def _rwkv_prep(ca, prev, p, tiles_per_seq):
    n = ca.shape[0]
    vec = _full_spec((1, W_A))
    lora_spec = _full_spec((DECAY_LORA + AAA_LORA, W_A))
    p_spec = _before_spec(COLS_A) if tiles_per_seq else _row_spec(COLS_A)
    w_scan = H_PAD * HEAD_DIM if tiles_per_seq else W_A
    widths = [w_scan] * 6 + [W_A] * 2
    return pl.pallas_call(
        functools.partial(_rwkv_prep_kernel, tiles_per_seq=tiles_per_seq),
        grid=(n // TM,),
        in_specs=[_row_spec(COLS_A), p_spec, _full_spec((1, COLS_A)), vec, vec, vec, vec, vec,
                  lora_spec, lora_spec, _full_spec((GATE_LORA, W_A)), _full_spec((W_A, W_A))],
        out_specs=[_row_spec(w) for w in widths],
        out_shape=[jax.ShapeDtypeStruct((n, w), F32) for w in widths],
        compiler_params=_params(("parallel",)),
        name="rwkv_prep",
    )(ca, ca if tiles_per_seq else prev, p["mu"], p["w0"], p["a0"], p["k_k"], p["k_a"], p["r_k"],
      p["w_up"], p["a_up"], p["g_up"], p["bd_a"])


def _rope_kernel(ang_ref, cos_o, sin_o):
    ang = ang_ref[...]
    lane = lax.broadcasted_iota(jnp.int32, ang.shape, 1)
    cos_o[...] = jnp.cos(ang)
    sin_o[...] = jnp.where((lane % HEAD_DIM) < (HEAD_DIM // 2), -jnp.sin(ang), jnp.sin(ang))


def _rope_tables(t_prompt, t_sample):
    theta = 1.0 / (ROPE_BASE ** jnp.linspace(0.0, 1.0, HEAD_DIM // 2, dtype=F32))
    pos = jnp.concatenate([jnp.arange(t_prompt, dtype=F32), PAST_LEN + (jnp.arange(TM) % t_sample).astype(F32)])
    ang = jnp.tile(pos[:, None] * theta[None, :], (1, W_B // (HEAD_DIM // 2)))
    n = ang.shape[0]
    return pl.pallas_call(
        _rope_kernel,
        grid=(n // TM,),
        in_specs=[_row_spec(W_B)],
        out_specs=[_row_spec(W_B)] * 2,
        out_shape=[jax.ShapeDtypeStruct((n, W_B), F32)] * 2,
        compiler_params=_params(("parallel",)),
        name="rope_tables",
    )(ang)


def _ret_prep_kernel(q_ref, k_ref, cos_ref, sin_ref, q_o, k_o):
    cos = cos_ref[...]
    sin = sin_ref[...]
    lane = lax.broadcasted_iota(jnp.int32, (TM, W_B), 1)
    first_half = (lane % HEAD_DIM) < (HEAD_DIM // 2)

    def rope(x):
        partner = jnp.where(first_half, pltpu.roll(x, W_B - HEAD_DIM // 2, 1), pltpu.roll(x, HEAD_DIM // 2, 1))
        return x * cos + partner * sin

    q_o[...] = rope(q_ref[...])
    k_o[...] = rope(k_ref[...]) * (HEAD_DIM ** -0.5)


def _ret_prep(cb, cos, sin, table_block):
    n = cb.shape[0]
    tab = pl.BlockSpec((TM, W_B), lambda i: (table_block(i), 0))
    return pl.pallas_call(
        _ret_prep_kernel,
        grid=(n // TM,),
        in_specs=[_row_spec(W_B, 0), _row_spec(W_B, 1), tab, tab],
        out_specs=[_row_spec(W_B)] * 2,
        out_shape=[jax.ShapeDtypeStruct((n, W_B), F32)] * 2,
        compiler_params=_params(("parallel",)),
        name="ret_prep",
    )(cb, cb, cos, sin)


def _ssm_prep_kernel(*refs, tiles_per_seq):
    n_shift_refs = 1 if tiles_per_seq else CONV_W - 1
    cc_ref = refs[0]
    shift_refs = refs[1:1 + n_shift_refs]
    cw_ref, cb_ref, dtb_ref, alog_ref, ex_ref, xdt_o, b_o, c_o, dec_o, xs_o, da_o = refs[1 + n_shift_refs:]
    cc = cc_ref[...]
    x0 = cc[:, W_C:W_C + CONV_DIM]
    if tiles_per_seq:
        before = shift_refs[0][:, W_C:W_C + CONV_DIM]
        first = pl.program_id(0) % tiles_per_seq == 0
        xs_prev = [_shift_rows(x0, before, j, first) for j in range(1, CONV_W)]
    else:
        xs_prev = [r[...] for r in shift_refs]
    cw = cw_ref[...]
    conv = x0 * cw[CONV_W - 1:CONV_W, :] + cb_ref[...]
    for j in range(1, CONV_W):
        conv = conv + xs_prev[j - 1] * cw[CONV_W - 1 - j:CONV_W - j, :]
    act = conv * _sigmoid(conv)
    xs = act[:, 0:W_C]
    dt = _softplus(cc[:, W_C + CONV_DIM:] + dtb_ref[...])
    da = dt * (-jnp.exp(alog_ref[...]))
    da_o[...] = da
    dec_o[...] = jnp.exp(da)
    xdt_o[...] = xs * _dot_hi(dt, ex_ref[...])
    b_o[...] = act[:, W_C:W_C + N_GROUPS * N_STATE]
    c_o[...] = act[:, W_C + N_GROUPS * N_STATE:]
    xs_o[...] = xs


def _ssm_prep(cc, shifted, p, tiles_per_seq):
    n = cc.shape[0]
    gn = N_GROUPS * N_STATE
    if tiles_per_seq:
        shift_args, shift_specs = [cc], [_before_spec(COLS_C_PAD)]
    else:
        shift_args, shift_specs = list(shifted), [_row_spec(CONV_DIM)] * (CONV_W - 1)
    return pl.pallas_call(
        functools.partial(_ssm_prep_kernel, tiles_per_seq=tiles_per_seq),
        grid=(n // TM,),
        in_specs=[_row_spec(COLS_C_PAD)] + shift_specs + [
            _full_spec((CONV_W, CONV_DIM)), _full_spec((1, CONV_DIM)), _full_spec((1, LANES)),
            _full_spec((1, LANES)), _full_spec((LANES, W_C))],
        out_specs=[_row_spec(W_C), _row_spec(gn), _row_spec(gn), _row_spec(LANES), _row_spec(W_C),
                   _row_spec(LANES)],
        out_shape=[jax.ShapeDtypeStruct((n, W_C), F32), jax.ShapeDtypeStruct((n, gn), F32),
                   jax.ShapeDtypeStruct((n, gn), F32), jax.ShapeDtypeStruct((n, LANES), F32),
                   jax.ShapeDtypeStruct((n, W_C), F32), jax.ShapeDtypeStruct((n, LANES), F32)],
        compiler_params=_params(("parallel",)),
        name="ssm_prep",
    )(cc, *shift_args, p["conv_w"], p["conv_b"], p["dt_bias"], p["a_log"], p["expand"])


def _scan_kernel(*refs, mode, ni, tt_len, n_tt, fold_halves):
    if mode == "rwkv":
        q_ref, d_ref, a_ref, b_ref, n_ref, a2_ref, s0_ref, y_ref, so_ref, st = refs
    else:
        q_ref, d_ref, a_ref, b_ref, s0_ref, y_ref, so_ref, st = refs
    tt = pl.program_id(1)

    @pl.when(tt == 0)
    def _():
        st[...] = s0_ref[...]

    def row(ref, t, i):
        return ref[t, pl.ds(i, 1), :]

    def total(parts):
        x = parts[0] + parts[1]
        return x + pltpu.roll(x, LANES // 2, 1) if fold_halves else x

    nj = b_ref.shape[1]
    j_parts = [slice(0, nj // 2), slice(nj // 2, nj)] if fold_halves else [slice(0, nj)]

    def step(t, carry):
        bvs = [b_ref[t, js, :] for js in j_parts]
        if mode == "rwkv":
            sas = []
            for js, bv in zip(j_parts, bvs):
                sa_parts = [jnp.zeros_like(bv), jnp.zeros_like(bv)]
                for i in range(ni):
                    sa_parts[i % 2] = sa_parts[i % 2] + st[i, js, :] * row(n_ref, t, i)
                sas.append(total(sa_parts))
        elif mode == "ssd":
            d = d_ref[t]
        else:
            d = d_ref[0]
        for p, (js, bv) in enumerate(zip(j_parts, bvs)):
            y_parts = [jnp.zeros_like(bv), jnp.zeros_like(bv)]
            for i in range(ni):
                if mode == "rwkv":
                    s = st[i, js, :] * row(d_ref, t, i) + row(a_ref, t, i) * bv + row(a2_ref, t, i) * sas[p]
                else:
                    s = st[i, js, :] * d + row(a_ref, t, i) * bv
                st[i, js, :] = s
                y_parts[i % 2] = y_parts[i % 2] + s * row(q_ref, t, i)
            y_ref[t, js, :] = total(y_parts)
        return carry

    lax.fori_loop(0, tt_len, step, 0)

    @pl.when(tt == n_tt - 1)
    def _():
        so_ref[...] = st[...]


def _scan(mode, q, d, a, b, s0, n=None, a2=None, tt_len=32, fold_halves=False):
    t_len, ni, lanes = q.shape
    nj = b.shape[1]
    tt_len = min(tt_len, t_len)
    n_tt = t_len // tt_len
    groups = lanes // LANES
    i_spec = pl.BlockSpec((tt_len, ni, LANES), lambda g, t: (t, 0, g))
    j_spec = pl.BlockSpec((tt_len, nj, LANES), lambda g, t: (t, 0, g))
    s_spec = pl.BlockSpec((ni, nj, LANES), lambda g, t: (0, 0, g))
    if mode == "rwkv":
        d_spec = i_spec
    elif mode == "ssd":
        d_spec = pl.BlockSpec((tt_len, 1, LANES), lambda g, t: (t, 0, g))
    else:
        d_spec = pl.BlockSpec((1, 1, LANES), lambda g, t: (0, 0, g))
    args = [q, d, a, b]
    specs = [i_spec, d_spec, i_spec, j_spec]
    if mode == "rwkv":
        args += [n, a2]
        specs += [i_spec, i_spec]
    args.append(s0)
    specs.append(s_spec)
    return pl.pallas_call(
        functools.partial(_scan_kernel, mode=mode, ni=ni, tt_len=tt_len, n_tt=n_tt, fold_halves=fold_halves),
        grid=(groups, n_tt),
        in_specs=specs,
        out_specs=[j_spec, s_spec],
        out_shape=[jax.ShapeDtypeStruct((t_len, nj, lanes), F32), jax.ShapeDtypeStruct((ni, nj, lanes), F32)],
        scratch_shapes=[pltpu.VMEM((ni, nj, LANES), F32)],
        compiler_params=_params(("parallel", "arbitrary")),
        name="scan_" + mode,
    )(*args)


def _ret_chunk_kernel(q_ref, k_ref, v_ref, y_ref, s_ref, st):
    c = pl.program_id(1)

    @pl.when(c == 0)
    def _():
        st[...] = jnp.zeros_like(st)

    q = q_ref[...]
    k = k_ref[...]
    vb = v_ref[...].astype(BF16)
    kb = k.astype(BF16)
    row = lax.broadcasted_iota(jnp.int32, (LC, W_B), 0)
    head = lax.broadcasted_iota(jnp.int32, (LC, W_B), 1) // HEAD_DIM
    log_g = [math.log(1.0 - 2.0 ** (-5.0 - h)) for h in range(H_B)]
    lg = jnp.full((LC, W_B), log_g[0], F32)
    for h in range(1, H_B):
        lg = jnp.where(head == h, log_g[h], lg)
    rowf = row.astype(F32)
    diff = (lax.broadcasted_iota(jnp.int32, (LC, LC), 0) - lax.broadcasted_iota(jnp.int32, (LC, LC), 1))
    causal = diff >= 0
    difff = jnp.maximum(diff, 0).astype(F32)

    out = jnp.dot((q * jnp.exp(lg * (rowf + 1.0))).astype(BF16), st[...].astype(BF16), preferred_element_type=F32)
    for h in range(H_B):
        qm = jnp.where(head == h, q, 0.0).astype(BF16)
        s = lax.dot_general(qm, kb, (((1,), (1,)), ((), ())), preferred_element_type=F32)
        p = jnp.where(causal, s * jnp.exp(log_g[h] * difff), 0.0).astype(BF16)
        out = out + jnp.where(head == h, jnp.dot(p, vb, preferred_element_type=F32), 0.0)
    y_ref[...] = out

    kt = (k * jnp.exp(lg * (LC - 1.0 - rowf))).T.astype(BF16)
    kv = jnp.dot(kt, vb, preferred_element_type=F32)
    r2 = lax.broadcasted_iota(jnp.int32, (W_B, W_B), 0) // HEAD_DIM
    c2 = lax.broadcasted_iota(jnp.int32, (W_B, W_B), 1) // HEAD_DIM
    cdec = jnp.full((W_B, W_B), math.exp(log_g[0] * LC), F32)
    for h in range(1, H_B):
        cdec = jnp.where(r2 == h, math.exp(log_g[h] * LC), cdec)
    st[...] = st[...] * cdec + jnp.where(r2 == c2, kv, 0.0)
    s_ref[0] = st[...]


def _ret_chunk(q, k, cb, n_seq, t_len):
    n_c = t_len // LC
    rows = lambda col: pl.BlockSpec((LC, W_B), lambda b, c, col=col: (b * n_c + c, col))
    return pl.pallas_call(
        _ret_chunk_kernel,
        grid=(n_seq, n_c),
        in_specs=[rows(0), rows(0), rows(2)],
        out_specs=[rows(0), pl.BlockSpec((1, W_B, W_B), lambda b, c: (b, 0, 0))],
        out_shape=[jax.ShapeDtypeStruct((n_seq * t_len, W_B), F32), jax.ShapeDtypeStruct((n_seq, W_B, W_B), F32)],
        scratch_shapes=[pltpu.VMEM((W_B, W_B), F32)],
        compiler_params=_params(("parallel", "arbitrary")),
        name="ret_chunk",
    )(q, k, cb)


def _ssd_chunk_kernel(x_ref, b_ref, c_ref, da_ref, ex_ref, y_ref, h_ref, st):
    ci = pl.program_id(1)

    @pl.when(ci == 0)
    def _():
        st[...] = jnp.zeros_like(st)

    gn = N_STATE
    half = W_C // N_GROUPS
    ex = ex_ref[...]
    x = x_ref[...]
    xb = x.astype(BF16)
    ri = lax.broadcasted_iota(jnp.int32, (LC, LC), 0)
    cj = lax.broadcasted_iota(jnp.int32, (LC, LC), 1)
    causal = ri >= cj
    cum = _dot_hi(causal.astype(F32), da_ref[...])
    cum_e = _dot_hi(cum, ex)
    last_e = cum_e[LC - 1:LC, :]
    cum_t = cum.T
    head = lax.broadcasted_iota(jnp.int32, (LC, W_C), 1) // HEAD_DIM
    bg = [b_ref[:, g * gn:(g + 1) * gn].astype(BF16) for g in range(N_GROUPS)]
    cg = [c_ref[:, g * gn:(g + 1) * gn].astype(BF16) for g in range(N_GROUPS)]
    cb = [lax.dot_general(cg[g], bg[g], (((1,), (1,)), ((), ())), preferred_element_type=F32)
          for g in range(N_GROUPS)]

    hb = st[...].astype(BF16)
    ys = [lax.dot_general(cg[g], hb, (((1,), (1,)), ((), ())), preferred_element_type=F32)
          for g in range(N_GROUPS)]
    y = jnp.where(head < H_C // N_GROUPS, ys[0], ys[1]) * jnp.exp(cum_e)
    for h in range(H_C):
        seg = cum[:, h:h + 1] - cum_t[h:h + 1, :]
        p = (cb[h // (H_C // N_GROUPS)] * jnp.exp(jnp.where(causal, seg, -jnp.inf))).astype(BF16)
        y = y + jnp.where(head == h, jnp.dot(p, xb, preferred_element_type=F32), 0.0)
    y_ref[...] = y

    xt = (x * jnp.exp(last_e - cum_e)).T.astype(BF16)
    upd = [jnp.dot(xt, bg[g], preferred_element_type=F32) for g in range(N_GROUPS)]
    rowi = lax.broadcasted_iota(jnp.int32, (W_C, gn), 0)
    sel = lax.broadcasted_iota(jnp.int32, (W_C, LANES), 1) == lax.broadcasted_iota(jnp.int32, (W_C, LANES), 0) // HEAD_DIM
    tot = jnp.sum(jnp.where(sel, cum[LC - 1:LC, :], 0.0), axis=1, keepdims=True)
    st[...] = st[...] * jnp.exp(tot) + jnp.where(rowi < half, upd[0], upd[1])
    h_ref[0] = st[...]


def _ssd_chunk(xdt, bm, cm, da, ex, n_seq, t_len):
    n_c = t_len // LC
    gn = N_GROUPS * N_STATE
    rows = lambda w: pl.BlockSpec((LC, w), lambda b, c: (b * n_c + c, 0))
    return pl.pallas_call(
        _ssd_chunk_kernel,
        grid=(n_seq, n_c),
        in_specs=[rows(W_C), rows(gn), rows(gn), rows(LANES), pl.BlockSpec((LANES, W_C), lambda b, c: (0, 0))],
        out_specs=[rows(W_C), pl.BlockSpec((1, W_C, N_STATE), lambda b, c: (b, 0, 0))],
        out_shape=[jax.ShapeDtypeStruct((n_seq * t_len, W_C), F32),
                   jax.ShapeDtypeStruct((n_seq, W_C, N_STATE), F32)],
        scratch_shapes=[pltpu.VMEM((W_C, N_STATE), F32)],
        compiler_params=_params(("parallel", "arbitrary")),
        name="ssd_chunk",
    )(xdt, bm, cm, da, ex)


def _post_kernel(x_ref, ya_ref, bonus_ref, ga_ref, ob_ref, gb_ref, yc_ref, xs_ref, z_ref, lnw_ref, lnb_ref,
                 rn_ref, dsk_ref, sn_ref, bda_ref, bdb_ref, wo_ref, out_ref):
    inv_hd = 1.0 / HEAD_DIM
    y = ya_ref[...]
    bda = bda_ref[...]
    mean = _dot_hi(y, bda) * inv_hd
    yd = y - mean
    var = _dot_hi(yd * yd, bda) * inv_hd
    ya = (yd * lax.rsqrt(var + GN_EPS) * lnw_ref[...] + lnb_ref[...] + bonus_ref[...]) * ga_ref[...]
    o = ob_ref[...]
    ms = _dot_hi(o * o, bdb_ref[...]) * inv_hd
    gb = gb_ref[...]
    yb = (gb * _sigmoid(gb)) * (o * lax.rsqrt(ms + RMS_EPS) * rn_ref[...])
    z = z_ref[...]
    yc = (yc_ref[...] + dsk_ref[...] * xs_ref[...]) * (z * _sigmoid(z))
    yc = _rmsnorm(yc, sn_ref[...], GATED_NORM_EPS)
    out_ref[...] = (x_ref[...] + _dot(ya, wo_ref[0:W_A, :]) + _dot(yb, wo_ref[W_A:W_A + W_B, :])
                    + _dot(yc, wo_ref[W_A + W_B:, :]))


def _post(x, ya, bonus, ga, ob, cb, yc, xs, cc, p):
    n = x.shape[0]
    va = _full_spec((1, W_A))
    return pl.pallas_call(
        _post_kernel,
        grid=(n // TM,),
        in_specs=[_row_spec(D_MODEL), _row_spec(W_A), _row_spec(W_A), _row_spec(W_A), _row_spec(W_B),
                  _row_spec(W_B, 3), _row_spec(W_C), _row_spec(W_C), _row_spec(W_C, 0), va, va,
                  _full_spec((1, W_B)), va, va, _full_spec((W_A, W_A)), _full_spec((W_B, W_B)),
                  _full_spec((D_MODEL, D_MODEL))],
        out_specs=_row_spec(D_MODEL),
        out_shape=jax.ShapeDtypeStruct((n, D_MODEL), F32),
        compiler_params=_params(("parallel",)),
        name="post_outproj",
    )(x, ya, bonus, ga, ob, cb, yc, xs, cc, p["ln_w"], p["ln_b"], p["ret_norm"], p["d_skip"], p["ssm_norm"],
      p["bd_a"], p["bd_b"], p["w_out"])


def _ffn_kernel(*refs, gated, final_norm, n_e):
    refs = list(refs)
    x_ref, nf_ref = refs[:2]
    pos = 2
    if gated:
        router_ref = refs[pos]
        pos += 1
    wg_ref, wu_ref, wd_ref = refs[pos:pos + 3]
    pos += 3
    if final_norm:
        nfin_ref = refs[pos]
        pos += 1
    out_ref = refs[pos]
    hb_s, acc_s = refs[pos + 1:pos + 3]
    if gated:
        gate_s = refs[pos + 3]
    e = pl.program_id(1)

    @pl.when(e == 0)
    def _():
        h = _rmsnorm(x_ref[...], nf_ref[...], RMS_EPS)
        hb_s[...] = h.astype(BF16)
        acc_s[...] = jnp.zeros_like(acc_s)
        if gated:
            lane = lax.broadcasted_iota(jnp.int32, (TM, LANES), 1)
            logits = jnp.where(lane < N_EXPERTS, _dot_hi(h, router_ref[...]), -jnp.inf)
            p = jnp.exp(logits - jnp.max(logits, axis=-1, keepdims=True))
            p = p / jnp.sum(p, axis=-1, keepdims=True)
            p1 = jnp.max(p, axis=-1, keepdims=True)
            i1 = jnp.min(jnp.where(p == p1, lane, LANES), axis=-1, keepdims=True)
            rest = jnp.where(lane == i1, -1.0, p)
            p2 = jnp.max(rest, axis=-1, keepdims=True)
            i2 = jnp.min(jnp.where(rest == p2, lane, LANES), axis=-1, keepdims=True)
            gate_s[...] = jnp.where(lane == i1, p1, jnp.where(lane == i2, p2, 0.0)) / (p1 + p2)

    hb = hb_s[...]
    g = jnp.dot(hb, wg_ref[0], preferred_element_type=F32)
    u = jnp.dot(hb, wu_ref[0], preferred_element_type=F32)
    o = _dot(g * _sigmoid(g) * u, wd_ref[0])
    if gated:
        lane = lax.broadcasted_iota(jnp.int32, (TM, LANES), 1)
        o = o * jnp.sum(jnp.where(lane == e, gate_s[...], 0.0), axis=-1, keepdims=True)
    acc_s[...] += o

    @pl.when(e == n_e - 1)
    def _():
        y = x_ref[...] + acc_s[...]
        if final_norm:
            y = _rmsnorm(y, nfin_ref[...], RMS_EPS)
        out_ref[...] = y


def _ffn(x, nf, wg, wu, wd, router=None, nfin=None):
    n = x.shape[0]
    n_e, _, dff = wg.shape
    gated = router is not None
    final_norm = nfin is not None
    vec = pl.BlockSpec((1, D_MODEL), lambda i, e: (0, 0))
    args = [x, nf]
    specs = [pl.BlockSpec((TM, D_MODEL), lambda i, e: (i, 0)), vec]
    if gated:
        args.append(router)
        specs.append(pl.BlockSpec((D_MODEL, LANES), lambda i, e: (0, 0)))
    args += [wg, wu, wd]
    specs += [pl.BlockSpec((1, D_MODEL, dff), lambda i, e: (e, 0, 0)),
              pl.BlockSpec((1, D_MODEL, dff), lambda i, e: (e, 0, 0)),
              pl.BlockSpec((1, dff, D_MODEL), lambda i, e: (e, 0, 0))]
    if final_norm:
        args.append(nfin)
        specs.append(vec)
    scratch = [pltpu.VMEM((TM, D_MODEL), BF16), pltpu.VMEM((TM, D_MODEL), F32)]
    if gated:
        scratch.append(pltpu.VMEM((TM, LANES), F32))
    return pl.pallas_call(
        functools.partial(_ffn_kernel, gated=gated, final_norm=final_norm, n_e=n_e),
        grid=(n // TM, n_e),
        in_specs=specs,
        out_specs=pl.BlockSpec((TM, D_MODEL), lambda i, e: (i, 0)),
        out_shape=jax.ShapeDtypeStruct((n, D_MODEL), F32),
        scratch_shapes=scratch,
        compiler_params=_params(("parallel", "arbitrary")),
        name="moe" if gated else "ffn",
    )(*args)


def _heads(x, b, h):
    n, w = x.shape
    return x.reshape(b, n // b, h, w // h)


def _lanes(x):
    b, t, h, f = x.shape
    return jnp.transpose(x, (1, 3, 0, 2)).reshape(t, f, b * h)


def _from_lanes(y, b, h):
    t, j, _ = y.shape
    return jnp.transpose(y.reshape(t, j, b, h), (2, 0, 3, 1)).reshape(b * t, h * j)


def _state_to_lanes(s, j_first):
    b, h = s.shape[:2]
    y = jnp.transpose(s, (3, 2, 0, 1) if j_first else (2, 3, 0, 1))
    return y.reshape(y.shape[0], y.shape[1], b * h)


def _state_from_lanes(s, b, h, j_first):
    ni, nj, _ = s.shape
    return jnp.transpose(s.reshape(ni, nj, b, h), (2, 3, 1, 0) if j_first else (2, 3, 0, 1))


def _rwkv_long(feats, b):
    r, dec, k2, v, nkk, beta = feats
    t = r.shape[0] // b
    kl = HEAD_DIM // K_SPLIT
    assert K_SPLIT * b * H_PAD == LANES

    def key_rows(x):
        return jnp.transpose(x.reshape(b, t, H_PAD, K_SPLIT, kl), (1, 4, 3, 0, 2)).reshape(t, kl, LANES)

    vv = jnp.transpose(v.reshape(b, t, H_PAD, HEAD_DIM), (1, 3, 0, 2))
    vv = jnp.broadcast_to(vv[:, :, None], (t, HEAD_DIM, K_SPLIT, b, H_PAD)).reshape(t, HEAD_DIM, LANES)
    y, s = _scan("rwkv", key_rows(r), key_rows(dec), key_rows(k2), vv, jnp.zeros((kl, HEAD_DIM, LANES), F32),
                 n=key_rows(nkk), a2=key_rows(beta), tt_len=64, fold_halves=True)
    y = y[:, :, :b * H_PAD].reshape(t, HEAD_DIM, b, H_PAD)[..., :H_A]
    ya = jnp.transpose(y, (2, 0, 3, 1)).reshape(b * t, W_A)
    s = s.reshape(kl, HEAD_DIM, K_SPLIT, b, H_PAD)[..., :H_A]
    return ya, jnp.transpose(s, (3, 4, 1, 2, 0)).reshape(b, H_A, HEAD_DIM, HEAD_DIM)


def _rwkv_short(feats, b, s0):
    r, dec, k2, v, nkk, beta = [_lanes(_heads(f, b, H_A)) for f in feats]
    y, s = _scan("rwkv", r, dec, k2, v, _state_to_lanes(s0, True), n=nkk, a2=beta)
    return _from_lanes(y, b, H_A), _state_from_lanes(s, b, H_A, True)


def _ret_short(qh, kh, vb, b, s0):
    gam = np.tile(1.0 - 2.0 ** (-5.0 - np.arange(H_B, dtype=np.float64)), b).astype(np.float32)
    y, s = _scan("ret", _lanes(_heads(qh, b, H_B)), jnp.asarray(gam.reshape(1, 1, -1)), _lanes(_heads(kh, b, H_B)),
                 _lanes(_heads(vb, b, H_B)), _state_to_lanes(s0, False))
    return _from_lanes(y, b, H_B), _state_from_lanes(s, b, H_B, False)


def _ssd_short(xdt, bm, cm, dssm, b, s0):
    rep = H_C // N_GROUPS
    bh = jnp.repeat(_heads(bm, b, N_GROUPS), rep, axis=2)
    ch = jnp.repeat(_heads(cm, b, N_GROUPS), rep, axis=2)
    dec = _heads(dssm, b, 1)[:, :, 0, :H_C, None]
    y, s = _scan("ssd", _lanes(ch), _lanes(dec), _lanes(bh), _lanes(_heads(xdt, b, H_C)), _state_to_lanes(s0, True))
    return _from_lanes(y, b, H_C), _state_from_lanes(s, b, H_C, True)


def _block_diag_ones(width):
    idx = np.arange(width) // HEAD_DIM
    return jnp.asarray((idx[:, None] == idx[None, :]).astype(np.float32))


def _shifted(x, prev_rows, shift):
    b, t, c = x.shape
    p = prev_rows.shape[1]
    full = jnp.concatenate([prev_rows, x], axis=1)
    return full[:, p - shift:p - shift + t].reshape(b * t, c)


def _layer(x, b, p, st, rope, table_block):
    n = x.shape[0]
    t = n // b
    fresh = st is None
    tiles_per_seq = t // TM if fresh else 0
    ca, cb, cc = _in_proj(x, p["norm_mix"], p["w_in"])
    ca3 = ca.reshape(b, t, COLS_A)
    xbc_tail = cc.reshape(b, t, COLS_C_PAD)[:, -(CONV_W - 1):, W_C:W_C + CONV_DIM]
    if fresh:
        prev, shifted = None, None
        conv_new = xbc_tail
    else:
        prev = _shifted(ca3, st["shift"][:, None, :], 1)
        xbc = cc[:, W_C:W_C + CONV_DIM].reshape(b, t, CONV_DIM)
        shifted = [_shifted(xbc, st["conv"], j) for j in range(1, CONV_W)]
        conv_new = jnp.concatenate([st["conv"], xbc_tail], axis=1)[:, -(CONV_W - 1):]

    r, dec, k2, v, nkk, beta, ga, bonus = _rwkv_prep(ca, prev, p, tiles_per_seq)
    qh, kh = _ret_prep(cb, rope[0], rope[1], table_block)
    xdt, bm, cm, dssm, xs, da = _ssm_prep(cc, shifted, p, tiles_per_seq)

    feats = (r, dec, k2, v, nkk, beta)
    if fresh:
        ya, s_rwkv = _rwkv_long(feats, b)
        yb, s_full = _ret_chunk(qh, kh, cb, b, t)
        s_ret = jnp.stack([s_full[:, h * HEAD_DIM:(h + 1) * HEAD_DIM, h * HEAD_DIM:(h + 1) * HEAD_DIM]
                           for h in range(H_B)], axis=1)
        yc, s_ssm = _ssd_chunk(xdt, bm, cm, da, p["expand"], b, t)
        s_ssm = s_ssm.reshape(b, H_C, HEAD_DIM, N_STATE)
    else:
        ya, s_rwkv = _rwkv_short(feats, b, st["rwkv"])
        yb, s_ret = _ret_short(qh, kh, cb[:, 2 * W_B:3 * W_B], b, st["ret"])
        yc, s_ssm = _ssd_short(xdt, bm, cm, dssm, b, st["ssm"])

    x2 = _post(x, ya, bonus, ga, yb, cb, yc, xs, cc, p)
    x = _ffn(x2, p["norm_ffn"], p["wg"], p["wu"], p["wd"], router=p["router"], nfin=p["norm_final"])
    return x, (s_rwkv, ca3[:, -1], s_ret, s_ssm, conv_new)


def kernel(x_prompt, x_sample, state_rwkv, state_shift, state_ret, state_ssm, state_conv, norm_mix, w_in, rwkv_mu, rwkv_w0, rwkv_w_up, rwkv_a0, rwkv_a_up, rwkv_g_up, rwkv_k_k, rwkv_k_a, rwkv_r_k, rwkv_ln_w, rwkv_ln_b, ret_norm, ssm_conv_w, ssm_conv_b, ssm_dt_bias, ssm_a_log, ssm_d, ssm_norm, w_out, norm_ffn, ffn_w_gate, ffn_w_up, ffn_w_down, moe_router, moe_w_gate, moe_w_up, moe_w_down, norm_final):
    bp, tp, _ = x_prompt.shape
    bs, ts, _ = x_sample.shape
    depth = w_in.shape[0]
    assert tp % TM == 0 and tp % LC == 0 and (bs * ts) % TM == 0 and TM % ts == 0

    rope = _rope_tables(tp, ts)
    bd_a, bd_b = _block_diag_ones(W_A), _block_diag_ones(W_B)
    expand = np.zeros((LANES, W_C), np.float32)
    expand[np.arange(W_C) // HEAD_DIM, np.arange(W_C)] = 1.0
    expand = jnp.asarray(expand)
    row = lambda v: v.reshape(1, -1)
    pad_l = lambda v: jnp.pad(v, (0, LANES - v.shape[0])).reshape(1, LANES)

    xp = x_prompt.reshape(bp * tp, D_MODEL)
    xs = x_sample.reshape(bs * ts, D_MODEL)
    new_p, new_s = [], []
    for i in range(depth):
        j = i // 2
        p = dict(
            norm_mix=row(norm_mix[i]),
            w_in=jnp.pad(w_in[i], ((0, 0), (0, COLS_C_PAD - COLS_C))).astype(BF16),
            mu=row(rwkv_mu[i]), w0=row(rwkv_w0[i]), a0=row(rwkv_a0[i]), k_k=row(rwkv_k_k[i]),
            k_a=row(rwkv_k_a[i]), r_k=row(rwkv_r_k[i]),
            w_up=jnp.pad(rwkv_w_up[i], ((0, AAA_LORA), (0, 0))).astype(BF16),
            a_up=jnp.pad(rwkv_a_up[i], ((DECAY_LORA, 0), (0, 0))).astype(BF16),
            g_up=rwkv_g_up[i].astype(BF16), bd_a=bd_a, bd_b=bd_b,
            conv_w=ssm_conv_w[i], conv_b=row(ssm_conv_b[i]), dt_bias=pad_l(ssm_dt_bias[i]),
            a_log=pad_l(ssm_a_log[i]), expand=expand,
            ln_w=row(rwkv_ln_w[i]), ln_b=row(rwkv_ln_b[i]), ret_norm=row(ret_norm[i]),
            d_skip=row(jnp.repeat(ssm_d[i], HEAD_DIM)), ssm_norm=row(ssm_norm[i]), w_out=w_out[i].astype(BF16),
            norm_ffn=row(norm_ffn[i]), norm_final=row(norm_final) if i == depth - 1 else None)
        if i % 2 == 0:
            dff = ffn_w_gate.shape[-1] // 2
            p.update(router=None,
                     wg=ffn_w_gate[j].reshape(D_MODEL, 2, dff).transpose(1, 0, 2).astype(BF16),
                     wu=ffn_w_up[j].reshape(D_MODEL, 2, dff).transpose(1, 0, 2).astype(BF16),
                     wd=ffn_w_down[j].reshape(2, dff, D_MODEL).astype(BF16))
        else:
            p.update(router=jnp.pad(moe_router[j], ((0, 0), (0, LANES - N_EXPERTS))),
                     wg=moe_w_gate[j].astype(BF16), wu=moe_w_up[j].astype(BF16), wd=moe_w_down[j].astype(BF16))

        xp, st_p = _layer(xp, bp, p, None, rope, lambda t: t % (tp // TM))
        st = dict(rwkv=state_rwkv[i], shift=state_shift[i], ret=state_ret[i], ssm=state_ssm[i],
                  conv=state_conv[i])
        xs, st_s = _layer(xs, bs, p, st, rope, lambda t: tp // TM)
        new_p.append(st_p)
        new_s.append(st_s)

    stack = lambda sts: tuple(jnp.stack(s) for s in zip(*sts))
    return (xp.reshape(bp, tp, D_MODEL), xs.reshape(bs, ts, D_MODEL)) + stack(new_p) + stack(new_s)
```

```python
import functools
import math

import numpy as np
import jax
import jax.numpy as jnp
from jax import lax
from jax.experimental import pallas as pl
from jax.experimental.pallas import tpu as pltpu

F32 = jnp.float32
BF16 = jnp.bfloat16
HIGHEST = lax.Precision.HIGHEST

LANES = 128
SUBLANES = 8
VMEM_LIMIT = 56 * 1024 * 1024

D_MODEL = 1024
HEAD_DIM = 64
H_A, H_B, H_C = 6, 4, 6
W_A, W_B, W_C = H_A * HEAD_DIM, H_B * HEAD_DIM, H_C * HEAD_DIM
DECAY_LORA, AAA_LORA, GATE_LORA = 64, 64, 128
COLS_A = 3 * W_A + DECAY_LORA + AAA_LORA + GATE_LORA
COLS_B = 4 * W_B
N_STATE, N_GROUPS, CONV_W = 128, 2, 4
CONV_DIM = W_C + 2 * N_GROUPS * N_STATE
COLS_C = W_C + CONV_DIM + H_C
COLS_C_PAD = 1408
ROPE_BASE = 10000.0
RMS_EPS = 1e-6
GN_EPS = 64e-5
GATED_NORM_EPS = 1e-5
N_EXPERTS = 8
PAST_LEN = 16384

TM = 512
LC = 256
H_PAD = 8
K_SPLIT = 2


def _dot(a, b):
    return jnp.dot(a.astype(BF16), b.astype(BF16), preferred_element_type=F32)


def _dot_hi(a, b):
    return jnp.dot(a, b, precision=HIGHEST, preferred_element_type=F32)


def _sigmoid(x):
    return 1.0 / (1.0 + jnp.exp(-x))


def _softplus(x):
    return jnp.maximum(x, 0.0) + jnp.log1p(jnp.exp(-jnp.abs(x)))


def _rmsnorm(x, g, eps):
    return x * lax.rsqrt(jnp.mean(x * x, axis=-1, keepdims=True) + eps) * g


def _params(sem):
    return pltpu.CompilerParams(dimension_semantics=sem, vmem_limit_bytes=VMEM_LIMIT)


def _row_spec(width, col=0):
    return pl.BlockSpec((TM, width), lambda i, c=col: (i, c))


def _before_spec(width):
    return pl.BlockSpec((SUBLANES, width), lambda i: (jnp.maximum(i * (TM // SUBLANES) - 1, 0), 0))


def _full_spec(shape):
    nd = len(shape)
    return pl.BlockSpec(shape, lambda i, n=nd: (0,) * n)


def _shift_rows(x, before, j, first):
    rolled = pltpu.roll(x, j, 0)
    prev = jnp.where(first, 0.0, pltpu.roll(before, j, 0))
    row = lax.broadcasted_iota(jnp.int32, prev.shape, 0)
    top = jnp.where(row < j, prev, rolled[0:SUBLANES])
    return jnp.concatenate([top, rolled[SUBLANES:]], axis=0)


def _in_proj_kernel(x_ref, g_ref, w_ref, oa_ref, ob_ref, oc_ref):
    h = _rmsnorm(x_ref[...], g_ref[...], RMS_EPS).astype(BF16)
    oa_ref[...] = jnp.dot(h, w_ref[:, 0:COLS_A], preferred_element_type=F32)
    ob_ref[...] = jnp.dot(h, w_ref[:, COLS_A:COLS_A + COLS_B], preferred_element_type=F32)
    oc_ref[...] = jnp.dot(h, w_ref[:, COLS_A + COLS_B:], preferred_element_type=F32)


def _in_proj(x, g, w):
    n = x.shape[0]
    wtot = w.shape[1]
    return pl.pallas_call(
        _in_proj_kernel,
        grid=(n // TM,),
        in_specs=[_row_spec(D_MODEL), _full_spec((1, D_MODEL)), _full_spec((D_MODEL, wtot))],
        out_specs=[_row_spec(COLS_A), _row_spec(COLS_B), _row_spec(COLS_C_PAD)],
        out_shape=[jax.ShapeDtypeStruct((n, COLS_A), F32), jax.ShapeDtypeStruct((n, COLS_B), F32),
                   jax.ShapeDtypeStruct((n, COLS_C_PAD), F32)],
        compiler_params=_params(("parallel",)),
        name="in_proj",
    )(x, g, w)


def _rwkv_prep_kernel(c_ref, p_ref, mu_ref, w0_ref, a0_ref, kk_ref, ka_ref, rk_ref, wup_ref, aup_ref,
                      gup_ref, bd_ref, r_o, d_o, k_o, v_o, n_o, b_o, g_o, bonus_o, *, tiles_per_seq):
    c = c_ref[...]
    if tiles_per_seq:
        prev = _shift_rows(c, p_ref[...], 1, pl.program_id(0) % tiles_per_seq == 0)
    else:
        prev = p_ref[...]
    xm = c + (prev - c) * mu_ref[...]
    r = xm[:, 0:W_A]
    k = xm[:, W_A:2 * W_A]
    v = xm[:, 2 * W_A:3 * W_A]
    lora = xm[:, 3 * W_A:3 * W_A + DECAY_LORA + AAA_LORA]
    gd = xm[:, 3 * W_A + DECAY_LORA + AAA_LORA:]
    w = w0_ref[...] + _dot(jnp.tanh(lora), wup_ref[...])
    decay = jnp.exp(-math.exp(-0.5) * _sigmoid(w))
    a = _sigmoid(a0_ref[...] + _dot(lora, aup_ref[...]))
    g = _dot(_sigmoid(gd), gup_ref[...])
    bd = bd_ref[...]
    kk = k * kk_ref[...]
    kk = kk / jnp.maximum(jnp.sqrt(_dot_hi(kk * kk, bd)), 1e-12)
    k2 = k * (1.0 + (a - 1.0) * ka_ref[...])
    for o_ref, val in ((r_o, r), (d_o, decay), (k_o, k2), (v_o, v), (n_o, -kk), (b_o, kk * a)):
        o_ref[:, 0:W_A] = val
        if o_ref.shape[1] > W_A:
            o_ref[:, W_A:] = jnp.zeros((TM, o_ref.shape[1] - W_A), F32)
    g_o[...] = g
    bonus_o[...] = _dot_hi(r * k2 * rk_ref[...], bd) * v


def _rwkv_prep(ca, prev, p, tiles_per_seq):
    n = ca.shape[0]
    vec = _full_spec((1, W_A))
    lora_spec = _full_spec((DECAY_LORA + AAA_LORA, W_A))
    p_spec = _before_spec(COLS_A) if tiles_per_seq else _row_spec(COLS_A)
    w_scan = H_PAD * HEAD_DIM if tiles_per_seq else W_A
    widths = [w_scan] * 6 + [W_A] * 2
    return pl.pallas_call(
        functools.partial(_rwkv_prep_kernel, tiles_per_seq=tiles_per_seq),
        grid=(n // TM,),
        in_specs=[_row_spec(COLS_A), p_spec, _full_spec((1, COLS_A)), vec, vec, vec, vec, vec,
                  lora_spec, lora_spec, _full_spec((GATE_LORA, W_A)), _full_spec((W_A, W_A))],
        out_specs=[_row_spec(w) for w in widths],
        out_shape=[jax.ShapeDtypeStruct((n, w), F32) for w in widths],
        compiler_params=_params(("parallel",)),
        name="rwkv_prep",
    )(ca, ca if tiles_per_seq else prev, p["mu"], p["w0"], p["a0"], p["k_k"], p["k_a"], p["r_k"],
      p["w_up"], p["a_up"], p["g_up"], p["bd_a"])


def _rope_kernel(ang_ref, cos_o, sin_o):
    ang = ang_ref[...]
    lane = lax.broadcasted_iota(jnp.int32, ang.shape, 1)
    cos_o[...] = jnp.cos(ang)
    sin_o[...] = jnp.where((lane % HEAD_DIM) < (HEAD_DIM // 2), -jnp.sin(ang), jnp.sin(ang))


def _rope_tables(t_prompt, t_sample):
    theta = 1.0 / (ROPE_BASE ** jnp.linspace(0.0, 1.0, HEAD_DIM // 2, dtype=F32))
    pos = jnp.concatenate([jnp.arange(t_prompt, dtype=F32), PAST_LEN + (jnp.arange(TM) % t_sample).astype(F32)])
    ang = jnp.tile(pos[:, None] * theta[None, :], (1, W_B // (HEAD_DIM // 2)))
    n = ang.shape[0]
    return pl.pallas_call(
        _rope_kernel,
        grid=(n // TM,),
        in_specs=[_row_spec(W_B)],
        out_specs=[_row_spec(W_B)] * 2,
        out_shape=[jax.ShapeDtypeStruct((n, W_B), F32)] * 2,
        compiler_params=_params(("parallel",)),
        name="rope_tables",
    )(ang)


def _ret_prep_kernel(q_ref, k_ref, cos_ref, sin_ref, q_o, k_o):
    cos = cos_ref[...]
    sin = sin_ref[...]
    lane = lax.broadcasted_iota(jnp.int32, (TM, W_B), 1)
    first_half = (lane % HEAD_DIM) < (HEAD_DIM // 2)

    def rope(x):
        partner = jnp.where(first_half, pltpu.roll(x, W_B - HEAD_DIM // 2, 1), pltpu.roll(x, HEAD_DIM // 2, 1))
        return x * cos + partner * sin

    q_o[...] = rope(q_ref[...])
    k_o[...] = rope(k_ref[...]) * (HEAD_DIM ** -0.5)


def _ret_prep(cb, cos, sin, table_block):
    n = cb.shape[0]
    tab = pl.BlockSpec((TM, W_B), lambda i: (table_block(i), 0))
    return pl.pallas_call(
        _ret_prep_kernel,
        grid=(n // TM,),
        in_specs=[_row_spec(W_B, 0), _row_spec(W_B, 1), tab, tab],
        out_specs=[_row_spec(W_B)] * 2,
        out_shape=[jax.ShapeDtypeStruct((n, W_B), F32)] * 2,
        compiler_params=_params(("parallel",)),
        name="ret_prep",
    )(cb, cb, cos, sin)


def _ssm_prep_kernel(*refs, tiles_per_seq):
    n_shift_refs = 1 if tiles_per_seq else CONV_W - 1
    cc_ref = refs[0]
    shift_refs = refs[1:1 + n_shift_refs]
    cw_ref, cb_ref, dtb_ref, alog_ref, ex_ref, xdt_o, b_o, c_o, dec_o, xs_o, da_o = refs[1 + n_shift_refs:]
    cc = cc_ref[...]
    x0 = cc[:, W_C:W_C + CONV_DIM]
    if tiles_per_seq:
        before = shift_refs[0][:, W_C:W_C + CONV_DIM]
        first = pl.program_id(0) % tiles_per_seq == 0
        xs_prev = [_shift_rows(x0, before, j, first) for j in range(1, CONV_W)]
    else:
        xs_prev = [r[...] for r in shift_refs]
    cw = cw_ref[...]
    conv = x0 * cw[CONV_W - 1:CONV_W, :] + cb_ref[...]
    for j in range(1, CONV_W):
        conv = conv + xs_prev[j - 1] * cw[CONV_W - 1 - j:CONV_W - j, :]
    act = conv * _sigmoid(conv)
    xs = act[:, 0:W_C]
    dt = _softplus(cc[:, W_C + CONV_DIM:] + dtb_ref[...])
    da = dt * (-jnp.exp(alog_ref[...]))
    da_o[...] = da
    dec_o[...] = jnp.exp(da)
    xdt_o[...] = xs * _dot_hi(dt, ex_ref[...])
    b_o[...] = act[:, W_C:W_C + N_GROUPS * N_STATE]
    c_o[...] = act[:, W_C + N_GROUPS * N_STATE:]
    xs_o[...] = xs


def _ssm_prep(cc, shifted, p, tiles_per_seq):
    n = cc.shape[0]
    gn = N_GROUPS * N_STATE
    if tiles_per_seq:
        shift_args, shift_specs = [cc], [_before_spec(COLS_C_PAD)]
    else:
        shift_args, shift_specs = list(shifted), [_row_spec(CONV_DIM)] * (CONV_W - 1)
    return pl.pallas_call(
        functools.partial(_ssm_prep_kernel, tiles_per_seq=tiles_per_seq),
        grid=(n // TM,),
        in_specs=[_row_spec(COLS_C_PAD)] + shift_specs + [
            _full_spec((CONV_W, CONV_DIM)), _full_spec((1, CONV_DIM)), _full_spec((1, LANES)),
            _full_spec((1, LANES)), _full_spec((LANES, W_C))],
        out_specs=[_row_spec(W_C), _row_spec(gn), _row_spec(gn), _row_spec(LANES), _row_spec(W_C),
                   _row_spec(LANES)],
        out_shape=[jax.ShapeDtypeStruct((n, W_C), F32), jax.ShapeDtypeStruct((n, gn), F32),
                   jax.ShapeDtypeStruct((n, gn), F32), jax.ShapeDtypeStruct((n, LANES), F32),
                   jax.ShapeDtypeStruct((n, W_C), F32), jax.ShapeDtypeStruct((n, LANES), F32)],
        compiler_params=_params(("parallel",)),
        name="ssm_prep",
    )(cc, *shift_args, p["conv_w"], p["conv_b"], p["dt_bias"], p["a_log"], p["expand"])


def _scan_kernel(*refs, mode, ni, tt_len, n_tt):
    if mode == "rwkv":
        q_ref, d_ref, a_ref, b_ref, n_ref, a2_ref, s0_ref, y_ref, so_ref, st = refs
    else:
        q_ref, d_ref, a_ref, b_ref, s0_ref, y_ref, so_ref, st = refs
    tt = pl.program_id(1)

    @pl.when(tt == 0)
    def _():
        st[...] = s0_ref[...]

    def row(ref, t, i):
        return ref[t, pl.ds(i, 1), :]

    def step(t, carry):
        bv = b_ref[t]
        if mode == "rwkv":
            sa_parts = [jnp.zeros_like(bv), jnp.zeros_like(bv)]
            for i in range(ni):
                sa_parts[i % 2] = sa_parts[i % 2] + st[i] * row(n_ref, t, i)
            sa = sa_parts[0] + sa_parts[1]
        elif mode == "ssd":
            d = d_ref[t]
        else:
            d = d_ref[0]
        y_parts = [jnp.zeros_like(bv), jnp.zeros_like(bv)]
        for i in range(ni):
            if mode == "rwkv":
                s = st[i] * row(d_ref, t, i) + row(a_ref, t, i) * bv + row(a2_ref, t, i) * sa
            else:
                s = st[i] * d + row(a_ref, t, i) * bv
            st[i] = s
            y_parts[i % 2] = y_parts[i % 2] + s * row(q_ref, t, i)
        y_ref[t] = y_parts[0] + y_parts[1]
        return carry

    lax.fori_loop(0, tt_len, step, 0)

    @pl.when(tt == n_tt - 1)
    def _():
        so_ref[...] = st[...]


def _scan(mode, q, d, a, b, s0, n=None, a2=None, tt_len=32):
    t_len, ni, lanes = q.shape
    nj = b.shape[1]
    tt_len = min(tt_len, t_len)
    n_tt = t_len // tt_len
    groups = lanes // LANES
    i_spec = pl.BlockSpec((tt_len, ni, LANES), lambda g, t: (t, 0, g))
    j_spec = pl.BlockSpec((tt_len, nj, LANES), lambda g, t: (t, 0, g))
    s_spec = pl.BlockSpec((ni, nj, LANES), lambda g, t: (0, 0, g))
    if mode == "rwkv":
        d_spec = i_spec
    elif mode == "ssd":
        d_spec = pl.BlockSpec((tt_len, 1, LANES), lambda g, t: (t, 0, g))
    else:
        d_spec = pl.BlockSpec((1, 1, LANES), lambda g, t: (0, 0, g))
    args = [q, d, a, b]
    specs = [i_spec, d_spec, i_spec, j_spec]
    if mode == "rwkv":
        args += [n, a2]
        specs += [i_spec, i_spec]
    args.append(s0)
    specs.append(s_spec)
    return pl.pallas_call(
        functools.partial(_scan_kernel, mode=mode, ni=ni, tt_len=tt_len, n_tt=n_tt),
        grid=(groups, n_tt),
        in_specs=specs,
        out_specs=[j_spec, s_spec],
        out_shape=[jax.ShapeDtypeStruct((t_len, nj, lanes), F32), jax.ShapeDtypeStruct((ni, nj, lanes), F32)],
        scratch_shapes=[pltpu.VMEM((ni, nj, LANES), F32)],
        compiler_params=_params(("parallel", "arbitrary")),
        name="scan_" + mode,
    )(*args)


def _rwkv_long_kernel(q_ref, d_ref, a_ref, b_ref, n1_ref, a2_ref, y_ref, so_ref, st, sa_s, *, ni, tt_len, n_tt):
    tt = pl.program_id(1)

    @pl.when(tt == 0)
    def _():
        st[...] = jnp.zeros_like(st)
        sa_s[...] = jnp.zeros_like(sa_s)

    def row(ref, t, i):
        return ref[t, pl.ds(i, 1), :]

    def fold(x):
        return x + pltpu.roll(x, LANES // 2, 1)

    def step(t, sa):
        bv = b_ref[t]
        a, a2, q, n1 = a_ref[t], a2_ref[t], q_ref[t], n1_ref[t]
        aq, a2q, an, a2n = [fold(jnp.sum(u * w, axis=0, keepdims=True))
                            for u, w in ((a, q), (a2, q), (a, n1), (a2, n1))]
        y0 = [jnp.zeros_like(bv), jnp.zeros_like(bv)]
        n0 = [jnp.zeros_like(bv), jnp.zeros_like(bv)]
        for i in range(ni):
            sd = st[i] * row(d_ref, t, i)
            st[i] = sd
            y0[i % 2] = y0[i % 2] + sd * row(q_ref, t, i)
            n0[i % 2] = n0[i % 2] + sd * row(n1_ref, t, i)
        y_ref[t] = fold(y0[0] + y0[1]) + bv * aq + sa * a2q
        sa_next = fold(n0[0] + n0[1]) + bv * an + sa * a2n
        for i in range(ni):
            st[i] = st[i] + row(a_ref, t, i) * bv + row(a2_ref, t, i) * sa
        return sa_next

    sa_s[...] = lax.fori_loop(0, tt_len, step, sa_s[...])

    @pl.when(tt == n_tt - 1)
    def _():
        so_ref[...] = st[...]


def _rwkv_long_scan(q, d, a, b, n1, a2, tt_len=64):
    t_len, ni, _ = q.shape
    nj = b.shape[1]
    n_tt = t_len // tt_len
    i_spec = pl.BlockSpec((tt_len, ni, LANES), lambda g, t: (t, 0, 0))
    j_spec = pl.BlockSpec((tt_len, nj, LANES), lambda g, t: (t, 0, 0))
    s_spec = pl.BlockSpec((ni, nj, LANES), lambda g, t: (0, 0, 0))
    return pl.pallas_call(
        functools.partial(_rwkv_long_kernel, ni=ni, tt_len=tt_len, n_tt=n_tt),
        grid=(1, n_tt),
        in_specs=[i_spec, i_spec, i_spec, j_spec, i_spec, i_spec],
        out_specs=[j_spec, s_spec],
        out_shape=[jax.ShapeDtypeStruct((t_len, nj, LANES), F32), jax.ShapeDtypeStruct((ni, nj, LANES), F32)],
        scratch_shapes=[pltpu.VMEM((ni, nj, LANES), F32), pltpu.VMEM((nj, LANES), F32)],
        compiler_params=_params(("parallel", "arbitrary")),
        name="scan_rwkv_long",
    )(q, d, a, b, n1, a2)


def _ret_chunk_kernel(q_ref, k_ref, v_ref, y_ref, s_ref, st):
    c = pl.program_id(1)

    @pl.when(c == 0)
    def _():
        st[...] = jnp.zeros_like(st)

    q = q_ref[...]
    k = k_ref[...]
    vb = v_ref[...].astype(BF16)
    kb = k.astype(BF16)
    row = lax.broadcasted_iota(jnp.int32, (LC, W_B), 0)
    head = lax.broadcasted_iota(jnp.int32, (LC, W_B), 1) // HEAD_DIM
    log_g = [math.log(1.0 - 2.0 ** (-5.0 - h)) for h in range(H_B)]
    lg = jnp.full((LC, W_B), log_g[0], F32)
    for h in range(1, H_B):
        lg = jnp.where(head == h, log_g[h], lg)
    rowf = row.astype(F32)
    diff = (lax.broadcasted_iota(jnp.int32, (LC, LC), 0) - lax.broadcasted_iota(jnp.int32, (LC, LC), 1))
    causal = diff >= 0
    difff = jnp.maximum(diff, 0).astype(F32)

    out = jnp.dot((q * jnp.exp(lg * (rowf + 1.0))).astype(BF16), st[...].astype(BF16), preferred_element_type=F32)
    for h in range(H_B):
        qm = jnp.where(head == h, q, 0.0).astype(BF16)
        s = lax.dot_general(qm, kb, (((1,), (1,)), ((), ())), preferred_element_type=F32)
        p = jnp.where(causal, s * jnp.exp(log_g[h] * difff), 0.0).astype(BF16)
        out = out + jnp.where(head == h, jnp.dot(p, vb, preferred_element_type=F32), 0.0)
    y_ref[...] = out

    kt = (k * jnp.exp(lg * (LC - 1.0 - rowf))).T.astype(BF16)
    kv = jnp.dot(kt, vb, preferred_element_type=F32)
    r2 = lax.broadcasted_iota(jnp.int32, (W_B, W_B), 0) // HEAD_DIM
    c2 = lax.broadcasted_iota(jnp.int32, (W_B, W_B), 1) // HEAD_DIM
    cdec = jnp.full((W_B, W_B), math.exp(log_g[0] * LC), F32)
    for h in range(1, H_B):
        cdec = jnp.where(r2 == h, math.exp(log_g[h] * LC), cdec)
    st[...] = st[...] * cdec + jnp.where(r2 == c2, kv, 0.0)
    s_ref[0] = st[...]


def _ret_chunk(q, k, cb, n_seq, t_len):
    n_c = t_len // LC
    rows = lambda col: pl.BlockSpec((LC, W_B), lambda b, c, col=col: (b * n_c + c, col))
    return pl.pallas_call(
        _ret_chunk_kernel,
        grid=(n_seq, n_c),
        in_specs=[rows(0), rows(0), rows(2)],
        out_specs=[rows(0), pl.BlockSpec((1, W_B, W_B), lambda b, c: (b, 0, 0))],
        out_shape=[jax.ShapeDtypeStruct((n_seq * t_len, W_B), F32), jax.ShapeDtypeStruct((n_seq, W_B, W_B), F32)],
        scratch_shapes=[pltpu.VMEM((W_B, W_B), F32)],
        compiler_params=_params(("parallel", "arbitrary")),
        name="ret_chunk",
    )(q, k, cb)


def _ssd_chunk_kernel(x_ref, b_ref, c_ref, da_ref, ex_ref, y_ref, h_ref, st):
    ci = pl.program_id(1)

    @pl.when(ci == 0)
    def _():
        st[...] = jnp.zeros_like(st)

    gn = N_STATE
    half = W_C // N_GROUPS
    ex = ex_ref[...]
    x = x_ref[...]
    xb = x.astype(BF16)
    ri = lax.broadcasted_iota(jnp.int32, (LC, LC), 0)
    cj = lax.broadcasted_iota(jnp.int32, (LC, LC), 1)
    causal = ri >= cj
    cum = _dot_hi(causal.astype(F32), da_ref[...])
    cum_e = _dot_hi(cum, ex)
    last_e = cum_e[LC - 1:LC, :]
    cum_t = cum.T
    head = lax.broadcasted_iota(jnp.int32, (LC, W_C), 1) // HEAD_DIM
    bg = [b_ref[:, g * gn:(g + 1) * gn].astype(BF16) for g in range(N_GROUPS)]
    cg = [c_ref[:, g * gn:(g + 1) * gn].astype(BF16) for g in range(N_GROUPS)]
    cb = [lax.dot_general(cg[g], bg[g], (((1,), (1,)), ((), ())), preferred_element_type=F32)
          for g in range(N_GROUPS)]

    hb = st[...].astype(BF16)
    ys = [lax.dot_general(cg[g], hb, (((1,), (1,)), ((), ())), preferred_element_type=F32)
          for g in range(N_GROUPS)]
    y = jnp.where(head < H_C // N_GROUPS, ys[0], ys[1]) * jnp.exp(cum_e)
    for h in range(H_C):
        seg = cum[:, h:h + 1] - cum_t[h:h + 1, :]
        p = (cb[h // (H_C // N_GROUPS)] * jnp.exp(jnp.where(causal, seg, -jnp.inf))).astype(BF16)
        y = y + jnp.where(head == h, jnp.dot(p, xb, preferred_element_type=F32), 0.0)
    y_ref[...] = y

    xt = (x * jnp.exp(last_e - cum_e)).T.astype(BF16)
    upd = [jnp.dot(xt, bg[g], preferred_element_type=F32) for g in range(N_GROUPS)]
    rowi = lax.broadcasted_iota(jnp.int32, (W_C, gn), 0)
    sel = lax.broadcasted_iota(jnp.int32, (W_C, LANES), 1) == lax.broadcasted_iota(jnp.int32, (W_C, LANES), 0) // HEAD_DIM
    tot = jnp.sum(jnp.where(sel, cum[LC - 1:LC, :], 0.0), axis=1, keepdims=True)
    st[...] = st[...] * jnp.exp(tot) + jnp.where(rowi < half, upd[0], upd[1])
    h_ref[0] = st[...]


def _ssd_chunk(xdt, bm, cm, da, ex, n_seq, t_len):
    n_c = t_len // LC
    gn = N_GROUPS * N_STATE
    rows = lambda w: pl.BlockSpec((LC, w), lambda b, c: (b * n_c + c, 0))
    return pl.pallas_call(
        _ssd_chunk_kernel,
        grid=(n_seq, n_c),
        in_specs=[rows(W_C), rows(gn), rows(gn), rows(LANES), pl.BlockSpec((LANES, W_C), lambda b, c: (0, 0))],
        out_specs=[rows(W_C), pl.BlockSpec((1, W_C, N_STATE), lambda b, c: (b, 0, 0))],
        out_shape=[jax.ShapeDtypeStruct((n_seq * t_len, W_C), F32),
                   jax.ShapeDtypeStruct((n_seq, W_C, N_STATE), F32)],
        scratch_shapes=[pltpu.VMEM((W_C, N_STATE), F32)],
        compiler_params=_params(("parallel", "arbitrary")),
        name="ssd_chunk",
    )(xdt, bm, cm, da, ex)


def _post_kernel(x_ref, ya_ref, bonus_ref, ga_ref, ob_ref, gb_ref, yc_ref, xs_ref, z_ref, lnw_ref, lnb_ref,
                 rn_ref, dsk_ref, sn_ref, bda_ref, bdb_ref, wo_ref, out_ref):
    inv_hd = 1.0 / HEAD_DIM
    y = ya_ref[...]
    bda = bda_ref[...]
    mean = _dot_hi(y, bda) * inv_hd
    yd = y - mean
    var = _dot_hi(yd * yd, bda) * inv_hd
    ya = (yd * lax.rsqrt(var + GN_EPS) * lnw_ref[...] + lnb_ref[...] + bonus_ref[...]) * ga_ref[...]
    o = ob_ref[...]
    ms = _dot_hi(o * o, bdb_ref[...]) * inv_hd
    gb = gb_ref[...]
    yb = (gb * _sigmoid(gb)) * (o * lax.rsqrt(ms + RMS_EPS) * rn_ref[...])
    z = z_ref[...]
    yc = (yc_ref[...] + dsk_ref[...] * xs_ref[...]) * (z * _sigmoid(z))
    yc = _rmsnorm(yc, sn_ref[...], GATED_NORM_EPS)
    out_ref[...] = (x_ref[...] + _dot(ya, wo_ref[0:W_A, :]) + _dot(yb, wo_ref[W_A:W_A + W_B, :])
                    + _dot(yc, wo_ref[W_A + W_B:, :]))


def _post(x, ya, bonus, ga, ob, cb, yc, xs, cc, p):
    n = x.shape[0]
    va = _full_spec((1, W_A))
    return pl.pallas_call(
        _post_kernel,
        grid=(n // TM,),
        in_specs=[_row_spec(D_MODEL), _row_spec(W_A), _row_spec(W_A), _row_spec(W_A), _row_spec(W_B),
                  _row_spec(W_B, 3), _row_spec(W_C), _row_spec(W_C), _row_spec(W_C, 0), va, va,
                  _full_spec((1, W_B)), va, va, _full_spec((W_A, W_A)), _full_spec((W_B, W_B)),
                  _full_spec((D_MODEL, D_MODEL))],
        out_specs=_row_spec(D_MODEL),
        out_shape=jax.ShapeDtypeStruct((n, D_MODEL), F32),
        compiler_params=_params(("parallel",)),
        name="post_outproj",
    )(x, ya, bonus, ga, ob, cb, yc, xs, cc, p["ln_w"], p["ln_b"], p["ret_norm"], p["d_skip"], p["ssm_norm"],
      p["bd_a"], p["bd_b"], p["w_out"])


def _ffn_kernel(*refs, gated, final_norm, n_e):
    refs = list(refs)
    x_ref, nf_ref = refs[:2]
    pos = 2
    if gated:
        router_ref = refs[pos]
        pos += 1
    wg_ref, wu_ref, wd_ref = refs[pos:pos + 3]
    pos += 3
    if final_norm:
        nfin_ref = refs[pos]
        pos += 1
    out_ref = refs[pos]
    hb_s, acc_s = refs[pos + 1:pos + 3]
    if gated:
        gate_s = refs[pos + 3]
    e = pl.program_id(1)

    @pl.when(e == 0)
    def _():
        h = _rmsnorm(x_ref[...], nf_ref[...], RMS_EPS)
        hb_s[...] = h.astype(BF16)
        acc_s[...] = jnp.zeros_like(acc_s)
        if gated:
            lane = lax.broadcasted_iota(jnp.int32, (TM, LANES), 1)
            logits = jnp.where(lane < N_EXPERTS, _dot_hi(h, router_ref[...]), -jnp.inf)
            p = jnp.exp(logits - jnp.max(logits, axis=-1, keepdims=True))
            p = p / jnp.sum(p, axis=-1, keepdims=True)
            p1 = jnp.max(p, axis=-1, keepdims=True)
            i1 = jnp.min(jnp.where(p == p1, lane, LANES), axis=-1, keepdims=True)
            rest = jnp.where(lane == i1, -1.0, p)
            p2 = jnp.max(rest, axis=-1, keepdims=True)
            i2 = jnp.min(jnp.where(rest == p2, lane, LANES), axis=-1, keepdims=True)
            gate_s[...] = jnp.where(lane == i1, p1, jnp.where(lane == i2, p2, 0.0)) / (p1 + p2)

    hb = hb_s[...]
    g = jnp.dot(hb, wg_ref[0], preferred_element_type=F32)
    u = jnp.dot(hb, wu_ref[0], preferred_element_type=F32)
    o = _dot(g * _sigmoid(g) * u, wd_ref[0])
    if gated:
        lane = lax.broadcasted_iota(jnp.int32, (TM, LANES), 1)
        o = o * jnp.sum(jnp.where(lane == e, gate_s[...], 0.0), axis=-1, keepdims=True)
    acc_s[...] += o

    @pl.when(e == n_e - 1)
    def _():
        y = x_ref[...] + acc_s[...]
        if final_norm:
            y = _rmsnorm(y, nfin_ref[...], RMS_EPS)
        out_ref[...] = y


def _ffn(x, nf, wg, wu, wd, router=None, nfin=None):
    n = x.shape[0]
    n_e, _, dff = wg.shape
    gated = router is not None
    final_norm = nfin is not None
    vec = pl.BlockSpec((1, D_MODEL), lambda i, e: (0, 0))
    args = [x, nf]
    specs = [pl.BlockSpec((TM, D_MODEL), lambda i, e: (i, 0)), vec]
    if gated:
        args.append(router)
        specs.append(pl.BlockSpec((D_MODEL, LANES), lambda i, e: (0, 0)))
    args += [wg, wu, wd]
    specs += [pl.BlockSpec((1, D_MODEL, dff), lambda i, e: (e, 0, 0)),
              pl.BlockSpec((1, D_MODEL, dff), lambda i, e: (e, 0, 0)),
              pl.BlockSpec((1, dff, D_MODEL), lambda i, e: (e, 0, 0))]
    if final_norm:
        args.append(nfin)
        specs.append(vec)
    scratch = [pltpu.VMEM((TM, D_MODEL), BF16), pltpu.VMEM((TM, D_MODEL), F32)]
    if gated:
        scratch.append(pltpu.VMEM((TM, LANES), F32))
    return pl.pallas_call(
        functools.partial(_ffn_kernel, gated=gated, final_norm=final_norm, n_e=n_e),
        grid=(n // TM, n_e),
        in_specs=specs,
        out_specs=pl.BlockSpec((TM, D_MODEL), lambda i, e: (i, 0)),
        out_shape=jax.ShapeDtypeStruct((n, D_MODEL), F32),
        scratch_shapes=scratch,
        compiler_params=_params(("parallel", "arbitrary")),
        name="moe" if gated else "ffn",
    )(*args)


def _heads(x, b, h):
    n, w = x.shape
    return x.reshape(b, n // b, h, w // h)


def _lanes(x):
    b, t, h, f = x.shape
    return jnp.transpose(x, (1, 3, 0, 2)).reshape(t, f, b * h)


def _from_lanes(y, b, h):
    t, j, _ = y.shape
    return jnp.transpose(y.reshape(t, j, b, h), (2, 0, 3, 1)).reshape(b * t, h * j)


def _state_to_lanes(s, j_first):
    b, h = s.shape[:2]
    y = jnp.transpose(s, (3, 2, 0, 1) if j_first else (2, 3, 0, 1))
    return y.reshape(y.shape[0], y.shape[1], b * h)


def _state_from_lanes(s, b, h, j_first):
    ni, nj, _ = s.shape
    return jnp.transpose(s.reshape(ni, nj, b, h), (2, 3, 1, 0) if j_first else (2, 3, 0, 1))


def _rwkv_long(feats, b):
    r, dec, k2, v, nkk, beta = feats
    t = r.shape[0] // b
    kl = HEAD_DIM // K_SPLIT
    assert K_SPLIT * b * H_PAD == LANES

    def key_rows(x):
        return jnp.transpose(x.reshape(b, t, H_PAD, K_SPLIT, kl), (1, 4, 3, 0, 2)).reshape(t, kl, LANES)

    vv = jnp.transpose(v.reshape(b, t, H_PAD, HEAD_DIM), (1, 3, 0, 2))
    vv = jnp.broadcast_to(vv[:, :, None], (t, HEAD_DIM, K_SPLIT, b, H_PAD)).reshape(t, HEAD_DIM, LANES)
    nkk_next = jnp.concatenate([nkk[1:], jnp.zeros((1, nkk.shape[1]), F32)], axis=0)
    y, s = _rwkv_long_scan(key_rows(r), key_rows(dec), key_rows(k2), vv, key_rows(nkk_next), key_rows(beta))
    y = y[:, :, :b * H_PAD].reshape(t, HEAD_DIM, b, H_PAD)[..., :H_A]
    ya = jnp.transpose(y, (2, 0, 3, 1)).reshape(b * t, W_A)
    s = s.reshape(kl, HEAD_DIM, K_SPLIT, b, H_PAD)[..., :H_A]
    return ya, jnp.transpose(s, (3, 4, 1, 2, 0)).reshape(b, H_A, HEAD_DIM, HEAD_DIM)


def _rwkv_short(feats, b, s0):
    r, dec, k2, v, nkk, beta = [_lanes(_heads(f, b, H_A)) for f in feats]
    y, s = _scan("rwkv", r, dec, k2, v, _state_to_lanes(s0, True), n=nkk, a2=beta)
    return _from_lanes(y, b, H_A), _state_from_lanes(s, b, H_A, True)


def _ret_short(qh, kh, vb, b, s0):
    gam = np.tile(1.0 - 2.0 ** (-5.0 - np.arange(H_B, dtype=np.float64)), b).astype(np.float32)
    y, s = _scan("ret", _lanes(_heads(qh, b, H_B)), jnp.asarray(gam.reshape(1, 1, -1)), _lanes(_heads(kh, b, H_B)),
                 _lanes(_heads(vb, b, H_B)), _state_to_lanes(s0, False))
    return _from_lanes(y, b, H_B), _state_from_lanes(s, b, H_B, False)


def _ssd_short(xdt, bm, cm, dssm, b, s0):
    rep = H_C // N_GROUPS
    bh = jnp.repeat(_heads(bm, b, N_GROUPS), rep, axis=2)
    ch = jnp.repeat(_heads(cm, b, N_GROUPS), rep, axis=2)
    dec = _heads(dssm, b, 1)[:, :, 0, :H_C, None]
    y, s = _scan("ssd", _lanes(ch), _lanes(dec), _lanes(bh), _lanes(_heads(xdt, b, H_C)), _state_to_lanes(s0, True))
    return _from_lanes(y, b, H_C), _state_from_lanes(s, b, H_C, True)


def _block_diag_ones(width):
    idx = np.arange(width) // HEAD_DIM
    return jnp.asarray((idx[:, None] == idx[None, :]).astype(np.float32))


def _shifted(x, prev_rows, shift):
    b, t, c = x.shape
    p = prev_rows.shape[1]
    full = jnp.concatenate([prev_rows, x], axis=1)
    return full[:, p - shift:p - shift + t].reshape(b * t, c)


def _layer(x, b, p, st, rope, table_block):
    n = x.shape[0]
    t = n // b
    fresh = st is None
    tiles_per_seq = t // TM if fresh else 0
    ca, cb, cc = _in_proj(x, p["norm_mix"], p["w_in"])
    ca3 = ca.reshape(b, t, COLS_A)
    xbc_tail = cc.reshape(b, t, COLS_C_PAD)[:, -(CONV_W - 1):, W_C:W_C + CONV_DIM]
    if fresh:
        prev, shifted = None, None
        conv_new = xbc_tail
    else:
        prev = _shifted(ca3, st["shift"][:, None, :], 1)
        xbc = cc[:, W_C:W_C + CONV_DIM].reshape(b, t, CONV_DIM)
        shifted = [_shifted(xbc, st["conv"], j) for j in range(1, CONV_W)]
        conv_new = jnp.concatenate([st["conv"], xbc_tail], axis=1)[:, -(CONV_W - 1):]

    r, dec, k2, v, nkk, beta, ga, bonus = _rwkv_prep(ca, prev, p, tiles_per_seq)
    qh, kh = _ret_prep(cb, rope[0], rope[1], table_block)
    xdt, bm, cm, dssm, xs, da = _ssm_prep(cc, shifted, p, tiles_per_seq)

    feats = (r, dec, k2, v, nkk, beta)
    if fresh:
        ya, s_rwkv = _rwkv_long(feats, b)
        yb, s_full = _ret_chunk(qh, kh, cb, b, t)
        s_ret = jnp.stack([s_full[:, h * HEAD_DIM:(h + 1) * HEAD_DIM, h * HEAD_DIM:(h + 1) * HEAD_DIM]
                           for h in range(H_B)], axis=1)
        yc, s_ssm = _ssd_chunk(xdt, bm, cm, da, p["expand"], b, t)
        s_ssm = s_ssm.reshape(b, H_C, HEAD_DIM, N_STATE)
    else:
        ya, s_rwkv = _rwkv_short(feats, b, st["rwkv"])
        yb, s_ret = _ret_short(qh, kh, cb[:, 2 * W_B:3 * W_B], b, st["ret"])
        yc, s_ssm = _ssd_short(xdt, bm, cm, dssm, b, st["ssm"])

    x2 = _post(x, ya, bonus, ga, yb, cb, yc, xs, cc, p)
    x = _ffn(x2, p["norm_ffn"], p["wg"], p["wu"], p["wd"], router=p["router"], nfin=p["norm_final"])
    return x, (s_rwkv, ca3[:, -1], s_ret, s_ssm, conv_new)


def kernel(x_prompt, x_sample, state_rwkv, state_shift, state_ret, state_ssm, state_conv, norm_mix, w_in, rwkv_mu, rwkv_w0, rwkv_w_up, rwkv_a0, rwkv_a_up, rwkv_g_up, rwkv_k_k, rwkv_k_a, rwkv_r_k, rwkv_ln_w, rwkv_ln_b, ret_norm, ssm_conv_w, ssm_conv_b, ssm_dt_bias, ssm_a_log, ssm_d, ssm_norm, w_out, norm_ffn, ffn_w_gate, ffn_w_up, ffn_w_down, moe_router, moe_w_gate, moe_w_up, moe_w_down, norm_final):
    bp, tp, _ = x_prompt.shape
    bs, ts, _ = x_sample.shape
    depth = w_in.shape[0]
    assert tp % TM == 0 and tp % LC == 0 and (bs * ts) % TM == 0 and TM % ts == 0

    rope = _rope_tables(tp, ts)
    bd_a, bd_b = _block_diag_ones(W_A), _block_diag_ones(W_B)
    expand = np.zeros((LANES, W_C), np.float32)
    expand[np.arange(W_C) // HEAD_DIM, np.arange(W_C)] = 1.0
    expand = jnp.asarray(expand)
    row = lambda v: v.reshape(1, -1)
    pad_l = lambda v: jnp.pad(v, (0, LANES - v.shape[0])).reshape(1, LANES)

    xp = x_prompt.reshape(bp * tp, D_MODEL)
    xs = x_sample.reshape(bs * ts, D_MODEL)
    new_p, new_s = [], []
    for i in range(depth):
        j = i // 2
        p = dict(
            norm_mix=row(norm_mix[i]),
            w_in=jnp.pad(w_in[i], ((0, 0), (0, COLS_C_PAD - COLS_C))).astype(BF16),
            mu=row(rwkv_mu[i]), w0=row(rwkv_w0[i]), a0=row(rwkv_a0[i]), k_k=row(rwkv_k_k[i]),
            k_a=row(rwkv_k_a[i]), r_k=row(rwkv_r_k[i]),
            w_up=jnp.pad(rwkv_w_up[i], ((0, AAA_LORA), (0, 0))).astype(BF16),
            a_up=jnp.pad(rwkv_a_up[i], ((DECAY_LORA, 0), (0, 0))).astype(BF16),
            g_up=rwkv_g_up[i].astype(BF16), bd_a=bd_a, bd_b=bd_b,
            conv_w=ssm_conv_w[i], conv_b=row(ssm_conv_b[i]), dt_bias=pad_l(ssm_dt_bias[i]),
            a_log=pad_l(ssm_a_log[i]), expand=expand,
            ln_w=row(rwkv_ln_w[i]), ln_b=row(rwkv_ln_b[i]), ret_norm=row(ret_norm[i]),
            d_skip=row(jnp.repeat(ssm_d[i], HEAD_DIM)), ssm_norm=row(ssm_norm[i]), w_out=w_out[i].astype(BF16),
            norm_ffn=row(norm_ffn[i]), norm_final=row(norm_final) if i == depth - 1 else None)
        if i % 2 == 0:
            dff = ffn_w_gate.shape[-1] // 2
            p.update(router=None,
                     wg=ffn_w_gate[j].reshape(D_MODEL, 2, dff).transpose(1, 0, 2).astype(BF16),
                     wu=ffn_w_up[j].reshape(D_MODEL, 2, dff).transpose(1, 0, 2).astype(BF16),
                     wd=ffn_w_down[j].reshape(2, dff, D_MODEL).astype(BF16))
        else:
            p.update(router=jnp.pad(moe_router[j], ((0, 0), (0, LANES - N_EXPERTS))),
                     wg=moe_w_gate[j].astype(BF16), wu=moe_w_up[j].astype(BF16), wd=moe_w_down[j].astype(BF16))

        xp, st_p = _layer(xp, bp, p, None, rope, lambda t: t % (tp // TM))
        st = dict(rwkv=state_rwkv[i], shift=state_shift[i], ret=state_ret[i], ssm=state_ssm[i],
                  conv=state_conv[i])
        xs, st_s = _layer(xs, bs, p, st, rope, lambda t: tp // TM)
        new_p.append(st_p)
        new_s.append(st_s)

    stack = lambda sts: tuple(jnp.stack(s) for s in zip(*sts))
    return (xp.reshape(bp, tp, D_MODEL), xs.reshape(bs, ts, D_MODEL)) + stack(new_p) + stack(new_s)
```

```python
import functools
import math

import numpy as np
import jax
import jax.numpy as jnp
from jax import lax
from jax.experimental import pallas as pl
from jax.experimental.pallas import tpu as pltpu

F32 = jnp.float32
BF16 = jnp.bfloat16
HIGHEST = lax.Precision.HIGHEST

LANES = 128
SUBLANES = 8
VMEM_LIMIT = 56 * 1024 * 1024

D_MODEL = 1024
HEAD_DIM = 64
H_A, H_B, H_C = 6, 4, 6
W_A, W_B, W_C = H_A * HEAD_DIM, H_B * HEAD_DIM, H_C * HEAD_DIM
DECAY_LORA, AAA_LORA, GATE_LORA = 64, 64, 128
COLS_A = 3 * W_A + DECAY_LORA + AAA_LORA + GATE_LORA
COLS_B = 4 * W_B
N_STATE, N_GROUPS, CONV_W = 128, 2, 4
CONV_DIM = W_C + 2 * N_GROUPS * N_STATE
COLS_C = W_C + CONV_DIM + H_C
COLS_C_PAD = 1408
ROPE_BASE = 10000.0
RMS_EPS = 1e-6
GN_EPS = 64e-5
GATED_NORM_EPS = 1e-5
N_EXPERTS = 8
PAST_LEN = 16384

TM = 512
LC = 256
H_PAD = 8
K_SPLIT = 2


def _dot(a, b):
    return jnp.dot(a.astype(BF16), b.astype(BF16), preferred_element_type=F32)


def _dot_hi(a, b):
    return jnp.dot(a, b, precision=HIGHEST, preferred_element_type=F32)


def _sigmoid(x):
    return 1.0 / (1.0 + jnp.exp(-x))


def _softplus(x):
    return jnp.maximum(x, 0.0) + jnp.log1p(jnp.exp(-jnp.abs(x)))


def _rmsnorm(x, g, eps):
    return x * lax.rsqrt(jnp.mean(x * x, axis=-1, keepdims=True) + eps) * g


def _params(sem):
    return pltpu.CompilerParams(dimension_semantics=sem, vmem_limit_bytes=VMEM_LIMIT)


def _row_spec(width, col=0):
    return pl.BlockSpec((TM, width), lambda i, c=col: (i, c))


def _before_spec(width):
    return pl.BlockSpec((SUBLANES, width), lambda i: (jnp.maximum(i * (TM // SUBLANES) - 1, 0), 0))


def _full_spec(shape):
    nd = len(shape)
    return pl.BlockSpec(shape, lambda i, n=nd: (0,) * n)


def _shift_rows(x, before, j, first):
    rolled = pltpu.roll(x, j, 0)
    prev = jnp.where(first, 0.0, pltpu.roll(before, j, 0))
    row = lax.broadcasted_iota(jnp.int32, prev.shape, 0)
    top = jnp.where(row < j, prev, rolled[0:SUBLANES])
    return jnp.concatenate([top, rolled[SUBLANES:]], axis=0)


def _in_proj_kernel(x_ref, g_ref, w_ref, oa_ref, ob_ref, oc_ref):
    h = _rmsnorm(x_ref[...], g_ref[...], RMS_EPS).astype(BF16)
    oa_ref[...] = jnp.dot(h, w_ref[:, 0:COLS_A], preferred_element_type=F32)
    ob_ref[...] = jnp.dot(h, w_ref[:, COLS_A:COLS_A + COLS_B], preferred_element_type=F32)
    oc_ref[...] = jnp.dot(h, w_ref[:, COLS_A + COLS_B:], preferred_element_type=F32)


def _in_proj(x, g, w):
    n = x.shape[0]
    wtot = w.shape[1]
    return pl.pallas_call(
        _in_proj_kernel,
        grid=(n // TM,),
        in_specs=[_row_spec(D_MODEL), _full_spec((1, D_MODEL)), _full_spec((D_MODEL, wtot))],
        out_specs=[_row_spec(COLS_A), _row_spec(COLS_B), _row_spec(COLS_C_PAD)],
        out_shape=[jax.ShapeDtypeStruct((n, COLS_A), F32), jax.ShapeDtypeStruct((n, COLS_B), F32),
                   jax.ShapeDtypeStruct((n, COLS_C_PAD), F32)],
        compiler_params=_params(("parallel",)),
        name="in_proj",
    )(x, g, w)


def _rwkv_prep_kernel(c_ref, p_ref, mu_ref, w0_ref, a0_ref, kk_ref, ka_ref, rk_ref, wup_ref, aup_ref,
                      gup_ref, bd_ref, r_o, d_o, k_o, v_o, n_o, b_o, g_o, bonus_o, *, tiles_per_seq):
    c = c_ref[...]
    if tiles_per_seq:
        prev = _shift_rows(c, p_ref[...], 1, pl.program_id(0) % tiles_per_seq == 0)
    else:
        prev = p_ref[...]
    xm = c + (prev - c) * mu_ref[...]
    r = xm[:, 0:W_A]
    k = xm[:, W_A:2 * W_A]
    v = xm[:, 2 * W_A:3 * W_A]
    lora = xm[:, 3 * W_A:3 * W_A + DECAY_LORA + AAA_LORA]
    gd = xm[:, 3 * W_A + DECAY_LORA + AAA_LORA:]
    w = w0_ref[...] + _dot(jnp.tanh(lora), wup_ref[...])
    decay = jnp.exp(-math.exp(-0.5) * _sigmoid(w))
    a = _sigmoid(a0_ref[...] + _dot(lora, aup_ref[...]))
    g = _dot(_sigmoid(gd), gup_ref[...])
    bd = bd_ref[...]
    kk = k * kk_ref[...]
    kk = kk / jnp.maximum(jnp.sqrt(_dot_hi(kk * kk, bd)), 1e-12)
    k2 = k * (1.0 + (a - 1.0) * ka_ref[...])
    for o_ref, val in ((r_o, r), (d_o, decay), (k_o, k2), (v_o, v), (n_o, -kk), (b_o, kk * a)):
        o_ref[:, 0:W_A] = val
        if o_ref.shape[1] > W_A:
            o_ref[:, W_A:] = jnp.zeros((TM, o_ref.shape[1] - W_A), F32)
    g_o[...] = g
    bonus_o[...] = _dot_hi(r * k2 * rk_ref[...], bd) * v


def _rwkv_prep(ca, prev, p, tiles_per_seq):
    n = ca.shape[0]
    vec = _full_spec((1, W_A))
    lora_spec = _full_spec((DECAY_LORA + AAA_LORA, W_A))
    p_spec = _before_spec(COLS_A) if tiles_per_seq else _row_spec(COLS_A)
    w_scan = H_PAD * HEAD_DIM if tiles_per_seq else W_A
    widths = [w_scan] * 6 + [W_A] * 2
    return pl.pallas_call(
        functools.partial(_rwkv_prep_kernel, tiles_per_seq=tiles_per_seq),
        grid=(n // TM,),
        in_specs=[_row_spec(COLS_A), p_spec, _full_spec((1, COLS_A)), vec, vec, vec, vec, vec,
                  lora_spec, lora_spec, _full_spec((GATE_LORA, W_A)), _full_spec((W_A, W_A))],
        out_specs=[_row_spec(w) for w in widths],
        out_shape=[jax.ShapeDtypeStruct((n, w), F32) for w in widths],
        compiler_params=_params(("parallel",)),
        name="rwkv_prep",
    )(ca, ca if tiles_per_seq else prev, p["mu"], p["w0"], p["a0"], p["k_k"], p["k_a"], p["r_k"],
      p["w_up"], p["a_up"], p["g_up"], p["bd_a"])


def _rope_kernel(ang_ref, cos_o, sin_o):
    ang = ang_ref[...]
    lane = lax.broadcasted_iota(jnp.int32, ang.shape, 1)
    cos_o[...] = jnp.cos(ang)
    sin_o[...] = jnp.where((lane % HEAD_DIM) < (HEAD_DIM // 2), -jnp.sin(ang), jnp.sin(ang))


def _rope_tables(t_prompt, t_sample):
    theta = 1.0 / (ROPE_BASE ** jnp.linspace(0.0, 1.0, HEAD_DIM // 2, dtype=F32))
    pos = jnp.concatenate([jnp.arange(t_prompt, dtype=F32), PAST_LEN + (jnp.arange(TM) % t_sample).astype(F32)])
    ang = jnp.tile(pos[:, None] * theta[None, :], (1, W_B // (HEAD_DIM // 2)))
    n = ang.shape[0]
    return pl.pallas_call(
        _rope_kernel,
        grid=(n // TM,),
        in_specs=[_row_spec(W_B)],
        out_specs=[_row_spec(W_B)] * 2,
        out_shape=[jax.ShapeDtypeStruct((n, W_B), F32)] * 2,
        compiler_params=_params(("parallel",)),
        name="rope_tables",
    )(ang)


def _ret_prep_kernel(q_ref, k_ref, cos_ref, sin_ref, q_o, k_o):
    cos = cos_ref[...]
    sin = sin_ref[...]
    lane = lax.broadcasted_iota(jnp.int32, (TM, W_B), 1)
    first_half = (lane % HEAD_DIM) < (HEAD_DIM // 2)

    def rope(x):
        partner = jnp.where(first_half, pltpu.roll(x, W_B - HEAD_DIM // 2, 1), pltpu.roll(x, HEAD_DIM // 2, 1))
        return x * cos + partner * sin

    q_o[...] = rope(q_ref[...])
    k_o[...] = rope(k_ref[...]) * (HEAD_DIM ** -0.5)


def _ret_prep(cb, cos, sin, table_block):
    n = cb.shape[0]
    tab = pl.BlockSpec((TM, W_B), lambda i: (table_block(i), 0))
    return pl.pallas_call(
        _ret_prep_kernel,
        grid=(n // TM,),
        in_specs=[_row_spec(W_B, 0), _row_spec(W_B, 1), tab, tab],
        out_specs=[_row_spec(W_B)] * 2,
        out_shape=[jax.ShapeDtypeStruct((n, W_B), F32)] * 2,
        compiler_params=_params(("parallel",)),
        name="ret_prep",
    )(cb, cb, cos, sin)


def _ssm_prep_kernel(*refs, tiles_per_seq):
    n_shift_refs = 1 if tiles_per_seq else CONV_W - 1
    cc_ref = refs[0]
    shift_refs = refs[1:1 + n_shift_refs]
    cw_ref, cb_ref, dtb_ref, alog_ref, ex_ref, xdt_o, b_o, c_o, dec_o, xs_o, da_o = refs[1 + n_shift_refs:]
    cc = cc_ref[...]
    x0 = cc[:, W_C:W_C + CONV_DIM]
    if tiles_per_seq:
        before = shift_refs[0][:, W_C:W_C + CONV_DIM]
        first = pl.program_id(0) % tiles_per_seq == 0
        xs_prev = [_shift_rows(x0, before, j, first) for j in range(1, CONV_W)]
    else:
        xs_prev = [r[...] for r in shift_refs]
    cw = cw_ref[...]
    conv = x0 * cw[CONV_W - 1:CONV_W, :] + cb_ref[...]
    for j in range(1, CONV_W):
        conv = conv + xs_prev[j - 1] * cw[CONV_W - 1 - j:CONV_W - j, :]
    act = conv * _sigmoid(conv)
    xs = act[:, 0:W_C]
    dt = _softplus(cc[:, W_C + CONV_DIM:] + dtb_ref[...])
    da = dt * (-jnp.exp(alog_ref[...]))
    da_o[...] = da
    dec_o[...] = jnp.exp(da)
    xdt_o[...] = xs * _dot_hi(dt, ex_ref[...])
    b_o[...] = act[:, W_C:W_C + N_GROUPS * N_STATE]
    c_o[...] = act[:, W_C + N_GROUPS * N_STATE:]
    xs_o[...] = xs


def _ssm_prep(cc, shifted, p, tiles_per_seq):
    n = cc.shape[0]
    gn = N_GROUPS * N_STATE
    if tiles_per_seq:
        shift_args, shift_specs = [cc], [_before_spec(COLS_C_PAD)]
    else:
        shift_args, shift_specs = list(shifted), [_row_spec(CONV_DIM)] * (CONV_W - 1)
    return pl.pallas_call(
        functools.partial(_ssm_prep_kernel, tiles_per_seq=tiles_per_seq),
        grid=(n // TM,),
        in_specs=[_row_spec(COLS_C_PAD)] + shift_specs + [
            _full_spec((CONV_W, CONV_DIM)), _full_spec((1, CONV_DIM)), _full_spec((1, LANES)),
            _full_spec((1, LANES)), _full_spec((LANES, W_C))],
        out_specs=[_row_spec(W_C), _row_spec(gn), _row_spec(gn), _row_spec(LANES), _row_spec(W_C),
                   _row_spec(LANES)],
        out_shape=[jax.ShapeDtypeStruct((n, W_C), F32), jax.ShapeDtypeStruct((n, gn), F32),
                   jax.ShapeDtypeStruct((n, gn), F32), jax.ShapeDtypeStruct((n, LANES), F32),
                   jax.ShapeDtypeStruct((n, W_C), F32), jax.ShapeDtypeStruct((n, LANES), F32)],
        compiler_params=_params(("parallel",)),
        name="ssm_prep",
    )(cc, *shift_args, p["conv_w"], p["conv_b"], p["dt_bias"], p["a_log"], p["expand"])


def _scan_kernel(*refs, mode, ni, tt_len, n_tt):
    if mode == "rwkv":
        q_ref, d_ref, a_ref, b_ref, n_ref, a2_ref, s0_ref, y_ref, so_ref, st = refs
    else:
        q_ref, d_ref, a_ref, b_ref, s0_ref, y_ref, so_ref, st = refs
    tt = pl.program_id(1)

    @pl.when(tt == 0)
    def _():
        st[...] = s0_ref[...]

    def row(ref, t, i):
        return ref[t, pl.ds(i, 1), :]

    def step(t, carry):
        bv = b_ref[t]
        if mode == "rwkv":
            sa_parts = [jnp.zeros_like(bv), jnp.zeros_like(bv)]
            for i in range(ni):
                sa_parts[i % 2] = sa_parts[i % 2] + st[i] * row(n_ref, t, i)
            sa = sa_parts[0] + sa_parts[1]
        elif mode == "ssd":
            d = d_ref[t]
        else:
            d = d_ref[0]
        y_parts = [jnp.zeros_like(bv), jnp.zeros_like(bv)]
        for i in range(ni):
            if mode == "rwkv":
                s = st[i] * row(d_ref, t, i) + row(a_ref, t, i) * bv + row(a2_ref, t, i) * sa
            else:
                s = st[i] * d + row(a_ref, t, i) * bv
            st[i] = s
            y_parts[i % 2] = y_parts[i % 2] + s * row(q_ref, t, i)
        y_ref[t] = y_parts[0] + y_parts[1]
        return carry

    lax.fori_loop(0, tt_len, step, 0)

    @pl.when(tt == n_tt - 1)
    def _():
        so_ref[...] = st[...]


def _scan(mode, q, d, a, b, s0, n=None, a2=None, tt_len=32):
    t_len, ni, lanes = q.shape
    nj = b.shape[1]
    tt_len = min(tt_len, t_len)
    n_tt = t_len // tt_len
    groups = lanes // LANES
    i_spec = pl.BlockSpec((tt_len, ni, LANES), lambda g, t: (t, 0, g))
    j_spec = pl.BlockSpec((tt_len, nj, LANES), lambda g, t: (t, 0, g))
    s_spec = pl.BlockSpec((ni, nj, LANES), lambda g, t: (0, 0, g))
    if mode == "rwkv":
        d_spec = i_spec
    elif mode == "ssd":
        d_spec = pl.BlockSpec((tt_len, 1, LANES), lambda g, t: (t, 0, g))
    else:
        d_spec = pl.BlockSpec((1, 1, LANES), lambda g, t: (0, 0, g))
    args = [q, d, a, b]
    specs = [i_spec, d_spec, i_spec, j_spec]
    if mode == "rwkv":
        args += [n, a2]
        specs += [i_spec, i_spec]
    args.append(s0)
    specs.append(s_spec)
    return pl.pallas_call(
        functools.partial(_scan_kernel, mode=mode, ni=ni, tt_len=tt_len, n_tt=n_tt),
        grid=(groups, n_tt),
        in_specs=specs,
        out_specs=[j_spec, s_spec],
        out_shape=[jax.ShapeDtypeStruct((t_len, nj, lanes), F32), jax.ShapeDtypeStruct((ni, nj, lanes), F32)],
        scratch_shapes=[pltpu.VMEM((ni, nj, LANES), F32)],
        compiler_params=_params(("parallel", "arbitrary")),
        name="scan_" + mode,
    )(*args)


def _rwkv_long_kernel(q_ref, d_ref, a_ref, b_ref, n1_ref, a2_ref, y_ref, so_ref, st, sa_s, *, ni, tt_len, n_tt):
    tt = pl.program_id(1)

    @pl.when(tt == 0)
    def _():
        st[...] = jnp.zeros_like(st)
        sa_s[...] = jnp.zeros_like(sa_s)

    def row(ref, t, i):
        return ref[t, pl.ds(i, 1), :]

    def fold(x):
        lane = lax.broadcasted_iota(jnp.int32, x.shape, 1)
        return x + jnp.where((lane & H_PAD) == 0, pltpu.roll(x, LANES - H_PAD, 1), pltpu.roll(x, H_PAD, 1))

    def step(t, sa):
        bv = b_ref[t]
        a, a2, q, n1 = a_ref[t], a2_ref[t], q_ref[t], n1_ref[t]
        aq, a2q, an, a2n = [fold(jnp.sum(u * w, axis=0, keepdims=True))
                            for u, w in ((a, q), (a2, q), (a, n1), (a2, n1))]
        y0 = [jnp.zeros_like(bv), jnp.zeros_like(bv)]
        n0 = [jnp.zeros_like(bv), jnp.zeros_like(bv)]
        for i in range(ni):
            sd = st[i] * row(d_ref, t, i)
            st[i] = sd
            y0[i % 2] = y0[i % 2] + sd * row(q_ref, t, i)
            n0[i % 2] = n0[i % 2] + sd * row(n1_ref, t, i)
        y_ref[t] = fold(y0[0] + y0[1]) + bv * aq + sa * a2q
        sa_next = fold(n0[0] + n0[1]) + bv * an + sa * a2n
        for i in range(ni):
            st[i] = st[i] + row(a_ref, t, i) * bv + row(a2_ref, t, i) * sa
        return sa_next

    sa_s[...] = lax.fori_loop(0, tt_len, step, sa_s[...])

    @pl.when(tt == n_tt - 1)
    def _():
        so_ref[...] = st[...]


def _rwkv_long_scan(q, d, a, b, n1, a2, tt_len=64):
    t_len, ni, _ = q.shape
    nj = b.shape[1]
    n_tt = t_len // tt_len
    i_spec = pl.BlockSpec((tt_len, ni, LANES), lambda g, t: (t, 0, 0))
    j_spec = pl.BlockSpec((tt_len, nj, LANES), lambda g, t: (t, 0, 0))
    s_spec = pl.BlockSpec((ni, nj, LANES), lambda g, t: (0, 0, 0))
    return pl.pallas_call(
        functools.partial(_rwkv_long_kernel, ni=ni, tt_len=tt_len, n_tt=n_tt),
        grid=(1, n_tt),
        in_specs=[i_spec, i_spec, i_spec, j_spec, i_spec, i_spec],
        out_specs=[j_spec, s_spec],
        out_shape=[jax.ShapeDtypeStruct((t_len, nj, LANES), F32), jax.ShapeDtypeStruct((ni, nj, LANES), F32)],
        scratch_shapes=[pltpu.VMEM((ni, nj, LANES), F32), pltpu.VMEM((nj, LANES), F32)],
        compiler_params=_params(("parallel", "arbitrary")),
        name="scan_rwkv_long",
    )(q, d, a, b, n1, a2)


TR = 128
PLANES_PER_B = K_SPLIT * H_PAD


def _to_lanes_kernel(*refs, n_ops, rows, per_half):
    x_refs, o_refs, y2d = refs[:n_ops], refs[n_ops:2 * n_ops], refs[2 * n_ops]
    n_b = x_refs[0].shape[0]
    for x_ref, o_ref in zip(x_refs, o_refs):
        for b in range(n_b):
            xt = x_ref[b].T
            for h in range(H_PAD):
                for s in range(K_SPLIT):
                    lane = b * PLANES_PER_B + s * H_PAD + h
                    f0 = h * HEAD_DIM + (s * rows if per_half else 0)
                    y2d[pl.ds(lane, rows, stride=LANES), :] = xt[f0:f0 + rows, :]
        for r in range(rows):
            o_ref[pl.ds(r, TR, stride=rows), :] = y2d[r * LANES:(r + 1) * LANES, :].T


def _long_to_lanes(xs, b, rows, per_half):
    n_ops = len(xs)
    t = xs[0].shape[0] // b
    width = H_PAD * HEAD_DIM
    outs = pl.pallas_call(
        functools.partial(_to_lanes_kernel, n_ops=n_ops, rows=rows, per_half=per_half),
        grid=(t // TR,),
        in_specs=[pl.BlockSpec((b, TR, width), lambda i: (0, i, 0))] * n_ops,
        out_specs=[pl.BlockSpec((TR * rows, LANES), lambda i: (i, 0))] * n_ops,
        out_shape=[jax.ShapeDtypeStruct((t * rows, LANES), F32)] * n_ops,
        scratch_shapes=[pltpu.VMEM((rows * LANES, TR), F32)],
        compiler_params=_params(("parallel",)),
        name="to_lanes",
    )(*[x.reshape(b, t, width) for x in xs])
    return [o.reshape(t, rows, LANES) for o in outs]


def _from_lanes_kernel(y_ref, o_ref, z2d):
    for v in range(HEAD_DIM):
        z2d[pl.ds(v, LANES, stride=HEAD_DIM), :] = y_ref[pl.ds(v, TR, stride=HEAD_DIM), :].T
    for b in range(o_ref.shape[0]):
        r0 = b * PLANES_PER_B * HEAD_DIM
        o_ref[b] = z2d[r0:r0 + W_A, :].T


def _long_from_lanes(y, b):
    t = y.shape[0]
    out = pl.pallas_call(
        _from_lanes_kernel,
        grid=(t // TR,),
        in_specs=[pl.BlockSpec((TR * HEAD_DIM, LANES), lambda i: (i, 0))],
        out_specs=pl.BlockSpec((b, TR, W_A), lambda i: (0, i, 0)),
        out_shape=jax.ShapeDtypeStruct((b, t, W_A), F32),
        scratch_shapes=[pltpu.VMEM((LANES * HEAD_DIM, TR), F32)],
        compiler_params=_params(("parallel",)),
        name="from_lanes",
    )(y.reshape(t * HEAD_DIM, LANES))
    return out.reshape(b * t, W_A)


def _ret_chunk_kernel(q_ref, k_ref, v_ref, y_ref, s_ref, st):
    c = pl.program_id(1)

    @pl.when(c == 0)
    def _():
        st[...] = jnp.zeros_like(st)

    q = q_ref[...]
    k = k_ref[...]
    vb = v_ref[...].astype(BF16)
    kb = k.astype(BF16)
    row = lax.broadcasted_iota(jnp.int32, (LC, W_B), 0)
    head = lax.broadcasted_iota(jnp.int32, (LC, W_B), 1) // HEAD_DIM
    log_g = [math.log(1.0 - 2.0 ** (-5.0 - h)) for h in range(H_B)]
    lg = jnp.full((LC, W_B), log_g[0], F32)
    for h in range(1, H_B):
        lg = jnp.where(head == h, log_g[h], lg)
    rowf = row.astype(F32)
    diff = (lax.broadcasted_iota(jnp.int32, (LC, LC), 0) - lax.broadcasted_iota(jnp.int32, (LC, LC), 1))
    causal = diff >= 0
    difff = jnp.maximum(diff, 0).astype(F32)

    out = jnp.dot((q * jnp.exp(lg * (rowf + 1.0))).astype(BF16), st[...].astype(BF16), preferred_element_type=F32)
    for h in range(H_B):
        qm = jnp.where(head == h, q, 0.0).astype(BF16)
        s = lax.dot_general(qm, kb, (((1,), (1,)), ((), ())), preferred_element_type=F32)
        p = jnp.where(causal, s * jnp.exp(log_g[h] * difff), 0.0).astype(BF16)
        out = out + jnp.where(head == h, jnp.dot(p, vb, preferred_element_type=F32), 0.0)
    y_ref[...] = out

    kt = (k * jnp.exp(lg * (LC - 1.0 - rowf))).T.astype(BF16)
    kv = jnp.dot(kt, vb, preferred_element_type=F32)
    r2 = lax.broadcasted_iota(jnp.int32, (W_B, W_B), 0) // HEAD_DIM
    c2 = lax.broadcasted_iota(jnp.int32, (W_B, W_B), 1) // HEAD_DIM
    cdec = jnp.full((W_B, W_B), math.exp(log_g[0] * LC), F32)
    for h in range(1, H_B):
        cdec = jnp.where(r2 == h, math.exp(log_g[h] * LC), cdec)
    st[...] = st[...] * cdec + jnp.where(r2 == c2, kv, 0.0)
    s_ref[0] = st[...]


def _ret_chunk(q, k, cb, n_seq, t_len):
    n_c = t_len // LC
    rows = lambda col: pl.BlockSpec((LC, W_B), lambda b, c, col=col: (b * n_c + c, col))
    return pl.pallas_call(
        _ret_chunk_kernel,
        grid=(n_seq, n_c),
        in_specs=[rows(0), rows(0), rows(2)],
        out_specs=[rows(0), pl.BlockSpec((1, W_B, W_B), lambda b, c: (b, 0, 0))],
        out_shape=[jax.ShapeDtypeStruct((n_seq * t_len, W_B), F32), jax.ShapeDtypeStruct((n_seq, W_B, W_B), F32)],
        scratch_shapes=[pltpu.VMEM((W_B, W_B), F32)],
        compiler_params=_params(("parallel", "arbitrary")),
        name="ret_chunk",
    )(q, k, cb)


def _ssd_chunk_kernel(x_ref, b_ref, c_ref, da_ref, ex_ref, y_ref, h_ref, st):
    ci = pl.program_id(1)

    @pl.when(ci == 0)
    def _():
        st[...] = jnp.zeros_like(st)

    gn = N_STATE
    half = W_C // N_GROUPS
    ex = ex_ref[...]
    x = x_ref[...]
    xb = x.astype(BF16)
    ri = lax.broadcasted_iota(jnp.int32, (LC, LC), 0)
    cj = lax.broadcasted_iota(jnp.int32, (LC, LC), 1)
    causal = ri >= cj
    cum = _dot_hi(causal.astype(F32), da_ref[...])
    cum_e = _dot_hi(cum, ex)
    last_e = cum_e[LC - 1:LC, :]
    cum_t = cum.T
    head = lax.broadcasted_iota(jnp.int32, (LC, W_C), 1) // HEAD_DIM
    bg = [b_ref[:, g * gn:(g + 1) * gn].astype(BF16) for g in range(N_GROUPS)]
    cg = [c_ref[:, g * gn:(g + 1) * gn].astype(BF16) for g in range(N_GROUPS)]
    cb = [lax.dot_general(cg[g], bg[g], (((1,), (1,)), ((), ())), preferred_element_type=F32)
          for g in range(N_GROUPS)]

    hb = st[...].astype(BF16)
    ys = [lax.dot_general(cg[g], hb, (((1,), (1,)), ((), ())), preferred_element_type=F32)
          for g in range(N_GROUPS)]
    y = jnp.where(head < H_C // N_GROUPS, ys[0], ys[1]) * jnp.exp(cum_e)
    for h in range(H_C):
        seg = cum[:, h:h + 1] - cum_t[h:h + 1, :]
        p = (cb[h // (H_C // N_GROUPS)] * jnp.exp(jnp.where(causal, seg, -jnp.inf))).astype(BF16)
        y = y + jnp.where(head == h, jnp.dot(p, xb, preferred_element_type=F32), 0.0)
    y_ref[...] = y

    xt = (x * jnp.exp(last_e - cum_e)).T.astype(BF16)
    upd = [jnp.dot(xt, bg[g], preferred_element_type=F32) for g in range(N_GROUPS)]
    rowi = lax.broadcasted_iota(jnp.int32, (W_C, gn), 0)
    sel = lax.broadcasted_iota(jnp.int32, (W_C, LANES), 1) == lax.broadcasted_iota(jnp.int32, (W_C, LANES), 0) // HEAD_DIM
    tot = jnp.sum(jnp.where(sel, cum[LC - 1:LC, :], 0.0), axis=1, keepdims=True)
    st[...] = st[...] * jnp.exp(tot) + jnp.where(rowi < half, upd[0], upd[1])
    h_ref[0] = st[...]


def _ssd_chunk(xdt, bm, cm, da, ex, n_seq, t_len):
    n_c = t_len // LC
    gn = N_GROUPS * N_STATE
    rows = lambda w: pl.BlockSpec((LC, w), lambda b, c: (b * n_c + c, 0))
    return pl.pallas_call(
        _ssd_chunk_kernel,
        grid=(n_seq, n_c),
        in_specs=[rows(W_C), rows(gn), rows(gn), rows(LANES), pl.BlockSpec((LANES, W_C), lambda b, c: (0, 0))],
        out_specs=[rows(W_C), pl.BlockSpec((1, W_C, N_STATE), lambda b, c: (b, 0, 0))],
        out_shape=[jax.ShapeDtypeStruct((n_seq * t_len, W_C), F32),
                   jax.ShapeDtypeStruct((n_seq, W_C, N_STATE), F32)],
        scratch_shapes=[pltpu.VMEM((W_C, N_STATE), F32)],
        compiler_params=_params(("parallel", "arbitrary")),
        name="ssd_chunk",
    )(xdt, bm, cm, da, ex)


def _post_kernel(x_ref, ya_ref, bonus_ref, ga_ref, ob_ref, gb_ref, yc_ref, xs_ref, z_ref, lnw_ref, lnb_ref,
                 rn_ref, dsk_ref, sn_ref, bda_ref, bdb_ref, wo_ref, out_ref):
    inv_hd = 1.0 / HEAD_DIM
    y = ya_ref[...]
    bda = bda_ref[...]
    mean = _dot_hi(y, bda) * inv_hd
    yd = y - mean
    var = _dot_hi(yd * yd, bda) * inv_hd
    ya = (yd * lax.rsqrt(var + GN_EPS) * lnw_ref[...] + lnb_ref[...] + bonus_ref[...]) * ga_ref[...]
    o = ob_ref[...]
    ms = _dot_hi(o * o, bdb_ref[...]) * inv_hd
    gb = gb_ref[...]
    yb = (gb * _sigmoid(gb)) * (o * lax.rsqrt(ms + RMS_EPS) * rn_ref[...])
    z = z_ref[...]
    yc = (yc_ref[...] + dsk_ref[...] * xs_ref[...]) * (z * _sigmoid(z))
    yc = _rmsnorm(yc, sn_ref[...], GATED_NORM_EPS)
    out_ref[...] = (x_ref[...] + _dot(ya, wo_ref[0:W_A, :]) + _dot(yb, wo_ref[W_A:W_A + W_B, :])
                    + _dot(yc, wo_ref[W_A + W_B:, :]))


def _post(x, ya, bonus, ga, ob, cb, yc, xs, cc, p):
    n = x.shape[0]
    va = _full_spec((1, W_A))
    return pl.pallas_call(
        _post_kernel,
        grid=(n // TM,),
        in_specs=[_row_spec(D_MODEL), _row_spec(W_A), _row_spec(W_A), _row_spec(W_A), _row_spec(W_B),
                  _row_spec(W_B, 3), _row_spec(W_C), _row_spec(W_C), _row_spec(W_C, 0), va, va,
                  _full_spec((1, W_B)), va, va, _full_spec((W_A, W_A)), _full_spec((W_B, W_B)),
                  _full_spec((D_MODEL, D_MODEL))],
        out_specs=_row_spec(D_MODEL),
        out_shape=jax.ShapeDtypeStruct((n, D_MODEL), F32),
        compiler_params=_params(("parallel",)),
        name="post_outproj",
    )(x, ya, bonus, ga, ob, cb, yc, xs, cc, p["ln_w"], p["ln_b"], p["ret_norm"], p["d_skip"], p["ssm_norm"],
      p["bd_a"], p["bd_b"], p["w_out"])


def _ffn_kernel(*refs, gated, final_norm, n_e):
    refs = list(refs)
    x_ref, nf_ref = refs[:2]
    pos = 2
    if gated:
        router_ref = refs[pos]
        pos += 1
    wg_ref, wu_ref, wd_ref = refs[pos:pos + 3]
    pos += 3
    if final_norm:
        nfin_ref = refs[pos]
        pos += 1
    out_ref = refs[pos]
    hb_s, acc_s = refs[pos + 1:pos + 3]
    if gated:
        gate_s = refs[pos + 3]
    e = pl.program_id(1)

    @pl.when(e == 0)
    def _():
        h = _rmsnorm(x_ref[...], nf_ref[...], RMS_EPS)
        hb_s[...] = h.astype(BF16)
        acc_s[...] = jnp.zeros_like(acc_s)
        if gated:
            lane = lax.broadcasted_iota(jnp.int32, (TM, LANES), 1)
            logits = jnp.where(lane < N_EXPERTS, _dot_hi(h, router_ref[...]), -jnp.inf)
            p = jnp.exp(logits - jnp.max(logits, axis=-1, keepdims=True))
            p = p / jnp.sum(p, axis=-1, keepdims=True)
            p1 = jnp.max(p, axis=-1, keepdims=True)
            i1 = jnp.min(jnp.where(p == p1, lane, LANES), axis=-1, keepdims=True)
            rest = jnp.where(lane == i1, -1.0, p)
            p2 = jnp.max(rest, axis=-1, keepdims=True)
            i2 = jnp.min(jnp.where(rest == p2, lane, LANES), axis=-1, keepdims=True)
            gate_s[...] = jnp.where(lane == i1, p1, jnp.where(lane == i2, p2, 0.0)) / (p1 + p2)

    hb = hb_s[...]
    g = jnp.dot(hb, wg_ref[0], preferred_element_type=F32)
    u = jnp.dot(hb, wu_ref[0], preferred_element_type=F32)
    o = _dot(g * _sigmoid(g) * u, wd_ref[0])
    if gated:
        lane = lax.broadcasted_iota(jnp.int32, (TM, LANES), 1)
        o = o * jnp.sum(jnp.where(lane == e, gate_s[...], 0.0), axis=-1, keepdims=True)
    acc_s[...] += o

    @pl.when(e == n_e - 1)
    def _():
        y = x_ref[...] + acc_s[...]
        if final_norm:
            y = _rmsnorm(y, nfin_ref[...], RMS_EPS)
        out_ref[...] = y


def _ffn(x, nf, wg, wu, wd, router=None, nfin=None):
    n = x.shape[0]
    n_e, _, dff = wg.shape
    gated = router is not None
    final_norm = nfin is not None
    vec = pl.BlockSpec((1, D_MODEL), lambda i, e: (0, 0))
    args = [x, nf]
    specs = [pl.BlockSpec((TM, D_MODEL), lambda i, e: (i, 0)), vec]
    if gated:
        args.append(router)
        specs.append(pl.BlockSpec((D_MODEL, LANES), lambda i, e: (0, 0)))
    args += [wg, wu, wd]
    specs += [pl.BlockSpec((1, D_MODEL, dff), lambda i, e: (e, 0, 0)),
              pl.BlockSpec((1, D_MODEL, dff), lambda i, e: (e, 0, 0)),
              pl.BlockSpec((1, dff, D_MODEL), lambda i, e: (e, 0, 0))]
    if final_norm:
        args.append(nfin)
        specs.append(vec)
    scratch = [pltpu.VMEM((TM, D_MODEL), BF16), pltpu.VMEM((TM, D_MODEL), F32)]
    if gated:
        scratch.append(pltpu.VMEM((TM, LANES), F32))
    return pl.pallas_call(
        functools.partial(_ffn_kernel, gated=gated, final_norm=final_norm, n_e=n_e),
        grid=(n // TM, n_e),
        in_specs=specs,
        out_specs=pl.BlockSpec((TM, D_MODEL), lambda i, e: (i, 0)),
        out_shape=jax.ShapeDtypeStruct((n, D_MODEL), F32),
        scratch_shapes=scratch,
        compiler_params=_params(("parallel", "arbitrary")),
        name="moe" if gated else "ffn",
    )(*args)


def _heads(x, b, h):
    n, w = x.shape
    return x.reshape(b, n // b, h, w // h)


def _lanes(x):
    b, t, h, f = x.shape
    return jnp.transpose(x, (1, 3, 0, 2)).reshape(t, f, b * h)


def _from_lanes(y, b, h):
    t, j, _ = y.shape
    return jnp.transpose(y.reshape(t, j, b, h), (2, 0, 3, 1)).reshape(b * t, h * j)


def _state_to_lanes(s, j_first):
    b, h = s.shape[:2]
    y = jnp.transpose(s, (3, 2, 0, 1) if j_first else (2, 3, 0, 1))
    return y.reshape(y.shape[0], y.shape[1], b * h)


def _state_from_lanes(s, b, h, j_first):
    ni, nj, _ = s.shape
    return jnp.transpose(s.reshape(ni, nj, b, h), (2, 3, 1, 0) if j_first else (2, 3, 0, 1))


def _rwkv_long(feats, b):
    r, dec, k2, v, nkk, beta = feats
    t = r.shape[0] // b
    kl = HEAD_DIM // K_SPLIT
    assert b * PLANES_PER_B == LANES and t % TR == 0
    nkk_next = jnp.concatenate([nkk[1:], jnp.zeros((1, nkk.shape[1]), F32)], axis=0)
    q, d, a, n1, a2 = _long_to_lanes([r, dec, k2, nkk_next, beta], b, kl, True)
    (vv,) = _long_to_lanes([v], b, HEAD_DIM, False)
    y, s = _rwkv_long_scan(q, d, a, vv, n1, a2)
    s = s.reshape(kl, HEAD_DIM, b, K_SPLIT, H_PAD)[..., :H_A]
    return _long_from_lanes(y, b), jnp.transpose(s, (2, 4, 1, 3, 0)).reshape(b, H_A, HEAD_DIM, HEAD_DIM)


def _rwkv_short(feats, b, s0):
    r, dec, k2, v, nkk, beta = [_lanes(_heads(f, b, H_A)) for f in feats]
    y, s = _scan("rwkv", r, dec, k2, v, _state_to_lanes(s0, True), n=nkk, a2=beta)
    return _from_lanes(y, b, H_A), _state_from_lanes(s, b, H_A, True)


def _ret_short(qh, kh, vb, b, s0):
    gam = np.tile(1.0 - 2.0 ** (-5.0 - np.arange(H_B, dtype=np.float64)), b).astype(np.float32)
    y, s = _scan("ret", _lanes(_heads(qh, b, H_B)), jnp.asarray(gam.reshape(1, 1, -1)), _lanes(_heads(kh, b, H_B)),
                 _lanes(_heads(vb, b, H_B)), _state_to_lanes(s0, False))
    return _from_lanes(y, b, H_B), _state_from_lanes(s, b, H_B, False)


def _ssd_short(xdt, bm, cm, dssm, b, s0):
    rep = H_C // N_GROUPS
    bh = jnp.repeat(_heads(bm, b, N_GROUPS), rep, axis=2)
    ch = jnp.repeat(_heads(cm, b, N_GROUPS), rep, axis=2)
    dec = _heads(dssm, b, 1)[:, :, 0, :H_C, None]
    y, s = _scan("ssd", _lanes(ch), _lanes(dec), _lanes(bh), _lanes(_heads(xdt, b, H_C)), _state_to_lanes(s0, True))
    return _from_lanes(y, b, H_C), _state_from_lanes(s, b, H_C, True)


def _block_diag_ones(width):
    idx = np.arange(width) // HEAD_DIM
    return jnp.asarray((idx[:, None] == idx[None, :]).astype(np.float32))


def _shifted(x, prev_rows, shift):
    b, t, c = x.shape
    p = prev_rows.shape[1]
    full = jnp.concatenate([prev_rows, x], axis=1)
    return full[:, p - shift:p - shift + t].reshape(b * t, c)


def _layer(x, b, p, st, rope, table_block):
    n = x.shape[0]
    t = n // b
    fresh = st is None
    tiles_per_seq = t // TM if fresh else 0
    ca, cb, cc = _in_proj(x, p["norm_mix"], p["w_in"])
    ca3 = ca.reshape(b, t, COLS_A)
    xbc_tail = cc.reshape(b, t, COLS_C_PAD)[:, -(CONV_W - 1):, W_C:W_C + CONV_DIM]
    if fresh:
        prev, shifted = None, None
        conv_new = xbc_tail
    else:
        prev = _shifted(ca3, st["shift"][:, None, :], 1)
        xbc = cc[:, W_C:W_C + CONV_DIM].reshape(b, t, CONV_DIM)
        shifted = [_shifted(xbc, st["conv"], j) for j in range(1, CONV_W)]
        conv_new = jnp.concatenate([st["conv"], xbc_tail], axis=1)[:, -(CONV_W - 1):]

    r, dec, k2, v, nkk, beta, ga, bonus = _rwkv_prep(ca, prev, p, tiles_per_seq)
    qh, kh = _ret_prep(cb, rope[0], rope[1], table_block)
    xdt, bm, cm, dssm, xs, da = _ssm_prep(cc, shifted, p, tiles_per_seq)

    feats = (r, dec, k2, v, nkk, beta)
    if fresh:
        ya, s_rwkv = _rwkv_long(feats, b)
        yb, s_full = _ret_chunk(qh, kh, cb, b, t)
        s_ret = jnp.stack([s_full[:, h * HEAD_DIM:(h + 1) * HEAD_DIM, h * HEAD_DIM:(h + 1) * HEAD_DIM]
                           for h in range(H_B)], axis=1)
        yc, s_ssm = _ssd_chunk(xdt, bm, cm, da, p["expand"], b, t)
        s_ssm = s_ssm.reshape(b, H_C, HEAD_DIM, N_STATE)
    else:
        ya, s_rwkv = _rwkv_short(feats, b, st["rwkv"])
        yb, s_ret = _ret_short(qh, kh, cb[:, 2 * W_B:3 * W_B], b, st["ret"])
        yc, s_ssm = _ssd_short(xdt, bm, cm, dssm, b, st["ssm"])

    x2 = _post(x, ya, bonus, ga, yb, cb, yc, xs, cc, p)
    x = _ffn(x2, p["norm_ffn"], p["wg"], p["wu"], p["wd"], router=p["router"], nfin=p["norm_final"])
    return x, (s_rwkv, ca3[:, -1], s_ret, s_ssm, conv_new)


def kernel(x_prompt, x_sample, state_rwkv, state_shift, state_ret, state_ssm, state_conv, norm_mix, w_in, rwkv_mu, rwkv_w0, rwkv_w_up, rwkv_a0, rwkv_a_up, rwkv_g_up, rwkv_k_k, rwkv_k_a, rwkv_r_k, rwkv_ln_w, rwkv_ln_b, ret_norm, ssm_conv_w, ssm_conv_b, ssm_dt_bias, ssm_a_log, ssm_d, ssm_norm, w_out, norm_ffn, ffn_w_gate, ffn_w_up, ffn_w_down, moe_router, moe_w_gate, moe_w_up, moe_w_down, norm_final):
    bp, tp, _ = x_prompt.shape
    bs, ts, _ = x_sample.shape
    depth = w_in.shape[0]
    assert tp % TM == 0 and tp % LC == 0 and (bs * ts) % TM == 0 and TM % ts == 0

    rope = _rope_tables(tp, ts)
    bd_a, bd_b = _block_diag_ones(W_A), _block_diag_ones(W_B)
    expand = np.zeros((LANES, W_C), np.float32)
    expand[np.arange(W_C) // HEAD_DIM, np.arange(W_C)] = 1.0
    expand = jnp.asarray(expand)
    row = lambda v: v.reshape(1, -1)
    pad_l = lambda v: jnp.pad(v, (0, LANES - v.shape[0])).reshape(1, LANES)

    xp = x_prompt.reshape(bp * tp, D_MODEL)
    xs = x_sample.reshape(bs * ts, D_MODEL)
    new_p, new_s = [], []
    for i in range(depth):
        j = i // 2
        p = dict(
            norm_mix=row(norm_mix[i]),
            w_in=jnp.pad(w_in[i], ((0, 0), (0, COLS_C_PAD - COLS_C))).astype(BF16),
            mu=row(rwkv_mu[i]), w0=row(rwkv_w0[i]), a0=row(rwkv_a0[i]), k_k=row(rwkv_k_k[i]),
            k_a=row(rwkv_k_a[i]), r_k=row(rwkv_r_k[i]),
            w_up=jnp.pad(rwkv_w_up[i], ((0, AAA_LORA), (0, 0))).astype(BF16),
            a_up=jnp.pad(rwkv_a_up[i], ((DECAY_LORA, 0), (0, 0))).astype(BF16),
            g_up=rwkv_g_up[i].astype(BF16), bd_a=bd_a, bd_b=bd_b,
            conv_w=ssm_conv_w[i], conv_b=row(ssm_conv_b[i]), dt_bias=pad_l(ssm_dt_bias[i]),
            a_log=pad_l(ssm_a_log[i]), expand=expand,
            ln_w=row(rwkv_ln_w[i]), ln_b=row(rwkv_ln_b[i]), ret_norm=row(ret_norm[i]),
            d_skip=row(jnp.repeat(ssm_d[i], HEAD_DIM)), ssm_norm=row(ssm_norm[i]), w_out=w_out[i].astype(BF16),
            norm_ffn=row(norm_ffn[i]), norm_final=row(norm_final) if i == depth - 1 else None)
        if i % 2 == 0:
            dff = ffn_w_gate.shape[-1] // 2
            p.update(router=None,
                     wg=ffn_w_gate[j].reshape(D_MODEL, 2, dff).transpose(1, 0, 2).astype(BF16),
                     wu=ffn_w_up[j].reshape(D_MODEL, 2, dff).transpose(1, 0, 2).astype(BF16),
                     wd=ffn_w_down[j].reshape(2, dff, D_MODEL).astype(BF16))
        else:
            p.update(router=jnp.pad(moe_router[j], ((0, 0), (0, LANES - N_EXPERTS))),
                     wg=moe_w_gate[j].astype(BF16), wu=moe_w_up[j].astype(BF16), wd=moe_w_down[j].astype(BF16))

        xp, st_p = _layer(xp, bp, p, None, rope, lambda t: t % (tp // TM))
        st = dict(rwkv=state_rwkv[i], shift=state_shift[i], ret=state_ret[i], ssm=state_ssm[i],
                  conv=state_conv[i])
        xs, st_s = _layer(xs, bs, p, st, rope, lambda t: tp // TM)
        new_p.append(st_p)
        new_s.append(st_s)

    stack = lambda sts: tuple(jnp.stack(s) for s in zip(*sts))
    return (xp.reshape(bp, tp, D_MODEL), xs.reshape(bs, ts, D_MODEL)) + stack(new_p) + stack(new_s)
```

```python
import functools
import math

import numpy as np
import jax
import jax.numpy as jnp
from jax import lax
from jax.experimental import pallas as pl
from jax.experimental.pallas import tpu as pltpu

F32 = jnp.float32
BF16 = jnp.bfloat16
HIGHEST = lax.Precision.HIGHEST

LANES = 128
SUBLANES = 8
VMEM_LIMIT = 56 * 1024 * 1024

D_MODEL = 1024
HEAD_DIM = 64
H_A, H_B, H_C = 6, 4, 6
W_A, W_B, W_C = H_A * HEAD_DIM, H_B * HEAD_DIM, H_C * HEAD_DIM
DECAY_LORA, AAA_LORA, GATE_LORA = 64, 64, 128
COLS_A = 3 * W_A + DECAY_LORA + AAA_LORA + GATE_LORA
COLS_B = 4 * W_B
N_STATE, N_GROUPS, CONV_W = 128, 2, 4
CONV_DIM = W_C + 2 * N_GROUPS * N_STATE
COLS_C = W_C + CONV_DIM + H_C
COLS_C_PAD = 1408
ROPE_BASE = 10000.0
RMS_EPS = 1e-6
GN_EPS = 64e-5
GATED_NORM_EPS = 1e-5
N_EXPERTS = 8
PAST_LEN = 16384

TM = 512
LC = 256
H_PAD = 8
K_SPLIT = 2


def _dot(a, b):
    return jnp.dot(a.astype(BF16), b.astype(BF16), preferred_element_type=F32)


def _dot_hi(a, b):
    return jnp.dot(a, b, precision=HIGHEST, preferred_element_type=F32)


def _sigmoid(x):
    return 1.0 / (1.0 + jnp.exp(-x))


def _softplus(x):
    return jnp.maximum(x, 0.0) + jnp.log1p(jnp.exp(-jnp.abs(x)))


def _rmsnorm(x, g, eps):
    return x * lax.rsqrt(jnp.mean(x * x, axis=-1, keepdims=True) + eps) * g


def _params(sem):
    return pltpu.CompilerParams(dimension_semantics=sem, vmem_limit_bytes=VMEM_LIMIT)


def _row_spec(width, col=0):
    return pl.BlockSpec((TM, width), lambda i, c=col: (i, c))


def _before_spec(width):
    return pl.BlockSpec((SUBLANES, width), lambda i: (jnp.maximum(i * (TM // SUBLANES) - 1, 0), 0))


def _full_spec(shape):
    nd = len(shape)
    return pl.BlockSpec(shape, lambda i, n=nd: (0,) * n)


def _shift_rows(x, before, j, first):
    rolled = pltpu.roll(x, j, 0)
    prev = jnp.where(first, 0.0, pltpu.roll(before, j, 0))
    row = lax.broadcasted_iota(jnp.int32, prev.shape, 0)
    top = jnp.where(row < j, prev, rolled[0:SUBLANES])
    return jnp.concatenate([top, rolled[SUBLANES:]], axis=0)


def _in_proj_kernel(x_ref, g_ref, w_ref, oa_ref, ob_ref, oc_ref):
    h = _rmsnorm(x_ref[...], g_ref[...], RMS_EPS).astype(BF16)
    oa_ref[...] = jnp.dot(h, w_ref[:, 0:COLS_A], preferred_element_type=F32)
    ob_ref[...] = jnp.dot(h, w_ref[:, COLS_A:COLS_A + COLS_B], preferred_element_type=F32)
    oc_ref[...] = jnp.dot(h, w_ref[:, COLS_A + COLS_B:], preferred_element_type=F32)


def _in_proj(x, g, w):
    n = x.shape[0]
    wtot = w.shape[1]
    return pl.pallas_call(
        _in_proj_kernel,
        grid=(n // TM,),
        in_specs=[_row_spec(D_MODEL), _full_spec((1, D_MODEL)), _full_spec((D_MODEL, wtot))],
        out_specs=[_row_spec(COLS_A), _row_spec(COLS_B), _row_spec(COLS_C_PAD)],
        out_shape=[jax.ShapeDtypeStruct((n, COLS_A), F32), jax.ShapeDtypeStruct((n, COLS_B), F32),
                   jax.ShapeDtypeStruct((n, COLS_C_PAD), F32)],
        compiler_params=_params(("parallel",)),
        name="in_proj",
    )(x, g, w)


def _rwkv_prep_kernel(c_ref, p_ref, mu_ref, w0_ref, a0_ref, kk_ref, ka_ref, rk_ref, wup_ref, aup_ref,
                      gup_ref, bd_ref, r_o, d_o, k_o, v_o, n_o, b_o, g_o, bonus_o, *, tiles_per_seq):
    c = c_ref[...]
    if tiles_per_seq:
        prev = _shift_rows(c, p_ref[...], 1, pl.program_id(0) % tiles_per_seq == 0)
    else:
        prev = p_ref[...]
    xm = c + (prev - c) * mu_ref[...]
    r = xm[:, 0:W_A]
    k = xm[:, W_A:2 * W_A]
    v = xm[:, 2 * W_A:3 * W_A]
    lora = xm[:, 3 * W_A:3 * W_A + DECAY_LORA + AAA_LORA]
    gd = xm[:, 3 * W_A + DECAY_LORA + AAA_LORA:]
    w = w0_ref[...] + _dot(jnp.tanh(lora), wup_ref[...])
    decay = jnp.exp(-math.exp(-0.5) * _sigmoid(w))
    a = _sigmoid(a0_ref[...] + _dot(lora, aup_ref[...]))
    g = _dot(_sigmoid(gd), gup_ref[...])
    bd = bd_ref[...]
    kk = k * kk_ref[...]
    kk = kk / jnp.maximum(jnp.sqrt(_dot_hi(kk * kk, bd)), 1e-12)
    k2 = k * (1.0 + (a - 1.0) * ka_ref[...])
    for o_ref, val in ((r_o, r), (d_o, decay), (k_o, k2), (v_o, v), (n_o, -kk), (b_o, kk * a)):
        o_ref[:, 0:W_A] = val
        if o_ref.shape[1] > W_A:
            o_ref[:, W_A:] = jnp.zeros((TM, o_ref.shape[1] - W_A), F32)
    g_o[...] = g
    bonus_o[...] = _dot_hi(r * k2 * rk_ref[...], bd) * v


def _rwkv_prep(ca, prev, p, tiles_per_seq):
    n = ca.shape[0]
    vec = _full_spec((1, W_A))
    lora_spec = _full_spec((DECAY_LORA + AAA_LORA, W_A))
    p_spec = _before_spec(COLS_A) if tiles_per_seq else _row_spec(COLS_A)
    w_scan = H_PAD * HEAD_DIM if tiles_per_seq else W_A
    widths = [w_scan] * 6 + [W_A] * 2
    return pl.pallas_call(
        functools.partial(_rwkv_prep_kernel, tiles_per_seq=tiles_per_seq),
        grid=(n // TM,),
        in_specs=[_row_spec(COLS_A), p_spec, _full_spec((1, COLS_A)), vec, vec, vec, vec, vec,
                  lora_spec, lora_spec, _full_spec((GATE_LORA, W_A)), _full_spec((W_A, W_A))],
        out_specs=[_row_spec(w) for w in widths],
        out_shape=[jax.ShapeDtypeStruct((n, w), F32) for w in widths],
        compiler_params=_params(("parallel",)),
        name="rwkv_prep",
    )(ca, ca if tiles_per_seq else prev, p["mu"], p["w0"], p["a0"], p["k_k"], p["k_a"], p["r_k"],
      p["w_up"], p["a_up"], p["g_up"], p["bd_a"])


def _rope_kernel(ang_ref, cos_o, sin_o):
    ang = ang_ref[...]
    lane = lax.broadcasted_iota(jnp.int32, ang.shape, 1)
    cos_o[...] = jnp.cos(ang)
    sin_o[...] = jnp.where((lane % HEAD_DIM) < (HEAD_DIM // 2), -jnp.sin(ang), jnp.sin(ang))


def _rope_tables(t_prompt, t_sample):
    theta = 1.0 / (ROPE_BASE ** jnp.linspace(0.0, 1.0, HEAD_DIM // 2, dtype=F32))
    pos = jnp.concatenate([jnp.arange(t_prompt, dtype=F32), PAST_LEN + (jnp.arange(TM) % t_sample).astype(F32)])
    ang = jnp.tile(pos[:, None] * theta[None, :], (1, W_B // (HEAD_DIM // 2)))
    n = ang.shape[0]
    return pl.pallas_call(
        _rope_kernel,
        grid=(n // TM,),
        in_specs=[_row_spec(W_B)],
        out_specs=[_row_spec(W_B)] * 2,
        out_shape=[jax.ShapeDtypeStruct((n, W_B), F32)] * 2,
        compiler_params=_params(("parallel",)),
        name="rope_tables",
    )(ang)


def _ret_prep_kernel(q_ref, k_ref, cos_ref, sin_ref, q_o, k_o):
    cos = cos_ref[...]
    sin = sin_ref[...]
    lane = lax.broadcasted_iota(jnp.int32, (TM, W_B), 1)
    first_half = (lane % HEAD_DIM) < (HEAD_DIM // 2)

    def rope(x):
        partner = jnp.where(first_half, pltpu.roll(x, W_B - HEAD_DIM // 2, 1), pltpu.roll(x, HEAD_DIM // 2, 1))
        return x * cos + partner * sin

    q_o[...] = rope(q_ref[...])
    k_o[...] = rope(k_ref[...]) * (HEAD_DIM ** -0.5)


def _ret_prep(cb, cos, sin, table_block):
    n = cb.shape[0]
    tab = pl.BlockSpec((TM, W_B), lambda i: (table_block(i), 0))
    return pl.pallas_call(
        _ret_prep_kernel,
        grid=(n // TM,),
        in_specs=[_row_spec(W_B, 0), _row_spec(W_B, 1), tab, tab],
        out_specs=[_row_spec(W_B)] * 2,
        out_shape=[jax.ShapeDtypeStruct((n, W_B), F32)] * 2,
        compiler_params=_params(("parallel",)),
        name="ret_prep",
    )(cb, cb, cos, sin)


def _ssm_prep_kernel(*refs, tiles_per_seq):
    n_shift_refs = 1 if tiles_per_seq else CONV_W - 1
    cc_ref = refs[0]
    shift_refs = refs[1:1 + n_shift_refs]
    cw_ref, cb_ref, dtb_ref, alog_ref, ex_ref, xdt_o, b_o, c_o, dec_o, xs_o, da_o = refs[1 + n_shift_refs:]
    cc = cc_ref[...]
    x0 = cc[:, W_C:W_C + CONV_DIM]
    if tiles_per_seq:
        before = shift_refs[0][:, W_C:W_C + CONV_DIM]
        first = pl.program_id(0) % tiles_per_seq == 0
        xs_prev = [_shift_rows(x0, before, j, first) for j in range(1, CONV_W)]
    else:
        xs_prev = [r[...] for r in shift_refs]
    cw = cw_ref[...]
    conv = x0 * cw[CONV_W - 1:CONV_W, :] + cb_ref[...]
    for j in range(1, CONV_W):
        conv = conv + xs_prev[j - 1] * cw[CONV_W - 1 - j:CONV_W - j, :]
    act = conv * _sigmoid(conv)
    xs = act[:, 0:W_C]
    dt = _softplus(cc[:, W_C + CONV_DIM:] + dtb_ref[...])
    da = dt * (-jnp.exp(alog_ref[...]))
    da_o[...] = da
    dec_o[...] = jnp.exp(da)
    xdt_o[...] = xs * _dot_hi(dt, ex_ref[...])
    b_o[...] = act[:, W_C:W_C + N_GROUPS * N_STATE]
    c_o[...] = act[:, W_C + N_GROUPS * N_STATE:]
    xs_o[...] = xs


def _ssm_prep(cc, shifted, p, tiles_per_seq):
    n = cc.shape[0]
    gn = N_GROUPS * N_STATE
    if tiles_per_seq:
        shift_args, shift_specs = [cc], [_before_spec(COLS_C_PAD)]
    else:
        shift_args, shift_specs = list(shifted), [_row_spec(CONV_DIM)] * (CONV_W - 1)
    return pl.pallas_call(
        functools.partial(_ssm_prep_kernel, tiles_per_seq=tiles_per_seq),
        grid=(n // TM,),
        in_specs=[_row_spec(COLS_C_PAD)] + shift_specs + [
            _full_spec((CONV_W, CONV_DIM)), _full_spec((1, CONV_DIM)), _full_spec((1, LANES)),
            _full_spec((1, LANES)), _full_spec((LANES, W_C))],
        out_specs=[_row_spec(W_C), _row_spec(gn), _row_spec(gn), _row_spec(LANES), _row_spec(W_C),
                   _row_spec(LANES)],
        out_shape=[jax.ShapeDtypeStruct((n, W_C), F32), jax.ShapeDtypeStruct((n, gn), F32),
                   jax.ShapeDtypeStruct((n, gn), F32), jax.ShapeDtypeStruct((n, LANES), F32),
                   jax.ShapeDtypeStruct((n, W_C), F32), jax.ShapeDtypeStruct((n, LANES), F32)],
        compiler_params=_params(("parallel",)),
        name="ssm_prep",
    )(cc, *shift_args, p["conv_w"], p["conv_b"], p["dt_bias"], p["a_log"], p["expand"])


def _scan_kernel(*refs, mode, ni, tt_len, n_tt):
    if mode == "rwkv":
        q_ref, d_ref, a_ref, b_ref, n_ref, a2_ref, s0_ref, y_ref, so_ref, st = refs
    else:
        q_ref, d_ref, a_ref, b_ref, s0_ref, y_ref, so_ref, st = refs
    tt = pl.program_id(1)

    @pl.when(tt == 0)
    def _():
        st[...] = s0_ref[...]

    def row(ref, t, i):
        return ref[t, pl.ds(i, 1), :]

    def step(t, carry):
        bv = b_ref[t]
        if mode == "rwkv":
            sa_parts = [jnp.zeros_like(bv), jnp.zeros_like(bv)]
            for i in range(ni):
                sa_parts[i % 2] = sa_parts[i % 2] + st[i] * row(n_ref, t, i)
            sa = sa_parts[0] + sa_parts[1]
        elif mode == "ssd":
            d = d_ref[t]
        else:
            d = d_ref[0]
        y_parts = [jnp.zeros_like(bv), jnp.zeros_like(bv)]
        for i in range(ni):
            if mode == "rwkv":
                s = st[i] * row(d_ref, t, i) + row(a_ref, t, i) * bv + row(a2_ref, t, i) * sa
            else:
                s = st[i] * d + row(a_ref, t, i) * bv
            st[i] = s
            y_parts[i % 2] = y_parts[i % 2] + s * row(q_ref, t, i)
        y_ref[t] = y_parts[0] + y_parts[1]
        return carry

    lax.fori_loop(0, tt_len, step, 0)

    @pl.when(tt == n_tt - 1)
    def _():
        so_ref[...] = st[...]


def _scan(mode, q, d, a, b, s0, n=None, a2=None, tt_len=32):
    t_len, ni, lanes = q.shape
    nj = b.shape[1]
    tt_len = min(tt_len, t_len)
    n_tt = t_len // tt_len
    groups = lanes // LANES
    i_spec = pl.BlockSpec((tt_len, ni, LANES), lambda g, t: (t, 0, g))
    j_spec = pl.BlockSpec((tt_len, nj, LANES), lambda g, t: (t, 0, g))
    s_spec = pl.BlockSpec((ni, nj, LANES), lambda g, t: (0, 0, g))
    if mode == "rwkv":
        d_spec = i_spec
    elif mode == "ssd":
        d_spec = pl.BlockSpec((tt_len, 1, LANES), lambda g, t: (t, 0, g))
    else:
        d_spec = pl.BlockSpec((1, 1, LANES), lambda g, t: (0, 0, g))
    args = [q, d, a, b]
    specs = [i_spec, d_spec, i_spec, j_spec]
    if mode == "rwkv":
        args += [n, a2]
        specs += [i_spec, i_spec]
    args.append(s0)
    specs.append(s_spec)
    return pl.pallas_call(
        functools.partial(_scan_kernel, mode=mode, ni=ni, tt_len=tt_len, n_tt=n_tt),
        grid=(groups, n_tt),
        in_specs=specs,
        out_specs=[j_spec, s_spec],
        out_shape=[jax.ShapeDtypeStruct((t_len, nj, lanes), F32), jax.ShapeDtypeStruct((ni, nj, lanes), F32)],
        scratch_shapes=[pltpu.VMEM((ni, nj, LANES), F32)],
        compiler_params=_params(("parallel", "arbitrary")),
        name="scan_" + mode,
    )(*args)


def _rwkv_long_kernel(q_ref, d_ref, a_ref, b_ref, n1_ref, a2_ref, y_ref, so_ref, st, sa_s, *, ni, tt_len, n_tt):
    tt = pl.program_id(1)

    @pl.when(tt == 0)
    def _():
        st[...] = jnp.zeros_like(st)
        sa_s[...] = jnp.zeros_like(sa_s)

    def row(ref, t, i):
        return ref[t, pl.ds(i, 1), :]

    def fold(x):
        return x + pltpu.roll(x, LANES // 2, 1)

    def step(t, sa):
        bv = b_ref[t]
        a, a2, q, n1 = a_ref[t], a2_ref[t], q_ref[t], n1_ref[t]
        aq, a2q, an, a2n = [fold(jnp.sum(u * w, axis=0, keepdims=True))
                            for u, w in ((a, q), (a2, q), (a, n1), (a2, n1))]
        y0 = [jnp.zeros_like(bv), jnp.zeros_like(bv)]
        n0 = [jnp.zeros_like(bv), jnp.zeros_like(bv)]
        for i in range(ni):
            sd = st[i] * row(d_ref, t, i)
            st[i] = sd
            y0[i % 2] = y0[i % 2] + sd * row(q_ref, t, i)
            n0[i % 2] = n0[i % 2] + sd * row(n1_ref, t, i)
        y_ref[t] = fold(y0[0] + y0[1]) + bv * aq + sa * a2q
        sa_next = fold(n0[0] + n0[1]) + bv * an + sa * a2n
        for i in range(ni):
            st[i] = st[i] + row(a_ref, t, i) * bv + row(a2_ref, t, i) * sa
        return sa_next

    sa_s[...] = lax.fori_loop(0, tt_len, step, sa_s[...])

    @pl.when(tt == n_tt - 1)
    def _():
        so_ref[...] = st[...]


def _rwkv_long_scan(q, d, a, b, n1, a2, tt_len=64):
    t_len, ni, _ = q.shape
    nj = b.shape[1]
    n_tt = t_len // tt_len
    i_spec = pl.BlockSpec((tt_len, ni, LANES), lambda g, t: (t, 0, 0))
    j_spec = pl.BlockSpec((tt_len, nj, LANES), lambda g, t: (t, 0, 0))
    s_spec = pl.BlockSpec((ni, nj, LANES), lambda g, t: (0, 0, 0))
    return pl.pallas_call(
        functools.partial(_rwkv_long_kernel, ni=ni, tt_len=tt_len, n_tt=n_tt),
        grid=(1, n_tt),
        in_specs=[i_spec, i_spec, i_spec, j_spec, i_spec, i_spec],
        out_specs=[j_spec, s_spec],
        out_shape=[jax.ShapeDtypeStruct((t_len, nj, LANES), F32), jax.ShapeDtypeStruct((ni, nj, LANES), F32)],
        scratch_shapes=[pltpu.VMEM((ni, nj, LANES), F32), pltpu.VMEM((nj, LANES), F32)],
        compiler_params=_params(("parallel", "arbitrary")),
        name="scan_rwkv_long",
    )(q, d, a, b, n1, a2)


TR = 128
HALF = LANES // K_SPLIT


def _to_lanes_kernel(*refs, n_ops, rows, per_half):
    x_refs, o_refs, y2d = refs[:n_ops], refs[n_ops:2 * n_ops], refs[2 * n_ops]
    n_b = x_refs[0].shape[0]
    for x_ref, o_ref in zip(x_refs, o_refs):
        for b in range(n_b):
            xt = x_ref[b].T
            for h in range(H_PAD):
                for s in range(K_SPLIT):
                    lane = s * HALF + b * H_PAD + h
                    f0 = h * HEAD_DIM + (s * rows if per_half else 0)
                    y2d[pl.ds(lane, rows, stride=LANES), :] = xt[f0:f0 + rows, :]
        for r in range(rows):
            o_ref[pl.ds(r, TR, stride=rows), :] = y2d[r * LANES:(r + 1) * LANES, :].T


def _long_to_lanes(xs, b, rows, per_half):
    n_ops = len(xs)
    t = xs[0].shape[0] // b
    width = H_PAD * HEAD_DIM
    outs = pl.pallas_call(
        functools.partial(_to_lanes_kernel, n_ops=n_ops, rows=rows, per_half=per_half),
        grid=(t // TR,),
        in_specs=[pl.BlockSpec((b, TR, width), lambda i: (0, i, 0))] * n_ops,
        out_specs=[pl.BlockSpec((TR * rows, LANES), lambda i: (i, 0))] * n_ops,
        out_shape=[jax.ShapeDtypeStruct((t * rows, LANES), F32)] * n_ops,
        scratch_shapes=[pltpu.VMEM((rows * LANES, TR), F32)],
        compiler_params=_params(("parallel",)),
        name="to_lanes",
    )(*[x.reshape(b, t, width) for x in xs])
    return [o.reshape(t, rows, LANES) for o in outs]


def _from_lanes_kernel(y_ref, o_ref, z2d):
    for v in range(HEAD_DIM):
        z2d[pl.ds(v, LANES, stride=HEAD_DIM), :] = y_ref[pl.ds(v, TR, stride=HEAD_DIM), :].T
    for b in range(o_ref.shape[0]):
        r0 = b * H_PAD * HEAD_DIM
        o_ref[b] = z2d[r0:r0 + W_A, :].T


def _long_from_lanes(y, b):
    t = y.shape[0]
    out = pl.pallas_call(
        _from_lanes_kernel,
        grid=(t // TR,),
        in_specs=[pl.BlockSpec((TR * HEAD_DIM, LANES), lambda i: (i, 0))],
        out_specs=pl.BlockSpec((b, TR, W_A), lambda i: (0, i, 0)),
        out_shape=jax.ShapeDtypeStruct((b, t, W_A), F32),
        scratch_shapes=[pltpu.VMEM((LANES * HEAD_DIM, TR), F32)],
        compiler_params=_params(("parallel",)),
        name="from_lanes",
    )(y.reshape(t * HEAD_DIM, LANES))
    return out.reshape(b * t, W_A)


def _ret_chunk_kernel(q_ref, k_ref, v_ref, y_ref, s_ref, st):
    c = pl.program_id(1)

    @pl.when(c == 0)
    def _():
        st[...] = jnp.zeros_like(st)

    q = q_ref[...]
    k = k_ref[...]
    vb = v_ref[...].astype(BF16)
    kb = k.astype(BF16)
    row = lax.broadcasted_iota(jnp.int32, (LC, W_B), 0)
    head = lax.broadcasted_iota(jnp.int32, (LC, W_B), 1) // HEAD_DIM
    log_g = [math.log(1.0 - 2.0 ** (-5.0 - h)) for h in range(H_B)]
    lg = jnp.full((LC, W_B), log_g[0], F32)
    for h in range(1, H_B):
        lg = jnp.where(head == h, log_g[h], lg)
    rowf = row.astype(F32)
    diff = (lax.broadcasted_iota(jnp.int32, (LC, LC), 0) - lax.broadcasted_iota(jnp.int32, (LC, LC), 1))
    causal = diff >= 0
    difff = jnp.maximum(diff, 0).astype(F32)

    out = jnp.dot((q * jnp.exp(lg * (rowf + 1.0))).astype(BF16), st[...].astype(BF16), preferred_element_type=F32)
    for h in range(H_B):
        qm = jnp.where(head == h, q, 0.0).astype(BF16)
        s = lax.dot_general(qm, kb, (((1,), (1,)), ((), ())), preferred_element_type=F32)
        p = jnp.where(causal, s * jnp.exp(log_g[h] * difff), 0.0).astype(BF16)
        out = out + jnp.where(head == h, jnp.dot(p, vb, preferred_element_type=F32), 0.0)
    y_ref[...] = out

    kt = (k * jnp.exp(lg * (LC - 1.0 - rowf))).T.astype(BF16)
    kv = jnp.dot(kt, vb, preferred_element_type=F32)
    r2 = lax.broadcasted_iota(jnp.int32, (W_B, W_B), 0) // HEAD_DIM
    c2 = lax.broadcasted_iota(jnp.int32, (W_B, W_B), 1) // HEAD_DIM
    cdec = jnp.full((W_B, W_B), math.exp(log_g[0] * LC), F32)
    for h in range(1, H_B):
        cdec = jnp.where(r2 == h, math.exp(log_g[h] * LC), cdec)
    st[...] = st[...] * cdec + jnp.where(r2 == c2, kv, 0.0)
    s_ref[0] = st[...]


def _ret_chunk(q, k, cb, n_seq, t_len):
    n_c = t_len // LC
    rows = lambda col: pl.BlockSpec((LC, W_B), lambda b, c, col=col: (b * n_c + c, col))
    return pl.pallas_call(
        _ret_chunk_kernel,
        grid=(n_seq, n_c),
        in_specs=[rows(0), rows(0), rows(2)],
        out_specs=[rows(0), pl.BlockSpec((1, W_B, W_B), lambda b, c: (b, 0, 0))],
        out_shape=[jax.ShapeDtypeStruct((n_seq * t_len, W_B), F32), jax.ShapeDtypeStruct((n_seq, W_B, W_B), F32)],
        scratch_shapes=[pltpu.VMEM((W_B, W_B), F32)],
        compiler_params=_params(("parallel", "arbitrary")),
        name="ret_chunk",
    )(q, k, cb)


def _ssd_chunk_kernel(x_ref, b_ref, c_ref, da_ref, ex_ref, y_ref, h_ref, st):
    ci = pl.program_id(1)

    @pl.when(ci == 0)
    def _():
        st[...] = jnp.zeros_like(st)

    gn = N_STATE
    half = W_C // N_GROUPS
    ex = ex_ref[...]
    x = x_ref[...]
    xb = x.astype(BF16)
    ri = lax.broadcasted_iota(jnp.int32, (LC, LC), 0)
    cj = lax.broadcasted_iota(jnp.int32, (LC, LC), 1)
    causal = ri >= cj
    cum = _dot_hi(causal.astype(F32), da_ref[...])
    cum_e = _dot_hi(cum, ex)
    last_e = cum_e[LC - 1:LC, :]
    cum_t = cum.T
    head = lax.broadcasted_iota(jnp.int32, (LC, W_C), 1) // HEAD_DIM
    bg = [b_ref[:, g * gn:(g + 1) * gn].astype(BF16) for g in range(N_GROUPS)]
    cg = [c_ref[:, g * gn:(g + 1) * gn].astype(BF16) for g in range(N_GROUPS)]
    cb = [lax.dot_general(cg[g], bg[g], (((1,), (1,)), ((), ())), preferred_element_type=F32)
          for g in range(N_GROUPS)]

    hb = st[...].astype(BF16)
    ys = [lax.dot_general(cg[g], hb, (((1,), (1,)), ((), ())), preferred_element_type=F32)
          for g in range(N_GROUPS)]
    y = jnp.where(head < H_C // N_GROUPS, ys[0], ys[1]) * jnp.exp(cum_e)
    for h in range(H_C):
        seg = cum[:, h:h + 1] - cum_t[h:h + 1, :]
        p = (cb[h // (H_C // N_GROUPS)] * jnp.exp(jnp.where(causal, seg, -jnp.inf))).astype(BF16)
        y = y + jnp.where(head == h, jnp.dot(p, xb, preferred_element_type=F32), 0.0)
    y_ref[...] = y

    xt = (x * jnp.exp(last_e - cum_e)).T.astype(BF16)
    upd = [jnp.dot(xt, bg[g], preferred_element_type=F32) for g in range(N_GROUPS)]
    rowi = lax.broadcasted_iota(jnp.int32, (W_C, gn), 0)
    sel = lax.broadcasted_iota(jnp.int32, (W_C, LANES), 1) == lax.broadcasted_iota(jnp.int32, (W_C, LANES), 0) // HEAD_DIM
    tot = jnp.sum(jnp.where(sel, cum[LC - 1:LC, :], 0.0), axis=1, keepdims=True)
    st[...] = st[...] * jnp.exp(tot) + jnp.where(rowi < half, upd[0], upd[1])
    h_ref[0] = st[...]


def _ssd_chunk(xdt, bm, cm, da, ex, n_seq, t_len):
    n_c = t_len // LC
    gn = N_GROUPS * N_STATE
    rows = lambda w: pl.BlockSpec((LC, w), lambda b, c: (b * n_c + c, 0))
    return pl.pallas_call(
        _ssd_chunk_kernel,
        grid=(n_seq, n_c),
        in_specs=[rows(W_C), rows(gn), rows(gn), rows(LANES), pl.BlockSpec((LANES, W_C), lambda b, c: (0, 0))],
        out_specs=[rows(W_C), pl.BlockSpec((1, W_C, N_STATE), lambda b, c: (b, 0, 0))],
        out_shape=[jax.ShapeDtypeStruct((n_seq * t_len, W_C), F32),
                   jax.ShapeDtypeStruct((n_seq, W_C, N_STATE), F32)],
        scratch_shapes=[pltpu.VMEM((W_C, N_STATE), F32)],
        compiler_params=_params(("parallel", "arbitrary")),
        name="ssd_chunk",
    )(xdt, bm, cm, da, ex)


def _post_kernel(x_ref, ya_ref, bonus_ref, ga_ref, ob_ref, gb_ref, yc_ref, xs_ref, z_ref, lnw_ref, lnb_ref,
                 rn_ref, dsk_ref, sn_ref, bda_ref, bdb_ref, wo_ref, out_ref):
    inv_hd = 1.0 / HEAD_DIM
    y = ya_ref[...]
    bda = bda_ref[...]
    mean = _dot_hi(y, bda) * inv_hd
    yd = y - mean
    var = _dot_hi(yd * yd, bda) * inv_hd
    ya = (yd * lax.rsqrt(var + GN_EPS) * lnw_ref[...] + lnb_ref[...] + bonus_ref[...]) * ga_ref[...]
    o = ob_ref[...]
    ms = _dot_hi(o * o, bdb_ref[...]) * inv_hd
    gb = gb_ref[...]
    yb = (gb * _sigmoid(gb)) * (o * lax.rsqrt(ms + RMS_EPS) * rn_ref[...])
    z = z_ref[...]
    yc = (yc_ref[...] + dsk_ref[...] * xs_ref[...]) * (z * _sigmoid(z))
    yc = _rmsnorm(yc, sn_ref[...], GATED_NORM_EPS)
    out_ref[...] = (x_ref[...] + _dot(ya, wo_ref[0:W_A, :]) + _dot(yb, wo_ref[W_A:W_A + W_B, :])
                    + _dot(yc, wo_ref[W_A + W_B:, :]))


def _post(x, ya, bonus, ga, ob, cb, yc, xs, cc, p):
    n = x.shape[0]
    va = _full_spec((1, W_A))
    return pl.pallas_call(
        _post_kernel,
        grid=(n // TM,),
        in_specs=[_row_spec(D_MODEL), _row_spec(W_A), _row_spec(W_A), _row_spec(W_A), _row_spec(W_B),
                  _row_spec(W_B, 3), _row_spec(W_C), _row_spec(W_C), _row_spec(W_C, 0), va, va,
                  _full_spec((1, W_B)), va, va, _full_spec((W_A, W_A)), _full_spec((W_B, W_B)),
                  _full_spec((D_MODEL, D_MODEL))],
        out_specs=_row_spec(D_MODEL),
        out_shape=jax.ShapeDtypeStruct((n, D_MODEL), F32),
        compiler_params=_params(("parallel",)),
        name="post_outproj",
    )(x, ya, bonus, ga, ob, cb, yc, xs, cc, p["ln_w"], p["ln_b"], p["ret_norm"], p["d_skip"], p["ssm_norm"],
      p["bd_a"], p["bd_b"], p["w_out"])


def _ffn_kernel(*refs, gated, final_norm, n_e):
    refs = list(refs)
    x_ref, nf_ref = refs[:2]
    pos = 2
    if gated:
        router_ref = refs[pos]
        pos += 1
    wg_ref, wu_ref, wd_ref = refs[pos:pos + 3]
    pos += 3
    if final_norm:
        nfin_ref = refs[pos]
        pos += 1
    out_ref = refs[pos]
    hb_s, acc_s = refs[pos + 1:pos + 3]
    if gated:
        gate_s = refs[pos + 3]
    e = pl.program_id(1)

    @pl.when(e == 0)
    def _():
        h = _rmsnorm(x_ref[...], nf_ref[...], RMS_EPS)
        hb_s[...] = h.astype(BF16)
        acc_s[...] = jnp.zeros_like(acc_s)
        if gated:
            lane = lax.broadcasted_iota(jnp.int32, (TM, LANES), 1)
            logits = jnp.where(lane < N_EXPERTS, _dot_hi(h, router_ref[...]), -jnp.inf)
            p = jnp.exp(logits - jnp.max(logits, axis=-1, keepdims=True))
            p = p / jnp.sum(p, axis=-1, keepdims=True)
            p1 = jnp.max(p, axis=-1, keepdims=True)
            i1 = jnp.min(jnp.where(p == p1, lane, LANES), axis=-1, keepdims=True)
            rest = jnp.where(lane == i1, -1.0, p)
            p2 = jnp.max(rest, axis=-1, keepdims=True)
            i2 = jnp.min(jnp.where(rest == p2, lane, LANES), axis=-1, keepdims=True)
            gate_s[...] = jnp.where(lane == i1, p1, jnp.where(lane == i2, p2, 0.0)) / (p1 + p2)

    hb = hb_s[...]
    g = jnp.dot(hb, wg_ref[0], preferred_element_type=F32)
    u = jnp.dot(hb, wu_ref[0], preferred_element_type=F32)
    o = _dot(g * _sigmoid(g) * u, wd_ref[0])
    if gated:
        lane = lax.broadcasted_iota(jnp.int32, (TM, LANES), 1)
        o = o * jnp.sum(jnp.where(lane == e, gate_s[...], 0.0), axis=-1, keepdims=True)
    acc_s[...] += o

    @pl.when(e == n_e - 1)
    def _():
        y = x_ref[...] + acc_s[...]
        if final_norm:
            y = _rmsnorm(y, nfin_ref[...], RMS_EPS)
        out_ref[...] = y


def _ffn(x, nf, wg, wu, wd, router=None, nfin=None):
    n = x.shape[0]
    n_e, _, dff = wg.shape
    gated = router is not None
    final_norm = nfin is not None
    vec = pl.BlockSpec((1, D_MODEL), lambda i, e: (0, 0))
    args = [x, nf]
    specs = [pl.BlockSpec((TM, D_MODEL), lambda i, e: (i, 0)), vec]
    if gated:
        args.append(router)
        specs.append(pl.BlockSpec((D_MODEL, LANES), lambda i, e: (0, 0)))
    args += [wg, wu, wd]
    specs += [pl.BlockSpec((1, D_MODEL, dff), lambda i, e: (e, 0, 0)),
              pl.BlockSpec((1, D_MODEL, dff), lambda i, e: (e, 0, 0)),
              pl.BlockSpec((1, dff, D_MODEL), lambda i, e: (e, 0, 0))]
    if final_norm:
        args.append(nfin)
        specs.append(vec)
    scratch = [pltpu.VMEM((TM, D_MODEL), BF16), pltpu.VMEM((TM, D_MODEL), F32)]
    if gated:
        scratch.append(pltpu.VMEM((TM, LANES), F32))
    return pl.pallas_call(
        functools.partial(_ffn_kernel, gated=gated, final_norm=final_norm, n_e=n_e),
        grid=(n // TM, n_e),
        in_specs=specs,
        out_specs=pl.BlockSpec((TM, D_MODEL), lambda i, e: (i, 0)),
        out_shape=jax.ShapeDtypeStruct((n, D_MODEL), F32),
        scratch_shapes=scratch,
        compiler_params=_params(("parallel", "arbitrary")),
        name="moe" if gated else "ffn",
    )(*args)


def _heads(x, b, h):
    n, w = x.shape
    return x.reshape(b, n // b, h, w // h)


def _lanes(x):
    b, t, h, f = x.shape
    return jnp.transpose(x, (1, 3, 0, 2)).reshape(t, f, b * h)


def _from_lanes(y, b, h):
    t, j, _ = y.shape
    return jnp.transpose(y.reshape(t, j, b, h), (2, 0, 3, 1)).reshape(b * t, h * j)


def _state_to_lanes(s, j_first):
    b, h = s.shape[:2]
    y = jnp.transpose(s, (3, 2, 0, 1) if j_first else (2, 3, 0, 1))
    return y.reshape(y.shape[0], y.shape[1], b * h)


def _state_from_lanes(s, b, h, j_first):
    ni, nj, _ = s.shape
    return jnp.transpose(s.reshape(ni, nj, b, h), (2, 3, 1, 0) if j_first else (2, 3, 0, 1))


def _rwkv_long(feats, b):
    r, dec, k2, v, nkk, beta = feats
    t = r.shape[0] // b
    kl = HEAD_DIM // K_SPLIT
    assert b * H_PAD == HALF and t % TR == 0
    nkk_next = jnp.concatenate([nkk[1:], jnp.zeros((1, nkk.shape[1]), F32)], axis=0)
    q, d, a, n1, a2 = _long_to_lanes([r, dec, k2, nkk_next, beta], b, kl, True)
    (vv,) = _long_to_lanes([v], b, HEAD_DIM, False)
    y, s = _rwkv_long_scan(q, d, a, vv, n1, a2)
    s = s.reshape(kl, HEAD_DIM, K_SPLIT, b, H_PAD)[..., :H_A]
    return _long_from_lanes(y, b), jnp.transpose(s, (3, 4, 1, 2, 0)).reshape(b, H_A, HEAD_DIM, HEAD_DIM)


def _rwkv_short(feats, b, s0):
    r, dec, k2, v, nkk, beta = [_lanes(_heads(f, b, H_A)) for f in feats]
    y, s = _scan("rwkv", r, dec, k2, v, _state_to_lanes(s0, True), n=nkk, a2=beta)
    return _from_lanes(y, b, H_A), _state_from_lanes(s, b, H_A, True)


def _ret_short(qh, kh, vb, b, s0):
    gam = np.tile(1.0 - 2.0 ** (-5.0 - np.arange(H_B, dtype=np.float64)), b).astype(np.float32)
    y, s = _scan("ret", _lanes(_heads(qh, b, H_B)), jnp.asarray(gam.reshape(1, 1, -1)), _lanes(_heads(kh, b, H_B)),
                 _lanes(_heads(vb, b, H_B)), _state_to_lanes(s0, False))
    return _from_lanes(y, b, H_B), _state_from_lanes(s, b, H_B, False)


def _ssd_short(xdt, bm, cm, dssm, b, s0):
    rep = H_C // N_GROUPS
    bh = jnp.repeat(_heads(bm, b, N_GROUPS), rep, axis=2)
    ch = jnp.repeat(_heads(cm, b, N_GROUPS), rep, axis=2)
    dec = _heads(dssm, b, 1)[:, :, 0, :H_C, None]
    y, s = _scan("ssd", _lanes(ch), _lanes(dec), _lanes(bh), _lanes(_heads(xdt, b, H_C)), _state_to_lanes(s0, True))
    return _from_lanes(y, b, H_C), _state_from_lanes(s, b, H_C, True)


def _block_diag_ones(width):
    idx = np.arange(width) // HEAD_DIM
    return jnp.asarray((idx[:, None] == idx[None, :]).astype(np.float32))


def _shifted(x, prev_rows, shift):
    b, t, c = x.shape
    p = prev_rows.shape[1]
    full = jnp.concatenate([prev_rows, x], axis=1)
    return full[:, p - shift:p - shift + t].reshape(b * t, c)


def _layer(x, b, p, st, rope, table_block):
    n = x.shape[0]
    t = n // b
    fresh = st is None
    tiles_per_seq = t // TM if fresh else 0
    ca, cb, cc = _in_proj(x, p["norm_mix"], p["w_in"])
    ca3 = ca.reshape(b, t, COLS_A)
    xbc_tail = cc.reshape(b, t, COLS_C_PAD)[:, -(CONV_W - 1):, W_C:W_C + CONV_DIM]
    if fresh:
        prev, shifted = None, None
        conv_new = xbc_tail
    else:
        prev = _shifted(ca3, st["shift"][:, None, :], 1)
        xbc = cc[:, W_C:W_C + CONV_DIM].reshape(b, t, CONV_DIM)
        shifted = [_shifted(xbc, st["conv"], j) for j in range(1, CONV_W)]
        conv_new = jnp.concatenate([st["conv"], xbc_tail], axis=1)[:, -(CONV_W - 1):]

    r, dec, k2, v, nkk, beta, ga, bonus = _rwkv_prep(ca, prev, p, tiles_per_seq)
    qh, kh = _ret_prep(cb, rope[0], rope[1], table_block)
    xdt, bm, cm, dssm, xs, da = _ssm_prep(cc, shifted, p, tiles_per_seq)

    feats = (r, dec, k2, v, nkk, beta)
    if fresh:
        ya, s_rwkv = _rwkv_long(feats, b)
        yb, s_full = _ret_chunk(qh, kh, cb, b, t)
        s_ret = jnp.stack([s_full[:, h * HEAD_DIM:(h + 1) * HEAD_DIM, h * HEAD_DIM:(h + 1) * HEAD_DIM]
                           for h in range(H_B)], axis=1)
        yc, s_ssm = _ssd_chunk(xdt, bm, cm, da, p["expand"], b, t)
        s_ssm = s_ssm.reshape(b, H_C, HEAD_DIM, N_STATE)
    else:
        ya, s_rwkv = _rwkv_short(feats, b, st["rwkv"])
        yb, s_ret = _ret_short(qh, kh, cb[:, 2 * W_B:3 * W_B], b, st["ret"])
        yc, s_ssm = _ssd_short(xdt, bm, cm, dssm, b, st["ssm"])

    x2 = _post(x, ya, bonus, ga, yb, cb, yc, xs, cc, p)
    x = _ffn(x2, p["norm_ffn"], p["wg"], p["wu"], p["wd"], router=p["router"], nfin=p["norm_final"])
    return x, (s_rwkv, ca3[:, -1], s_ret, s_ssm, conv_new)


def kernel(x_prompt, x_sample, state_rwkv, state_shift, state_ret, state_ssm, state_conv, norm_mix, w_in, rwkv_mu, rwkv_w0, rwkv_w_up, rwkv_a0, rwkv_a_up, rwkv_g_up, rwkv_k_k, rwkv_k_a, rwkv_r_k, rwkv_ln_w, rwkv_ln_b, ret_norm, ssm_conv_w, ssm_conv_b, ssm_dt_bias, ssm_a_log, ssm_d, ssm_norm, w_out, norm_ffn, ffn_w_gate, ffn_w_up, ffn_w_down, moe_router, moe_w_gate, moe_w_up, moe_w_down, norm_final):
    bp, tp, _ = x_prompt.shape
    bs, ts, _ = x_sample.shape
    depth = w_in.shape[0]
    assert tp % TM == 0 and tp % LC == 0 and (bs * ts) % TM == 0 and TM % ts == 0

    rope = _rope_tables(tp, ts)
    bd_a, bd_b = _block_diag_ones(W_A), _block_diag_ones(W_B)
    expand = np.zeros((LANES, W_C), np.float32)
    expand[np.arange(W_C) // HEAD_DIM, np.arange(W_C)] = 1.0
    expand = jnp.asarray(expand)
    row = lambda v: v.reshape(1, -1)
    pad_l = lambda v: jnp.pad(v, (0, LANES - v.shape[0])).reshape(1, LANES)

    xp = x_prompt.reshape(bp * tp, D_MODEL)
    xs = x_sample.reshape(bs * ts, D_MODEL)
    new_p, new_s = [], []
    for i in range(depth):
        j = i // 2
        p = dict(
            norm_mix=row(norm_mix[i]),
            w_in=jnp.pad(w_in[i], ((0, 0), (0, COLS_C_PAD - COLS_C))).astype(BF16),
            mu=row(rwkv_mu[i]), w0=row(rwkv_w0[i]), a0=row(rwkv_a0[i]), k_k=row(rwkv_k_k[i]),
            k_a=row(rwkv_k_a[i]), r_k=row(rwkv_r_k[i]),
            w_up=jnp.pad(rwkv_w_up[i], ((0, AAA_LORA), (0, 0))).astype(BF16),
            a_up=jnp.pad(rwkv_a_up[i], ((DECAY_LORA, 0), (0, 0))).astype(BF16),
            g_up=rwkv_g_up[i].astype(BF16), bd_a=bd_a, bd_b=bd_b,
            conv_w=ssm_conv_w[i], conv_b=row(ssm_conv_b[i]), dt_bias=pad_l(ssm_dt_bias[i]),
            a_log=pad_l(ssm_a_log[i]), expand=expand,
            ln_w=row(rwkv_ln_w[i]), ln_b=row(rwkv_ln_b[i]), ret_norm=row(ret_norm[i]),
            d_skip=row(jnp.repeat(ssm_d[i], HEAD_DIM)), ssm_norm=row(ssm_norm[i]), w_out=w_out[i].astype(BF16),
            norm_ffn=row(norm_ffn[i]), norm_final=row(norm_final) if i == depth - 1 else None)
        if i % 2 == 0:
            dff = ffn_w_gate.shape[-1] // 2
            p.update(router=None,
                     wg=ffn_w_gate[j].reshape(D_MODEL, 2, dff).transpose(1, 0, 2).astype(BF16),
                     wu=ffn_w_up[j].reshape(D_MODEL, 2, dff).transpose(1, 0, 2).astype(BF16),
                     wd=ffn_w_down[j].reshape(2, dff, D_MODEL).astype(BF16))
        else:
            p.update(router=jnp.pad(moe_router[j], ((0, 0), (0, LANES - N_EXPERTS))),
                     wg=moe_w_gate[j].astype(BF16), wu=moe_w_up[j].astype(BF16), wd=moe_w_down[j].astype(BF16))

        xp, st_p = _layer(xp, bp, p, None, rope, lambda t: t % (tp // TM))
        st = dict(rwkv=state_rwkv[i], shift=state_shift[i], ret=state_ret[i], ssm=state_ssm[i],
                  conv=state_conv[i])
        xs, st_s = _layer(xs, bs, p, st, rope, lambda t: tp // TM)
        new_p.append(st_p)
        new_s.append(st_s)

    stack = lambda sts: tuple(jnp.stack(s) for s in zip(*sts))
    return (xp.reshape(bp, tp, D_MODEL), xs.reshape(bs, ts, D_MODEL)) + stack(new_p) + stack(new_s)
```

```python
import functools
import math

import numpy as np
import jax
import jax.numpy as jnp
from jax import lax
from jax.experimental import pallas as pl
from jax.experimental.pallas import tpu as pltpu

F32 = jnp.float32
BF16 = jnp.bfloat16
HIGHEST = lax.Precision.HIGHEST

LANES = 128
SUBLANES = 8
VMEM_LIMIT = 56 * 1024 * 1024

D_MODEL = 1024
HEAD_DIM = 64
H_A, H_B, H_C = 6, 4, 6
W_A, W_B, W_C = H_A * HEAD_DIM, H_B * HEAD_DIM, H_C * HEAD_DIM
DECAY_LORA, AAA_LORA, GATE_LORA = 64, 64, 128
COLS_A = 3 * W_A + DECAY_LORA + AAA_LORA + GATE_LORA
COLS_B = 4 * W_B
N_STATE, N_GROUPS, CONV_W = 128, 2, 4
CONV_DIM = W_C + 2 * N_GROUPS * N_STATE
COLS_C = W_C + CONV_DIM + H_C
COLS_C_PAD = 1408
ROPE_BASE = 10000.0
RMS_EPS = 1e-6
GN_EPS = 64e-5
GATED_NORM_EPS = 1e-5
N_EXPERTS = 8
PAST_LEN = 16384

TM = 512
LC = 256
H_PAD = 8
K_SPLIT = 2


def _dot(a, b):
    return jnp.dot(a.astype(BF16), b.astype(BF16), preferred_element_type=F32)


def _dot_hi(a, b):
    return jnp.dot(a, b, precision=HIGHEST, preferred_element_type=F32)


def _sigmoid(x):
    return 1.0 / (1.0 + jnp.exp(-x))


def _softplus(x):
    return jnp.maximum(x, 0.0) + jnp.log1p(jnp.exp(-jnp.abs(x)))


def _rmsnorm(x, g, eps):
    return x * lax.rsqrt(jnp.mean(x * x, axis=-1, keepdims=True) + eps) * g


def _params(sem):
    return pltpu.CompilerParams(dimension_semantics=sem, vmem_limit_bytes=VMEM_LIMIT)


def _row_spec(width, col=0):
    return pl.BlockSpec((TM, width), lambda i, c=col: (i, c))


def _before_spec(width):
    return pl.BlockSpec((SUBLANES, width), lambda i: (jnp.maximum(i * (TM // SUBLANES) - 1, 0), 0))


def _full_spec(shape):
    nd = len(shape)
    return pl.BlockSpec(shape, lambda i, n=nd: (0,) * n)


def _shift_rows(x, before, j, first):
    rolled = pltpu.roll(x, j, 0)
    prev = jnp.where(first, 0.0, pltpu.roll(before, j, 0))
    row = lax.broadcasted_iota(jnp.int32, prev.shape, 0)
    top = jnp.where(row < j, prev, rolled[0:SUBLANES])
    return jnp.concatenate([top, rolled[SUBLANES:]], axis=0)


def _in_proj_kernel(x_ref, g_ref, w_ref, oa_ref, ob_ref, oc_ref):
    h = _rmsnorm(x_ref[...], g_ref[...], RMS_EPS).astype(BF16)
    oa_ref[...] = jnp.dot(h, w_ref[:, 0:COLS_A], preferred_element_type=F32)
    ob_ref[...] = jnp.dot(h, w_ref[:, COLS_A:COLS_A + COLS_B], preferred_element_type=F32)
    oc_ref[...] = jnp.dot(h, w_ref[:, COLS_A + COLS_B:], preferred_element_type=F32)


def _in_proj(x, g, w):
    n = x.shape[0]
    wtot = w.shape[1]
    return pl.pallas_call(
        _in_proj_kernel,
        grid=(n // TM,),
        in_specs=[_row_spec(D_MODEL), _full_spec((1, D_MODEL)), _full_spec((D_MODEL, wtot))],
        out_specs=[_row_spec(COLS_A), _row_spec(COLS_B), _row_spec(COLS_C_PAD)],
        out_shape=[jax.ShapeDtypeStruct((n, COLS_A), F32), jax.ShapeDtypeStruct((n, COLS_B), F32),
                   jax.ShapeDtypeStruct((n, COLS_C_PAD), F32)],
        compiler_params=_params(("parallel",)),
        name="in_proj",
    )(x, g, w)


def _rwkv_prep_kernel(c_ref, p_ref, mu_ref, w0_ref, a0_ref, kk_ref, ka_ref, rk_ref, wup_ref, aup_ref,
                      gup_ref, bd_ref, r_o, d_o, k_o, v_o, n_o, b_o, g_o, bonus_o, *, tiles_per_seq):
    c = c_ref[...]
    if tiles_per_seq:
        prev = _shift_rows(c, p_ref[...], 1, pl.program_id(0) % tiles_per_seq == 0)
    else:
        prev = p_ref[...]
    xm = c + (prev - c) * mu_ref[...]
    r = xm[:, 0:W_A]
    k = xm[:, W_A:2 * W_A]
    v = xm[:, 2 * W_A:3 * W_A]
    lora = xm[:, 3 * W_A:3 * W_A + DECAY_LORA + AAA_LORA]
    gd = xm[:, 3 * W_A + DECAY_LORA + AAA_LORA:]
    w = w0_ref[...] + _dot(jnp.tanh(lora), wup_ref[...])
    decay = jnp.exp(-math.exp(-0.5) * _sigmoid(w))
    a = _sigmoid(a0_ref[...] + _dot(lora, aup_ref[...]))
    g = _dot(_sigmoid(gd), gup_ref[...])
    bd = bd_ref[...]
    kk = k * kk_ref[...]
    kk = kk / jnp.maximum(jnp.sqrt(_dot_hi(kk * kk, bd)), 1e-12)
    k2 = k * (1.0 + (a - 1.0) * ka_ref[...])
    for o_ref, val in ((r_o, r), (d_o, decay), (k_o, k2), (v_o, v), (n_o, -kk), (b_o, kk * a)):
        o_ref[:, 0:W_A] = val
        if o_ref.shape[1] > W_A:
            o_ref[:, W_A:] = jnp.zeros((TM, o_ref.shape[1] - W_A), F32)
    g_o[...] = g
    bonus_o[...] = _dot_hi(r * k2 * rk_ref[...], bd) * v


def _rwkv_prep(ca, prev, p, tiles_per_seq):
    n = ca.shape[0]
    vec = _full_spec((1, W_A))
    lora_spec = _full_spec((DECAY_LORA + AAA_LORA, W_A))
    p_spec = _before_spec(COLS_A) if tiles_per_seq else _row_spec(COLS_A)
    w_scan = H_PAD * HEAD_DIM if tiles_per_seq else W_A
    widths = [w_scan] * 6 + [W_A] * 2
    return pl.pallas_call(
        functools.partial(_rwkv_prep_kernel, tiles_per_seq=tiles_per_seq),
        grid=(n // TM,),
        in_specs=[_row_spec(COLS_A), p_spec, _full_spec((1, COLS_A)), vec, vec, vec, vec, vec,
                  lora_spec, lora_spec, _full_spec((GATE_LORA, W_A)), _full_spec((W_A, W_A))],
        out_specs=[_row_spec(w) for w in widths],
        out_shape=[jax.ShapeDtypeStruct((n, w), F32) for w in widths],
        compiler_params=_params(("parallel",)),
        name="rwkv_prep",
    )(ca, ca if tiles_per_seq else prev, p["mu"], p["w0"], p["a0"], p["k_k"], p["k_a"], p["r_k"],
      p["w_up"], p["a_up"], p["g_up"], p["bd_a"])


def _rope_kernel(ang_ref, cos_o, sin_o):
    ang = ang_ref[...]
    lane = lax.broadcasted_iota(jnp.int32, ang.shape, 1)
    cos_o[...] = jnp.cos(ang)
    sin_o[...] = jnp.where((lane % HEAD_DIM) < (HEAD_DIM // 2), -jnp.sin(ang), jnp.sin(ang))


def _rope_tables(t_prompt, t_sample):
    theta = 1.0 / (ROPE_BASE ** jnp.linspace(0.0, 1.0, HEAD_DIM // 2, dtype=F32))
    pos = jnp.concatenate([jnp.arange(t_prompt, dtype=F32), PAST_LEN + (jnp.arange(TM) % t_sample).astype(F32)])
    ang = jnp.tile(pos[:, None] * theta[None, :], (1, W_B // (HEAD_DIM // 2)))
    n = ang.shape[0]
    return pl.pallas_call(
        _rope_kernel,
        grid=(n // TM,),
        in_specs=[_row_spec(W_B)],
        out_specs=[_row_spec(W_B)] * 2,
        out_shape=[jax.ShapeDtypeStruct((n, W_B), F32)] * 2,
        compiler_params=_params(("parallel",)),
        name="rope_tables",
    )(ang)


def _ret_prep_kernel(q_ref, k_ref, cos_ref, sin_ref, q_o, k_o):
    cos = cos_ref[...]
    sin = sin_ref[...]
    lane = lax.broadcasted_iota(jnp.int32, (TM, W_B), 1)
    first_half = (lane % HEAD_DIM) < (HEAD_DIM // 2)

    def rope(x):
        partner = jnp.where(first_half, pltpu.roll(x, W_B - HEAD_DIM // 2, 1), pltpu.roll(x, HEAD_DIM // 2, 1))
        return x * cos + partner * sin

    q_o[...] = rope(q_ref[...])
    k_o[...] = rope(k_ref[...]) * (HEAD_DIM ** -0.5)


def _ret_prep(cb, cos, sin, table_block):
    n = cb.shape[0]
    tab = pl.BlockSpec((TM, W_B), lambda i: (table_block(i), 0))
    return pl.pallas_call(
        _ret_prep_kernel,
        grid=(n // TM,),
        in_specs=[_row_spec(W_B, 0), _row_spec(W_B, 1), tab, tab],
        out_specs=[_row_spec(W_B)] * 2,
        out_shape=[jax.ShapeDtypeStruct((n, W_B), F32)] * 2,
        compiler_params=_params(("parallel",)),
        name="ret_prep",
    )(cb, cb, cos, sin)


def _ssm_prep_kernel(*refs, tiles_per_seq):
    n_shift_refs = 1 if tiles_per_seq else CONV_W - 1
    cc_ref = refs[0]
    shift_refs = refs[1:1 + n_shift_refs]
    cw_ref, cb_ref, dtb_ref, alog_ref, ex_ref, xdt_o, b_o, c_o, dec_o, xs_o, da_o = refs[1 + n_shift_refs:]
    cc = cc_ref[...]
    x0 = cc[:, W_C:W_C + CONV_DIM]
    if tiles_per_seq:
        before = shift_refs[0][:, W_C:W_C + CONV_DIM]
        first = pl.program_id(0) % tiles_per_seq == 0
        xs_prev = [_shift_rows(x0, before, j, first) for j in range(1, CONV_W)]
    else:
        xs_prev = [r[...] for r in shift_refs]
    cw = cw_ref[...]
    conv = x0 * cw[CONV_W - 1:CONV_W, :] + cb_ref[...]
    for j in range(1, CONV_W):
        conv = conv + xs_prev[j - 1] * cw[CONV_W - 1 - j:CONV_W - j, :]
    act = conv * _sigmoid(conv)
    xs = act[:, 0:W_C]
    dt = _softplus(cc[:, W_C + CONV_DIM:] + dtb_ref[...])
    da = dt * (-jnp.exp(alog_ref[...]))
    da_o[...] = da
    dec_o[...] = jnp.exp(da)
    xdt_o[...] = xs * _dot_hi(dt, ex_ref[...])
    b_o[...] = act[:, W_C:W_C + N_GROUPS * N_STATE]
    c_o[...] = act[:, W_C + N_GROUPS * N_STATE:]
    xs_o[...] = xs


def _ssm_prep(cc, shifted, p, tiles_per_seq):
    n = cc.shape[0]
    gn = N_GROUPS * N_STATE
    if tiles_per_seq:
        shift_args, shift_specs = [cc], [_before_spec(COLS_C_PAD)]
    else:
        shift_args, shift_specs = list(shifted), [_row_spec(CONV_DIM)] * (CONV_W - 1)
    return pl.pallas_call(
        functools.partial(_ssm_prep_kernel, tiles_per_seq=tiles_per_seq),
        grid=(n // TM,),
        in_specs=[_row_spec(COLS_C_PAD)] + shift_specs + [
            _full_spec((CONV_W, CONV_DIM)), _full_spec((1, CONV_DIM)), _full_spec((1, LANES)),
            _full_spec((1, LANES)), _full_spec((LANES, W_C))],
        out_specs=[_row_spec(W_C), _row_spec(gn), _row_spec(gn), _row_spec(LANES), _row_spec(W_C),
                   _row_spec(LANES)],
        out_shape=[jax.ShapeDtypeStruct((n, W_C), F32), jax.ShapeDtypeStruct((n, gn), F32),
                   jax.ShapeDtypeStruct((n, gn), F32), jax.ShapeDtypeStruct((n, LANES), F32),
                   jax.ShapeDtypeStruct((n, W_C), F32), jax.ShapeDtypeStruct((n, LANES), F32)],
        compiler_params=_params(("parallel",)),
        name="ssm_prep",
    )(cc, *shift_args, p["conv_w"], p["conv_b"], p["dt_bias"], p["a_log"], p["expand"])


def _scan_kernel(*refs, mode, ni, tt_len, n_tt):
    if mode == "rwkv":
        q_ref, d_ref, a_ref, b_ref, n_ref, a2_ref, s0_ref, y_ref, so_ref, st = refs
    else:
        q_ref, d_ref, a_ref, b_ref, s0_ref, y_ref, so_ref, st = refs
    head = pl.program_id(0)
    tt = pl.program_id(1)

    @pl.when(tt == 0)
    def _():
        st[...] = s0_ref[...]

    def row(ref, t, i):
        return ref[t, pl.ds(i, 1), :]

    def step(t, carry):
        bv = b_ref[t]
        if mode == "rwkv":
            sa_parts = [jnp.zeros_like(bv), jnp.zeros_like(bv)]
            for i in range(ni):
                sa_parts[i % 2] = sa_parts[i % 2] + st[i] * row(n_ref, t, i)
            sa = sa_parts[0] + sa_parts[1]
        elif mode == "ssd":
            d = d_ref[t, pl.ds(head, 1), :]
        else:
            d = d_ref[pl.ds(head, 1), :]
        y_parts = [jnp.zeros_like(bv), jnp.zeros_like(bv)]
        for i in range(ni):
            if mode == "rwkv":
                s = st[i] * row(d_ref, t, i) + row(a_ref, t, i) * bv + row(a2_ref, t, i) * sa
            else:
                s = st[i] * d + row(a_ref, t, i) * bv
            st[i] = s
            y_parts[i % 2] = y_parts[i % 2] + s * row(q_ref, t, i)
        y_ref[t] = y_parts[0] + y_parts[1]
        return carry

    lax.fori_loop(0, tt_len, step, 0)

    @pl.when(tt == n_tt - 1)
    def _():
        so_ref[...] = st[...]


def _scan(mode, n_heads, ni, nj, q, d, a, b, s0, n=None, a2=None):
    t_len = b[0].shape[0]

    def rows(op, width):
        arr, block_of = op
        return arr, pl.BlockSpec((t_len, width, LANES), lambda h, t, f=block_of: (0, f(h), 0))

    ops = [rows(q, ni)]
    if mode == "rwkv":
        ops.append(rows(d, ni))
    elif mode == "ssd":
        ops.append((d, pl.BlockSpec((t_len, SUBLANES, LANES), lambda h, t: (0, 0, 0))))
    else:
        ops.append((d, pl.BlockSpec((SUBLANES, LANES), lambda h, t: (0, 0))))
    ops += [rows(a, ni), rows(b, nj)]
    if mode == "rwkv":
        ops += [rows(n, ni), rows(a2, ni)]
    s_spec = pl.BlockSpec((ni, nj, LANES), lambda h, t: (h, 0, 0))
    y_spec = pl.BlockSpec((t_len, nj, LANES), lambda h, t: (0, h, 0))
    return pl.pallas_call(
        functools.partial(_scan_kernel, mode=mode, ni=ni, tt_len=t_len, n_tt=1),
        grid=(n_heads, 1),
        in_specs=[spec for _, spec in ops] + [s_spec],
        out_specs=[y_spec, s_spec],
        out_shape=[jax.ShapeDtypeStruct((t_len, n_heads * nj, LANES), F32),
                   jax.ShapeDtypeStruct((n_heads * ni, nj, LANES), F32)],
        scratch_shapes=[pltpu.VMEM((ni, nj, LANES), F32)],
        compiler_params=_params(("parallel", "arbitrary")),
        name="scan_" + mode,
    )(*[arr for arr, _ in ops], s0)


def _rwkv_long_kernel(q_ref, d_ref, a_ref, b_ref, n1_ref, a2_ref, y_ref, so_ref, st, sa_s, *, ni, tt_len, n_tt):
    tt = pl.program_id(1)

    @pl.when(tt == 0)
    def _():
        st[...] = jnp.zeros_like(st)
        sa_s[...] = jnp.zeros_like(sa_s)

    def row(ref, t, i):
        return ref[t, pl.ds(i, 1), :]

    def fold(x):
        return x + pltpu.roll(x, LANES // 2, 1)

    def step(t, sa):
        bv = b_ref[t]
        a, a2, q, n1 = a_ref[t], a2_ref[t], q_ref[t], n1_ref[t]
        aq, a2q, an, a2n = [fold(jnp.sum(u * w, axis=0, keepdims=True))
                            for u, w in ((a, q), (a2, q), (a, n1), (a2, n1))]
        y0 = [jnp.zeros_like(bv), jnp.zeros_like(bv)]
        n0 = [jnp.zeros_like(bv), jnp.zeros_like(bv)]
        for i in range(ni):
            sd = st[i] * row(d_ref, t, i)
            st[i] = sd
            y0[i % 2] = y0[i % 2] + sd * row(q_ref, t, i)
            n0[i % 2] = n0[i % 2] + sd * row(n1_ref, t, i)
        y_ref[t] = fold(y0[0] + y0[1]) + bv * aq + sa * a2q
        sa_next = fold(n0[0] + n0[1]) + bv * an + sa * a2n
        for i in range(ni):
            st[i] = st[i] + row(a_ref, t, i) * bv + row(a2_ref, t, i) * sa
        return sa_next

    sa_s[...] = lax.fori_loop(0, tt_len, step, sa_s[...])

    @pl.when(tt == n_tt - 1)
    def _():
        so_ref[...] = st[...]


def _rwkv_long_scan(q, d, a, b, n1, a2, tt_len=64):
    t_len, ni, _ = q.shape
    nj = b.shape[1]
    n_tt = t_len // tt_len
    i_spec = pl.BlockSpec((tt_len, ni, LANES), lambda g, t: (t, 0, 0))
    j_spec = pl.BlockSpec((tt_len, nj, LANES), lambda g, t: (t, 0, 0))
    s_spec = pl.BlockSpec((ni, nj, LANES), lambda g, t: (0, 0, 0))
    return pl.pallas_call(
        functools.partial(_rwkv_long_kernel, ni=ni, tt_len=tt_len, n_tt=n_tt),
        grid=(1, n_tt),
        in_specs=[i_spec, i_spec, i_spec, j_spec, i_spec, i_spec],
        out_specs=[j_spec, s_spec],
        out_shape=[jax.ShapeDtypeStruct((t_len, nj, LANES), F32), jax.ShapeDtypeStruct((ni, nj, LANES), F32)],
        scratch_shapes=[pltpu.VMEM((ni, nj, LANES), F32), pltpu.VMEM((nj, LANES), F32)],
        compiler_params=_params(("parallel", "arbitrary")),
        name="scan_rwkv_long",
    )(q, d, a, b, n1, a2)


TR = 128
HALF = LANES // K_SPLIT


def _to_lanes_kernel(*refs, n_ops, rows, per_half):
    x_refs, o_refs, y2d = refs[:n_ops], refs[n_ops:2 * n_ops], refs[2 * n_ops]
    n_b = x_refs[0].shape[0]
    for x_ref, o_ref in zip(x_refs, o_refs):
        for b in range(n_b):
            xt = x_ref[b].T
            for h in range(H_PAD):
                for s in range(K_SPLIT):
                    lane = s * HALF + b * H_PAD + h
                    f0 = h * HEAD_DIM + (s * rows if per_half else 0)
                    y2d[pl.ds(lane, rows, stride=LANES), :] = xt[f0:f0 + rows, :]
        for r in range(rows):
            o_ref[pl.ds(r, TR, stride=rows), :] = y2d[r * LANES:(r + 1) * LANES, :].T


def _long_to_lanes(xs, b, rows, per_half):
    n_ops = len(xs)
    t = xs[0].shape[0] // b
    width = H_PAD * HEAD_DIM
    outs = pl.pallas_call(
        functools.partial(_to_lanes_kernel, n_ops=n_ops, rows=rows, per_half=per_half),
        grid=(t // TR,),
        in_specs=[pl.BlockSpec((b, TR, width), lambda i: (0, i, 0))] * n_ops,
        out_specs=[pl.BlockSpec((TR * rows, LANES), lambda i: (i, 0))] * n_ops,
        out_shape=[jax.ShapeDtypeStruct((t * rows, LANES), F32)] * n_ops,
        scratch_shapes=[pltpu.VMEM((rows * LANES, TR), F32)],
        compiler_params=_params(("parallel",)),
        name="to_lanes",
    )(*[x.reshape(b, t, width) for x in xs])
    return [o.reshape(t, rows, LANES) for o in outs]


def _from_lanes_kernel(y_ref, o_ref, z2d):
    for v in range(HEAD_DIM):
        z2d[pl.ds(v, LANES, stride=HEAD_DIM), :] = y_ref[pl.ds(v, TR, stride=HEAD_DIM), :].T
    for b in range(o_ref.shape[0]):
        r0 = b * H_PAD * HEAD_DIM
        o_ref[b] = z2d[r0:r0 + W_A, :].T


def _long_from_lanes(y, b):
    t = y.shape[0]
    out = pl.pallas_call(
        _from_lanes_kernel,
        grid=(t // TR,),
        in_specs=[pl.BlockSpec((TR * HEAD_DIM, LANES), lambda i: (i, 0))],
        out_specs=pl.BlockSpec((b, TR, W_A), lambda i: (0, i, 0)),
        out_shape=jax.ShapeDtypeStruct((b, t, W_A), F32),
        scratch_shapes=[pltpu.VMEM((LANES * HEAD_DIM, TR), F32)],
        compiler_params=_params(("parallel",)),
        name="from_lanes",
    )(y.reshape(t * HEAD_DIM, LANES))
    return out.reshape(b * t, W_A)


def _ret_chunk_kernel(q_ref, k_ref, v_ref, y_ref, s_ref, st):
    c = pl.program_id(1)

    @pl.when(c == 0)
    def _():
        st[...] = jnp.zeros_like(st)

    q = q_ref[...]
    k = k_ref[...]
    vb = v_ref[...].astype(BF16)
    kb = k.astype(BF16)
    row = lax.broadcasted_iota(jnp.int32, (LC, W_B), 0)
    head = lax.broadcasted_iota(jnp.int32, (LC, W_B), 1) // HEAD_DIM
    log_g = [math.log(1.0 - 2.0 ** (-5.0 - h)) for h in range(H_B)]
    lg = jnp.full((LC, W_B), log_g[0], F32)
    for h in range(1, H_B):
        lg = jnp.where(head == h, log_g[h], lg)
    rowf = row.astype(F32)
    diff = (lax.broadcasted_iota(jnp.int32, (LC, LC), 0) - lax.broadcasted_iota(jnp.int32, (LC, LC), 1))
    causal = diff >= 0
    difff = jnp.maximum(diff, 0).astype(F32)

    out = jnp.dot((q * jnp.exp(lg * (rowf + 1.0))).astype(BF16), st[...].astype(BF16), preferred_element_type=F32)
    for h in range(H_B):
        qm = jnp.where(head == h, q, 0.0).astype(BF16)
        s = lax.dot_general(qm, kb, (((1,), (1,)), ((), ())), preferred_element_type=F32)
        p = jnp.where(causal, s * jnp.exp(log_g[h] * difff), 0.0).astype(BF16)
        out = out + jnp.where(head == h, jnp.dot(p, vb, preferred_element_type=F32), 0.0)
    y_ref[...] = out

    kt = (k * jnp.exp(lg * (LC - 1.0 - rowf))).T.astype(BF16)
    kv = jnp.dot(kt, vb, preferred_element_type=F32)
    r2 = lax.broadcasted_iota(jnp.int32, (W_B, W_B), 0) // HEAD_DIM
    c2 = lax.broadcasted_iota(jnp.int32, (W_B, W_B), 1) // HEAD_DIM
    cdec = jnp.full((W_B, W_B), math.exp(log_g[0] * LC), F32)
    for h in range(1, H_B):
        cdec = jnp.where(r2 == h, math.exp(log_g[h] * LC), cdec)
    st[...] = st[...] * cdec + jnp.where(r2 == c2, kv, 0.0)
    s_ref[0] = st[...]


def _ret_chunk(q, k, cb, n_seq, t_len):
    n_c = t_len // LC
    rows = lambda col: pl.BlockSpec((LC, W_B), lambda b, c, col=col: (b * n_c + c, col))
    return pl.pallas_call(
        _ret_chunk_kernel,
        grid=(n_seq, n_c),
        in_specs=[rows(0), rows(0), rows(2)],
        out_specs=[rows(0), pl.BlockSpec((1, W_B, W_B), lambda b, c: (b, 0, 0))],
        out_shape=[jax.ShapeDtypeStruct((n_seq * t_len, W_B), F32), jax.ShapeDtypeStruct((n_seq, W_B, W_B), F32)],
        scratch_shapes=[pltpu.VMEM((W_B, W_B), F32)],
        compiler_params=_params(("parallel", "arbitrary")),
        name="ret_chunk",
    )(q, k, cb)


def _ssd_chunk_kernel(x_ref, b_ref, c_ref, da_ref, ex_ref, y_ref, h_ref, st):
    ci = pl.program_id(1)

    @pl.when(ci == 0)
    def _():
        st[...] = jnp.zeros_like(st)

    gn = N_STATE
    half = W_C // N_GROUPS
    ex = ex_ref[...]
    x = x_ref[...]
    xb = x.astype(BF16)
    ri = lax.broadcasted_iota(jnp.int32, (LC, LC), 0)
    cj = lax.broadcasted_iota(jnp.int32, (LC, LC), 1)
    causal = ri >= cj
    cum = _dot_hi(causal.astype(F32), da_ref[...])
    cum_e = _dot_hi(cum, ex)
    last_e = cum_e[LC - 1:LC, :]
    cum_t = cum.T
    head = lax.broadcasted_iota(jnp.int32, (LC, W_C), 1) // HEAD_DIM
    bg = [b_ref[:, g * gn:(g + 1) * gn].astype(BF16) for g in range(N_GROUPS)]
    cg = [c_ref[:, g * gn:(g + 1) * gn].astype(BF16) for g in range(N_GROUPS)]
    cb = [lax.dot_general(cg[g], bg[g], (((1,), (1,)), ((), ())), preferred_element_type=F32)
          for g in range(N_GROUPS)]

    hb = st[...].astype(BF16)
    ys = [lax.dot_general(cg[g], hb, (((1,), (1,)), ((), ())), preferred_element_type=F32)
          for g in range(N_GROUPS)]
    y = jnp.where(head < H_C // N_GROUPS, ys[0], ys[1]) * jnp.exp(cum_e)
    for h in range(H_C):
        seg = cum[:, h:h + 1] - cum_t[h:h + 1, :]
        p = (cb[h // (H_C // N_GROUPS)] * jnp.exp(jnp.where(causal, seg, -jnp.inf))).astype(BF16)
        y = y + jnp.where(head == h, jnp.dot(p, xb, preferred_element_type=F32), 0.0)
    y_ref[...] = y

    xt = (x * jnp.exp(last_e - cum_e)).T.astype(BF16)
    upd = [jnp.dot(xt, bg[g], preferred_element_type=F32) for g in range(N_GROUPS)]
    rowi = lax.broadcasted_iota(jnp.int32, (W_C, gn), 0)
    sel = lax.broadcasted_iota(jnp.int32, (W_C, LANES), 1) == lax.broadcasted_iota(jnp.int32, (W_C, LANES), 0) // HEAD_DIM
    tot = jnp.sum(jnp.where(sel, cum[LC - 1:LC, :], 0.0), axis=1, keepdims=True)
    st[...] = st[...] * jnp.exp(tot) + jnp.where(rowi < half, upd[0], upd[1])
    h_ref[0] = st[...]


def _ssd_chunk(xdt, bm, cm, da, ex, n_seq, t_len):
    n_c = t_len // LC
    gn = N_GROUPS * N_STATE
    rows = lambda w: pl.BlockSpec((LC, w), lambda b, c: (b * n_c + c, 0))
    return pl.pallas_call(
        _ssd_chunk_kernel,
        grid=(n_seq, n_c),
        in_specs=[rows(W_C), rows(gn), rows(gn), rows(LANES), pl.BlockSpec((LANES, W_C), lambda b, c: (0, 0))],
        out_specs=[rows(W_C), pl.BlockSpec((1, W_C, N_STATE), lambda b, c: (b, 0, 0))],
        out_shape=[jax.ShapeDtypeStruct((n_seq * t_len, W_C), F32),
                   jax.ShapeDtypeStruct((n_seq, W_C, N_STATE), F32)],
        scratch_shapes=[pltpu.VMEM((W_C, N_STATE), F32)],
        compiler_params=_params(("parallel", "arbitrary")),
        name="ssd_chunk",
    )(xdt, bm, cm, da, ex)


def _post_kernel(x_ref, ya_ref, bonus_ref, ga_ref, ob_ref, gb_ref, yc_ref, xs_ref, z_ref, lnw_ref, lnb_ref,
                 rn_ref, dsk_ref, sn_ref, bda_ref, bdb_ref, wo_ref, out_ref):
    inv_hd = 1.0 / HEAD_DIM
    y = ya_ref[...]
    bda = bda_ref[...]
    mean = _dot_hi(y, bda) * inv_hd
    yd = y - mean
    var = _dot_hi(yd * yd, bda) * inv_hd
    ya = (yd * lax.rsqrt(var + GN_EPS) * lnw_ref[...] + lnb_ref[...] + bonus_ref[...]) * ga_ref[...]
    o = ob_ref[...]
    ms = _dot_hi(o * o, bdb_ref[...]) * inv_hd
    gb = gb_ref[...]
    yb = (gb * _sigmoid(gb)) * (o * lax.rsqrt(ms + RMS_EPS) * rn_ref[...])
    z = z_ref[...]
    yc = (yc_ref[...] + dsk_ref[...] * xs_ref[...]) * (z * _sigmoid(z))
    yc = _rmsnorm(yc, sn_ref[...], GATED_NORM_EPS)
    out_ref[...] = (x_ref[...] + _dot(ya, wo_ref[0:W_A, :]) + _dot(yb, wo_ref[W_A:W_A + W_B, :])
                    + _dot(yc, wo_ref[W_A + W_B:, :]))


def _post(x, ya, bonus, ga, ob, cb, yc, xs, cc, p):
    n = x.shape[0]
    va = _full_spec((1, W_A))
    return pl.pallas_call(
        _post_kernel,
        grid=(n // TM,),
        in_specs=[_row_spec(D_MODEL), _row_spec(W_A), _row_spec(W_A), _row_spec(W_A), _row_spec(W_B),
                  _row_spec(W_B, 3), _row_spec(W_C), _row_spec(W_C), _row_spec(W_C, 0), va, va,
                  _full_spec((1, W_B)), va, va, _full_spec((W_A, W_A)), _full_spec((W_B, W_B)),
                  _full_spec((D_MODEL, D_MODEL))],
        out_specs=_row_spec(D_MODEL),
        out_shape=jax.ShapeDtypeStruct((n, D_MODEL), F32),
        compiler_params=_params(("parallel",)),
        name="post_outproj",
    )(x, ya, bonus, ga, ob, cb, yc, xs, cc, p["ln_w"], p["ln_b"], p["ret_norm"], p["d_skip"], p["ssm_norm"],
      p["bd_a"], p["bd_b"], p["w_out"])


def _ffn_kernel(*refs, gated, final_norm, n_e):
    refs = list(refs)
    x_ref, nf_ref = refs[:2]
    pos = 2
    if gated:
        router_ref = refs[pos]
        pos += 1
    wg_ref, wu_ref, wd_ref = refs[pos:pos + 3]
    pos += 3
    if final_norm:
        nfin_ref = refs[pos]
        pos += 1
    out_ref = refs[pos]
    hb_s, acc_s = refs[pos + 1:pos + 3]
    if gated:
        gate_s = refs[pos + 3]
    e = pl.program_id(1)

    @pl.when(e == 0)
    def _():
        h = _rmsnorm(x_ref[...], nf_ref[...], RMS_EPS)
        hb_s[...] = h.astype(BF16)
        acc_s[...] = jnp.zeros_like(acc_s)
        if gated:
            lane = lax.broadcasted_iota(jnp.int32, (TM, LANES), 1)
            logits = jnp.where(lane < N_EXPERTS, _dot_hi(h, router_ref[...]), -jnp.inf)
            p = jnp.exp(logits - jnp.max(logits, axis=-1, keepdims=True))
            p = p / jnp.sum(p, axis=-1, keepdims=True)
            p1 = jnp.max(p, axis=-1, keepdims=True)
            i1 = jnp.min(jnp.where(p == p1, lane, LANES), axis=-1, keepdims=True)
            rest = jnp.where(lane == i1, -1.0, p)
            p2 = jnp.max(rest, axis=-1, keepdims=True)
            i2 = jnp.min(jnp.where(rest == p2, lane, LANES), axis=-1, keepdims=True)
            gate_s[...] = jnp.where(lane == i1, p1, jnp.where(lane == i2, p2, 0.0)) / (p1 + p2)

    hb = hb_s[...]
    g = jnp.dot(hb, wg_ref[0], preferred_element_type=F32)
    u = jnp.dot(hb, wu_ref[0], preferred_element_type=F32)
    o = _dot(g * _sigmoid(g) * u, wd_ref[0])
    if gated:
        lane = lax.broadcasted_iota(jnp.int32, (TM, LANES), 1)
        o = o * jnp.sum(jnp.where(lane == e, gate_s[...], 0.0), axis=-1, keepdims=True)
    acc_s[...] += o

    @pl.when(e == n_e - 1)
    def _():
        y = x_ref[...] + acc_s[...]
        if final_norm:
            y = _rmsnorm(y, nfin_ref[...], RMS_EPS)
        out_ref[...] = y


def _ffn(x, nf, wg, wu, wd, router=None, nfin=None):
    n = x.shape[0]
    n_e, _, dff = wg.shape
    gated = router is not None
    final_norm = nfin is not None
    vec = pl.BlockSpec((1, D_MODEL), lambda i, e: (0, 0))
    args = [x, nf]
    specs = [pl.BlockSpec((TM, D_MODEL), lambda i, e: (i, 0)), vec]
    if gated:
        args.append(router)
        specs.append(pl.BlockSpec((D_MODEL, LANES), lambda i, e: (0, 0)))
    args += [wg, wu, wd]
    specs += [pl.BlockSpec((1, D_MODEL, dff), lambda i, e: (e, 0, 0)),
              pl.BlockSpec((1, D_MODEL, dff), lambda i, e: (e, 0, 0)),
              pl.BlockSpec((1, dff, D_MODEL), lambda i, e: (e, 0, 0))]
    if final_norm:
        args.append(nfin)
        specs.append(vec)
    scratch = [pltpu.VMEM((TM, D_MODEL), BF16), pltpu.VMEM((TM, D_MODEL), F32)]
    if gated:
        scratch.append(pltpu.VMEM((TM, LANES), F32))
    return pl.pallas_call(
        functools.partial(_ffn_kernel, gated=gated, final_norm=final_norm, n_e=n_e),
        grid=(n // TM, n_e),
        in_specs=specs,
        out_specs=pl.BlockSpec((TM, D_MODEL), lambda i, e: (i, 0)),
        out_shape=jax.ShapeDtypeStruct((n, D_MODEL), F32),
        scratch_shapes=scratch,
        compiler_params=_params(("parallel", "arbitrary")),
        name="moe" if gated else "ffn",
    )(*args)


def _short_to_lanes_kernel(*refs, n_ops, t_len):
    for x_ref, o_ref in zip(refs[:n_ops], refs[n_ops:]):
        for t in range(t_len):
            o_ref[t] = x_ref[pl.ds(t, LANES, stride=t_len), :].T


def _short_to_lanes(xs, t_len):
    n_chunks = [width // LANES for _, _, width in xs]
    grid = max(n_chunks)
    clamp = lambda c, n: jnp.minimum(c, n - 1)
    outs = pl.pallas_call(
        functools.partial(_short_to_lanes_kernel, n_ops=len(xs), t_len=t_len),
        grid=(grid,),
        in_specs=[pl.BlockSpec((LANES * t_len, LANES), lambda c, c0=col // LANES, n=n: (0, c0 + clamp(c, n)))
                  for (_, col, _), n in zip(xs, n_chunks)],
        out_specs=[pl.BlockSpec((t_len, LANES, LANES), lambda c, n=n: (0, clamp(c, n), 0)) for n in n_chunks],
        out_shape=[jax.ShapeDtypeStruct((t_len, width, LANES), F32) for _, _, width in xs],
        compiler_params=_params(("arbitrary",)),
        name="short_to_lanes",
    )(*[x for x, _, _ in xs])
    return outs


def _short_from_lanes_kernel(y_ref, o_ref, *, t_len):
    for t in range(t_len):
        o_ref[pl.ds(t, LANES, stride=t_len), :] = y_ref[t].T


def _short_from_lanes(y):
    t_len, w, _ = y.shape
    return pl.pallas_call(
        functools.partial(_short_from_lanes_kernel, t_len=t_len),
        grid=(w // LANES,),
        in_specs=[pl.BlockSpec((t_len, LANES, LANES), lambda c: (0, c, 0))],
        out_specs=pl.BlockSpec((LANES * t_len, LANES), lambda c: (0, c)),
        out_shape=jax.ShapeDtypeStruct((LANES * t_len, w), F32),
        compiler_params=_params(("parallel",)),
        name="short_from_lanes",
    )(y)


def _state_to_lanes_kernel(x_ref, o_ref, *, ni, nj, j_first):
    xt = x_ref[...].T
    if j_first:
        for j in range(nj):
            o_ref[pl.ds(j, ni, stride=nj), :] = xt[j * ni:(j + 1) * ni, :]
    else:
        o_ref[...] = xt


def _state_to_lanes(s, ni, nj, j_first):
    h = s.shape[1]
    out = pl.pallas_call(
        functools.partial(_state_to_lanes_kernel, ni=ni, nj=nj, j_first=j_first),
        grid=(h,),
        in_specs=[pl.BlockSpec((LANES, ni * nj), lambda i: (0, i))],
        out_specs=pl.BlockSpec((ni * nj, LANES), lambda i: (i, 0)),
        out_shape=jax.ShapeDtypeStruct((h * ni * nj, LANES), F32),
        compiler_params=_params(("parallel",)),
        name="state_to_lanes",
    )(s.reshape(LANES, h * ni * nj))
    return out.reshape(h * ni, nj, LANES)


def _state_from_lanes_kernel(s_ref, o_ref, tmp, *, ni, nj, j_first):
    if j_first:
        for j in range(nj):
            tmp[j * ni:(j + 1) * ni, :] = s_ref[pl.ds(j, ni, stride=nj), :]
        o_ref[...] = tmp[...].T
    else:
        o_ref[...] = s_ref[...].T


def _state_from_lanes(s, n_heads, ni, nj, j_first):
    out = pl.pallas_call(
        functools.partial(_state_from_lanes_kernel, ni=ni, nj=nj, j_first=j_first),
        grid=(n_heads,),
        in_specs=[pl.BlockSpec((ni * nj, LANES), lambda i: (i, 0))],
        out_specs=pl.BlockSpec((LANES, ni * nj), lambda i: (0, i)),
        out_shape=jax.ShapeDtypeStruct((LANES, n_heads * ni * nj), F32),
        scratch_shapes=[pltpu.VMEM((ni * nj, LANES), F32)],
        compiler_params=_params(("parallel",)),
        name="state_from_lanes",
    )(s.reshape(n_heads * ni * nj, LANES))
    return out.reshape((LANES, n_heads, nj, ni) if j_first else (LANES, n_heads, ni, nj))


def _rwkv_long(feats, b):
    r, dec, k2, v, nkk, beta = feats
    t = r.shape[0] // b
    kl = HEAD_DIM // K_SPLIT
    assert b * H_PAD == HALF and t % TR == 0
    nkk_next = jnp.concatenate([nkk[1:], jnp.zeros((1, nkk.shape[1]), F32)], axis=0)
    q, d, a, n1, a2 = _long_to_lanes([r, dec, k2, nkk_next, beta], b, kl, True)
    (vv,) = _long_to_lanes([v], b, HEAD_DIM, False)
    y, s = _rwkv_long_scan(q, d, a, vv, n1, a2)
    s = s.reshape(kl, HEAD_DIM, K_SPLIT, b, H_PAD)[..., :H_A]
    return _long_from_lanes(y, b), jnp.transpose(s, (3, 4, 1, 2, 0)).reshape(b, H_A, HEAD_DIM, HEAD_DIM)


def _rwkv_short(feats, t_len, s0):
    r, dec, k2, v, nkk, beta = _short_to_lanes([(f, 0, W_A) for f in feats], t_len)
    hd = HEAD_DIM
    own = lambda h: h
    y, s = _scan("rwkv", H_A, hd, hd, (r, own), (dec, own), (k2, own), (v, own),
                 _state_to_lanes(s0, hd, hd, True), n=(nkk, own), a2=(beta, own))
    return _short_from_lanes(y), _state_from_lanes(s, H_A, hd, hd, True)


def _ret_short(qh, kh, cb, t_len, s0):
    q, k, v = _short_to_lanes([(qh, 0, W_B), (kh, 0, W_B), (cb, 2 * W_B, W_B)], t_len)
    hd = HEAD_DIM
    gam = np.ones((SUBLANES, LANES), np.float32)
    gam[:H_B] = (1.0 - 2.0 ** (-5.0 - np.arange(H_B, dtype=np.float64)))[:, None]
    own = lambda h: h
    y, s = _scan("ret", H_B, hd, hd, (q, own), jnp.asarray(gam), (k, own), (v, own),
                 _state_to_lanes(s0, hd, hd, False))
    return _short_from_lanes(y), _state_from_lanes(s, H_B, hd, hd, False)


def _ssd_short(xdt, bm, cm, dssm, t_len, s0):
    gn = N_GROUPS * N_STATE
    x, bl, cl, dl = _short_to_lanes([(xdt, 0, W_C), (bm, 0, gn), (cm, 0, gn), (dssm, 0, LANES)], t_len)
    group = lambda h: h // (H_C // N_GROUPS)
    y, s = _scan("ssd", H_C, N_STATE, HEAD_DIM, (cl, group), dl, (bl, group), (x, lambda h: h),
                 _state_to_lanes(s0, N_STATE, HEAD_DIM, True))
    return _short_from_lanes(y), _state_from_lanes(s, H_C, N_STATE, HEAD_DIM, True)


def _block_diag_ones(width):
    idx = np.arange(width) // HEAD_DIM
    return jnp.asarray((idx[:, None] == idx[None, :]).astype(np.float32))


def _shifted(x, prev_rows, shift):
    b, t, c = x.shape
    p = prev_rows.shape[1]
    full = jnp.concatenate([prev_rows, x], axis=1)
    return full[:, p - shift:p - shift + t].reshape(b * t, c)


def _layer(x, b, p, st, rope, table_block):
    n = x.shape[0]
    t = n // b
    fresh = st is None
    tiles_per_seq = t // TM if fresh else 0
    ca, cb, cc = _in_proj(x, p["norm_mix"], p["w_in"])
    ca3 = ca.reshape(b, t, COLS_A)
    xbc_tail = cc.reshape(b, t, COLS_C_PAD)[:, -(CONV_W - 1):, W_C:W_C + CONV_DIM]
    if fresh:
        prev, shifted = None, None
        conv_new = xbc_tail
    else:
        prev = _shifted(ca3, st["shift"][:, None, :], 1)
        xbc = cc[:, W_C:W_C + CONV_DIM].reshape(b, t, CONV_DIM)
        shifted = [_shifted(xbc, st["conv"], j) for j in range(1, CONV_W)]
        conv_new = jnp.concatenate([st["conv"], xbc_tail], axis=1)[:, -(CONV_W - 1):]

    r, dec, k2, v, nkk, beta, ga, bonus = _rwkv_prep(ca, prev, p, tiles_per_seq)
    qh, kh = _ret_prep(cb, rope[0], rope[1], table_block)
    xdt, bm, cm, dssm, xs, da = _ssm_prep(cc, shifted, p, tiles_per_seq)

    feats = (r, dec, k2, v, nkk, beta)
    if fresh:
        ya, s_rwkv = _rwkv_long(feats, b)
        yb, s_full = _ret_chunk(qh, kh, cb, b, t)
        s_ret = jnp.stack([s_full[:, h * HEAD_DIM:(h + 1) * HEAD_DIM, h * HEAD_DIM:(h + 1) * HEAD_DIM]
                           for h in range(H_B)], axis=1)
        yc, s_ssm = _ssd_chunk(xdt, bm, cm, da, p["expand"], b, t)
        s_ssm = s_ssm.reshape(b, H_C, HEAD_DIM, N_STATE)
    else:
        assert b == LANES
        ya, s_rwkv = _rwkv_short(feats, t, st["rwkv"])
        yb, s_ret = _ret_short(qh, kh, cb, t, st["ret"])
        yc, s_ssm = _ssd_short(xdt, bm, cm, dssm, t, st["ssm"])

    x2 = _post(x, ya, bonus, ga, yb, cb, yc, xs, cc, p)
    x = _ffn(x2, p["norm_ffn"], p["wg"], p["wu"], p["wd"], router=p["router"], nfin=p["norm_final"])
    return x, (s_rwkv, ca3[:, -1], s_ret, s_ssm, conv_new)


def kernel(x_prompt, x_sample, state_rwkv, state_shift, state_ret, state_ssm, state_conv, norm_mix, w_in, rwkv_mu, rwkv_w0, rwkv_w_up, rwkv_a0, rwkv_a_up, rwkv_g_up, rwkv_k_k, rwkv_k_a, rwkv_r_k, rwkv_ln_w, rwkv_ln_b, ret_norm, ssm_conv_w, ssm_conv_b, ssm_dt_bias, ssm_a_log, ssm_d, ssm_norm, w_out, norm_ffn, ffn_w_gate, ffn_w_up, ffn_w_down, moe_router, moe_w_gate, moe_w_up, moe_w_down, norm_final):
    bp, tp, _ = x_prompt.shape
    bs, ts, _ = x_sample.shape
    depth = w_in.shape[0]
    assert tp % TM == 0 and tp % LC == 0 and (bs * ts) % TM == 0 and TM % ts == 0

    rope = _rope_tables(tp, ts)
    bd_a, bd_b = _block_diag_ones(W_A), _block_diag_ones(W_B)
    expand = np.zeros((LANES, W_C), np.float32)
    expand[np.arange(W_C) // HEAD_DIM, np.arange(W_C)] = 1.0
    expand = jnp.asarray(expand)
    row = lambda v: v.reshape(1, -1)
    pad_l = lambda v: jnp.pad(v, (0, LANES - v.shape[0])).reshape(1, LANES)

    xp = x_prompt.reshape(bp * tp, D_MODEL)
    xs = x_sample.reshape(bs * ts, D_MODEL)
    new_p, new_s = [], []
    for i in range(depth):
        j = i // 2
        p = dict(
            norm_mix=row(norm_mix[i]),
            w_in=jnp.pad(w_in[i], ((0, 0), (0, COLS_C_PAD - COLS_C))).astype(BF16),
            mu=row(rwkv_mu[i]), w0=row(rwkv_w0[i]), a0=row(rwkv_a0[i]), k_k=row(rwkv_k_k[i]),
            k_a=row(rwkv_k_a[i]), r_k=row(rwkv_r_k[i]),
            w_up=jnp.pad(rwkv_w_up[i], ((0, AAA_LORA), (0, 0))).astype(BF16),
            a_up=jnp.pad(rwkv_a_up[i], ((DECAY_LORA, 0), (0, 0))).astype(BF16),
            g_up=rwkv_g_up[i].astype(BF16), bd_a=bd_a, bd_b=bd_b,
            conv_w=ssm_conv_w[i], conv_b=row(ssm_conv_b[i]), dt_bias=pad_l(ssm_dt_bias[i]),
            a_log=pad_l(ssm_a_log[i]), expand=expand,
            ln_w=row(rwkv_ln_w[i]), ln_b=row(rwkv_ln_b[i]), ret_norm=row(ret_norm[i]),
            d_skip=row(jnp.repeat(ssm_d[i], HEAD_DIM)), ssm_norm=row(ssm_norm[i]), w_out=w_out[i].astype(BF16),
            norm_ffn=row(norm_ffn[i]), norm_final=row(norm_final) if i == depth - 1 else None)
        if i % 2 == 0:
            dff = ffn_w_gate.shape[-1] // 2
            p.update(router=None,
                     wg=ffn_w_gate[j].reshape(D_MODEL, 2, dff).transpose(1, 0, 2).astype(BF16),
                     wu=ffn_w_up[j].reshape(D_MODEL, 2, dff).transpose(1, 0, 2).astype(BF16),
                     wd=ffn_w_down[j].reshape(2, dff, D_MODEL).astype(BF16))
        else:
            p.update(router=jnp.pad(moe_router[j], ((0, 0), (0, LANES - N_EXPERTS))),
                     wg=moe_w_gate[j].astype(BF16), wu=moe_w_up[j].astype(BF16), wd=moe_w_down[j].astype(BF16))

        xp, st_p = _layer(xp, bp, p, None, rope, lambda t: t % (tp // TM))
        st = dict(rwkv=state_rwkv[i], shift=state_shift[i], ret=state_ret[i], ssm=state_ssm[i],
                  conv=state_conv[i])
        xs, st_s = _layer(xs, bs, p, st, rope, lambda t: tp // TM)
        new_p.append(st_p)
        new_s.append(st_s)

    stack = lambda sts: tuple(jnp.stack(s) for s in zip(*sts))
    return (xp.reshape(bp, tp, D_MODEL), xs.reshape(bs, ts, D_MODEL)) + stack(new_p) + stack(new_s)
```

```python
import functools
import math

import numpy as np
import jax
import jax.numpy as jnp
from jax import lax
from jax.experimental import pallas as pl
from jax.experimental.pallas import tpu as pltpu

F32 = jnp.float32
BF16 = jnp.bfloat16
HIGHEST = lax.Precision.HIGHEST

LANES = 128
SUBLANES = 8
VMEM_LIMIT = 56 * 1024 * 1024

D_MODEL = 1024
HEAD_DIM = 64
H_A, H_B, H_C = 6, 4, 6
W_A, W_B, W_C = H_A * HEAD_DIM, H_B * HEAD_DIM, H_C * HEAD_DIM
DECAY_LORA, AAA_LORA, GATE_LORA = 64, 64, 128
COLS_A = 3 * W_A + DECAY_LORA + AAA_LORA + GATE_LORA
COLS_B = 4 * W_B
N_STATE, N_GROUPS, CONV_W = 128, 2, 4
CONV_DIM = W_C + 2 * N_GROUPS * N_STATE
COLS_C = W_C + CONV_DIM + H_C
COLS_C_PAD = 1408
ROPE_BASE = 10000.0
RMS_EPS = 1e-6
GN_EPS = 64e-5
GATED_NORM_EPS = 1e-5
N_EXPERTS = 8
PAST_LEN = 16384

TM = 512
MOE_CHUNK = 128
LC = 256
H_PAD = 8
K_SPLIT = 2


def _dot(a, b):
    return jnp.dot(a.astype(BF16), b.astype(BF16), preferred_element_type=F32)


def _dot_hi(a, b):
    return jnp.dot(a, b, precision=HIGHEST, preferred_element_type=F32)


def _sigmoid(x):
    return 1.0 / (1.0 + jnp.exp(-x))


def _softplus(x):
    return jnp.maximum(x, 0.0) + jnp.log1p(jnp.exp(-jnp.abs(x)))


def _rmsnorm(x, g, eps):
    return x * lax.rsqrt(jnp.mean(x * x, axis=-1, keepdims=True) + eps) * g


def _params(sem):
    return pltpu.CompilerParams(dimension_semantics=sem, vmem_limit_bytes=VMEM_LIMIT)


def _row_spec(width, col=0):
    return pl.BlockSpec((TM, width), lambda i, c=col: (i, c))


def _before_spec(width):
    return pl.BlockSpec((SUBLANES, width), lambda i: (jnp.maximum(i * (TM // SUBLANES) - 1, 0), 0))


def _full_spec(shape):
    nd = len(shape)
    return pl.BlockSpec(shape, lambda i, n=nd: (0,) * n)


def _shift_rows(x, before, j, first):
    rolled = pltpu.roll(x, j, 0)
    prev = jnp.where(first, 0.0, pltpu.roll(before, j, 0))
    row = lax.broadcasted_iota(jnp.int32, prev.shape, 0)
    top = jnp.where(row < j, prev, rolled[0:SUBLANES])
    return jnp.concatenate([top, rolled[SUBLANES:]], axis=0)


def _in_proj_kernel(x_ref, g_ref, w_ref, oa_ref, ob_ref, oc_ref):
    h = _rmsnorm(x_ref[...], g_ref[...], RMS_EPS).astype(BF16)
    oa_ref[...] = jnp.dot(h, w_ref[:, 0:COLS_A], preferred_element_type=F32)
    ob_ref[...] = jnp.dot(h, w_ref[:, COLS_A:COLS_A + COLS_B], preferred_element_type=F32)
    oc_ref[...] = jnp.dot(h, w_ref[:, COLS_A + COLS_B:], preferred_element_type=F32)


def _in_proj(x, g, w):
    n = x.shape[0]
    wtot = w.shape[1]
    return pl.pallas_call(
        _in_proj_kernel,
        grid=(n // TM,),
        in_specs=[_row_spec(D_MODEL), _full_spec((1, D_MODEL)), _full_spec((D_MODEL, wtot))],
        out_specs=[_row_spec(COLS_A), _row_spec(COLS_B), _row_spec(COLS_C_PAD)],
        out_shape=[jax.ShapeDtypeStruct((n, COLS_A), F32), jax.ShapeDtypeStruct((n, COLS_B), F32),
                   jax.ShapeDtypeStruct((n, COLS_C_PAD), F32)],
        compiler_params=_params(("parallel",)),
        name="in_proj",
    )(x, g, w)


def _rwkv_prep_kernel(c_ref, p_ref, mu_ref, w0_ref, a0_ref, kk_ref, ka_ref, rk_ref, wup_ref, aup_ref,
                      gup_ref, bd_ref, r_o, d_o, k_o, v_o, n_o, b_o, g_o, bonus_o, *, tiles_per_seq):
    c = c_ref[...]
    if tiles_per_seq:
        prev = _shift_rows(c, p_ref[...], 1, pl.program_id(0) % tiles_per_seq == 0)
    else:
        prev = p_ref[...]
    xm = c + (prev - c) * mu_ref[...]
    r = xm[:, 0:W_A]
    k = xm[:, W_A:2 * W_A]
    v = xm[:, 2 * W_A:3 * W_A]
    lora = xm[:, 3 * W_A:3 * W_A + DECAY_LORA + AAA_LORA]
    gd = xm[:, 3 * W_A + DECAY_LORA + AAA_LORA:]
    w = w0_ref[...] + _dot(jnp.tanh(lora), wup_ref[...])
    decay = jnp.exp(-math.exp(-0.5) * _sigmoid(w))
    a = _sigmoid(a0_ref[...] + _dot(lora, aup_ref[...]))
    g = _dot(_sigmoid(gd), gup_ref[...])
    bd = bd_ref[...]
    kk = k * kk_ref[...]
    kk = kk / jnp.maximum(jnp.sqrt(_dot_hi(kk * kk, bd)), 1e-12)
    k2 = k * (1.0 + (a - 1.0) * ka_ref[...])
    for o_ref, val in ((r_o, r), (d_o, decay), (k_o, k2), (v_o, v), (n_o, -kk), (b_o, kk * a)):
        o_ref[:, 0:W_A] = val
        if o_ref.shape[1] > W_A:
            o_ref[:, W_A:] = jnp.zeros((TM, o_ref.shape[1] - W_A), F32)
    g_o[...] = g
    bonus_o[...] = _dot_hi(r * k2 * rk_ref[...], bd) * v


def _rwkv_prep(ca, prev, p, tiles_per_seq):
    n = ca.shape[0]
    vec = _full_spec((1, W_A))
    lora_spec = _full_spec((DECAY_LORA + AAA_LORA, W_A))
    p_spec = _before_spec(COLS_A) if tiles_per_seq else _row_spec(COLS_A)
    w_scan = H_PAD * HEAD_DIM if tiles_per_seq else W_A
    widths = [w_scan] * 6 + [W_A] * 2
    return pl.pallas_call(
        functools.partial(_rwkv_prep_kernel, tiles_per_seq=tiles_per_seq),
        grid=(n // TM,),
        in_specs=[_row_spec(COLS_A), p_spec, _full_spec((1, COLS_A)), vec, vec, vec, vec, vec,
                  lora_spec, lora_spec, _full_spec((GATE_LORA, W_A)), _full_spec((W_A, W_A))],
        out_specs=[_row_spec(w) for w in widths],
        out_shape=[jax.ShapeDtypeStruct((n, w), F32) for w in widths],
        compiler_params=_params(("parallel",)),
        name="rwkv_prep",
    )(ca, ca if tiles_per_seq else prev, p["mu"], p["w0"], p["a0"], p["k_k"], p["k_a"], p["r_k"],
      p["w_up"], p["a_up"], p["g_up"], p["bd_a"])


def _rope_kernel(ang_ref, cos_o, sin_o):
    ang = ang_ref[...]
    lane = lax.broadcasted_iota(jnp.int32, ang.shape, 1)
    cos_o[...] = jnp.cos(ang)
    sin_o[...] = jnp.where((lane % HEAD_DIM) < (HEAD_DIM // 2), -jnp.sin(ang), jnp.sin(ang))


def _rope_tables(t_prompt, t_sample):
    theta = 1.0 / (ROPE_BASE ** jnp.linspace(0.0, 1.0, HEAD_DIM // 2, dtype=F32))
    pos = jnp.concatenate([jnp.arange(t_prompt, dtype=F32), PAST_LEN + (jnp.arange(TM) % t_sample).astype(F32)])
    ang = jnp.tile(pos[:, None] * theta[None, :], (1, W_B // (HEAD_DIM // 2)))
    n = ang.shape[0]
    return pl.pallas_call(
        _rope_kernel,
        grid=(n // TM,),
        in_specs=[_row_spec(W_B)],
        out_specs=[_row_spec(W_B)] * 2,
        out_shape=[jax.ShapeDtypeStruct((n, W_B), F32)] * 2,
        compiler_params=_params(("parallel",)),
        name="rope_tables",
    )(ang)


def _ret_prep_kernel(q_ref, k_ref, cos_ref, sin_ref, q_o, k_o):
    cos = cos_ref[...]
    sin = sin_ref[...]
    lane = lax.broadcasted_iota(jnp.int32, (TM, W_B), 1)
    first_half = (lane % HEAD_DIM) < (HEAD_DIM // 2)

    def rope(x):
        partner = jnp.where(first_half, pltpu.roll(x, W_B - HEAD_DIM // 2, 1), pltpu.roll(x, HEAD_DIM // 2, 1))
        return x * cos + partner * sin

    q_o[...] = rope(q_ref[...])
    k_o[...] = rope(k_ref[...]) * (HEAD_DIM ** -0.5)


def _ret_prep(cb, cos, sin, table_block):
    n = cb.shape[0]
    tab = pl.BlockSpec((TM, W_B), lambda i: (table_block(i), 0))
    return pl.pallas_call(
        _ret_prep_kernel,
        grid=(n // TM,),
        in_specs=[_row_spec(W_B, 0), _row_spec(W_B, 1), tab, tab],
        out_specs=[_row_spec(W_B)] * 2,
        out_shape=[jax.ShapeDtypeStruct((n, W_B), F32)] * 2,
        compiler_params=_params(("parallel",)),
        name="ret_prep",
    )(cb, cb, cos, sin)


def _ssm_prep_kernel(*refs, tiles_per_seq):
    n_shift_refs = 1 if tiles_per_seq else CONV_W - 1
    cc_ref = refs[0]
    shift_refs = refs[1:1 + n_shift_refs]
    cw_ref, cb_ref, dtb_ref, alog_ref, ex_ref, xdt_o, b_o, c_o, dec_o, xs_o, da_o = refs[1 + n_shift_refs:]
    cc = cc_ref[...]
    x0 = cc[:, W_C:W_C + CONV_DIM]
    if tiles_per_seq:
        before = shift_refs[0][:, W_C:W_C + CONV_DIM]
        first = pl.program_id(0) % tiles_per_seq == 0
        xs_prev = [_shift_rows(x0, before, j, first) for j in range(1, CONV_W)]
    else:
        xs_prev = [r[...] for r in shift_refs]
    cw = cw_ref[...]
    conv = x0 * cw[CONV_W - 1:CONV_W, :] + cb_ref[...]
    for j in range(1, CONV_W):
        conv = conv + xs_prev[j - 1] * cw[CONV_W - 1 - j:CONV_W - j, :]
    act = conv * _sigmoid(conv)
    xs = act[:, 0:W_C]
    dt = _softplus(cc[:, W_C + CONV_DIM:] + dtb_ref[...])
    da = dt * (-jnp.exp(alog_ref[...]))
    da_o[...] = da
    dec_o[...] = jnp.exp(da)
    xdt_o[...] = xs * _dot_hi(dt, ex_ref[...])
    b_o[...] = act[:, W_C:W_C + N_GROUPS * N_STATE]
    c_o[...] = act[:, W_C + N_GROUPS * N_STATE:]
    xs_o[...] = xs


def _ssm_prep(cc, shifted, p, tiles_per_seq):
    n = cc.shape[0]
    gn = N_GROUPS * N_STATE
    if tiles_per_seq:
        shift_args, shift_specs = [cc], [_before_spec(COLS_C_PAD)]
    else:
        shift_args, shift_specs = list(shifted), [_row_spec(CONV_DIM)] * (CONV_W - 1)
    return pl.pallas_call(
        functools.partial(_ssm_prep_kernel, tiles_per_seq=tiles_per_seq),
        grid=(n // TM,),
        in_specs=[_row_spec(COLS_C_PAD)] + shift_specs + [
            _full_spec((CONV_W, CONV_DIM)), _full_spec((1, CONV_DIM)), _full_spec((1, LANES)),
            _full_spec((1, LANES)), _full_spec((LANES, W_C))],
        out_specs=[_row_spec(W_C), _row_spec(gn), _row_spec(gn), _row_spec(LANES), _row_spec(W_C),
                   _row_spec(LANES)],
        out_shape=[jax.ShapeDtypeStruct((n, W_C), F32), jax.ShapeDtypeStruct((n, gn), F32),
                   jax.ShapeDtypeStruct((n, gn), F32), jax.ShapeDtypeStruct((n, LANES), F32),
                   jax.ShapeDtypeStruct((n, W_C), F32), jax.ShapeDtypeStruct((n, LANES), F32)],
        compiler_params=_params(("parallel",)),
        name="ssm_prep",
    )(cc, *shift_args, p["conv_w"], p["conv_b"], p["dt_bias"], p["a_log"], p["expand"])


def _scan_kernel(*refs, mode, ni, tt_len, n_tt):
    if mode == "rwkv":
        q_ref, d_ref, a_ref, b_ref, n_ref, a2_ref, s0_ref, y_ref, so_ref, st = refs
    else:
        q_ref, d_ref, a_ref, b_ref, s0_ref, y_ref, so_ref, st = refs
    head = pl.program_id(0)
    tt = pl.program_id(1)

    @pl.when(tt == 0)
    def _():
        st[...] = s0_ref[...]

    def row(ref, t, i):
        return ref[t, pl.ds(i, 1), :]

    def step(t, carry):
        bv = b_ref[t]
        if mode == "rwkv":
            sa_parts = [jnp.zeros_like(bv), jnp.zeros_like(bv)]
            for i in range(ni):
                sa_parts[i % 2] = sa_parts[i % 2] + st[i] * row(n_ref, t, i)
            sa = sa_parts[0] + sa_parts[1]
        elif mode == "ssd":
            d = d_ref[t, pl.ds(head, 1), :]
        else:
            d = d_ref[pl.ds(head, 1), :]
        y_parts = [jnp.zeros_like(bv), jnp.zeros_like(bv)]
        for i in range(ni):
            if mode == "rwkv":
                s = st[i] * row(d_ref, t, i) + row(a_ref, t, i) * bv + row(a2_ref, t, i) * sa
            else:
                s = st[i] * d + row(a_ref, t, i) * bv
            st[i] = s
            y_parts[i % 2] = y_parts[i % 2] + s * row(q_ref, t, i)
        y_ref[t] = y_parts[0] + y_parts[1]
        return carry

    lax.fori_loop(0, tt_len, step, 0)

    @pl.when(tt == n_tt - 1)
    def _():
        so_ref[...] = st[...]


def _scan(mode, n_heads, ni, nj, q, d, a, b, s0, n=None, a2=None):
    t_len = b[0].shape[0]

    def rows(op, width):
        arr, block_of = op
        return arr, pl.BlockSpec((t_len, width, LANES), lambda h, t, f=block_of: (0, f(h), 0))

    ops = [rows(q, ni)]
    if mode == "rwkv":
        ops.append(rows(d, ni))
    elif mode == "ssd":
        ops.append((d, pl.BlockSpec((t_len, SUBLANES, LANES), lambda h, t: (0, 0, 0))))
    else:
        ops.append((d, pl.BlockSpec((SUBLANES, LANES), lambda h, t: (0, 0))))
    ops += [rows(a, ni), rows(b, nj)]
    if mode == "rwkv":
        ops += [rows(n, ni), rows(a2, ni)]
    s_spec = pl.BlockSpec((ni, nj, LANES), lambda h, t: (h, 0, 0))
    y_spec = pl.BlockSpec((t_len, nj, LANES), lambda h, t: (0, h, 0))
    return pl.pallas_call(
        functools.partial(_scan_kernel, mode=mode, ni=ni, tt_len=t_len, n_tt=1),
        grid=(n_heads, 1),
        in_specs=[spec for _, spec in ops] + [s_spec],
        out_specs=[y_spec, s_spec],
        out_shape=[jax.ShapeDtypeStruct((t_len, n_heads * nj, LANES), F32),
                   jax.ShapeDtypeStruct((n_heads * ni, nj, LANES), F32)],
        scratch_shapes=[pltpu.VMEM((ni, nj, LANES), F32)],
        compiler_params=_params(("parallel", "arbitrary")),
        name="scan_" + mode,
    )(*[arr for arr, _ in ops], s0)


def _rwkv_long_kernel(q_ref, d_ref, a_ref, b_ref, n1_ref, a2_ref, y_ref, so_ref, st, sa_s, *, ni, tt_len, n_tt):
    tt = pl.program_id(1)

    @pl.when(tt == 0)
    def _():
        st[...] = jnp.zeros_like(st)
        sa_s[...] = jnp.zeros_like(sa_s)

    def row(ref, t, i):
        return ref[t, pl.ds(i, 1), :]

    def fold(x):
        return x + pltpu.roll(x, LANES // 2, 1)

    def step(t, sa):
        bv = b_ref[t]
        a, a2, q, n1 = a_ref[t], a2_ref[t], q_ref[t], n1_ref[t]
        aq, a2q, an, a2n = [fold(jnp.sum(u * w, axis=0, keepdims=True))
                            for u, w in ((a, q), (a2, q), (a, n1), (a2, n1))]
        y0 = [jnp.zeros_like(bv), jnp.zeros_like(bv)]
        n0 = [jnp.zeros_like(bv), jnp.zeros_like(bv)]
        for i in range(ni):
            sd = st[i] * row(d_ref, t, i)
            st[i] = sd
            y0[i % 2] = y0[i % 2] + sd * row(q_ref, t, i)
            n0[i % 2] = n0[i % 2] + sd * row(n1_ref, t, i)
        y_ref[t] = fold(y0[0] + y0[1]) + bv * aq + sa * a2q
        sa_next = fold(n0[0] + n0[1]) + bv * an + sa * a2n
        for i in range(ni):
            st[i] = st[i] + row(a_ref, t, i) * bv + row(a2_ref, t, i) * sa
        return sa_next

    sa_s[...] = lax.fori_loop(0, tt_len, step, sa_s[...])

    @pl.when(tt == n_tt - 1)
    def _():
        so_ref[...] = st[...]


def _rwkv_long_scan(q, d, a, b, n1, a2, tt_len=64):
    t_len, ni, _ = q.shape
    nj = b.shape[1]
    n_tt = t_len // tt_len
    i_spec = pl.BlockSpec((tt_len, ni, LANES), lambda g, t: (t, 0, 0))
    j_spec = pl.BlockSpec((tt_len, nj, LANES), lambda g, t: (t, 0, 0))
    s_spec = pl.BlockSpec((ni, nj, LANES), lambda g, t: (0, 0, 0))
    return pl.pallas_call(
        functools.partial(_rwkv_long_kernel, ni=ni, tt_len=tt_len, n_tt=n_tt),
        grid=(1, n_tt),
        in_specs=[i_spec, i_spec, i_spec, j_spec, i_spec, i_spec],
        out_specs=[j_spec, s_spec],
        out_shape=[jax.ShapeDtypeStruct((t_len, nj, LANES), F32), jax.ShapeDtypeStruct((ni, nj, LANES), F32)],
        scratch_shapes=[pltpu.VMEM((ni, nj, LANES), F32), pltpu.VMEM((nj, LANES), F32)],
        compiler_params=_params(("parallel", "arbitrary")),
        name="scan_rwkv_long",
    )(q, d, a, b, n1, a2)


TR = 128
HALF = LANES // K_SPLIT


def _to_lanes_kernel(*refs, n_ops, rows, per_half):
    x_refs, o_refs, y2d = refs[:n_ops], refs[n_ops:2 * n_ops], refs[2 * n_ops]
    n_b = x_refs[0].shape[0]
    for x_ref, o_ref in zip(x_refs, o_refs):
        for b in range(n_b):
            xt = x_ref[b].T
            for h in range(H_PAD):
                for s in range(K_SPLIT):
                    lane = s * HALF + b * H_PAD + h
                    f0 = h * HEAD_DIM + (s * rows if per_half else 0)
                    y2d[pl.ds(lane, rows, stride=LANES), :] = xt[f0:f0 + rows, :]
        for r in range(rows):
            o_ref[pl.ds(r, TR, stride=rows), :] = y2d[r * LANES:(r + 1) * LANES, :].T


def _long_to_lanes(xs, b, rows, per_half):
    n_ops = len(xs)
    t = xs[0].shape[0] // b
    width = H_PAD * HEAD_DIM
    outs = pl.pallas_call(
        functools.partial(_to_lanes_kernel, n_ops=n_ops, rows=rows, per_half=per_half),
        grid=(t // TR,),
        in_specs=[pl.BlockSpec((b, TR, width), lambda i: (0, i, 0))] * n_ops,
        out_specs=[pl.BlockSpec((TR * rows, LANES), lambda i: (i, 0))] * n_ops,
        out_shape=[jax.ShapeDtypeStruct((t * rows, LANES), F32)] * n_ops,
        scratch_shapes=[pltpu.VMEM((rows * LANES, TR), F32)],
        compiler_params=_params(("parallel",)),
        name="to_lanes",
    )(*[x.reshape(b, t, width) for x in xs])
    return [o.reshape(t, rows, LANES) for o in outs]


def _from_lanes_kernel(y_ref, o_ref, z2d):
    for v in range(HEAD_DIM):
        z2d[pl.ds(v, LANES, stride=HEAD_DIM), :] = y_ref[pl.ds(v, TR, stride=HEAD_DIM), :].T
    for b in range(o_ref.shape[0]):
        r0 = b * H_PAD * HEAD_DIM
        o_ref[b] = z2d[r0:r0 + W_A, :].T


def _long_from_lanes(y, b):
    t = y.shape[0]
    out = pl.pallas_call(
        _from_lanes_kernel,
        grid=(t // TR,),
        in_specs=[pl.BlockSpec((TR * HEAD_DIM, LANES), lambda i: (i, 0))],
        out_specs=pl.BlockSpec((b, TR, W_A), lambda i: (0, i, 0)),
        out_shape=jax.ShapeDtypeStruct((b, t, W_A), F32),
        scratch_shapes=[pltpu.VMEM((LANES * HEAD_DIM, TR), F32)],
        compiler_params=_params(("parallel",)),
        name="from_lanes",
    )(y.reshape(t * HEAD_DIM, LANES))
    return out.reshape(b * t, W_A)


def _ret_chunk_kernel(q_ref, k_ref, v_ref, y_ref, s_ref, st):
    c = pl.program_id(1)

    @pl.when(c == 0)
    def _():
        st[...] = jnp.zeros_like(st)

    q = q_ref[...]
    k = k_ref[...]
    vb = v_ref[...].astype(BF16)
    kb = k.astype(BF16)
    row = lax.broadcasted_iota(jnp.int32, (LC, W_B), 0)
    head = lax.broadcasted_iota(jnp.int32, (LC, W_B), 1) // HEAD_DIM
    log_g = [math.log(1.0 - 2.0 ** (-5.0 - h)) for h in range(H_B)]
    lg = jnp.full((LC, W_B), log_g[0], F32)
    for h in range(1, H_B):
        lg = jnp.where(head == h, log_g[h], lg)
    rowf = row.astype(F32)
    diff = (lax.broadcasted_iota(jnp.int32, (LC, LC), 0) - lax.broadcasted_iota(jnp.int32, (LC, LC), 1))
    causal = diff >= 0
    difff = jnp.maximum(diff, 0).astype(F32)

    out = jnp.dot((q * jnp.exp(lg * (rowf + 1.0))).astype(BF16), st[...].astype(BF16), preferred_element_type=F32)
    for h in range(H_B):
        qm = jnp.where(head == h, q, 0.0).astype(BF16)
        s = lax.dot_general(qm, kb, (((1,), (1,)), ((), ())), preferred_element_type=F32)
        p = jnp.where(causal, s * jnp.exp(log_g[h] * difff), 0.0).astype(BF16)
        out = out + jnp.where(head == h, jnp.dot(p, vb, preferred_element_type=F32), 0.0)
    y_ref[...] = out

    kt = (k * jnp.exp(lg * (LC - 1.0 - rowf))).T.astype(BF16)
    kv = jnp.dot(kt, vb, preferred_element_type=F32)
    r2 = lax.broadcasted_iota(jnp.int32, (W_B, W_B), 0) // HEAD_DIM
    c2 = lax.broadcasted_iota(jnp.int32, (W_B, W_B), 1) // HEAD_DIM
    cdec = jnp.full((W_B, W_B), math.exp(log_g[0] * LC), F32)
    for h in range(1, H_B):
        cdec = jnp.where(r2 == h, math.exp(log_g[h] * LC), cdec)
    st[...] = st[...] * cdec + jnp.where(r2 == c2, kv, 0.0)
    s_ref[0] = st[...]


def _ret_chunk(q, k, cb, n_seq, t_len):
    n_c = t_len // LC
    rows = lambda col: pl.BlockSpec((LC, W_B), lambda b, c, col=col: (b * n_c + c, col))
    return pl.pallas_call(
        _ret_chunk_kernel,
        grid=(n_seq, n_c),
        in_specs=[rows(0), rows(0), rows(2)],
        out_specs=[rows(0), pl.BlockSpec((1, W_B, W_B), lambda b, c: (b, 0, 0))],
        out_shape=[jax.ShapeDtypeStruct((n_seq * t_len, W_B), F32), jax.ShapeDtypeStruct((n_seq, W_B, W_B), F32)],
        scratch_shapes=[pltpu.VMEM((W_B, W_B), F32)],
        compiler_params=_params(("parallel", "arbitrary")),
        name="ret_chunk",
    )(q, k, cb)


def _ssd_chunk_kernel(x_ref, b_ref, c_ref, da_ref, ex_ref, y_ref, h_ref, st):
    ci = pl.program_id(1)

    @pl.when(ci == 0)
    def _():
        st[...] = jnp.zeros_like(st)

    gn = N_STATE
    half = W_C // N_GROUPS
    ex = ex_ref[...]
    x = x_ref[...]
    xb = x.astype(BF16)
    ri = lax.broadcasted_iota(jnp.int32, (LC, LC), 0)
    cj = lax.broadcasted_iota(jnp.int32, (LC, LC), 1)
    causal = ri >= cj
    cum = _dot_hi(causal.astype(F32), da_ref[...])
    cum_e = _dot_hi(cum, ex)
    last_e = cum_e[LC - 1:LC, :]
    cum_t = cum.T
    head = lax.broadcasted_iota(jnp.int32, (LC, W_C), 1) // HEAD_DIM
    bg = [b_ref[:, g * gn:(g + 1) * gn].astype(BF16) for g in range(N_GROUPS)]
    cg = [c_ref[:, g * gn:(g + 1) * gn].astype(BF16) for g in range(N_GROUPS)]
    cb = [lax.dot_general(cg[g], bg[g], (((1,), (1,)), ((), ())), preferred_element_type=F32)
          for g in range(N_GROUPS)]

    hb = st[...].astype(BF16)
    ys = [lax.dot_general(cg[g], hb, (((1,), (1,)), ((), ())), preferred_element_type=F32)
          for g in range(N_GROUPS)]
    y = jnp.where(head < H_C // N_GROUPS, ys[0], ys[1]) * jnp.exp(cum_e)
    for h in range(H_C):
        seg = cum[:, h:h + 1] - cum_t[h:h + 1, :]
        p = (cb[h // (H_C // N_GROUPS)] * jnp.exp(jnp.where(causal, seg, -jnp.inf))).astype(BF16)
        y = y + jnp.where(head == h, jnp.dot(p, xb, preferred_element_type=F32), 0.0)
    y_ref[...] = y

    xt = (x * jnp.exp(last_e - cum_e)).T.astype(BF16)
    upd = [jnp.dot(xt, bg[g], preferred_element_type=F32) for g in range(N_GROUPS)]
    rowi = lax.broadcasted_iota(jnp.int32, (W_C, gn), 0)
    sel = lax.broadcasted_iota(jnp.int32, (W_C, LANES), 1) == lax.broadcasted_iota(jnp.int32, (W_C, LANES), 0) // HEAD_DIM
    tot = jnp.sum(jnp.where(sel, cum[LC - 1:LC, :], 0.0), axis=1, keepdims=True)
    st[...] = st[...] * jnp.exp(tot) + jnp.where(rowi < half, upd[0], upd[1])
    h_ref[0] = st[...]


def _ssd_chunk(xdt, bm, cm, da, ex, n_seq, t_len):
    n_c = t_len // LC
    gn = N_GROUPS * N_STATE
    rows = lambda w: pl.BlockSpec((LC, w), lambda b, c: (b * n_c + c, 0))
    return pl.pallas_call(
        _ssd_chunk_kernel,
        grid=(n_seq, n_c),
        in_specs=[rows(W_C), rows(gn), rows(gn), rows(LANES), pl.BlockSpec((LANES, W_C), lambda b, c: (0, 0))],
        out_specs=[rows(W_C), pl.BlockSpec((1, W_C, N_STATE), lambda b, c: (b, 0, 0))],
        out_shape=[jax.ShapeDtypeStruct((n_seq * t_len, W_C), F32),
                   jax.ShapeDtypeStruct((n_seq, W_C, N_STATE), F32)],
        scratch_shapes=[pltpu.VMEM((W_C, N_STATE), F32)],
        compiler_params=_params(("parallel", "arbitrary")),
        name="ssd_chunk",
    )(xdt, bm, cm, da, ex)


def _post_kernel(x_ref, ya_ref, bonus_ref, ga_ref, ob_ref, gb_ref, yc_ref, xs_ref, z_ref, lnw_ref, lnb_ref,
                 rn_ref, dsk_ref, sn_ref, bda_ref, bdb_ref, wo_ref, out_ref):
    inv_hd = 1.0 / HEAD_DIM
    y = ya_ref[...]
    bda = bda_ref[...]
    mean = _dot_hi(y, bda) * inv_hd
    yd = y - mean
    var = _dot_hi(yd * yd, bda) * inv_hd
    ya = (yd * lax.rsqrt(var + GN_EPS) * lnw_ref[...] + lnb_ref[...] + bonus_ref[...]) * ga_ref[...]
    o = ob_ref[...]
    ms = _dot_hi(o * o, bdb_ref[...]) * inv_hd
    gb = gb_ref[...]
    yb = (gb * _sigmoid(gb)) * (o * lax.rsqrt(ms + RMS_EPS) * rn_ref[...])
    z = z_ref[...]
    yc = (yc_ref[...] + dsk_ref[...] * xs_ref[...]) * (z * _sigmoid(z))
    yc = _rmsnorm(yc, sn_ref[...], GATED_NORM_EPS)
    out_ref[...] = (x_ref[...] + _dot(ya, wo_ref[0:W_A, :]) + _dot(yb, wo_ref[W_A:W_A + W_B, :])
                    + _dot(yc, wo_ref[W_A + W_B:, :]))


def _post(x, ya, bonus, ga, ob, cb, yc, xs, cc, p):
    n = x.shape[0]
    va = _full_spec((1, W_A))
    return pl.pallas_call(
        _post_kernel,
        grid=(n // TM,),
        in_specs=[_row_spec(D_MODEL), _row_spec(W_A), _row_spec(W_A), _row_spec(W_A), _row_spec(W_B),
                  _row_spec(W_B, 3), _row_spec(W_C), _row_spec(W_C), _row_spec(W_C, 0), va, va,
                  _full_spec((1, W_B)), va, va, _full_spec((W_A, W_A)), _full_spec((W_B, W_B)),
                  _full_spec((D_MODEL, D_MODEL))],
        out_specs=_row_spec(D_MODEL),
        out_shape=jax.ShapeDtypeStruct((n, D_MODEL), F32),
        compiler_params=_params(("parallel",)),
        name="post_outproj",
    )(x, ya, bonus, ga, ob, cb, yc, xs, cc, p["ln_w"], p["ln_b"], p["ret_norm"], p["d_skip"], p["ssm_norm"],
      p["bd_a"], p["bd_b"], p["w_out"])


def _ffn_kernel(*refs, gated, final_norm, n_e):
    refs = list(refs)
    x_ref, nf_ref = refs[:2]
    pos = 2
    if gated:
        router_ref = refs[pos]
        pos += 1
    wg_ref, wu_ref, wd_ref = refs[pos:pos + 3]
    pos += 3
    if final_norm:
        nfin_ref = refs[pos]
        pos += 1
    out_ref = refs[pos]
    hb_s, acc_s = refs[pos + 1:pos + 3]
    if gated:
        gate_s, asg_s, pos_s, asg_t, pos_t = refs[pos + 3:pos + 8]
    e = pl.program_id(1)

    @pl.when(e == 0)
    def _():
        h = _rmsnorm(x_ref[...], nf_ref[...], RMS_EPS)
        hb_s[...] = h.astype(BF16)
        acc_s[...] = jnp.zeros_like(acc_s)
        if gated:
            lane = lax.broadcasted_iota(jnp.int32, (TM, LANES), 1)
            logits = jnp.where(lane < N_EXPERTS, _dot_hi(h, router_ref[...]), -jnp.inf)
            p = jnp.exp(logits - jnp.max(logits, axis=-1, keepdims=True))
            p = p / jnp.sum(p, axis=-1, keepdims=True)
            p1 = jnp.max(p, axis=-1, keepdims=True)
            i1 = jnp.min(jnp.where(p == p1, lane, LANES), axis=-1, keepdims=True)
            rest = jnp.where(lane == i1, -1.0, p)
            p2 = jnp.max(rest, axis=-1, keepdims=True)
            i2 = jnp.min(jnp.where(rest == p2, lane, LANES), axis=-1, keepdims=True)
            gate_s[...] = jnp.where(lane == i1, p1, jnp.where(lane == i2, p2, 0.0)) / (p1 + p2)
            assigned = jnp.where((lane == i1) | (lane == i2), 1.0, 0.0)
            earlier = (lax.broadcasted_iota(jnp.int32, (TM, TM), 0) > lax.broadcasted_iota(jnp.int32, (TM, TM), 1))
            rank = jnp.dot(earlier.astype(BF16), assigned.astype(BF16), preferred_element_type=F32)
            asg_s[...] = assigned
            pos_s[...] = rank
            asg_t[...] = assigned.T
            pos_t[...] = rank.T

    if not gated:
        hb = hb_s[...]
        g = jnp.dot(hb, wg_ref[0], preferred_element_type=F32)
        u = jnp.dot(hb, wu_ref[0], preferred_element_type=F32)
        acc_s[...] += _dot(g * _sigmoid(g) * u, wd_ref[0])
    else:
        lane = lax.broadcasted_iota(jnp.int32, (TM, LANES), 1)
        mine = lane == e
        col = lambda ref: jnp.sum(jnp.where(mine, ref[...], 0.0), axis=-1, keepdims=True)
        gate_c, asg_c, pos_c = col(gate_s), col(asg_s), col(pos_s)
        asg_r, pos_r = asg_t[pl.ds(e, 1), :], pos_t[pl.ds(e, 1), :]
        n_chunks = (jnp.sum(asg_r).astype(jnp.int32) + MOE_CHUNK - 1) // MOE_CHUNK

        def chunk(c, carry):
            base = (c * MOE_CHUNK).astype(F32)
            slot_r = lax.broadcasted_iota(jnp.int32, (MOE_CHUNK, TM), 0).astype(F32) + base
            pick = jnp.where((pos_r == slot_r) & (asg_r > 0.0), 1.0, 0.0).astype(BF16)
            rows = jnp.dot(pick, hb_s[...], preferred_element_type=F32).astype(BF16)
            g = jnp.dot(rows, wg_ref[0], preferred_element_type=F32)
            u = jnp.dot(rows, wu_ref[0], preferred_element_type=F32)
            o = _dot(g * _sigmoid(g) * u, wd_ref[0])
            o_hi = o.astype(BF16)
            o_lo = (o - o_hi.astype(F32)).astype(BF16)
            slot_c = lax.broadcasted_iota(jnp.int32, (TM, MOE_CHUNK), 1).astype(F32) + base
            place = jnp.where((pos_c == slot_c) & (asg_c > 0.0), 1.0, 0.0).astype(BF16)
            back = (jnp.dot(place, o_hi, preferred_element_type=F32)
                    + jnp.dot(place, o_lo, preferred_element_type=F32))
            acc_s[...] += gate_c * back
            return carry

        lax.fori_loop(0, n_chunks, chunk, 0)

    @pl.when(e == n_e - 1)
    def _():
        y = x_ref[...] + acc_s[...]
        if final_norm:
            y = _rmsnorm(y, nfin_ref[...], RMS_EPS)
        out_ref[...] = y


def _ffn(x, nf, wg, wu, wd, router=None, nfin=None):
    n = x.shape[0]
    n_e, _, dff = wg.shape
    gated = router is not None
    final_norm = nfin is not None
    vec = pl.BlockSpec((1, D_MODEL), lambda i, e: (0, 0))
    args = [x, nf]
    specs = [pl.BlockSpec((TM, D_MODEL), lambda i, e: (i, 0)), vec]
    if gated:
        args.append(router)
        specs.append(pl.BlockSpec((D_MODEL, LANES), lambda i, e: (0, 0)))
    args += [wg, wu, wd]
    specs += [pl.BlockSpec((1, D_MODEL, dff), lambda i, e: (e, 0, 0)),
              pl.BlockSpec((1, D_MODEL, dff), lambda i, e: (e, 0, 0)),
              pl.BlockSpec((1, dff, D_MODEL), lambda i, e: (e, 0, 0))]
    if final_norm:
        args.append(nfin)
        specs.append(vec)
    scratch = [pltpu.VMEM((TM, D_MODEL), BF16), pltpu.VMEM((TM, D_MODEL), F32)]
    if gated:
        scratch += [pltpu.VMEM((TM, LANES), F32)] * 3 + [pltpu.VMEM((LANES, TM), F32)] * 2
    return pl.pallas_call(
        functools.partial(_ffn_kernel, gated=gated, final_norm=final_norm, n_e=n_e),
        grid=(n // TM, n_e),
        in_specs=specs,
        out_specs=pl.BlockSpec((TM, D_MODEL), lambda i, e: (i, 0)),
        out_shape=jax.ShapeDtypeStruct((n, D_MODEL), F32),
        scratch_shapes=scratch,
        compiler_params=_params(("parallel", "arbitrary")),
        name="moe" if gated else "ffn",
    )(*args)


def _short_to_lanes_kernel(*refs, n_ops, t_len):
    for x_ref, o_ref in zip(refs[:n_ops], refs[n_ops:]):
        for t in range(t_len):
            o_ref[t] = x_ref[pl.ds(t, LANES, stride=t_len), :].T


def _short_to_lanes(xs, t_len):
    n_chunks = [width // LANES for _, _, width in xs]
    grid = max(n_chunks)
    clamp = lambda c, n: jnp.minimum(c, n - 1)
    outs = pl.pallas_call(
        functools.partial(_short_to_lanes_kernel, n_ops=len(xs), t_len=t_len),
        grid=(grid,),
        in_specs=[pl.BlockSpec((LANES * t_len, LANES), lambda c, c0=col // LANES, n=n: (0, c0 + clamp(c, n)))
                  for (_, col, _), n in zip(xs, n_chunks)],
        out_specs=[pl.BlockSpec((t_len, LANES, LANES), lambda c, n=n: (0, clamp(c, n), 0)) for n in n_chunks],
        out_shape=[jax.ShapeDtypeStruct((t_len, width, LANES), F32) for _, _, width in xs],
        compiler_params=_params(("arbitrary",)),
        name="short_to_lanes",
    )(*[x for x, _, _ in xs])
    return outs


def _short_from_lanes_kernel(y_ref, o_ref, *, t_len):
    for t in range(t_len):
        o_ref[pl.ds(t, LANES, stride=t_len), :] = y_ref[t].T


def _short_from_lanes(y):
    t_len, w, _ = y.shape
    return pl.pallas_call(
        functools.partial(_short_from_lanes_kernel, t_len=t_len),
        grid=(w // LANES,),
        in_specs=[pl.BlockSpec((t_len, LANES, LANES), lambda c: (0, c, 0))],
        out_specs=pl.BlockSpec((LANES * t_len, LANES), lambda c: (0, c)),
        out_shape=jax.ShapeDtypeStruct((LANES * t_len, w), F32),
        compiler_params=_params(("parallel",)),
        name="short_from_lanes",
    )(y)


def _state_to_lanes_kernel(x_ref, o_ref, *, ni, nj, j_first):
    xt = x_ref[...].T
    if j_first:
        for j in range(nj):
            o_ref[pl.ds(j, ni, stride=nj), :] = xt[j * ni:(j + 1) * ni, :]
    else:
        o_ref[...] = xt


def _state_to_lanes(s, ni, nj, j_first):
    h = s.shape[1]
    out = pl.pallas_call(
        functools.partial(_state_to_lanes_kernel, ni=ni, nj=nj, j_first=j_first),
        grid=(h,),
        in_specs=[pl.BlockSpec((LANES, ni * nj), lambda i: (0, i))],
        out_specs=pl.BlockSpec((ni * nj, LANES), lambda i: (i, 0)),
        out_shape=jax.ShapeDtypeStruct((h * ni * nj, LANES), F32),
        compiler_params=_params(("parallel",)),
        name="state_to_lanes",
    )(s.reshape(LANES, h * ni * nj))
    return out.reshape(h * ni, nj, LANES)


def _state_from_lanes_kernel(s_ref, o_ref, tmp, *, ni, nj, j_first):
    if j_first:
        for j in range(nj):
            tmp[j * ni:(j + 1) * ni, :] = s_ref[pl.ds(j, ni, stride=nj), :]
        o_ref[...] = tmp[...].T
    else:
        o_ref[...] = s_ref[...].T


def _state_from_lanes(s, n_heads, ni, nj, j_first):
    out = pl.pallas_call(
        functools.partial(_state_from_lanes_kernel, ni=ni, nj=nj, j_first=j_first),
        grid=(n_heads,),
        in_specs=[pl.BlockSpec((ni * nj, LANES), lambda i: (i, 0))],
        out_specs=pl.BlockSpec((LANES, ni * nj), lambda i: (0, i)),
        out_shape=jax.ShapeDtypeStruct((LANES, n_heads * ni * nj), F32),
        scratch_shapes=[pltpu.VMEM((ni * nj, LANES), F32)],
        compiler_params=_params(("parallel",)),
        name="state_from_lanes",
    )(s.reshape(n_heads * ni * nj, LANES))
    return out.reshape((LANES, n_heads, nj, ni) if j_first else (LANES, n_heads, ni, nj))


def _rwkv_long(feats, b):
    r, dec, k2, v, nkk, beta = feats
    t = r.shape[0] // b
    kl = HEAD_DIM // K_SPLIT
    assert b * H_PAD == HALF and t % TR == 0
    nkk_next = jnp.concatenate([nkk[1:], jnp.zeros((1, nkk.shape[1]), F32)], axis=0)
    q, d, a, n1, a2 = _long_to_lanes([r, dec, k2, nkk_next, beta], b, kl, True)
    (vv,) = _long_to_lanes([v], b, HEAD_DIM, False)
    y, s = _rwkv_long_scan(q, d, a, vv, n1, a2)
    s = s.reshape(kl, HEAD_DIM, K_SPLIT, b, H_PAD)[..., :H_A]
    return _long_from_lanes(y, b), jnp.transpose(s, (3, 4, 1, 2, 0)).reshape(b, H_A, HEAD_DIM, HEAD_DIM)


def _rwkv_short(feats, t_len, s0):
    r, dec, k2, v, nkk, beta = _short_to_lanes([(f, 0, W_A) for f in feats], t_len)
    hd = HEAD_DIM
    own = lambda h: h
    y, s = _scan("rwkv", H_A, hd, hd, (r, own), (dec, own), (k2, own), (v, own),
                 _state_to_lanes(s0, hd, hd, True), n=(nkk, own), a2=(beta, own))
    return _short_from_lanes(y), _state_from_lanes(s, H_A, hd, hd, True)


def _ret_short(qh, kh, cb, t_len, s0):
    q, k, v = _short_to_lanes([(qh, 0, W_B), (kh, 0, W_B), (cb, 2 * W_B, W_B)], t_len)
    hd = HEAD_DIM
    gam = np.ones((SUBLANES, LANES), np.float32)
    gam[:H_B] = (1.0 - 2.0 ** (-5.0 - np.arange(H_B, dtype=np.float64)))[:, None]
    own = lambda h: h
    y, s = _scan("ret", H_B, hd, hd, (q, own), jnp.asarray(gam), (k, own), (v, own),
                 _state_to_lanes(s0, hd, hd, False))
    return _short_from_lanes(y), _state_from_lanes(s, H_B, hd, hd, False)


def _ssd_short(xdt, bm, cm, dssm, t_len, s0):
    gn = N_GROUPS * N_STATE
    x, bl, cl, dl = _short_to_lanes([(xdt, 0, W_C), (bm, 0, gn), (cm, 0, gn), (dssm, 0, LANES)], t_len)
    group = lambda h: h // (H_C // N_GROUPS)
    y, s = _scan("ssd", H_C, N_STATE, HEAD_DIM, (cl, group), dl, (bl, group), (x, lambda h: h),
                 _state_to_lanes(s0, N_STATE, HEAD_DIM, True))
    return _short_from_lanes(y), _state_from_lanes(s, H_C, N_STATE, HEAD_DIM, True)


def _block_diag_ones(width):
    idx = np.arange(width) // HEAD_DIM
    return jnp.asarray((idx[:, None] == idx[None, :]).astype(np.float32))


def _shifted(x, prev_rows, shift):
    b, t, c = x.shape
    p = prev_rows.shape[1]
    full = jnp.concatenate([prev_rows, x], axis=1)
    return full[:, p - shift:p - shift + t].reshape(b * t, c)


def _layer(x, b, p, st, rope, table_block):
    n = x.shape[0]
    t = n // b
    fresh = st is None
    tiles_per_seq = t // TM if fresh else 0
    ca, cb, cc = _in_proj(x, p["norm_mix"], p["w_in"])
    ca3 = ca.reshape(b, t, COLS_A)
    xbc_tail = cc.reshape(b, t, COLS_C_PAD)[:, -(CONV_W - 1):, W_C:W_C + CONV_DIM]
    if fresh:
        prev, shifted = None, None
        conv_new = xbc_tail
    else:
        prev = _shifted(ca3, st["shift"][:, None, :], 1)
        xbc = cc[:, W_C:W_C + CONV_DIM].reshape(b, t, CONV_DIM)
        shifted = [_shifted(xbc, st["conv"], j) for j in range(1, CONV_W)]
        conv_new = jnp.concatenate([st["conv"], xbc_tail], axis=1)[:, -(CONV_W - 1):]

    r, dec, k2, v, nkk, beta, ga, bonus = _rwkv_prep(ca, prev, p, tiles_per_seq)
    qh, kh = _ret_prep(cb, rope[0], rope[1], table_block)
    xdt, bm, cm, dssm, xs, da = _ssm_prep(cc, shifted, p, tiles_per_seq)

    feats = (r, dec, k2, v, nkk, beta)
    if fresh:
        ya, s_rwkv = _rwkv_long(feats, b)
        yb, s_full = _ret_chunk(qh, kh, cb, b, t)
        s_ret = jnp.stack([s_full[:, h * HEAD_DIM:(h + 1) * HEAD_DIM, h * HEAD_DIM:(h + 1) * HEAD_DIM]
                           for h in range(H_B)], axis=1)
        yc, s_ssm = _ssd_chunk(xdt, bm, cm, da, p["expand"], b, t)
        s_ssm = s_ssm.reshape(b, H_C, HEAD_DIM, N_STATE)
    else:
        assert b == LANES
        ya, s_rwkv = _rwkv_short(feats, t, st["rwkv"])
        yb, s_ret = _ret_short(qh, kh, cb, t, st["ret"])
        yc, s_ssm = _ssd_short(xdt, bm, cm, dssm, t, st["ssm"])

    x2 = _post(x, ya, bonus, ga, yb, cb, yc, xs, cc, p)
    x = _ffn(x2, p["norm_ffn"], p["wg"], p["wu"], p["wd"], router=p["router"], nfin=p["norm_final"])
    return x, (s_rwkv, ca3[:, -1], s_ret, s_ssm, conv_new)


def kernel(x_prompt, x_sample, state_rwkv, state_shift, state_ret, state_ssm, state_conv, norm_mix, w_in, rwkv_mu, rwkv_w0, rwkv_w_up, rwkv_a0, rwkv_a_up, rwkv_g_up, rwkv_k_k, rwkv_k_a, rwkv_r_k, rwkv_ln_w, rwkv_ln_b, ret_norm, ssm_conv_w, ssm_conv_b, ssm_dt_bias, ssm_a_log, ssm_d, ssm_norm, w_out, norm_ffn, ffn_w_gate, ffn_w_up, ffn_w_down, moe_router, moe_w_gate, moe_w_up, moe_w_down, norm_final):
    bp, tp, _ = x_prompt.shape
    bs, ts, _ = x_sample.shape
    depth = w_in.shape[0]
    assert tp % TM == 0 and tp % LC == 0 and (bs * ts) % TM == 0 and TM % ts == 0

    rope = _rope_tables(tp, ts)
    bd_a, bd_b = _block_diag_ones(W_A), _block_diag_ones(W_B)
    expand = np.zeros((LANES, W_C), np.float32)
    expand[np.arange(W_C) // HEAD_DIM, np.arange(W_C)] = 1.0
    expand = jnp.asarray(expand)
    row = lambda v: v.reshape(1, -1)
    pad_l = lambda v: jnp.pad(v, (0, LANES - v.shape[0])).reshape(1, LANES)

    xp = x_prompt.reshape(bp * tp, D_MODEL)
    xs = x_sample.reshape(bs * ts, D_MODEL)
    new_p, new_s = [], []
    for i in range(depth):
        j = i // 2
        p = dict(
            norm_mix=row(norm_mix[i]),
            w_in=jnp.pad(w_in[i], ((0, 0), (0, COLS_C_PAD - COLS_C))).astype(BF16),
            mu=row(rwkv_mu[i]), w0=row(rwkv_w0[i]), a0=row(rwkv_a0[i]), k_k=row(rwkv_k_k[i]),
            k_a=row(rwkv_k_a[i]), r_k=row(rwkv_r_k[i]),
            w_up=jnp.pad(rwkv_w_up[i], ((0, AAA_LORA), (0, 0))).astype(BF16),
            a_up=jnp.pad(rwkv_a_up[i], ((DECAY_LORA, 0), (0, 0))).astype(BF16),
            g_up=rwkv_g_up[i].astype(BF16), bd_a=bd_a, bd_b=bd_b,
            conv_w=ssm_conv_w[i], conv_b=row(ssm_conv_b[i]), dt_bias=pad_l(ssm_dt_bias[i]),
            a_log=pad_l(ssm_a_log[i]), expand=expand,
            ln_w=row(rwkv_ln_w[i]), ln_b=row(rwkv_ln_b[i]), ret_norm=row(ret_norm[i]),
            d_skip=row(jnp.repeat(ssm_d[i], HEAD_DIM)), ssm_norm=row(ssm_norm[i]), w_out=w_out[i].astype(BF16),
            norm_ffn=row(norm_ffn[i]), norm_final=row(norm_final) if i == depth - 1 else None)
        if i % 2 == 0:
            dff = ffn_w_gate.shape[-1] // 2
            p.update(router=None,
                     wg=ffn_w_gate[j].reshape(D_MODEL, 2, dff).transpose(1, 0, 2).astype(BF16),
                     wu=ffn_w_up[j].reshape(D_MODEL, 2, dff).transpose(1, 0, 2).astype(BF16),
                     wd=ffn_w_down[j].reshape(2, dff, D_MODEL).astype(BF16))
        else:
            p.update(router=jnp.pad(moe_router[j], ((0, 0), (0, LANES - N_EXPERTS))),
                     wg=moe_w_gate[j].astype(BF16), wu=moe_w_up[j].astype(BF16), wd=moe_w_down[j].astype(BF16))

        xp, st_p = _layer(xp, bp, p, None, rope, lambda t: t % (tp // TM))
        st = dict(rwkv=state_rwkv[i], shift=state_shift[i], ret=state_ret[i], ssm=state_ssm[i],
                  conv=state_conv[i])
        xs, st_s = _layer(xs, bs, p, st, rope, lambda t: tp // TM)
        new_p.append(st_p)
        new_s.append(st_s)

    stack = lambda sts: tuple(jnp.stack(s) for s in zip(*sts))
    return (xp.reshape(bp, tp, D_MODEL), xs.reshape(bs, ts, D_MODEL)) + stack(new_p) + stack(new_s)
```

```python
import functools
import math

import numpy as np
import jax
import jax.numpy as jnp
from jax import lax
from jax.experimental import pallas as pl
from jax.experimental.pallas import tpu as pltpu

F32 = jnp.float32
BF16 = jnp.bfloat16
HIGHEST = lax.Precision.HIGHEST

LANES = 128
SUBLANES = 8
VMEM_LIMIT = 56 * 1024 * 1024

D_MODEL = 1024
HEAD_DIM = 64
H_A, H_B, H_C = 6, 4, 6
W_A, W_B, W_C = H_A * HEAD_DIM, H_B * HEAD_DIM, H_C * HEAD_DIM
DECAY_LORA, AAA_LORA, GATE_LORA = 64, 64, 128
COLS_A = 3 * W_A + DECAY_LORA + AAA_LORA + GATE_LORA
COLS_B = 4 * W_B
N_STATE, N_GROUPS, CONV_W = 128, 2, 4
CONV_DIM = W_C + 2 * N_GROUPS * N_STATE
COLS_C = W_C + CONV_DIM + H_C
COLS_C_PAD = 1408
ROPE_BASE = 10000.0
RMS_EPS = 1e-6
GN_EPS = 64e-5
GATED_NORM_EPS = 1e-5
N_EXPERTS = 8
PAST_LEN = 16384

TM = 512
MOE_CHUNK = 160
LC = 256
H_PAD = 8
K_SPLIT = 2


def _dot(a, b):
    return jnp.dot(a.astype(BF16), b.astype(BF16), preferred_element_type=F32)


def _dot_hi(a, b):
    return jnp.dot(a, b, precision=HIGHEST, preferred_element_type=F32)


def _sigmoid(x):
    return 1.0 / (1.0 + jnp.exp(-x))


def _softplus(x):
    return jnp.maximum(x, 0.0) + jnp.log1p(jnp.exp(-jnp.abs(x)))


def _rmsnorm(x, g, eps):
    return x * lax.rsqrt(jnp.mean(x * x, axis=-1, keepdims=True) + eps) * g


def _params(sem):
    return pltpu.CompilerParams(dimension_semantics=sem, vmem_limit_bytes=VMEM_LIMIT)


def _row_spec(width, col=0):
    return pl.BlockSpec((TM, width), lambda i, c=col: (i, c))


def _before_spec(width):
    return pl.BlockSpec((SUBLANES, width), lambda i: (jnp.maximum(i * (TM // SUBLANES) - 1, 0), 0))


def _full_spec(shape):
    nd = len(shape)
    return pl.BlockSpec(shape, lambda i, n=nd: (0,) * n)


def _shift_rows(x, before, j, first):
    rolled = pltpu.roll(x, j, 0)
    prev = jnp.where(first, 0.0, pltpu.roll(before, j, 0))
    row = lax.broadcasted_iota(jnp.int32, prev.shape, 0)
    top = jnp.where(row < j, prev, rolled[0:SUBLANES])
    return jnp.concatenate([top, rolled[SUBLANES:]], axis=0)


def _in_proj_kernel(x_ref, g_ref, w_ref, oa_ref, ob_ref, oc_ref):
    h = _rmsnorm(x_ref[...], g_ref[...], RMS_EPS).astype(BF16)
    oa_ref[...] = jnp.dot(h, w_ref[:, 0:COLS_A], preferred_element_type=F32)
    ob_ref[...] = jnp.dot(h, w_ref[:, COLS_A:COLS_A + COLS_B], preferred_element_type=F32)
    oc_ref[...] = jnp.dot(h, w_ref[:, COLS_A + COLS_B:], preferred_element_type=F32)


def _in_proj(x, g, w):
    n = x.shape[0]
    wtot = w.shape[1]
    return pl.pallas_call(
        _in_proj_kernel,
        grid=(n // TM,),
        in_specs=[_row_spec(D_MODEL), _full_spec((1, D_MODEL)), _full_spec((D_MODEL, wtot))],
        out_specs=[_row_spec(COLS_A), _row_spec(COLS_B), _row_spec(COLS_C_PAD)],
        out_shape=[jax.ShapeDtypeStruct((n, COLS_A), F32), jax.ShapeDtypeStruct((n, COLS_B), F32),
                   jax.ShapeDtypeStruct((n, COLS_C_PAD), F32)],
        compiler_params=_params(("parallel",)),
        name="in_proj",
    )(x, g, w)


def _rwkv_prep_kernel(c_ref, p_ref, mu_ref, w0_ref, a0_ref, kk_ref, ka_ref, rk_ref, wup_ref, aup_ref,
                      gup_ref, bd_ref, r_o, d_o, k_o, v_o, n_o, b_o, g_o, bonus_o, *, tiles_per_seq):
    c = c_ref[...]
    if tiles_per_seq:
        prev = _shift_rows(c, p_ref[...], 1, pl.program_id(0) % tiles_per_seq == 0)
    else:
        prev = p_ref[...]
    xm = c + (prev - c) * mu_ref[...]
    r = xm[:, 0:W_A]
    k = xm[:, W_A:2 * W_A]
    v = xm[:, 2 * W_A:3 * W_A]
    lora = xm[:, 3 * W_A:3 * W_A + DECAY_LORA + AAA_LORA]
    gd = xm[:, 3 * W_A + DECAY_LORA + AAA_LORA:]
    w = w0_ref[...] + _dot(jnp.tanh(lora), wup_ref[...])
    decay = jnp.exp(-math.exp(-0.5) * _sigmoid(w))
    a = _sigmoid(a0_ref[...] + _dot(lora, aup_ref[...]))
    g = _dot(_sigmoid(gd), gup_ref[...])
    bd = bd_ref[...]
    kk = k * kk_ref[...]
    kk = kk / jnp.maximum(jnp.sqrt(_dot_hi(kk * kk, bd)), 1e-12)
    k2 = k * (1.0 + (a - 1.0) * ka_ref[...])
    for o_ref, val in ((r_o, r), (d_o, decay), (k_o, k2), (v_o, v), (n_o, -kk), (b_o, kk * a)):
        o_ref[:, 0:W_A] = val
        if o_ref.shape[1] > W_A:
            o_ref[:, W_A:] = jnp.zeros((TM, o_ref.shape[1] - W_A), F32)
    g_o[...] = g
    bonus_o[...] = _dot_hi(r * k2 * rk_ref[...], bd) * v


def _rwkv_prep(ca, prev, p, tiles_per_seq):
    n = ca.shape[0]
    vec = _full_spec((1, W_A))
    lora_spec = _full_spec((DECAY_LORA + AAA_LORA, W_A))
    p_spec = _before_spec(COLS_A) if tiles_per_seq else _row_spec(COLS_A)
    w_scan = H_PAD * HEAD_DIM if tiles_per_seq else W_A
    widths = [w_scan] * 6 + [W_A] * 2
    return pl.pallas_call(
        functools.partial(_rwkv_prep_kernel, tiles_per_seq=tiles_per_seq),
        grid=(n // TM,),
        in_specs=[_row_spec(COLS_A), p_spec, _full_spec((1, COLS_A)), vec, vec, vec, vec, vec,
                  lora_spec, lora_spec, _full_spec((GATE_LORA, W_A)), _full_spec((W_A, W_A))],
        out_specs=[_row_spec(w) for w in widths],
        out_shape=[jax.ShapeDtypeStruct((n, w), F32) for w in widths],
        compiler_params=_params(("parallel",)),
        name="rwkv_prep",
    )(ca, ca if tiles_per_seq else prev, p["mu"], p["w0"], p["a0"], p["k_k"], p["k_a"], p["r_k"],
      p["w_up"], p["a_up"], p["g_up"], p["bd_a"])


def _rope_kernel(ang_ref, cos_o, sin_o):
    ang = ang_ref[...]
    lane = lax.broadcasted_iota(jnp.int32, ang.shape, 1)
    cos_o[...] = jnp.cos(ang)
    sin_o[...] = jnp.where((lane % HEAD_DIM) < (HEAD_DIM // 2), -jnp.sin(ang), jnp.sin(ang))


def _rope_tables(t_prompt, t_sample):
    theta = 1.0 / (ROPE_BASE ** jnp.linspace(0.0, 1.0, HEAD_DIM // 2, dtype=F32))
    pos = jnp.concatenate([jnp.arange(t_prompt, dtype=F32), PAST_LEN + (jnp.arange(TM) % t_sample).astype(F32)])
    ang = jnp.tile(pos[:, None] * theta[None, :], (1, W_B // (HEAD_DIM // 2)))
    n = ang.shape[0]
    return pl.pallas_call(
        _rope_kernel,
        grid=(n // TM,),
        in_specs=[_row_spec(W_B)],
        out_specs=[_row_spec(W_B)] * 2,
        out_shape=[jax.ShapeDtypeStruct((n, W_B), F32)] * 2,
        compiler_params=_params(("parallel",)),
        name="rope_tables",
    )(ang)


def _ret_prep_kernel(q_ref, k_ref, cos_ref, sin_ref, q_o, k_o):
    cos = cos_ref[...]
    sin = sin_ref[...]
    lane = lax.broadcasted_iota(jnp.int32, (TM, W_B), 1)
    first_half = (lane % HEAD_DIM) < (HEAD_DIM // 2)

    def rope(x):
        partner = jnp.where(first_half, pltpu.roll(x, W_B - HEAD_DIM // 2, 1), pltpu.roll(x, HEAD_DIM // 2, 1))
        return x * cos + partner * sin

    q_o[...] = rope(q_ref[...])
    k_o[...] = rope(k_ref[...]) * (HEAD_DIM ** -0.5)


def _ret_prep(cb, cos, sin, table_block):
    n = cb.shape[0]
    tab = pl.BlockSpec((TM, W_B), lambda i: (table_block(i), 0))
    return pl.pallas_call(
        _ret_prep_kernel,
        grid=(n // TM,),
        in_specs=[_row_spec(W_B, 0), _row_spec(W_B, 1), tab, tab],
        out_specs=[_row_spec(W_B)] * 2,
        out_shape=[jax.ShapeDtypeStruct((n, W_B), F32)] * 2,
        compiler_params=_params(("parallel",)),
        name="ret_prep",
    )(cb, cb, cos, sin)


def _ssm_prep_kernel(*refs, tiles_per_seq):
    n_shift_refs = 1 if tiles_per_seq else CONV_W - 1
    cc_ref = refs[0]
    shift_refs = refs[1:1 + n_shift_refs]
    cw_ref, cb_ref, dtb_ref, alog_ref, ex_ref, xdt_o, b_o, c_o, dec_o, xs_o, da_o = refs[1 + n_shift_refs:]
    cc = cc_ref[...]
    x0 = cc[:, W_C:W_C + CONV_DIM]
    if tiles_per_seq:
        before = shift_refs[0][:, W_C:W_C + CONV_DIM]
        first = pl.program_id(0) % tiles_per_seq == 0
        xs_prev = [_shift_rows(x0, before, j, first) for j in range(1, CONV_W)]
    else:
        xs_prev = [r[...] for r in shift_refs]
    cw = cw_ref[...]
    conv = x0 * cw[CONV_W - 1:CONV_W, :] + cb_ref[...]
    for j in range(1, CONV_W):
        conv = conv + xs_prev[j - 1] * cw[CONV_W - 1 - j:CONV_W - j, :]
    act = conv * _sigmoid(conv)
    xs = act[:, 0:W_C]
    dt = _softplus(cc[:, W_C + CONV_DIM:] + dtb_ref[...])
    da = dt * (-jnp.exp(alog_ref[...]))
    da_o[...] = da
    dec_o[...] = jnp.exp(da)
    xdt_o[...] = xs * _dot_hi(dt, ex_ref[...])
    b_o[...] = act[:, W_C:W_C + N_GROUPS * N_STATE]
    c_o[...] = act[:, W_C + N_GROUPS * N_STATE:]
    xs_o[...] = xs


def _ssm_prep(cc, shifted, p, tiles_per_seq):
    n = cc.shape[0]
    gn = N_GROUPS * N_STATE
    if tiles_per_seq:
        shift_args, shift_specs = [cc], [_before_spec(COLS_C_PAD)]
    else:
        shift_args, shift_specs = list(shifted), [_row_spec(CONV_DIM)] * (CONV_W - 1)
    return pl.pallas_call(
        functools.partial(_ssm_prep_kernel, tiles_per_seq=tiles_per_seq),
        grid=(n // TM,),
        in_specs=[_row_spec(COLS_C_PAD)] + shift_specs + [
            _full_spec((CONV_W, CONV_DIM)), _full_spec((1, CONV_DIM)), _full_spec((1, LANES)),
            _full_spec((1, LANES)), _full_spec((LANES, W_C))],
        out_specs=[_row_spec(W_C), _row_spec(gn), _row_spec(gn), _row_spec(LANES), _row_spec(W_C),
                   _row_spec(LANES)],
        out_shape=[jax.ShapeDtypeStruct((n, W_C), F32), jax.ShapeDtypeStruct((n, gn), F32),
                   jax.ShapeDtypeStruct((n, gn), F32), jax.ShapeDtypeStruct((n, LANES), F32),
                   jax.ShapeDtypeStruct((n, W_C), F32), jax.ShapeDtypeStruct((n, LANES), F32)],
        compiler_params=_params(("parallel",)),
        name="ssm_prep",
    )(cc, *shift_args, p["conv_w"], p["conv_b"], p["dt_bias"], p["a_log"], p["expand"])


def _scan_kernel(*refs, mode, ni, tt_len, n_tt):
    if mode == "rwkv":
        q_ref, d_ref, a_ref, b_ref, n_ref, a2_ref, s0_ref, y_ref, so_ref, st = refs
    else:
        q_ref, d_ref, a_ref, b_ref, s0_ref, y_ref, so_ref, st = refs
    head = pl.program_id(0)
    tt = pl.program_id(1)

    @pl.when(tt == 0)
    def _():
        st[...] = s0_ref[...]

    def row(ref, t, i):
        return ref[t, pl.ds(i, 1), :]

    def step(t, carry):
        bv = b_ref[t]
        if mode == "rwkv":
            sa_parts = [jnp.zeros_like(bv), jnp.zeros_like(bv)]
            for i in range(ni):
                sa_parts[i % 2] = sa_parts[i % 2] + st[i] * row(n_ref, t, i)
            sa = sa_parts[0] + sa_parts[1]
        elif mode == "ssd":
            d = d_ref[t, pl.ds(head, 1), :]
        else:
            d = d_ref[pl.ds(head, 1), :]
        y_parts = [jnp.zeros_like(bv), jnp.zeros_like(bv)]
        for i in range(ni):
            if mode == "rwkv":
                s = st[i] * row(d_ref, t, i) + row(a_ref, t, i) * bv + row(a2_ref, t, i) * sa
            else:
                s = st[i] * d + row(a_ref, t, i) * bv
            st[i] = s
            y_parts[i % 2] = y_parts[i % 2] + s * row(q_ref, t, i)
        y_ref[t] = y_parts[0] + y_parts[1]
        return carry

    lax.fori_loop(0, tt_len, step, 0)

    @pl.when(tt == n_tt - 1)
    def _():
        so_ref[...] = st[...]


def _scan(mode, n_heads, ni, nj, q, d, a, b, s0, n=None, a2=None):
    t_len = b[0].shape[0]

    def rows(op, width):
        arr, block_of = op
        return arr, pl.BlockSpec((t_len, width, LANES), lambda h, t, f=block_of: (0, f(h), 0))

    ops = [rows(q, ni)]
    if mode == "rwkv":
        ops.append(rows(d, ni))
    elif mode == "ssd":
        ops.append((d, pl.BlockSpec((t_len, SUBLANES, LANES), lambda h, t: (0, 0, 0))))
    else:
        ops.append((d, pl.BlockSpec((SUBLANES, LANES), lambda h, t: (0, 0))))
    ops += [rows(a, ni), rows(b, nj)]
    if mode == "rwkv":
        ops += [rows(n, ni), rows(a2, ni)]
    s_spec = pl.BlockSpec((ni, nj, LANES), lambda h, t: (h, 0, 0))
    y_spec = pl.BlockSpec((t_len, nj, LANES), lambda h, t: (0, h, 0))
    return pl.pallas_call(
        functools.partial(_scan_kernel, mode=mode, ni=ni, tt_len=t_len, n_tt=1),
        grid=(n_heads, 1),
        in_specs=[spec for _, spec in ops] + [s_spec],
        out_specs=[y_spec, s_spec],
        out_shape=[jax.ShapeDtypeStruct((t_len, n_heads * nj, LANES), F32),
                   jax.ShapeDtypeStruct((n_heads * ni, nj, LANES), F32)],
        scratch_shapes=[pltpu.VMEM((ni, nj, LANES), F32)],
        compiler_params=_params(("parallel", "arbitrary")),
        name="scan_" + mode,
    )(*[arr for arr, _ in ops], s0)


def _rwkv_long_kernel(q_ref, d_ref, a_ref, b_ref, n1_ref, a2_ref, y_ref, so_ref, st, sa_s, *, ni, tt_len, n_tt):
    tt = pl.program_id(1)

    @pl.when(tt == 0)
    def _():
        st[...] = jnp.zeros_like(st)
        sa_s[...] = jnp.zeros_like(sa_s)

    def row(ref, t, i):
        return ref[t, pl.ds(i, 1), :]

    def fold(x):
        return x + pltpu.roll(x, LANES // 2, 1)

    def step(t, sa):
        bv = b_ref[t]
        a, a2, q, n1 = a_ref[t], a2_ref[t], q_ref[t], n1_ref[t]
        aq, a2q, an, a2n = [fold(jnp.sum(u * w, axis=0, keepdims=True))
                            for u, w in ((a, q), (a2, q), (a, n1), (a2, n1))]
        y0 = [jnp.zeros_like(bv), jnp.zeros_like(bv)]
        n0 = [jnp.zeros_like(bv), jnp.zeros_like(bv)]
        for i in range(ni):
            sd = st[i] * row(d_ref, t, i)
            st[i] = sd
            y0[i % 2] = y0[i % 2] + sd * row(q_ref, t, i)
            n0[i % 2] = n0[i % 2] + sd * row(n1_ref, t, i)
        y_ref[t] = fold(y0[0] + y0[1]) + bv * aq + sa * a2q
        sa_next = fold(n0[0] + n0[1]) + bv * an + sa * a2n
        for i in range(ni):
            st[i] = st[i] + row(a_ref, t, i) * bv + row(a2_ref, t, i) * sa
        return sa_next

    sa_s[...] = lax.fori_loop(0, tt_len, step, sa_s[...])

    @pl.when(tt == n_tt - 1)
    def _():
        so_ref[...] = st[...]


def _rwkv_long_scan(q, d, a, b, n1, a2, tt_len=64):
    t_len, ni, _ = q.shape
    nj = b.shape[1]
    n_tt = t_len // tt_len
    i_spec = pl.BlockSpec((tt_len, ni, LANES), lambda g, t: (t, 0, 0))
    j_spec = pl.BlockSpec((tt_len, nj, LANES), lambda g, t: (t, 0, 0))
    s_spec = pl.BlockSpec((ni, nj, LANES), lambda g, t: (0, 0, 0))
    return pl.pallas_call(
        functools.partial(_rwkv_long_kernel, ni=ni, tt_len=tt_len, n_tt=n_tt),
        grid=(1, n_tt),
        in_specs=[i_spec, i_spec, i_spec, j_spec, i_spec, i_spec],
        out_specs=[j_spec, s_spec],
        out_shape=[jax.ShapeDtypeStruct((t_len, nj, LANES), F32), jax.ShapeDtypeStruct((ni, nj, LANES), F32)],
        scratch_shapes=[pltpu.VMEM((ni, nj, LANES), F32), pltpu.VMEM((nj, LANES), F32)],
        compiler_params=_params(("parallel", "arbitrary")),
        name="scan_rwkv_long",
    )(q, d, a, b, n1, a2)


TR = 128
HALF = LANES // K_SPLIT


def _to_lanes_kernel(*refs, n_ops, rows, per_half):
    x_refs, o_refs, y2d = refs[:n_ops], refs[n_ops:2 * n_ops], refs[2 * n_ops]
    n_b = x_refs[0].shape[0]
    for x_ref, o_ref in zip(x_refs, o_refs):
        for b in range(n_b):
            xt = x_ref[b].T
            for h in range(H_PAD):
                for s in range(K_SPLIT):
                    lane = s * HALF + b * H_PAD + h
                    f0 = h * HEAD_DIM + (s * rows if per_half else 0)
                    y2d[pl.ds(lane, rows, stride=LANES), :] = xt[f0:f0 + rows, :]
        for r in range(rows):
            o_ref[pl.ds(r, TR, stride=rows), :] = y2d[r * LANES:(r + 1) * LANES, :].T


def _long_to_lanes(xs, b, rows, per_half):
    n_ops = len(xs)
    t = xs[0].shape[0] // b
    width = H_PAD * HEAD_DIM
    outs = pl.pallas_call(
        functools.partial(_to_lanes_kernel, n_ops=n_ops, rows=rows, per_half=per_half),
        grid=(t // TR,),
        in_specs=[pl.BlockSpec((b, TR, width), lambda i: (0, i, 0))] * n_ops,
        out_specs=[pl.BlockSpec((TR * rows, LANES), lambda i: (i, 0))] * n_ops,
        out_shape=[jax.ShapeDtypeStruct((t * rows, LANES), F32)] * n_ops,
        scratch_shapes=[pltpu.VMEM((rows * LANES, TR), F32)],
        compiler_params=_params(("parallel",)),
        name="to_lanes",
    )(*[x.reshape(b, t, width) for x in xs])
    return [o.reshape(t, rows, LANES) for o in outs]


def _from_lanes_kernel(y_ref, o_ref, z2d):
    for v in range(HEAD_DIM):
        z2d[pl.ds(v, LANES, stride=HEAD_DIM), :] = y_ref[pl.ds(v, TR, stride=HEAD_DIM), :].T
    for b in range(o_ref.shape[0]):
        r0 = b * H_PAD * HEAD_DIM
        o_ref[b] = z2d[r0:r0 + W_A, :].T


def _long_from_lanes(y, b):
    t = y.shape[0]
    out = pl.pallas_call(
        _from_lanes_kernel,
        grid=(t // TR,),
        in_specs=[pl.BlockSpec((TR * HEAD_DIM, LANES), lambda i: (i, 0))],
        out_specs=pl.BlockSpec((b, TR, W_A), lambda i: (0, i, 0)),
        out_shape=jax.ShapeDtypeStruct((b, t, W_A), F32),
        scratch_shapes=[pltpu.VMEM((LANES * HEAD_DIM, TR), F32)],
        compiler_params=_params(("parallel",)),
        name="from_lanes",
    )(y.reshape(t * HEAD_DIM, LANES))
    return out.reshape(b * t, W_A)


def _ret_chunk_kernel(q_ref, k_ref, v_ref, y_ref, s_ref, st):
    c = pl.program_id(1)

    @pl.when(c == 0)
    def _():
        st[...] = jnp.zeros_like(st)

    q = q_ref[...]
    k = k_ref[...]
    vb = v_ref[...].astype(BF16)
    kb = k.astype(BF16)
    row = lax.broadcasted_iota(jnp.int32, (LC, W_B), 0)
    head = lax.broadcasted_iota(jnp.int32, (LC, W_B), 1) // HEAD_DIM
    log_g = [math.log(1.0 - 2.0 ** (-5.0 - h)) for h in range(H_B)]
    lg = jnp.full((LC, W_B), log_g[0], F32)
    for h in range(1, H_B):
        lg = jnp.where(head == h, log_g[h], lg)
    rowf = row.astype(F32)
    diff = (lax.broadcasted_iota(jnp.int32, (LC, LC), 0) - lax.broadcasted_iota(jnp.int32, (LC, LC), 1))
    causal = diff >= 0
    difff = jnp.maximum(diff, 0).astype(F32)

    out = jnp.dot((q * jnp.exp(lg * (rowf + 1.0))).astype(BF16), st[...].astype(BF16), preferred_element_type=F32)
    for h in range(H_B):
        qm = jnp.where(head == h, q, 0.0).astype(BF16)
        s = lax.dot_general(qm, kb, (((1,), (1,)), ((), ())), preferred_element_type=F32)
        p = jnp.where(causal, s * jnp.exp(log_g[h] * difff), 0.0).astype(BF16)
        out = out + jnp.where(head == h, jnp.dot(p, vb, preferred_element_type=F32), 0.0)
    y_ref[...] = out

    kt = (k * jnp.exp(lg * (LC - 1.0 - rowf))).T.astype(BF16)
    kv = jnp.dot(kt, vb, preferred_element_type=F32)
    r2 = lax.broadcasted_iota(jnp.int32, (W_B, W_B), 0) // HEAD_DIM
    c2 = lax.broadcasted_iota(jnp.int32, (W_B, W_B), 1) // HEAD_DIM
    cdec = jnp.full((W_B, W_B), math.exp(log_g[0] * LC), F32)
    for h in range(1, H_B):
        cdec = jnp.where(r2 == h, math.exp(log_g[h] * LC), cdec)
    st[...] = st[...] * cdec + jnp.where(r2 == c2, kv, 0.0)
    s_ref[0] = st[...]


def _ret_chunk(q, k, cb, n_seq, t_len):
    n_c = t_len // LC
    rows = lambda col: pl.BlockSpec((LC, W_B), lambda b, c, col=col: (b * n_c + c, col))
    return pl.pallas_call(
        _ret_chunk_kernel,
        grid=(n_seq, n_c),
        in_specs=[rows(0), rows(0), rows(2)],
        out_specs=[rows(0), pl.BlockSpec((1, W_B, W_B), lambda b, c: (b, 0, 0))],
        out_shape=[jax.ShapeDtypeStruct((n_seq * t_len, W_B), F32), jax.ShapeDtypeStruct((n_seq, W_B, W_B), F32)],
        scratch_shapes=[pltpu.VMEM((W_B, W_B), F32)],
        compiler_params=_params(("parallel", "arbitrary")),
        name="ret_chunk",
    )(q, k, cb)


def _ssd_chunk_kernel(x_ref, b_ref, c_ref, da_ref, ex_ref, y_ref, h_ref, st):
    ci = pl.program_id(1)

    @pl.when(ci == 0)
    def _():
        st[...] = jnp.zeros_like(st)

    gn = N_STATE
    half = W_C // N_GROUPS
    ex = ex_ref[...]
    x = x_ref[...]
    xb = x.astype(BF16)
    ri = lax.broadcasted_iota(jnp.int32, (LC, LC), 0)
    cj = lax.broadcasted_iota(jnp.int32, (LC, LC), 1)
    causal = ri >= cj
    cum = _dot_hi(causal.astype(F32), da_ref[...])
    cum_e = _dot_hi(cum, ex)
    last_e = cum_e[LC - 1:LC, :]
    cum_t = cum.T
    head = lax.broadcasted_iota(jnp.int32, (LC, W_C), 1) // HEAD_DIM
    bg = [b_ref[:, g * gn:(g + 1) * gn].astype(BF16) for g in range(N_GROUPS)]
    cg = [c_ref[:, g * gn:(g + 1) * gn].astype(BF16) for g in range(N_GROUPS)]
    cb = [lax.dot_general(cg[g], bg[g], (((1,), (1,)), ((), ())), preferred_element_type=F32)
          for g in range(N_GROUPS)]

    hb = st[...].astype(BF16)
    ys = [lax.dot_general(cg[g], hb, (((1,), (1,)), ((), ())), preferred_element_type=F32)
          for g in range(N_GROUPS)]
    y = jnp.where(head < H_C // N_GROUPS, ys[0], ys[1]) * jnp.exp(cum_e)
    for h in range(H_C):
        seg = cum[:, h:h + 1] - cum_t[h:h + 1, :]
        p = (cb[h // (H_C // N_GROUPS)] * jnp.exp(jnp.where(causal, seg, -jnp.inf))).astype(BF16)
        y = y + jnp.where(head == h, jnp.dot(p, xb, preferred_element_type=F32), 0.0)
    y_ref[...] = y

    xt = (x * jnp.exp(last_e - cum_e)).T.astype(BF16)
    upd = [jnp.dot(xt, bg[g], preferred_element_type=F32) for g in range(N_GROUPS)]
    rowi = lax.broadcasted_iota(jnp.int32, (W_C, gn), 0)
    sel = lax.broadcasted_iota(jnp.int32, (W_C, LANES), 1) == lax.broadcasted_iota(jnp.int32, (W_C, LANES), 0) // HEAD_DIM
    tot = jnp.sum(jnp.where(sel, cum[LC - 1:LC, :], 0.0), axis=1, keepdims=True)
    st[...] = st[...] * jnp.exp(tot) + jnp.where(rowi < half, upd[0], upd[1])
    h_ref[0] = st[...]


def _ssd_chunk(xdt, bm, cm, da, ex, n_seq, t_len):
    n_c = t_len // LC
    gn = N_GROUPS * N_STATE
    rows = lambda w: pl.BlockSpec((LC, w), lambda b, c: (b * n_c + c, 0))
    return pl.pallas_call(
        _ssd_chunk_kernel,
        grid=(n_seq, n_c),
        in_specs=[rows(W_C), rows(gn), rows(gn), rows(LANES), pl.BlockSpec((LANES, W_C), lambda b, c: (0, 0))],
        out_specs=[rows(W_C), pl.BlockSpec((1, W_C, N_STATE), lambda b, c: (b, 0, 0))],
        out_shape=[jax.ShapeDtypeStruct((n_seq * t_len, W_C), F32),
                   jax.ShapeDtypeStruct((n_seq, W_C, N_STATE), F32)],
        scratch_shapes=[pltpu.VMEM((W_C, N_STATE), F32)],
        compiler_params=_params(("parallel", "arbitrary")),
        name="ssd_chunk",
    )(xdt, bm, cm, da, ex)


def _post_kernel(x_ref, ya_ref, bonus_ref, ga_ref, ob_ref, gb_ref, yc_ref, xs_ref, z_ref, lnw_ref, lnb_ref,
                 rn_ref, dsk_ref, sn_ref, bda_ref, bdb_ref, wo_ref, out_ref):
    inv_hd = 1.0 / HEAD_DIM
    y = ya_ref[...]
    bda = bda_ref[...]
    mean = _dot_hi(y, bda) * inv_hd
    yd = y - mean
    var = _dot_hi(yd * yd, bda) * inv_hd
    ya = (yd * lax.rsqrt(var + GN_EPS) * lnw_ref[...] + lnb_ref[...] + bonus_ref[...]) * ga_ref[...]
    o = ob_ref[...]
    ms = _dot_hi(o * o, bdb_ref[...]) * inv_hd
    gb = gb_ref[...]
    yb = (gb * _sigmoid(gb)) * (o * lax.rsqrt(ms + RMS_EPS) * rn_ref[...])
    z = z_ref[...]
    yc = (yc_ref[...] + dsk_ref[...] * xs_ref[...]) * (z * _sigmoid(z))
    yc = _rmsnorm(yc, sn_ref[...], GATED_NORM_EPS)
    out_ref[...] = (x_ref[...] + _dot(ya, wo_ref[0:W_A, :]) + _dot(yb, wo_ref[W_A:W_A + W_B, :])
                    + _dot(yc, wo_ref[W_A + W_B:, :]))


def _post(x, ya, bonus, ga, ob, cb, yc, xs, cc, p):
    n = x.shape[0]
    va = _full_spec((1, W_A))
    return pl.pallas_call(
        _post_kernel,
        grid=(n // TM,),
        in_specs=[_row_spec(D_MODEL), _row_spec(W_A), _row_spec(W_A), _row_spec(W_A), _row_spec(W_B),
                  _row_spec(W_B, 3), _row_spec(W_C), _row_spec(W_C), _row_spec(W_C, 0), va, va,
                  _full_spec((1, W_B)), va, va, _full_spec((W_A, W_A)), _full_spec((W_B, W_B)),
                  _full_spec((D_MODEL, D_MODEL))],
        out_specs=_row_spec(D_MODEL),
        out_shape=jax.ShapeDtypeStruct((n, D_MODEL), F32),
        compiler_params=_params(("parallel",)),
        name="post_outproj",
    )(x, ya, bonus, ga, ob, cb, yc, xs, cc, p["ln_w"], p["ln_b"], p["ret_norm"], p["d_skip"], p["ssm_norm"],
      p["bd_a"], p["bd_b"], p["w_out"])


def _ffn_kernel(*refs, gated, final_norm, n_e):
    refs = list(refs)
    x_ref, nf_ref = refs[:2]
    pos = 2
    if gated:
        router_ref = refs[pos]
        pos += 1
    wg_ref, wu_ref, wd_ref = refs[pos:pos + 3]
    pos += 3
    if final_norm:
        nfin_ref = refs[pos]
        pos += 1
    out_ref = refs[pos]
    hb_s, acc_s = refs[pos + 1:pos + 3]
    if gated:
        gate_s, asg_s, pos_s, asg_t, pos_t = refs[pos + 3:pos + 8]
    e = pl.program_id(1)

    @pl.when(e == 0)
    def _():
        h = _rmsnorm(x_ref[...], nf_ref[...], RMS_EPS)
        hb_s[...] = h.astype(BF16)
        acc_s[...] = jnp.zeros_like(acc_s)
        if gated:
            lane = lax.broadcasted_iota(jnp.int32, (TM, LANES), 1)
            logits = jnp.where(lane < N_EXPERTS, _dot_hi(h, router_ref[...]), -jnp.inf)
            p = jnp.exp(logits - jnp.max(logits, axis=-1, keepdims=True))
            p = p / jnp.sum(p, axis=-1, keepdims=True)
            p1 = jnp.max(p, axis=-1, keepdims=True)
            i1 = jnp.min(jnp.where(p == p1, lane, LANES), axis=-1, keepdims=True)
            rest = jnp.where(lane == i1, -1.0, p)
            p2 = jnp.max(rest, axis=-1, keepdims=True)
            i2 = jnp.min(jnp.where(rest == p2, lane, LANES), axis=-1, keepdims=True)
            gate_s[...] = jnp.where(lane == i1, p1, jnp.where(lane == i2, p2, 0.0)) / (p1 + p2)
            assigned = jnp.where((lane == i1) | (lane == i2), 1.0, 0.0)
            earlier = (lax.broadcasted_iota(jnp.int32, (TM, TM), 0) > lax.broadcasted_iota(jnp.int32, (TM, TM), 1))
            rank = jnp.dot(earlier.astype(BF16), assigned.astype(BF16), preferred_element_type=F32)
            asg_s[...] = assigned
            pos_s[...] = rank
            asg_t[...] = assigned.T
            pos_t[...] = rank.T

    if not gated:
        hb = hb_s[...]
        g = jnp.dot(hb, wg_ref[0], preferred_element_type=F32)
        u = jnp.dot(hb, wu_ref[0], preferred_element_type=F32)
        acc_s[...] += _dot(g * _sigmoid(g) * u, wd_ref[0])
    else:
        lane = lax.broadcasted_iota(jnp.int32, (TM, LANES), 1)
        mine = lane == e
        col = lambda ref: jnp.sum(jnp.where(mine, ref[...], 0.0), axis=-1, keepdims=True)
        gate_c, asg_c, pos_c = col(gate_s), col(asg_s), col(pos_s)
        asg_r, pos_r = asg_t[pl.ds(e, 1), :], pos_t[pl.ds(e, 1), :]
        n_chunks = (jnp.sum(asg_r).astype(jnp.int32) + MOE_CHUNK - 1) // MOE_CHUNK

        def chunk(c, carry):
            base = (c * MOE_CHUNK).astype(F32)
            slot_r = lax.broadcasted_iota(jnp.int32, (MOE_CHUNK, TM), 0).astype(F32) + base
            pick = jnp.where((pos_r == slot_r) & (asg_r > 0.0), 1.0, 0.0).astype(BF16)
            rows = jnp.dot(pick, hb_s[...], preferred_element_type=F32).astype(BF16)
            g = jnp.dot(rows, wg_ref[0], preferred_element_type=F32)
            u = jnp.dot(rows, wu_ref[0], preferred_element_type=F32)
            o = _dot(g * _sigmoid(g) * u, wd_ref[0])
            o_hi = o.astype(BF16)
            o_lo = (o - o_hi.astype(F32)).astype(BF16)
            slot_c = lax.broadcasted_iota(jnp.int32, (TM, MOE_CHUNK), 1).astype(F32) + base
            place = jnp.where((pos_c == slot_c) & (asg_c > 0.0), 1.0, 0.0).astype(BF16)
            back = (jnp.dot(place, o_hi, preferred_element_type=F32)
                    + jnp.dot(place, o_lo, preferred_element_type=F32))
            acc_s[...] += gate_c * back
            return carry

        lax.fori_loop(0, n_chunks, chunk, 0)

    @pl.when(e == n_e - 1)
    def _():
        y = x_ref[...] + acc_s[...]
        if final_norm:
            y = _rmsnorm(y, nfin_ref[...], RMS_EPS)
        out_ref[...] = y


def _ffn(x, nf, wg, wu, wd, router=None, nfin=None):
    n = x.shape[0]
    n_e, _, dff = wg.shape
    gated = router is not None
    final_norm = nfin is not None
    vec = pl.BlockSpec((1, D_MODEL), lambda i, e: (0, 0))
    args = [x, nf]
    specs = [pl.BlockSpec((TM, D_MODEL), lambda i, e: (i, 0)), vec]
    if gated:
        args.append(router)
        specs.append(pl.BlockSpec((D_MODEL, LANES), lambda i, e: (0, 0)))
    args += [wg, wu, wd]
    specs += [pl.BlockSpec((1, D_MODEL, dff), lambda i, e: (e, 0, 0)),
              pl.BlockSpec((1, D_MODEL, dff), lambda i, e: (e, 0, 0)),
              pl.BlockSpec((1, dff, D_MODEL), lambda i, e: (e, 0, 0))]
    if final_norm:
        args.append(nfin)
        specs.append(vec)
    scratch = [pltpu.VMEM((TM, D_MODEL), BF16), pltpu.VMEM((TM, D_MODEL), F32)]
    if gated:
        scratch += [pltpu.VMEM((TM, LANES), F32)] * 3 + [pltpu.VMEM((LANES, TM), F32)] * 2
    return pl.pallas_call(
        functools.partial(_ffn_kernel, gated=gated, final_norm=final_norm, n_e=n_e),
        grid=(n // TM, n_e),
        in_specs=specs,
        out_specs=pl.BlockSpec((TM, D_MODEL), lambda i, e: (i, 0)),
        out_shape=jax.ShapeDtypeStruct((n, D_MODEL), F32),
        scratch_shapes=scratch,
        compiler_params=_params(("parallel", "arbitrary")),
        name="moe" if gated else "ffn",
    )(*args)


def _short_to_lanes_kernel(*refs, n_ops, t_len):
    for x_ref, o_ref in zip(refs[:n_ops], refs[n_ops:]):
        for t in range(t_len):
            o_ref[t] = x_ref[pl.ds(t, LANES, stride=t_len), :].T


def _short_to_lanes(xs, t_len):
    n_chunks = [width // LANES for _, _, width in xs]
    grid = max(n_chunks)
    clamp = lambda c, n: jnp.minimum(c, n - 1)
    outs = pl.pallas_call(
        functools.partial(_short_to_lanes_kernel, n_ops=len(xs), t_len=t_len),
        grid=(grid,),
        in_specs=[pl.BlockSpec((LANES * t_len, LANES), lambda c, c0=col // LANES, n=n: (0, c0 + clamp(c, n)))
                  for (_, col, _), n in zip(xs, n_chunks)],
        out_specs=[pl.BlockSpec((t_len, LANES, LANES), lambda c, n=n: (0, clamp(c, n), 0)) for n in n_chunks],
        out_shape=[jax.ShapeDtypeStruct((t_len, width, LANES), F32) for _, _, width in xs],
        compiler_params=_params(("arbitrary",)),
        name="short_to_lanes",
    )(*[x for x, _, _ in xs])
    return outs


def _short_from_lanes_kernel(y_ref, o_ref, *, t_len):
    for t in range(t_len):
        o_ref[pl.ds(t, LANES, stride=t_len), :] = y_ref[t].T


def _short_from_lanes(y):
    t_len, w, _ = y.shape
    return pl.pallas_call(
        functools.partial(_short_from_lanes_kernel, t_len=t_len),
        grid=(w // LANES,),
        in_specs=[pl.BlockSpec((t_len, LANES, LANES), lambda c: (0, c, 0))],
        out_specs=pl.BlockSpec((LANES * t_len, LANES), lambda c: (0, c)),
        out_shape=jax.ShapeDtypeStruct((LANES * t_len, w), F32),
        compiler_params=_params(("parallel",)),
        name="short_from_lanes",
    )(y)


def _state_to_lanes_kernel(x_ref, o_ref, *, ni, nj, j_first):
    xt = x_ref[...].T
    if j_first:
        for j in range(nj):
            o_ref[pl.ds(j, ni, stride=nj), :] = xt[j * ni:(j + 1) * ni, :]
    else:
        o_ref[...] = xt


def _state_to_lanes(s, ni, nj, j_first):
    h = s.shape[1]
    out = pl.pallas_call(
        functools.partial(_state_to_lanes_kernel, ni=ni, nj=nj, j_first=j_first),
        grid=(h,),
        in_specs=[pl.BlockSpec((LANES, ni * nj), lambda i: (0, i))],
        out_specs=pl.BlockSpec((ni * nj, LANES), lambda i: (i, 0)),
        out_shape=jax.ShapeDtypeStruct((h * ni * nj, LANES), F32),
        compiler_params=_params(("parallel",)),
        name="state_to_lanes",
    )(s.reshape(LANES, h * ni * nj))
    return out.reshape(h * ni, nj, LANES)


def _state_from_lanes_kernel(s_ref, o_ref, tmp, *, ni, nj, j_first):
    if j_first:
        for j in range(nj):
            tmp[j * ni:(j + 1) * ni, :] = s_ref[pl.ds(j, ni, stride=nj), :]
        o_ref[...] = tmp[...].T
    else:
        o_ref[...] = s_ref[...].T


def _state_from_lanes(s, n_heads, ni, nj, j_first):
    out = pl.pallas_call(
        functools.partial(_state_from_lanes_kernel, ni=ni, nj=nj, j_first=j_first),
        grid=(n_heads,),
        in_specs=[pl.BlockSpec((ni * nj, LANES), lambda i: (i, 0))],
        out_specs=pl.BlockSpec((LANES, ni * nj), lambda i: (0, i)),
        out_shape=jax.ShapeDtypeStruct((LANES, n_heads * ni * nj), F32),
        scratch_shapes=[pltpu.VMEM((ni * nj, LANES), F32)],
        compiler_params=_params(("parallel",)),
        name="state_from_lanes",
    )(s.reshape(n_heads * ni * nj, LANES))
    return out.reshape((LANES, n_heads, nj, ni) if j_first else (LANES, n_heads, ni, nj))


def _rwkv_long(feats, b):
    r, dec, k2, v, nkk, beta = feats
    t = r.shape[0] // b
    kl = HEAD_DIM // K_SPLIT
    assert b * H_PAD == HALF and t % TR == 0
    nkk_next = jnp.concatenate([nkk[1:], jnp.zeros((1, nkk.shape[1]), F32)], axis=0)
    q, d, a, n1, a2 = _long_to_lanes([r, dec, k2, nkk_next, beta], b, kl, True)
    (vv,) = _long_to_lanes([v], b, HEAD_DIM, False)
    y, s = _rwkv_long_scan(q, d, a, vv, n1, a2)
    s = s.reshape(kl, HEAD_DIM, K_SPLIT, b, H_PAD)[..., :H_A]
    return _long_from_lanes(y, b), jnp.transpose(s, (3, 4, 1, 2, 0)).reshape(b, H_A, HEAD_DIM, HEAD_DIM)


def _rwkv_short(feats, t_len, s0):
    r, dec, k2, v, nkk, beta = _short_to_lanes([(f, 0, W_A) for f in feats], t_len)
    hd = HEAD_DIM
    own = lambda h: h
    y, s = _scan("rwkv", H_A, hd, hd, (r, own), (dec, own), (k2, own), (v, own),
                 _state_to_lanes(s0, hd, hd, True), n=(nkk, own), a2=(beta, own))
    return _short_from_lanes(y), _state_from_lanes(s, H_A, hd, hd, True)


def _ret_short(qh, kh, cb, t_len, s0):
    q, k, v = _short_to_lanes([(qh, 0, W_B), (kh, 0, W_B), (cb, 2 * W_B, W_B)], t_len)
    hd = HEAD_DIM
    gam = np.ones((SUBLANES, LANES), np.float32)
    gam[:H_B] = (1.0 - 2.0 ** (-5.0 - np.arange(H_B, dtype=np.float64)))[:, None]
    own = lambda h: h
    y, s = _scan("ret", H_B, hd, hd, (q, own), jnp.asarray(gam), (k, own), (v, own),
                 _state_to_lanes(s0, hd, hd, False))
    return _short_from_lanes(y), _state_from_lanes(s, H_B, hd, hd, False)


def _ssd_short(xdt, bm, cm, dssm, t_len, s0):
    gn = N_GROUPS * N_STATE
    x, bl, cl, dl = _short_to_lanes([(xdt, 0, W_C), (bm, 0, gn), (cm, 0, gn), (dssm, 0, LANES)], t_len)
    group = lambda h: h // (H_C // N_GROUPS)
    y, s = _scan("ssd", H_C, N_STATE, HEAD_DIM, (cl, group), dl, (bl, group), (x, lambda h: h),
                 _state_to_lanes(s0, N_STATE, HEAD_DIM, True))
    return _short_from_lanes(y), _state_from_lanes(s, H_C, N_STATE, HEAD_DIM, True)


def _block_diag_ones(width):
    idx = np.arange(width) // HEAD_DIM
    return jnp.asarray((idx[:, None] == idx[None, :]).astype(np.float32))


def _shifted(x, prev_rows, shift):
    b, t, c = x.shape
    p = prev_rows.shape[1]
    full = jnp.concatenate([prev_rows, x], axis=1)
    return full[:, p - shift:p - shift + t].reshape(b * t, c)


def _layer(x, b, p, st, rope, table_block):
    n = x.shape[0]
    t = n // b
    fresh = st is None
    tiles_per_seq = t // TM if fresh else 0
    ca, cb, cc = _in_proj(x, p["norm_mix"], p["w_in"])
    ca3 = ca.reshape(b, t, COLS_A)
    xbc_tail = cc.reshape(b, t, COLS_C_PAD)[:, -(CONV_W - 1):, W_C:W_C + CONV_DIM]
    if fresh:
        prev, shifted = None, None
        conv_new = xbc_tail
    else:
        prev = _shifted(ca3, st["shift"][:, None, :], 1)
        xbc = cc[:, W_C:W_C + CONV_DIM].reshape(b, t, CONV_DIM)
        shifted = [_shifted(xbc, st["conv"], j) for j in range(1, CONV_W)]
        conv_new = jnp.concatenate([st["conv"], xbc_tail], axis=1)[:, -(CONV_W - 1):]

    r, dec, k2, v, nkk, beta, ga, bonus = _rwkv_prep(ca, prev, p, tiles_per_seq)
    qh, kh = _ret_prep(cb, rope[0], rope[1], table_block)
    xdt, bm, cm, dssm, xs, da = _ssm_prep(cc, shifted, p, tiles_per_seq)

    feats = (r, dec, k2, v, nkk, beta)
    if fresh:
        ya, s_rwkv = _rwkv_long(feats, b)
        yb, s_full = _ret_chunk(qh, kh, cb, b, t)
        s_ret = jnp.stack([s_full[:, h * HEAD_DIM:(h + 1) * HEAD_DIM, h * HEAD_DIM:(h + 1) * HEAD_DIM]
                           for h in range(H_B)], axis=1)
        yc, s_ssm = _ssd_chunk(xdt, bm, cm, da, p["expand"], b, t)
        s_ssm = s_ssm.reshape(b, H_C, HEAD_DIM, N_STATE)
    else:
        assert b == LANES
        ya, s_rwkv = _rwkv_short(feats, t, st["rwkv"])
        yb, s_ret = _ret_short(qh, kh, cb, t, st["ret"])
        yc, s_ssm = _ssd_short(xdt, bm, cm, dssm, t, st["ssm"])

    x2 = _post(x, ya, bonus, ga, yb, cb, yc, xs, cc, p)
    x = _ffn(x2, p["norm_ffn"], p["wg"], p["wu"], p["wd"], router=p["router"], nfin=p["norm_final"])
    return x, (s_rwkv, ca3[:, -1], s_ret, s_ssm, conv_new)


def kernel(x_prompt, x_sample, state_rwkv, state_shift, state_ret, state_ssm, state_conv, norm_mix, w_in, rwkv_mu, rwkv_w0, rwkv_w_up, rwkv_a0, rwkv_a_up, rwkv_g_up, rwkv_k_k, rwkv_k_a, rwkv_r_k, rwkv_ln_w, rwkv_ln_b, ret_norm, ssm_conv_w, ssm_conv_b, ssm_dt_bias, ssm_a_log, ssm_d, ssm_norm, w_out, norm_ffn, ffn_w_gate, ffn_w_up, ffn_w_down, moe_router, moe_w_gate, moe_w_up, moe_w_down, norm_final):
    bp, tp, _ = x_prompt.shape
    bs, ts, _ = x_sample.shape
    depth = w_in.shape[0]
    assert tp % TM == 0 and tp % LC == 0 and (bs * ts) % TM == 0 and TM % ts == 0

    rope = _rope_tables(tp, ts)
    bd_a, bd_b = _block_diag_ones(W_A), _block_diag_ones(W_B)
    expand = np.zeros((LANES, W_C), np.float32)
    expand[np.arange(W_C) // HEAD_DIM, np.arange(W_C)] = 1.0
    expand = jnp.asarray(expand)
    row = lambda v: v.reshape(1, -1)
    pad_l = lambda v: jnp.pad(v, (0, LANES - v.shape[0])).reshape(1, LANES)

    xp = x_prompt.reshape(bp * tp, D_MODEL)
    xs = x_sample.reshape(bs * ts, D_MODEL)
    new_p, new_s = [], []
    for i in range(depth):
        j = i // 2
        p = dict(
            norm_mix=row(norm_mix[i]),
            w_in=jnp.pad(w_in[i], ((0, 0), (0, COLS_C_PAD - COLS_C))).astype(BF16),
            mu=row(rwkv_mu[i]), w0=row(rwkv_w0[i]), a0=row(rwkv_a0[i]), k_k=row(rwkv_k_k[i]),
            k_a=row(rwkv_k_a[i]), r_k=row(rwkv_r_k[i]),
            w_up=jnp.pad(rwkv_w_up[i], ((0, AAA_LORA), (0, 0))).astype(BF16),
            a_up=jnp.pad(rwkv_a_up[i], ((DECAY_LORA, 0), (0, 0))).astype(BF16),
            g_up=rwkv_g_up[i].astype(BF16), bd_a=bd_a, bd_b=bd_b,
            conv_w=ssm_conv_w[i], conv_b=row(ssm_conv_b[i]), dt_bias=pad_l(ssm_dt_bias[i]),
            a_log=pad_l(ssm_a_log[i]), expand=expand,
            ln_w=row(rwkv_ln_w[i]), ln_b=row(rwkv_ln_b[i]), ret_norm=row(ret_norm[i]),
            d_skip=row(jnp.repeat(ssm_d[i], HEAD_DIM)), ssm_norm=row(ssm_norm[i]), w_out=w_out[i].astype(BF16),
            norm_ffn=row(norm_ffn[i]), norm_final=row(norm_final) if i == depth - 1 else None)
        if i % 2 == 0:
            dff = ffn_w_gate.shape[-1] // 2
            p.update(router=None,
                     wg=ffn_w_gate[j].reshape(D_MODEL, 2, dff).transpose(1, 0, 2).astype(BF16),
                     wu=ffn_w_up[j].reshape(D_MODEL, 2, dff).transpose(1, 0, 2).astype(BF16),
                     wd=ffn_w_down[j].reshape(2, dff, D_MODEL).astype(BF16))
        else:
            p.update(router=jnp.pad(moe_router[j], ((0, 0), (0, LANES - N_EXPERTS))),
                     wg=moe_w_gate[j].astype(BF16), wu=moe_w_up[j].astype(BF16), wd=moe_w_down[j].astype(BF16))

        xp, st_p = _layer(xp, bp, p, None, rope, lambda t: t % (tp // TM))
        st = dict(rwkv=state_rwkv[i], shift=state_shift[i], ret=state_ret[i], ssm=state_ssm[i],
                  conv=state_conv[i])
        xs, st_s = _layer(xs, bs, p, st, rope, lambda t: tp // TM)
        new_p.append(st_p)
        new_s.append(st_s)

    stack = lambda sts: tuple(jnp.stack(s) for s in zip(*sts))
    return (xp.reshape(bp, tp, D_MODEL), xs.reshape(bs, ts, D_MODEL)) + stack(new_p) + stack(new_s)
```

```python
import functools
import math

import numpy as np
import jax
import jax.numpy as jnp
from jax import lax
from jax.experimental import pallas as pl
from jax.experimental.pallas import tpu as pltpu

F32 = jnp.float32
BF16 = jnp.bfloat16
HIGHEST = lax.Precision.HIGHEST

LANES = 128
SUBLANES = 8
VMEM_LIMIT = 56 * 1024 * 1024

D_MODEL = 1024
HEAD_DIM = 64
H_A, H_B, H_C = 6, 4, 6
W_A, W_B, W_C = H_A * HEAD_DIM, H_B * HEAD_DIM, H_C * HEAD_DIM
DECAY_LORA, AAA_LORA, GATE_LORA = 64, 64, 128
COLS_A = 3 * W_A + DECAY_LORA + AAA_LORA + GATE_LORA
COLS_B = 4 * W_B
N_STATE, N_GROUPS, CONV_W = 128, 2, 4
CONV_DIM = W_C + 2 * N_GROUPS * N_STATE
COLS_C = W_C + CONV_DIM + H_C
COLS_C_PAD = 1408
ROPE_BASE = 10000.0
RMS_EPS = 1e-6
GN_EPS = 64e-5
GATED_NORM_EPS = 1e-5
N_EXPERTS = 8
PAST_LEN = 16384

TM = 512
MOE_CHUNK = 160
LC = 256
H_PAD = 8
K_SPLIT = 2


def _dot(a, b):
    return jnp.dot(a.astype(BF16), b.astype(BF16), preferred_element_type=F32)


def _dot_hi(a, b):
    return jnp.dot(a, b, precision=HIGHEST, preferred_element_type=F32)


def _dot_select(a, sel, terms=2):
    out = None
    rest = a
    for _ in range(terms):
        piece = rest.astype(BF16)
        part = jnp.dot(piece, sel, preferred_element_type=F32)
        out = part if out is None else out + part
        rest = rest - piece.astype(F32)
    return out


def _sigmoid(x):
    return 1.0 / (1.0 + jnp.exp(-x))


def _softplus(x):
    return jnp.maximum(x, 0.0) + jnp.log1p(jnp.exp(-jnp.abs(x)))


def _rmsnorm(x, g, eps):
    return x * lax.rsqrt(jnp.mean(x * x, axis=-1, keepdims=True) + eps) * g


def _params(sem):
    return pltpu.CompilerParams(dimension_semantics=sem, vmem_limit_bytes=VMEM_LIMIT)


def _row_spec(width, col=0):
    return pl.BlockSpec((TM, width), lambda i, c=col: (i, c))


def _before_spec(width):
    return pl.BlockSpec((SUBLANES, width), lambda i: (jnp.maximum(i * (TM // SUBLANES) - 1, 0), 0))


def _full_spec(shape):
    nd = len(shape)
    return pl.BlockSpec(shape, lambda i, n=nd: (0,) * n)


def _shift_rows(x, before, j, first):
    rolled = pltpu.roll(x, j, 0)
    prev = jnp.where(first, 0.0, pltpu.roll(before, j, 0))
    row = lax.broadcasted_iota(jnp.int32, prev.shape, 0)
    top = jnp.where(row < j, prev, rolled[0:SUBLANES])
    return jnp.concatenate([top, rolled[SUBLANES:]], axis=0)


def _in_proj_kernel(x_ref, g_ref, w_ref, oa_ref, ob_ref, oc_ref):
    h = _rmsnorm(x_ref[...], g_ref[...], RMS_EPS).astype(BF16)
    oa_ref[...] = jnp.dot(h, w_ref[:, 0:COLS_A], preferred_element_type=F32)
    ob_ref[...] = jnp.dot(h, w_ref[:, COLS_A:COLS_A + COLS_B], preferred_element_type=F32)
    oc_ref[...] = jnp.dot(h, w_ref[:, COLS_A + COLS_B:], preferred_element_type=F32)


def _in_proj(x, g, w):
    n = x.shape[0]
    wtot = w.shape[1]
    return pl.pallas_call(
        _in_proj_kernel,
        grid=(n // TM,),
        in_specs=[_row_spec(D_MODEL), _full_spec((1, D_MODEL)), _full_spec((D_MODEL, wtot))],
        out_specs=[_row_spec(COLS_A), _row_spec(COLS_B), _row_spec(COLS_C_PAD)],
        out_shape=[jax.ShapeDtypeStruct((n, COLS_A), F32), jax.ShapeDtypeStruct((n, COLS_B), F32),
                   jax.ShapeDtypeStruct((n, COLS_C_PAD), F32)],
        compiler_params=_params(("parallel",)),
        name="in_proj",
    )(x, g, w)


def _rwkv_prep_kernel(c_ref, p_ref, mu_ref, w0_ref, a0_ref, kk_ref, ka_ref, rk_ref, wup_ref, aup_ref,
                      gup_ref, bd_ref, r_o, d_o, k_o, v_o, n_o, b_o, g_o, bonus_o, *, tiles_per_seq):
    c = c_ref[...]
    if tiles_per_seq:
        prev = _shift_rows(c, p_ref[...], 1, pl.program_id(0) % tiles_per_seq == 0)
    else:
        prev = p_ref[...]
    xm = c + (prev - c) * mu_ref[...]
    r = xm[:, 0:W_A]
    k = xm[:, W_A:2 * W_A]
    v = xm[:, 2 * W_A:3 * W_A]
    lora = xm[:, 3 * W_A:3 * W_A + DECAY_LORA + AAA_LORA]
    gd = xm[:, 3 * W_A + DECAY_LORA + AAA_LORA:]
    w = w0_ref[...] + _dot(jnp.tanh(lora), wup_ref[...])
    decay = jnp.exp(-math.exp(-0.5) * _sigmoid(w))
    a = _sigmoid(a0_ref[...] + _dot(lora, aup_ref[...]))
    g = _dot(_sigmoid(gd), gup_ref[...])
    bd = bd_ref[...]
    kk = k * kk_ref[...]
    kk = kk / jnp.maximum(jnp.sqrt(_dot_select(kk * kk, bd)), 1e-12)
    k2 = k * (1.0 + (a - 1.0) * ka_ref[...])
    for o_ref, val in ((r_o, r), (d_o, decay), (k_o, k2), (v_o, v), (n_o, -kk), (b_o, kk * a)):
        o_ref[:, 0:W_A] = val
        if o_ref.shape[1] > W_A:
            o_ref[:, W_A:] = jnp.zeros((TM, o_ref.shape[1] - W_A), F32)
    g_o[...] = g
    bonus_o[...] = _dot_select(r * k2 * rk_ref[...], bd) * v


def _rwkv_prep(ca, prev, p, tiles_per_seq):
    n = ca.shape[0]
    vec = _full_spec((1, W_A))
    lora_spec = _full_spec((DECAY_LORA + AAA_LORA, W_A))
    p_spec = _before_spec(COLS_A) if tiles_per_seq else _row_spec(COLS_A)
    w_scan = H_PAD * HEAD_DIM if tiles_per_seq else W_A
    widths = [w_scan] * 6 + [W_A] * 2
    return pl.pallas_call(
        functools.partial(_rwkv_prep_kernel, tiles_per_seq=tiles_per_seq),
        grid=(n // TM,),
        in_specs=[_row_spec(COLS_A), p_spec, _full_spec((1, COLS_A)), vec, vec, vec, vec, vec,
                  lora_spec, lora_spec, _full_spec((GATE_LORA, W_A)), _full_spec((W_A, W_A))],
        out_specs=[_row_spec(w) for w in widths],
        out_shape=[jax.ShapeDtypeStruct((n, w), F32) for w in widths],
        compiler_params=_params(("parallel",)),
        name="rwkv_prep",
    )(ca, ca if tiles_per_seq else prev, p["mu"], p["w0"], p["a0"], p["k_k"], p["k_a"], p["r_k"],
      p["w_up"], p["a_up"], p["g_up"], p["bd_a"])


def _rope_kernel(ang_ref, cos_o, sin_o):
    ang = ang_ref[...]
    lane = lax.broadcasted_iota(jnp.int32, ang.shape, 1)
    cos_o[...] = jnp.cos(ang)
    sin_o[...] = jnp.where((lane % HEAD_DIM) < (HEAD_DIM // 2), -jnp.sin(ang), jnp.sin(ang))


def _rope_tables(t_prompt, t_sample):
    theta = 1.0 / (ROPE_BASE ** jnp.linspace(0.0, 1.0, HEAD_DIM // 2, dtype=F32))
    pos = jnp.concatenate([jnp.arange(t_prompt, dtype=F32), PAST_LEN + (jnp.arange(TM) % t_sample).astype(F32)])
    ang = jnp.tile(pos[:, None] * theta[None, :], (1, W_B // (HEAD_DIM // 2)))
    n = ang.shape[0]
    return pl.pallas_call(
        _rope_kernel,
        grid=(n // TM,),
        in_specs=[_row_spec(W_B)],
        out_specs=[_row_spec(W_B)] * 2,
        out_shape=[jax.ShapeDtypeStruct((n, W_B), F32)] * 2,
        compiler_params=_params(("parallel",)),
        name="rope_tables",
    )(ang)


def _ret_prep_kernel(q_ref, k_ref, cos_ref, sin_ref, q_o, k_o):
    cos = cos_ref[...]
    sin = sin_ref[...]
    lane = lax.broadcasted_iota(jnp.int32, (TM, W_B), 1)
    first_half = (lane % HEAD_DIM) < (HEAD_DIM // 2)

    def rope(x):
        partner = jnp.where(first_half, pltpu.roll(x, W_B - HEAD_DIM // 2, 1), pltpu.roll(x, HEAD_DIM // 2, 1))
        return x * cos + partner * sin

    q_o[...] = rope(q_ref[...])
    k_o[...] = rope(k_ref[...]) * (HEAD_DIM ** -0.5)


def _ret_prep(cb, cos, sin, table_block):
    n = cb.shape[0]
    tab = pl.BlockSpec((TM, W_B), lambda i: (table_block(i), 0))
    return pl.pallas_call(
        _ret_prep_kernel,
        grid=(n // TM,),
        in_specs=[_row_spec(W_B, 0), _row_spec(W_B, 1), tab, tab],
        out_specs=[_row_spec(W_B)] * 2,
        out_shape=[jax.ShapeDtypeStruct((n, W_B), F32)] * 2,
        compiler_params=_params(("parallel",)),
        name="ret_prep",
    )(cb, cb, cos, sin)


def _ssm_prep_kernel(*refs, tiles_per_seq):
    n_shift_refs = 1 if tiles_per_seq else CONV_W - 1
    cc_ref = refs[0]
    shift_refs = refs[1:1 + n_shift_refs]
    cw_ref, cb_ref, dtb_ref, alog_ref, ex_ref, xdt_o, b_o, c_o, dec_o, xs_o, da_o = refs[1 + n_shift_refs:]
    cc = cc_ref[...]
    x0 = cc[:, W_C:W_C + CONV_DIM]
    if tiles_per_seq:
        before = shift_refs[0][:, W_C:W_C + CONV_DIM]
        first = pl.program_id(0) % tiles_per_seq == 0
        xs_prev = [_shift_rows(x0, before, j, first) for j in range(1, CONV_W)]
    else:
        xs_prev = [r[...] for r in shift_refs]
    cw = cw_ref[...]
    conv = x0 * cw[CONV_W - 1:CONV_W, :] + cb_ref[...]
    for j in range(1, CONV_W):
        conv = conv + xs_prev[j - 1] * cw[CONV_W - 1 - j:CONV_W - j, :]
    act = conv * _sigmoid(conv)
    xs = act[:, 0:W_C]
    dt = _softplus(cc[:, W_C + CONV_DIM:] + dtb_ref[...])
    da = dt * (-jnp.exp(alog_ref[...]))
    da_o[...] = da
    dec_o[...] = jnp.exp(da)
    xdt_o[...] = xs * _dot_select(dt, ex_ref[...])
    b_o[...] = act[:, W_C:W_C + N_GROUPS * N_STATE]
    c_o[...] = act[:, W_C + N_GROUPS * N_STATE:]
    xs_o[...] = xs


def _ssm_prep(cc, shifted, p, tiles_per_seq):
    n = cc.shape[0]
    gn = N_GROUPS * N_STATE
    if tiles_per_seq:
        shift_args, shift_specs = [cc], [_before_spec(COLS_C_PAD)]
    else:
        shift_args, shift_specs = list(shifted), [_row_spec(CONV_DIM)] * (CONV_W - 1)
    return pl.pallas_call(
        functools.partial(_ssm_prep_kernel, tiles_per_seq=tiles_per_seq),
        grid=(n // TM,),
        in_specs=[_row_spec(COLS_C_PAD)] + shift_specs + [
            _full_spec((CONV_W, CONV_DIM)), _full_spec((1, CONV_DIM)), _full_spec((1, LANES)),
            _full_spec((1, LANES)), _full_spec((LANES, W_C))],
        out_specs=[_row_spec(W_C), _row_spec(gn), _row_spec(gn), _row_spec(LANES), _row_spec(W_C),
                   _row_spec(LANES)],
        out_shape=[jax.ShapeDtypeStruct((n, W_C), F32), jax.ShapeDtypeStruct((n, gn), F32),
                   jax.ShapeDtypeStruct((n, gn), F32), jax.ShapeDtypeStruct((n, LANES), F32),
                   jax.ShapeDtypeStruct((n, W_C), F32), jax.ShapeDtypeStruct((n, LANES), F32)],
        compiler_params=_params(("parallel",)),
        name="ssm_prep",
    )(cc, *shift_args, p["conv_w"], p["conv_b"], p["dt_bias"], p["a_log"], p["expand"])


def _scan_kernel(*refs, mode, ni, tt_len, n_tt):
    if mode == "rwkv":
        q_ref, d_ref, a_ref, b_ref, n_ref, a2_ref, s0_ref, y_ref, so_ref, st = refs
    else:
        q_ref, d_ref, a_ref, b_ref, s0_ref, y_ref, so_ref, st = refs
    head = pl.program_id(0)
    tt = pl.program_id(1)

    @pl.when(tt == 0)
    def _():
        st[...] = s0_ref[...]

    def row(ref, t, i):
        return ref[t, pl.ds(i, 1), :]

    def step(t, carry):
        bv = b_ref[t]
        if mode == "rwkv":
            sa_parts = [jnp.zeros_like(bv), jnp.zeros_like(bv)]
            for i in range(ni):
                sa_parts[i % 2] = sa_parts[i % 2] + st[i] * row(n_ref, t, i)
            sa = sa_parts[0] + sa_parts[1]
        elif mode == "ssd":
            d = d_ref[t, pl.ds(head, 1), :]
        else:
            d = d_ref[pl.ds(head, 1), :]
        y_parts = [jnp.zeros_like(bv), jnp.zeros_like(bv)]
        for i in range(ni):
            if mode == "rwkv":
                s = st[i] * row(d_ref, t, i) + row(a_ref, t, i) * bv + row(a2_ref, t, i) * sa
            else:
                s = st[i] * d + row(a_ref, t, i) * bv
            st[i] = s
            y_parts[i % 2] = y_parts[i % 2] + s * row(q_ref, t, i)
        y_ref[t] = y_parts[0] + y_parts[1]
        return carry

    lax.fori_loop(0, tt_len, step, 0)

    @pl.when(tt == n_tt - 1)
    def _():
        so_ref[...] = st[...]


def _scan(mode, n_heads, ni, nj, q, d, a, b, s0, n=None, a2=None):
    t_len = b[0].shape[0]

    def rows(op, width):
        arr, block_of = op
        return arr, pl.BlockSpec((t_len, width, LANES), lambda h, t, f=block_of: (0, f(h), 0))

    ops = [rows(q, ni)]
    if mode == "rwkv":
        ops.append(rows(d, ni))
    elif mode == "ssd":
        ops.append((d, pl.BlockSpec((t_len, SUBLANES, LANES), lambda h, t: (0, 0, 0))))
    else:
        ops.append((d, pl.BlockSpec((SUBLANES, LANES), lambda h, t: (0, 0))))
    ops += [rows(a, ni), rows(b, nj)]
    if mode == "rwkv":
        ops += [rows(n, ni), rows(a2, ni)]
    s_spec = pl.BlockSpec((ni, nj, LANES), lambda h, t: (h, 0, 0))
    y_spec = pl.BlockSpec((t_len, nj, LANES), lambda h, t: (0, h, 0))
    return pl.pallas_call(
        functools.partial(_scan_kernel, mode=mode, ni=ni, tt_len=t_len, n_tt=1),
        grid=(n_heads, 1),
        in_specs=[spec for _, spec in ops] + [s_spec],
        out_specs=[y_spec, s_spec],
        out_shape=[jax.ShapeDtypeStruct((t_len, n_heads * nj, LANES), F32),
                   jax.ShapeDtypeStruct((n_heads * ni, nj, LANES), F32)],
        scratch_shapes=[pltpu.VMEM((ni, nj, LANES), F32)],
        compiler_params=_params(("parallel", "arbitrary")),
        name="scan_" + mode,
    )(*[arr for arr, _ in ops], s0)


def _rwkv_long_kernel(q_ref, d_ref, a_ref, b_ref, n1_ref, a2_ref, y_ref, so_ref, st, sa_s, *, ni, tt_len, n_tt):
    tt = pl.program_id(1)

    @pl.when(tt == 0)
    def _():
        st[...] = jnp.zeros_like(st)
        sa_s[...] = jnp.zeros_like(sa_s)

    def row(ref, t, i):
        return ref[t, pl.ds(i, 1), :]

    def fold(x):
        return x + pltpu.roll(x, LANES // 2, 1)

    def step(t, sa):
        bv = b_ref[t]
        a, a2, q, n1 = a_ref[t], a2_ref[t], q_ref[t], n1_ref[t]
        aq, a2q, an, a2n = [fold(jnp.sum(u * w, axis=0, keepdims=True))
                            for u, w in ((a, q), (a2, q), (a, n1), (a2, n1))]
        y0 = [jnp.zeros_like(bv), jnp.zeros_like(bv)]
        n0 = [jnp.zeros_like(bv), jnp.zeros_like(bv)]
        for i in range(ni):
            sd = st[i] * row(d_ref, t, i)
            st[i] = sd
            y0[i % 2] = y0[i % 2] + sd * row(q_ref, t, i)
            n0[i % 2] = n0[i % 2] + sd * row(n1_ref, t, i)
        y_ref[t] = fold(y0[0] + y0[1]) + bv * aq + sa * a2q
        sa_next = fold(n0[0] + n0[1]) + bv * an + sa * a2n
        for i in range(ni):
            st[i] = st[i] + row(a_ref, t, i) * bv + row(a2_ref, t, i) * sa
        return sa_next

    sa_s[...] = lax.fori_loop(0, tt_len, step, sa_s[...])

    @pl.when(tt == n_tt - 1)
    def _():
        so_ref[...] = st[...]


def _rwkv_long_scan(q, d, a, b, n1, a2, tt_len=64):
    t_len, ni, _ = q.shape
    nj = b.shape[1]
    n_tt = t_len // tt_len
    i_spec = pl.BlockSpec((tt_len, ni, LANES), lambda g, t: (t, 0, 0))
    j_spec = pl.BlockSpec((tt_len, nj, LANES), lambda g, t: (t, 0, 0))
    s_spec = pl.BlockSpec((ni, nj, LANES), lambda g, t: (0, 0, 0))
    return pl.pallas_call(
        functools.partial(_rwkv_long_kernel, ni=ni, tt_len=tt_len, n_tt=n_tt),
        grid=(1, n_tt),
        in_specs=[i_spec, i_spec, i_spec, j_spec, i_spec, i_spec],
        out_specs=[j_spec, s_spec],
        out_shape=[jax.ShapeDtypeStruct((t_len, nj, LANES), F32), jax.ShapeDtypeStruct((ni, nj, LANES), F32)],
        scratch_shapes=[pltpu.VMEM((ni, nj, LANES), F32), pltpu.VMEM((nj, LANES), F32)],
        compiler_params=_params(("parallel", "arbitrary")),
        name="scan_rwkv_long",
    )(q, d, a, b, n1, a2)


TR = 128
HALF = LANES // K_SPLIT


def _to_lanes_kernel(*refs, n_ops, rows, per_half):
    x_refs, o_refs, y2d = refs[:n_ops], refs[n_ops:2 * n_ops], refs[2 * n_ops]
    n_b = x_refs[0].shape[0]
    for x_ref, o_ref in zip(x_refs, o_refs):
        for b in range(n_b):
            xt = x_ref[b].T
            for h in range(H_PAD):
                for s in range(K_SPLIT):
                    lane = s * HALF + b * H_PAD + h
                    f0 = h * HEAD_DIM + (s * rows if per_half else 0)
                    y2d[pl.ds(lane, rows, stride=LANES), :] = xt[f0:f0 + rows, :]
        for r in range(rows):
            o_ref[pl.ds(r, TR, stride=rows), :] = y2d[r * LANES:(r + 1) * LANES, :].T


def _long_to_lanes(xs, b, rows, per_half):
    n_ops = len(xs)
    t = xs[0].shape[0] // b
    width = H_PAD * HEAD_DIM
    outs = pl.pallas_call(
        functools.partial(_to_lanes_kernel, n_ops=n_ops, rows=rows, per_half=per_half),
        grid=(t // TR,),
        in_specs=[pl.BlockSpec((b, TR, width), lambda i: (0, i, 0))] * n_ops,
        out_specs=[pl.BlockSpec((TR * rows, LANES), lambda i: (i, 0))] * n_ops,
        out_shape=[jax.ShapeDtypeStruct((t * rows, LANES), F32)] * n_ops,
        scratch_shapes=[pltpu.VMEM((rows * LANES, TR), F32)],
        compiler_params=_params(("parallel",)),
        name="to_lanes",
    )(*[x.reshape(b, t, width) for x in xs])
    return [o.reshape(t, rows, LANES) for o in outs]


def _from_lanes_kernel(y_ref, o_ref, z2d):
    for v in range(HEAD_DIM):
        z2d[pl.ds(v, LANES, stride=HEAD_DIM), :] = y_ref[pl.ds(v, TR, stride=HEAD_DIM), :].T
    for b in range(o_ref.shape[0]):
        r0 = b * H_PAD * HEAD_DIM
        o_ref[b] = z2d[r0:r0 + W_A, :].T


def _long_from_lanes(y, b):
    t = y.shape[0]
    out = pl.pallas_call(
        _from_lanes_kernel,
        grid=(t // TR,),
        in_specs=[pl.BlockSpec((TR * HEAD_DIM, LANES), lambda i: (i, 0))],
        out_specs=pl.BlockSpec((b, TR, W_A), lambda i: (0, i, 0)),
        out_shape=jax.ShapeDtypeStruct((b, t, W_A), F32),
        scratch_shapes=[pltpu.VMEM((LANES * HEAD_DIM, TR), F32)],
        compiler_params=_params(("parallel",)),
        name="from_lanes",
    )(y.reshape(t * HEAD_DIM, LANES))
    return out.reshape(b * t, W_A)


def _ret_chunk_kernel(q_ref, k_ref, v_ref, y_ref, s_ref, st):
    c = pl.program_id(1)

    @pl.when(c == 0)
    def _():
        st[...] = jnp.zeros_like(st)

    q = q_ref[...]
    k = k_ref[...]
    vb = v_ref[...].astype(BF16)
    kb = k.astype(BF16)
    row = lax.broadcasted_iota(jnp.int32, (LC, W_B), 0)
    head = lax.broadcasted_iota(jnp.int32, (LC, W_B), 1) // HEAD_DIM
    log_g = [math.log(1.0 - 2.0 ** (-5.0 - h)) for h in range(H_B)]
    lg = jnp.full((LC, W_B), log_g[0], F32)
    for h in range(1, H_B):
        lg = jnp.where(head == h, log_g[h], lg)
    rowf = row.astype(F32)
    diff = (lax.broadcasted_iota(jnp.int32, (LC, LC), 0) - lax.broadcasted_iota(jnp.int32, (LC, LC), 1))
    causal = diff >= 0
    difff = jnp.maximum(diff, 0).astype(F32)

    out = jnp.dot((q * jnp.exp(lg * (rowf + 1.0))).astype(BF16), st[...].astype(BF16), preferred_element_type=F32)
    for h in range(H_B):
        qm = jnp.where(head == h, q, 0.0).astype(BF16)
        s = lax.dot_general(qm, kb, (((1,), (1,)), ((), ())), preferred_element_type=F32)
        p = jnp.where(causal, s * jnp.exp(log_g[h] * difff), 0.0).astype(BF16)
        out = out + jnp.where(head == h, jnp.dot(p, vb, preferred_element_type=F32), 0.0)
    y_ref[...] = out

    kt = (k * jnp.exp(lg * (LC - 1.0 - rowf))).T.astype(BF16)
    kv = jnp.dot(kt, vb, preferred_element_type=F32)
    r2 = lax.broadcasted_iota(jnp.int32, (W_B, W_B), 0) // HEAD_DIM
    c2 = lax.broadcasted_iota(jnp.int32, (W_B, W_B), 1) // HEAD_DIM
    cdec = jnp.full((W_B, W_B), math.exp(log_g[0] * LC), F32)
    for h in range(1, H_B):
        cdec = jnp.where(r2 == h, math.exp(log_g[h] * LC), cdec)
    st[...] = st[...] * cdec + jnp.where(r2 == c2, kv, 0.0)
    s_ref[0] = st[...]


def _ret_chunk(q, k, cb, n_seq, t_len):
    n_c = t_len // LC
    rows = lambda col: pl.BlockSpec((LC, W_B), lambda b, c, col=col: (b * n_c + c, col))
    return pl.pallas_call(
        _ret_chunk_kernel,
        grid=(n_seq, n_c),
        in_specs=[rows(0), rows(0), rows(2)],
        out_specs=[rows(0), pl.BlockSpec((1, W_B, W_B), lambda b, c: (b, 0, 0))],
        out_shape=[jax.ShapeDtypeStruct((n_seq * t_len, W_B), F32), jax.ShapeDtypeStruct((n_seq, W_B, W_B), F32)],
        scratch_shapes=[pltpu.VMEM((W_B, W_B), F32)],
        compiler_params=_params(("parallel", "arbitrary")),
        name="ret_chunk",
    )(q, k, cb)


def _ssd_chunk_kernel(x_ref, b_ref, c_ref, da_ref, ex_ref, y_ref, h_ref, st):
    ci = pl.program_id(1)

    @pl.when(ci == 0)
    def _():
        st[...] = jnp.zeros_like(st)

    gn = N_STATE
    half = W_C // N_GROUPS
    ex = ex_ref[...]
    x = x_ref[...]
    xb = x.astype(BF16)
    ri = lax.broadcasted_iota(jnp.int32, (LC, LC), 0)
    cj = lax.broadcasted_iota(jnp.int32, (LC, LC), 1)
    causal = ri >= cj
    cum = _dot_hi(causal.astype(F32), da_ref[...])
    cum_e = _dot_select(cum, ex, 3)
    last_e = cum_e[LC - 1:LC, :]
    cum_t = cum.T
    head = lax.broadcasted_iota(jnp.int32, (LC, W_C), 1) // HEAD_DIM
    bg = [b_ref[:, g * gn:(g + 1) * gn].astype(BF16) for g in range(N_GROUPS)]
    cg = [c_ref[:, g * gn:(g + 1) * gn].astype(BF16) for g in range(N_GROUPS)]
    cb = [lax.dot_general(cg[g], bg[g], (((1,), (1,)), ((), ())), preferred_element_type=F32)
          for g in range(N_GROUPS)]

    hb = st[...].astype(BF16)
    ys = [lax.dot_general(cg[g], hb, (((1,), (1,)), ((), ())), preferred_element_type=F32)
          for g in range(N_GROUPS)]
    y = jnp.where(head < H_C // N_GROUPS, ys[0], ys[1]) * jnp.exp(cum_e)
    for h in range(H_C):
        seg = cum[:, h:h + 1] - cum_t[h:h + 1, :]
        p = (cb[h // (H_C // N_GROUPS)] * jnp.exp(jnp.where(causal, seg, -jnp.inf))).astype(BF16)
        y = y + jnp.where(head == h, jnp.dot(p, xb, preferred_element_type=F32), 0.0)
    y_ref[...] = y

    xt = (x * jnp.exp(last_e - cum_e)).T.astype(BF16)
    upd = [jnp.dot(xt, bg[g], preferred_element_type=F32) for g in range(N_GROUPS)]
    rowi = lax.broadcasted_iota(jnp.int32, (W_C, gn), 0)
    sel = lax.broadcasted_iota(jnp.int32, (W_C, LANES), 1) == lax.broadcasted_iota(jnp.int32, (W_C, LANES), 0) // HEAD_DIM
    tot = jnp.sum(jnp.where(sel, cum[LC - 1:LC, :], 0.0), axis=1, keepdims=True)
    st[...] = st[...] * jnp.exp(tot) + jnp.where(rowi < half, upd[0], upd[1])
    h_ref[0] = st[...]


def _ssd_chunk(xdt, bm, cm, da, ex, n_seq, t_len):
    n_c = t_len // LC
    gn = N_GROUPS * N_STATE
    rows = lambda w: pl.BlockSpec((LC, w), lambda b, c: (b * n_c + c, 0))
    return pl.pallas_call(
        _ssd_chunk_kernel,
        grid=(n_seq, n_c),
        in_specs=[rows(W_C), rows(gn), rows(gn), rows(LANES), pl.BlockSpec((LANES, W_C), lambda b, c: (0, 0))],
        out_specs=[rows(W_C), pl.BlockSpec((1, W_C, N_STATE), lambda b, c: (b, 0, 0))],
        out_shape=[jax.ShapeDtypeStruct((n_seq * t_len, W_C), F32),
                   jax.ShapeDtypeStruct((n_seq, W_C, N_STATE), F32)],
        scratch_shapes=[pltpu.VMEM((W_C, N_STATE), F32)],
        compiler_params=_params(("parallel", "arbitrary")),
        name="ssd_chunk",
    )(xdt, bm, cm, da, ex)


def _post_kernel(x_ref, ya_ref, bonus_ref, ga_ref, ob_ref, gb_ref, yc_ref, xs_ref, z_ref, lnw_ref, lnb_ref,
                 rn_ref, dsk_ref, sn_ref, bda_ref, bdb_ref, wo_ref, out_ref):
    inv_hd = 1.0 / HEAD_DIM
    y = ya_ref[...]
    bda = bda_ref[...]
    mean = _dot_select(y, bda) * inv_hd
    yd = y - mean
    var = _dot_select(yd * yd, bda) * inv_hd
    ya = (yd * lax.rsqrt(var + GN_EPS) * lnw_ref[...] + lnb_ref[...] + bonus_ref[...]) * ga_ref[...]
    o = ob_ref[...]
    ms = _dot_select(o * o, bdb_ref[...]) * inv_hd
    gb = gb_ref[...]
    yb = (gb * _sigmoid(gb)) * (o * lax.rsqrt(ms + RMS_EPS) * rn_ref[...])
    z = z_ref[...]
    yc = (yc_ref[...] + dsk_ref[...] * xs_ref[...]) * (z * _sigmoid(z))
    yc = _rmsnorm(yc, sn_ref[...], GATED_NORM_EPS)
    out_ref[...] = (x_ref[...] + _dot(ya, wo_ref[0:W_A, :]) + _dot(yb, wo_ref[W_A:W_A + W_B, :])
                    + _dot(yc, wo_ref[W_A + W_B:, :]))


def _post(x, ya, bonus, ga, ob, cb, yc, xs, cc, p):
    n = x.shape[0]
    va = _full_spec((1, W_A))
    return pl.pallas_call(
        _post_kernel,
        grid=(n // TM,),
        in_specs=[_row_spec(D_MODEL), _row_spec(W_A), _row_spec(W_A), _row_spec(W_A), _row_spec(W_B),
                  _row_spec(W_B, 3), _row_spec(W_C), _row_spec(W_C), _row_spec(W_C, 0), va, va,
                  _full_spec((1, W_B)), va, va, _full_spec((W_A, W_A)), _full_spec((W_B, W_B)),
                  _full_spec((D_MODEL, D_MODEL))],
        out_specs=_row_spec(D_MODEL),
        out_shape=jax.ShapeDtypeStruct((n, D_MODEL), F32),
        compiler_params=_params(("parallel",)),
        name="post_outproj",
    )(x, ya, bonus, ga, ob, cb, yc, xs, cc, p["ln_w"], p["ln_b"], p["ret_norm"], p["d_skip"], p["ssm_norm"],
      p["bd_a"], p["bd_b"], p["w_out"])


def _ffn_kernel(*refs, gated, final_norm, n_e):
    refs = list(refs)
    x_ref, nf_ref = refs[:2]
    pos = 2
    if gated:
        router_ref = refs[pos]
        pos += 1
    wg_ref, wu_ref, wd_ref = refs[pos:pos + 3]
    pos += 3
    if final_norm:
        nfin_ref = refs[pos]
        pos += 1
    out_ref = refs[pos]
    hb_s, acc_s = refs[pos + 1:pos + 3]
    if gated:
        gate_s, asg_s, pos_s, asg_t, pos_t = refs[pos + 3:pos + 8]
    e = pl.program_id(1)

    @pl.when(e == 0)
    def _():
        h = _rmsnorm(x_ref[...], nf_ref[...], RMS_EPS)
        hb_s[...] = h.astype(BF16)
        acc_s[...] = jnp.zeros_like(acc_s)
        if gated:
            lane = lax.broadcasted_iota(jnp.int32, (TM, LANES), 1)
            logits = jnp.where(lane < N_EXPERTS, _dot_hi(h, router_ref[...]), -jnp.inf)
            p = jnp.exp(logits - jnp.max(logits, axis=-1, keepdims=True))
            p = p / jnp.sum(p, axis=-1, keepdims=True)
            p1 = jnp.max(p, axis=-1, keepdims=True)
            i1 = jnp.min(jnp.where(p == p1, lane, LANES), axis=-1, keepdims=True)
            rest = jnp.where(lane == i1, -1.0, p)
            p2 = jnp.max(rest, axis=-1, keepdims=True)
            i2 = jnp.min(jnp.where(rest == p2, lane, LANES), axis=-1, keepdims=True)
            gate_s[...] = jnp.where(lane == i1, p1, jnp.where(lane == i2, p2, 0.0)) / (p1 + p2)
            assigned = jnp.where((lane == i1) | (lane == i2), 1.0, 0.0)
            earlier = (lax.broadcasted_iota(jnp.int32, (TM, TM), 0) > lax.broadcasted_iota(jnp.int32, (TM, TM), 1))
            rank = jnp.dot(earlier.astype(BF16), assigned.astype(BF16), preferred_element_type=F32)
            asg_s[...] = assigned
            pos_s[...] = rank
            asg_t[...] = assigned.T
            pos_t[...] = rank.T

    if not gated:
        hb = hb_s[...]
        g = jnp.dot(hb, wg_ref[0], preferred_element_type=F32)
        u = jnp.dot(hb, wu_ref[0], preferred_element_type=F32)
        acc_s[...] += _dot(g * _sigmoid(g) * u, wd_ref[0])
    else:
        lane = lax.broadcasted_iota(jnp.int32, (TM, LANES), 1)
        mine = lane == e
        col = lambda ref: jnp.sum(jnp.where(mine, ref[...], 0.0), axis=-1, keepdims=True)
        gate_c, asg_c, pos_c = col(gate_s), col(asg_s), col(pos_s)
        asg_r, pos_r = asg_t[pl.ds(e, 1), :], pos_t[pl.ds(e, 1), :]
        n_chunks = (jnp.sum(asg_r).astype(jnp.int32) + MOE_CHUNK - 1) // MOE_CHUNK

        def chunk(c, carry):
            base = (c * MOE_CHUNK).astype(F32)
            slot_r = lax.broadcasted_iota(jnp.int32, (MOE_CHUNK, TM), 0).astype(F32) + base
            pick = jnp.where((pos_r == slot_r) & (asg_r > 0.0), 1.0, 0.0).astype(BF16)
            rows = jnp.dot(pick, hb_s[...], preferred_element_type=F32).astype(BF16)
            g = jnp.dot(rows, wg_ref[0], preferred_element_type=F32)
            u = jnp.dot(rows, wu_ref[0], preferred_element_type=F32)
            o = _dot(g * _sigmoid(g) * u, wd_ref[0])
            o_hi = o.astype(BF16)
            o_lo = (o - o_hi.astype(F32)).astype(BF16)
            slot_c = lax.broadcasted_iota(jnp.int32, (TM, MOE_CHUNK), 1).astype(F32) + base
            place = jnp.where((pos_c == slot_c) & (asg_c > 0.0), 1.0, 0.0).astype(BF16)
            back = (jnp.dot(place, o_hi, preferred_element_type=F32)
                    + jnp.dot(place, o_lo, preferred_element_type=F32))
            acc_s[...] += gate_c * back
            return carry

        lax.fori_loop(0, n_chunks, chunk, 0)

    @pl.when(e == n_e - 1)
    def _():
        y = x_ref[...] + acc_s[...]
        if final_norm:
            y = _rmsnorm(y, nfin_ref[...], RMS_EPS)
        out_ref[...] = y


def _ffn(x, nf, wg, wu, wd, router=None, nfin=None):
    n = x.shape[0]
    n_e, _, dff = wg.shape
    gated = router is not None
    final_norm = nfin is not None
    vec = pl.BlockSpec((1, D_MODEL), lambda i, e: (0, 0))
    args = [x, nf]
    specs = [pl.BlockSpec((TM, D_MODEL), lambda i, e: (i, 0)), vec]
    if gated:
        args.append(router)
        specs.append(pl.BlockSpec((D_MODEL, LANES), lambda i, e: (0, 0)))
    args += [wg, wu, wd]
    specs += [pl.BlockSpec((1, D_MODEL, dff), lambda i, e: (e, 0, 0)),
              pl.BlockSpec((1, D_MODEL, dff), lambda i, e: (e, 0, 0)),
              pl.BlockSpec((1, dff, D_MODEL), lambda i, e: (e, 0, 0))]
    if final_norm:
        args.append(nfin)
        specs.append(vec)
    scratch = [pltpu.VMEM((TM, D_MODEL), BF16), pltpu.VMEM((TM, D_MODEL), F32)]
    if gated:
        scratch += [pltpu.VMEM((TM, LANES), F32)] * 3 + [pltpu.VMEM((LANES, TM), F32)] * 2
    return pl.pallas_call(
        functools.partial(_ffn_kernel, gated=gated, final_norm=final_norm, n_e=n_e),
        grid=(n // TM, n_e),
        in_specs=specs,
        out_specs=pl.BlockSpec((TM, D_MODEL), lambda i, e: (i, 0)),
        out_shape=jax.ShapeDtypeStruct((n, D_MODEL), F32),
        scratch_shapes=scratch,
        compiler_params=_params(("parallel", "arbitrary")),
        name="moe" if gated else "ffn",
    )(*args)


def _short_to_lanes_kernel(*refs, n_ops, t_len):
    for x_ref, o_ref in zip(refs[:n_ops], refs[n_ops:]):
        for t in range(t_len):
            o_ref[t] = x_ref[pl.ds(t, LANES, stride=t_len), :].T


def _short_to_lanes(xs, t_len):
    n_chunks = [width // LANES for _, _, width in xs]
    grid = max(n_chunks)
    clamp = lambda c, n: jnp.minimum(c, n - 1)
    outs = pl.pallas_call(
        functools.partial(_short_to_lanes_kernel, n_ops=len(xs), t_len=t_len),
        grid=(grid,),
        in_specs=[pl.BlockSpec((LANES * t_len, LANES), lambda c, c0=col // LANES, n=n: (0, c0 + clamp(c, n)))
                  for (_, col, _), n in zip(xs, n_chunks)],
        out_specs=[pl.BlockSpec((t_len, LANES, LANES), lambda c, n=n: (0, clamp(c, n), 0)) for n in n_chunks],
        out_shape=[jax.ShapeDtypeStruct((t_len, width, LANES), F32) for _, _, width in xs],
        compiler_params=_params(("arbitrary",)),
        name="short_to_lanes",
    )(*[x for x, _, _ in xs])
    return outs


def _short_from_lanes_kernel(y_ref, o_ref, *, t_len):
    for t in range(t_len):
        o_ref[pl.ds(t, LANES, stride=t_len), :] = y_ref[t].T


def _short_from_lanes(y):
    t_len, w, _ = y.shape
    return pl.pallas_call(
        functools.partial(_short_from_lanes_kernel, t_len=t_len),
        grid=(w // LANES,),
        in_specs=[pl.BlockSpec((t_len, LANES, LANES), lambda c: (0, c, 0))],
        out_specs=pl.BlockSpec((LANES * t_len, LANES), lambda c: (0, c)),
        out_shape=jax.ShapeDtypeStruct((LANES * t_len, w), F32),
        compiler_params=_params(("parallel",)),
        name="short_from_lanes",
    )(y)


def _state_to_lanes_kernel(x_ref, o_ref, *, ni, nj, j_first):
    xt = x_ref[...].T
    if j_first:
        for j in range(nj):
            o_ref[pl.ds(j, ni, stride=nj), :] = xt[j * ni:(j + 1) * ni, :]
    else:
        o_ref[...] = xt


def _state_to_lanes(s, ni, nj, j_first):
    h = s.shape[1]
    out = pl.pallas_call(
        functools.partial(_state_to_lanes_kernel, ni=ni, nj=nj, j_first=j_first),
        grid=(h,),
        in_specs=[pl.BlockSpec((LANES, ni * nj), lambda i: (0, i))],
        out_specs=pl.BlockSpec((ni * nj, LANES), lambda i: (i, 0)),
        out_shape=jax.ShapeDtypeStruct((h * ni * nj, LANES), F32),
        compiler_params=_params(("parallel",)),
        name="state_to_lanes",
    )(s.reshape(LANES, h * ni * nj))
    return out.reshape(h * ni, nj, LANES)


def _state_from_lanes_kernel(s_ref, o_ref, tmp, *, ni, nj, j_first):
    if j_first:
        for j in range(nj):
            tmp[j * ni:(j + 1) * ni, :] = s_ref[pl.ds(j, ni, stride=nj), :]
        o_ref[...] = tmp[...].T
    else:
        o_ref[...] = s_ref[...].T


def _state_from_lanes(s, n_heads, ni, nj, j_first):
    out = pl.pallas_call(
        functools.partial(_state_from_lanes_kernel, ni=ni, nj=nj, j_first=j_first),
        grid=(n_heads,),
        in_specs=[pl.BlockSpec((ni * nj, LANES), lambda i: (i, 0))],
        out_specs=pl.BlockSpec((LANES, ni * nj), lambda i: (0, i)),
        out_shape=jax.ShapeDtypeStruct((LANES, n_heads * ni * nj), F32),
        scratch_shapes=[pltpu.VMEM((ni * nj, LANES), F32)],
        compiler_params=_params(("parallel",)),
        name="state_from_lanes",
    )(s.reshape(n_heads * ni * nj, LANES))
    return out.reshape((LANES, n_heads, nj, ni) if j_first else (LANES, n_heads, ni, nj))


def _rwkv_long(feats, b):
    r, dec, k2, v, nkk, beta = feats
    t = r.shape[0] // b
    kl = HEAD_DIM // K_SPLIT
    assert b * H_PAD == HALF and t % TR == 0
    nkk_next = jnp.concatenate([nkk[1:], jnp.zeros((1, nkk.shape[1]), F32)], axis=0)
    q, d, a, n1, a2 = _long_to_lanes([r, dec, k2, nkk_next, beta], b, kl, True)
    (vv,) = _long_to_lanes([v], b, HEAD_DIM, False)
    y, s = _rwkv_long_scan(q, d, a, vv, n1, a2)
    s = s.reshape(kl, HEAD_DIM, K_SPLIT, b, H_PAD)[..., :H_A]
    return _long_from_lanes(y, b), jnp.transpose(s, (3, 4, 1, 2, 0)).reshape(b, H_A, HEAD_DIM, HEAD_DIM)


def _rwkv_short(feats, t_len, s0):
    r, dec, k2, v, nkk, beta = _short_to_lanes([(f, 0, W_A) for f in feats], t_len)
    hd = HEAD_DIM
    own = lambda h: h
    y, s = _scan("rwkv", H_A, hd, hd, (r, own), (dec, own), (k2, own), (v, own),
                 _state_to_lanes(s0, hd, hd, True), n=(nkk, own), a2=(beta, own))
    return _short_from_lanes(y), _state_from_lanes(s, H_A, hd, hd, True)


def _ret_short(qh, kh, cb, t_len, s0):
    q, k, v = _short_to_lanes([(qh, 0, W_B), (kh, 0, W_B), (cb, 2 * W_B, W_B)], t_len)
    hd = HEAD_DIM
    gam = np.ones((SUBLANES, LANES), np.float32)
    gam[:H_B] = (1.0 - 2.0 ** (-5.0 - np.arange(H_B, dtype=np.float64)))[:, None]
    own = lambda h: h
    y, s = _scan("ret", H_B, hd, hd, (q, own), jnp.asarray(gam), (k, own), (v, own),
                 _state_to_lanes(s0, hd, hd, False))
    return _short_from_lanes(y), _state_from_lanes(s, H_B, hd, hd, False)


def _ssd_short(xdt, bm, cm, dssm, t_len, s0):
    gn = N_GROUPS * N_STATE
    x, bl, cl, dl = _short_to_lanes([(xdt, 0, W_C), (bm, 0, gn), (cm, 0, gn), (dssm, 0, LANES)], t_len)
    group = lambda h: h // (H_C // N_GROUPS)
    y, s = _scan("ssd", H_C, N_STATE, HEAD_DIM, (cl, group), dl, (bl, group), (x, lambda h: h),
                 _state_to_lanes(s0, N_STATE, HEAD_DIM, True))
    return _short_from_lanes(y), _state_from_lanes(s, H_C, N_STATE, HEAD_DIM, True)


def _block_diag_ones(width):
    idx = np.arange(width) // HEAD_DIM
    return jnp.asarray((idx[:, None] == idx[None, :]).astype(np.float32), BF16)


def _shifted(x, prev_rows, shift):
    b, t, c = x.shape
    p = prev_rows.shape[1]
    full = jnp.concatenate([prev_rows, x], axis=1)
    return full[:, p - shift:p - shift + t].reshape(b * t, c)


def _layer(x, b, p, st, rope, table_block):
    n = x.shape[0]
    t = n // b
    fresh = st is None
    tiles_per_seq = t // TM if fresh else 0
    ca, cb, cc = _in_proj(x, p["norm_mix"], p["w_in"])
    ca3 = ca.reshape(b, t, COLS_A)
    xbc_tail = cc.reshape(b, t, COLS_C_PAD)[:, -(CONV_W - 1):, W_C:W_C + CONV_DIM]
    if fresh:
        prev, shifted = None, None
        conv_new = xbc_tail
    else:
        prev = _shifted(ca3, st["shift"][:, None, :], 1)
        xbc = cc[:, W_C:W_C + CONV_DIM].reshape(b, t, CONV_DIM)
        shifted = [_shifted(xbc, st["conv"], j) for j in range(1, CONV_W)]
        conv_new = jnp.concatenate([st["conv"], xbc_tail], axis=1)[:, -(CONV_W - 1):]

    r, dec, k2, v, nkk, beta, ga, bonus = _rwkv_prep(ca, prev, p, tiles_per_seq)
    qh, kh = _ret_prep(cb, rope[0], rope[1], table_block)
    xdt, bm, cm, dssm, xs, da = _ssm_prep(cc, shifted, p, tiles_per_seq)

    feats = (r, dec, k2, v, nkk, beta)
    if fresh:
        ya, s_rwkv = _rwkv_long(feats, b)
        yb, s_full = _ret_chunk(qh, kh, cb, b, t)
        s_ret = jnp.stack([s_full[:, h * HEAD_DIM:(h + 1) * HEAD_DIM, h * HEAD_DIM:(h + 1) * HEAD_DIM]
                           for h in range(H_B)], axis=1)
        yc, s_ssm = _ssd_chunk(xdt, bm, cm, da, p["expand"], b, t)
        s_ssm = s_ssm.reshape(b, H_C, HEAD_DIM, N_STATE)
    else:
        assert b == LANES
        ya, s_rwkv = _rwkv_short(feats, t, st["rwkv"])
        yb, s_ret = _ret_short(qh, kh, cb, t, st["ret"])
        yc, s_ssm = _ssd_short(xdt, bm, cm, dssm, t, st["ssm"])

    x2 = _post(x, ya, bonus, ga, yb, cb, yc, xs, cc, p)
    x = _ffn(x2, p["norm_ffn"], p["wg"], p["wu"], p["wd"], router=p["router"], nfin=p["norm_final"])
    return x, (s_rwkv, ca3[:, -1], s_ret, s_ssm, conv_new)


def kernel(x_prompt, x_sample, state_rwkv, state_shift, state_ret, state_ssm, state_conv, norm_mix, w_in, rwkv_mu, rwkv_w0, rwkv_w_up, rwkv_a0, rwkv_a_up, rwkv_g_up, rwkv_k_k, rwkv_k_a, rwkv_r_k, rwkv_ln_w, rwkv_ln_b, ret_norm, ssm_conv_w, ssm_conv_b, ssm_dt_bias, ssm_a_log, ssm_d, ssm_norm, w_out, norm_ffn, ffn_w_gate, ffn_w_up, ffn_w_down, moe_router, moe_w_gate, moe_w_up, moe_w_down, norm_final):
    bp, tp, _ = x_prompt.shape
    bs, ts, _ = x_sample.shape
    depth = w_in.shape[0]
    assert tp % TM == 0 and tp % LC == 0 and (bs * ts) % TM == 0 and TM % ts == 0

    rope = _rope_tables(tp, ts)
    bd_a, bd_b = _block_diag_ones(W_A), _block_diag_ones(W_B)
    expand = np.zeros((LANES, W_C), np.float32)
    expand[np.arange(W_C) // HEAD_DIM, np.arange(W_C)] = 1.0
    expand = jnp.asarray(expand, BF16)
    row = lambda v: v.reshape(1, -1)
    pad_l = lambda v: jnp.pad(v, (0, LANES - v.shape[0])).reshape(1, LANES)

    xp = x_prompt.reshape(bp * tp, D_MODEL)
    xs = x_sample.reshape(bs * ts, D_MODEL)
    new_p, new_s = [], []
    for i in range(depth):
        j = i // 2
        p = dict(
            norm_mix=row(norm_mix[i]),
            w_in=jnp.pad(w_in[i], ((0, 0), (0, COLS_C_PAD - COLS_C))).astype(BF16),
            mu=row(rwkv_mu[i]), w0=row(rwkv_w0[i]), a0=row(rwkv_a0[i]), k_k=row(rwkv_k_k[i]),
            k_a=row(rwkv_k_a[i]), r_k=row(rwkv_r_k[i]),
            w_up=jnp.pad(rwkv_w_up[i], ((0, AAA_LORA), (0, 0))).astype(BF16),
            a_up=jnp.pad(rwkv_a_up[i], ((DECAY_LORA, 0), (0, 0))).astype(BF16),
            g_up=rwkv_g_up[i].astype(BF16), bd_a=bd_a, bd_b=bd_b,
            conv_w=ssm_conv_w[i], conv_b=row(ssm_conv_b[i]), dt_bias=pad_l(ssm_dt_bias[i]),
            a_log=pad_l(ssm_a_log[i]), expand=expand,
            ln_w=row(rwkv_ln_w[i]), ln_b=row(rwkv_ln_b[i]), ret_norm=row(ret_norm[i]),
            d_skip=row(jnp.repeat(ssm_d[i], HEAD_DIM)), ssm_norm=row(ssm_norm[i]), w_out=w_out[i].astype(BF16),
            norm_ffn=row(norm_ffn[i]), norm_final=row(norm_final) if i == depth - 1 else None)
        if i % 2 == 0:
            dff = ffn_w_gate.shape[-1] // 2
            p.update(router=None,
                     wg=ffn_w_gate[j].reshape(D_MODEL, 2, dff).transpose(1, 0, 2).astype(BF16),
                     wu=ffn_w_up[j].reshape(D_MODEL, 2, dff).transpose(1, 0, 2).astype(BF16),
                     wd=ffn_w_down[j].reshape(2, dff, D_MODEL).astype(BF16))
        else:
            p.update(router=jnp.pad(moe_router[j], ((0, 0), (0, LANES - N_EXPERTS))),
                     wg=moe_w_gate[j].astype(BF16), wu=moe_w_up[j].astype(BF16), wd=moe_w_down[j].astype(BF16))

        xp, st_p = _layer(xp, bp, p, None, rope, lambda t: t % (tp // TM))
        st = dict(rwkv=state_rwkv[i], shift=state_shift[i], ret=state_ret[i], ssm=state_ssm[i],
                  conv=state_conv[i])
        xs, st_s = _layer(xs, bs, p, st, rope, lambda t: tp // TM)
        new_p.append(st_p)
        new_s.append(st_s)

    stack = lambda sts: tuple(jnp.stack(s) for s in zip(*sts))
    return (xp.reshape(bp, tp, D_MODEL), xs.reshape(bs, ts, D_MODEL)) + stack(new_p) + stack(new_s)
```

```python
import functools
import math

import numpy as np
import jax
import jax.numpy as jnp
from jax import lax
from jax.experimental import pallas as pl
from jax.experimental.pallas import tpu as pltpu

F32 = jnp.float32
BF16 = jnp.bfloat16
HIGHEST = lax.Precision.HIGHEST

LANES = 128
SUBLANES = 8
VMEM_LIMIT = 56 * 1024 * 1024

D_MODEL = 1024
HEAD_DIM = 64
H_A, H_B, H_C = 6, 4, 6
W_A, W_B, W_C = H_A * HEAD_DIM, H_B * HEAD_DIM, H_C * HEAD_DIM
DECAY_LORA, AAA_LORA, GATE_LORA = 64, 64, 128
COLS_A = 3 * W_A + DECAY_LORA + AAA_LORA + GATE_LORA
COLS_B = 4 * W_B
N_STATE, N_GROUPS, CONV_W = 128, 2, 4
CONV_DIM = W_C + 2 * N_GROUPS * N_STATE
COLS_C = W_C + CONV_DIM + H_C
COLS_C_PAD = 1408
ROPE_BASE = 10000.0
RMS_EPS = 1e-6
GN_EPS = 64e-5
GATED_NORM_EPS = 1e-5
N_EXPERTS = 8
PAST_LEN = 16384

TM = 512
MOE_CHUNK = 160
LC = 256
H_PAD = 8
K_SPLIT = 2


def _dot(a, b):
    return jnp.dot(a.astype(BF16), b.astype(BF16), preferred_element_type=F32)


def _dot_hi(a, b):
    return jnp.dot(a, b, precision=HIGHEST, preferred_element_type=F32)


def _dot_select(a, sel, terms=2):
    out = None
    rest = a
    for _ in range(terms):
        piece = rest.astype(BF16)
        part = jnp.dot(piece, sel, preferred_element_type=F32)
        out = part if out is None else out + part
        rest = rest - piece.astype(F32)
    return out


def _sigmoid(x):
    return 1.0 / (1.0 + jnp.exp(-x))


def _softplus(x):
    return jnp.maximum(x, 0.0) + jnp.log1p(jnp.exp(-jnp.abs(x)))


def _rmsnorm(x, g, eps):
    return x * lax.rsqrt(jnp.mean(x * x, axis=-1, keepdims=True) + eps) * g


def _params(sem):
    return pltpu.CompilerParams(dimension_semantics=sem, vmem_limit_bytes=VMEM_LIMIT)


def _row_spec(width, col=0):
    return pl.BlockSpec((TM, width), lambda i, c=col: (i, c))


def _before_spec(width):
    return pl.BlockSpec((SUBLANES, width), lambda i: (jnp.maximum(i * (TM // SUBLANES) - 1, 0), 0))


def _full_spec(shape):
    nd = len(shape)
    return pl.BlockSpec(shape, lambda i, n=nd: (0,) * n)


def _shift_rows(x, before, j, first):
    rolled = pltpu.roll(x, j, 0)
    prev = jnp.where(first, 0.0, pltpu.roll(before, j, 0))
    row = lax.broadcasted_iota(jnp.int32, prev.shape, 0)
    top = jnp.where(row < j, prev, rolled[0:SUBLANES])
    return jnp.concatenate([top, rolled[SUBLANES:]], axis=0)


def _in_proj_kernel(x_ref, g_ref, w_ref, oa_ref, ob_ref, oc_ref):
    h = _rmsnorm(x_ref[...], g_ref[...], RMS_EPS).astype(BF16)
    oa_ref[...] = jnp.dot(h, w_ref[:, 0:COLS_A], preferred_element_type=F32)
    ob_ref[...] = jnp.dot(h, w_ref[:, COLS_A:COLS_A + COLS_B], preferred_element_type=F32)
    oc_ref[...] = jnp.dot(h, w_ref[:, COLS_A + COLS_B:], preferred_element_type=F32)


def _in_proj(x, g, w):
    n = x.shape[0]
    wtot = w.shape[1]
    return pl.pallas_call(
        _in_proj_kernel,
        grid=(n // TM,),
        in_specs=[_row_spec(D_MODEL), _full_spec((1, D_MODEL)), _full_spec((D_MODEL, wtot))],
        out_specs=[_row_spec(COLS_A), _row_spec(COLS_B), _row_spec(COLS_C_PAD)],
        out_shape=[jax.ShapeDtypeStruct((n, COLS_A), F32), jax.ShapeDtypeStruct((n, COLS_B), F32),
                   jax.ShapeDtypeStruct((n, COLS_C_PAD), F32)],
        compiler_params=_params(("parallel",)),
        name="in_proj",
    )(x, g, w)


def _rwkv_prep_kernel(c_ref, p_ref, mu_ref, w0_ref, a0_ref, kk_ref, ka_ref, rk_ref, wup_ref, aup_ref,
                      gup_ref, bd_ref, r_o, d_o, k_o, v_o, n_o, b_o, g_o, bonus_o, *, tiles_per_seq):
    c = c_ref[...]
    if tiles_per_seq:
        prev = _shift_rows(c, p_ref[...], 1, pl.program_id(0) % tiles_per_seq == 0)
    else:
        prev = p_ref[...]
    xm = c + (prev - c) * mu_ref[...]
    r = xm[:, 0:W_A]
    k = xm[:, W_A:2 * W_A]
    v = xm[:, 2 * W_A:3 * W_A]
    lora = xm[:, 3 * W_A:3 * W_A + DECAY_LORA + AAA_LORA]
    gd = xm[:, 3 * W_A + DECAY_LORA + AAA_LORA:]
    w = w0_ref[...] + _dot(jnp.tanh(lora), wup_ref[...])
    decay = jnp.exp(-math.exp(-0.5) * _sigmoid(w))
    a = _sigmoid(a0_ref[...] + _dot(lora, aup_ref[...]))
    g = _dot(_sigmoid(gd), gup_ref[...])
    bd = bd_ref[...]
    kk = k * kk_ref[...]
    kk = kk / jnp.maximum(jnp.sqrt(_dot_select(kk * kk, bd)), 1e-12)
    k2 = k * (1.0 + (a - 1.0) * ka_ref[...])
    for o_ref, val in ((r_o, r), (d_o, decay), (k_o, k2), (v_o, v), (n_o, -kk), (b_o, kk * a)):
        o_ref[:, 0:W_A] = val
        if o_ref.shape[1] > W_A:
            o_ref[:, W_A:] = jnp.zeros((TM, o_ref.shape[1] - W_A), F32)
    g_o[...] = g
    bonus_o[...] = _dot_select(r * k2 * rk_ref[...], bd) * v


def _rwkv_prep(ca, prev, p, tiles_per_seq):
    n = ca.shape[0]
    vec = _full_spec((1, W_A))
    lora_spec = _full_spec((DECAY_LORA + AAA_LORA, W_A))
    p_spec = _before_spec(COLS_A) if tiles_per_seq else _row_spec(COLS_A)
    w_scan = H_PAD * HEAD_DIM if tiles_per_seq else W_A
    widths = [w_scan] * 6 + [W_A] * 2
    return pl.pallas_call(
        functools.partial(_rwkv_prep_kernel, tiles_per_seq=tiles_per_seq),
        grid=(n // TM,),
        in_specs=[_row_spec(COLS_A), p_spec, _full_spec((1, COLS_A)), vec, vec, vec, vec, vec,
                  lora_spec, lora_spec, _full_spec((GATE_LORA, W_A)), _full_spec((W_A, W_A))],
        out_specs=[_row_spec(w) for w in widths],
        out_shape=[jax.ShapeDtypeStruct((n, w), F32) for w in widths],
        compiler_params=_params(("parallel",)),
        name="rwkv_prep",
    )(ca, ca if tiles_per_seq else prev, p["mu"], p["w0"], p["a0"], p["k_k"], p["k_a"], p["r_k"],
      p["w_up"], p["a_up"], p["g_up"], p["bd_a"])


def _rope_kernel(ang_ref, cos_o, sin_o):
    ang = ang_ref[...]
    lane = lax.broadcasted_iota(jnp.int32, ang.shape, 1)
    cos_o[...] = jnp.cos(ang)
    sin_o[...] = jnp.where((lane % HEAD_DIM) < (HEAD_DIM // 2), -jnp.sin(ang), jnp.sin(ang))


def _rope_tables(t_prompt, t_sample):
    theta = 1.0 / (ROPE_BASE ** jnp.linspace(0.0, 1.0, HEAD_DIM // 2, dtype=F32))
    pos = jnp.concatenate([jnp.arange(t_prompt, dtype=F32), PAST_LEN + (jnp.arange(TM) % t_sample).astype(F32)])
    ang = jnp.tile(pos[:, None] * theta[None, :], (1, W_B // (HEAD_DIM // 2)))
    n = ang.shape[0]
    return pl.pallas_call(
        _rope_kernel,
        grid=(n // TM,),
        in_specs=[_row_spec(W_B)],
        out_specs=[_row_spec(W_B)] * 2,
        out_shape=[jax.ShapeDtypeStruct((n, W_B), F32)] * 2,
        compiler_params=_params(("parallel",)),
        name="rope_tables",
    )(ang)


def _ret_prep_kernel(q_ref, k_ref, cos_ref, sin_ref, q_o, k_o):
    cos = cos_ref[...]
    sin = sin_ref[...]
    lane = lax.broadcasted_iota(jnp.int32, (TM, W_B), 1)
    first_half = (lane % HEAD_DIM) < (HEAD_DIM // 2)

    def rope(x):
        partner = jnp.where(first_half, pltpu.roll(x, W_B - HEAD_DIM // 2, 1), pltpu.roll(x, HEAD_DIM // 2, 1))
        return x * cos + partner * sin

    q_o[...] = rope(q_ref[...])
    k_o[...] = rope(k_ref[...]) * (HEAD_DIM ** -0.5)


def _ret_prep(cb, cos, sin, table_block):
    n = cb.shape[0]
    tab = pl.BlockSpec((TM, W_B), lambda i: (table_block(i), 0))
    return pl.pallas_call(
        _ret_prep_kernel,
        grid=(n // TM,),
        in_specs=[_row_spec(W_B, 0), _row_spec(W_B, 1), tab, tab],
        out_specs=[_row_spec(W_B)] * 2,
        out_shape=[jax.ShapeDtypeStruct((n, W_B), F32)] * 2,
        compiler_params=_params(("parallel",)),
        name="ret_prep",
    )(cb, cb, cos, sin)


def _ssm_prep_kernel(*refs, tiles_per_seq):
    n_shift_refs = 1 if tiles_per_seq else CONV_W - 1
    cc_ref = refs[0]
    shift_refs = refs[1:1 + n_shift_refs]
    cw_ref, cb_ref, dtb_ref, alog_ref, ex_ref, xdt_o, b_o, c_o, dec_o, xs_o, da_o = refs[1 + n_shift_refs:]
    cc = cc_ref[...]
    x0 = cc[:, W_C:W_C + CONV_DIM]
    if tiles_per_seq:
        before = shift_refs[0][:, W_C:W_C + CONV_DIM]
        first = pl.program_id(0) % tiles_per_seq == 0
        xs_prev = [_shift_rows(x0, before, j, first) for j in range(1, CONV_W)]
    else:
        xs_prev = [r[...] for r in shift_refs]
    cw = cw_ref[...]
    conv = x0 * cw[CONV_W - 1:CONV_W, :] + cb_ref[...]
    for j in range(1, CONV_W):
        conv = conv + xs_prev[j - 1] * cw[CONV_W - 1 - j:CONV_W - j, :]
    act = conv * _sigmoid(conv)
    xs = act[:, 0:W_C]
    dt = _softplus(cc[:, W_C + CONV_DIM:] + dtb_ref[...])
    da = dt * (-jnp.exp(alog_ref[...]))
    da_o[...] = da
    dec_o[...] = jnp.exp(da)
    xdt_o[...] = xs * _dot_select(dt, ex_ref[...])
    b_o[...] = act[:, W_C:W_C + N_GROUPS * N_STATE]
    c_o[...] = act[:, W_C + N_GROUPS * N_STATE:]
    xs_o[...] = xs


def _ssm_prep(cc, shifted, p, tiles_per_seq):
    n = cc.shape[0]
    gn = N_GROUPS * N_STATE
    if tiles_per_seq:
        shift_args, shift_specs = [cc], [_before_spec(COLS_C_PAD)]
    else:
        shift_args, shift_specs = list(shifted), [_row_spec(CONV_DIM)] * (CONV_W - 1)
    return pl.pallas_call(
        functools.partial(_ssm_prep_kernel, tiles_per_seq=tiles_per_seq),
        grid=(n // TM,),
        in_specs=[_row_spec(COLS_C_PAD)] + shift_specs + [
            _full_spec((CONV_W, CONV_DIM)), _full_spec((1, CONV_DIM)), _full_spec((1, LANES)),
            _full_spec((1, LANES)), _full_spec((LANES, W_C))],
        out_specs=[_row_spec(W_C), _row_spec(gn), _row_spec(gn), _row_spec(LANES), _row_spec(W_C),
                   _row_spec(LANES)],
        out_shape=[jax.ShapeDtypeStruct((n, W_C), F32), jax.ShapeDtypeStruct((n, gn), F32),
                   jax.ShapeDtypeStruct((n, gn), F32), jax.ShapeDtypeStruct((n, LANES), F32),
                   jax.ShapeDtypeStruct((n, W_C), F32), jax.ShapeDtypeStruct((n, LANES), F32)],
        compiler_params=_params(("parallel",)),
        name="ssm_prep",
    )(cc, *shift_args, p["conv_w"], p["conv_b"], p["dt_bias"], p["a_log"], p["expand"])


def _scan_kernel(*refs, mode, ni, tt_len, n_tt):
    if mode == "rwkv":
        q_ref, d_ref, a_ref, b_ref, n_ref, a2_ref, s0_ref, y_ref, so_ref, st = refs
    else:
        q_ref, d_ref, a_ref, b_ref, s0_ref, y_ref, so_ref, st = refs
    head = pl.program_id(0)
    tt = pl.program_id(1)

    @pl.when(tt == 0)
    def _():
        st[...] = s0_ref[...]

    def row(ref, t, i):
        return ref[t, pl.ds(i, 1), :]

    def step(t, carry):
        bv = b_ref[t]
        if mode == "rwkv":
            sa_parts = [jnp.zeros_like(bv), jnp.zeros_like(bv)]
            for i in range(ni):
                sa_parts[i % 2] = sa_parts[i % 2] + st[i] * row(n_ref, t, i)
            sa = sa_parts[0] + sa_parts[1]
        elif mode == "ssd":
            d = d_ref[t, pl.ds(head, 1), :]
        else:
            d = d_ref[pl.ds(head, 1), :]
        y_parts = [jnp.zeros_like(bv), jnp.zeros_like(bv)]
        for i in range(ni):
            if mode == "rwkv":
                s = st[i] * row(d_ref, t, i) + row(a_ref, t, i) * bv + row(a2_ref, t, i) * sa
            else:
                s = st[i] * d + row(a_ref, t, i) * bv
            st[i] = s
            y_parts[i % 2] = y_parts[i % 2] + s * row(q_ref, t, i)
        y_ref[t] = y_parts[0] + y_parts[1]
        return carry

    lax.fori_loop(0, tt_len, step, 0)

    @pl.when(tt == n_tt - 1)
    def _():
        so_ref[...] = st[...]


def _scan(mode, n_heads, ni, nj, q, d, a, b, s0, n=None, a2=None):
    t_len = b[0].shape[0]

    def rows(op, width):
        arr, block_of = op
        return arr, pl.BlockSpec((t_len, width, LANES), lambda h, t, f=block_of: (0, f(h), 0))

    ops = [rows(q, ni)]
    if mode == "rwkv":
        ops.append(rows(d, ni))
    elif mode == "ssd":
        ops.append((d, pl.BlockSpec((t_len, SUBLANES, LANES), lambda h, t: (0, 0, 0))))
    else:
        ops.append((d, pl.BlockSpec((SUBLANES, LANES), lambda h, t: (0, 0))))
    ops += [rows(a, ni), rows(b, nj)]
    if mode == "rwkv":
        ops += [rows(n, ni), rows(a2, ni)]
    s_spec = pl.BlockSpec((ni, nj, LANES), lambda h, t: (h, 0, 0))
    y_spec = pl.BlockSpec((t_len, nj, LANES), lambda h, t: (0, h, 0))
    return pl.pallas_call(
        functools.partial(_scan_kernel, mode=mode, ni=ni, tt_len=t_len, n_tt=1),
        grid=(n_heads, 1),
        in_specs=[spec for _, spec in ops] + [s_spec],
        out_specs=[y_spec, s_spec],
        out_shape=[jax.ShapeDtypeStruct((t_len, n_heads * nj, LANES), F32),
                   jax.ShapeDtypeStruct((n_heads * ni, nj, LANES), F32)],
        scratch_shapes=[pltpu.VMEM((ni, nj, LANES), F32)],
        compiler_params=_params(("parallel", "arbitrary")),
        name="scan_" + mode,
    )(*[arr for arr, _ in ops], s0)


def _rwkv_long_kernel(q_ref, d_ref, a_ref, b_ref, n1_ref, a2_ref, y_ref, so_ref, st, sa_s, dots_s, *,
                      ni, tt_len, n_tt):
    tt = pl.program_id(1)

    @pl.when(tt == 0)
    def _():
        st[...] = jnp.zeros_like(st)
        sa_s[...] = jnp.zeros_like(sa_s)

    def row(ref, t, i):
        return ref[i, pl.ds(t, 1), :]

    def fold(x):
        return x + pltpu.roll(x, LANES // 2, 1)

    for k, (u_ref, w_ref) in enumerate(((a_ref, q_ref), (a2_ref, q_ref), (a_ref, n1_ref), (a2_ref, n1_ref))):
        acc = u_ref[0] * w_ref[0]
        for i in range(1, ni):
            acc = acc + u_ref[i] * w_ref[i]
        dots_s[k] = fold(acc)

    def step(t, sa):
        bv = b_ref[t]
        aq, a2q, an, a2n = [dots_s[k, pl.ds(t, 1), :] for k in range(4)]
        y0 = [jnp.zeros_like(bv), jnp.zeros_like(bv)]
        n0 = [jnp.zeros_like(bv), jnp.zeros_like(bv)]
        for i in range(ni):
            sd = st[i] * row(d_ref, t, i)
            st[i] = sd
            y0[i % 2] = y0[i % 2] + sd * row(q_ref, t, i)
            n0[i % 2] = n0[i % 2] + sd * row(n1_ref, t, i)
        y_ref[t] = fold(y0[0] + y0[1]) + bv * aq + sa * a2q
        sa_next = fold(n0[0] + n0[1]) + bv * an + sa * a2n
        for i in range(ni):
            st[i] = st[i] + row(a_ref, t, i) * bv + row(a2_ref, t, i) * sa
        return sa_next

    sa_s[...] = lax.fori_loop(0, tt_len, step, sa_s[...])

    @pl.when(tt == n_tt - 1)
    def _():
        so_ref[...] = st[...]


def _rwkv_long_scan(q, d, a, b, n1, a2, tt_len=64):
    ni, t_len, _ = q.shape
    nj = b.shape[1]
    n_tt = t_len // tt_len
    i_spec = pl.BlockSpec((ni, tt_len, LANES), lambda g, t: (0, t, 0))
    j_spec = pl.BlockSpec((tt_len, nj, LANES), lambda g, t: (t, 0, 0))
    s_spec = pl.BlockSpec((ni, nj, LANES), lambda g, t: (0, 0, 0))
    return pl.pallas_call(
        functools.partial(_rwkv_long_kernel, ni=ni, tt_len=tt_len, n_tt=n_tt),
        grid=(1, n_tt),
        in_specs=[i_spec, i_spec, i_spec, j_spec, i_spec, i_spec],
        out_specs=[j_spec, s_spec],
        out_shape=[jax.ShapeDtypeStruct((t_len, nj, LANES), F32), jax.ShapeDtypeStruct((ni, nj, LANES), F32)],
        scratch_shapes=[pltpu.VMEM((ni, nj, LANES), F32), pltpu.VMEM((nj, LANES), F32),
                        pltpu.VMEM((4, tt_len, LANES), F32)],
        compiler_params=_params(("parallel", "arbitrary")),
        name="scan_rwkv_long",
    )(q, d, a, b, n1, a2)


TR = 128
HALF = LANES // K_SPLIT


Y_PITCH = LANES + SUBLANES


def _to_lanes_kernel(*refs, n_ops, rows, per_half):
    x_refs, o_refs, y2d = refs[:n_ops], refs[n_ops:2 * n_ops], refs[2 * n_ops]
    n_b = x_refs[0].shape[0]
    for x_ref, o_ref in zip(x_refs, o_refs):
        for b in range(n_b):
            xt = x_ref[b].T
            for h in range(H_PAD):
                for s in range(K_SPLIT):
                    lane = s * HALF + b * H_PAD + h
                    f0 = h * HEAD_DIM + (s * rows if per_half else 0)
                    y2d[pl.ds(lane, rows, stride=Y_PITCH), :] = xt[f0:f0 + rows, :]
        for r in range(rows):
            slab = y2d[r * Y_PITCH:r * Y_PITCH + LANES, :].T
            if per_half:
                o_ref[r] = slab
            else:
                o_ref[pl.ds(r, TR, stride=rows), :] = slab


def _long_to_lanes(xs, b, rows, per_half):
    n_ops = len(xs)
    t = xs[0].shape[0] // b
    width = H_PAD * HEAD_DIM
    if per_half:
        o_spec = pl.BlockSpec((rows, TR, LANES), lambda i: (0, i, 0))
        o_shape = jax.ShapeDtypeStruct((rows, t, LANES), F32)
    else:
        o_spec = pl.BlockSpec((TR * rows, LANES), lambda i: (i, 0))
        o_shape = jax.ShapeDtypeStruct((t * rows, LANES), F32)
    outs = pl.pallas_call(
        functools.partial(_to_lanes_kernel, n_ops=n_ops, rows=rows, per_half=per_half),
        grid=(t // TR,),
        in_specs=[pl.BlockSpec((b, TR, width), lambda i: (0, i, 0))] * n_ops,
        out_specs=[o_spec] * n_ops,
        out_shape=[o_shape] * n_ops,
        scratch_shapes=[pltpu.VMEM((rows * Y_PITCH, TR), F32)],
        compiler_params=_params(("parallel",)),
        name="to_lanes",
    )(*[x.reshape(b, t, width) for x in xs])
    return outs if per_half else [o.reshape(t, rows, LANES) for o in outs]


def _from_lanes_kernel(y_ref, o_ref, z2d):
    for v in range(HEAD_DIM):
        z2d[pl.ds(v, LANES, stride=HEAD_DIM), :] = y_ref[pl.ds(v, TR, stride=HEAD_DIM), :].T
    for b in range(o_ref.shape[0]):
        r0 = b * H_PAD * HEAD_DIM
        o_ref[b] = z2d[r0:r0 + W_A, :].T


def _long_from_lanes(y, b):
    t = y.shape[0]
    out = pl.pallas_call(
        _from_lanes_kernel,
        grid=(t // TR,),
        in_specs=[pl.BlockSpec((TR * HEAD_DIM, LANES), lambda i: (i, 0))],
        out_specs=pl.BlockSpec((b, TR, W_A), lambda i: (0, i, 0)),
        out_shape=jax.ShapeDtypeStruct((b, t, W_A), F32),
        scratch_shapes=[pltpu.VMEM((LANES * HEAD_DIM, TR), F32)],
        compiler_params=_params(("parallel",)),
        name="from_lanes",
    )(y.reshape(t * HEAD_DIM, LANES))
    return out.reshape(b * t, W_A)


def _ret_chunk_kernel(q_ref, k_ref, v_ref, y_ref, s_ref, st):
    c = pl.program_id(1)

    @pl.when(c == 0)
    def _():
        st[...] = jnp.zeros_like(st)

    q = q_ref[...]
    k = k_ref[...]
    vb = v_ref[...].astype(BF16)
    kb = k.astype(BF16)
    row = lax.broadcasted_iota(jnp.int32, (LC, W_B), 0)
    head = lax.broadcasted_iota(jnp.int32, (LC, W_B), 1) // HEAD_DIM
    log_g = [math.log(1.0 - 2.0 ** (-5.0 - h)) for h in range(H_B)]
    lg = jnp.full((LC, W_B), log_g[0], F32)
    for h in range(1, H_B):
        lg = jnp.where(head == h, log_g[h], lg)
    rowf = row.astype(F32)
    diff = (lax.broadcasted_iota(jnp.int32, (LC, LC), 0) - lax.broadcasted_iota(jnp.int32, (LC, LC), 1))
    causal = diff >= 0
    difff = jnp.maximum(diff, 0).astype(F32)

    out = jnp.dot((q * jnp.exp(lg * (rowf + 1.0))).astype(BF16), st[...].astype(BF16), preferred_element_type=F32)
    for h in range(H_B):
        qm = jnp.where(head == h, q, 0.0).astype(BF16)
        s = lax.dot_general(qm, kb, (((1,), (1,)), ((), ())), preferred_element_type=F32)
        p = jnp.where(causal, s * jnp.exp(log_g[h] * difff), 0.0).astype(BF16)
        out = out + jnp.where(head == h, jnp.dot(p, vb, preferred_element_type=F32), 0.0)
    y_ref[...] = out

    kt = (k * jnp.exp(lg * (LC - 1.0 - rowf))).T.astype(BF16)
    kv = jnp.dot(kt, vb, preferred_element_type=F32)
    r2 = lax.broadcasted_iota(jnp.int32, (W_B, W_B), 0) // HEAD_DIM
    c2 = lax.broadcasted_iota(jnp.int32, (W_B, W_B), 1) // HEAD_DIM
    cdec = jnp.full((W_B, W_B), math.exp(log_g[0] * LC), F32)
    for h in range(1, H_B):
        cdec = jnp.where(r2 == h, math.exp(log_g[h] * LC), cdec)
    st[...] = st[...] * cdec + jnp.where(r2 == c2, kv, 0.0)
    s_ref[0] = st[...]


def _ret_chunk(q, k, cb, n_seq, t_len):
    n_c = t_len // LC
    rows = lambda col: pl.BlockSpec((LC, W_B), lambda b, c, col=col: (b * n_c + c, col))
    return pl.pallas_call(
        _ret_chunk_kernel,
        grid=(n_seq, n_c),
        in_specs=[rows(0), rows(0), rows(2)],
        out_specs=[rows(0), pl.BlockSpec((1, W_B, W_B), lambda b, c: (b, 0, 0))],
        out_shape=[jax.ShapeDtypeStruct((n_seq * t_len, W_B), F32), jax.ShapeDtypeStruct((n_seq, W_B, W_B), F32)],
        scratch_shapes=[pltpu.VMEM((W_B, W_B), F32)],
        compiler_params=_params(("parallel", "arbitrary")),
        name="ret_chunk",
    )(q, k, cb)


def _ssd_chunk_kernel(x_ref, b_ref, c_ref, da_ref, ex_ref, y_ref, h_ref, st):
    ci = pl.program_id(1)

    @pl.when(ci == 0)
    def _():
        st[...] = jnp.zeros_like(st)

    gn = N_STATE
    half = W_C // N_GROUPS
    ex = ex_ref[...]
    x = x_ref[...]
    xb = x.astype(BF16)
    ri = lax.broadcasted_iota(jnp.int32, (LC, LC), 0)
    cj = lax.broadcasted_iota(jnp.int32, (LC, LC), 1)
    causal = ri >= cj
    cum = _dot_hi(causal.astype(F32), da_ref[...])
    cum_e = _dot_select(cum, ex, 3)
    last_e = cum_e[LC - 1:LC, :]
    cum_t = cum.T
    head = lax.broadcasted_iota(jnp.int32, (LC, W_C), 1) // HEAD_DIM
    bg = [b_ref[:, g * gn:(g + 1) * gn].astype(BF16) for g in range(N_GROUPS)]
    cg = [c_ref[:, g * gn:(g + 1) * gn].astype(BF16) for g in range(N_GROUPS)]
    cb = [lax.dot_general(cg[g], bg[g], (((1,), (1,)), ((), ())), preferred_element_type=F32)
          for g in range(N_GROUPS)]

    hb = st[...].astype(BF16)
    ys = [lax.dot_general(cg[g], hb, (((1,), (1,)), ((), ())), preferred_element_type=F32)
          for g in range(N_GROUPS)]
    y = jnp.where(head < H_C // N_GROUPS, ys[0], ys[1]) * jnp.exp(cum_e)
    for h in range(H_C):
        seg = cum[:, h:h + 1] - cum_t[h:h + 1, :]
        p = (cb[h // (H_C // N_GROUPS)] * jnp.exp(jnp.where(causal, seg, -jnp.inf))).astype(BF16)
        y = y + jnp.where(head == h, jnp.dot(p, xb, preferred_element_type=F32), 0.0)
    y_ref[...] = y

    xt = (x * jnp.exp(last_e - cum_e)).T.astype(BF16)
    upd = [jnp.dot(xt, bg[g], preferred_element_type=F32) for g in range(N_GROUPS)]
    rowi = lax.broadcasted_iota(jnp.int32, (W_C, gn), 0)
    sel = lax.broadcasted_iota(jnp.int32, (W_C, LANES), 1) == lax.broadcasted_iota(jnp.int32, (W_C, LANES), 0) // HEAD_DIM
    tot = jnp.sum(jnp.where(sel, cum[LC - 1:LC, :], 0.0), axis=1, keepdims=True)
    st[...] = st[...] * jnp.exp(tot) + jnp.where(rowi < half, upd[0], upd[1])
    h_ref[0] = st[...]


def _ssd_chunk(xdt, bm, cm, da, ex, n_seq, t_len):
    n_c = t_len // LC
    gn = N_GROUPS * N_STATE
    rows = lambda w: pl.BlockSpec((LC, w), lambda b, c: (b * n_c + c, 0))
    return pl.pallas_call(
        _ssd_chunk_kernel,
        grid=(n_seq, n_c),
        in_specs=[rows(W_C), rows(gn), rows(gn), rows(LANES), pl.BlockSpec((LANES, W_C), lambda b, c: (0, 0))],
        out_specs=[rows(W_C), pl.BlockSpec((1, W_C, N_STATE), lambda b, c: (b, 0, 0))],
        out_shape=[jax.ShapeDtypeStruct((n_seq * t_len, W_C), F32),
                   jax.ShapeDtypeStruct((n_seq, W_C, N_STATE), F32)],
        scratch_shapes=[pltpu.VMEM((W_C, N_STATE), F32)],
        compiler_params=_params(("parallel", "arbitrary")),
        name="ssd_chunk",
    )(xdt, bm, cm, da, ex)


def _post_kernel(x_ref, ya_ref, bonus_ref, ga_ref, ob_ref, gb_ref, yc_ref, xs_ref, z_ref, lnw_ref, lnb_ref,
                 rn_ref, dsk_ref, sn_ref, bda_ref, bdb_ref, wo_ref, out_ref):
    inv_hd = 1.0 / HEAD_DIM
    y = ya_ref[...]
    bda = bda_ref[...]
    mean = _dot_select(y, bda) * inv_hd
    yd = y - mean
    var = _dot_select(yd * yd, bda) * inv_hd
    ya = (yd * lax.rsqrt(var + GN_EPS) * lnw_ref[...] + lnb_ref[...] + bonus_ref[...]) * ga_ref[...]
    o = ob_ref[...]
    ms = _dot_select(o * o, bdb_ref[...]) * inv_hd
    gb = gb_ref[...]
    yb = (gb * _sigmoid(gb)) * (o * lax.rsqrt(ms + RMS_EPS) * rn_ref[...])
    z = z_ref[...]
    yc = (yc_ref[...] + dsk_ref[...] * xs_ref[...]) * (z * _sigmoid(z))
    yc = _rmsnorm(yc, sn_ref[...], GATED_NORM_EPS)
    out_ref[...] = (x_ref[...] + _dot(ya, wo_ref[0:W_A, :]) + _dot(yb, wo_ref[W_A:W_A + W_B, :])
                    + _dot(yc, wo_ref[W_A + W_B:, :]))


def _post(x, ya, bonus, ga, ob, cb, yc, xs, cc, p):
    n = x.shape[0]
    va = _full_spec((1, W_A))
    return pl.pallas_call(
        _post_kernel,
        grid=(n // TM,),
        in_specs=[_row_spec(D_MODEL), _row_spec(W_A), _row_spec(W_A), _row_spec(W_A), _row_spec(W_B),
                  _row_spec(W_B, 3), _row_spec(W_C), _row_spec(W_C), _row_spec(W_C, 0), va, va,
                  _full_spec((1, W_B)), va, va, _full_spec((W_A, W_A)), _full_spec((W_B, W_B)),
                  _full_spec((D_MODEL, D_MODEL))],
        out_specs=_row_spec(D_MODEL),
        out_shape=jax.ShapeDtypeStruct((n, D_MODEL), F32),
        compiler_params=_params(("parallel",)),
        name="post_outproj",
    )(x, ya, bonus, ga, ob, cb, yc, xs, cc, p["ln_w"], p["ln_b"], p["ret_norm"], p["d_skip"], p["ssm_norm"],
      p["bd_a"], p["bd_b"], p["w_out"])


def _ffn_kernel(*refs, gated, final_norm, n_e):
    refs = list(refs)
    x_ref, nf_ref = refs[:2]
    pos = 2
    if gated:
        router_ref = refs[pos]
        pos += 1
    wg_ref, wu_ref, wd_ref = refs[pos:pos + 3]
    pos += 3
    if final_norm:
        nfin_ref = refs[pos]
        pos += 1
    out_ref = refs[pos]
    hb_s, acc_s = refs[pos + 1:pos + 3]
    if gated:
        gate_s, asg_s, pos_s, asg_t, pos_t = refs[pos + 3:pos + 8]
    e = pl.program_id(1)

    @pl.when(e == 0)
    def _():
        h = _rmsnorm(x_ref[...], nf_ref[...], RMS_EPS)
        hb_s[...] = h.astype(BF16)
        acc_s[...] = jnp.zeros_like(acc_s)
        if gated:
            lane = lax.broadcasted_iota(jnp.int32, (TM, LANES), 1)
            logits = jnp.where(lane < N_EXPERTS, _dot_hi(h, router_ref[...]), -jnp.inf)
            p = jnp.exp(logits - jnp.max(logits, axis=-1, keepdims=True))
            p = p / jnp.sum(p, axis=-1, keepdims=True)
            p1 = jnp.max(p, axis=-1, keepdims=True)
            i1 = jnp.min(jnp.where(p == p1, lane, LANES), axis=-1, keepdims=True)
            rest = jnp.where(lane == i1, -1.0, p)
            p2 = jnp.max(rest, axis=-1, keepdims=True)
            i2 = jnp.min(jnp.where(rest == p2, lane, LANES), axis=-1, keepdims=True)
            gate_s[...] = jnp.where(lane == i1, p1, jnp.where(lane == i2, p2, 0.0)) / (p1 + p2)
            assigned = jnp.where((lane == i1) | (lane == i2), 1.0, 0.0)
            earlier = (lax.broadcasted_iota(jnp.int32, (TM, TM), 0) > lax.broadcasted_iota(jnp.int32, (TM, TM), 1))
            rank = jnp.dot(earlier.astype(BF16), assigned.astype(BF16), preferred_element_type=F32)
            asg_s[...] = assigned
            pos_s[...] = rank
            asg_t[...] = assigned.T
            pos_t[...] = rank.T

    if not gated:
        hb = hb_s[...]
        g = jnp.dot(hb, wg_ref[0], preferred_element_type=F32)
        u = jnp.dot(hb, wu_ref[0], preferred_element_type=F32)
        acc_s[...] += _dot(g * _sigmoid(g) * u, wd_ref[0])
    else:
        lane = lax.broadcasted_iota(jnp.int32, (TM, LANES), 1)
        mine = lane == e
        col = lambda ref: jnp.sum(jnp.where(mine, ref[...], 0.0), axis=-1, keepdims=True)
        gate_c, asg_c, pos_c = col(gate_s), col(asg_s), col(pos_s)
        asg_r, pos_r = asg_t[pl.ds(e, 1), :], pos_t[pl.ds(e, 1), :]
        n_chunks = (jnp.sum(asg_r).astype(jnp.int32) + MOE_CHUNK - 1) // MOE_CHUNK

        def chunk(c, carry):
            base = (c * MOE_CHUNK).astype(F32)
            slot_r = lax.broadcasted_iota(jnp.int32, (MOE_CHUNK, TM), 0).astype(F32) + base
            pick = jnp.where((pos_r == slot_r) & (asg_r > 0.0), 1.0, 0.0).astype(BF16)
            rows = jnp.dot(pick, hb_s[...], preferred_element_type=F32).astype(BF16)
            g = jnp.dot(rows, wg_ref[0], preferred_element_type=F32)
            u = jnp.dot(rows, wu_ref[0], preferred_element_type=F32)
            o = _dot(g * _sigmoid(g) * u, wd_ref[0])
            o_hi = o.astype(BF16)
            o_lo = (o - o_hi.astype(F32)).astype(BF16)
            slot_c = lax.broadcasted_iota(jnp.int32, (TM, MOE_CHUNK), 1).astype(F32) + base
            place = jnp.where((pos_c == slot_c) & (asg_c > 0.0), 1.0, 0.0).astype(BF16)
            back = (jnp.dot(place, o_hi, preferred_element_type=F32)
                    + jnp.dot(place, o_lo, preferred_element_type=F32))
            acc_s[...] += gate_c * back
            return carry

        lax.fori_loop(0, n_chunks, chunk, 0)

    @pl.when(e == n_e - 1)
    def _():
        y = x_ref[...] + acc_s[...]
        if final_norm:
            y = _rmsnorm(y, nfin_ref[...], RMS_EPS)
        out_ref[...] = y


def _ffn(x, nf, wg, wu, wd, router=None, nfin=None):
    n = x.shape[0]
    n_e, _, dff = wg.shape
    gated = router is not None
    final_norm = nfin is not None
    vec = pl.BlockSpec((1, D_MODEL), lambda i, e: (0, 0))
    args = [x, nf]
    specs = [pl.BlockSpec((TM, D_MODEL), lambda i, e: (i, 0)), vec]
    if gated:
        args.append(router)
        specs.append(pl.BlockSpec((D_MODEL, LANES), lambda i, e: (0, 0)))
    args += [wg, wu, wd]
    specs += [pl.BlockSpec((1, D_MODEL, dff), lambda i, e: (e, 0, 0)),
              pl.BlockSpec((1, D_MODEL, dff), lambda i, e: (e, 0, 0)),
              pl.BlockSpec((1, dff, D_MODEL), lambda i, e: (e, 0, 0))]
    if final_norm:
        args.append(nfin)
        specs.append(vec)
    scratch = [pltpu.VMEM((TM, D_MODEL), BF16), pltpu.VMEM((TM, D_MODEL), F32)]
    if gated:
        scratch += [pltpu.VMEM((TM, LANES), F32)] * 3 + [pltpu.VMEM((LANES, TM), F32)] * 2
    return pl.pallas_call(
        functools.partial(_ffn_kernel, gated=gated, final_norm=final_norm, n_e=n_e),
        grid=(n // TM, n_e),
        in_specs=specs,
        out_specs=pl.BlockSpec((TM, D_MODEL), lambda i, e: (i, 0)),
        out_shape=jax.ShapeDtypeStruct((n, D_MODEL), F32),
        scratch_shapes=scratch,
        compiler_params=_params(("parallel", "arbitrary")),
        name="moe" if gated else "ffn",
    )(*args)


def _short_to_lanes_kernel(*refs, n_ops, t_len):
    for x_ref, o_ref in zip(refs[:n_ops], refs[n_ops:]):
        for t in range(t_len):
            o_ref[t] = x_ref[pl.ds(t, LANES, stride=t_len), :].T


def _short_to_lanes(xs, t_len):
    n_chunks = [width // LANES for _, _, width in xs]
    grid = max(n_chunks)
    clamp = lambda c, n: jnp.minimum(c, n - 1)
    outs = pl.pallas_call(
        functools.partial(_short_to_lanes_kernel, n_ops=len(xs), t_len=t_len),
        grid=(grid,),
        in_specs=[pl.BlockSpec((LANES * t_len, LANES), lambda c, c0=col // LANES, n=n: (0, c0 + clamp(c, n)))
                  for (_, col, _), n in zip(xs, n_chunks)],
        out_specs=[pl.BlockSpec((t_len, LANES, LANES), lambda c, n=n: (0, clamp(c, n), 0)) for n in n_chunks],
        out_shape=[jax.ShapeDtypeStruct((t_len, width, LANES), F32) for _, _, width in xs],
        compiler_params=_params(("arbitrary",)),
        name="short_to_lanes",
    )(*[x for x, _, _ in xs])
    return outs


def _short_from_lanes_kernel(y_ref, o_ref, *, t_len):
    for t in range(t_len):
        o_ref[pl.ds(t, LANES, stride=t_len), :] = y_ref[t].T


def _short_from_lanes(y):
    t_len, w, _ = y.shape
    return pl.pallas_call(
        functools.partial(_short_from_lanes_kernel, t_len=t_len),
        grid=(w // LANES,),
        in_specs=[pl.BlockSpec((t_len, LANES, LANES), lambda c: (0, c, 0))],
        out_specs=pl.BlockSpec((LANES * t_len, LANES), lambda c: (0, c)),
        out_shape=jax.ShapeDtypeStruct((LANES * t_len, w), F32),
        compiler_params=_params(("parallel",)),
        name="short_from_lanes",
    )(y)


def _state_to_lanes_kernel(x_ref, o_ref, *, ni, nj, j_first):
    xt = x_ref[...].T
    if j_first:
        for j in range(nj):
            o_ref[pl.ds(j, ni, stride=nj), :] = xt[j * ni:(j + 1) * ni, :]
    else:
        o_ref[...] = xt


def _state_to_lanes(s, ni, nj, j_first):
    h = s.shape[1]
    out = pl.pallas_call(
        functools.partial(_state_to_lanes_kernel, ni=ni, nj=nj, j_first=j_first),
        grid=(h,),
        in_specs=[pl.BlockSpec((LANES, ni * nj), lambda i: (0, i))],
        out_specs=pl.BlockSpec((ni * nj, LANES), lambda i: (i, 0)),
        out_shape=jax.ShapeDtypeStruct((h * ni * nj, LANES), F32),
        compiler_params=_params(("parallel",)),
        name="state_to_lanes",
    )(s.reshape(LANES, h * ni * nj))
    return out.reshape(h * ni, nj, LANES)


def _state_from_lanes_kernel(s_ref, o_ref, tmp, *, ni, nj, j_first):
    if j_first:
        for j in range(nj):
            tmp[j * ni:(j + 1) * ni, :] = s_ref[pl.ds(j, ni, stride=nj), :]
        o_ref[...] = tmp[...].T
    else:
        o_ref[...] = s_ref[...].T


def _state_from_lanes(s, n_heads, ni, nj, j_first):
    out = pl.pallas_call(
        functools.partial(_state_from_lanes_kernel, ni=ni, nj=nj, j_first=j_first),
        grid=(n_heads,),
        in_specs=[pl.BlockSpec((ni * nj, LANES), lambda i: (i, 0))],
        out_specs=pl.BlockSpec((LANES, ni * nj), lambda i: (0, i)),
        out_shape=jax.ShapeDtypeStruct((LANES, n_heads * ni * nj), F32),
        scratch_shapes=[pltpu.VMEM((ni * nj, LANES), F32)],
        compiler_params=_params(("parallel",)),
        name="state_from_lanes",
    )(s.reshape(n_heads * ni * nj, LANES))
    return out.reshape((LANES, n_heads, nj, ni) if j_first else (LANES, n_heads, ni, nj))


def _rwkv_long(feats, b):
    r, dec, k2, v, nkk, beta = feats
    t = r.shape[0] // b
    kl = HEAD_DIM // K_SPLIT
    assert b * H_PAD == HALF and t % TR == 0
    nkk_next = jnp.concatenate([nkk[1:], jnp.zeros((1, nkk.shape[1]), F32)], axis=0)
    q, d, a, n1, a2 = _long_to_lanes([r, dec, k2, nkk_next, beta], b, kl, True)
    (vv,) = _long_to_lanes([v], b, HEAD_DIM, False)
    y, s = _rwkv_long_scan(q, d, a, vv, n1, a2)
    s = s.reshape(kl, HEAD_DIM, K_SPLIT, b, H_PAD)[..., :H_A]
    return _long_from_lanes(y, b), jnp.transpose(s, (3, 4, 1, 2, 0)).reshape(b, H_A, HEAD_DIM, HEAD_DIM)


def _rwkv_short(feats, t_len, s0):
    r, dec, k2, v, nkk, beta = _short_to_lanes([(f, 0, W_A) for f in feats], t_len)
    hd = HEAD_DIM
    own = lambda h: h
    y, s = _scan("rwkv", H_A, hd, hd, (r, own), (dec, own), (k2, own), (v, own),
                 _state_to_lanes(s0, hd, hd, True), n=(nkk, own), a2=(beta, own))
    return _short_from_lanes(y), _state_from_lanes(s, H_A, hd, hd, True)


def _ret_short(qh, kh, cb, t_len, s0):
    q, k, v = _short_to_lanes([(qh, 0, W_B), (kh, 0, W_B), (cb, 2 * W_B, W_B)], t_len)
    hd = HEAD_DIM
    gam = np.ones((SUBLANES, LANES), np.float32)
    gam[:H_B] = (1.0 - 2.0 ** (-5.0 - np.arange(H_B, dtype=np.float64)))[:, None]
    own = lambda h: h
    y, s = _scan("ret", H_B, hd, hd, (q, own), jnp.asarray(gam), (k, own), (v, own),
                 _state_to_lanes(s0, hd, hd, False))
    return _short_from_lanes(y), _state_from_lanes(s, H_B, hd, hd, False)


def _ssd_short(xdt, bm, cm, dssm, t_len, s0):
    gn = N_GROUPS * N_STATE
    x, bl, cl, dl = _short_to_lanes([(xdt, 0, W_C), (bm, 0, gn), (cm, 0, gn), (dssm, 0, LANES)], t_len)
    group = lambda h: h // (H_C // N_GROUPS)
    y, s = _scan("ssd", H_C, N_STATE, HEAD_DIM, (cl, group), dl, (bl, group), (x, lambda h: h),
                 _state_to_lanes(s0, N_STATE, HEAD_DIM, True))
    return _short_from_lanes(y), _state_from_lanes(s, H_C, N_STATE, HEAD_DIM, True)


def _block_diag_ones(width):
    idx = np.arange(width) // HEAD_DIM
    return jnp.asarray((idx[:, None] == idx[None, :]).astype(np.float32), BF16)


def _shifted(x, prev_rows, shift):
    b, t, c = x.shape
    p = prev_rows.shape[1]
    full = jnp.concatenate([prev_rows, x], axis=1)
    return full[:, p - shift:p - shift + t].reshape(b * t, c)


def _layer(x, b, p, st, rope, table_block):
    n = x.shape[0]
    t = n // b
    fresh = st is None
    tiles_per_seq = t // TM if fresh else 0
    ca, cb, cc = _in_proj(x, p["norm_mix"], p["w_in"])
    ca3 = ca.reshape(b, t, COLS_A)
    xbc_tail = cc.reshape(b, t, COLS_C_PAD)[:, -(CONV_W - 1):, W_C:W_C + CONV_DIM]
    if fresh:
        prev, shifted = None, None
        conv_new = xbc_tail
    else:
        prev = _shifted(ca3, st["shift"][:, None, :], 1)
        xbc = cc[:, W_C:W_C + CONV_DIM].reshape(b, t, CONV_DIM)
        shifted = [_shifted(xbc, st["conv"], j) for j in range(1, CONV_W)]
        conv_new = jnp.concatenate([st["conv"], xbc_tail], axis=1)[:, -(CONV_W - 1):]

    r, dec, k2, v, nkk, beta, ga, bonus = _rwkv_prep(ca, prev, p, tiles_per_seq)
    qh, kh = _ret_prep(cb, rope[0], rope[1], table_block)
    xdt, bm, cm, dssm, xs, da = _ssm_prep(cc, shifted, p, tiles_per_seq)

    feats = (r, dec, k2, v, nkk, beta)
    if fresh:
        ya, s_rwkv = _rwkv_long(feats, b)
        yb, s_full = _ret_chunk(qh, kh, cb, b, t)
        s_ret = jnp.stack([s_full[:, h * HEAD_DIM:(h + 1) * HEAD_DIM, h * HEAD_DIM:(h + 1) * HEAD_DIM]
                           for h in range(H_B)], axis=1)
        yc, s_ssm = _ssd_chunk(xdt, bm, cm, da, p["expand"], b, t)
        s_ssm = s_ssm.reshape(b, H_C, HEAD_DIM, N_STATE)
    else:
        assert b == LANES
        ya, s_rwkv = _rwkv_short(feats, t, st["rwkv"])
        yb, s_ret = _ret_short(qh, kh, cb, t, st["ret"])
        yc, s_ssm = _ssd_short(xdt, bm, cm, dssm, t, st["ssm"])

    x2 = _post(x, ya, bonus, ga, yb, cb, yc, xs, cc, p)
    x = _ffn(x2, p["norm_ffn"], p["wg"], p["wu"], p["wd"], router=p["router"], nfin=p["norm_final"])
    return x, (s_rwkv, ca3[:, -1], s_ret, s_ssm, conv_new)


def kernel(x_prompt, x_sample, state_rwkv, state_shift, state_ret, state_ssm, state_conv, norm_mix, w_in, rwkv_mu, rwkv_w0, rwkv_w_up, rwkv_a0, rwkv_a_up, rwkv_g_up, rwkv_k_k, rwkv_k_a, rwkv_r_k, rwkv_ln_w, rwkv_ln_b, ret_norm, ssm_conv_w, ssm_conv_b, ssm_dt_bias, ssm_a_log, ssm_d, ssm_norm, w_out, norm_ffn, ffn_w_gate, ffn_w_up, ffn_w_down, moe_router, moe_w_gate, moe_w_up, moe_w_down, norm_final):
    bp, tp, _ = x_prompt.shape
    bs, ts, _ = x_sample.shape
    depth = w_in.shape[0]
    assert tp % TM == 0 and tp % LC == 0 and (bs * ts) % TM == 0 and TM % ts == 0

    rope = _rope_tables(tp, ts)
    bd_a, bd_b = _block_diag_ones(W_A), _block_diag_ones(W_B)
    expand = np.zeros((LANES, W_C), np.float32)
    expand[np.arange(W_C) // HEAD_DIM, np.arange(W_C)] = 1.0
    expand = jnp.asarray(expand, BF16)
    row = lambda v: v.reshape(1, -1)
    pad_l = lambda v: jnp.pad(v, (0, LANES - v.shape[0])).reshape(1, LANES)

    xp = x_prompt.reshape(bp * tp, D_MODEL)
    xs = x_sample.reshape(bs * ts, D_MODEL)
    new_p, new_s = [], []
    for i in range(depth):
        j = i // 2
        p = dict(
            norm_mix=row(norm_mix[i]),
            w_in=jnp.pad(w_in[i], ((0, 0), (0, COLS_C_PAD - COLS_C))).astype(BF16),
            mu=row(rwkv_mu[i]), w0=row(rwkv_w0[i]), a0=row(rwkv_a0[i]), k_k=row(rwkv_k_k[i]),
            k_a=row(rwkv_k_a[i]), r_k=row(rwkv_r_k[i]),
            w_up=jnp.pad(rwkv_w_up[i], ((0, AAA_LORA), (0, 0))).astype(BF16),
            a_up=jnp.pad(rwkv_a_up[i], ((DECAY_LORA, 0), (0, 0))).astype(BF16),
            g_up=rwkv_g_up[i].astype(BF16), bd_a=bd_a, bd_b=bd_b,
            conv_w=ssm_conv_w[i], conv_b=row(ssm_conv_b[i]), dt_bias=pad_l(ssm_dt_bias[i]),
            a_log=pad_l(ssm_a_log[i]), expand=expand,
            ln_w=row(rwkv_ln_w[i]), ln_b=row(rwkv_ln_b[i]), ret_norm=row(ret_norm[i]),
            d_skip=row(jnp.repeat(ssm_d[i], HEAD_DIM)), ssm_norm=row(ssm_norm[i]), w_out=w_out[i].astype(BF16),
            norm_ffn=row(norm_ffn[i]), norm_final=row(norm_final) if i == depth - 1 else None)
        if i % 2 == 0:
            dff = ffn_w_gate.shape[-1] // 2
            p.update(router=None,
                     wg=ffn_w_gate[j].reshape(D_MODEL, 2, dff).transpose(1, 0, 2).astype(BF16),
                     wu=ffn_w_up[j].reshape(D_MODEL, 2, dff).transpose(1, 0, 2).astype(BF16),
                     wd=ffn_w_down[j].reshape(2, dff, D_MODEL).astype(BF16))
        else:
            p.update(router=jnp.pad(moe_router[j], ((0, 0), (0, LANES - N_EXPERTS))),
                     wg=moe_w_gate[j].astype(BF16), wu=moe_w_up[j].astype(BF16), wd=moe_w_down[j].astype(BF16))

        xp, st_p = _layer(xp, bp, p, None, rope, lambda t: t % (tp // TM))
        st = dict(rwkv=state_rwkv[i], shift=state_shift[i], ret=state_ret[i], ssm=state_ssm[i],
                  conv=state_conv[i])
        xs, st_s = _layer(xs, bs, p, st, rope, lambda t: tp // TM)
        new_p.append(st_p)
        new_s.append(st_s)

    stack = lambda sts: tuple(jnp.stack(s) for s in zip(*sts))
    return (xp.reshape(bp, tp, D_MODEL), xs.reshape(bs, ts, D_MODEL)) + stack(new_p) + stack(new_s)
```

```python
import functools
import math

import numpy as np
import jax
import jax.numpy as jnp
from jax import lax
from jax.experimental import pallas as pl
from jax.experimental.pallas import tpu as pltpu

F32 = jnp.float32
BF16 = jnp.bfloat16
HIGHEST = lax.Precision.HIGHEST

LANES = 128
SUBLANES = 8
VMEM_LIMIT = 56 * 1024 * 1024

D_MODEL = 1024
HEAD_DIM = 64
H_A, H_B, H_C = 6, 4, 6
W_A, W_B, W_C = H_A * HEAD_DIM, H_B * HEAD_DIM, H_C * HEAD_DIM
DECAY_LORA, AAA_LORA, GATE_LORA = 64, 64, 128
COLS_A = 3 * W_A + DECAY_LORA + AAA_LORA + GATE_LORA
COLS_B = 4 * W_B
N_STATE, N_GROUPS, CONV_W = 128, 2, 4
CONV_DIM = W_C + 2 * N_GROUPS * N_STATE
COLS_C = W_C + CONV_DIM + H_C
COLS_C_PAD = 1408
ROPE_BASE = 10000.0
RMS_EPS = 1e-6
GN_EPS = 64e-5
GATED_NORM_EPS = 1e-5
N_EXPERTS = 8
PAST_LEN = 16384

TM = 512
MOE_CHUNK = 160
LC = 256
H_PAD = 8
K_SPLIT = 2


def _dot(a, b):
    return jnp.dot(a.astype(BF16), b.astype(BF16), preferred_element_type=F32)


def _dot_hi(a, b):
    return jnp.dot(a, b, precision=HIGHEST, preferred_element_type=F32)


def _dot_select(a, sel, terms=2):
    out = None
    rest = a
    for _ in range(terms):
        piece = rest.astype(BF16)
        part = jnp.dot(piece, sel, preferred_element_type=F32)
        out = part if out is None else out + part
        rest = rest - piece.astype(F32)
    return out


def _sigmoid(x):
    return 1.0 / (1.0 + jnp.exp(-x))


def _softplus(x):
    return jnp.maximum(x, 0.0) + jnp.log1p(jnp.exp(-jnp.abs(x)))


def _rmsnorm(x, g, eps):
    return x * lax.rsqrt(jnp.mean(x * x, axis=-1, keepdims=True) + eps) * g


def _params(sem):
    return pltpu.CompilerParams(dimension_semantics=sem, vmem_limit_bytes=VMEM_LIMIT)


def _row_spec(width, col=0):
    return pl.BlockSpec((TM, width), lambda i, c=col: (i, c))


def _before_spec(width):
    return pl.BlockSpec((SUBLANES, width), lambda i: (jnp.maximum(i * (TM // SUBLANES) - 1, 0), 0))


def _full_spec(shape):
    nd = len(shape)
    return pl.BlockSpec(shape, lambda i, n=nd: (0,) * n)


def _shift_rows(x, before, j, first):
    rolled = pltpu.roll(x, j, 0)
    prev = jnp.where(first, 0.0, pltpu.roll(before, j, 0))
    row = lax.broadcasted_iota(jnp.int32, prev.shape, 0)
    top = jnp.where(row < j, prev, rolled[0:SUBLANES])
    return jnp.concatenate([top, rolled[SUBLANES:]], axis=0)


def _in_proj_kernel(x_ref, g_ref, w_ref, oa_ref, ob_ref, oc_ref):
    h = _rmsnorm(x_ref[...], g_ref[...], RMS_EPS).astype(BF16)
    oa_ref[...] = jnp.dot(h, w_ref[:, 0:COLS_A], preferred_element_type=F32)
    ob_ref[...] = jnp.dot(h, w_ref[:, COLS_A:COLS_A + COLS_B], preferred_element_type=F32)
    oc_ref[...] = jnp.dot(h, w_ref[:, COLS_A + COLS_B:], preferred_element_type=F32)


def _in_proj(x, g, w):
    n = x.shape[0]
    wtot = w.shape[1]
    return pl.pallas_call(
        _in_proj_kernel,
        grid=(n // TM,),
        in_specs=[_row_spec(D_MODEL), _full_spec((1, D_MODEL)), _full_spec((D_MODEL, wtot))],
        out_specs=[_row_spec(COLS_A), _row_spec(COLS_B), _row_spec(COLS_C_PAD)],
        out_shape=[jax.ShapeDtypeStruct((n, COLS_A), F32), jax.ShapeDtypeStruct((n, COLS_B), F32),
                   jax.ShapeDtypeStruct((n, COLS_C_PAD), F32)],
        compiler_params=_params(("parallel",)),
        name="in_proj",
    )(x, g, w)


def _rwkv_prep_kernel(c_ref, p_ref, mu_ref, w0_ref, a0_ref, kk_ref, ka_ref, rk_ref, wup_ref, aup_ref,
                      gup_ref, bd_ref, r_o, d_o, k_o, v_o, n_o, b_o, g_o, bonus_o, *, tiles_per_seq):
    c = c_ref[...]
    if tiles_per_seq:
        prev = _shift_rows(c, p_ref[...], 1, pl.program_id(0) % tiles_per_seq == 0)
    else:
        prev = p_ref[...]
    xm = c + (prev - c) * mu_ref[...]
    r = xm[:, 0:W_A]
    k = xm[:, W_A:2 * W_A]
    v = xm[:, 2 * W_A:3 * W_A]
    lora = xm[:, 3 * W_A:3 * W_A + DECAY_LORA + AAA_LORA]
    gd = xm[:, 3 * W_A + DECAY_LORA + AAA_LORA:]
    w = w0_ref[...] + _dot(jnp.tanh(lora), wup_ref[...])
    decay = jnp.exp(-math.exp(-0.5) * _sigmoid(w))
    a = _sigmoid(a0_ref[...] + _dot(lora, aup_ref[...]))
    g = _dot(_sigmoid(gd), gup_ref[...])
    bd = bd_ref[...]
    kk = k * kk_ref[...]
    kk = kk / jnp.maximum(jnp.sqrt(_dot_select(kk * kk, bd)), 1e-12)
    k2 = k * (1.0 + (a - 1.0) * ka_ref[...])
    for o_ref, val in ((r_o, r), (d_o, decay), (k_o, k2), (v_o, v), (n_o, -kk), (b_o, kk * a)):
        o_ref[:, 0:W_A] = val
        if o_ref.shape[1] > W_A:
            o_ref[:, W_A:] = jnp.zeros((TM, o_ref.shape[1] - W_A), F32)
    g_o[...] = g
    bonus_o[...] = _dot_select(r * k2 * rk_ref[...], bd) * v


def _rwkv_prep(ca, prev, p, tiles_per_seq):
    n = ca.shape[0]
    vec = _full_spec((1, W_A))
    lora_spec = _full_spec((DECAY_LORA + AAA_LORA, W_A))
    p_spec = _before_spec(COLS_A) if tiles_per_seq else _row_spec(COLS_A)
    w_scan = H_PAD * HEAD_DIM if tiles_per_seq else W_A
    widths = [w_scan] * 6 + [W_A] * 2
    return pl.pallas_call(
        functools.partial(_rwkv_prep_kernel, tiles_per_seq=tiles_per_seq),
        grid=(n // TM,),
        in_specs=[_row_spec(COLS_A), p_spec, _full_spec((1, COLS_A)), vec, vec, vec, vec, vec,
                  lora_spec, lora_spec, _full_spec((GATE_LORA, W_A)), _full_spec((W_A, W_A))],
        out_specs=[_row_spec(w) for w in widths],
        out_shape=[jax.ShapeDtypeStruct((n, w), F32) for w in widths],
        compiler_params=_params(("parallel",)),
        name="rwkv_prep",
    )(ca, ca if tiles_per_seq else prev, p["mu"], p["w0"], p["a0"], p["k_k"], p["k_a"], p["r_k"],
      p["w_up"], p["a_up"], p["g_up"], p["bd_a"])


def _rope_kernel(ang_ref, cos_o, sin_o):
    ang = ang_ref[...]
    lane = lax.broadcasted_iota(jnp.int32, ang.shape, 1)
    cos_o[...] = jnp.cos(ang)
    sin_o[...] = jnp.where((lane % HEAD_DIM) < (HEAD_DIM // 2), -jnp.sin(ang), jnp.sin(ang))


def _rope_tables(t_prompt, t_sample):
    theta = 1.0 / (ROPE_BASE ** jnp.linspace(0.0, 1.0, HEAD_DIM // 2, dtype=F32))
    pos = jnp.concatenate([jnp.arange(t_prompt, dtype=F32), PAST_LEN + (jnp.arange(TM) % t_sample).astype(F32)])
    ang = jnp.tile(pos[:, None] * theta[None, :], (1, W_B // (HEAD_DIM // 2)))
    n = ang.shape[0]
    return pl.pallas_call(
        _rope_kernel,
        grid=(n // TM,),
        in_specs=[_row_spec(W_B)],
        out_specs=[_row_spec(W_B)] * 2,
        out_shape=[jax.ShapeDtypeStruct((n, W_B), F32)] * 2,
        compiler_params=_params(("parallel",)),
        name="rope_tables",
    )(ang)


def _ret_prep_kernel(q_ref, k_ref, cos_ref, sin_ref, q_o, k_o):
    cos = cos_ref[...]
    sin = sin_ref[...]
    lane = lax.broadcasted_iota(jnp.int32, (TM, W_B), 1)
    first_half = (lane % HEAD_DIM) < (HEAD_DIM // 2)

    def rope(x):
        partner = jnp.where(first_half, pltpu.roll(x, W_B - HEAD_DIM // 2, 1), pltpu.roll(x, HEAD_DIM // 2, 1))
        return x * cos + partner * sin

    q_o[...] = rope(q_ref[...])
    k_o[...] = rope(k_ref[...]) * (HEAD_DIM ** -0.5)


def _ret_prep(cb, cos, sin, table_block):
    n = cb.shape[0]
    tab = pl.BlockSpec((TM, W_B), lambda i: (table_block(i), 0))
    return pl.pallas_call(
        _ret_prep_kernel,
        grid=(n // TM,),
        in_specs=[_row_spec(W_B, 0), _row_spec(W_B, 1), tab, tab],
        out_specs=[_row_spec(W_B)] * 2,
        out_shape=[jax.ShapeDtypeStruct((n, W_B), F32)] * 2,
        compiler_params=_params(("parallel",)),
        name="ret_prep",
    )(cb, cb, cos, sin)


def _ssm_prep_kernel(*refs, tiles_per_seq):
    n_shift_refs = 1 if tiles_per_seq else CONV_W - 1
    cc_ref = refs[0]
    shift_refs = refs[1:1 + n_shift_refs]
    cw_ref, cb_ref, dtb_ref, alog_ref, ex_ref, xdt_o, b_o, c_o, dec_o, xs_o, da_o = refs[1 + n_shift_refs:]
    cc = cc_ref[...]
    x0 = cc[:, W_C:W_C + CONV_DIM]
    if tiles_per_seq:
        before = shift_refs[0][:, W_C:W_C + CONV_DIM]
        first = pl.program_id(0) % tiles_per_seq == 0
        xs_prev = [_shift_rows(x0, before, j, first) for j in range(1, CONV_W)]
    else:
        xs_prev = [r[...] for r in shift_refs]
    cw = cw_ref[...]
    conv = x0 * cw[CONV_W - 1:CONV_W, :] + cb_ref[...]
    for j in range(1, CONV_W):
        conv = conv + xs_prev[j - 1] * cw[CONV_W - 1 - j:CONV_W - j, :]
    act = conv * _sigmoid(conv)
    xs = act[:, 0:W_C]
    dt = _softplus(cc[:, W_C + CONV_DIM:] + dtb_ref[...])
    da = dt * (-jnp.exp(alog_ref[...]))
    da_o[...] = da
    dec_o[...] = jnp.exp(da)
    xdt_o[...] = xs * _dot_select(dt, ex_ref[...])
    b_o[...] = act[:, W_C:W_C + N_GROUPS * N_STATE]
    c_o[...] = act[:, W_C + N_GROUPS * N_STATE:]
    xs_o[...] = xs


def _ssm_prep(cc, shifted, p, tiles_per_seq):
    n = cc.shape[0]
    gn = N_GROUPS * N_STATE
    if tiles_per_seq:
        shift_args, shift_specs = [cc], [_before_spec(COLS_C_PAD)]
    else:
        shift_args, shift_specs = list(shifted), [_row_spec(CONV_DIM)] * (CONV_W - 1)
    return pl.pallas_call(
        functools.partial(_ssm_prep_kernel, tiles_per_seq=tiles_per_seq),
        grid=(n // TM,),
        in_specs=[_row_spec(COLS_C_PAD)] + shift_specs + [
            _full_spec((CONV_W, CONV_DIM)), _full_spec((1, CONV_DIM)), _full_spec((1, LANES)),
            _full_spec((1, LANES)), _full_spec((LANES, W_C))],
        out_specs=[_row_spec(W_C), _row_spec(gn), _row_spec(gn), _row_spec(LANES), _row_spec(W_C),
                   _row_spec(LANES)],
        out_shape=[jax.ShapeDtypeStruct((n, W_C), F32), jax.ShapeDtypeStruct((n, gn), F32),
                   jax.ShapeDtypeStruct((n, gn), F32), jax.ShapeDtypeStruct((n, LANES), F32),
                   jax.ShapeDtypeStruct((n, W_C), F32), jax.ShapeDtypeStruct((n, LANES), F32)],
        compiler_params=_params(("parallel",)),
        name="ssm_prep",
    )(cc, *shift_args, p["conv_w"], p["conv_b"], p["dt_bias"], p["a_log"], p["expand"])


def _scan_kernel(*refs, mode, ni, tt_len, n_tt):
    if mode == "rwkv":
        q_ref, d_ref, a_ref, b_ref, n_ref, a2_ref, s0_ref, y_ref, so_ref, st = refs
    else:
        q_ref, d_ref, a_ref, b_ref, s0_ref, y_ref, so_ref, st = refs
    head = pl.program_id(0)
    tt = pl.program_id(1)

    @pl.when(tt == 0)
    def _():
        st[...] = s0_ref[...]

    def row(ref, t, i):
        return ref[t, pl.ds(i, 1), :]

    def step(t, carry):
        bv = b_ref[t]
        if mode == "rwkv":
            sa_parts = [jnp.zeros_like(bv), jnp.zeros_like(bv)]
            for i in range(ni):
                sa_parts[i % 2] = sa_parts[i % 2] + st[i] * row(n_ref, t, i)
            sa = sa_parts[0] + sa_parts[1]
        elif mode == "ssd":
            d = d_ref[t, pl.ds(head, 1), :]
        else:
            d = d_ref[pl.ds(head, 1), :]
        y_parts = [jnp.zeros_like(bv), jnp.zeros_like(bv)]
        for i in range(ni):
            if mode == "rwkv":
                s = st[i] * row(d_ref, t, i) + row(a_ref, t, i) * bv + row(a2_ref, t, i) * sa
            else:
                s = st[i] * d + row(a_ref, t, i) * bv
            st[i] = s
            y_parts[i % 2] = y_parts[i % 2] + s * row(q_ref, t, i)
        y_ref[t] = y_parts[0] + y_parts[1]
        return carry

    lax.fori_loop(0, tt_len, step, 0)

    @pl.when(tt == n_tt - 1)
    def _():
        so_ref[...] = st[...]


def _scan(mode, n_heads, ni, nj, q, d, a, b, s0, n=None, a2=None):
    t_len = b[0].shape[0]

    def rows(op, width):
        arr, block_of = op
        return arr, pl.BlockSpec((t_len, width, LANES), lambda h, t, f=block_of: (0, f(h), 0))

    ops = [rows(q, ni)]
    if mode == "rwkv":
        ops.append(rows(d, ni))
    elif mode == "ssd":
        ops.append((d, pl.BlockSpec((t_len, SUBLANES, LANES), lambda h, t: (0, 0, 0))))
    else:
        ops.append((d, pl.BlockSpec((SUBLANES, LANES), lambda h, t: (0, 0))))
    ops += [rows(a, ni), rows(b, nj)]
    if mode == "rwkv":
        ops += [rows(n, ni), rows(a2, ni)]
    s_spec = pl.BlockSpec((ni, nj, LANES), lambda h, t: (h, 0, 0))
    y_spec = pl.BlockSpec((t_len, nj, LANES), lambda h, t: (0, h, 0))
    return pl.pallas_call(
        functools.partial(_scan_kernel, mode=mode, ni=ni, tt_len=t_len, n_tt=1),
        grid=(n_heads, 1),
        in_specs=[spec for _, spec in ops] + [s_spec],
        out_specs=[y_spec, s_spec],
        out_shape=[jax.ShapeDtypeStruct((t_len, n_heads * nj, LANES), F32),
                   jax.ShapeDtypeStruct((n_heads * ni, nj, LANES), F32)],
        scratch_shapes=[pltpu.VMEM((ni, nj, LANES), F32)],
        compiler_params=_params(("parallel", "arbitrary")),
        name="scan_" + mode,
    )(*[arr for arr, _ in ops], s0)


def _rwkv_long_kernel(q_ref, d_ref, a_ref, b_ref, n1_ref, a2_ref, y_ref, so_ref, st, sa_s, dots_s, rows_s, *,
                      ni, tt_len, n_tt):
    tt = pl.program_id(1)

    @pl.when(tt == 0)
    def _():
        st[...] = jnp.zeros_like(st)
        sa_s[...] = jnp.zeros_like(sa_s)

    refs_i = (q_ref, d_ref, a_ref, n1_ref, a2_ref)
    pitch = tt_len + SUBLANES
    for k, ref in enumerate(refs_i):
        for i in range(ni):
            rows_s[i, k * pitch:k * pitch + tt_len, :] = ref[i]

    op_q, op_d, op_a, op_n1, op_a2 = range(len(refs_i))

    def row(k, t, i):
        return rows_s[i, pl.ds(k * pitch + t, 1), :]

    def fold(x):
        return x + pltpu.roll(x, LANES // 2, 1)

    for k, (u_ref, w_ref) in enumerate(((a_ref, q_ref), (a2_ref, q_ref), (a_ref, n1_ref), (a2_ref, n1_ref))):
        acc = u_ref[0] * w_ref[0]
        for i in range(1, ni):
            acc = acc + u_ref[i] * w_ref[i]
        dots_s[k] = fold(acc)

    def step(t, sa):
        bv = b_ref[t]
        aq, a2q, an, a2n = [dots_s[k, pl.ds(t, 1), :] for k in range(4)]
        y0 = [jnp.zeros_like(bv), jnp.zeros_like(bv)]
        n0 = [jnp.zeros_like(bv), jnp.zeros_like(bv)]
        for i in range(ni):
            sd = st[i] * row(op_d, t, i)
            st[i] = sd
            y0[i % 2] = y0[i % 2] + sd * row(op_q, t, i)
            n0[i % 2] = n0[i % 2] + sd * row(op_n1, t, i)
        y_ref[t] = fold(y0[0] + y0[1]) + bv * aq + sa * a2q
        sa_next = fold(n0[0] + n0[1]) + bv * an + sa * a2n
        for i in range(ni):
            st[i] = st[i] + row(op_a, t, i) * bv + row(op_a2, t, i) * sa
        return sa_next

    sa_s[...] = lax.fori_loop(0, tt_len, step, sa_s[...])

    @pl.when(tt == n_tt - 1)
    def _():
        so_ref[...] = st[...]


def _rwkv_long_scan(q, d, a, b, n1, a2, tt_len=64):
    ni, t_len, _ = q.shape
    nj = b.shape[1]
    n_tt = t_len // tt_len
    i_spec = pl.BlockSpec((ni, tt_len, LANES), lambda g, t: (0, t, 0))
    j_spec = pl.BlockSpec((tt_len, nj, LANES), lambda g, t: (t, 0, 0))
    s_spec = pl.BlockSpec((ni, nj, LANES), lambda g, t: (0, 0, 0))
    return pl.pallas_call(
        functools.partial(_rwkv_long_kernel, ni=ni, tt_len=tt_len, n_tt=n_tt),
        grid=(1, n_tt),
        in_specs=[i_spec, i_spec, i_spec, j_spec, i_spec, i_spec],
        out_specs=[j_spec, s_spec],
        out_shape=[jax.ShapeDtypeStruct((t_len, nj, LANES), F32), jax.ShapeDtypeStruct((ni, nj, LANES), F32)],
        scratch_shapes=[pltpu.VMEM((ni, nj, LANES), F32), pltpu.VMEM((nj, LANES), F32),
                        pltpu.VMEM((4, tt_len, LANES), F32),
                        pltpu.VMEM((ni, 5 * (tt_len + SUBLANES), LANES), F32)],
        compiler_params=_params(("parallel", "arbitrary")),
        name="scan_rwkv_long",
    )(q, d, a, b, n1, a2)


TR = 128
HALF = LANES // K_SPLIT


Y_PITCH = LANES + SUBLANES


def _to_lanes_kernel(*refs, n_ops, rows, per_half):
    x_refs, o_refs, y2d = refs[:n_ops], refs[n_ops:2 * n_ops], refs[2 * n_ops]
    n_b = x_refs[0].shape[0]
    for x_ref, o_ref in zip(x_refs, o_refs):
        for b in range(n_b):
            xt = x_ref[b].T
            for h in range(H_PAD):
                for s in range(K_SPLIT):
                    lane = s * HALF + b * H_PAD + h
                    f0 = h * HEAD_DIM + (s * rows if per_half else 0)
                    y2d[pl.ds(lane, rows, stride=Y_PITCH), :] = xt[f0:f0 + rows, :]
        for r in range(rows):
            slab = y2d[r * Y_PITCH:r * Y_PITCH + LANES, :].T
            if per_half:
                o_ref[r] = slab
            else:
                o_ref[pl.ds(r, TR, stride=rows), :] = slab


def _long_to_lanes(xs, b, rows, per_half):
    n_ops = len(xs)
    t = xs[0].shape[0] // b
    width = H_PAD * HEAD_DIM
    if per_half:
        o_spec = pl.BlockSpec((rows, TR, LANES), lambda i: (0, i, 0))
        o_shape = jax.ShapeDtypeStruct((rows, t, LANES), F32)
    else:
        o_spec = pl.BlockSpec((TR * rows, LANES), lambda i: (i, 0))
        o_shape = jax.ShapeDtypeStruct((t * rows, LANES), F32)
    outs = pl.pallas_call(
        functools.partial(_to_lanes_kernel, n_ops=n_ops, rows=rows, per_half=per_half),
        grid=(t // TR,),
        in_specs=[pl.BlockSpec((b, TR, width), lambda i: (0, i, 0))] * n_ops,
        out_specs=[o_spec] * n_ops,
        out_shape=[o_shape] * n_ops,
        scratch_shapes=[pltpu.VMEM((rows * Y_PITCH, TR), F32)],
        compiler_params=_params(("parallel",)),
        name="to_lanes",
    )(*[x.reshape(b, t, width) for x in xs])
    return outs if per_half else [o.reshape(t, rows, LANES) for o in outs]


def _from_lanes_kernel(y_ref, o_ref, z2d):
    for v in range(HEAD_DIM):
        z2d[pl.ds(v, LANES, stride=HEAD_DIM), :] = y_ref[pl.ds(v, TR, stride=HEAD_DIM), :].T
    for b in range(o_ref.shape[0]):
        r0 = b * H_PAD * HEAD_DIM
        o_ref[b] = z2d[r0:r0 + W_A, :].T


def _long_from_lanes(y, b):
    t = y.shape[0]
    out = pl.pallas_call(
        _from_lanes_kernel,
        grid=(t // TR,),
        in_specs=[pl.BlockSpec((TR * HEAD_DIM, LANES), lambda i: (i, 0))],
        out_specs=pl.BlockSpec((b, TR, W_A), lambda i: (0, i, 0)),
        out_shape=jax.ShapeDtypeStruct((b, t, W_A), F32),
        scratch_shapes=[pltpu.VMEM((LANES * HEAD_DIM, TR), F32)],
        compiler_params=_params(("parallel",)),
        name="from_lanes",
    )(y.reshape(t * HEAD_DIM, LANES))
    return out.reshape(b * t, W_A)


def _ret_chunk_kernel(q_ref, k_ref, v_ref, y_ref, s_ref, st):
    c = pl.program_id(1)

    @pl.when(c == 0)
    def _():
        st[...] = jnp.zeros_like(st)

    q = q_ref[...]
    k = k_ref[...]
    vb = v_ref[...].astype(BF16)
    kb = k.astype(BF16)
    row = lax.broadcasted_iota(jnp.int32, (LC, W_B), 0)
    head = lax.broadcasted_iota(jnp.int32, (LC, W_B), 1) // HEAD_DIM
    log_g = [math.log(1.0 - 2.0 ** (-5.0 - h)) for h in range(H_B)]
    lg = jnp.full((LC, W_B), log_g[0], F32)
    for h in range(1, H_B):
        lg = jnp.where(head == h, log_g[h], lg)
    rowf = row.astype(F32)
    diff = (lax.broadcasted_iota(jnp.int32, (LC, LC), 0) - lax.broadcasted_iota(jnp.int32, (LC, LC), 1))
    causal = diff >= 0
    difff = jnp.maximum(diff, 0).astype(F32)

    out = jnp.dot((q * jnp.exp(lg * (rowf + 1.0))).astype(BF16), st[...].astype(BF16), preferred_element_type=F32)
    for h in range(H_B):
        qm = jnp.where(head == h, q, 0.0).astype(BF16)
        s = lax.dot_general(qm, kb, (((1,), (1,)), ((), ())), preferred_element_type=F32)
        p = jnp.where(causal, s * jnp.exp(log_g[h] * difff), 0.0).astype(BF16)
        out = out + jnp.where(head == h, jnp.dot(p, vb, preferred_element_type=F32), 0.0)
    y_ref[...] = out

    kt = (k * jnp.exp(lg * (LC - 1.0 - rowf))).T.astype(BF16)
    kv = jnp.dot(kt, vb, preferred_element_type=F32)
    r2 = lax.broadcasted_iota(jnp.int32, (W_B, W_B), 0) // HEAD_DIM
    c2 = lax.broadcasted_iota(jnp.int32, (W_B, W_B), 1) // HEAD_DIM
    cdec = jnp.full((W_B, W_B), math.exp(log_g[0] * LC), F32)
    for h in range(1, H_B):
        cdec = jnp.where(r2 == h, math.exp(log_g[h] * LC), cdec)
    st[...] = st[...] * cdec + jnp.where(r2 == c2, kv, 0.0)
    s_ref[0] = st[...]


def _ret_chunk(q, k, cb, n_seq, t_len):
    n_c = t_len // LC
    rows = lambda col: pl.BlockSpec((LC, W_B), lambda b, c, col=col: (b * n_c + c, col))
    return pl.pallas_call(
        _ret_chunk_kernel,
        grid=(n_seq, n_c),
        in_specs=[rows(0), rows(0), rows(2)],
        out_specs=[rows(0), pl.BlockSpec((1, W_B, W_B), lambda b, c: (b, 0, 0))],
        out_shape=[jax.ShapeDtypeStruct((n_seq * t_len, W_B), F32), jax.ShapeDtypeStruct((n_seq, W_B, W_B), F32)],
        scratch_shapes=[pltpu.VMEM((W_B, W_B), F32)],
        compiler_params=_params(("parallel", "arbitrary")),
        name="ret_chunk",
    )(q, k, cb)


def _ssd_chunk_kernel(x_ref, b_ref, c_ref, da_ref, ex_ref, y_ref, h_ref, st):
    ci = pl.program_id(1)

    @pl.when(ci == 0)
    def _():
        st[...] = jnp.zeros_like(st)

    gn = N_STATE
    half = W_C // N_GROUPS
    ex = ex_ref[...]
    x = x_ref[...]
    xb = x.astype(BF16)
    ri = lax.broadcasted_iota(jnp.int32, (LC, LC), 0)
    cj = lax.broadcasted_iota(jnp.int32, (LC, LC), 1)
    causal = ri >= cj
    cum = _dot_hi(causal.astype(F32), da_ref[...])
    cum_e = _dot_select(cum, ex, 3)
    last_e = cum_e[LC - 1:LC, :]
    cum_t = cum.T
    head = lax.broadcasted_iota(jnp.int32, (LC, W_C), 1) // HEAD_DIM
    bg = [b_ref[:, g * gn:(g + 1) * gn].astype(BF16) for g in range(N_GROUPS)]
    cg = [c_ref[:, g * gn:(g + 1) * gn].astype(BF16) for g in range(N_GROUPS)]
    cb = [lax.dot_general(cg[g], bg[g], (((1,), (1,)), ((), ())), preferred_element_type=F32)
          for g in range(N_GROUPS)]

    hb = st[...].astype(BF16)
    ys = [lax.dot_general(cg[g], hb, (((1,), (1,)), ((), ())), preferred_element_type=F32)
          for g in range(N_GROUPS)]
    y = jnp.where(head < H_C // N_GROUPS, ys[0], ys[1]) * jnp.exp(cum_e)
    for h in range(H_C):
        seg = cum[:, h:h + 1] - cum_t[h:h + 1, :]
        p = (cb[h // (H_C // N_GROUPS)] * jnp.exp(jnp.where(causal, seg, -jnp.inf))).astype(BF16)
        y = y + jnp.where(head == h, jnp.dot(p, xb, preferred_element_type=F32), 0.0)
    y_ref[...] = y

    xt = (x * jnp.exp(last_e - cum_e)).T.astype(BF16)
    upd = [jnp.dot(xt, bg[g], preferred_element_type=F32) for g in range(N_GROUPS)]
    rowi = lax.broadcasted_iota(jnp.int32, (W_C, gn), 0)
    sel = lax.broadcasted_iota(jnp.int32, (W_C, LANES), 1) == lax.broadcasted_iota(jnp.int32, (W_C, LANES), 0) // HEAD_DIM
    tot = jnp.sum(jnp.where(sel, cum[LC - 1:LC, :], 0.0), axis=1, keepdims=True)
    st[...] = st[...] * jnp.exp(tot) + jnp.where(rowi < half, upd[0], upd[1])
    h_ref[0] = st[...]


def _ssd_chunk(xdt, bm, cm, da, ex, n_seq, t_len):
    n_c = t_len // LC
    gn = N_GROUPS * N_STATE
    rows = lambda w: pl.BlockSpec((LC, w), lambda b, c: (b * n_c + c, 0))
    return pl.pallas_call(
        _ssd_chunk_kernel,
        grid=(n_seq, n_c),
        in_specs=[rows(W_C), rows(gn), rows(gn), rows(LANES), pl.BlockSpec((LANES, W_C), lambda b, c: (0, 0))],
        out_specs=[rows(W_C), pl.BlockSpec((1, W_C, N_STATE), lambda b, c: (b, 0, 0))],
        out_shape=[jax.ShapeDtypeStruct((n_seq * t_len, W_C), F32),
                   jax.ShapeDtypeStruct((n_seq, W_C, N_STATE), F32)],
        scratch_shapes=[pltpu.VMEM((W_C, N_STATE), F32)],
        compiler_params=_params(("parallel", "arbitrary")),
        name="ssd_chunk",
    )(xdt, bm, cm, da, ex)


def _post_kernel(x_ref, ya_ref, bonus_ref, ga_ref, ob_ref, gb_ref, yc_ref, xs_ref, z_ref, lnw_ref, lnb_ref,
                 rn_ref, dsk_ref, sn_ref, bda_ref, bdb_ref, wo_ref, out_ref):
    inv_hd = 1.0 / HEAD_DIM
    y = ya_ref[...]
    bda = bda_ref[...]
    mean = _dot_select(y, bda) * inv_hd
    yd = y - mean
    var = _dot_select(yd * yd, bda) * inv_hd
    ya = (yd * lax.rsqrt(var + GN_EPS) * lnw_ref[...] + lnb_ref[...] + bonus_ref[...]) * ga_ref[...]
    o = ob_ref[...]
    ms = _dot_select(o * o, bdb_ref[...]) * inv_hd
    gb = gb_ref[...]
    yb = (gb * _sigmoid(gb)) * (o * lax.rsqrt(ms + RMS_EPS) * rn_ref[...])
    z = z_ref[...]
    yc = (yc_ref[...] + dsk_ref[...] * xs_ref[...]) * (z * _sigmoid(z))
    yc = _rmsnorm(yc, sn_ref[...], GATED_NORM_EPS)
    out_ref[...] = (x_ref[...] + _dot(ya, wo_ref[0:W_A, :]) + _dot(yb, wo_ref[W_A:W_A + W_B, :])
                    + _dot(yc, wo_ref[W_A + W_B:, :]))


def _post(x, ya, bonus, ga, ob, cb, yc, xs, cc, p):
    n = x.shape[0]
    va = _full_spec((1, W_A))
    return pl.pallas_call(
        _post_kernel,
        grid=(n // TM,),
        in_specs=[_row_spec(D_MODEL), _row_spec(W_A), _row_spec(W_A), _row_spec(W_A), _row_spec(W_B),
                  _row_spec(W_B, 3), _row_spec(W_C), _row_spec(W_C), _row_spec(W_C, 0), va, va,
                  _full_spec((1, W_B)), va, va, _full_spec((W_A, W_A)), _full_spec((W_B, W_B)),
                  _full_spec((D_MODEL, D_MODEL))],
        out_specs=_row_spec(D_MODEL),
        out_shape=jax.ShapeDtypeStruct((n, D_MODEL), F32),
        compiler_params=_params(("parallel",)),
        name="post_outproj",
    )(x, ya, bonus, ga, ob, cb, yc, xs, cc, p["ln_w"], p["ln_b"], p["ret_norm"], p["d_skip"], p["ssm_norm"],
      p["bd_a"], p["bd_b"], p["w_out"])


def _ffn_kernel(*refs, gated, final_norm, n_e):
    refs = list(refs)
    x_ref, nf_ref = refs[:2]
    pos = 2
    if gated:
        router_ref = refs[pos]
        pos += 1
    wg_ref, wu_ref, wd_ref = refs[pos:pos + 3]
    pos += 3
    if final_norm:
        nfin_ref = refs[pos]
        pos += 1
    out_ref = refs[pos]
    hb_s, acc_s = refs[pos + 1:pos + 3]
    if gated:
        gate_s, asg_s, pos_s, asg_t, pos_t = refs[pos + 3:pos + 8]
    e = pl.program_id(1)

    @pl.when(e == 0)
    def _():
        h = _rmsnorm(x_ref[...], nf_ref[...], RMS_EPS)
        hb_s[...] = h.astype(BF16)
        acc_s[...] = jnp.zeros_like(acc_s)
        if gated:
            lane = lax.broadcasted_iota(jnp.int32, (TM, LANES), 1)
            logits = jnp.where(lane < N_EXPERTS, _dot_hi(h, router_ref[...]), -jnp.inf)
            p = jnp.exp(logits - jnp.max(logits, axis=-1, keepdims=True))
            p = p / jnp.sum(p, axis=-1, keepdims=True)
            p1 = jnp.max(p, axis=-1, keepdims=True)
            i1 = jnp.min(jnp.where(p == p1, lane, LANES), axis=-1, keepdims=True)
            rest = jnp.where(lane == i1, -1.0, p)
            p2 = jnp.max(rest, axis=-1, keepdims=True)
            i2 = jnp.min(jnp.where(rest == p2, lane, LANES), axis=-1, keepdims=True)
            gate_s[...] = jnp.where(lane == i1, p1, jnp.where(lane == i2, p2, 0.0)) / (p1 + p2)
            assigned = jnp.where((lane == i1) | (lane == i2), 1.0, 0.0)
            earlier = (lax.broadcasted_iota(jnp.int32, (TM, TM), 0) > lax.broadcasted_iota(jnp.int32, (TM, TM), 1))
            rank = jnp.dot(earlier.astype(BF16), assigned.astype(BF16), preferred_element_type=F32)
            asg_s[...] = assigned
            pos_s[...] = rank
            asg_t[...] = assigned.T
            pos_t[...] = rank.T

    if not gated:
        hb = hb_s[...]
        g = jnp.dot(hb, wg_ref[0], preferred_element_type=F32)
        u = jnp.dot(hb, wu_ref[0], preferred_element_type=F32)
        acc_s[...] += _dot(g * _sigmoid(g) * u, wd_ref[0])
    else:
        lane = lax.broadcasted_iota(jnp.int32, (TM, LANES), 1)
        mine = lane == e
        col = lambda ref: jnp.sum(jnp.where(mine, ref[...], 0.0), axis=-1, keepdims=True)
        gate_c, asg_c, pos_c = col(gate_s), col(asg_s), col(pos_s)
        asg_r, pos_r = asg_t[pl.ds(e, 1), :], pos_t[pl.ds(e, 1), :]
        n_chunks = (jnp.sum(asg_r).astype(jnp.int32) + MOE_CHUNK - 1) // MOE_CHUNK

        def chunk(c, carry):
            base = (c * MOE_CHUNK).astype(F32)
            slot_r = lax.broadcasted_iota(jnp.int32, (MOE_CHUNK, TM), 0).astype(F32) + base
            pick = jnp.where((pos_r == slot_r) & (asg_r > 0.0), 1.0, 0.0).astype(BF16)
            rows = jnp.dot(pick, hb_s[...], preferred_element_type=F32).astype(BF16)
            g = jnp.dot(rows, wg_ref[0], preferred_element_type=F32)
            u = jnp.dot(rows, wu_ref[0], preferred_element_type=F32)
            o = _dot(g * _sigmoid(g) * u, wd_ref[0])
            o_hi = o.astype(BF16)
            o_lo = (o - o_hi.astype(F32)).astype(BF16)
            slot_c = lax.broadcasted_iota(jnp.int32, (TM, MOE_CHUNK), 1).astype(F32) + base
            place = jnp.where((pos_c == slot_c) & (asg_c > 0.0), 1.0, 0.0).astype(BF16)
            back = (jnp.dot(place, o_hi, preferred_element_type=F32)
                    + jnp.dot(place, o_lo, preferred_element_type=F32))
            acc_s[...] += gate_c * back
            return carry

        lax.fori_loop(0, n_chunks, chunk, 0)

    @pl.when(e == n_e - 1)
    def _():
        y = x_ref[...] + acc_s[...]
        if final_norm:
            y = _rmsnorm(y, nfin_ref[...], RMS_EPS)
        out_ref[...] = y


def _ffn(x, nf, wg, wu, wd, router=None, nfin=None):
    n = x.shape[0]
    n_e, _, dff = wg.shape
    gated = router is not None
    final_norm = nfin is not None
    vec = pl.BlockSpec((1, D_MODEL), lambda i, e: (0, 0))
    args = [x, nf]
    specs = [pl.BlockSpec((TM, D_MODEL), lambda i, e: (i, 0)), vec]
    if gated:
        args.append(router)
        specs.append(pl.BlockSpec((D_MODEL, LANES), lambda i, e: (0, 0)))
    args += [wg, wu, wd]
    specs += [pl.BlockSpec((1, D_MODEL, dff), lambda i, e: (e, 0, 0)),
              pl.BlockSpec((1, D_MODEL, dff), lambda i, e: (e, 0, 0)),
              pl.BlockSpec((1, dff, D_MODEL), lambda i, e: (e, 0, 0))]
    if final_norm:
        args.append(nfin)
        specs.append(vec)
    scratch = [pltpu.VMEM((TM, D_MODEL), BF16), pltpu.VMEM((TM, D_MODEL), F32)]
    if gated:
        scratch += [pltpu.VMEM((TM, LANES), F32)] * 3 + [pltpu.VMEM((LANES, TM), F32)] * 2
    return pl.pallas_call(
        functools.partial(_ffn_kernel, gated=gated, final_norm=final_norm, n_e=n_e),
        grid=(n // TM, n_e),
        in_specs=specs,
        out_specs=pl.BlockSpec((TM, D_MODEL), lambda i, e: (i, 0)),
        out_shape=jax.ShapeDtypeStruct((n, D_MODEL), F32),
        scratch_shapes=scratch,
        compiler_params=_params(("parallel", "arbitrary")),
        name="moe" if gated else "ffn",
    )(*args)


def _short_to_lanes_kernel(*refs, n_ops, t_len):
    for x_ref, o_ref in zip(refs[:n_ops], refs[n_ops:]):
        for t in range(t_len):
            o_ref[t] = x_ref[pl.ds(t, LANES, stride=t_len), :].T


def _short_to_lanes(xs, t_len):
    n_chunks = [width // LANES for _, _, width in xs]
    grid = max(n_chunks)
    clamp = lambda c, n: jnp.minimum(c, n - 1)
    outs = pl.pallas_call(
        functools.partial(_short_to_lanes_kernel, n_ops=len(xs), t_len=t_len),
        grid=(grid,),
        in_specs=[pl.BlockSpec((LANES * t_len, LANES), lambda c, c0=col // LANES, n=n: (0, c0 + clamp(c, n)))
                  for (_, col, _), n in zip(xs, n_chunks)],
        out_specs=[pl.BlockSpec((t_len, LANES, LANES), lambda c, n=n: (0, clamp(c, n), 0)) for n in n_chunks],
        out_shape=[jax.ShapeDtypeStruct((t_len, width, LANES), F32) for _, _, width in xs],
        compiler_params=_params(("arbitrary",)),
        name="short_to_lanes",
    )(*[x for x, _, _ in xs])
    return outs


def _short_from_lanes_kernel(y_ref, o_ref, *, t_len):
    for t in range(t_len):
        o_ref[pl.ds(t, LANES, stride=t_len), :] = y_ref[t].T


def _short_from_lanes(y):
    t_len, w, _ = y.shape
    return pl.pallas_call(
        functools.partial(_short_from_lanes_kernel, t_len=t_len),
        grid=(w // LANES,),
        in_specs=[pl.BlockSpec((t_len, LANES, LANES), lambda c: (0, c, 0))],
        out_specs=pl.BlockSpec((LANES * t_len, LANES), lambda c: (0, c)),
        out_shape=jax.ShapeDtypeStruct((LANES * t_len, w), F32),
        compiler_params=_params(("parallel",)),
        name="short_from_lanes",
    )(y)


def _state_to_lanes_kernel(x_ref, o_ref, *, ni, nj, j_first):
    xt = x_ref[...].T
    if j_first:
        for j in range(nj):
            o_ref[pl.ds(j, ni, stride=nj), :] = xt[j * ni:(j + 1) * ni, :]
    else:
        o_ref[...] = xt


def _state_to_lanes(s, ni, nj, j_first):
    h = s.shape[1]
    out = pl.pallas_call(
        functools.partial(_state_to_lanes_kernel, ni=ni, nj=nj, j_first=j_first),
        grid=(h,),
        in_specs=[pl.BlockSpec((LANES, ni * nj), lambda i: (0, i))],
        out_specs=pl.BlockSpec((ni * nj, LANES), lambda i: (i, 0)),
        out_shape=jax.ShapeDtypeStruct((h * ni * nj, LANES), F32),
        compiler_params=_params(("parallel",)),
        name="state_to_lanes",
    )(s.reshape(LANES, h * ni * nj))
    return out.reshape(h * ni, nj, LANES)


def _state_from_lanes_kernel(s_ref, o_ref, tmp, *, ni, nj, j_first):
    if j_first:
        for j in range(nj):
            tmp[j * ni:(j + 1) * ni, :] = s_ref[pl.ds(j, ni, stride=nj), :]
        o_ref[...] = tmp[...].T
    else:
        o_ref[...] = s_ref[...].T


def _state_from_lanes(s, n_heads, ni, nj, j_first):
    out = pl.pallas_call(
        functools.partial(_state_from_lanes_kernel, ni=ni, nj=nj, j_first=j_first),
        grid=(n_heads,),
        in_specs=[pl.BlockSpec((ni * nj, LANES), lambda i: (i, 0))],
        out_specs=pl.BlockSpec((LANES, ni * nj), lambda i: (0, i)),
        out_shape=jax.ShapeDtypeStruct((LANES, n_heads * ni * nj), F32),
        scratch_shapes=[pltpu.VMEM((ni * nj, LANES), F32)],
        compiler_params=_params(("parallel",)),
        name="state_from_lanes",
    )(s.reshape(n_heads * ni * nj, LANES))
    return out.reshape((LANES, n_heads, nj, ni) if j_first else (LANES, n_heads, ni, nj))


def _rwkv_long(feats, b):
    r, dec, k2, v, nkk, beta = feats
    t = r.shape[0] // b
    kl = HEAD_DIM // K_SPLIT
    assert b * H_PAD == HALF and t % TR == 0
    nkk_next = jnp.concatenate([nkk[1:], jnp.zeros((1, nkk.shape[1]), F32)], axis=0)
    q, d, a, n1, a2 = _long_to_lanes([r, dec, k2, nkk_next, beta], b, kl, True)
    (vv,) = _long_to_lanes([v], b, HEAD_DIM, False)
    y, s = _rwkv_long_scan(q, d, a, vv, n1, a2)
    s = s.reshape(kl, HEAD_DIM, K_SPLIT, b, H_PAD)[..., :H_A]
    return _long_from_lanes(y, b), jnp.transpose(s, (3, 4, 1, 2, 0)).reshape(b, H_A, HEAD_DIM, HEAD_DIM)


def _rwkv_short(feats, t_len, s0):
    r, dec, k2, v, nkk, beta = _short_to_lanes([(f, 0, W_A) for f in feats], t_len)
    hd = HEAD_DIM
    own = lambda h: h
    y, s = _scan("rwkv", H_A, hd, hd, (r, own), (dec, own), (k2, own), (v, own),
                 _state_to_lanes(s0, hd, hd, True), n=(nkk, own), a2=(beta, own))
    return _short_from_lanes(y), _state_from_lanes(s, H_A, hd, hd, True)


def _ret_short(qh, kh, cb, t_len, s0):
    q, k, v = _short_to_lanes([(qh, 0, W_B), (kh, 0, W_B), (cb, 2 * W_B, W_B)], t_len)
    hd = HEAD_DIM
    gam = np.ones((SUBLANES, LANES), np.float32)
    gam[:H_B] = (1.0 - 2.0 ** (-5.0 - np.arange(H_B, dtype=np.float64)))[:, None]
    own = lambda h: h
    y, s = _scan("ret", H_B, hd, hd, (q, own), jnp.asarray(gam), (k, own), (v, own),
                 _state_to_lanes(s0, hd, hd, False))
    return _short_from_lanes(y), _state_from_lanes(s, H_B, hd, hd, False)


def _ssd_short(xdt, bm, cm, dssm, t_len, s0):
    gn = N_GROUPS * N_STATE
    x, bl, cl, dl = _short_to_lanes([(xdt, 0, W_C), (bm, 0, gn), (cm, 0, gn), (dssm, 0, LANES)], t_len)
    group = lambda h: h // (H_C // N_GROUPS)
    y, s = _scan("ssd", H_C, N_STATE, HEAD_DIM, (cl, group), dl, (bl, group), (x, lambda h: h),
                 _state_to_lanes(s0, N_STATE, HEAD_DIM, True))
    return _short_from_lanes(y), _state_from_lanes(s, H_C, N_STATE, HEAD_DIM, True)


def _block_diag_ones(width):
    idx = np.arange(width) // HEAD_DIM
    return jnp.asarray((idx[:, None] == idx[None, :]).astype(np.float32), BF16)


def _shifted(x, prev_rows, shift):
    b, t, c = x.shape
    p = prev_rows.shape[1]
    full = jnp.concatenate([prev_rows, x], axis=1)
    return full[:, p - shift:p - shift + t].reshape(b * t, c)


def _layer(x, b, p, st, rope, table_block):
    n = x.shape[0]
    t = n // b
    fresh = st is None
    tiles_per_seq = t // TM if fresh else 0
    ca, cb, cc = _in_proj(x, p["norm_mix"], p["w_in"])
    ca3 = ca.reshape(b, t, COLS_A)
    xbc_tail = cc.reshape(b, t, COLS_C_PAD)[:, -(CONV_W - 1):, W_C:W_C + CONV_DIM]
    if fresh:
        prev, shifted = None, None
        conv_new = xbc_tail
    else:
        prev = _shifted(ca3, st["shift"][:, None, :], 1)
        xbc = cc[:, W_C:W_C + CONV_DIM].reshape(b, t, CONV_DIM)
        shifted = [_shifted(xbc, st["conv"], j) for j in range(1, CONV_W)]
        conv_new = jnp.concatenate([st["conv"], xbc_tail], axis=1)[:, -(CONV_W - 1):]

    r, dec, k2, v, nkk, beta, ga, bonus = _rwkv_prep(ca, prev, p, tiles_per_seq)
    qh, kh = _ret_prep(cb, rope[0], rope[1], table_block)
    xdt, bm, cm, dssm, xs, da = _ssm_prep(cc, shifted, p, tiles_per_seq)

    feats = (r, dec, k2, v, nkk, beta)
    if fresh:
        ya, s_rwkv = _rwkv_long(feats, b)
        yb, s_full = _ret_chunk(qh, kh, cb, b, t)
        s_ret = jnp.stack([s_full[:, h * HEAD_DIM:(h + 1) * HEAD_DIM, h * HEAD_DIM:(h + 1) * HEAD_DIM]
                           for h in range(H_B)], axis=1)
        yc, s_ssm = _ssd_chunk(xdt, bm, cm, da, p["expand"], b, t)
        s_ssm = s_ssm.reshape(b, H_C, HEAD_DIM, N_STATE)
    else:
        assert b == LANES
        ya, s_rwkv = _rwkv_short(feats, t, st["rwkv"])
        yb, s_ret = _ret_short(qh, kh, cb, t, st["ret"])
        yc, s_ssm = _ssd_short(xdt, bm, cm, dssm, t, st["ssm"])

    x2 = _post(x, ya, bonus, ga, yb, cb, yc, xs, cc, p)
    x = _ffn(x2, p["norm_ffn"], p["wg"], p["wu"], p["wd"], router=p["router"], nfin=p["norm_final"])
    return x, (s_rwkv, ca3[:, -1], s_ret, s_ssm, conv_new)


def kernel(x_prompt, x_sample, state_rwkv, state_shift, state_ret, state_ssm, state_conv, norm_mix, w_in, rwkv_mu, rwkv_w0, rwkv_w_up, rwkv_a0, rwkv_a_up, rwkv_g_up, rwkv_k_k, rwkv_k_a, rwkv_r_k, rwkv_ln_w, rwkv_ln_b, ret_norm, ssm_conv_w, ssm_conv_b, ssm_dt_bias, ssm_a_log, ssm_d, ssm_norm, w_out, norm_ffn, ffn_w_gate, ffn_w_up, ffn_w_down, moe_router, moe_w_gate, moe_w_up, moe_w_down, norm_final):
    bp, tp, _ = x_prompt.shape
    bs, ts, _ = x_sample.shape
    depth = w_in.shape[0]
    assert tp % TM == 0 and tp % LC == 0 and (bs * ts) % TM == 0 and TM % ts == 0

    rope = _rope_tables(tp, ts)
    bd_a, bd_b = _block_diag_ones(W_A), _block_diag_ones(W_B)
    expand = np.zeros((LANES, W_C), np.float32)
    expand[np.arange(W_C) // HEAD_DIM, np.arange(W_C)] = 1.0
    expand = jnp.asarray(expand, BF16)
    row = lambda v: v.reshape(1, -1)
    pad_l = lambda v: jnp.pad(v, (0, LANES - v.shape[0])).reshape(1, LANES)

    xp = x_prompt.reshape(bp * tp, D_MODEL)
    xs = x_sample.reshape(bs * ts, D_MODEL)
    new_p, new_s = [], []
    for i in range(depth):
        j = i // 2
        p = dict(
            norm_mix=row(norm_mix[i]),
            w_in=jnp.pad(w_in[i], ((0, 0), (0, COLS_C_PAD - COLS_C))).astype(BF16),
            mu=row(rwkv_mu[i]), w0=row(rwkv_w0[i]), a0=row(rwkv_a0[i]), k_k=row(rwkv_k_k[i]),
            k_a=row(rwkv_k_a[i]), r_k=row(rwkv_r_k[i]),
            w_up=jnp.pad(rwkv_w_up[i], ((0, AAA_LORA), (0, 0))).astype(BF16),
            a_up=jnp.pad(rwkv_a_up[i], ((DECAY_LORA, 0), (0, 0))).astype(BF16),
            g_up=rwkv_g_up[i].astype(BF16), bd_a=bd_a, bd_b=bd_b,
            conv_w=ssm_conv_w[i], conv_b=row(ssm_conv_b[i]), dt_bias=pad_l(ssm_dt_bias[i]),
            a_log=pad_l(ssm_a_log[i]), expand=expand,
            ln_w=row(rwkv_ln_w[i]), ln_b=row(rwkv_ln_b[i]), ret_norm=row(ret_norm[i]),
            d_skip=row(jnp.repeat(ssm_d[i], HEAD_DIM)), ssm_norm=row(ssm_norm[i]), w_out=w_out[i].astype(BF16),
            norm_ffn=row(norm_ffn[i]), norm_final=row(norm_final) if i == depth - 1 else None)
        if i % 2 == 0:
            dff = ffn_w_gate.shape[-1] // 2
            p.update(router=None,
                     wg=ffn_w_gate[j].reshape(D_MODEL, 2, dff).transpose(1, 0, 2).astype(BF16),
                     wu=ffn_w_up[j].reshape(D_MODEL, 2, dff).transpose(1, 0, 2).astype(BF16),
                     wd=ffn_w_down[j].reshape(2, dff, D_MODEL).astype(BF16))
        else:
            p.update(router=jnp.pad(moe_router[j], ((0, 0), (0, LANES - N_EXPERTS))),
                     wg=moe_w_gate[j].astype(BF16), wu=moe_w_up[j].astype(BF16), wd=moe_w_down[j].astype(BF16))

        xp, st_p = _layer(xp, bp, p, None, rope, lambda t: t % (tp // TM))
        st = dict(rwkv=state_rwkv[i], shift=state_shift[i], ret=state_ret[i], ssm=state_ssm[i],
                  conv=state_conv[i])
        xs, st_s = _layer(xs, bs, p, st, rope, lambda t: tp // TM)
        new_p.append(st_p)
        new_s.append(st_s)

    stack = lambda sts: tuple(jnp.stack(s) for s in zip(*sts))
    return (xp.reshape(bp, tp, D_MODEL), xs.reshape(bs, ts, D_MODEL)) + stack(new_p) + stack(new_s)
```

```python
import functools
import math

import numpy as np
import jax
import jax.numpy as jnp
from jax import lax
from jax.experimental import pallas as pl
from jax.experimental.pallas import tpu as pltpu

F32 = jnp.float32
BF16 = jnp.bfloat16
HIGHEST = lax.Precision.HIGHEST

LANES = 128
SUBLANES = 8
VMEM_LIMIT = 56 * 1024 * 1024

D_MODEL = 1024
HEAD_DIM = 64
H_A, H_B, H_C = 6, 4, 6
W_A, W_B, W_C = H_A * HEAD_DIM, H_B * HEAD_DIM, H_C * HEAD_DIM
DECAY_LORA, AAA_LORA, GATE_LORA = 64, 64, 128
COLS_A = 3 * W_A + DECAY_LORA + AAA_LORA + GATE_LORA
COLS_B = 4 * W_B
N_STATE, N_GROUPS, CONV_W = 128, 2, 4
CONV_DIM = W_C + 2 * N_GROUPS * N_STATE
COLS_C = W_C + CONV_DIM + H_C
COLS_C_PAD = 1408
ROPE_BASE = 10000.0
RMS_EPS = 1e-6
GN_EPS = 64e-5
GATED_NORM_EPS = 1e-5
N_EXPERTS = 8
PAST_LEN = 16384

TM = 512
MOE_CHUNK = 160
MOE_SUB = 2
LC = 256
H_PAD = 8
K_SPLIT = 2


def _dot(a, b):
    return jnp.dot(a.astype(BF16), b.astype(BF16), preferred_element_type=F32)


def _dot_hi(a, b):
    return jnp.dot(a, b, precision=HIGHEST, preferred_element_type=F32)


def _dot_select(a, sel, terms=2):
    out = None
    rest = a
    for _ in range(terms):
        piece = rest.astype(BF16)
        part = jnp.dot(piece, sel, preferred_element_type=F32)
        out = part if out is None else out + part
        rest = rest - piece.astype(F32)
    return out


def _sigmoid(x):
    return 1.0 / (1.0 + jnp.exp(-x))


def _softplus(x):
    return jnp.maximum(x, 0.0) + jnp.log1p(jnp.exp(-jnp.abs(x)))


def _rmsnorm(x, g, eps):
    return x * lax.rsqrt(jnp.mean(x * x, axis=-1, keepdims=True) + eps) * g


def _params(sem):
    return pltpu.CompilerParams(dimension_semantics=sem, vmem_limit_bytes=VMEM_LIMIT)


def _row_spec(width, col=0):
    return pl.BlockSpec((TM, width), lambda i, c=col: (i, c))


def _before_spec(width):
    return pl.BlockSpec((SUBLANES, width), lambda i: (jnp.maximum(i * (TM // SUBLANES) - 1, 0), 0))


def _full_spec(shape):
    nd = len(shape)
    return pl.BlockSpec(shape, lambda i, n=nd: (0,) * n)


def _shift_rows(x, before, j, first):
    rolled = pltpu.roll(x, j, 0)
    prev = jnp.where(first, 0.0, pltpu.roll(before, j, 0))
    row = lax.broadcasted_iota(jnp.int32, prev.shape, 0)
    top = jnp.where(row < j, prev, rolled[0:SUBLANES])
    return jnp.concatenate([top, rolled[SUBLANES:]], axis=0)


def _in_proj_kernel(x_ref, g_ref, w_ref, oa_ref, ob_ref, oc_ref):
    h = _rmsnorm(x_ref[...], g_ref[...], RMS_EPS).astype(BF16)
    oa_ref[...] = jnp.dot(h, w_ref[:, 0:COLS_A], preferred_element_type=F32)
    ob_ref[...] = jnp.dot(h, w_ref[:, COLS_A:COLS_A + COLS_B], preferred_element_type=F32)
    oc_ref[...] = jnp.dot(h, w_ref[:, COLS_A + COLS_B:], preferred_element_type=F32)


def _in_proj(x, g, w):
    n = x.shape[0]
    wtot = w.shape[1]
    return pl.pallas_call(
        _in_proj_kernel,
        grid=(n // TM,),
        in_specs=[_row_spec(D_MODEL), _full_spec((1, D_MODEL)), _full_spec((D_MODEL, wtot))],
        out_specs=[_row_spec(COLS_A), _row_spec(COLS_B), _row_spec(COLS_C_PAD)],
        out_shape=[jax.ShapeDtypeStruct((n, COLS_A), F32), jax.ShapeDtypeStruct((n, COLS_B), F32),
                   jax.ShapeDtypeStruct((n, COLS_C_PAD), F32)],
        compiler_params=_params(("parallel",)),
        name="in_proj",
    )(x, g, w)


def _rwkv_prep_kernel(c_ref, p_ref, mu_ref, w0_ref, a0_ref, kk_ref, ka_ref, rk_ref, wup_ref, aup_ref,
                      gup_ref, bd_ref, r_o, d_o, k_o, v_o, n_o, b_o, g_o, bonus_o, *, tiles_per_seq):
    c = c_ref[...]
    if tiles_per_seq:
        prev = _shift_rows(c, p_ref[...], 1, pl.program_id(0) % tiles_per_seq == 0)
    else:
        prev = p_ref[...]
    xm = c + (prev - c) * mu_ref[...]
    r = xm[:, 0:W_A]
    k = xm[:, W_A:2 * W_A]
    v = xm[:, 2 * W_A:3 * W_A]
    lora = xm[:, 3 * W_A:3 * W_A + DECAY_LORA + AAA_LORA]
    gd = xm[:, 3 * W_A + DECAY_LORA + AAA_LORA:]
    w = w0_ref[...] + _dot(jnp.tanh(lora), wup_ref[...])
    decay = jnp.exp(-math.exp(-0.5) * _sigmoid(w))
    a = _sigmoid(a0_ref[...] + _dot(lora, aup_ref[...]))
    g = _dot(_sigmoid(gd), gup_ref[...])
    bd = bd_ref[...]
    kk = k * kk_ref[...]
    kk = kk / jnp.maximum(jnp.sqrt(_dot_select(kk * kk, bd)), 1e-12)
    k2 = k * (1.0 + (a - 1.0) * ka_ref[...])
    for o_ref, val in ((r_o, r), (d_o, decay), (k_o, k2), (v_o, v), (n_o, -kk), (b_o, kk * a)):
        o_ref[:, 0:W_A] = val
        if o_ref.shape[1] > W_A:
            o_ref[:, W_A:] = jnp.zeros((TM, o_ref.shape[1] - W_A), F32)
    g_o[...] = g
    bonus_o[...] = _dot_select(r * k2 * rk_ref[...], bd) * v


def _rwkv_prep(ca, prev, p, tiles_per_seq):
    n = ca.shape[0]
    vec = _full_spec((1, W_A))
    lora_spec = _full_spec((DECAY_LORA + AAA_LORA, W_A))
    p_spec = _before_spec(COLS_A) if tiles_per_seq else _row_spec(COLS_A)
    w_scan = H_PAD * HEAD_DIM if tiles_per_seq else W_A
    widths = [w_scan] * 6 + [W_A] * 2
    return pl.pallas_call(
        functools.partial(_rwkv_prep_kernel, tiles_per_seq=tiles_per_seq),
        grid=(n // TM,),
        in_specs=[_row_spec(COLS_A), p_spec, _full_spec((1, COLS_A)), vec, vec, vec, vec, vec,
                  lora_spec, lora_spec, _full_spec((GATE_LORA, W_A)), _full_spec((W_A, W_A))],
        out_specs=[_row_spec(w) for w in widths],
        out_shape=[jax.ShapeDtypeStruct((n, w), F32) for w in widths],
        compiler_params=_params(("parallel",)),
        name="rwkv_prep",
    )(ca, ca if tiles_per_seq else prev, p["mu"], p["w0"], p["a0"], p["k_k"], p["k_a"], p["r_k"],
      p["w_up"], p["a_up"], p["g_up"], p["bd_a"])


def _rope_kernel(ang_ref, cos_o, sin_o):
    ang = ang_ref[...]
    lane = lax.broadcasted_iota(jnp.int32, ang.shape, 1)
    cos_o[...] = jnp.cos(ang)
    sin_o[...] = jnp.where((lane % HEAD_DIM) < (HEAD_DIM // 2), -jnp.sin(ang), jnp.sin(ang))


def _rope_tables(t_prompt, t_sample):
    theta = 1.0 / (ROPE_BASE ** jnp.linspace(0.0, 1.0, HEAD_DIM // 2, dtype=F32))
    pos = jnp.concatenate([jnp.arange(t_prompt, dtype=F32), PAST_LEN + (jnp.arange(TM) % t_sample).astype(F32)])
    ang = jnp.tile(pos[:, None] * theta[None, :], (1, W_B // (HEAD_DIM // 2)))
    n = ang.shape[0]
    return pl.pallas_call(
        _rope_kernel,
        grid=(n // TM,),
        in_specs=[_row_spec(W_B)],
        out_specs=[_row_spec(W_B)] * 2,
        out_shape=[jax.ShapeDtypeStruct((n, W_B), F32)] * 2,
        compiler_params=_params(("parallel",)),
        name="rope_tables",
    )(ang)


def _ret_prep_kernel(q_ref, k_ref, cos_ref, sin_ref, q_o, k_o):
    cos = cos_ref[...]
    sin = sin_ref[...]
    lane = lax.broadcasted_iota(jnp.int32, (TM, W_B), 1)
    first_half = (lane % HEAD_DIM) < (HEAD_DIM // 2)

    def rope(x):
        partner = jnp.where(first_half, pltpu.roll(x, W_B - HEAD_DIM // 2, 1), pltpu.roll(x, HEAD_DIM // 2, 1))
        return x * cos + partner * sin

    q_o[...] = rope(q_ref[...])
    k_o[...] = rope(k_ref[...]) * (HEAD_DIM ** -0.5)


def _ret_prep(cb, cos, sin, table_block):
    n = cb.shape[0]
    tab = pl.BlockSpec((TM, W_B), lambda i: (table_block(i), 0))
    return pl.pallas_call(
        _ret_prep_kernel,
        grid=(n // TM,),
        in_specs=[_row_spec(W_B, 0), _row_spec(W_B, 1), tab, tab],
        out_specs=[_row_spec(W_B)] * 2,
        out_shape=[jax.ShapeDtypeStruct((n, W_B), F32)] * 2,
        compiler_params=_params(("parallel",)),
        name="ret_prep",
    )(cb, cb, cos, sin)


def _ssm_prep_kernel(*refs, tiles_per_seq):
    n_shift_refs = 1 if tiles_per_seq else CONV_W - 1
    cc_ref = refs[0]
    shift_refs = refs[1:1 + n_shift_refs]
    cw_ref, cb_ref, dtb_ref, alog_ref, ex_ref, xdt_o, b_o, c_o, dec_o, xs_o, da_o = refs[1 + n_shift_refs:]
    cc = cc_ref[...]
    x0 = cc[:, W_C:W_C + CONV_DIM]
    if tiles_per_seq:
        before = shift_refs[0][:, W_C:W_C + CONV_DIM]
        first = pl.program_id(0) % tiles_per_seq == 0
        xs_prev = [_shift_rows(x0, before, j, first) for j in range(1, CONV_W)]
    else:
        xs_prev = [r[...] for r in shift_refs]
    cw = cw_ref[...]
    conv = x0 * cw[CONV_W - 1:CONV_W, :] + cb_ref[...]
    for j in range(1, CONV_W):
        conv = conv + xs_prev[j - 1] * cw[CONV_W - 1 - j:CONV_W - j, :]
    act = conv * _sigmoid(conv)
    xs = act[:, 0:W_C]
    dt = _softplus(cc[:, W_C + CONV_DIM:] + dtb_ref[...])
    da = dt * (-jnp.exp(alog_ref[...]))
    da_o[...] = da
    dec_o[...] = jnp.exp(da)
    xdt_o[...] = xs * _dot_select(dt, ex_ref[...])
    b_o[...] = act[:, W_C:W_C + N_GROUPS * N_STATE]
    c_o[...] = act[:, W_C + N_GROUPS * N_STATE:]
    xs_o[...] = xs


def _ssm_prep(cc, shifted, p, tiles_per_seq):
    n = cc.shape[0]
    gn = N_GROUPS * N_STATE
    if tiles_per_seq:
        shift_args, shift_specs = [cc], [_before_spec(COLS_C_PAD)]
    else:
        shift_args, shift_specs = list(shifted), [_row_spec(CONV_DIM)] * (CONV_W - 1)
    return pl.pallas_call(
        functools.partial(_ssm_prep_kernel, tiles_per_seq=tiles_per_seq),
        grid=(n // TM,),
        in_specs=[_row_spec(COLS_C_PAD)] + shift_specs + [
            _full_spec((CONV_W, CONV_DIM)), _full_spec((1, CONV_DIM)), _full_spec((1, LANES)),
            _full_spec((1, LANES)), _full_spec((LANES, W_C))],
        out_specs=[_row_spec(W_C), _row_spec(gn), _row_spec(gn), _row_spec(LANES), _row_spec(W_C),
                   _row_spec(LANES)],
        out_shape=[jax.ShapeDtypeStruct((n, W_C), F32), jax.ShapeDtypeStruct((n, gn), F32),
                   jax.ShapeDtypeStruct((n, gn), F32), jax.ShapeDtypeStruct((n, LANES), F32),
                   jax.ShapeDtypeStruct((n, W_C), F32), jax.ShapeDtypeStruct((n, LANES), F32)],
        compiler_params=_params(("parallel",)),
        name="ssm_prep",
    )(cc, *shift_args, p["conv_w"], p["conv_b"], p["dt_bias"], p["a_log"], p["expand"])


def _scan_kernel(*refs, mode, ni, tt_len, n_tt):
    if mode == "rwkv":
        q_ref, d_ref, a_ref, b_ref, n_ref, a2_ref, s0_ref, y_ref, so_ref, st = refs
    else:
        q_ref, d_ref, a_ref, b_ref, s0_ref, y_ref, so_ref, st = refs
    head = pl.program_id(0)
    tt = pl.program_id(1)

    @pl.when(tt == 0)
    def _():
        st[...] = s0_ref[...]

    def row(ref, t, i):
        return ref[t, pl.ds(i, 1), :]

    def step(t, carry):
        bv = b_ref[t]
        if mode == "rwkv":
            sa_parts = [jnp.zeros_like(bv), jnp.zeros_like(bv)]
            for i in range(ni):
                sa_parts[i % 2] = sa_parts[i % 2] + st[i] * row(n_ref, t, i)
            sa = sa_parts[0] + sa_parts[1]
        elif mode == "ssd":
            d = d_ref[t, pl.ds(head, 1), :]
        else:
            d = d_ref[pl.ds(head, 1), :]
        y_parts = [jnp.zeros_like(bv), jnp.zeros_like(bv)]
        for i in range(ni):
            if mode == "rwkv":
                s = st[i] * row(d_ref, t, i) + row(a_ref, t, i) * bv + row(a2_ref, t, i) * sa
            else:
                s = st[i] * d + row(a_ref, t, i) * bv
            st[i] = s
            y_parts[i % 2] = y_parts[i % 2] + s * row(q_ref, t, i)
        y_ref[t] = y_parts[0] + y_parts[1]
        return carry

    lax.fori_loop(0, tt_len, step, 0)

    @pl.when(tt == n_tt - 1)
    def _():
        so_ref[...] = st[...]


def _scan(mode, n_heads, ni, nj, q, d, a, b, s0, n=None, a2=None):
    t_len = b[0].shape[0]

    def rows(op, width):
        arr, block_of = op
        return arr, pl.BlockSpec((t_len, width, LANES), lambda h, t, f=block_of: (0, f(h), 0))

    ops = [rows(q, ni)]
    if mode == "rwkv":
        ops.append(rows(d, ni))
    elif mode == "ssd":
        ops.append((d, pl.BlockSpec((t_len, SUBLANES, LANES), lambda h, t: (0, 0, 0))))
    else:
        ops.append((d, pl.BlockSpec((SUBLANES, LANES), lambda h, t: (0, 0))))
    ops += [rows(a, ni), rows(b, nj)]
    if mode == "rwkv":
        ops += [rows(n, ni), rows(a2, ni)]
    s_spec = pl.BlockSpec((ni, nj, LANES), lambda h, t: (h, 0, 0))
    y_spec = pl.BlockSpec((t_len, nj, LANES), lambda h, t: (0, h, 0))
    return pl.pallas_call(
        functools.partial(_scan_kernel, mode=mode, ni=ni, tt_len=t_len, n_tt=1),
        grid=(n_heads, 1),
        in_specs=[spec for _, spec in ops] + [s_spec],
        out_specs=[y_spec, s_spec],
        out_shape=[jax.ShapeDtypeStruct((t_len, n_heads * nj, LANES), F32),
                   jax.ShapeDtypeStruct((n_heads * ni, nj, LANES), F32)],
        scratch_shapes=[pltpu.VMEM((ni, nj, LANES), F32)],
        compiler_params=_params(("parallel", "arbitrary")),
        name="scan_" + mode,
    )(*[arr for arr, _ in ops], s0)


def _rwkv_long_kernel(q_ref, d_ref, a_ref, b_ref, n1_ref, a2_ref, y_ref, so_ref, st, sa_s, dots_s, rows_s, *,
                      ni, tt_len, n_tt):
    tt = pl.program_id(1)

    @pl.when(tt == 0)
    def _():
        st[...] = jnp.zeros_like(st)
        sa_s[...] = jnp.zeros_like(sa_s)

    refs_i = (q_ref, d_ref, a_ref, n1_ref, a2_ref)
    pitch = tt_len + SUBLANES
    for k, ref in enumerate(refs_i):
        for i in range(ni):
            rows_s[i, k * pitch:k * pitch + tt_len, :] = ref[i]

    op_q, op_d, op_a, op_n1, op_a2 = range(len(refs_i))

    def row(k, t, i):
        return rows_s[i, pl.ds(k * pitch + t, 1), :]

    def fold(x):
        return x + pltpu.roll(x, LANES // 2, 1)

    for k, (u_ref, w_ref) in enumerate(((a_ref, q_ref), (a2_ref, q_ref), (a_ref, n1_ref), (a2_ref, n1_ref))):
        acc = u_ref[0] * w_ref[0]
        for i in range(1, ni):
            acc = acc + u_ref[i] * w_ref[i]
        dots_s[k] = fold(acc)

    def step(t, sa):
        bv = b_ref[t]
        aq, a2q, an, a2n = [dots_s[k, pl.ds(t, 1), :] for k in range(4)]
        y0 = [jnp.zeros_like(bv), jnp.zeros_like(bv)]
        n0 = [jnp.zeros_like(bv), jnp.zeros_like(bv)]
        for i in range(ni):
            sd = st[i] * row(op_d, t, i)
            st[i] = sd
            y0[i % 2] = y0[i % 2] + sd * row(op_q, t, i)
            n0[i % 2] = n0[i % 2] + sd * row(op_n1, t, i)
        y_ref[t] = fold(y0[0] + y0[1]) + bv * aq + sa * a2q
        sa_next = fold(n0[0] + n0[1]) + bv * an + sa * a2n
        for i in range(ni):
            st[i] = st[i] + row(op_a, t, i) * bv + row(op_a2, t, i) * sa
        return sa_next

    sa_s[...] = lax.fori_loop(0, tt_len, step, sa_s[...])

    @pl.when(tt == n_tt - 1)
    def _():
        so_ref[...] = st[...]


def _rwkv_long_scan(q, d, a, b, n1, a2, tt_len=64):
    ni, t_len, _ = q.shape
    nj = b.shape[1]
    n_tt = t_len // tt_len
    i_spec = pl.BlockSpec((ni, tt_len, LANES), lambda g, t: (0, t, 0))
    j_spec = pl.BlockSpec((tt_len, nj, LANES), lambda g, t: (t, 0, 0))
    s_spec = pl.BlockSpec((ni, nj, LANES), lambda g, t: (0, 0, 0))
    return pl.pallas_call(
        functools.partial(_rwkv_long_kernel, ni=ni, tt_len=tt_len, n_tt=n_tt),
        grid=(1, n_tt),
        in_specs=[i_spec, i_spec, i_spec, j_spec, i_spec, i_spec],
        out_specs=[j_spec, s_spec],
        out_shape=[jax.ShapeDtypeStruct((t_len, nj, LANES), F32), jax.ShapeDtypeStruct((ni, nj, LANES), F32)],
        scratch_shapes=[pltpu.VMEM((ni, nj, LANES), F32), pltpu.VMEM((nj, LANES), F32),
                        pltpu.VMEM((4, tt_len, LANES), F32),
                        pltpu.VMEM((ni, 5 * (tt_len + SUBLANES), LANES), F32)],
        compiler_params=_params(("parallel", "arbitrary")),
        name="scan_rwkv_long",
    )(q, d, a, b, n1, a2)


TR = 128
HALF = LANES // K_SPLIT


Y_PITCH = LANES + SUBLANES


def _to_lanes_kernel(*refs, n_ops, rows, per_half):
    x_refs, o_refs, y2d = refs[:n_ops], refs[n_ops:2 * n_ops], refs[2 * n_ops]
    n_b = x_refs[0].shape[0]
    for x_ref, o_ref in zip(x_refs, o_refs):
        for b in range(n_b):
            xt = x_ref[b].T
            for h in range(H_PAD):
                for s in range(K_SPLIT):
                    lane = s * HALF + b * H_PAD + h
                    f0 = h * HEAD_DIM + (s * rows if per_half else 0)
                    y2d[pl.ds(lane, rows, stride=Y_PITCH), :] = xt[f0:f0 + rows, :]
        for r in range(rows):
            slab = y2d[r * Y_PITCH:r * Y_PITCH + LANES, :].T
            if per_half:
                o_ref[r] = slab
            else:
                o_ref[pl.ds(r, TR, stride=rows), :] = slab


def _long_to_lanes(xs, b, rows, per_half):
    n_ops = len(xs)
    t = xs[0].shape[0] // b
    width = H_PAD * HEAD_DIM
    if per_half:
        o_spec = pl.BlockSpec((rows, TR, LANES), lambda i: (0, i, 0))
        o_shape = jax.ShapeDtypeStruct((rows, t, LANES), F32)
    else:
        o_spec = pl.BlockSpec((TR * rows, LANES), lambda i: (i, 0))
        o_shape = jax.ShapeDtypeStruct((t * rows, LANES), F32)
    outs = pl.pallas_call(
        functools.partial(_to_lanes_kernel, n_ops=n_ops, rows=rows, per_half=per_half),
        grid=(t // TR,),
        in_specs=[pl.BlockSpec((b, TR, width), lambda i: (0, i, 0))] * n_ops,
        out_specs=[o_spec] * n_ops,
        out_shape=[o_shape] * n_ops,
        scratch_shapes=[pltpu.VMEM((rows * Y_PITCH, TR), F32)],
        compiler_params=_params(("parallel",)),
        name="to_lanes",
    )(*[x.reshape(b, t, width) for x in xs])
    return outs if per_half else [o.reshape(t, rows, LANES) for o in outs]


def _from_lanes_kernel(y_ref, o_ref, z2d):
    for v in range(HEAD_DIM):
        z2d[pl.ds(v, LANES, stride=HEAD_DIM), :] = y_ref[pl.ds(v, TR, stride=HEAD_DIM), :].T
    for b in range(o_ref.shape[0]):
        r0 = b * H_PAD * HEAD_DIM
        o_ref[b] = z2d[r0:r0 + W_A, :].T


def _long_from_lanes(y, b):
    t = y.shape[0]
    out = pl.pallas_call(
        _from_lanes_kernel,
        grid=(t // TR,),
        in_specs=[pl.BlockSpec((TR * HEAD_DIM, LANES), lambda i: (i, 0))],
        out_specs=pl.BlockSpec((b, TR, W_A), lambda i: (0, i, 0)),
        out_shape=jax.ShapeDtypeStruct((b, t, W_A), F32),
        scratch_shapes=[pltpu.VMEM((LANES * HEAD_DIM, TR), F32)],
        compiler_params=_params(("parallel",)),
        name="from_lanes",
    )(y.reshape(t * HEAD_DIM, LANES))
    return out.reshape(b * t, W_A)


def _ret_chunk_kernel(q_ref, k_ref, v_ref, y_ref, s_ref, st):
    c = pl.program_id(1)

    @pl.when(c == 0)
    def _():
        st[...] = jnp.zeros_like(st)

    q = q_ref[...]
    k = k_ref[...]
    vb = v_ref[...].astype(BF16)
    kb = k.astype(BF16)
    row = lax.broadcasted_iota(jnp.int32, (LC, W_B), 0)
    head = lax.broadcasted_iota(jnp.int32, (LC, W_B), 1) // HEAD_DIM
    log_g = [math.log(1.0 - 2.0 ** (-5.0 - h)) for h in range(H_B)]
    lg = jnp.full((LC, W_B), log_g[0], F32)
    for h in range(1, H_B):
        lg = jnp.where(head == h, log_g[h], lg)
    rowf = row.astype(F32)
    diff = (lax.broadcasted_iota(jnp.int32, (LC, LC), 0) - lax.broadcasted_iota(jnp.int32, (LC, LC), 1))
    causal = diff >= 0
    difff = jnp.maximum(diff, 0).astype(F32)

    out = jnp.dot((q * jnp.exp(lg * (rowf + 1.0))).astype(BF16), st[...].astype(BF16), preferred_element_type=F32)
    for h in range(H_B):
        qm = jnp.where(head == h, q, 0.0).astype(BF16)
        s = lax.dot_general(qm, kb, (((1,), (1,)), ((), ())), preferred_element_type=F32)
        p = jnp.where(causal, s * jnp.exp(log_g[h] * difff), 0.0).astype(BF16)
        out = out + jnp.where(head == h, jnp.dot(p, vb, preferred_element_type=F32), 0.0)
    y_ref[...] = out

    kt = (k * jnp.exp(lg * (LC - 1.0 - rowf))).T.astype(BF16)
    kv = jnp.dot(kt, vb, preferred_element_type=F32)
    r2 = lax.broadcasted_iota(jnp.int32, (W_B, W_B), 0) // HEAD_DIM
    c2 = lax.broadcasted_iota(jnp.int32, (W_B, W_B), 1) // HEAD_DIM
    cdec = jnp.full((W_B, W_B), math.exp(log_g[0] * LC), F32)
    for h in range(1, H_B):
        cdec = jnp.where(r2 == h, math.exp(log_g[h] * LC), cdec)
    st[...] = st[...] * cdec + jnp.where(r2 == c2, kv, 0.0)
    s_ref[0] = st[...]


def _ret_chunk(q, k, cb, n_seq, t_len):
    n_c = t_len // LC
    rows = lambda col: pl.BlockSpec((LC, W_B), lambda b, c, col=col: (b * n_c + c, col))
    return pl.pallas_call(
        _ret_chunk_kernel,
        grid=(n_seq, n_c),
        in_specs=[rows(0), rows(0), rows(2)],
        out_specs=[rows(0), pl.BlockSpec((1, W_B, W_B), lambda b, c: (b, 0, 0))],
        out_shape=[jax.ShapeDtypeStruct((n_seq * t_len, W_B), F32), jax.ShapeDtypeStruct((n_seq, W_B, W_B), F32)],
        scratch_shapes=[pltpu.VMEM((W_B, W_B), F32)],
        compiler_params=_params(("parallel", "arbitrary")),
        name="ret_chunk",
    )(q, k, cb)


def _ssd_chunk_kernel(x_ref, b_ref, c_ref, da_ref, ex_ref, y_ref, h_ref, st):
    ci = pl.program_id(1)

    @pl.when(ci == 0)
    def _():
        st[...] = jnp.zeros_like(st)

    gn = N_STATE
    half = W_C // N_GROUPS
    ex = ex_ref[...]
    x = x_ref[...]
    xb = x.astype(BF16)
    ri = lax.broadcasted_iota(jnp.int32, (LC, LC), 0)
    cj = lax.broadcasted_iota(jnp.int32, (LC, LC), 1)
    causal = ri >= cj
    cum = _dot_hi(causal.astype(F32), da_ref[...])
    cum_e = _dot_select(cum, ex, 3)
    last_e = cum_e[LC - 1:LC, :]
    cum_t = cum.T
    head = lax.broadcasted_iota(jnp.int32, (LC, W_C), 1) // HEAD_DIM
    bg = [b_ref[:, g * gn:(g + 1) * gn].astype(BF16) for g in range(N_GROUPS)]
    cg = [c_ref[:, g * gn:(g + 1) * gn].astype(BF16) for g in range(N_GROUPS)]
    cb = [lax.dot_general(cg[g], bg[g], (((1,), (1,)), ((), ())), preferred_element_type=F32)
          for g in range(N_GROUPS)]

    hb = st[...].astype(BF16)
    ys = [lax.dot_general(cg[g], hb, (((1,), (1,)), ((), ())), preferred_element_type=F32)
          for g in range(N_GROUPS)]
    y = jnp.where(head < H_C // N_GROUPS, ys[0], ys[1]) * jnp.exp(cum_e)
    for h in range(H_C):
        seg = cum[:, h:h + 1] - cum_t[h:h + 1, :]
        p = (cb[h // (H_C // N_GROUPS)] * jnp.exp(jnp.where(causal, seg, -jnp.inf))).astype(BF16)
        y = y + jnp.where(head == h, jnp.dot(p, xb, preferred_element_type=F32), 0.0)
    y_ref[...] = y

    xt = (x * jnp.exp(last_e - cum_e)).T.astype(BF16)
    upd = [jnp.dot(xt, bg[g], preferred_element_type=F32) for g in range(N_GROUPS)]
    rowi = lax.broadcasted_iota(jnp.int32, (W_C, gn), 0)
    sel = lax.broadcasted_iota(jnp.int32, (W_C, LANES), 1) == lax.broadcasted_iota(jnp.int32, (W_C, LANES), 0) // HEAD_DIM
    tot = jnp.sum(jnp.where(sel, cum[LC - 1:LC, :], 0.0), axis=1, keepdims=True)
    st[...] = st[...] * jnp.exp(tot) + jnp.where(rowi < half, upd[0], upd[1])
    h_ref[0] = st[...]


def _ssd_chunk(xdt, bm, cm, da, ex, n_seq, t_len):
    n_c = t_len // LC
    gn = N_GROUPS * N_STATE
    rows = lambda w: pl.BlockSpec((LC, w), lambda b, c: (b * n_c + c, 0))
    return pl.pallas_call(
        _ssd_chunk_kernel,
        grid=(n_seq, n_c),
        in_specs=[rows(W_C), rows(gn), rows(gn), rows(LANES), pl.BlockSpec((LANES, W_C), lambda b, c: (0, 0))],
        out_specs=[rows(W_C), pl.BlockSpec((1, W_C, N_STATE), lambda b, c: (b, 0, 0))],
        out_shape=[jax.ShapeDtypeStruct((n_seq * t_len, W_C), F32),
                   jax.ShapeDtypeStruct((n_seq, W_C, N_STATE), F32)],
        scratch_shapes=[pltpu.VMEM((W_C, N_STATE), F32)],
        compiler_params=_params(("parallel", "arbitrary")),
        name="ssd_chunk",
    )(xdt, bm, cm, da, ex)


def _post_kernel(x_ref, ya_ref, bonus_ref, ga_ref, ob_ref, gb_ref, yc_ref, xs_ref, z_ref, lnw_ref, lnb_ref,
                 rn_ref, dsk_ref, sn_ref, bda_ref, bdb_ref, wo_ref, out_ref):
    inv_hd = 1.0 / HEAD_DIM
    y = ya_ref[...]
    bda = bda_ref[...]
    mean = _dot_select(y, bda) * inv_hd
    yd = y - mean
    var = _dot_select(yd * yd, bda) * inv_hd
    ya = (yd * lax.rsqrt(var + GN_EPS) * lnw_ref[...] + lnb_ref[...] + bonus_ref[...]) * ga_ref[...]
    o = ob_ref[...]
    ms = _dot_select(o * o, bdb_ref[...]) * inv_hd
    gb = gb_ref[...]
    yb = (gb * _sigmoid(gb)) * (o * lax.rsqrt(ms + RMS_EPS) * rn_ref[...])
    z = z_ref[...]
    yc = (yc_ref[...] + dsk_ref[...] * xs_ref[...]) * (z * _sigmoid(z))
    yc = _rmsnorm(yc, sn_ref[...], GATED_NORM_EPS)
    out_ref[...] = (x_ref[...] + _dot(ya, wo_ref[0:W_A, :]) + _dot(yb, wo_ref[W_A:W_A + W_B, :])
                    + _dot(yc, wo_ref[W_A + W_B:, :]))


def _post(x, ya, bonus, ga, ob, cb, yc, xs, cc, p):
    n = x.shape[0]
    va = _full_spec((1, W_A))
    return pl.pallas_call(
        _post_kernel,
        grid=(n // TM,),
        in_specs=[_row_spec(D_MODEL), _row_spec(W_A), _row_spec(W_A), _row_spec(W_A), _row_spec(W_B),
                  _row_spec(W_B, 3), _row_spec(W_C), _row_spec(W_C), _row_spec(W_C, 0), va, va,
                  _full_spec((1, W_B)), va, va, _full_spec((W_A, W_A)), _full_spec((W_B, W_B)),
                  _full_spec((D_MODEL, D_MODEL))],
        out_specs=_row_spec(D_MODEL),
        out_shape=jax.ShapeDtypeStruct((n, D_MODEL), F32),
        compiler_params=_params(("parallel",)),
        name="post_outproj",
    )(x, ya, bonus, ga, ob, cb, yc, xs, cc, p["ln_w"], p["ln_b"], p["ret_norm"], p["d_skip"], p["ssm_norm"],
      p["bd_a"], p["bd_b"], p["w_out"])


def _ffn_kernel(*refs, gated, final_norm, n_e, n_sub):
    refs = list(refs)
    x_ref, nf_ref = refs[:2]
    pos = 2
    if gated:
        router_ref = refs[pos]
        pos += 1
    wg_ref, wu_ref, wd_ref = refs[pos:pos + 3]
    pos += 3
    if final_norm:
        nfin_ref = refs[pos]
        pos += 1
    out_ref = refs[pos]
    hb_s, acc_s = refs[pos + 1:pos + 3]
    if gated:
        gate_s, asg_s, pos_s, asg_t, pos_t = refs[pos + 3:pos + 8]
    e = pl.program_id(1)
    subs = [(slice(s * TM, (s + 1) * TM), slice(s * LANES, (s + 1) * LANES)) for s in range(n_sub)]

    @pl.when(e == 0)
    def _():
        acc_s[...] = jnp.zeros_like(acc_s)
        for rs, ts in subs:
            h = _rmsnorm(x_ref[rs, :], nf_ref[...], RMS_EPS)
            hb_s[rs, :] = h.astype(BF16)
            if gated:
                lane = lax.broadcasted_iota(jnp.int32, (TM, LANES), 1)
                logits = jnp.where(lane < N_EXPERTS, _dot_hi(h, router_ref[...]), -jnp.inf)
                p = jnp.exp(logits - jnp.max(logits, axis=-1, keepdims=True))
                p = p / jnp.sum(p, axis=-1, keepdims=True)
                p1 = jnp.max(p, axis=-1, keepdims=True)
                i1 = jnp.min(jnp.where(p == p1, lane, LANES), axis=-1, keepdims=True)
                rest = jnp.where(lane == i1, -1.0, p)
                p2 = jnp.max(rest, axis=-1, keepdims=True)
                i2 = jnp.min(jnp.where(rest == p2, lane, LANES), axis=-1, keepdims=True)
                gate_s[rs, :] = jnp.where(lane == i1, p1, jnp.where(lane == i2, p2, 0.0)) / (p1 + p2)
                assigned = jnp.where((lane == i1) | (lane == i2), 1.0, 0.0)
                earlier = (lax.broadcasted_iota(jnp.int32, (TM, TM), 0)
                           > lax.broadcasted_iota(jnp.int32, (TM, TM), 1))
                rank = jnp.dot(earlier.astype(BF16), assigned.astype(BF16), preferred_element_type=F32)
                asg_s[rs, :] = assigned
                pos_s[rs, :] = rank
                asg_t[ts, :] = assigned.T
                pos_t[ts, :] = rank.T

    if not gated:
        hb = hb_s[...]
        g = jnp.dot(hb, wg_ref[0], preferred_element_type=F32)
        u = jnp.dot(hb, wu_ref[0], preferred_element_type=F32)
        acc_s[...] += _dot(g * _sigmoid(g) * u, wd_ref[0])
    else:
        lane = lax.broadcasted_iota(jnp.int32, (TM, LANES), 1)
        mine = lane == e
        for s, (rs, ts) in enumerate(subs):
            col = lambda ref, rs=rs: jnp.sum(jnp.where(mine, ref[rs, :], 0.0), axis=-1, keepdims=True)
            gate_c, asg_c, pos_c = col(gate_s), col(asg_s), col(pos_s)
            asg_r = asg_t[pl.ds(s * LANES + e, 1), :]
            pos_r = pos_t[pl.ds(s * LANES + e, 1), :]
            n_chunks = (jnp.sum(asg_r).astype(jnp.int32) + MOE_CHUNK - 1) // MOE_CHUNK

            def chunk(c, carry, rs=rs, gate_c=gate_c, asg_c=asg_c, pos_c=pos_c, asg_r=asg_r, pos_r=pos_r):
                base = (c * MOE_CHUNK).astype(F32)
                slot_r = lax.broadcasted_iota(jnp.int32, (MOE_CHUNK, TM), 0).astype(F32) + base
                pick = jnp.where((pos_r == slot_r) & (asg_r > 0.0), 1.0, 0.0).astype(BF16)
                rows = jnp.dot(pick, hb_s[rs, :], preferred_element_type=F32).astype(BF16)
                g = jnp.dot(rows, wg_ref[0], preferred_element_type=F32)
                u = jnp.dot(rows, wu_ref[0], preferred_element_type=F32)
                o = _dot(g * _sigmoid(g) * u, wd_ref[0])
                o_hi = o.astype(BF16)
                o_lo = (o - o_hi.astype(F32)).astype(BF16)
                slot_c = lax.broadcasted_iota(jnp.int32, (TM, MOE_CHUNK), 1).astype(F32) + base
                place = jnp.where((pos_c == slot_c) & (asg_c > 0.0), 1.0, 0.0).astype(BF16)
                back = (jnp.dot(place, o_hi, preferred_element_type=F32)
                        + jnp.dot(place, o_lo, preferred_element_type=F32))
                acc_s[rs, :] += gate_c * back
                return carry

            lax.fori_loop(0, n_chunks, chunk, 0)

    @pl.when(e == n_e - 1)
    def _():
        y = x_ref[...] + acc_s[...]
        if final_norm:
            y = _rmsnorm(y, nfin_ref[...], RMS_EPS)
        out_ref[...] = y


def _ffn(x, nf, wg, wu, wd, router=None, nfin=None):
    n = x.shape[0]
    n_e, _, dff = wg.shape
    gated = router is not None
    final_norm = nfin is not None
    n_sub = MOE_SUB if gated else 1
    rows = n_sub * TM
    assert n % rows == 0
    vec = pl.BlockSpec((1, D_MODEL), lambda i, e: (0, 0))
    args = [x, nf]
    specs = [pl.BlockSpec((rows, D_MODEL), lambda i, e: (i, 0)), vec]
    if gated:
        args.append(router)
        specs.append(pl.BlockSpec((D_MODEL, LANES), lambda i, e: (0, 0)))
    args += [wg, wu, wd]
    specs += [pl.BlockSpec((1, D_MODEL, dff), lambda i, e: (e, 0, 0)),
              pl.BlockSpec((1, D_MODEL, dff), lambda i, e: (e, 0, 0)),
              pl.BlockSpec((1, dff, D_MODEL), lambda i, e: (e, 0, 0))]
    if final_norm:
        args.append(nfin)
        specs.append(vec)
    scratch = [pltpu.VMEM((rows, D_MODEL), BF16), pltpu.VMEM((rows, D_MODEL), F32)]
    if gated:
        scratch += [pltpu.VMEM((rows, LANES), F32)] * 3 + [pltpu.VMEM((n_sub * LANES, TM), F32)] * 2
    return pl.pallas_call(
        functools.partial(_ffn_kernel, gated=gated, final_norm=final_norm, n_e=n_e, n_sub=n_sub),
        grid=(n // rows, n_e),
        in_specs=specs,
        out_specs=pl.BlockSpec((rows, D_MODEL), lambda i, e: (i, 0)),
        out_shape=jax.ShapeDtypeStruct((n, D_MODEL), F32),
        scratch_shapes=scratch,
        compiler_params=_params(("parallel", "arbitrary")),
        name="moe" if gated else "ffn",
    )(*args)


def _short_to_lanes_kernel(*refs, n_ops, t_len):
    for x_ref, o_ref in zip(refs[:n_ops], refs[n_ops:]):
        for t in range(t_len):
            o_ref[t] = x_ref[pl.ds(t, LANES, stride=t_len), :].T


def _short_to_lanes(xs, t_len):
    n_chunks = [width // LANES for _, _, width in xs]
    grid = max(n_chunks)
    clamp = lambda c, n: jnp.minimum(c, n - 1)
    outs = pl.pallas_call(
        functools.partial(_short_to_lanes_kernel, n_ops=len(xs), t_len=t_len),
        grid=(grid,),
        in_specs=[pl.BlockSpec((LANES * t_len, LANES), lambda c, c0=col // LANES, n=n: (0, c0 + clamp(c, n)))
                  for (_, col, _), n in zip(xs, n_chunks)],
        out_specs=[pl.BlockSpec((t_len, LANES, LANES), lambda c, n=n: (0, clamp(c, n), 0)) for n in n_chunks],
        out_shape=[jax.ShapeDtypeStruct((t_len, width, LANES), F32) for _, _, width in xs],
        compiler_params=_params(("arbitrary",)),
        name="short_to_lanes",
    )(*[x for x, _, _ in xs])
    return outs


def _short_from_lanes_kernel(y_ref, o_ref, *, t_len):
    for t in range(t_len):
        o_ref[pl.ds(t, LANES, stride=t_len), :] = y_ref[t].T


def _short_from_lanes(y):
    t_len, w, _ = y.shape
    return pl.pallas_call(
        functools.partial(_short_from_lanes_kernel, t_len=t_len),
        grid=(w // LANES,),
        in_specs=[pl.BlockSpec((t_len, LANES, LANES), lambda c: (0, c, 0))],
        out_specs=pl.BlockSpec((LANES * t_len, LANES), lambda c: (0, c)),
        out_shape=jax.ShapeDtypeStruct((LANES * t_len, w), F32),
        compiler_params=_params(("parallel",)),
        name="short_from_lanes",
    )(y)


def _state_to_lanes_kernel(x_ref, o_ref, *, ni, nj, j_first):
    xt = x_ref[...].T
    if j_first:
        for j in range(nj):
            o_ref[pl.ds(j, ni, stride=nj), :] = xt[j * ni:(j + 1) * ni, :]
    else:
        o_ref[...] = xt


def _state_to_lanes(s_all, layer, ni, nj, j_first):
    depth, _, h = s_all.shape[:3]
    out = pl.pallas_call(
        functools.partial(_state_to_lanes_kernel, ni=ni, nj=nj, j_first=j_first),
        grid=(h,),
        in_specs=[pl.BlockSpec((LANES, ni * nj), lambda i: (layer, i))],
        out_specs=pl.BlockSpec((ni * nj, LANES), lambda i: (i, 0)),
        out_shape=jax.ShapeDtypeStruct((h * ni * nj, LANES), F32),
        compiler_params=_params(("parallel",)),
        name="state_to_lanes",
    )(s_all.reshape(depth * LANES, h * ni * nj))
    return out.reshape(h * ni, nj, LANES)


def _state_from_lanes_kernel(s_ref, o_ref, tmp, *, ni, nj, j_first):
    if j_first:
        for j in range(nj):
            tmp[j * ni:(j + 1) * ni, :] = s_ref[pl.ds(j, ni, stride=nj), :]
        o_ref[...] = tmp[...].T
    else:
        o_ref[...] = s_ref[...].T


def _state_from_lanes(s, n_heads, ni, nj, j_first):
    out = pl.pallas_call(
        functools.partial(_state_from_lanes_kernel, ni=ni, nj=nj, j_first=j_first),
        grid=(n_heads,),
        in_specs=[pl.BlockSpec((ni * nj, LANES), lambda i: (i, 0))],
        out_specs=pl.BlockSpec((LANES, ni * nj), lambda i: (0, i)),
        out_shape=jax.ShapeDtypeStruct((LANES, n_heads * ni * nj), F32),
        scratch_shapes=[pltpu.VMEM((ni * nj, LANES), F32)],
        compiler_params=_params(("parallel",)),
        name="state_from_lanes",
    )(s.reshape(n_heads * ni * nj, LANES))
    return out.reshape((LANES, n_heads, nj, ni) if j_first else (LANES, n_heads, ni, nj))


def _rwkv_long(feats, b):
    r, dec, k2, v, nkk, beta = feats
    t = r.shape[0] // b
    kl = HEAD_DIM // K_SPLIT
    assert b * H_PAD == HALF and t % TR == 0
    nkk_next = jnp.concatenate([nkk[1:], jnp.zeros((1, nkk.shape[1]), F32)], axis=0)
    q, d, a, n1, a2 = _long_to_lanes([r, dec, k2, nkk_next, beta], b, kl, True)
    (vv,) = _long_to_lanes([v], b, HEAD_DIM, False)
    y, s = _rwkv_long_scan(q, d, a, vv, n1, a2)
    s = s.reshape(kl, HEAD_DIM, K_SPLIT, b, H_PAD)[..., :H_A]
    return _long_from_lanes(y, b), jnp.transpose(s, (3, 4, 1, 2, 0)).reshape(b, H_A, HEAD_DIM, HEAD_DIM)


def _rwkv_short(feats, t_len, s0):
    r, dec, k2, v, nkk, beta = _short_to_lanes([(f, 0, W_A) for f in feats], t_len)
    hd = HEAD_DIM
    own = lambda h: h
    y, s = _scan("rwkv", H_A, hd, hd, (r, own), (dec, own), (k2, own), (v, own),
                 _state_to_lanes(*s0, hd, hd, True), n=(nkk, own), a2=(beta, own))
    return _short_from_lanes(y), _state_from_lanes(s, H_A, hd, hd, True)


def _ret_short(qh, kh, cb, t_len, s0):
    q, k, v = _short_to_lanes([(qh, 0, W_B), (kh, 0, W_B), (cb, 2 * W_B, W_B)], t_len)
    hd = HEAD_DIM
    gam = np.ones((SUBLANES, LANES), np.float32)
    gam[:H_B] = (1.0 - 2.0 ** (-5.0 - np.arange(H_B, dtype=np.float64)))[:, None]
    own = lambda h: h
    y, s = _scan("ret", H_B, hd, hd, (q, own), jnp.asarray(gam), (k, own), (v, own),
                 _state_to_lanes(*s0, hd, hd, False))
    return _short_from_lanes(y), _state_from_lanes(s, H_B, hd, hd, False)


def _ssd_short(xdt, bm, cm, dssm, t_len, s0):
    gn = N_GROUPS * N_STATE
    x, bl, cl, dl = _short_to_lanes([(xdt, 0, W_C), (bm, 0, gn), (cm, 0, gn), (dssm, 0, LANES)], t_len)
    group = lambda h: h // (H_C // N_GROUPS)
    y, s = _scan("ssd", H_C, N_STATE, HEAD_DIM, (cl, group), dl, (bl, group), (x, lambda h: h),
                 _state_to_lanes(*s0, N_STATE, HEAD_DIM, True))
    return _short_from_lanes(y), _state_from_lanes(s, H_C, N_STATE, HEAD_DIM, True)


def _block_diag_ones(width):
    idx = np.arange(width) // HEAD_DIM
    return jnp.asarray((idx[:, None] == idx[None, :]).astype(np.float32), BF16)


def _shifted(x, prev_rows, shift):
    b, t, c = x.shape
    p = prev_rows.shape[1]
    full = jnp.concatenate([prev_rows, x], axis=1)
    return full[:, p - shift:p - shift + t].reshape(b * t, c)


def _layer(x, b, p, st, rope, table_block):
    n = x.shape[0]
    t = n // b
    fresh = st is None
    tiles_per_seq = t // TM if fresh else 0
    ca, cb, cc = _in_proj(x, p["norm_mix"], p["w_in"])
    ca3 = ca.reshape(b, t, COLS_A)
    xbc_tail = cc.reshape(b, t, COLS_C_PAD)[:, -(CONV_W - 1):, W_C:W_C + CONV_DIM]
    if fresh:
        prev, shifted = None, None
        conv_new = xbc_tail
    else:
        prev = _shifted(ca3, st["shift"][:, None, :], 1)
        xbc = cc[:, W_C:W_C + CONV_DIM].reshape(b, t, CONV_DIM)
        shifted = [_shifted(xbc, st["conv"], j) for j in range(1, CONV_W)]
        conv_new = jnp.concatenate([st["conv"], xbc_tail], axis=1)[:, -(CONV_W - 1):]

    r, dec, k2, v, nkk, beta, ga, bonus = _rwkv_prep(ca, prev, p, tiles_per_seq)
    qh, kh = _ret_prep(cb, rope[0], rope[1], table_block)
    xdt, bm, cm, dssm, xs, da = _ssm_prep(cc, shifted, p, tiles_per_seq)

    feats = (r, dec, k2, v, nkk, beta)
    if fresh:
        ya, s_rwkv = _rwkv_long(feats, b)
        yb, s_full = _ret_chunk(qh, kh, cb, b, t)
        s_ret = jnp.stack([s_full[:, h * HEAD_DIM:(h + 1) * HEAD_DIM, h * HEAD_DIM:(h + 1) * HEAD_DIM]
                           for h in range(H_B)], axis=1)
        yc, s_ssm = _ssd_chunk(xdt, bm, cm, da, p["expand"], b, t)
        s_ssm = s_ssm.reshape(b, H_C, HEAD_DIM, N_STATE)
    else:
        assert b == LANES
        ya, s_rwkv = _rwkv_short(feats, t, st["rwkv"])
        yb, s_ret = _ret_short(qh, kh, cb, t, st["ret"])
        yc, s_ssm = _ssd_short(xdt, bm, cm, dssm, t, st["ssm"])

    x2 = _post(x, ya, bonus, ga, yb, cb, yc, xs, cc, p)
    x = _ffn(x2, p["norm_ffn"], p["wg"], p["wu"], p["wd"], router=p["router"], nfin=p["norm_final"])
    return x, (s_rwkv, ca3[:, -1], s_ret, s_ssm, conv_new)


def kernel(x_prompt, x_sample, state_rwkv, state_shift, state_ret, state_ssm, state_conv, norm_mix, w_in, rwkv_mu, rwkv_w0, rwkv_w_up, rwkv_a0, rwkv_a_up, rwkv_g_up, rwkv_k_k, rwkv_k_a, rwkv_r_k, rwkv_ln_w, rwkv_ln_b, ret_norm, ssm_conv_w, ssm_conv_b, ssm_dt_bias, ssm_a_log, ssm_d, ssm_norm, w_out, norm_ffn, ffn_w_gate, ffn_w_up, ffn_w_down, moe_router, moe_w_gate, moe_w_up, moe_w_down, norm_final):
    bp, tp, _ = x_prompt.shape
    bs, ts, _ = x_sample.shape
    depth = w_in.shape[0]
    assert tp % TM == 0 and tp % LC == 0 and (bs * ts) % TM == 0 and TM % ts == 0

    rope = _rope_tables(tp, ts)
    bd_a, bd_b = _block_diag_ones(W_A), _block_diag_ones(W_B)
    expand = np.zeros((LANES, W_C), np.float32)
    expand[np.arange(W_C) // HEAD_DIM, np.arange(W_C)] = 1.0
    expand = jnp.asarray(expand, BF16)
    row = lambda v: v.reshape(1, -1)
    pad_l = lambda v: jnp.pad(v, (0, LANES - v.shape[0])).reshape(1, LANES)

    xp = x_prompt.reshape(bp * tp, D_MODEL)
    xs = x_sample.reshape(bs * ts, D_MODEL)
    new_p, new_s = [], []
    for i in range(depth):
        j = i // 2
        p = dict(
            norm_mix=row(norm_mix[i]),
            w_in=jnp.pad(w_in[i], ((0, 0), (0, COLS_C_PAD - COLS_C))).astype(BF16),
            mu=row(rwkv_mu[i]), w0=row(rwkv_w0[i]), a0=row(rwkv_a0[i]), k_k=row(rwkv_k_k[i]),
            k_a=row(rwkv_k_a[i]), r_k=row(rwkv_r_k[i]),
            w_up=jnp.pad(rwkv_w_up[i], ((0, AAA_LORA), (0, 0))).astype(BF16),
            a_up=jnp.pad(rwkv_a_up[i], ((DECAY_LORA, 0), (0, 0))).astype(BF16),
            g_up=rwkv_g_up[i].astype(BF16), bd_a=bd_a, bd_b=bd_b,
            conv_w=ssm_conv_w[i], conv_b=row(ssm_conv_b[i]), dt_bias=pad_l(ssm_dt_bias[i]),
            a_log=pad_l(ssm_a_log[i]), expand=expand,
            ln_w=row(rwkv_ln_w[i]), ln_b=row(rwkv_ln_b[i]), ret_norm=row(ret_norm[i]),
            d_skip=row(jnp.repeat(ssm_d[i], HEAD_DIM)), ssm_norm=row(ssm_norm[i]), w_out=w_out[i].astype(BF16),
            norm_ffn=row(norm_ffn[i]), norm_final=row(norm_final) if i == depth - 1 else None)
        if i % 2 == 0:
            dff = ffn_w_gate.shape[-1] // 2
            p.update(router=None,
                     wg=ffn_w_gate[j].reshape(D_MODEL, 2, dff).transpose(1, 0, 2).astype(BF16),
                     wu=ffn_w_up[j].reshape(D_MODEL, 2, dff).transpose(1, 0, 2).astype(BF16),
                     wd=ffn_w_down[j].reshape(2, dff, D_MODEL).astype(BF16))
        else:
            p.update(router=jnp.pad(moe_router[j], ((0, 0), (0, LANES - N_EXPERTS))),
                     wg=moe_w_gate[j].astype(BF16), wu=moe_w_up[j].astype(BF16), wd=moe_w_down[j].astype(BF16))

        xp, st_p = _layer(xp, bp, p, None, rope, lambda t: t % (tp // TM))
        st = dict(rwkv=(state_rwkv, i), shift=state_shift[i], ret=(state_ret, i), ssm=(state_ssm, i),
                  conv=state_conv[i])
        xs, st_s = _layer(xs, bs, p, st, rope, lambda t: tp // TM)
        new_p.append(st_p)
        new_s.append(st_s)

    stack = lambda sts: tuple(jnp.stack(s) for s in zip(*sts))
    return (xp.reshape(bp, tp, D_MODEL), xs.reshape(bs, ts, D_MODEL)) + stack(new_p) + stack(new_s)
```

```python
import functools
import math

import numpy as np
import jax
import jax.numpy as jnp
from jax import lax
from jax.experimental import pallas as pl
from jax.experimental.pallas import tpu as pltpu

F32 = jnp.float32
BF16 = jnp.bfloat16
HIGHEST = lax.Precision.HIGHEST

LANES = 128
SUBLANES = 8
VMEM_LIMIT = 56 * 1024 * 1024

D_MODEL = 1024
HEAD_DIM = 64
H_A, H_B, H_C = 6, 4, 6
W_A, W_B, W_C = H_A * HEAD_DIM, H_B * HEAD_DIM, H_C * HEAD_DIM
DECAY_LORA, AAA_LORA, GATE_LORA = 64, 64, 128
COLS_A = 3 * W_A + DECAY_LORA + AAA_LORA + GATE_LORA
COLS_B = 4 * W_B
N_STATE, N_GROUPS, CONV_W = 128, 2, 4
CONV_DIM = W_C + 2 * N_GROUPS * N_STATE
COLS_C = W_C + CONV_DIM + H_C
COLS_C_PAD = 1408
ROPE_BASE = 10000.0
RMS_EPS = 1e-6
GN_EPS = 64e-5
GATED_NORM_EPS = 1e-5
N_EXPERTS = 8
PAST_LEN = 16384

TM = 512
MOE_CHUNK = 160
MOE_SUB = 2
LC = 256
H_PAD = 8
K_SPLIT = 2


def _dot(a, b):
    return jnp.dot(a.astype(BF16), b.astype(BF16), preferred_element_type=F32)


def _dot_hi(a, b):
    return jnp.dot(a, b, precision=HIGHEST, preferred_element_type=F32)


def _dot_select(a, sel, terms=2):
    out = None
    rest = a
    for _ in range(terms):
        piece = rest.astype(BF16)
        part = jnp.dot(piece, sel, preferred_element_type=F32)
        out = part if out is None else out + part
        rest = rest - piece.astype(F32)
    return out


def _sigmoid(x):
    return 1.0 / (1.0 + jnp.exp(-x))


def _softplus(x):
    return jnp.maximum(x, 0.0) + jnp.log1p(jnp.exp(-jnp.abs(x)))


def _rmsnorm(x, g, eps):
    return x * lax.rsqrt(jnp.mean(x * x, axis=-1, keepdims=True) + eps) * g


def _params(sem):
    return pltpu.CompilerParams(dimension_semantics=sem, vmem_limit_bytes=VMEM_LIMIT)


def _row_spec(width, col=0):
    return pl.BlockSpec((TM, width), lambda i, c=col: (i, c))


def _before_spec(width):
    return pl.BlockSpec((SUBLANES, width), lambda i: (jnp.maximum(i * (TM // SUBLANES) - 1, 0), 0))


def _full_spec(shape):
    nd = len(shape)
    return pl.BlockSpec(shape, lambda i, n=nd: (0,) * n)


def _shift_rows(x, before, j, first):
    rolled = pltpu.roll(x, j, 0)
    prev = jnp.where(first, 0.0, pltpu.roll(before, j, 0))
    row = lax.broadcasted_iota(jnp.int32, prev.shape, 0)
    top = jnp.where(row < j, prev, rolled[0:SUBLANES])
    return jnp.concatenate([top, rolled[SUBLANES:]], axis=0)


def _in_proj_kernel(x_ref, g_ref, w_ref, oa_ref, ob_ref, oc_ref):
    h = _rmsnorm(x_ref[...], g_ref[...], RMS_EPS).astype(BF16)
    oa_ref[...] = jnp.dot(h, w_ref[:, 0:COLS_A], preferred_element_type=F32)
    ob_ref[...] = jnp.dot(h, w_ref[:, COLS_A:COLS_A + COLS_B], preferred_element_type=F32)
    oc_ref[...] = jnp.dot(h, w_ref[:, COLS_A + COLS_B:], preferred_element_type=F32)


def _in_proj(x, g, w):
    n = x.shape[0]
    wtot = w.shape[1]
    return pl.pallas_call(
        _in_proj_kernel,
        grid=(n // TM,),
        in_specs=[_row_spec(D_MODEL), _full_spec((1, D_MODEL)), _full_spec((D_MODEL, wtot))],
        out_specs=[_row_spec(COLS_A), _row_spec(COLS_B), _row_spec(COLS_C_PAD)],
        out_shape=[jax.ShapeDtypeStruct((n, COLS_A), F32), jax.ShapeDtypeStruct((n, COLS_B), F32),
                   jax.ShapeDtypeStruct((n, COLS_C_PAD), F32)],
        compiler_params=_params(("parallel",)),
        name="in_proj",
    )(x, g, w)


def _rwkv_prep_kernel(c_ref, p_ref, mu_ref, w0_ref, a0_ref, kk_ref, ka_ref, rk_ref, wup_ref, aup_ref,
                      gup_ref, bd_ref, r_o, d_o, k_o, v_o, n_o, b_o, g_o, bonus_o, *, tiles_per_seq):
    c = c_ref[...]
    if tiles_per_seq:
        prev = _shift_rows(c, p_ref[...], 1, pl.program_id(0) % tiles_per_seq == 0)
    else:
        prev = p_ref[...]
    xm = c + (prev - c) * mu_ref[...]
    r = xm[:, 0:W_A]
    k = xm[:, W_A:2 * W_A]
    v = xm[:, 2 * W_A:3 * W_A]
    lora = xm[:, 3 * W_A:3 * W_A + DECAY_LORA + AAA_LORA]
    gd = xm[:, 3 * W_A + DECAY_LORA + AAA_LORA:]
    w = w0_ref[...] + _dot(jnp.tanh(lora), wup_ref[...])
    decay = jnp.exp(-math.exp(-0.5) * _sigmoid(w))
    a = _sigmoid(a0_ref[...] + _dot(lora, aup_ref[...]))
    g = _dot(_sigmoid(gd), gup_ref[...])
    bd = bd_ref[...]
    kk = k * kk_ref[...]
    kk = kk / jnp.maximum(jnp.sqrt(_dot_select(kk * kk, bd)), 1e-12)
    k2 = k * (1.0 + (a - 1.0) * ka_ref[...])
    for o_ref, val in ((r_o, r), (d_o, decay), (k_o, k2), (v_o, v), (n_o, -kk), (b_o, kk * a)):
        o_ref[:, 0:W_A] = val
        if o_ref.shape[1] > W_A:
            o_ref[:, W_A:] = jnp.zeros((TM, o_ref.shape[1] - W_A), F32)
    g_o[...] = g
    bonus_o[...] = _dot_select(r * k2 * rk_ref[...], bd) * v


def _rwkv_prep(ca, prev, p, tiles_per_seq):
    n = ca.shape[0]
    vec = _full_spec((1, W_A))
    lora_spec = _full_spec((DECAY_LORA + AAA_LORA, W_A))
    p_spec = _before_spec(COLS_A) if tiles_per_seq else _row_spec(COLS_A)
    w_scan = H_PAD * HEAD_DIM if tiles_per_seq else W_A
    widths = [w_scan] * 6 + [W_A] * 2
    return pl.pallas_call(
        functools.partial(_rwkv_prep_kernel, tiles_per_seq=tiles_per_seq),
        grid=(n // TM,),
        in_specs=[_row_spec(COLS_A), p_spec, _full_spec((1, COLS_A)), vec, vec, vec, vec, vec,
                  lora_spec, lora_spec, _full_spec((GATE_LORA, W_A)), _full_spec((W_A, W_A))],
        out_specs=[_row_spec(w) for w in widths],
        out_shape=[jax.ShapeDtypeStruct((n, w), F32) for w in widths],
        compiler_params=_params(("parallel",)),
        name="rwkv_prep",
    )(ca, ca if tiles_per_seq else prev, p["mu"], p["w0"], p["a0"], p["k_k"], p["k_a"], p["r_k"],
      p["w_up"], p["a_up"], p["g_up"], p["bd_a"])


def _rope_kernel(ang_ref, cos_o, sin_o):
    ang = ang_ref[...]
    lane = lax.broadcasted_iota(jnp.int32, ang.shape, 1)
    cos_o[...] = jnp.cos(ang)
    sin_o[...] = jnp.where((lane % HEAD_DIM) < (HEAD_DIM // 2), -jnp.sin(ang), jnp.sin(ang))


def _rope_tables(t_prompt, t_sample):
    theta = 1.0 / (ROPE_BASE ** jnp.linspace(0.0, 1.0, HEAD_DIM // 2, dtype=F32))
    pos = jnp.concatenate([jnp.arange(t_prompt, dtype=F32), PAST_LEN + (jnp.arange(TM) % t_sample).astype(F32)])
    ang = jnp.tile(pos[:, None] * theta[None, :], (1, W_B // (HEAD_DIM // 2)))
    n = ang.shape[0]
    return pl.pallas_call(
        _rope_kernel,
        grid=(n // TM,),
        in_specs=[_row_spec(W_B)],
        out_specs=[_row_spec(W_B)] * 2,
        out_shape=[jax.ShapeDtypeStruct((n, W_B), F32)] * 2,
        compiler_params=_params(("parallel",)),
        name="rope_tables",
    )(ang)


def _ret_prep_kernel(q_ref, k_ref, cos_ref, sin_ref, q_o, k_o):
    cos = cos_ref[...]
    sin = sin_ref[...]
    lane = lax.broadcasted_iota(jnp.int32, (TM, W_B), 1)
    first_half = (lane % HEAD_DIM) < (HEAD_DIM // 2)

    def rope(x):
        partner = jnp.where(first_half, pltpu.roll(x, W_B - HEAD_DIM // 2, 1), pltpu.roll(x, HEAD_DIM // 2, 1))
        return x * cos + partner * sin

    q_o[...] = rope(q_ref[...])
    k_o[...] = rope(k_ref[...]) * (HEAD_DIM ** -0.5)


def _ret_prep(cb, cos, sin, table_block):
    n = cb.shape[0]
    tab = pl.BlockSpec((TM, W_B), lambda i: (table_block(i), 0))
    return pl.pallas_call(
        _ret_prep_kernel,
        grid=(n // TM,),
        in_specs=[_row_spec(W_B, 0), _row_spec(W_B, 1), tab, tab],
        out_specs=[_row_spec(W_B)] * 2,
        out_shape=[jax.ShapeDtypeStruct((n, W_B), F32)] * 2,
        compiler_params=_params(("parallel",)),
        name="ret_prep",
    )(cb, cb, cos, sin)


def _ssm_prep_kernel(*refs, tiles_per_seq):
    n_shift_refs = 1 if tiles_per_seq else CONV_W - 1
    cc_ref = refs[0]
    shift_refs = refs[1:1 + n_shift_refs]
    cw_ref, cb_ref, dtb_ref, alog_ref, ex_ref, xdt_o, b_o, c_o, dec_o, xs_o, da_o = refs[1 + n_shift_refs:]
    cc = cc_ref[...]
    x0 = cc[:, W_C:W_C + CONV_DIM]
    if tiles_per_seq:
        before = shift_refs[0][:, W_C:W_C + CONV_DIM]
        first = pl.program_id(0) % tiles_per_seq == 0
        xs_prev = [_shift_rows(x0, before, j, first) for j in range(1, CONV_W)]
    else:
        xs_prev = [r[...] for r in shift_refs]
    cw = cw_ref[...]
    conv = x0 * cw[CONV_W - 1:CONV_W, :] + cb_ref[...]
    for j in range(1, CONV_W):
        conv = conv + xs_prev[j - 1] * cw[CONV_W - 1 - j:CONV_W - j, :]
    act = conv * _sigmoid(conv)
    xs = act[:, 0:W_C]
    dt = _softplus(cc[:, W_C + CONV_DIM:] + dtb_ref[...])
    da = dt * (-jnp.exp(alog_ref[...]))
    da_o[...] = da
    dec_o[...] = jnp.exp(da)
    xdt_o[...] = xs * _dot_select(dt, ex_ref[...])
    b_o[...] = act[:, W_C:W_C + N_GROUPS * N_STATE]
    c_o[...] = act[:, W_C + N_GROUPS * N_STATE:]
    xs_o[...] = xs


def _ssm_prep(cc, shifted, p, tiles_per_seq):
    n = cc.shape[0]
    gn = N_GROUPS * N_STATE
    if tiles_per_seq:
        shift_args, shift_specs = [cc], [_before_spec(COLS_C_PAD)]
    else:
        shift_args, shift_specs = list(shifted), [_row_spec(CONV_DIM)] * (CONV_W - 1)
    return pl.pallas_call(
        functools.partial(_ssm_prep_kernel, tiles_per_seq=tiles_per_seq),
        grid=(n // TM,),
        in_specs=[_row_spec(COLS_C_PAD)] + shift_specs + [
            _full_spec((CONV_W, CONV_DIM)), _full_spec((1, CONV_DIM)), _full_spec((1, LANES)),
            _full_spec((1, LANES)), _full_spec((LANES, W_C))],
        out_specs=[_row_spec(W_C), _row_spec(gn), _row_spec(gn), _row_spec(LANES), _row_spec(W_C),
                   _row_spec(LANES)],
        out_shape=[jax.ShapeDtypeStruct((n, W_C), F32), jax.ShapeDtypeStruct((n, gn), F32),
                   jax.ShapeDtypeStruct((n, gn), F32), jax.ShapeDtypeStruct((n, LANES), F32),
                   jax.ShapeDtypeStruct((n, W_C), F32), jax.ShapeDtypeStruct((n, LANES), F32)],
        compiler_params=_params(("parallel",)),
        name="ssm_prep",
    )(cc, *shift_args, p["conv_w"], p["conv_b"], p["dt_bias"], p["a_log"], p["expand"])


def _scan_kernel(*refs, mode, ni, tt_len, n_tt):
    if mode == "rwkv":
        q_ref, d_ref, a_ref, b_ref, n_ref, a2_ref, s0_ref, y_ref, so_ref, st = refs
    else:
        q_ref, d_ref, a_ref, b_ref, s0_ref, y_ref, so_ref, st = refs
    head = pl.program_id(0)
    tt = pl.program_id(1)

    @pl.when(tt == 0)
    def _():
        st[...] = s0_ref[...]

    def row(ref, t, i):
        return ref[t, pl.ds(i, 1), :]

    def step(t, carry):
        bv = b_ref[t]
        if mode == "rwkv":
            sa_parts = [jnp.zeros_like(bv), jnp.zeros_like(bv)]
            for i in range(ni):
                sa_parts[i % 2] = sa_parts[i % 2] + st[i] * row(n_ref, t, i)
            sa = sa_parts[0] + sa_parts[1]
        elif mode == "ssd":
            d = d_ref[t, pl.ds(head, 1), :]
        else:
            d = d_ref[pl.ds(head, 1), :]
        y_parts = [jnp.zeros_like(bv), jnp.zeros_like(bv)]
        for i in range(ni):
            if mode == "rwkv":
                s = st[i] * row(d_ref, t, i) + row(a_ref, t, i) * bv + row(a2_ref, t, i) * sa
            else:
                s = st[i] * d + row(a_ref, t, i) * bv
            st[i] = s
            y_parts[i % 2] = y_parts[i % 2] + s * row(q_ref, t, i)
        y_ref[t] = y_parts[0] + y_parts[1]
        return carry

    lax.fori_loop(0, tt_len, step, 0)

    @pl.when(tt == n_tt - 1)
    def _():
        so_ref[...] = st[...]


def _scan(mode, n_heads, ni, nj, q, d, a, b, s0, n=None, a2=None):
    t_len = b[0].shape[0]

    def rows(op, width):
        arr, block_of = op
        return arr, pl.BlockSpec((t_len, width, LANES), lambda h, t, f=block_of: (0, f(h), 0))

    ops = [rows(q, ni)]
    if mode == "rwkv":
        ops.append(rows(d, ni))
    elif mode == "ssd":
        ops.append((d, pl.BlockSpec((t_len, SUBLANES, LANES), lambda h, t: (0, 0, 0))))
    else:
        ops.append((d, pl.BlockSpec((SUBLANES, LANES), lambda h, t: (0, 0))))
    ops += [rows(a, ni), rows(b, nj)]
    if mode == "rwkv":
        ops += [rows(n, ni), rows(a2, ni)]
    s_spec = pl.BlockSpec((ni, nj, LANES), lambda h, t: (h, 0, 0))
    y_spec = pl.BlockSpec((t_len, nj, LANES), lambda h, t: (0, h, 0))
    return pl.pallas_call(
        functools.partial(_scan_kernel, mode=mode, ni=ni, tt_len=t_len, n_tt=1),
        grid=(n_heads, 1),
        in_specs=[spec for _, spec in ops] + [s_spec],
        out_specs=[y_spec, s_spec],
        out_shape=[jax.ShapeDtypeStruct((t_len, n_heads * nj, LANES), F32),
                   jax.ShapeDtypeStruct((n_heads * ni, nj, LANES), F32)],
        scratch_shapes=[pltpu.VMEM((ni, nj, LANES), F32)],
        compiler_params=_params(("parallel", "arbitrary")),
        name="scan_" + mode,
    )(*[arr for arr, _ in ops], s0)


def _rwkv_long_kernel(q_ref, d_ref, a_ref, b_ref, n1_ref, a2_ref, y_ref, so_ref, st, sa_s, dots_s, rows_s, *,
                      ni, tt_len, n_tt):
    tt = pl.program_id(1)

    @pl.when(tt == 0)
    def _():
        st[...] = jnp.zeros_like(st)
        sa_s[...] = jnp.zeros_like(sa_s)

    refs_i = (q_ref, d_ref, a_ref, n1_ref, a2_ref)
    pitch = tt_len + SUBLANES
    for k, ref in enumerate(refs_i):
        for i in range(ni):
            rows_s[i, k * pitch:k * pitch + tt_len, :] = ref[i]

    op_q, op_d, op_a, op_n1, op_a2 = range(len(refs_i))

    def row(k, t, i):
        return rows_s[i, pl.ds(k * pitch + t, 1), :]

    def fold(x):
        return x + pltpu.roll(x, LANES // 2, 1)

    for k, (u_ref, w_ref) in enumerate(((a_ref, q_ref), (a2_ref, q_ref), (a_ref, n1_ref), (a2_ref, n1_ref))):
        acc = u_ref[0] * w_ref[0]
        for i in range(1, ni):
            acc = acc + u_ref[i] * w_ref[i]
        dots_s[k] = fold(acc)

    def step(t, sa):
        bv = b_ref[t]
        aq, a2q, an, a2n = [dots_s[k, pl.ds(t, 1), :] for k in range(4)]
        y0 = [jnp.zeros_like(bv), jnp.zeros_like(bv)]
        n0 = [jnp.zeros_like(bv), jnp.zeros_like(bv)]
        for i in range(ni):
            sd = st[i] * row(op_d, t, i)
            st[i] = sd
            y0[i % 2] = y0[i % 2] + sd * row(op_q, t, i)
            n0[i % 2] = n0[i % 2] + sd * row(op_n1, t, i)
        y_ref[t] = fold(y0[0] + y0[1]) + bv * aq + sa * a2q
        sa_next = fold(n0[0] + n0[1]) + bv * an + sa * a2n
        for i in range(ni):
            st[i] = st[i] + row(op_a, t, i) * bv + row(op_a2, t, i) * sa
        return sa_next

    sa_s[...] = lax.fori_loop(0, tt_len, step, sa_s[...])

    @pl.when(tt == n_tt - 1)
    def _():
        so_ref[...] = st[...]


def _rwkv_long_scan(q, d, a, b, n1, a2, tt_len=64):
    ni, t_len, _ = q.shape
    nj = b.shape[1]
    n_tt = t_len // tt_len
    i_spec = pl.BlockSpec((ni, tt_len, LANES), lambda g, t: (0, t, 0))
    j_spec = pl.BlockSpec((tt_len, nj, LANES), lambda g, t: (t, 0, 0))
    s_spec = pl.BlockSpec((ni, nj, LANES), lambda g, t: (0, 0, 0))
    return pl.pallas_call(
        functools.partial(_rwkv_long_kernel, ni=ni, tt_len=tt_len, n_tt=n_tt),
        grid=(1, n_tt),
        in_specs=[i_spec, i_spec, i_spec, j_spec, i_spec, i_spec],
        out_specs=[j_spec, s_spec],
        out_shape=[jax.ShapeDtypeStruct((t_len, nj, LANES), F32), jax.ShapeDtypeStruct((ni, nj, LANES), F32)],
        scratch_shapes=[pltpu.VMEM((ni, nj, LANES), F32), pltpu.VMEM((nj, LANES), F32),
                        pltpu.VMEM((4, tt_len, LANES), F32),
                        pltpu.VMEM((ni, 5 * (tt_len + SUBLANES), LANES), F32)],
        compiler_params=_params(("parallel", "arbitrary")),
        name="scan_rwkv_long",
    )(q, d, a, b, n1, a2)


TR = 128
HALF = LANES // K_SPLIT


Y_PITCH = LANES + SUBLANES


def _to_lanes_kernel(*refs, n_ops, rows, per_half):
    x_refs, o_refs, y2d = refs[:n_ops], refs[n_ops:2 * n_ops], refs[2 * n_ops]
    n_b = x_refs[0].shape[0]
    for x_ref, o_ref in zip(x_refs, o_refs):
        for b in range(n_b):
            xt = x_ref[b].T
            for h in range(H_PAD):
                for s in range(K_SPLIT):
                    lane = s * HALF + b * H_PAD + h
                    f0 = h * HEAD_DIM + (s * rows if per_half else 0)
                    y2d[pl.ds(lane, rows, stride=Y_PITCH), :] = xt[f0:f0 + rows, :]
        for r in range(rows):
            slab = y2d[r * Y_PITCH:r * Y_PITCH + LANES, :].T
            if per_half:
                o_ref[r] = slab
            else:
                o_ref[pl.ds(r, TR, stride=rows), :] = slab


def _long_to_lanes(xs, b, rows, per_half):
    n_ops = len(xs)
    t = xs[0].shape[0] // b
    width = H_PAD * HEAD_DIM
    if per_half:
        o_spec = pl.BlockSpec((rows, TR, LANES), lambda i: (0, i, 0))
        o_shape = jax.ShapeDtypeStruct((rows, t, LANES), F32)
    else:
        o_spec = pl.BlockSpec((TR * rows, LANES), lambda i: (i, 0))
        o_shape = jax.ShapeDtypeStruct((t * rows, LANES), F32)
    outs = pl.pallas_call(
        functools.partial(_to_lanes_kernel, n_ops=n_ops, rows=rows, per_half=per_half),
        grid=(t // TR,),
        in_specs=[pl.BlockSpec((b, TR, width), lambda i: (0, i, 0))] * n_ops,
        out_specs=[o_spec] * n_ops,
        out_shape=[o_shape] * n_ops,
        scratch_shapes=[pltpu.VMEM((rows * Y_PITCH, TR), F32)],
        compiler_params=_params(("parallel",)),
        name="to_lanes",
    )(*[x.reshape(b, t, width) for x in xs])
    return outs if per_half else [o.reshape(t, rows, LANES) for o in outs]


def _from_lanes_kernel(y_ref, o_ref, z2d):
    for v in range(HEAD_DIM):
        z2d[pl.ds(v, LANES, stride=HEAD_DIM), :] = y_ref[pl.ds(v, TR, stride=HEAD_DIM), :].T
    for b in range(o_ref.shape[0]):
        r0 = b * H_PAD * HEAD_DIM
        o_ref[b] = z2d[r0:r0 + W_A, :].T


def _long_from_lanes(y, b):
    t = y.shape[0]
    out = pl.pallas_call(
        _from_lanes_kernel,
        grid=(t // TR,),
        in_specs=[pl.BlockSpec((TR * HEAD_DIM, LANES), lambda i: (i, 0))],
        out_specs=pl.BlockSpec((b, TR, W_A), lambda i: (0, i, 0)),
        out_shape=jax.ShapeDtypeStruct((b, t, W_A), F32),
        scratch_shapes=[pltpu.VMEM((LANES * HEAD_DIM, TR), F32)],
        compiler_params=_params(("parallel",)),
        name="from_lanes",
    )(y.reshape(t * HEAD_DIM, LANES))
    return out.reshape(b * t, W_A)


def _ret_chunk_kernel(q_ref, k_ref, v_ref, y_ref, s_ref, st):
    c = pl.program_id(1)

    @pl.when(c == 0)
    def _():
        st[...] = jnp.zeros_like(st)

    q = q_ref[...]
    k = k_ref[...]
    vb = v_ref[...].astype(BF16)
    kb = k.astype(BF16)
    row = lax.broadcasted_iota(jnp.int32, (LC, W_B), 0)
    head = lax.broadcasted_iota(jnp.int32, (LC, W_B), 1) // HEAD_DIM
    log_g = [math.log(1.0 - 2.0 ** (-5.0 - h)) for h in range(H_B)]
    lg = jnp.full((LC, W_B), log_g[0], F32)
    for h in range(1, H_B):
        lg = jnp.where(head == h, log_g[h], lg)
    rowf = row.astype(F32)
    diff = (lax.broadcasted_iota(jnp.int32, (LC, LC), 0) - lax.broadcasted_iota(jnp.int32, (LC, LC), 1))
    causal = diff >= 0
    difff = jnp.maximum(diff, 0).astype(F32)

    out = jnp.dot((q * jnp.exp(lg * (rowf + 1.0))).astype(BF16), st[...].astype(BF16), preferred_element_type=F32)
    for h in range(H_B):
        qm = jnp.where(head == h, q, 0.0).astype(BF16)
        s = lax.dot_general(qm, kb, (((1,), (1,)), ((), ())), preferred_element_type=F32)
        p = jnp.where(causal, s * jnp.exp(log_g[h] * difff), 0.0).astype(BF16)
        out = out + jnp.where(head == h, jnp.dot(p, vb, preferred_element_type=F32), 0.0)
    y_ref[...] = out

    kt = (k * jnp.exp(lg * (LC - 1.0 - rowf))).T.astype(BF16)
    kv = jnp.dot(kt, vb, preferred_element_type=F32)
    r2 = lax.broadcasted_iota(jnp.int32, (W_B, W_B), 0) // HEAD_DIM
    c2 = lax.broadcasted_iota(jnp.int32, (W_B, W_B), 1) // HEAD_DIM
    cdec = jnp.full((W_B, W_B), math.exp(log_g[0] * LC), F32)
    for h in range(1, H_B):
        cdec = jnp.where(r2 == h, math.exp(log_g[h] * LC), cdec)
    st[...] = st[...] * cdec + jnp.where(r2 == c2, kv, 0.0)
    s_ref[0] = st[...]


def _ret_chunk(q, k, cb, n_seq, t_len):
    n_c = t_len // LC
    rows = lambda col: pl.BlockSpec((LC, W_B), lambda b, c, col=col: (b * n_c + c, col))
    return pl.pallas_call(
        _ret_chunk_kernel,
        grid=(n_seq, n_c),
        in_specs=[rows(0), rows(0), rows(2)],
        out_specs=[rows(0), pl.BlockSpec((1, W_B, W_B), lambda b, c: (b, 0, 0))],
        out_shape=[jax.ShapeDtypeStruct((n_seq * t_len, W_B), F32), jax.ShapeDtypeStruct((n_seq, W_B, W_B), F32)],
        scratch_shapes=[pltpu.VMEM((W_B, W_B), F32)],
        compiler_params=_params(("parallel", "arbitrary")),
        name="ret_chunk",
    )(q, k, cb)


def _ssd_chunk_kernel(x_ref, b_ref, c_ref, da_ref, ex_ref, y_ref, h_ref, st):
    ci = pl.program_id(1)

    @pl.when(ci == 0)
    def _():
        st[...] = jnp.zeros_like(st)

    gn = N_STATE
    half = W_C // N_GROUPS
    ex = ex_ref[...]
    x = x_ref[...]
    xb = x.astype(BF16)
    ri = lax.broadcasted_iota(jnp.int32, (LC, LC), 0)
    cj = lax.broadcasted_iota(jnp.int32, (LC, LC), 1)
    causal = ri >= cj
    cum = _dot_hi(causal.astype(F32), da_ref[...])
    cum_e = _dot_select(cum, ex, 3)
    last_e = cum_e[LC - 1:LC, :]
    cum_t = cum.T
    head = lax.broadcasted_iota(jnp.int32, (LC, W_C), 1) // HEAD_DIM
    bg = [b_ref[:, g * gn:(g + 1) * gn].astype(BF16) for g in range(N_GROUPS)]
    cg = [c_ref[:, g * gn:(g + 1) * gn].astype(BF16) for g in range(N_GROUPS)]
    cb = [lax.dot_general(cg[g], bg[g], (((1,), (1,)), ((), ())), preferred_element_type=F32)
          for g in range(N_GROUPS)]

    hb = st[...].astype(BF16)
    ys = [lax.dot_general(cg[g], hb, (((1,), (1,)), ((), ())), preferred_element_type=F32)
          for g in range(N_GROUPS)]
    y = jnp.where(head < H_C // N_GROUPS, ys[0], ys[1]) * jnp.exp(cum_e)
    for h in range(H_C):
        seg = cum[:, h:h + 1] - cum_t[h:h + 1, :]
        p = (cb[h // (H_C // N_GROUPS)] * jnp.exp(jnp.where(causal, seg, -jnp.inf))).astype(BF16)
        y = y + jnp.where(head == h, jnp.dot(p, xb, preferred_element_type=F32), 0.0)
    y_ref[...] = y

    xt = (x * jnp.exp(last_e - cum_e)).T.astype(BF16)
    upd = [jnp.dot(xt, bg[g], preferred_element_type=F32) for g in range(N_GROUPS)]
    rowi = lax.broadcasted_iota(jnp.int32, (W_C, gn), 0)
    sel = lax.broadcasted_iota(jnp.int32, (W_C, LANES), 1) == lax.broadcasted_iota(jnp.int32, (W_C, LANES), 0) // HEAD_DIM
    tot = jnp.sum(jnp.where(sel, cum[LC - 1:LC, :], 0.0), axis=1, keepdims=True)
    st[...] = st[...] * jnp.exp(tot) + jnp.where(rowi < half, upd[0], upd[1])
    h_ref[0] = st[...]


def _ssd_chunk(xdt, bm, cm, da, ex, n_seq, t_len):
    n_c = t_len // LC
    gn = N_GROUPS * N_STATE
    rows = lambda w: pl.BlockSpec((LC, w), lambda b, c: (b * n_c + c, 0))
    return pl.pallas_call(
        _ssd_chunk_kernel,
        grid=(n_seq, n_c),
        in_specs=[rows(W_C), rows(gn), rows(gn), rows(LANES), pl.BlockSpec((LANES, W_C), lambda b, c: (0, 0))],
        out_specs=[rows(W_C), pl.BlockSpec((1, W_C, N_STATE), lambda b, c: (b, 0, 0))],
        out_shape=[jax.ShapeDtypeStruct((n_seq * t_len, W_C), F32),
                   jax.ShapeDtypeStruct((n_seq, W_C, N_STATE), F32)],
        scratch_shapes=[pltpu.VMEM((W_C, N_STATE), F32)],
        compiler_params=_params(("parallel", "arbitrary")),
        name="ssd_chunk",
    )(xdt, bm, cm, da, ex)


def _post_kernel(x_ref, ya_ref, bonus_ref, ga_ref, ob_ref, gb_ref, yc_ref, xs_ref, z_ref, lnw_ref, lnb_ref,
                 rn_ref, dsk_ref, sn_ref, bda_ref, bdb_ref, wo_ref, out_ref):
    inv_hd = 1.0 / HEAD_DIM
    y = ya_ref[...]
    bda = bda_ref[...]
    mean = _dot_select(y, bda) * inv_hd
    yd = y - mean
    var = _dot_select(yd * yd, bda) * inv_hd
    ya = (yd * lax.rsqrt(var + GN_EPS) * lnw_ref[...] + lnb_ref[...] + bonus_ref[...]) * ga_ref[...]
    o = ob_ref[...]
    ms = _dot_select(o * o, bdb_ref[...]) * inv_hd
    gb = gb_ref[...]
    yb = (gb * _sigmoid(gb)) * (o * lax.rsqrt(ms + RMS_EPS) * rn_ref[...])
    z = z_ref[...]
    yc = (yc_ref[...] + dsk_ref[...] * xs_ref[...]) * (z * _sigmoid(z))
    yc = _rmsnorm(yc, sn_ref[...], GATED_NORM_EPS)
    out_ref[...] = (x_ref[...] + _dot(ya, wo_ref[0:W_A, :]) + _dot(yb, wo_ref[W_A:W_A + W_B, :])
                    + _dot(yc, wo_ref[W_A + W_B:, :]))


def _post(x, ya, bonus, ga, ob, cb, yc, xs, cc, p):
    n = x.shape[0]
    va = _full_spec((1, W_A))
    return pl.pallas_call(
        _post_kernel,
        grid=(n // TM,),
        in_specs=[_row_spec(D_MODEL), _row_spec(W_A), _row_spec(W_A), _row_spec(W_A), _row_spec(W_B),
                  _row_spec(W_B, 3), _row_spec(W_C), _row_spec(W_C), _row_spec(W_C, 0), va, va,
                  _full_spec((1, W_B)), va, va, _full_spec((W_A, W_A)), _full_spec((W_B, W_B)),
                  _full_spec((D_MODEL, D_MODEL))],
        out_specs=_row_spec(D_MODEL),
        out_shape=jax.ShapeDtypeStruct((n, D_MODEL), F32),
        compiler_params=_params(("parallel",)),
        name="post_outproj",
    )(x, ya, bonus, ga, ob, cb, yc, xs, cc, p["ln_w"], p["ln_b"], p["ret_norm"], p["d_skip"], p["ssm_norm"],
      p["bd_a"], p["bd_b"], p["w_out"])


def _ffn_kernel(*refs, gated, final_norm, n_e, n_sub):
    refs = list(refs)
    x_ref, nf_ref = refs[:2]
    pos = 2
    if gated:
        router_ref = refs[pos]
        pos += 1
    wg_ref, wu_ref, wd_ref = refs[pos:pos + 3]
    pos += 3
    if final_norm:
        nfin_ref = refs[pos]
        pos += 1
    out_ref = refs[pos]
    hb_s, acc_s = refs[pos + 1:pos + 3]
    if gated:
        gate_s, asg_s, pos_s, asg_t, pos_t = refs[pos + 3:pos + 8]
    e = pl.program_id(1)
    subs = [(slice(s * TM, (s + 1) * TM), slice(s * LANES, (s + 1) * LANES)) for s in range(n_sub)]

    @pl.when(e == 0)
    def _():
        acc_s[...] = jnp.zeros_like(acc_s)
        for rs, ts in subs:
            h = _rmsnorm(x_ref[rs, :], nf_ref[...], RMS_EPS)
            hb_s[rs, :] = h.astype(BF16)
            if gated:
                lane = lax.broadcasted_iota(jnp.int32, (TM, LANES), 1)
                logits = jnp.where(lane < N_EXPERTS, _dot_hi(h, router_ref[...]), -jnp.inf)
                p = jnp.exp(logits - jnp.max(logits, axis=-1, keepdims=True))
                p = p / jnp.sum(p, axis=-1, keepdims=True)
                p1 = jnp.max(p, axis=-1, keepdims=True)
                i1 = jnp.min(jnp.where(p == p1, lane, LANES), axis=-1, keepdims=True)
                rest = jnp.where(lane == i1, -1.0, p)
                p2 = jnp.max(rest, axis=-1, keepdims=True)
                i2 = jnp.min(jnp.where(rest == p2, lane, LANES), axis=-1, keepdims=True)
                gate_s[rs, :] = jnp.where(lane == i1, p1, jnp.where(lane == i2, p2, 0.0)) / (p1 + p2)
                assigned = jnp.where((lane == i1) | (lane == i2), 1.0, 0.0)
                earlier = (lax.broadcasted_iota(jnp.int32, (TM, TM), 0)
                           > lax.broadcasted_iota(jnp.int32, (TM, TM), 1))
                rank = jnp.dot(earlier.astype(BF16), assigned.astype(BF16), preferred_element_type=F32)
                asg_s[rs, :] = assigned
                pos_s[rs, :] = rank
                asg_t[ts, :] = assigned.T
                pos_t[ts, :] = rank.T

    if not gated:
        hb = hb_s[...]
        g = jnp.dot(hb, wg_ref[0], preferred_element_type=F32)
        u = jnp.dot(hb, wu_ref[0], preferred_element_type=F32)
        acc_s[...] += _dot(g * _sigmoid(g) * u, wd_ref[0])
    else:
        lane = lax.broadcasted_iota(jnp.int32, (TM, LANES), 1)
        mine = lane == e
        col = lambda ref, rs: jnp.sum(jnp.where(mine, ref[rs, :], 0.0), axis=-1, keepdims=True)
        cols = [(col(gate_s, rs), col(asg_s, rs), col(pos_s, rs)) for rs, _ in subs]
        rws = [(asg_t[pl.ds(s * LANES + e, 1), :], pos_t[pl.ds(s * LANES + e, 1), :])
               for s in range(n_sub)]
        most = functools.reduce(jnp.maximum, [jnp.sum(asg_r) for asg_r, _ in rws])
        n_chunks = (most.astype(jnp.int32) + MOE_CHUNK - 1) // MOE_CHUNK

        def chunk(c, carry):
            base = (c * MOE_CHUNK).astype(F32)
            slot_r = lax.broadcasted_iota(jnp.int32, (MOE_CHUNK, TM), 0).astype(F32) + base
            picked = []
            for (rs, _), (asg_r, pos_r) in zip(subs, rws):
                pick = jnp.where((pos_r == slot_r) & (asg_r > 0.0), 1.0, 0.0).astype(BF16)
                picked.append(jnp.dot(pick, hb_s[rs, :], preferred_element_type=F32).astype(BF16))
            rows = jnp.concatenate(picked, axis=0)
            g = jnp.dot(rows, wg_ref[0], preferred_element_type=F32)
            u = jnp.dot(rows, wu_ref[0], preferred_element_type=F32)
            o = _dot(g * _sigmoid(g) * u, wd_ref[0]).astype(BF16)
            slot_c = lax.broadcasted_iota(jnp.int32, (TM, MOE_CHUNK), 1).astype(F32) + base
            for s, ((rs, _), (gate_c, asg_c, pos_c)) in enumerate(zip(subs, cols)):
                place = jnp.where((pos_c == slot_c) & (asg_c > 0.0), 1.0, 0.0).astype(BF16)
                back = jnp.dot(place, o[s * MOE_CHUNK:(s + 1) * MOE_CHUNK, :], preferred_element_type=F32)
                acc_s[rs, :] += gate_c * back
            return carry

        lax.fori_loop(0, n_chunks, chunk, 0)

    @pl.when(e == n_e - 1)
    def _():
        y = x_ref[...] + acc_s[...]
        if final_norm:
            y = _rmsnorm(y, nfin_ref[...], RMS_EPS)
        out_ref[...] = y


def _ffn(x, nf, wg, wu, wd, router=None, nfin=None):
    n = x.shape[0]
    n_e, _, dff = wg.shape
    gated = router is not None
    final_norm = nfin is not None
    n_sub = MOE_SUB if gated else 1
    rows = n_sub * TM
    assert n % rows == 0
    vec = pl.BlockSpec((1, D_MODEL), lambda i, e: (0, 0))
    args = [x, nf]
    specs = [pl.BlockSpec((rows, D_MODEL), lambda i, e: (i, 0)), vec]
    if gated:
        args.append(router)
        specs.append(pl.BlockSpec((D_MODEL, LANES), lambda i, e: (0, 0)))
    args += [wg, wu, wd]
    specs += [pl.BlockSpec((1, D_MODEL, dff), lambda i, e: (e, 0, 0)),
              pl.BlockSpec((1, D_MODEL, dff), lambda i, e: (e, 0, 0)),
              pl.BlockSpec((1, dff, D_MODEL), lambda i, e: (e, 0, 0))]
    if final_norm:
        args.append(nfin)
        specs.append(vec)
    scratch = [pltpu.VMEM((rows, D_MODEL), BF16), pltpu.VMEM((rows, D_MODEL), F32)]
    if gated:
        scratch += [pltpu.VMEM((rows, LANES), F32)] * 3 + [pltpu.VMEM((n_sub * LANES, TM), F32)] * 2
    return pl.pallas_call(
        functools.partial(_ffn_kernel, gated=gated, final_norm=final_norm, n_e=n_e, n_sub=n_sub),
        grid=(n // rows, n_e),
        in_specs=specs,
        out_specs=pl.BlockSpec((rows, D_MODEL), lambda i, e: (i, 0)),
        out_shape=jax.ShapeDtypeStruct((n, D_MODEL), F32),
        scratch_shapes=scratch,
        compiler_params=_params(("parallel", "arbitrary")),
        name="moe" if gated else "ffn",
    )(*args)


def _short_to_lanes_kernel(*refs, n_ops, t_len):
    for x_ref, o_ref in zip(refs[:n_ops], refs[n_ops:]):
        for t in range(t_len):
            o_ref[t] = x_ref[pl.ds(t, LANES, stride=t_len), :].T


def _short_to_lanes(xs, t_len):
    n_chunks = [width // LANES for _, _, width in xs]
    grid = max(n_chunks)
    clamp = lambda c, n: jnp.minimum(c, n - 1)
    outs = pl.pallas_call(
        functools.partial(_short_to_lanes_kernel, n_ops=len(xs), t_len=t_len),
        grid=(grid,),
        in_specs=[pl.BlockSpec((LANES * t_len, LANES), lambda c, c0=col // LANES, n=n: (0, c0 + clamp(c, n)))
                  for (_, col, _), n in zip(xs, n_chunks)],
        out_specs=[pl.BlockSpec((t_len, LANES, LANES), lambda c, n=n: (0, clamp(c, n), 0)) for n in n_chunks],
        out_shape=[jax.ShapeDtypeStruct((t_len, width, LANES), F32) for _, _, width in xs],
        compiler_params=_params(("arbitrary",)),
        name="short_to_lanes",
    )(*[x for x, _, _ in xs])
    return outs


def _short_from_lanes_kernel(y_ref, o_ref, *, t_len):
    for t in range(t_len):
        o_ref[pl.ds(t, LANES, stride=t_len), :] = y_ref[t].T


def _short_from_lanes(y):
    t_len, w, _ = y.shape
    return pl.pallas_call(
        functools.partial(_short_from_lanes_kernel, t_len=t_len),
        grid=(w // LANES,),
        in_specs=[pl.BlockSpec((t_len, LANES, LANES), lambda c: (0, c, 0))],
        out_specs=pl.BlockSpec((LANES * t_len, LANES), lambda c: (0, c)),
        out_shape=jax.ShapeDtypeStruct((LANES * t_len, w), F32),
        compiler_params=_params(("parallel",)),
        name="short_from_lanes",
    )(y)


def _state_to_lanes_kernel(x_ref, o_ref, *, ni, nj, j_first):
    xt = x_ref[...].T
    if j_first:
        for j in range(nj):
            o_ref[pl.ds(j, ni, stride=nj), :] = xt[j * ni:(j + 1) * ni, :]
    else:
        o_ref[...] = xt


def _state_to_lanes(s_all, layer, ni, nj, j_first):
    depth, _, h = s_all.shape[:3]
    out = pl.pallas_call(
        functools.partial(_state_to_lanes_kernel, ni=ni, nj=nj, j_first=j_first),
        grid=(h,),
        in_specs=[pl.BlockSpec((LANES, ni * nj), lambda i: (layer, i))],
        out_specs=pl.BlockSpec((ni * nj, LANES), lambda i: (i, 0)),
        out_shape=jax.ShapeDtypeStruct((h * ni * nj, LANES), F32),
        compiler_params=_params(("parallel",)),
        name="state_to_lanes",
    )(s_all.reshape(depth * LANES, h * ni * nj))
    return out.reshape(h * ni, nj, LANES)


def _state_from_lanes_kernel(s_ref, o_ref, tmp, *, ni, nj, j_first):
    if j_first:
        for j in range(nj):
            tmp[j * ni:(j + 1) * ni, :] = s_ref[pl.ds(j, ni, stride=nj), :]
        o_ref[...] = tmp[...].T
    else:
        o_ref[...] = s_ref[...].T


def _state_from_lanes(s, n_heads, ni, nj, j_first):
    out = pl.pallas_call(
        functools.partial(_state_from_lanes_kernel, ni=ni, nj=nj, j_first=j_first),
        grid=(n_heads,),
        in_specs=[pl.BlockSpec((ni * nj, LANES), lambda i: (i, 0))],
        out_specs=pl.BlockSpec((LANES, ni * nj), lambda i: (0, i)),
        out_shape=jax.ShapeDtypeStruct((LANES, n_heads * ni * nj), F32),
        scratch_shapes=[pltpu.VMEM((ni * nj, LANES), F32)],
        compiler_params=_params(("parallel",)),
        name="state_from_lanes",
    )(s.reshape(n_heads * ni * nj, LANES))
    return out.reshape((LANES, n_heads, nj, ni) if j_first else (LANES, n_heads, ni, nj))


def _rwkv_long(feats, b):
    r, dec, k2, v, nkk, beta = feats
    t = r.shape[0] // b
    kl = HEAD_DIM // K_SPLIT
    assert b * H_PAD == HALF and t % TR == 0
    nkk_next = jnp.concatenate([nkk[1:], jnp.zeros((1, nkk.shape[1]), F32)], axis=0)
    q, d, a, n1, a2 = _long_to_lanes([r, dec, k2, nkk_next, beta], b, kl, True)
    (vv,) = _long_to_lanes([v], b, HEAD_DIM, False)
    y, s = _rwkv_long_scan(q, d, a, vv, n1, a2)
    s = s.reshape(kl, HEAD_DIM, K_SPLIT, b, H_PAD)[..., :H_A]
    return _long_from_lanes(y, b), jnp.transpose(s, (3, 4, 1, 2, 0)).reshape(b, H_A, HEAD_DIM, HEAD_DIM)


def _rwkv_short(feats, t_len, s0):
    r, dec, k2, v, nkk, beta = _short_to_lanes([(f, 0, W_A) for f in feats], t_len)
    hd = HEAD_DIM
    own = lambda h: h
    y, s = _scan("rwkv", H_A, hd, hd, (r, own), (dec, own), (k2, own), (v, own),
                 _state_to_lanes(*s0, hd, hd, True), n=(nkk, own), a2=(beta, own))
    return _short_from_lanes(y), _state_from_lanes(s, H_A, hd, hd, True)


def _ret_short(qh, kh, cb, t_len, s0):
    q, k, v = _short_to_lanes([(qh, 0, W_B), (kh, 0, W_B), (cb, 2 * W_B, W_B)], t_len)
    hd = HEAD_DIM
    gam = np.ones((SUBLANES, LANES), np.float32)
    gam[:H_B] = (1.0 - 2.0 ** (-5.0 - np.arange(H_B, dtype=np.float64)))[:, None]
    own = lambda h: h
    y, s = _scan("ret", H_B, hd, hd, (q, own), jnp.asarray(gam), (k, own), (v, own),
                 _state_to_lanes(*s0, hd, hd, False))
    return _short_from_lanes(y), _state_from_lanes(s, H_B, hd, hd, False)


def _ssd_short(xdt, bm, cm, dssm, t_len, s0):
    gn = N_GROUPS * N_STATE
    x, bl, cl, dl = _short_to_lanes([(xdt, 0, W_C), (bm, 0, gn), (cm, 0, gn), (dssm, 0, LANES)], t_len)
    group = lambda h: h // (H_C // N_GROUPS)
    y, s = _scan("ssd", H_C, N_STATE, HEAD_DIM, (cl, group), dl, (bl, group), (x, lambda h: h),
                 _state_to_lanes(*s0, N_STATE, HEAD_DIM, True))
    return _short_from_lanes(y), _state_from_lanes(s, H_C, N_STATE, HEAD_DIM, True)


def _block_diag_ones(width):
    idx = np.arange(width) // HEAD_DIM
    return jnp.asarray((idx[:, None] == idx[None, :]).astype(np.float32), BF16)


def _shifted(x, prev_rows, shift):
    b, t, c = x.shape
    p = prev_rows.shape[1]
    full = jnp.concatenate([prev_rows, x], axis=1)
    return full[:, p - shift:p - shift + t].reshape(b * t, c)


def _layer(x, b, p, st, rope, table_block):
    n = x.shape[0]
    t = n // b
    fresh = st is None
    tiles_per_seq = t // TM if fresh else 0
    ca, cb, cc = _in_proj(x, p["norm_mix"], p["w_in"])
    ca3 = ca.reshape(b, t, COLS_A)
    xbc_tail = cc.reshape(b, t, COLS_C_PAD)[:, -(CONV_W - 1):, W_C:W_C + CONV_DIM]
    if fresh:
        prev, shifted = None, None
        conv_new = xbc_tail
    else:
        prev = _shifted(ca3, st["shift"][:, None, :], 1)
        xbc = cc[:, W_C:W_C + CONV_DIM].reshape(b, t, CONV_DIM)
        shifted = [_shifted(xbc, st["conv"], j) for j in range(1, CONV_W)]
        conv_new = jnp.concatenate([st["conv"], xbc_tail], axis=1)[:, -(CONV_W - 1):]

    r, dec, k2, v, nkk, beta, ga, bonus = _rwkv_prep(ca, prev, p, tiles_per_seq)
    qh, kh = _ret_prep(cb, rope[0], rope[1], table_block)
    xdt, bm, cm, dssm, xs, da = _ssm_prep(cc, shifted, p, tiles_per_seq)

    feats = (r, dec, k2, v, nkk, beta)
    if fresh:
        ya, s_rwkv = _rwkv_long(feats, b)
        yb, s_full = _ret_chunk(qh, kh, cb, b, t)
        s_ret = jnp.stack([s_full[:, h * HEAD_DIM:(h + 1) * HEAD_DIM, h * HEAD_DIM:(h + 1) * HEAD_DIM]
                           for h in range(H_B)], axis=1)
        yc, s_ssm = _ssd_chunk(xdt, bm, cm, da, p["expand"], b, t)
        s_ssm = s_ssm.reshape(b, H_C, HEAD_DIM, N_STATE)
    else:
        assert b == LANES
        ya, s_rwkv = _rwkv_short(feats, t, st["rwkv"])
        yb, s_ret = _ret_short(qh, kh, cb, t, st["ret"])
        yc, s_ssm = _ssd_short(xdt, bm, cm, dssm, t, st["ssm"])

    x2 = _post(x, ya, bonus, ga, yb, cb, yc, xs, cc, p)
    x = _ffn(x2, p["norm_ffn"], p["wg"], p["wu"], p["wd"], router=p["router"], nfin=p["norm_final"])
    return x, (s_rwkv, ca3[:, -1], s_ret, s_ssm, conv_new)


def kernel(x_prompt, x_sample, state_rwkv, state_shift, state_ret, state_ssm, state_conv, norm_mix, w_in, rwkv_mu, rwkv_w0, rwkv_w_up, rwkv_a0, rwkv_a_up, rwkv_g_up, rwkv_k_k, rwkv_k_a, rwkv_r_k, rwkv_ln_w, rwkv_ln_b, ret_norm, ssm_conv_w, ssm_conv_b, ssm_dt_bias, ssm_a_log, ssm_d, ssm_norm, w_out, norm_ffn, ffn_w_gate, ffn_w_up, ffn_w_down, moe_router, moe_w_gate, moe_w_up, moe_w_down, norm_final):
    bp, tp, _ = x_prompt.shape
    bs, ts, _ = x_sample.shape
    depth = w_in.shape[0]
    assert tp % TM == 0 and tp % LC == 0 and (bs * ts) % TM == 0 and TM % ts == 0

    rope = _rope_tables(tp, ts)
    bd_a, bd_b = _block_diag_ones(W_A), _block_diag_ones(W_B)
    expand = np.zeros((LANES, W_C), np.float32)
    expand[np.arange(W_C) // HEAD_DIM, np.arange(W_C)] = 1.0
    expand = jnp.asarray(expand, BF16)
    row = lambda v: v.reshape(1, -1)
    pad_l = lambda v: jnp.pad(v, (0, LANES - v.shape[0])).reshape(1, LANES)

    xp = x_prompt.reshape(bp * tp, D_MODEL)
    xs = x_sample.reshape(bs * ts, D_MODEL)
    new_p, new_s = [], []
    for i in range(depth):
        j = i // 2
        p = dict(
            norm_mix=row(norm_mix[i]),
            w_in=jnp.pad(w_in[i], ((0, 0), (0, COLS_C_PAD - COLS_C))).astype(BF16),
            mu=row(rwkv_mu[i]), w0=row(rwkv_w0[i]), a0=row(rwkv_a0[i]), k_k=row(rwkv_k_k[i]),
            k_a=row(rwkv_k_a[i]), r_k=row(rwkv_r_k[i]),
            w_up=jnp.pad(rwkv_w_up[i], ((0, AAA_LORA), (0, 0))).astype(BF16),
            a_up=jnp.pad(rwkv_a_up[i], ((DECAY_LORA, 0), (0, 0))).astype(BF16),
            g_up=rwkv_g_up[i].astype(BF16), bd_a=bd_a, bd_b=bd_b,
            conv_w=ssm_conv_w[i], conv_b=row(ssm_conv_b[i]), dt_bias=pad_l(ssm_dt_bias[i]),
            a_log=pad_l(ssm_a_log[i]), expand=expand,
            ln_w=row(rwkv_ln_w[i]), ln_b=row(rwkv_ln_b[i]), ret_norm=row(ret_norm[i]),
            d_skip=row(jnp.repeat(ssm_d[i], HEAD_DIM)), ssm_norm=row(ssm_norm[i]), w_out=w_out[i].astype(BF16),
            norm_ffn=row(norm_ffn[i]), norm_final=row(norm_final) if i == depth - 1 else None)
        if i % 2 == 0:
            dff = ffn_w_gate.shape[-1] // 2
            p.update(router=None,
                     wg=ffn_w_gate[j].reshape(D_MODEL, 2, dff).transpose(1, 0, 2).astype(BF16),
                     wu=ffn_w_up[j].reshape(D_MODEL, 2, dff).transpose(1, 0, 2).astype(BF16),
                     wd=ffn_w_down[j].reshape(2, dff, D_MODEL).astype(BF16))
        else:
            p.update(router=jnp.pad(moe_router[j], ((0, 0), (0, LANES - N_EXPERTS))),
                     wg=moe_w_gate[j].astype(BF16), wu=moe_w_up[j].astype(BF16), wd=moe_w_down[j].astype(BF16))

        xp, st_p = _layer(xp, bp, p, None, rope, lambda t: t % (tp // TM))
        st = dict(rwkv=(state_rwkv, i), shift=state_shift[i], ret=(state_ret, i), ssm=(state_ssm, i),
                  conv=state_conv[i])
        xs, st_s = _layer(xs, bs, p, st, rope, lambda t: tp // TM)
        new_p.append(st_p)
        new_s.append(st_s)

    stack = lambda sts: tuple(jnp.stack(s) for s in zip(*sts))
    return (xp.reshape(bp, tp, D_MODEL), xs.reshape(bs, ts, D_MODEL)) + stack(new_p) + stack(new_s)
```

```python
import functools
import math

import numpy as np
import jax
import jax.numpy as jnp
from jax import lax
from jax.experimental import pallas as pl
from jax.experimental.pallas import tpu as pltpu

F32 = jnp.float32
BF16 = jnp.bfloat16
HIGHEST = lax.Precision.HIGHEST

LANES = 128
SUBLANES = 8
VMEM_LIMIT = 56 * 1024 * 1024

D_MODEL = 1024
HEAD_DIM = 64
H_A, H_B, H_C = 6, 4, 6
W_A, W_B, W_C = H_A * HEAD_DIM, H_B * HEAD_DIM, H_C * HEAD_DIM
DECAY_LORA, AAA_LORA, GATE_LORA = 64, 64, 128
COLS_A = 3 * W_A + DECAY_LORA + AAA_LORA + GATE_LORA
COLS_B = 4 * W_B
N_STATE, N_GROUPS, CONV_W = 128, 2, 4
CONV_DIM = W_C + 2 * N_GROUPS * N_STATE
COLS_C = W_C + CONV_DIM + H_C
COLS_C_PAD = 1408
ROPE_BASE = 10000.0
RMS_EPS = 1e-6
GN_EPS = 64e-5
GATED_NORM_EPS = 1e-5
N_EXPERTS = 8
PAST_LEN = 16384

TM = 512
MOE_ROWS = 256
MOE_CHUNK = 80
MOE_SUB = 4
LC = 256
H_PAD = 8
K_SPLIT = 2


def _dot(a, b):
    return jnp.dot(a.astype(BF16), b.astype(BF16), preferred_element_type=F32)


def _dot_hi(a, b):
    return jnp.dot(a, b, precision=HIGHEST, preferred_element_type=F32)


def _dot_select(a, sel, terms=2):
    out = None
    rest = a
    for _ in range(terms):
        piece = rest.astype(BF16)
        part = jnp.dot(piece, sel, preferred_element_type=F32)
        out = part if out is None else out + part
        rest = rest - piece.astype(F32)
    return out


def _sigmoid(x):
    return 1.0 / (1.0 + jnp.exp(-x))


def _softplus(x):
    return jnp.maximum(x, 0.0) + jnp.log1p(jnp.exp(-jnp.abs(x)))


def _rmsnorm(x, g, eps):
    return x * lax.rsqrt(jnp.mean(x * x, axis=-1, keepdims=True) + eps) * g


def _params(sem):
    return pltpu.CompilerParams(dimension_semantics=sem, vmem_limit_bytes=VMEM_LIMIT)


def _row_spec(width, col=0):
    return pl.BlockSpec((TM, width), lambda i, c=col: (i, c))


def _before_spec(width):
    return pl.BlockSpec((SUBLANES, width), lambda i: (jnp.maximum(i * (TM // SUBLANES) - 1, 0), 0))


def _full_spec(shape):
    nd = len(shape)
    return pl.BlockSpec(shape, lambda i, n=nd: (0,) * n)


def _shift_rows(x, before, j, first):
    rolled = pltpu.roll(x, j, 0)
    prev = jnp.where(first, 0.0, pltpu.roll(before, j, 0))
    row = lax.broadcasted_iota(jnp.int32, prev.shape, 0)
    top = jnp.where(row < j, prev, rolled[0:SUBLANES])
    return jnp.concatenate([top, rolled[SUBLANES:]], axis=0)


def _in_proj_kernel(x_ref, g_ref, w_ref, oa_ref, ob_ref, oc_ref):
    h = _rmsnorm(x_ref[...], g_ref[...], RMS_EPS).astype(BF16)
    oa_ref[...] = jnp.dot(h, w_ref[:, 0:COLS_A], preferred_element_type=F32)
    ob_ref[...] = jnp.dot(h, w_ref[:, COLS_A:COLS_A + COLS_B], preferred_element_type=F32)
    oc_ref[...] = jnp.dot(h, w_ref[:, COLS_A + COLS_B:], preferred_element_type=F32)


def _in_proj(x, g, w):
    n = x.shape[0]
    wtot = w.shape[1]
    return pl.pallas_call(
        _in_proj_kernel,
        grid=(n // TM,),
        in_specs=[_row_spec(D_MODEL), _full_spec((1, D_MODEL)), _full_spec((D_MODEL, wtot))],
        out_specs=[_row_spec(COLS_A), _row_spec(COLS_B), _row_spec(COLS_C_PAD)],
        out_shape=[jax.ShapeDtypeStruct((n, COLS_A), F32), jax.ShapeDtypeStruct((n, COLS_B), F32),
                   jax.ShapeDtypeStruct((n, COLS_C_PAD), F32)],
        compiler_params=_params(("parallel",)),
        name="in_proj",
    )(x, g, w)


def _rwkv_prep_kernel(c_ref, p_ref, mu_ref, w0_ref, a0_ref, kk_ref, ka_ref, rk_ref, wup_ref, aup_ref,
                      gup_ref, bd_ref, r_o, d_o, k_o, v_o, n_o, b_o, g_o, bonus_o, *, tiles_per_seq):
    c = c_ref[...]
    if tiles_per_seq:
        prev = _shift_rows(c, p_ref[...], 1, pl.program_id(0) % tiles_per_seq == 0)
    else:
        prev = p_ref[...]
    xm = c + (prev - c) * mu_ref[...]
    r = xm[:, 0:W_A]
    k = xm[:, W_A:2 * W_A]
    v = xm[:, 2 * W_A:3 * W_A]
    lora = xm[:, 3 * W_A:3 * W_A + DECAY_LORA + AAA_LORA]
    gd = xm[:, 3 * W_A + DECAY_LORA + AAA_LORA:]
    w = w0_ref[...] + _dot(jnp.tanh(lora), wup_ref[...])
    decay = jnp.exp(-math.exp(-0.5) * _sigmoid(w))
    a = _sigmoid(a0_ref[...] + _dot(lora, aup_ref[...]))
    g = _dot(_sigmoid(gd), gup_ref[...])
    bd = bd_ref[...]
    kk = k * kk_ref[...]
    kk = kk / jnp.maximum(jnp.sqrt(_dot_select(kk * kk, bd)), 1e-12)
    k2 = k * (1.0 + (a - 1.0) * ka_ref[...])
    for o_ref, val in ((r_o, r), (d_o, decay), (k_o, k2), (v_o, v), (n_o, -kk), (b_o, kk * a)):
        o_ref[:, 0:W_A] = val
        if o_ref.shape[1] > W_A:
            o_ref[:, W_A:] = jnp.zeros((TM, o_ref.shape[1] - W_A), F32)
    g_o[...] = g
    bonus_o[...] = _dot_select(r * k2 * rk_ref[...], bd) * v


def _rwkv_prep(ca, prev, p, tiles_per_seq):
    n = ca.shape[0]
    vec = _full_spec((1, W_A))
    lora_spec = _full_spec((DECAY_LORA + AAA_LORA, W_A))
    p_spec = _before_spec(COLS_A) if tiles_per_seq else _row_spec(COLS_A)
    w_scan = H_PAD * HEAD_DIM if tiles_per_seq else W_A
    widths = [w_scan] * 6 + [W_A] * 2
    return pl.pallas_call(
        functools.partial(_rwkv_prep_kernel, tiles_per_seq=tiles_per_seq),
        grid=(n // TM,),
        in_specs=[_row_spec(COLS_A), p_spec, _full_spec((1, COLS_A)), vec, vec, vec, vec, vec,
                  lora_spec, lora_spec, _full_spec((GATE_LORA, W_A)), _full_spec((W_A, W_A))],
        out_specs=[_row_spec(w) for w in widths],
        out_shape=[jax.ShapeDtypeStruct((n, w), F32) for w in widths],
        compiler_params=_params(("parallel",)),
        name="rwkv_prep",
    )(ca, ca if tiles_per_seq else prev, p["mu"], p["w0"], p["a0"], p["k_k"], p["k_a"], p["r_k"],
      p["w_up"], p["a_up"], p["g_up"], p["bd_a"])


def _rope_kernel(ang_ref, cos_o, sin_o):
    ang = ang_ref[...]
    lane = lax.broadcasted_iota(jnp.int32, ang.shape, 1)
    cos_o[...] = jnp.cos(ang)
    sin_o[...] = jnp.where((lane % HEAD_DIM) < (HEAD_DIM // 2), -jnp.sin(ang), jnp.sin(ang))


def _rope_tables(t_prompt, t_sample):
    theta = 1.0 / (ROPE_BASE ** jnp.linspace(0.0, 1.0, HEAD_DIM // 2, dtype=F32))
    pos = jnp.concatenate([jnp.arange(t_prompt, dtype=F32), PAST_LEN + (jnp.arange(TM) % t_sample).astype(F32)])
    ang = jnp.tile(pos[:, None] * theta[None, :], (1, W_B // (HEAD_DIM // 2)))
    n = ang.shape[0]
    return pl.pallas_call(
        _rope_kernel,
        grid=(n // TM,),
        in_specs=[_row_spec(W_B)],
        out_specs=[_row_spec(W_B)] * 2,
        out_shape=[jax.ShapeDtypeStruct((n, W_B), F32)] * 2,
        compiler_params=_params(("parallel",)),
        name="rope_tables",
    )(ang)


def _ret_prep_kernel(q_ref, k_ref, cos_ref, sin_ref, q_o, k_o):
    cos = cos_ref[...]
    sin = sin_ref[...]
    lane = lax.broadcasted_iota(jnp.int32, (TM, W_B), 1)
    first_half = (lane % HEAD_DIM) < (HEAD_DIM // 2)

    def rope(x):
        partner = jnp.where(first_half, pltpu.roll(x, W_B - HEAD_DIM // 2, 1), pltpu.roll(x, HEAD_DIM // 2, 1))
        return x * cos + partner * sin

    q_o[...] = rope(q_ref[...])
    k_o[...] = rope(k_ref[...]) * (HEAD_DIM ** -0.5)


def _ret_prep(cb, cos, sin, table_block):
    n = cb.shape[0]
    tab = pl.BlockSpec((TM, W_B), lambda i: (table_block(i), 0))
    return pl.pallas_call(
        _ret_prep_kernel,
        grid=(n // TM,),
        in_specs=[_row_spec(W_B, 0), _row_spec(W_B, 1), tab, tab],
        out_specs=[_row_spec(W_B)] * 2,
        out_shape=[jax.ShapeDtypeStruct((n, W_B), F32)] * 2,
        compiler_params=_params(("parallel",)),
        name="ret_prep",
    )(cb, cb, cos, sin)


def _ssm_prep_kernel(*refs, tiles_per_seq):
    n_shift_refs = 1 if tiles_per_seq else CONV_W - 1
    cc_ref = refs[0]
    shift_refs = refs[1:1 + n_shift_refs]
    cw_ref, cb_ref, dtb_ref, alog_ref, ex_ref, xdt_o, b_o, c_o, dec_o, xs_o, da_o = refs[1 + n_shift_refs:]
    cc = cc_ref[...]
    x0 = cc[:, W_C:W_C + CONV_DIM]
    if tiles_per_seq:
        before = shift_refs[0][:, W_C:W_C + CONV_DIM]
        first = pl.program_id(0) % tiles_per_seq == 0
        xs_prev = [_shift_rows(x0, before, j, first) for j in range(1, CONV_W)]
    else:
        xs_prev = [r[...] for r in shift_refs]
    cw = cw_ref[...]
    conv = x0 * cw[CONV_W - 1:CONV_W, :] + cb_ref[...]
    for j in range(1, CONV_W):
        conv = conv + xs_prev[j - 1] * cw[CONV_W - 1 - j:CONV_W - j, :]
    act = conv * _sigmoid(conv)
    xs = act[:, 0:W_C]
    dt = _softplus(cc[:, W_C + CONV_DIM:] + dtb_ref[...])
    da = dt * (-jnp.exp(alog_ref[...]))
    da_o[...] = da
    dec_o[...] = jnp.exp(da)
    xdt_o[...] = xs * _dot_select(dt, ex_ref[...])
    b_o[...] = act[:, W_C:W_C + N_GROUPS * N_STATE]
    c_o[...] = act[:, W_C + N_GROUPS * N_STATE:]
    xs_o[...] = xs


def _ssm_prep(cc, shifted, p, tiles_per_seq):
    n = cc.shape[0]
    gn = N_GROUPS * N_STATE
    if tiles_per_seq:
        shift_args, shift_specs = [cc], [_before_spec(COLS_C_PAD)]
    else:
        shift_args, shift_specs = list(shifted), [_row_spec(CONV_DIM)] * (CONV_W - 1)
    return pl.pallas_call(
        functools.partial(_ssm_prep_kernel, tiles_per_seq=tiles_per_seq),
        grid=(n // TM,),
        in_specs=[_row_spec(COLS_C_PAD)] + shift_specs + [
            _full_spec((CONV_W, CONV_DIM)), _full_spec((1, CONV_DIM)), _full_spec((1, LANES)),
            _full_spec((1, LANES)), _full_spec((LANES, W_C))],
        out_specs=[_row_spec(W_C), _row_spec(gn), _row_spec(gn), _row_spec(LANES), _row_spec(W_C),
                   _row_spec(LANES)],
        out_shape=[jax.ShapeDtypeStruct((n, W_C), F32), jax.ShapeDtypeStruct((n, gn), F32),
                   jax.ShapeDtypeStruct((n, gn), F32), jax.ShapeDtypeStruct((n, LANES), F32),
                   jax.ShapeDtypeStruct((n, W_C), F32), jax.ShapeDtypeStruct((n, LANES), F32)],
        compiler_params=_params(("parallel",)),
        name="ssm_prep",
    )(cc, *shift_args, p["conv_w"], p["conv_b"], p["dt_bias"], p["a_log"], p["expand"])


def _scan_kernel(*refs, mode, ni, tt_len, n_tt):
    if mode == "rwkv":
        q_ref, d_ref, a_ref, b_ref, n_ref, a2_ref, s0_ref, y_ref, so_ref, st = refs
    else:
        q_ref, d_ref, a_ref, b_ref, s0_ref, y_ref, so_ref, st = refs
    head = pl.program_id(0)
    tt = pl.program_id(1)

    @pl.when(tt == 0)
    def _():
        st[...] = s0_ref[...]

    def row(ref, t, i):
        return ref[t, pl.ds(i, 1), :]

    def step(t, carry):
        bv = b_ref[t]
        if mode == "rwkv":
            sa_parts = [jnp.zeros_like(bv), jnp.zeros_like(bv)]
            for i in range(ni):
                sa_parts[i % 2] = sa_parts[i % 2] + st[i] * row(n_ref, t, i)
            sa = sa_parts[0] + sa_parts[1]
        elif mode == "ssd":
            d = d_ref[t, pl.ds(head, 1), :]
        else:
            d = d_ref[pl.ds(head, 1), :]
        y_parts = [jnp.zeros_like(bv), jnp.zeros_like(bv)]
        for i in range(ni):
            if mode == "rwkv":
                s = st[i] * row(d_ref, t, i) + row(a_ref, t, i) * bv + row(a2_ref, t, i) * sa
            else:
                s = st[i] * d + row(a_ref, t, i) * bv
            st[i] = s
            y_parts[i % 2] = y_parts[i % 2] + s * row(q_ref, t, i)
        y_ref[t] = y_parts[0] + y_parts[1]
        return carry

    lax.fori_loop(0, tt_len, step, 0)

    @pl.when(tt == n_tt - 1)
    def _():
        so_ref[...] = st[...]


def _scan(mode, n_heads, ni, nj, q, d, a, b, s0, n=None, a2=None):
    t_len = b[0].shape[0]

    def rows(op, width):
        arr, block_of = op
        return arr, pl.BlockSpec((t_len, width, LANES), lambda h, t, f=block_of: (0, f(h), 0))

    ops = [rows(q, ni)]
    if mode == "rwkv":
        ops.append(rows(d, ni))
    elif mode == "ssd":
        ops.append((d, pl.BlockSpec((t_len, SUBLANES, LANES), lambda h, t: (0, 0, 0))))
    else:
        ops.append((d, pl.BlockSpec((SUBLANES, LANES), lambda h, t: (0, 0))))
    ops += [rows(a, ni), rows(b, nj)]
    if mode == "rwkv":
        ops += [rows(n, ni), rows(a2, ni)]
    s_spec = pl.BlockSpec((ni, nj, LANES), lambda h, t: (h, 0, 0))
    y_spec = pl.BlockSpec((t_len, nj, LANES), lambda h, t: (0, h, 0))
    return pl.pallas_call(
        functools.partial(_scan_kernel, mode=mode, ni=ni, tt_len=t_len, n_tt=1),
        grid=(n_heads, 1),
        in_specs=[spec for _, spec in ops] + [s_spec],
        out_specs=[y_spec, s_spec],
        out_shape=[jax.ShapeDtypeStruct((t_len, n_heads * nj, LANES), F32),
                   jax.ShapeDtypeStruct((n_heads * ni, nj, LANES), F32)],
        scratch_shapes=[pltpu.VMEM((ni, nj, LANES), F32)],
        compiler_params=_params(("parallel", "arbitrary")),
        name="scan_" + mode,
    )(*[arr for arr, _ in ops], s0)


def _rwkv_long_kernel(q_ref, d_ref, a_ref, b_ref, n1_ref, a2_ref, y_ref, so_ref, st, sa_s, dots_s, rows_s, *,
                      ni, tt_len, n_tt):
    tt = pl.program_id(1)

    @pl.when(tt == 0)
    def _():
        st[...] = jnp.zeros_like(st)
        sa_s[...] = jnp.zeros_like(sa_s)

    refs_i = (q_ref, d_ref, a_ref, n1_ref, a2_ref)
    pitch = tt_len + SUBLANES
    for k, ref in enumerate(refs_i):
        for i in range(ni):
            rows_s[i, k * pitch:k * pitch + tt_len, :] = ref[i]

    op_q, op_d, op_a, op_n1, op_a2 = range(len(refs_i))

    def row(k, t, i):
        return rows_s[i, pl.ds(k * pitch + t, 1), :]

    def fold(x):
        return x + pltpu.roll(x, LANES // 2, 1)

    for k, (u_ref, w_ref) in enumerate(((a_ref, q_ref), (a2_ref, q_ref), (a_ref, n1_ref), (a2_ref, n1_ref))):
        acc = u_ref[0] * w_ref[0]
        for i in range(1, ni):
            acc = acc + u_ref[i] * w_ref[i]
        dots_s[k] = fold(acc)

    def step(t, sa):
        bv = b_ref[t]
        aq, a2q, an, a2n = [dots_s[k, pl.ds(t, 1), :] for k in range(4)]
        y0 = [jnp.zeros_like(bv), jnp.zeros_like(bv)]
        n0 = [jnp.zeros_like(bv), jnp.zeros_like(bv)]
        for i in range(ni):
            sd = st[i] * row(op_d, t, i)
            st[i] = sd
            y0[i % 2] = y0[i % 2] + sd * row(op_q, t, i)
            n0[i % 2] = n0[i % 2] + sd * row(op_n1, t, i)
        y_ref[t] = fold(y0[0] + y0[1]) + bv * aq + sa * a2q
        sa_next = fold(n0[0] + n0[1]) + bv * an + sa * a2n
        for i in range(ni):
            st[i] = st[i] + row(op_a, t, i) * bv + row(op_a2, t, i) * sa
        return sa_next

    sa_s[...] = lax.fori_loop(0, tt_len, step, sa_s[...])

    @pl.when(tt == n_tt - 1)
    def _():
        so_ref[...] = st[...]


def _rwkv_long_scan(q, d, a, b, n1, a2, tt_len=64):
    ni, t_len, _ = q.shape
    nj = b.shape[1]
    n_tt = t_len // tt_len
    i_spec = pl.BlockSpec((ni, tt_len, LANES), lambda g, t: (0, t, 0))
    j_spec = pl.BlockSpec((tt_len, nj, LANES), lambda g, t: (t, 0, 0))
    s_spec = pl.BlockSpec((ni, nj, LANES), lambda g, t: (0, 0, 0))
    return pl.pallas_call(
        functools.partial(_rwkv_long_kernel, ni=ni, tt_len=tt_len, n_tt=n_tt),
        grid=(1, n_tt),
        in_specs=[i_spec, i_spec, i_spec, j_spec, i_spec, i_spec],
        out_specs=[j_spec, s_spec],
        out_shape=[jax.ShapeDtypeStruct((t_len, nj, LANES), F32), jax.ShapeDtypeStruct((ni, nj, LANES), F32)],
        scratch_shapes=[pltpu.VMEM((ni, nj, LANES), F32), pltpu.VMEM((nj, LANES), F32),
                        pltpu.VMEM((4, tt_len, LANES), F32),
                        pltpu.VMEM((ni, 5 * (tt_len + SUBLANES), LANES), F32)],
        compiler_params=_params(("parallel", "arbitrary")),
        name="scan_rwkv_long",
    )(q, d, a, b, n1, a2)


TR = 128
HALF = LANES // K_SPLIT


Y_PITCH = LANES + SUBLANES


def _to_lanes_kernel(*refs, n_ops, rows, per_half):
    x_refs, o_refs, y2d = refs[:n_ops], refs[n_ops:2 * n_ops], refs[2 * n_ops]
    n_b = x_refs[0].shape[0]
    for x_ref, o_ref in zip(x_refs, o_refs):
        for b in range(n_b):
            xt = x_ref[b].T
            for h in range(H_PAD):
                for s in range(K_SPLIT):
                    lane = s * HALF + b * H_PAD + h
                    f0 = h * HEAD_DIM + (s * rows if per_half else 0)
                    y2d[pl.ds(lane, rows, stride=Y_PITCH), :] = xt[f0:f0 + rows, :]
        for r in range(rows):
            slab = y2d[r * Y_PITCH:r * Y_PITCH + LANES, :].T
            if per_half:
                o_ref[r] = slab
            else:
                o_ref[pl.ds(r, TR, stride=rows), :] = slab


def _long_to_lanes(xs, b, rows, per_half):
    n_ops = len(xs)
    t = xs[0].shape[0] // b
    width = H_PAD * HEAD_DIM
    if per_half:
        o_spec = pl.BlockSpec((rows, TR, LANES), lambda i: (0, i, 0))
        o_shape = jax.ShapeDtypeStruct((rows, t, LANES), F32)
    else:
        o_spec = pl.BlockSpec((TR * rows, LANES), lambda i: (i, 0))
        o_shape = jax.ShapeDtypeStruct((t * rows, LANES), F32)
    outs = pl.pallas_call(
        functools.partial(_to_lanes_kernel, n_ops=n_ops, rows=rows, per_half=per_half),
        grid=(t // TR,),
        in_specs=[pl.BlockSpec((b, TR, width), lambda i: (0, i, 0))] * n_ops,
        out_specs=[o_spec] * n_ops,
        out_shape=[o_shape] * n_ops,
        scratch_shapes=[pltpu.VMEM((rows * Y_PITCH, TR), F32)],
        compiler_params=_params(("parallel",)),
        name="to_lanes",
    )(*[x.reshape(b, t, width) for x in xs])
    return outs if per_half else [o.reshape(t, rows, LANES) for o in outs]


def _from_lanes_kernel(y_ref, o_ref, z2d):
    for v in range(HEAD_DIM):
        z2d[pl.ds(v, LANES, stride=HEAD_DIM), :] = y_ref[pl.ds(v, TR, stride=HEAD_DIM), :].T
    for b in range(o_ref.shape[0]):
        r0 = b * H_PAD * HEAD_DIM
        o_ref[b] = z2d[r0:r0 + W_A, :].T


def _long_from_lanes(y, b):
    t = y.shape[0]
    out = pl.pallas_call(
        _from_lanes_kernel,
        grid=(t // TR,),
        in_specs=[pl.BlockSpec((TR * HEAD_DIM, LANES), lambda i: (i, 0))],
        out_specs=pl.BlockSpec((b, TR, W_A), lambda i: (0, i, 0)),
        out_shape=jax.ShapeDtypeStruct((b, t, W_A), F32),
        scratch_shapes=[pltpu.VMEM((LANES * HEAD_DIM, TR), F32)],
        compiler_params=_params(("parallel",)),
        name="from_lanes",
    )(y.reshape(t * HEAD_DIM, LANES))
    return out.reshape(b * t, W_A)


def _ret_chunk_kernel(q_ref, k_ref, v_ref, y_ref, s_ref, st):
    c = pl.program_id(1)

    @pl.when(c == 0)
    def _():
        st[...] = jnp.zeros_like(st)

    q = q_ref[...]
    k = k_ref[...]
    vb = v_ref[...].astype(BF16)
    kb = k.astype(BF16)
    row = lax.broadcasted_iota(jnp.int32, (LC, W_B), 0)
    head = lax.broadcasted_iota(jnp.int32, (LC, W_B), 1) // HEAD_DIM
    log_g = [math.log(1.0 - 2.0 ** (-5.0 - h)) for h in range(H_B)]
    lg = jnp.full((LC, W_B), log_g[0], F32)
    for h in range(1, H_B):
        lg = jnp.where(head == h, log_g[h], lg)
    rowf = row.astype(F32)
    diff = (lax.broadcasted_iota(jnp.int32, (LC, LC), 0) - lax.broadcasted_iota(jnp.int32, (LC, LC), 1))
    causal = diff >= 0
    difff = jnp.maximum(diff, 0).astype(F32)

    out = jnp.dot((q * jnp.exp(lg * (rowf + 1.0))).astype(BF16), st[...].astype(BF16), preferred_element_type=F32)
    for h in range(H_B):
        qm = jnp.where(head == h, q, 0.0).astype(BF16)
        s = lax.dot_general(qm, kb, (((1,), (1,)), ((), ())), preferred_element_type=F32)
        p = jnp.where(causal, s * jnp.exp(log_g[h] * difff), 0.0).astype(BF16)
        out = out + jnp.where(head == h, jnp.dot(p, vb, preferred_element_type=F32), 0.0)
    y_ref[...] = out

    kt = (k * jnp.exp(lg * (LC - 1.0 - rowf))).T.astype(BF16)
    kv = jnp.dot(kt, vb, preferred_element_type=F32)
    r2 = lax.broadcasted_iota(jnp.int32, (W_B, W_B), 0) // HEAD_DIM
    c2 = lax.broadcasted_iota(jnp.int32, (W_B, W_B), 1) // HEAD_DIM
    cdec = jnp.full((W_B, W_B), math.exp(log_g[0] * LC), F32)
    for h in range(1, H_B):
        cdec = jnp.where(r2 == h, math.exp(log_g[h] * LC), cdec)
    st[...] = st[...] * cdec + jnp.where(r2 == c2, kv, 0.0)
    s_ref[0] = st[...]


def _ret_chunk(q, k, cb, n_seq, t_len):
    n_c = t_len // LC
    rows = lambda col: pl.BlockSpec((LC, W_B), lambda b, c, col=col: (b * n_c + c, col))
    return pl.pallas_call(
        _ret_chunk_kernel,
        grid=(n_seq, n_c),
        in_specs=[rows(0), rows(0), rows(2)],
        out_specs=[rows(0), pl.BlockSpec((1, W_B, W_B), lambda b, c: (b, 0, 0))],
        out_shape=[jax.ShapeDtypeStruct((n_seq * t_len, W_B), F32), jax.ShapeDtypeStruct((n_seq, W_B, W_B), F32)],
        scratch_shapes=[pltpu.VMEM((W_B, W_B), F32)],
        compiler_params=_params(("parallel", "arbitrary")),
        name="ret_chunk",
    )(q, k, cb)


def _ssd_chunk_kernel(x_ref, b_ref, c_ref, da_ref, ex_ref, y_ref, h_ref, st):
    ci = pl.program_id(1)

    @pl.when(ci == 0)
    def _():
        st[...] = jnp.zeros_like(st)

    gn = N_STATE
    half = W_C // N_GROUPS
    ex = ex_ref[...]
    x = x_ref[...]
    xb = x.astype(BF16)
    ri = lax.broadcasted_iota(jnp.int32, (LC, LC), 0)
    cj = lax.broadcasted_iota(jnp.int32, (LC, LC), 1)
    causal = ri >= cj
    cum = _dot_hi(causal.astype(F32), da_ref[...])
    cum_e = _dot_select(cum, ex, 3)
    last_e = cum_e[LC - 1:LC, :]
    cum_t = cum.T
    head = lax.broadcasted_iota(jnp.int32, (LC, W_C), 1) // HEAD_DIM
    bg = [b_ref[:, g * gn:(g + 1) * gn].astype(BF16) for g in range(N_GROUPS)]
    cg = [c_ref[:, g * gn:(g + 1) * gn].astype(BF16) for g in range(N_GROUPS)]
    cb = [lax.dot_general(cg[g], bg[g], (((1,), (1,)), ((), ())), preferred_element_type=F32)
          for g in range(N_GROUPS)]

    hb = st[...].astype(BF16)
    ys = [lax.dot_general(cg[g], hb, (((1,), (1,)), ((), ())), preferred_element_type=F32)
          for g in range(N_GROUPS)]
    y = jnp.where(head < H_C // N_GROUPS, ys[0], ys[1]) * jnp.exp(cum_e)
    for h in range(H_C):
        seg = cum[:, h:h + 1] - cum_t[h:h + 1, :]
        p = (cb[h // (H_C // N_GROUPS)] * jnp.exp(jnp.where(causal, seg, -jnp.inf))).astype(BF16)
        y = y + jnp.where(head == h, jnp.dot(p, xb, preferred_element_type=F32), 0.0)
    y_ref[...] = y

    xt = (x * jnp.exp(last_e - cum_e)).T.astype(BF16)
    upd = [jnp.dot(xt, bg[g], preferred_element_type=F32) for g in range(N_GROUPS)]
    rowi = lax.broadcasted_iota(jnp.int32, (W_C, gn), 0)
    sel = lax.broadcasted_iota(jnp.int32, (W_C, LANES), 1) == lax.broadcasted_iota(jnp.int32, (W_C, LANES), 0) // HEAD_DIM
    tot = jnp.sum(jnp.where(sel, cum[LC - 1:LC, :], 0.0), axis=1, keepdims=True)
    st[...] = st[...] * jnp.exp(tot) + jnp.where(rowi < half, upd[0], upd[1])
    h_ref[0] = st[...]


def _ssd_chunk(xdt, bm, cm, da, ex, n_seq, t_len):
    n_c = t_len // LC
    gn = N_GROUPS * N_STATE
    rows = lambda w: pl.BlockSpec((LC, w), lambda b, c: (b * n_c + c, 0))
    return pl.pallas_call(
        _ssd_chunk_kernel,
        grid=(n_seq, n_c),
        in_specs=[rows(W_C), rows(gn), rows(gn), rows(LANES), pl.BlockSpec((LANES, W_C), lambda b, c: (0, 0))],
        out_specs=[rows(W_C), pl.BlockSpec((1, W_C, N_STATE), lambda b, c: (b, 0, 0))],
        out_shape=[jax.ShapeDtypeStruct((n_seq * t_len, W_C), F32),
                   jax.ShapeDtypeStruct((n_seq, W_C, N_STATE), F32)],
        scratch_shapes=[pltpu.VMEM((W_C, N_STATE), F32)],
        compiler_params=_params(("parallel", "arbitrary")),
        name="ssd_chunk",
    )(xdt, bm, cm, da, ex)


def _post_kernel(x_ref, ya_ref, bonus_ref, ga_ref, ob_ref, gb_ref, yc_ref, xs_ref, z_ref, lnw_ref, lnb_ref,
                 rn_ref, dsk_ref, sn_ref, bda_ref, bdb_ref, wo_ref, out_ref):
    inv_hd = 1.0 / HEAD_DIM
    y = ya_ref[...]
    bda = bda_ref[...]
    mean = _dot_select(y, bda) * inv_hd
    yd = y - mean
    var = _dot_select(yd * yd, bda) * inv_hd
    ya = (yd * lax.rsqrt(var + GN_EPS) * lnw_ref[...] + lnb_ref[...] + bonus_ref[...]) * ga_ref[...]
    o = ob_ref[...]
    ms = _dot_select(o * o, bdb_ref[...]) * inv_hd
    gb = gb_ref[...]
    yb = (gb * _sigmoid(gb)) * (o * lax.rsqrt(ms + RMS_EPS) * rn_ref[...])
    z = z_ref[...]
    yc = (yc_ref[...] + dsk_ref[...] * xs_ref[...]) * (z * _sigmoid(z))
    yc = _rmsnorm(yc, sn_ref[...], GATED_NORM_EPS)
    out_ref[...] = (x_ref[...] + _dot(ya, wo_ref[0:W_A, :]) + _dot(yb, wo_ref[W_A:W_A + W_B, :])
                    + _dot(yc, wo_ref[W_A + W_B:, :]))


def _post(x, ya, bonus, ga, ob, cb, yc, xs, cc, p):
    n = x.shape[0]
    va = _full_spec((1, W_A))
    return pl.pallas_call(
        _post_kernel,
        grid=(n // TM,),
        in_specs=[_row_spec(D_MODEL), _row_spec(W_A), _row_spec(W_A), _row_spec(W_A), _row_spec(W_B),
                  _row_spec(W_B, 3), _row_spec(W_C), _row_spec(W_C), _row_spec(W_C, 0), va, va,
                  _full_spec((1, W_B)), va, va, _full_spec((W_A, W_A)), _full_spec((W_B, W_B)),
                  _full_spec((D_MODEL, D_MODEL))],
        out_specs=_row_spec(D_MODEL),
        out_shape=jax.ShapeDtypeStruct((n, D_MODEL), F32),
        compiler_params=_params(("parallel",)),
        name="post_outproj",
    )(x, ya, bonus, ga, ob, cb, yc, xs, cc, p["ln_w"], p["ln_b"], p["ret_norm"], p["d_skip"], p["ssm_norm"],
      p["bd_a"], p["bd_b"], p["w_out"])


def _ffn_kernel(*refs, gated, final_norm, n_e, n_sub):
    refs = list(refs)
    x_ref, nf_ref = refs[:2]
    pos = 2
    if gated:
        router_ref = refs[pos]
        pos += 1
    wg_ref, wu_ref, wd_ref = refs[pos:pos + 3]
    pos += 3
    if final_norm:
        nfin_ref = refs[pos]
        pos += 1
    out_ref = refs[pos]
    hb_s, acc_s = refs[pos + 1:pos + 3]
    if gated:
        gate_s, asg_s, pos_s, asg_t, pos_t = refs[pos + 3:pos + 8]
    e = pl.program_id(1)
    sr = x_ref.shape[0] // n_sub
    subs = [(slice(s * sr, (s + 1) * sr), slice(s * LANES, (s + 1) * LANES)) for s in range(n_sub)]

    @pl.when(e == 0)
    def _():
        acc_s[...] = jnp.zeros_like(acc_s)
        for rs, ts in subs:
            h = _rmsnorm(x_ref[rs, :], nf_ref[...], RMS_EPS)
            hb_s[rs, :] = h.astype(BF16)
            if gated:
                lane = lax.broadcasted_iota(jnp.int32, (sr, LANES), 1)
                logits = jnp.where(lane < N_EXPERTS, _dot_hi(h, router_ref[...]), -jnp.inf)
                p = jnp.exp(logits - jnp.max(logits, axis=-1, keepdims=True))
                p = p / jnp.sum(p, axis=-1, keepdims=True)
                p1 = jnp.max(p, axis=-1, keepdims=True)
                i1 = jnp.min(jnp.where(p == p1, lane, LANES), axis=-1, keepdims=True)
                rest = jnp.where(lane == i1, -1.0, p)
                p2 = jnp.max(rest, axis=-1, keepdims=True)
                i2 = jnp.min(jnp.where(rest == p2, lane, LANES), axis=-1, keepdims=True)
                gate_s[rs, :] = jnp.where(lane == i1, p1, jnp.where(lane == i2, p2, 0.0)) / (p1 + p2)
                assigned = jnp.where((lane == i1) | (lane == i2), 1.0, 0.0)
                earlier = (lax.broadcasted_iota(jnp.int32, (sr, sr), 0)
                           > lax.broadcasted_iota(jnp.int32, (sr, sr), 1))
                rank = jnp.dot(earlier.astype(BF16), assigned.astype(BF16), preferred_element_type=F32)
                asg_s[rs, :] = assigned
                pos_s[rs, :] = rank
                asg_t[ts, :] = assigned.T
                pos_t[ts, :] = rank.T

    if not gated:
        hb = hb_s[...]
        g = jnp.dot(hb, wg_ref[0], preferred_element_type=F32)
        u = jnp.dot(hb, wu_ref[0], preferred_element_type=F32)
        acc_s[...] += _dot(g * _sigmoid(g) * u, wd_ref[0])
    else:
        lane = lax.broadcasted_iota(jnp.int32, (sr, LANES), 1)
        mine = lane == e
        col = lambda ref, rs: jnp.sum(jnp.where(mine, ref[rs, :], 0.0), axis=-1, keepdims=True)
        cols = [(col(gate_s, rs), col(asg_s, rs), col(pos_s, rs)) for rs, _ in subs]
        rws = [(asg_t[pl.ds(s * LANES + e, 1), :], pos_t[pl.ds(s * LANES + e, 1), :])
               for s in range(n_sub)]
        most = functools.reduce(jnp.maximum, [jnp.sum(asg_r) for asg_r, _ in rws])
        ch = MOE_CHUNK
        n_chunks = (most.astype(jnp.int32) + ch - 1) // ch

        def chunk(c, carry):
            base = (c * ch).astype(F32)
            slot_r = lax.broadcasted_iota(jnp.int32, (ch, sr), 0).astype(F32) + base
            picked = []
            for (rs, _), (asg_r, pos_r) in zip(subs, rws):
                pick = jnp.where((pos_r == slot_r) & (asg_r > 0.0), 1.0, 0.0).astype(BF16)
                picked.append(jnp.dot(pick, hb_s[rs, :], preferred_element_type=F32).astype(BF16))
            rows = jnp.concatenate(picked, axis=0)
            g = jnp.dot(rows, wg_ref[0], preferred_element_type=F32)
            u = jnp.dot(rows, wu_ref[0], preferred_element_type=F32)
            o = _dot(g * _sigmoid(g) * u, wd_ref[0]).astype(BF16)
            slot_c = lax.broadcasted_iota(jnp.int32, (sr, ch), 1).astype(F32) + base
            for s, ((rs, _), (gate_c, asg_c, pos_c)) in enumerate(zip(subs, cols)):
                place = jnp.where((pos_c == slot_c) & (asg_c > 0.0), 1.0, 0.0).astype(BF16)
                back = jnp.dot(place, o[s * ch:(s + 1) * ch, :], preferred_element_type=F32)
                acc_s[rs, :] += gate_c * back
            return carry

        lax.fori_loop(0, n_chunks, chunk, 0)

    @pl.when(e == n_e - 1)
    def _():
        y = x_ref[...] + acc_s[...]
        if final_norm:
            y = _rmsnorm(y, nfin_ref[...], RMS_EPS)
        out_ref[...] = y


def _ffn(x, nf, wg, wu, wd, router=None, nfin=None):
    n = x.shape[0]
    n_e, _, dff = wg.shape
    gated = router is not None
    final_norm = nfin is not None
    n_sub, sub_rows = (MOE_SUB, MOE_ROWS) if gated else (1, TM)
    rows = n_sub * sub_rows
    assert n % rows == 0
    vec = pl.BlockSpec((1, D_MODEL), lambda i, e: (0, 0))
    args = [x, nf]
    specs = [pl.BlockSpec((rows, D_MODEL), lambda i, e: (i, 0)), vec]
    if gated:
        args.append(router)
        specs.append(pl.BlockSpec((D_MODEL, LANES), lambda i, e: (0, 0)))
    args += [wg, wu, wd]
    specs += [pl.BlockSpec((1, D_MODEL, dff), lambda i, e: (e, 0, 0)),
              pl.BlockSpec((1, D_MODEL, dff), lambda i, e: (e, 0, 0)),
              pl.BlockSpec((1, dff, D_MODEL), lambda i, e: (e, 0, 0))]
    if final_norm:
        args.append(nfin)
        specs.append(vec)
    scratch = [pltpu.VMEM((rows, D_MODEL), BF16), pltpu.VMEM((rows, D_MODEL), F32)]
    if gated:
        scratch += [pltpu.VMEM((rows, LANES), F32)] * 3 + [pltpu.VMEM((n_sub * LANES, sub_rows), F32)] * 2
    return pl.pallas_call(
        functools.partial(_ffn_kernel, gated=gated, final_norm=final_norm, n_e=n_e, n_sub=n_sub),
        grid=(n // rows, n_e),
        in_specs=specs,
        out_specs=pl.BlockSpec((rows, D_MODEL), lambda i, e: (i, 0)),
        out_shape=jax.ShapeDtypeStruct((n, D_MODEL), F32),
        scratch_shapes=scratch,
        compiler_params=_params(("parallel", "arbitrary")),
        name="moe" if gated else "ffn",
    )(*args)


def _short_to_lanes_kernel(*refs, n_ops, t_len):
    for x_ref, o_ref in zip(refs[:n_ops], refs[n_ops:]):
        for t in range(t_len):
            o_ref[t] = x_ref[pl.ds(t, LANES, stride=t_len), :].T


def _short_to_lanes(xs, t_len):
    n_chunks = [width // LANES for _, _, width in xs]
    grid = max(n_chunks)
    clamp = lambda c, n: jnp.minimum(c, n - 1)
    outs = pl.pallas_call(
        functools.partial(_short_to_lanes_kernel, n_ops=len(xs), t_len=t_len),
        grid=(grid,),
        in_specs=[pl.BlockSpec((LANES * t_len, LANES), lambda c, c0=col // LANES, n=n: (0, c0 + clamp(c, n)))
                  for (_, col, _), n in zip(xs, n_chunks)],
        out_specs=[pl.BlockSpec((t_len, LANES, LANES), lambda c, n=n: (0, clamp(c, n), 0)) for n in n_chunks],
        out_shape=[jax.ShapeDtypeStruct((t_len, width, LANES), F32) for _, _, width in xs],
        compiler_params=_params(("arbitrary",)),
        name="short_to_lanes",
    )(*[x for x, _, _ in xs])
    return outs


def _short_from_lanes_kernel(y_ref, o_ref, *, t_len):
    for t in range(t_len):
        o_ref[pl.ds(t, LANES, stride=t_len), :] = y_ref[t].T


def _short_from_lanes(y):
    t_len, w, _ = y.shape
    return pl.pallas_call(
        functools.partial(_short_from_lanes_kernel, t_len=t_len),
        grid=(w // LANES,),
        in_specs=[pl.BlockSpec((t_len, LANES, LANES), lambda c: (0, c, 0))],
        out_specs=pl.BlockSpec((LANES * t_len, LANES), lambda c: (0, c)),
        out_shape=jax.ShapeDtypeStruct((LANES * t_len, w), F32),
        compiler_params=_params(("parallel",)),
        name="short_from_lanes",
    )(y)


def _state_to_lanes_kernel(x_ref, o_ref, *, ni, nj, j_first):
    xt = x_ref[...].T
    if j_first:
        for j in range(nj):
            o_ref[pl.ds(j, ni, stride=nj), :] = xt[j * ni:(j + 1) * ni, :]
    else:
        o_ref[...] = xt


def _state_to_lanes(s_all, layer, ni, nj, j_first):
    depth, _, h = s_all.shape[:3]
    out = pl.pallas_call(
        functools.partial(_state_to_lanes_kernel, ni=ni, nj=nj, j_first=j_first),
        grid=(h,),
        in_specs=[pl.BlockSpec((LANES, ni * nj), lambda i: (layer, i))],
        out_specs=pl.BlockSpec((ni * nj, LANES), lambda i: (i, 0)),
        out_shape=jax.ShapeDtypeStruct((h * ni * nj, LANES), F32),
        compiler_params=_params(("parallel",)),
        name="state_to_lanes",
    )(s_all.reshape(depth * LANES, h * ni * nj))
    return out.reshape(h * ni, nj, LANES)


def _state_from_lanes_kernel(s_ref, o_ref, tmp, *, ni, nj, j_first):
    if j_first:
        for j in range(nj):
            tmp[j * ni:(j + 1) * ni, :] = s_ref[pl.ds(j, ni, stride=nj), :]
        o_ref[...] = tmp[...].T
    else:
        o_ref[...] = s_ref[...].T


def _state_from_lanes(s, n_heads, ni, nj, j_first):
    out = pl.pallas_call(
        functools.partial(_state_from_lanes_kernel, ni=ni, nj=nj, j_first=j_first),
        grid=(n_heads,),
        in_specs=[pl.BlockSpec((ni * nj, LANES), lambda i: (i, 0))],
        out_specs=pl.BlockSpec((LANES, ni * nj), lambda i: (0, i)),
        out_shape=jax.ShapeDtypeStruct((LANES, n_heads * ni * nj), F32),
        scratch_shapes=[pltpu.VMEM((ni * nj, LANES), F32)],
        compiler_params=_params(("parallel",)),
        name="state_from_lanes",
    )(s.reshape(n_heads * ni * nj, LANES))
    return out.reshape((LANES, n_heads, nj, ni) if j_first else (LANES, n_heads, ni, nj))


def _rwkv_long(feats, b):
    r, dec, k2, v, nkk, beta = feats
    t = r.shape[0] // b
    kl = HEAD_DIM // K_SPLIT
    assert b * H_PAD == HALF and t % TR == 0
    nkk_next = jnp.concatenate([nkk[1:], jnp.zeros((1, nkk.shape[1]), F32)], axis=0)
    q, d, a, n1, a2 = _long_to_lanes([r, dec, k2, nkk_next, beta], b, kl, True)
    (vv,) = _long_to_lanes([v], b, HEAD_DIM, False)
    y, s = _rwkv_long_scan(q, d, a, vv, n1, a2)
    s = s.reshape(kl, HEAD_DIM, K_SPLIT, b, H_PAD)[..., :H_A]
    return _long_from_lanes(y, b), jnp.transpose(s, (3, 4, 1, 2, 0)).reshape(b, H_A, HEAD_DIM, HEAD_DIM)


def _rwkv_short(feats, t_len, s0):
    r, dec, k2, v, nkk, beta = _short_to_lanes([(f, 0, W_A) for f in feats], t_len)
    hd = HEAD_DIM
    own = lambda h: h
    y, s = _scan("rwkv", H_A, hd, hd, (r, own), (dec, own), (k2, own), (v, own),
                 _state_to_lanes(*s0, hd, hd, True), n=(nkk, own), a2=(beta, own))
    return _short_from_lanes(y), _state_from_lanes(s, H_A, hd, hd, True)


def _ret_short(qh, kh, cb, t_len, s0):
    q, k, v = _short_to_lanes([(qh, 0, W_B), (kh, 0, W_B), (cb, 2 * W_B, W_B)], t_len)
    hd = HEAD_DIM
    gam = np.ones((SUBLANES, LANES), np.float32)
    gam[:H_B] = (1.0 - 2.0 ** (-5.0 - np.arange(H_B, dtype=np.float64)))[:, None]
    own = lambda h: h
    y, s = _scan("ret", H_B, hd, hd, (q, own), jnp.asarray(gam), (k, own), (v, own),
                 _state_to_lanes(*s0, hd, hd, False))
    return _short_from_lanes(y), _state_from_lanes(s, H_B, hd, hd, False)


def _ssd_short(xdt, bm, cm, dssm, t_len, s0):
    gn = N_GROUPS * N_STATE
    x, bl, cl, dl = _short_to_lanes([(xdt, 0, W_C), (bm, 0, gn), (cm, 0, gn), (dssm, 0, LANES)], t_len)
    group = lambda h: h // (H_C // N_GROUPS)
    y, s = _scan("ssd", H_C, N_STATE, HEAD_DIM, (cl, group), dl, (bl, group), (x, lambda h: h),
                 _state_to_lanes(*s0, N_STATE, HEAD_DIM, True))
    return _short_from_lanes(y), _state_from_lanes(s, H_C, N_STATE, HEAD_DIM, True)


def _block_diag_ones(width):
    idx = np.arange(width) // HEAD_DIM
    return jnp.asarray((idx[:, None] == idx[None, :]).astype(np.float32), BF16)


def _shifted(x, prev_rows, shift):
    b, t, c = x.shape
    p = prev_rows.shape[1]
    full = jnp.concatenate([prev_rows, x], axis=1)
    return full[:, p - shift:p - shift + t].reshape(b * t, c)


def _layer(x, b, p, st, rope, table_block):
    n = x.shape[0]
    t = n // b
    fresh = st is None
    tiles_per_seq = t // TM if fresh else 0
    ca, cb, cc = _in_proj(x, p["norm_mix"], p["w_in"])
    ca3 = ca.reshape(b, t, COLS_A)
    xbc_tail = cc.reshape(b, t, COLS_C_PAD)[:, -(CONV_W - 1):, W_C:W_C + CONV_DIM]
    if fresh:
        prev, shifted = None, None
        conv_new = xbc_tail
    else:
        prev = _shifted(ca3, st["shift"][:, None, :], 1)
        xbc = cc[:, W_C:W_C + CONV_DIM].reshape(b, t, CONV_DIM)
        shifted = [_shifted(xbc, st["conv"], j) for j in range(1, CONV_W)]
        conv_new = jnp.concatenate([st["conv"], xbc_tail], axis=1)[:, -(CONV_W - 1):]

    r, dec, k2, v, nkk, beta, ga, bonus = _rwkv_prep(ca, prev, p, tiles_per_seq)
    qh, kh = _ret_prep(cb, rope[0], rope[1], table_block)
    xdt, bm, cm, dssm, xs, da = _ssm_prep(cc, shifted, p, tiles_per_seq)

    feats = (r, dec, k2, v, nkk, beta)
    if fresh:
        ya, s_rwkv = _rwkv_long(feats, b)
        yb, s_full = _ret_chunk(qh, kh, cb, b, t)
        s_ret = jnp.stack([s_full[:, h * HEAD_DIM:(h + 1) * HEAD_DIM, h * HEAD_DIM:(h + 1) * HEAD_DIM]
                           for h in range(H_B)], axis=1)
        yc, s_ssm = _ssd_chunk(xdt, bm, cm, da, p["expand"], b, t)
        s_ssm = s_ssm.reshape(b, H_C, HEAD_DIM, N_STATE)
    else:
        assert b == LANES
        ya, s_rwkv = _rwkv_short(feats, t, st["rwkv"])
        yb, s_ret = _ret_short(qh, kh, cb, t, st["ret"])
        yc, s_ssm = _ssd_short(xdt, bm, cm, dssm, t, st["ssm"])

    x2 = _post(x, ya, bonus, ga, yb, cb, yc, xs, cc, p)
    x = _ffn(x2, p["norm_ffn"], p["wg"], p["wu"], p["wd"], router=p["router"], nfin=p["norm_final"])
    return x, (s_rwkv, ca3[:, -1], s_ret, s_ssm, conv_new)


def kernel(x_prompt, x_sample, state_rwkv, state_shift, state_ret, state_ssm, state_conv, norm_mix, w_in, rwkv_mu, rwkv_w0, rwkv_w_up, rwkv_a0, rwkv_a_up, rwkv_g_up, rwkv_k_k, rwkv_k_a, rwkv_r_k, rwkv_ln_w, rwkv_ln_b, ret_norm, ssm_conv_w, ssm_conv_b, ssm_dt_bias, ssm_a_log, ssm_d, ssm_norm, w_out, norm_ffn, ffn_w_gate, ffn_w_up, ffn_w_down, moe_router, moe_w_gate, moe_w_up, moe_w_down, norm_final):
    bp, tp, _ = x_prompt.shape
    bs, ts, _ = x_sample.shape
    depth = w_in.shape[0]
    assert tp % TM == 0 and tp % LC == 0 and (bs * ts) % TM == 0 and TM % ts == 0

    rope = _rope_tables(tp, ts)
    bd_a, bd_b = _block_diag_ones(W_A), _block_diag_ones(W_B)
    expand = np.zeros((LANES, W_C), np.float32)
    expand[np.arange(W_C) // HEAD_DIM, np.arange(W_C)] = 1.0
    expand = jnp.asarray(expand, BF16)
    row = lambda v: v.reshape(1, -1)
    pad_l = lambda v: jnp.pad(v, (0, LANES - v.shape[0])).reshape(1, LANES)

    xp = x_prompt.reshape(bp * tp, D_MODEL)
    xs = x_sample.reshape(bs * ts, D_MODEL)
    new_p, new_s = [], []
    for i in range(depth):
        j = i // 2
        p = dict(
            norm_mix=row(norm_mix[i]),
            w_in=jnp.pad(w_in[i], ((0, 0), (0, COLS_C_PAD - COLS_C))).astype(BF16),
            mu=row(rwkv_mu[i]), w0=row(rwkv_w0[i]), a0=row(rwkv_a0[i]), k_k=row(rwkv_k_k[i]),
            k_a=row(rwkv_k_a[i]), r_k=row(rwkv_r_k[i]),
            w_up=jnp.pad(rwkv_w_up[i], ((0, AAA_LORA), (0, 0))).astype(BF16),
            a_up=jnp.pad(rwkv_a_up[i], ((DECAY_LORA, 0), (0, 0))).astype(BF16),
            g_up=rwkv_g_up[i].astype(BF16), bd_a=bd_a, bd_b=bd_b,
            conv_w=ssm_conv_w[i], conv_b=row(ssm_conv_b[i]), dt_bias=pad_l(ssm_dt_bias[i]),
            a_log=pad_l(ssm_a_log[i]), expand=expand,
            ln_w=row(rwkv_ln_w[i]), ln_b=row(rwkv_ln_b[i]), ret_norm=row(ret_norm[i]),
            d_skip=row(jnp.repeat(ssm_d[i], HEAD_DIM)), ssm_norm=row(ssm_norm[i]), w_out=w_out[i].astype(BF16),
            norm_ffn=row(norm_ffn[i]), norm_final=row(norm_final) if i == depth - 1 else None)
        if i % 2 == 0:
            dff = ffn_w_gate.shape[-1] // 2
            p.update(router=None,
                     wg=ffn_w_gate[j].reshape(D_MODEL, 2, dff).transpose(1, 0, 2).astype(BF16),
                     wu=ffn_w_up[j].reshape(D_MODEL, 2, dff).transpose(1, 0, 2).astype(BF16),
                     wd=ffn_w_down[j].reshape(2, dff, D_MODEL).astype(BF16))
        else:
            p.update(router=jnp.pad(moe_router[j], ((0, 0), (0, LANES - N_EXPERTS))),
                     wg=moe_w_gate[j].astype(BF16), wu=moe_w_up[j].astype(BF16), wd=moe_w_down[j].astype(BF16))

        xp, st_p = _layer(xp, bp, p, None, rope, lambda t: t % (tp // TM))
        st = dict(rwkv=(state_rwkv, i), shift=state_shift[i], ret=(state_ret, i), ssm=(state_ssm, i),
                  conv=state_conv[i])
        xs, st_s = _layer(xs, bs, p, st, rope, lambda t: tp // TM)
        new_p.append(st_p)
        new_s.append(st_s)

    stack = lambda sts: tuple(jnp.stack(s) for s in zip(*sts))
    return (xp.reshape(bp, tp, D_MODEL), xs.reshape(bs, ts, D_MODEL)) + stack(new_p) + stack(new_s)
```

```python
import functools
import math

import numpy as np
import jax
import jax.numpy as jnp
from jax import lax
from jax.experimental import pallas as pl
from jax.experimental.pallas import tpu as pltpu

F32 = jnp.float32
BF16 = jnp.bfloat16
HIGHEST = lax.Precision.HIGHEST

LANES = 128
SUBLANES = 8
VMEM_LIMIT = 56 * 1024 * 1024

D_MODEL = 1024
HEAD_DIM = 64
H_A, H_B, H_C = 6, 4, 6
W_A, W_B, W_C = H_A * HEAD_DIM, H_B * HEAD_DIM, H_C * HEAD_DIM
DECAY_LORA, AAA_LORA, GATE_LORA = 64, 64, 128
COLS_A = 3 * W_A + DECAY_LORA + AAA_LORA + GATE_LORA
COLS_B = 4 * W_B
N_STATE, N_GROUPS, CONV_W = 128, 2, 4
CONV_DIM = W_C + 2 * N_GROUPS * N_STATE
COLS_C = W_C + CONV_DIM + H_C
COLS_C_PAD = 1408
ROPE_BASE = 10000.0
RMS_EPS = 1e-6
GN_EPS = 64e-5
GATED_NORM_EPS = 1e-5
N_EXPERTS = 8
PAST_LEN = 16384

TM = 512
MOE_ROWS = 512
MOE_CHUNK = 160
MOE_SUB = 2
LC = 256
H_PAD = 8
K_SPLIT = 2


def _dot(a, b):
    return jnp.dot(a.astype(BF16), b.astype(BF16), preferred_element_type=F32)


def _dot_hi(a, b):
    return jnp.dot(a, b, precision=HIGHEST, preferred_element_type=F32)


def _dot_select(a, sel, terms=2):
    out = None
    rest = a
    for _ in range(terms):
        piece = rest.astype(BF16)
        part = jnp.dot(piece, sel, preferred_element_type=F32)
        out = part if out is None else out + part
        rest = rest - piece.astype(F32)
    return out


def _sigmoid(x):
    return 1.0 / (1.0 + jnp.exp(-x))


def _softplus(x):
    return jnp.maximum(x, 0.0) + jnp.log1p(jnp.exp(-jnp.abs(x)))


def _rmsnorm(x, g, eps):
    return x * lax.rsqrt(jnp.mean(x * x, axis=-1, keepdims=True) + eps) * g


def _params(sem):
    return pltpu.CompilerParams(dimension_semantics=sem, vmem_limit_bytes=VMEM_LIMIT)


def _row_spec(width, col=0):
    return pl.BlockSpec((TM, width), lambda i, c=col: (i, c))


def _before_spec(width):
    return pl.BlockSpec((SUBLANES, width), lambda i: (jnp.maximum(i * (TM // SUBLANES) - 1, 0), 0))


def _full_spec(shape):
    nd = len(shape)
    return pl.BlockSpec(shape, lambda i, n=nd: (0,) * n)


def _shift_rows(x, before, j, first):
    rolled = pltpu.roll(x, j, 0)
    prev = jnp.where(first, 0.0, pltpu.roll(before, j, 0))
    row = lax.broadcasted_iota(jnp.int32, prev.shape, 0)
    top = jnp.where(row < j, prev, rolled[0:SUBLANES])
    return jnp.concatenate([top, rolled[SUBLANES:]], axis=0)


def _in_proj_kernel(x_ref, g_ref, w_ref, oa_ref, ob_ref, oc_ref):
    h = _rmsnorm(x_ref[...], g_ref[...], RMS_EPS).astype(BF16)
    oa_ref[...] = jnp.dot(h, w_ref[:, 0:COLS_A], preferred_element_type=F32)
    ob_ref[...] = jnp.dot(h, w_ref[:, COLS_A:COLS_A + COLS_B], preferred_element_type=F32)
    oc_ref[...] = jnp.dot(h, w_ref[:, COLS_A + COLS_B:], preferred_element_type=F32)


def _in_proj(x, g, w):
    n = x.shape[0]
    wtot = w.shape[1]
    return pl.pallas_call(
        _in_proj_kernel,
        grid=(n // TM,),
        in_specs=[_row_spec(D_MODEL), _full_spec((1, D_MODEL)), _full_spec((D_MODEL, wtot))],
        out_specs=[_row_spec(COLS_A), _row_spec(COLS_B), _row_spec(COLS_C_PAD)],
        out_shape=[jax.ShapeDtypeStruct((n, COLS_A), F32), jax.ShapeDtypeStruct((n, COLS_B), F32),
                   jax.ShapeDtypeStruct((n, COLS_C_PAD), F32)],
        compiler_params=_params(("parallel",)),
        name="in_proj",
    )(x, g, w)


def _rwkv_prep_kernel(c_ref, p_ref, mu_ref, w0_ref, a0_ref, kk_ref, ka_ref, rk_ref, wup_ref, aup_ref,
                      gup_ref, bd_ref, r_o, d_o, k_o, v_o, n_o, b_o, g_o, bonus_o, *, tiles_per_seq):
    c = c_ref[...]
    if tiles_per_seq:
        prev = _shift_rows(c, p_ref[...], 1, pl.program_id(0) % tiles_per_seq == 0)
    else:
        prev = p_ref[...]
    xm = c + (prev - c) * mu_ref[...]
    r = xm[:, 0:W_A]
    k = xm[:, W_A:2 * W_A]
    v = xm[:, 2 * W_A:3 * W_A]
    lora = xm[:, 3 * W_A:3 * W_A + DECAY_LORA + AAA_LORA]
    gd = xm[:, 3 * W_A + DECAY_LORA + AAA_LORA:]
    w = w0_ref[...] + _dot(jnp.tanh(lora), wup_ref[...])
    decay = jnp.exp(-math.exp(-0.5) * _sigmoid(w))
    a = _sigmoid(a0_ref[...] + _dot(lora, aup_ref[...]))
    g = _dot(_sigmoid(gd), gup_ref[...])
    bd = bd_ref[...]
    kk = k * kk_ref[...]
    kk = kk / jnp.maximum(jnp.sqrt(_dot_select(kk * kk, bd)), 1e-12)
    k2 = k * (1.0 + (a - 1.0) * ka_ref[...])
    for o_ref, val in ((r_o, r), (d_o, decay), (k_o, k2), (v_o, v), (n_o, -kk), (b_o, kk * a)):
        o_ref[:, 0:W_A] = val
        if o_ref.shape[1] > W_A:
            o_ref[:, W_A:] = jnp.zeros((TM, o_ref.shape[1] - W_A), F32)
    g_o[...] = g
    bonus_o[...] = _dot_select(r * k2 * rk_ref[...], bd) * v


def _rwkv_prep(ca, prev, p, tiles_per_seq):
    n = ca.shape[0]
    vec = _full_spec((1, W_A))
    lora_spec = _full_spec((DECAY_LORA + AAA_LORA, W_A))
    p_spec = _before_spec(COLS_A) if tiles_per_seq else _row_spec(COLS_A)
    w_scan = H_PAD * HEAD_DIM if tiles_per_seq else W_A
    widths = [w_scan] * 6 + [W_A] * 2
    return pl.pallas_call(
        functools.partial(_rwkv_prep_kernel, tiles_per_seq=tiles_per_seq),
        grid=(n // TM,),
        in_specs=[_row_spec(COLS_A), p_spec, _full_spec((1, COLS_A)), vec, vec, vec, vec, vec,
                  lora_spec, lora_spec, _full_spec((GATE_LORA, W_A)), _full_spec((W_A, W_A))],
        out_specs=[_row_spec(w) for w in widths],
        out_shape=[jax.ShapeDtypeStruct((n, w), F32) for w in widths],
        compiler_params=_params(("parallel",)),
        name="rwkv_prep",
    )(ca, ca if tiles_per_seq else prev, p["mu"], p["w0"], p["a0"], p["k_k"], p["k_a"], p["r_k"],
      p["w_up"], p["a_up"], p["g_up"], p["bd_a"])


def _rope_kernel(ang_ref, cos_o, sin_o):
    ang = ang_ref[...]
    lane = lax.broadcasted_iota(jnp.int32, ang.shape, 1)
    cos_o[...] = jnp.cos(ang)
    sin_o[...] = jnp.where((lane % HEAD_DIM) < (HEAD_DIM // 2), -jnp.sin(ang), jnp.sin(ang))


def _rope_tables(t_prompt, t_sample):
    theta = 1.0 / (ROPE_BASE ** jnp.linspace(0.0, 1.0, HEAD_DIM // 2, dtype=F32))
    pos = jnp.concatenate([jnp.arange(t_prompt, dtype=F32), PAST_LEN + (jnp.arange(TM) % t_sample).astype(F32)])
    ang = jnp.tile(pos[:, None] * theta[None, :], (1, W_B // (HEAD_DIM // 2)))
    n = ang.shape[0]
    return pl.pallas_call(
        _rope_kernel,
        grid=(n // TM,),
        in_specs=[_row_spec(W_B)],
        out_specs=[_row_spec(W_B)] * 2,
        out_shape=[jax.ShapeDtypeStruct((n, W_B), F32)] * 2,
        compiler_params=_params(("parallel",)),
        name="rope_tables",
    )(ang)


def _ret_prep_kernel(q_ref, k_ref, cos_ref, sin_ref, q_o, k_o):
    cos = cos_ref[...]
    sin = sin_ref[...]
    lane = lax.broadcasted_iota(jnp.int32, (TM, W_B), 1)
    first_half = (lane % HEAD_DIM) < (HEAD_DIM // 2)

    def rope(x):
        partner = jnp.where(first_half, pltpu.roll(x, W_B - HEAD_DIM // 2, 1), pltpu.roll(x, HEAD_DIM // 2, 1))
        return x * cos + partner * sin

    q_o[...] = rope(q_ref[...])
    k_o[...] = rope(k_ref[...]) * (HEAD_DIM ** -0.5)


def _ret_prep(cb, cos, sin, table_block):
    n = cb.shape[0]
    tab = pl.BlockSpec((TM, W_B), lambda i: (table_block(i), 0))
    return pl.pallas_call(
        _ret_prep_kernel,
        grid=(n // TM,),
        in_specs=[_row_spec(W_B, 0), _row_spec(W_B, 1), tab, tab],
        out_specs=[_row_spec(W_B)] * 2,
        out_shape=[jax.ShapeDtypeStruct((n, W_B), F32)] * 2,
        compiler_params=_params(("parallel",)),
        name="ret_prep",
    )(cb, cb, cos, sin)


def _ssm_prep_kernel(*refs, tiles_per_seq):
    n_shift_refs = 1 if tiles_per_seq else CONV_W - 1
    cc_ref = refs[0]
    shift_refs = refs[1:1 + n_shift_refs]
    cw_ref, cb_ref, dtb_ref, alog_ref, ex_ref, xdt_o, b_o, c_o, dec_o, xs_o, da_o = refs[1 + n_shift_refs:]
    cc = cc_ref[...]
    x0 = cc[:, W_C:W_C + CONV_DIM]
    if tiles_per_seq:
        before = shift_refs[0][:, W_C:W_C + CONV_DIM]
        first = pl.program_id(0) % tiles_per_seq == 0
        xs_prev = [_shift_rows(x0, before, j, first) for j in range(1, CONV_W)]
    else:
        xs_prev = [r[...] for r in shift_refs]
    cw = cw_ref[...]
    conv = x0 * cw[CONV_W - 1:CONV_W, :] + cb_ref[...]
    for j in range(1, CONV_W):
        conv = conv + xs_prev[j - 1] * cw[CONV_W - 1 - j:CONV_W - j, :]
    act = conv * _sigmoid(conv)
    xs = act[:, 0:W_C]
    dt = _softplus(cc[:, W_C + CONV_DIM:] + dtb_ref[...])
    da = dt * (-jnp.exp(alog_ref[...]))
    da_o[...] = da
    dec_o[...] = jnp.exp(da)
    xdt_o[...] = xs * _dot_select(dt, ex_ref[...])
    b_o[...] = act[:, W_C:W_C + N_GROUPS * N_STATE]
    c_o[...] = act[:, W_C + N_GROUPS * N_STATE:]
    xs_o[...] = xs


def _ssm_prep(cc, shifted, p, tiles_per_seq):
    n = cc.shape[0]
    gn = N_GROUPS * N_STATE
    if tiles_per_seq:
        shift_args, shift_specs = [cc], [_before_spec(COLS_C_PAD)]
    else:
        shift_args, shift_specs = list(shifted), [_row_spec(CONV_DIM)] * (CONV_W - 1)
    return pl.pallas_call(
        functools.partial(_ssm_prep_kernel, tiles_per_seq=tiles_per_seq),
        grid=(n // TM,),
        in_specs=[_row_spec(COLS_C_PAD)] + shift_specs + [
            _full_spec((CONV_W, CONV_DIM)), _full_spec((1, CONV_DIM)), _full_spec((1, LANES)),
            _full_spec((1, LANES)), _full_spec((LANES, W_C))],
        out_specs=[_row_spec(W_C), _row_spec(gn), _row_spec(gn), _row_spec(LANES), _row_spec(W_C),
                   _row_spec(LANES)],
        out_shape=[jax.ShapeDtypeStruct((n, W_C), F32), jax.ShapeDtypeStruct((n, gn), F32),
                   jax.ShapeDtypeStruct((n, gn), F32), jax.ShapeDtypeStruct((n, LANES), F32),
                   jax.ShapeDtypeStruct((n, W_C), F32), jax.ShapeDtypeStruct((n, LANES), F32)],
        compiler_params=_params(("parallel",)),
        name="ssm_prep",
    )(cc, *shift_args, p["conv_w"], p["conv_b"], p["dt_bias"], p["a_log"], p["expand"])


def _scan_kernel(*refs, mode, ni, tt_len, n_tt):
    if mode == "rwkv":
        q_ref, d_ref, a_ref, b_ref, n_ref, a2_ref, s0_ref, y_ref, so_ref, st = refs
    else:
        q_ref, d_ref, a_ref, b_ref, s0_ref, y_ref, so_ref, st = refs
    head = pl.program_id(0)
    tt = pl.program_id(1)

    @pl.when(tt == 0)
    def _():
        st[...] = s0_ref[...]

    def row(ref, t, i):
        return ref[t, pl.ds(i, 1), :]

    def step(t, carry):
        bv = b_ref[t]
        if mode == "rwkv":
            sa_parts = [jnp.zeros_like(bv), jnp.zeros_like(bv)]
            for i in range(ni):
                sa_parts[i % 2] = sa_parts[i % 2] + st[i] * row(n_ref, t, i)
            sa = sa_parts[0] + sa_parts[1]
        elif mode == "ssd":
            d = d_ref[t, pl.ds(head, 1), :]
        else:
            d = d_ref[pl.ds(head, 1), :]
        y_parts = [jnp.zeros_like(bv), jnp.zeros_like(bv)]
        for i in range(ni):
            if mode == "rwkv":
                s = st[i] * row(d_ref, t, i) + row(a_ref, t, i) * bv + row(a2_ref, t, i) * sa
            else:
                s = st[i] * d + row(a_ref, t, i) * bv
            st[i] = s
            y_parts[i % 2] = y_parts[i % 2] + s * row(q_ref, t, i)
        y_ref[t] = y_parts[0] + y_parts[1]
        return carry

    lax.fori_loop(0, tt_len, step, 0)

    @pl.when(tt == n_tt - 1)
    def _():
        so_ref[...] = st[...]


def _scan(mode, n_heads, ni, nj, q, d, a, b, s0, n=None, a2=None):
    t_len = b[0].shape[0]

    def rows(op, width):
        arr, block_of = op
        return arr, pl.BlockSpec((t_len, width, LANES), lambda h, t, f=block_of: (0, f(h), 0))

    ops = [rows(q, ni)]
    if mode == "rwkv":
        ops.append(rows(d, ni))
    elif mode == "ssd":
        ops.append((d, pl.BlockSpec((t_len, SUBLANES, LANES), lambda h, t: (0, 0, 0))))
    else:
        ops.append((d, pl.BlockSpec((SUBLANES, LANES), lambda h, t: (0, 0))))
    ops += [rows(a, ni), rows(b, nj)]
    if mode == "rwkv":
        ops += [rows(n, ni), rows(a2, ni)]
    s_spec = pl.BlockSpec((ni, nj, LANES), lambda h, t: (h, 0, 0))
    y_spec = pl.BlockSpec((t_len, nj, LANES), lambda h, t: (0, h, 0))
    return pl.pallas_call(
        functools.partial(_scan_kernel, mode=mode, ni=ni, tt_len=t_len, n_tt=1),
        grid=(n_heads, 1),
        in_specs=[spec for _, spec in ops] + [s_spec],
        out_specs=[y_spec, s_spec],
        out_shape=[jax.ShapeDtypeStruct((t_len, n_heads * nj, LANES), F32),
                   jax.ShapeDtypeStruct((n_heads * ni, nj, LANES), F32)],
        scratch_shapes=[pltpu.VMEM((ni, nj, LANES), F32)],
        compiler_params=_params(("parallel", "arbitrary")),
        name="scan_" + mode,
    )(*[arr for arr, _ in ops], s0)


def _rwkv_long_kernel(q_ref, d_ref, a_ref, b_ref, n1_ref, a2_ref, y_ref, so_ref, st, sa_s, dots_s, rows_s, *,
                      ni, tt_len, n_tt):
    tt = pl.program_id(1)

    @pl.when(tt == 0)
    def _():
        st[...] = jnp.zeros_like(st)
        sa_s[...] = jnp.zeros_like(sa_s)

    refs_i = (q_ref, d_ref, a_ref, n1_ref, a2_ref)
    pitch = tt_len + SUBLANES
    for k, ref in enumerate(refs_i):
        for i in range(ni):
            rows_s[i, k * pitch:k * pitch + tt_len, :] = ref[i]

    op_q, op_d, op_a, op_n1, op_a2 = range(len(refs_i))

    def row(k, t, i):
        return rows_s[i, pl.ds(k * pitch + t, 1), :]

    def fold(x):
        return x + pltpu.roll(x, LANES // 2, 1)

    for k, (u_ref, w_ref) in enumerate(((a_ref, q_ref), (a2_ref, q_ref), (a_ref, n1_ref), (a2_ref, n1_ref))):
        acc = u_ref[0] * w_ref[0]
        for i in range(1, ni):
            acc = acc + u_ref[i] * w_ref[i]
        dots_s[k] = fold(acc)

    def step(t, sa):
        bv = b_ref[t]
        aq, a2q, an, a2n = [dots_s[k, pl.ds(t, 1), :] for k in range(4)]
        y0 = [jnp.zeros_like(bv), jnp.zeros_like(bv)]
        n0 = [jnp.zeros_like(bv), jnp.zeros_like(bv)]
        for i in range(ni):
            sd = st[i] * row(op_d, t, i)
            st[i] = sd
            y0[i % 2] = y0[i % 2] + sd * row(op_q, t, i)
            n0[i % 2] = n0[i % 2] + sd * row(op_n1, t, i)
        y_ref[t] = fold(y0[0] + y0[1]) + bv * aq + sa * a2q
        sa_next = fold(n0[0] + n0[1]) + bv * an + sa * a2n
        for i in range(ni):
            st[i] = st[i] + row(op_a, t, i) * bv + row(op_a2, t, i) * sa
        return sa_next

    sa_s[...] = lax.fori_loop(0, tt_len, step, sa_s[...])

    @pl.when(tt == n_tt - 1)
    def _():
        so_ref[...] = st[...]


def _rwkv_long_scan(q, d, a, b, n1, a2, tt_len=128):
    ni, t_len, _ = q.shape
    nj = b.shape[1]
    n_tt = t_len // tt_len
    i_spec = pl.BlockSpec((ni, tt_len, LANES), lambda g, t: (0, t, 0))
    j_spec = pl.BlockSpec((tt_len, nj, LANES), lambda g, t: (t, 0, 0))
    s_spec = pl.BlockSpec((ni, nj, LANES), lambda g, t: (0, 0, 0))
    return pl.pallas_call(
        functools.partial(_rwkv_long_kernel, ni=ni, tt_len=tt_len, n_tt=n_tt),
        grid=(1, n_tt),
        in_specs=[i_spec, i_spec, i_spec, j_spec, i_spec, i_spec],
        out_specs=[j_spec, s_spec],
        out_shape=[jax.ShapeDtypeStruct((t_len, nj, LANES), F32), jax.ShapeDtypeStruct((ni, nj, LANES), F32)],
        scratch_shapes=[pltpu.VMEM((ni, nj, LANES), F32), pltpu.VMEM((nj, LANES), F32),
                        pltpu.VMEM((4, tt_len, LANES), F32),
                        pltpu.VMEM((ni, 5 * (tt_len + SUBLANES), LANES), F32)],
        compiler_params=_params(("parallel", "arbitrary")),
        name="scan_rwkv_long",
    )(q, d, a, b, n1, a2)


TR = 128
HALF = LANES // K_SPLIT


Y_PITCH = LANES + SUBLANES


def _to_lanes_kernel(*refs, n_ops, rows, per_half):
    x_refs, o_refs, y2d = refs[:n_ops], refs[n_ops:2 * n_ops], refs[2 * n_ops]
    n_b = x_refs[0].shape[0]
    for x_ref, o_ref in zip(x_refs, o_refs):
        for b in range(n_b):
            xt = x_ref[b].T
            for h in range(H_PAD):
                for s in range(K_SPLIT):
                    lane = s * HALF + b * H_PAD + h
                    f0 = h * HEAD_DIM + (s * rows if per_half else 0)
                    y2d[pl.ds(lane, rows, stride=Y_PITCH), :] = xt[f0:f0 + rows, :]
        for r in range(rows):
            slab = y2d[r * Y_PITCH:r * Y_PITCH + LANES, :].T
            if per_half:
                o_ref[r] = slab
            else:
                o_ref[pl.ds(r, TR, stride=rows), :] = slab


def _long_to_lanes(xs, b, rows, per_half):
    n_ops = len(xs)
    t = xs[0].shape[0] // b
    width = H_PAD * HEAD_DIM
    if per_half:
        o_spec = pl.BlockSpec((rows, TR, LANES), lambda i: (0, i, 0))
        o_shape = jax.ShapeDtypeStruct((rows, t, LANES), F32)
    else:
        o_spec = pl.BlockSpec((TR * rows, LANES), lambda i: (i, 0))
        o_shape = jax.ShapeDtypeStruct((t * rows, LANES), F32)
    outs = pl.pallas_call(
        functools.partial(_to_lanes_kernel, n_ops=n_ops, rows=rows, per_half=per_half),
        grid=(t // TR,),
        in_specs=[pl.BlockSpec((b, TR, width), lambda i: (0, i, 0))] * n_ops,
        out_specs=[o_spec] * n_ops,
        out_shape=[o_shape] * n_ops,
        scratch_shapes=[pltpu.VMEM((rows * Y_PITCH, TR), F32)],
        compiler_params=_params(("parallel",)),
        name="to_lanes",
    )(*[x.reshape(b, t, width) for x in xs])
    return outs if per_half else [o.reshape(t, rows, LANES) for o in outs]


def _from_lanes_kernel(y_ref, o_ref, z2d):
    for v in range(HEAD_DIM):
        z2d[pl.ds(v, LANES, stride=HEAD_DIM), :] = y_ref[pl.ds(v, TR, stride=HEAD_DIM), :].T
    for b in range(o_ref.shape[0]):
        r0 = b * H_PAD * HEAD_DIM
        o_ref[b] = z2d[r0:r0 + W_A, :].T


def _long_from_lanes(y, b):
    t = y.shape[0]
    out = pl.pallas_call(
        _from_lanes_kernel,
        grid=(t // TR,),
        in_specs=[pl.BlockSpec((TR * HEAD_DIM, LANES), lambda i: (i, 0))],
        out_specs=pl.BlockSpec((b, TR, W_A), lambda i: (0, i, 0)),
        out_shape=jax.ShapeDtypeStruct((b, t, W_A), F32),
        scratch_shapes=[pltpu.VMEM((LANES * HEAD_DIM, TR), F32)],
        compiler_params=_params(("parallel",)),
        name="from_lanes",
    )(y.reshape(t * HEAD_DIM, LANES))
    return out.reshape(b * t, W_A)


def _ret_chunk_kernel(q_ref, k_ref, v_ref, y_ref, s_ref, st):
    c = pl.program_id(1)

    @pl.when(c == 0)
    def _():
        st[...] = jnp.zeros_like(st)

    q = q_ref[...]
    k = k_ref[...]
    vb = v_ref[...].astype(BF16)
    kb = k.astype(BF16)
    row = lax.broadcasted_iota(jnp.int32, (LC, W_B), 0)
    head = lax.broadcasted_iota(jnp.int32, (LC, W_B), 1) // HEAD_DIM
    log_g = [math.log(1.0 - 2.0 ** (-5.0 - h)) for h in range(H_B)]
    lg = jnp.full((LC, W_B), log_g[0], F32)
    for h in range(1, H_B):
        lg = jnp.where(head == h, log_g[h], lg)
    rowf = row.astype(F32)
    diff = (lax.broadcasted_iota(jnp.int32, (LC, LC), 0) - lax.broadcasted_iota(jnp.int32, (LC, LC), 1))
    causal = diff >= 0
    difff = jnp.maximum(diff, 0).astype(F32)

    out = jnp.dot((q * jnp.exp(lg * (rowf + 1.0))).astype(BF16), st[...].astype(BF16), preferred_element_type=F32)
    for h in range(H_B):
        qm = jnp.where(head == h, q, 0.0).astype(BF16)
        s = lax.dot_general(qm, kb, (((1,), (1,)), ((), ())), preferred_element_type=F32)
        p = jnp.where(causal, s * jnp.exp(log_g[h] * difff), 0.0).astype(BF16)
        out = out + jnp.where(head == h, jnp.dot(p, vb, preferred_element_type=F32), 0.0)
    y_ref[...] = out

    kt = (k * jnp.exp(lg * (LC - 1.0 - rowf))).T.astype(BF16)
    kv = jnp.dot(kt, vb, preferred_element_type=F32)
    r2 = lax.broadcasted_iota(jnp.int32, (W_B, W_B), 0) // HEAD_DIM
    c2 = lax.broadcasted_iota(jnp.int32, (W_B, W_B), 1) // HEAD_DIM
    cdec = jnp.full((W_B, W_B), math.exp(log_g[0] * LC), F32)
    for h in range(1, H_B):
        cdec = jnp.where(r2 == h, math.exp(log_g[h] * LC), cdec)
    st[...] = st[...] * cdec + jnp.where(r2 == c2, kv, 0.0)
    s_ref[0] = st[...]


def _ret_chunk(q, k, cb, n_seq, t_len):
    n_c = t_len // LC
    rows = lambda col: pl.BlockSpec((LC, W_B), lambda b, c, col=col: (b * n_c + c, col))
    return pl.pallas_call(
        _ret_chunk_kernel,
        grid=(n_seq, n_c),
        in_specs=[rows(0), rows(0), rows(2)],
        out_specs=[rows(0), pl.BlockSpec((1, W_B, W_B), lambda b, c: (b, 0, 0))],
        out_shape=[jax.ShapeDtypeStruct((n_seq * t_len, W_B), F32), jax.ShapeDtypeStruct((n_seq, W_B, W_B), F32)],
        scratch_shapes=[pltpu.VMEM((W_B, W_B), F32)],
        compiler_params=_params(("parallel", "arbitrary")),
        name="ret_chunk",
    )(q, k, cb)


def _ssd_chunk_kernel(x_ref, b_ref, c_ref, da_ref, ex_ref, y_ref, h_ref, st):
    ci = pl.program_id(1)

    @pl.when(ci == 0)
    def _():
        st[...] = jnp.zeros_like(st)

    gn = N_STATE
    half = W_C // N_GROUPS
    ex = ex_ref[...]
    x = x_ref[...]
    xb = x.astype(BF16)
    ri = lax.broadcasted_iota(jnp.int32, (LC, LC), 0)
    cj = lax.broadcasted_iota(jnp.int32, (LC, LC), 1)
    causal = ri >= cj
    cum = _dot_hi(causal.astype(F32), da_ref[...])
    cum_e = _dot_select(cum, ex, 3)
    last_e = cum_e[LC - 1:LC, :]
    cum_t = cum.T
    head = lax.broadcasted_iota(jnp.int32, (LC, W_C), 1) // HEAD_DIM
    bg = [b_ref[:, g * gn:(g + 1) * gn].astype(BF16) for g in range(N_GROUPS)]
    cg = [c_ref[:, g * gn:(g + 1) * gn].astype(BF16) for g in range(N_GROUPS)]
    cb = [lax.dot_general(cg[g], bg[g], (((1,), (1,)), ((), ())), preferred_element_type=F32)
          for g in range(N_GROUPS)]

    hb = st[...].astype(BF16)
    ys = [lax.dot_general(cg[g], hb, (((1,), (1,)), ((), ())), preferred_element_type=F32)
          for g in range(N_GROUPS)]
    y = jnp.where(head < H_C // N_GROUPS, ys[0], ys[1]) * jnp.exp(cum_e)
    for h in range(H_C):
        seg = cum[:, h:h + 1] - cum_t[h:h + 1, :]
        p = (cb[h // (H_C // N_GROUPS)] * jnp.exp(jnp.where(causal, seg, -jnp.inf))).astype(BF16)
        y = y + jnp.where(head == h, jnp.dot(p, xb, preferred_element_type=F32), 0.0)
    y_ref[...] = y

    xt = (x * jnp.exp(last_e - cum_e)).T.astype(BF16)
    upd = [jnp.dot(xt, bg[g], preferred_element_type=F32) for g in range(N_GROUPS)]
    rowi = lax.broadcasted_iota(jnp.int32, (W_C, gn), 0)
    sel = lax.broadcasted_iota(jnp.int32, (W_C, LANES), 1) == lax.broadcasted_iota(jnp.int32, (W_C, LANES), 0) // HEAD_DIM
    tot = jnp.sum(jnp.where(sel, cum[LC - 1:LC, :], 0.0), axis=1, keepdims=True)
    st[...] = st[...] * jnp.exp(tot) + jnp.where(rowi < half, upd[0], upd[1])
    h_ref[0] = st[...]


def _ssd_chunk(xdt, bm, cm, da, ex, n_seq, t_len):
    n_c = t_len // LC
    gn = N_GROUPS * N_STATE
    rows = lambda w: pl.BlockSpec((LC, w), lambda b, c: (b * n_c + c, 0))
    return pl.pallas_call(
        _ssd_chunk_kernel,
        grid=(n_seq, n_c),
        in_specs=[rows(W_C), rows(gn), rows(gn), rows(LANES), pl.BlockSpec((LANES, W_C), lambda b, c: (0, 0))],
        out_specs=[rows(W_C), pl.BlockSpec((1, W_C, N_STATE), lambda b, c: (b, 0, 0))],
        out_shape=[jax.ShapeDtypeStruct((n_seq * t_len, W_C), F32),
                   jax.ShapeDtypeStruct((n_seq, W_C, N_STATE), F32)],
        scratch_shapes=[pltpu.VMEM((W_C, N_STATE), F32)],
        compiler_params=_params(("parallel", "arbitrary")),
        name="ssd_chunk",
    )(xdt, bm, cm, da, ex)


def _post_kernel(x_ref, ya_ref, bonus_ref, ga_ref, ob_ref, gb_ref, yc_ref, xs_ref, z_ref, lnw_ref, lnb_ref,
                 rn_ref, dsk_ref, sn_ref, bda_ref, bdb_ref, wo_ref, out_ref):
    inv_hd = 1.0 / HEAD_DIM
    y = ya_ref[...]
    bda = bda_ref[...]
    mean = _dot_select(y, bda) * inv_hd
    yd = y - mean
    var = _dot_select(yd * yd, bda) * inv_hd
    ya = (yd * lax.rsqrt(var + GN_EPS) * lnw_ref[...] + lnb_ref[...] + bonus_ref[...]) * ga_ref[...]
    o = ob_ref[...]
    ms = _dot_select(o * o, bdb_ref[...]) * inv_hd
    gb = gb_ref[...]
    yb = (gb * _sigmoid(gb)) * (o * lax.rsqrt(ms + RMS_EPS) * rn_ref[...])
    z = z_ref[...]
    yc = (yc_ref[...] + dsk_ref[...] * xs_ref[...]) * (z * _sigmoid(z))
    yc = _rmsnorm(yc, sn_ref[...], GATED_NORM_EPS)
    out_ref[...] = (x_ref[...] + _dot(ya, wo_ref[0:W_A, :]) + _dot(yb, wo_ref[W_A:W_A + W_B, :])
                    + _dot(yc, wo_ref[W_A + W_B:, :]))


def _post(x, ya, bonus, ga, ob, cb, yc, xs, cc, p):
    n = x.shape[0]
    va = _full_spec((1, W_A))
    return pl.pallas_call(
        _post_kernel,
        grid=(n // TM,),
        in_specs=[_row_spec(D_MODEL), _row_spec(W_A), _row_spec(W_A), _row_spec(W_A), _row_spec(W_B),
                  _row_spec(W_B, 3), _row_spec(W_C), _row_spec(W_C), _row_spec(W_C, 0), va, va,
                  _full_spec((1, W_B)), va, va, _full_spec((W_A, W_A)), _full_spec((W_B, W_B)),
                  _full_spec((D_MODEL, D_MODEL))],
        out_specs=_row_spec(D_MODEL),
        out_shape=jax.ShapeDtypeStruct((n, D_MODEL), F32),
        compiler_params=_params(("parallel",)),
        name="post_outproj",
    )(x, ya, bonus, ga, ob, cb, yc, xs, cc, p["ln_w"], p["ln_b"], p["ret_norm"], p["d_skip"], p["ssm_norm"],
      p["bd_a"], p["bd_b"], p["w_out"])


def _ffn_kernel(*refs, gated, final_norm, n_e, n_sub):
    refs = list(refs)
    x_ref, nf_ref = refs[:2]
    pos = 2
    if gated:
        router_ref = refs[pos]
        pos += 1
    wg_ref, wu_ref, wd_ref = refs[pos:pos + 3]
    pos += 3
    if final_norm:
        nfin_ref = refs[pos]
        pos += 1
    out_ref = refs[pos]
    hb_s, acc_s = refs[pos + 1:pos + 3]
    if gated:
        gate_s, asg_s, pos_s, asg_t, pos_t = refs[pos + 3:pos + 8]
    e = pl.program_id(1)
    sr = x_ref.shape[0] // n_sub
    subs = [(slice(s * sr, (s + 1) * sr), slice(s * LANES, (s + 1) * LANES)) for s in range(n_sub)]

    @pl.when(e == 0)
    def _():
        acc_s[...] = jnp.zeros_like(acc_s)
        for rs, ts in subs:
            h = _rmsnorm(x_ref[rs, :], nf_ref[...], RMS_EPS)
            hb_s[rs, :] = h.astype(BF16)
            if gated:
                lane = lax.broadcasted_iota(jnp.int32, (sr, LANES), 1)
                logits = jnp.where(lane < N_EXPERTS, _dot_hi(h, router_ref[...]), -jnp.inf)
                p = jnp.exp(logits - jnp.max(logits, axis=-1, keepdims=True))
                p = p / jnp.sum(p, axis=-1, keepdims=True)
                p1 = jnp.max(p, axis=-1, keepdims=True)
                i1 = jnp.min(jnp.where(p == p1, lane, LANES), axis=-1, keepdims=True)
                rest = jnp.where(lane == i1, -1.0, p)
                p2 = jnp.max(rest, axis=-1, keepdims=True)
                i2 = jnp.min(jnp.where(rest == p2, lane, LANES), axis=-1, keepdims=True)
                gate_s[rs, :] = jnp.where(lane == i1, p1, jnp.where(lane == i2, p2, 0.0)) / (p1 + p2)
                assigned = jnp.where((lane == i1) | (lane == i2), 1.0, 0.0)
                earlier = (lax.broadcasted_iota(jnp.int32, (sr, sr), 0)
                           > lax.broadcasted_iota(jnp.int32, (sr, sr), 1))
                rank = jnp.dot(earlier.astype(BF16), assigned.astype(BF16), preferred_element_type=F32)
                asg_s[rs, :] = assigned
                pos_s[rs, :] = rank
                asg_t[ts, :] = assigned.T
                pos_t[ts, :] = rank.T

    if not gated:
        hb = hb_s[...]
        g = jnp.dot(hb, wg_ref[0], preferred_element_type=F32)
        u = jnp.dot(hb, wu_ref[0], preferred_element_type=F32)
        acc_s[...] += _dot(g * _sigmoid(g) * u, wd_ref[0])
    else:
        lane = lax.broadcasted_iota(jnp.int32, (sr, LANES), 1)
        mine = lane == e
        col = lambda ref, rs: jnp.sum(jnp.where(mine, ref[rs, :], 0.0), axis=-1, keepdims=True)
        cols = [(col(gate_s, rs), col(asg_s, rs), col(pos_s, rs)) for rs, _ in subs]
        rws = [(asg_t[pl.ds(s * LANES + e, 1), :], pos_t[pl.ds(s * LANES + e, 1), :])
               for s in range(n_sub)]
        most = functools.reduce(jnp.maximum, [jnp.sum(asg_r) for asg_r, _ in rws])
        ch = MOE_CHUNK
        n_chunks = (most.astype(jnp.int32) + ch - 1) // ch

        def chunk(c, carry):
            base = (c * ch).astype(F32)
            slot_r = lax.broadcasted_iota(jnp.int32, (ch, sr), 0).astype(F32) + base
            picked = []
            for (rs, _), (asg_r, pos_r) in zip(subs, rws):
                pick = jnp.where((pos_r == slot_r) & (asg_r > 0.0), 1.0, 0.0).astype(BF16)
                picked.append(jnp.dot(pick, hb_s[rs, :], preferred_element_type=F32).astype(BF16))
            rows = jnp.concatenate(picked, axis=0)
            g = jnp.dot(rows, wg_ref[0], preferred_element_type=F32)
            u = jnp.dot(rows, wu_ref[0], preferred_element_type=F32)
            o = _dot(g * _sigmoid(g) * u, wd_ref[0]).astype(BF16)
            slot_c = lax.broadcasted_iota(jnp.int32, (sr, ch), 1).astype(F32) + base
            for s, ((rs, _), (gate_c, asg_c, pos_c)) in enumerate(zip(subs, cols)):
                place = jnp.where((pos_c == slot_c) & (asg_c > 0.0), 1.0, 0.0).astype(BF16)
                back = jnp.dot(place, o[s * ch:(s + 1) * ch, :], preferred_element_type=F32)
                acc_s[rs, :] += gate_c * back
            return carry

        lax.fori_loop(0, n_chunks, chunk, 0)

    @pl.when(e == n_e - 1)
    def _():
        y = x_ref[...] + acc_s[...]
        if final_norm:
            y = _rmsnorm(y, nfin_ref[...], RMS_EPS)
        out_ref[...] = y


def _ffn(x, nf, wg, wu, wd, router=None, nfin=None):
    n = x.shape[0]
    n_e, _, dff = wg.shape
    gated = router is not None
    final_norm = nfin is not None
    n_sub, sub_rows = (MOE_SUB, MOE_ROWS) if gated else (1, TM)
    rows = n_sub * sub_rows
    assert n % rows == 0
    vec = pl.BlockSpec((1, D_MODEL), lambda i, e: (0, 0))
    args = [x, nf]
    specs = [pl.BlockSpec((rows, D_MODEL), lambda i, e: (i, 0)), vec]
    if gated:
        args.append(router)
        specs.append(pl.BlockSpec((D_MODEL, LANES), lambda i, e: (0, 0)))
    args += [wg, wu, wd]
    specs += [pl.BlockSpec((1, D_MODEL, dff), lambda i, e: (e, 0, 0)),
              pl.BlockSpec((1, D_MODEL, dff), lambda i, e: (e, 0, 0)),
              pl.BlockSpec((1, dff, D_MODEL), lambda i, e: (e, 0, 0))]
    if final_norm:
        args.append(nfin)
        specs.append(vec)
    scratch = [pltpu.VMEM((rows, D_MODEL), BF16), pltpu.VMEM((rows, D_MODEL), F32)]
    if gated:
        scratch += [pltpu.VMEM((rows, LANES), F32)] * 3 + [pltpu.VMEM((n_sub * LANES, sub_rows), F32)] * 2
    return pl.pallas_call(
        functools.partial(_ffn_kernel, gated=gated, final_norm=final_norm, n_e=n_e, n_sub=n_sub),
        grid=(n // rows, n_e),
        in_specs=specs,
        out_specs=pl.BlockSpec((rows, D_MODEL), lambda i, e: (i, 0)),
        out_shape=jax.ShapeDtypeStruct((n, D_MODEL), F32),
        scratch_shapes=scratch,
        compiler_params=_params(("parallel", "arbitrary")),
        name="moe" if gated else "ffn",
    )(*args)


def _short_to_lanes_kernel(*refs, n_ops, t_len):
    for x_ref, o_ref in zip(refs[:n_ops], refs[n_ops:]):
        for t in range(t_len):
            o_ref[t] = x_ref[pl.ds(t, LANES, stride=t_len), :].T


def _short_to_lanes(xs, t_len):
    n_chunks = [width // LANES for _, _, width in xs]
    grid = max(n_chunks)
    clamp = lambda c, n: jnp.minimum(c, n - 1)
    outs = pl.pallas_call(
        functools.partial(_short_to_lanes_kernel, n_ops=len(xs), t_len=t_len),
        grid=(grid,),
        in_specs=[pl.BlockSpec((LANES * t_len, LANES), lambda c, c0=col // LANES, n=n: (0, c0 + clamp(c, n)))
                  for (_, col, _), n in zip(xs, n_chunks)],
        out_specs=[pl.BlockSpec((t_len, LANES, LANES), lambda c, n=n: (0, clamp(c, n), 0)) for n in n_chunks],
        out_shape=[jax.ShapeDtypeStruct((t_len, width, LANES), F32) for _, _, width in xs],
        compiler_params=_params(("arbitrary",)),
        name="short_to_lanes",
    )(*[x for x, _, _ in xs])
    return outs


def _short_from_lanes_kernel(y_ref, o_ref, *, t_len):
    for t in range(t_len):
        o_ref[pl.ds(t, LANES, stride=t_len), :] = y_ref[t].T


def _short_from_lanes(y):
    t_len, w, _ = y.shape
    return pl.pallas_call(
        functools.partial(_short_from_lanes_kernel, t_len=t_len),
        grid=(w // LANES,),
        in_specs=[pl.BlockSpec((t_len, LANES, LANES), lambda c: (0, c, 0))],
        out_specs=pl.BlockSpec((LANES * t_len, LANES), lambda c: (0, c)),
        out_shape=jax.ShapeDtypeStruct((LANES * t_len, w), F32),
        compiler_params=_params(("parallel",)),
        name="short_from_lanes",
    )(y)


def _state_to_lanes_kernel(x_ref, o_ref, *, ni, nj, j_first):
    xt = x_ref[...].T
    if j_first:
        for j in range(nj):
            o_ref[pl.ds(j, ni, stride=nj), :] = xt[j * ni:(j + 1) * ni, :]
    else:
        o_ref[...] = xt


def _state_to_lanes(s_all, layer, ni, nj, j_first):
    depth, _, h = s_all.shape[:3]
    out = pl.pallas_call(
        functools.partial(_state_to_lanes_kernel, ni=ni, nj=nj, j_first=j_first),
        grid=(h,),
        in_specs=[pl.BlockSpec((LANES, ni * nj), lambda i: (layer, i))],
        out_specs=pl.BlockSpec((ni * nj, LANES), lambda i: (i, 0)),
        out_shape=jax.ShapeDtypeStruct((h * ni * nj, LANES), F32),
        compiler_params=_params(("parallel",)),
        name="state_to_lanes",
    )(s_all.reshape(depth * LANES, h * ni * nj))
    return out.reshape(h * ni, nj, LANES)


def _state_from_lanes_kernel(s_ref, o_ref, tmp, *, ni, nj, j_first):
    if j_first:
        for j in range(nj):
            tmp[j * ni:(j + 1) * ni, :] = s_ref[pl.ds(j, ni, stride=nj), :]
        o_ref[...] = tmp[...].T
    else:
        o_ref[...] = s_ref[...].T


def _state_from_lanes(s, n_heads, ni, nj, j_first):
    out = pl.pallas_call(
        functools.partial(_state_from_lanes_kernel, ni=ni, nj=nj, j_first=j_first),
        grid=(n_heads,),
        in_specs=[pl.BlockSpec((ni * nj, LANES), lambda i: (i, 0))],
        out_specs=pl.BlockSpec((LANES, ni * nj), lambda i: (0, i)),
        out_shape=jax.ShapeDtypeStruct((LANES, n_heads * ni * nj), F32),
        scratch_shapes=[pltpu.VMEM((ni * nj, LANES), F32)],
        compiler_params=_params(("parallel",)),
        name="state_from_lanes",
    )(s.reshape(n_heads * ni * nj, LANES))
    return out.reshape((LANES, n_heads, nj, ni) if j_first else (LANES, n_heads, ni, nj))


def _rwkv_long(feats, b):
    r, dec, k2, v, nkk, beta = feats
    t = r.shape[0] // b
    kl = HEAD_DIM // K_SPLIT
    assert b * H_PAD == HALF and t % TR == 0
    nkk_next = jnp.concatenate([nkk[1:], jnp.zeros((1, nkk.shape[1]), F32)], axis=0)
    q, d, a, n1, a2 = _long_to_lanes([r, dec, k2, nkk_next, beta], b, kl, True)
    (vv,) = _long_to_lanes([v], b, HEAD_DIM, False)
    y, s = _rwkv_long_scan(q, d, a, vv, n1, a2)
    s = s.reshape(kl, HEAD_DIM, K_SPLIT, b, H_PAD)[..., :H_A]
    return _long_from_lanes(y, b), jnp.transpose(s, (3, 4, 1, 2, 0)).reshape(b, H_A, HEAD_DIM, HEAD_DIM)


def _rwkv_short(feats, t_len, s0):
    r, dec, k2, v, nkk, beta = _short_to_lanes([(f, 0, W_A) for f in feats], t_len)
    hd = HEAD_DIM
    own = lambda h: h
    y, s = _scan("rwkv", H_A, hd, hd, (r, own), (dec, own), (k2, own), (v, own),
                 _state_to_lanes(*s0, hd, hd, True), n=(nkk, own), a2=(beta, own))
    return _short_from_lanes(y), _state_from_lanes(s, H_A, hd, hd, True)


def _ret_short(qh, kh, cb, t_len, s0):
    q, k, v = _short_to_lanes([(qh, 0, W_B), (kh, 0, W_B), (cb, 2 * W_B, W_B)], t_len)
    hd = HEAD_DIM
    gam = np.ones((SUBLANES, LANES), np.float32)
    gam[:H_B] = (1.0 - 2.0 ** (-5.0 - np.arange(H_B, dtype=np.float64)))[:, None]
    own = lambda h: h
    y, s = _scan("ret", H_B, hd, hd, (q, own), jnp.asarray(gam), (k, own), (v, own),
                 _state_to_lanes(*s0, hd, hd, False))
    return _short_from_lanes(y), _state_from_lanes(s, H_B, hd, hd, False)


def _ssd_short(xdt, bm, cm, dssm, t_len, s0):
    gn = N_GROUPS * N_STATE
    x, bl, cl, dl = _short_to_lanes([(xdt, 0, W_C), (bm, 0, gn), (cm, 0, gn), (dssm, 0, LANES)], t_len)
    group = lambda h: h // (H_C // N_GROUPS)
    y, s = _scan("ssd", H_C, N_STATE, HEAD_DIM, (cl, group), dl, (bl, group), (x, lambda h: h),
                 _state_to_lanes(*s0, N_STATE, HEAD_DIM, True))
    return _short_from_lanes(y), _state_from_lanes(s, H_C, N_STATE, HEAD_DIM, True)


def _block_diag_ones(width):
    idx = np.arange(width) // HEAD_DIM
    return jnp.asarray((idx[:, None] == idx[None, :]).astype(np.float32), BF16)


def _shifted(x, prev_rows, shift):
    b, t, c = x.shape
    p = prev_rows.shape[1]
    full = jnp.concatenate([prev_rows, x], axis=1)
    return full[:, p - shift:p - shift + t].reshape(b * t, c)


def _layer(x, b, p, st, rope, table_block):
    n = x.shape[0]
    t = n // b
    fresh = st is None
    tiles_per_seq = t // TM if fresh else 0
    ca, cb, cc = _in_proj(x, p["norm_mix"], p["w_in"])
    ca3 = ca.reshape(b, t, COLS_A)
    xbc_tail = cc.reshape(b, t, COLS_C_PAD)[:, -(CONV_W - 1):, W_C:W_C + CONV_DIM]
    if fresh:
        prev, shifted = None, None
        conv_new = xbc_tail
    else:
        prev = _shifted(ca3, st["shift"][:, None, :], 1)
        xbc = cc[:, W_C:W_C + CONV_DIM].reshape(b, t, CONV_DIM)
        shifted = [_shifted(xbc, st["conv"], j) for j in range(1, CONV_W)]
        conv_new = jnp.concatenate([st["conv"], xbc_tail], axis=1)[:, -(CONV_W - 1):]

    r, dec, k2, v, nkk, beta, ga, bonus = _rwkv_prep(ca, prev, p, tiles_per_seq)
    qh, kh = _ret_prep(cb, rope[0], rope[1], table_block)
    xdt, bm, cm, dssm, xs, da = _ssm_prep(cc, shifted, p, tiles_per_seq)

    feats = (r, dec, k2, v, nkk, beta)
    if fresh:
        ya, s_rwkv = _rwkv_long(feats, b)
        yb, s_full = _ret_chunk(qh, kh, cb, b, t)
        s_ret = jnp.stack([s_full[:, h * HEAD_DIM:(h + 1) * HEAD_DIM, h * HEAD_DIM:(h + 1) * HEAD_DIM]
                           for h in range(H_B)], axis=1)
        yc, s_ssm = _ssd_chunk(xdt, bm, cm, da, p["expand"], b, t)
        s_ssm = s_ssm.reshape(b, H_C, HEAD_DIM, N_STATE)
    else:
        assert b == LANES
        ya, s_rwkv = _rwkv_short(feats, t, st["rwkv"])
        yb, s_ret = _ret_short(qh, kh, cb, t, st["ret"])
        yc, s_ssm = _ssd_short(xdt, bm, cm, dssm, t, st["ssm"])

    x2 = _post(x, ya, bonus, ga, yb, cb, yc, xs, cc, p)
    x = _ffn(x2, p["norm_ffn"], p["wg"], p["wu"], p["wd"], router=p["router"], nfin=p["norm_final"])
    return x, (s_rwkv, ca3[:, -1], s_ret, s_ssm, conv_new)


def kernel(x_prompt, x_sample, state_rwkv, state_shift, state_ret, state_ssm, state_conv, norm_mix, w_in, rwkv_mu, rwkv_w0, rwkv_w_up, rwkv_a0, rwkv_a_up, rwkv_g_up, rwkv_k_k, rwkv_k_a, rwkv_r_k, rwkv_ln_w, rwkv_ln_b, ret_norm, ssm_conv_w, ssm_conv_b, ssm_dt_bias, ssm_a_log, ssm_d, ssm_norm, w_out, norm_ffn, ffn_w_gate, ffn_w_up, ffn_w_down, moe_router, moe_w_gate, moe_w_up, moe_w_down, norm_final):
    bp, tp, _ = x_prompt.shape
    bs, ts, _ = x_sample.shape
    depth = w_in.shape[0]
    assert tp % TM == 0 and tp % LC == 0 and (bs * ts) % TM == 0 and TM % ts == 0

    rope = _rope_tables(tp, ts)
    bd_a, bd_b = _block_diag_ones(W_A), _block_diag_ones(W_B)
    expand = np.zeros((LANES, W_C), np.float32)
    expand[np.arange(W_C) // HEAD_DIM, np.arange(W_C)] = 1.0
    expand = jnp.asarray(expand, BF16)
    row = lambda v: v.reshape(1, -1)
    pad_l = lambda v: jnp.pad(v, (0, LANES - v.shape[0])).reshape(1, LANES)

    xp = x_prompt.reshape(bp * tp, D_MODEL)
    xs = x_sample.reshape(bs * ts, D_MODEL)
    new_p, new_s = [], []
    for i in range(depth):
        j = i // 2
        p = dict(
            norm_mix=row(norm_mix[i]),
            w_in=jnp.pad(w_in[i], ((0, 0), (0, COLS_C_PAD - COLS_C))).astype(BF16),
            mu=row(rwkv_mu[i]), w0=row(rwkv_w0[i]), a0=row(rwkv_a0[i]), k_k=row(rwkv_k_k[i]),
            k_a=row(rwkv_k_a[i]), r_k=row(rwkv_r_k[i]),
            w_up=jnp.pad(rwkv_w_up[i], ((0, AAA_LORA), (0, 0))).astype(BF16),
            a_up=jnp.pad(rwkv_a_up[i], ((DECAY_LORA, 0), (0, 0))).astype(BF16),
            g_up=rwkv_g_up[i].astype(BF16), bd_a=bd_a, bd_b=bd_b,
            conv_w=ssm_conv_w[i], conv_b=row(ssm_conv_b[i]), dt_bias=pad_l(ssm_dt_bias[i]),
            a_log=pad_l(ssm_a_log[i]), expand=expand,
            ln_w=row(rwkv_ln_w[i]), ln_b=row(rwkv_ln_b[i]), ret_norm=row(ret_norm[i]),
            d_skip=row(jnp.repeat(ssm_d[i], HEAD_DIM)), ssm_norm=row(ssm_norm[i]), w_out=w_out[i].astype(BF16),
            norm_ffn=row(norm_ffn[i]), norm_final=row(norm_final) if i == depth - 1 else None)
        if i % 2 == 0:
            dff = ffn_w_gate.shape[-1] // 2
            p.update(router=None,
                     wg=ffn_w_gate[j].reshape(D_MODEL, 2, dff).transpose(1, 0, 2).astype(BF16),
                     wu=ffn_w_up[j].reshape(D_MODEL, 2, dff).transpose(1, 0, 2).astype(BF16),
                     wd=ffn_w_down[j].reshape(2, dff, D_MODEL).astype(BF16))
        else:
            p.update(router=jnp.pad(moe_router[j], ((0, 0), (0, LANES - N_EXPERTS))),
                     wg=moe_w_gate[j].astype(BF16), wu=moe_w_up[j].astype(BF16), wd=moe_w_down[j].astype(BF16))

        xp, st_p = _layer(xp, bp, p, None, rope, lambda t: t % (tp // TM))
        st = dict(rwkv=(state_rwkv, i), shift=state_shift[i], ret=(state_ret, i), ssm=(state_ssm, i),
                  conv=state_conv[i])
        xs, st_s = _layer(xs, bs, p, st, rope, lambda t: tp // TM)
        new_p.append(st_p)
        new_s.append(st_s)

    stack = lambda sts: tuple(jnp.stack(s) for s in zip(*sts))
    return (xp.reshape(bp, tp, D_MODEL), xs.reshape(bs, ts, D_MODEL)) + stack(new_p) + stack(new_s)
```

```python
import functools
import math

import numpy as np
import jax
import jax.numpy as jnp
from jax import lax
from jax.experimental import pallas as pl
from jax.experimental.pallas import tpu as pltpu

F32 = jnp.float32
BF16 = jnp.bfloat16
HIGHEST = lax.Precision.HIGHEST

LANES = 128
SUBLANES = 8
VMEM_LIMIT = 56 * 1024 * 1024

D_MODEL = 1024
HEAD_DIM = 64
H_A, H_B, H_C = 6, 4, 6
W_A, W_B, W_C = H_A * HEAD_DIM, H_B * HEAD_DIM, H_C * HEAD_DIM
DECAY_LORA, AAA_LORA, GATE_LORA = 64, 64, 128
COLS_A = 3 * W_A + DECAY_LORA + AAA_LORA + GATE_LORA
COLS_B = 4 * W_B
N_STATE, N_GROUPS, CONV_W = 128, 2, 4
CONV_DIM = W_C + 2 * N_GROUPS * N_STATE
COLS_C = W_C + CONV_DIM + H_C
COLS_C_PAD = 1408
ROPE_BASE = 10000.0
RMS_EPS = 1e-6
GN_EPS = 64e-5
GATED_NORM_EPS = 1e-5
N_EXPERTS = 8
PAST_LEN = 16384

TM = 512
MOE_ROWS = 512
MOE_CHUNK = 160
MOE_SUB = 2
LC = 256
H_PAD = 8
K_SPLIT = 2


def _dot(a, b):
    return jnp.dot(a.astype(BF16), b.astype(BF16), preferred_element_type=F32)


def _dot_hi(a, b):
    return jnp.dot(a, b, precision=HIGHEST, preferred_element_type=F32)


def _dot_select(a, sel, terms=2):
    out = None
    rest = a
    for _ in range(terms):
        piece = rest.astype(BF16)
        part = jnp.dot(piece, sel, preferred_element_type=F32)
        out = part if out is None else out + part
        rest = rest - piece.astype(F32)
    return out


def _sigmoid(x):
    return 1.0 / (1.0 + jnp.exp(-x))


def _softplus(x):
    return jnp.maximum(x, 0.0) + jnp.log1p(jnp.exp(-jnp.abs(x)))


def _rmsnorm(x, g, eps):
    return x * lax.rsqrt(jnp.mean(x * x, axis=-1, keepdims=True) + eps) * g


def _params(sem):
    return pltpu.CompilerParams(dimension_semantics=sem, vmem_limit_bytes=VMEM_LIMIT)


def _row_spec(width, col=0):
    return pl.BlockSpec((TM, width), lambda i, c=col: (i, c))


def _before_spec(width):
    return pl.BlockSpec((SUBLANES, width), lambda i: (jnp.maximum(i * (TM // SUBLANES) - 1, 0), 0))


def _full_spec(shape):
    nd = len(shape)
    return pl.BlockSpec(shape, lambda i, n=nd: (0,) * n)


def _shift_rows(x, before, j, first):
    rolled = pltpu.roll(x, j, 0)
    prev = jnp.where(first, 0.0, pltpu.roll(before, j, 0))
    row = lax.broadcasted_iota(jnp.int32, prev.shape, 0)
    top = jnp.where(row < j, prev, rolled[0:SUBLANES])
    return jnp.concatenate([top, rolled[SUBLANES:]], axis=0)


def _in_proj_kernel(x_ref, g_ref, w_ref, oa_ref, ob_ref, oc_ref):
    h = _rmsnorm(x_ref[...], g_ref[...], RMS_EPS).astype(BF16)
    oa_ref[...] = jnp.dot(h, w_ref[:, 0:COLS_A], preferred_element_type=F32)
    ob_ref[...] = jnp.dot(h, w_ref[:, COLS_A:COLS_A + COLS_B], preferred_element_type=F32)
    oc_ref[...] = jnp.dot(h, w_ref[:, COLS_A + COLS_B:], preferred_element_type=F32)


def _in_proj(x, g, w):
    n = x.shape[0]
    wtot = w.shape[1]
    return pl.pallas_call(
        _in_proj_kernel,
        grid=(n // TM,),
        in_specs=[_row_spec(D_MODEL), _full_spec((1, D_MODEL)), _full_spec((D_MODEL, wtot))],
        out_specs=[_row_spec(COLS_A), _row_spec(COLS_B), _row_spec(COLS_C_PAD)],
        out_shape=[jax.ShapeDtypeStruct((n, COLS_A), F32), jax.ShapeDtypeStruct((n, COLS_B), F32),
                   jax.ShapeDtypeStruct((n, COLS_C_PAD), F32)],
        compiler_params=_params(("parallel",)),
        name="in_proj",
    )(x, g, w)


def _rwkv_prep_kernel(c_ref, p_ref, mu_ref, w0_ref, a0_ref, kk_ref, ka_ref, rk_ref, wup_ref, aup_ref,
                      gup_ref, bd_ref, r_o, d_o, k_o, v_o, n_o, b_o, g_o, bonus_o, *, tiles_per_seq):
    c = c_ref[...]
    if tiles_per_seq:
        prev = _shift_rows(c, p_ref[...], 1, pl.program_id(0) % tiles_per_seq == 0)
    else:
        prev = p_ref[...]
    xm = c + (prev - c) * mu_ref[...]
    r = xm[:, 0:W_A]
    k = xm[:, W_A:2 * W_A]
    v = xm[:, 2 * W_A:3 * W_A]
    lora = xm[:, 3 * W_A:3 * W_A + DECAY_LORA + AAA_LORA]
    gd = xm[:, 3 * W_A + DECAY_LORA + AAA_LORA:]
    w = w0_ref[...] + _dot(jnp.tanh(lora), wup_ref[...])
    decay = jnp.exp(-math.exp(-0.5) * _sigmoid(w))
    a = _sigmoid(a0_ref[...] + _dot(lora, aup_ref[...]))
    g = _dot(_sigmoid(gd), gup_ref[...])
    bd = bd_ref[...]
    kk = k * kk_ref[...]
    kk = kk / jnp.maximum(jnp.sqrt(_dot_select(kk * kk, bd)), 1e-12)
    k2 = k * (1.0 + (a - 1.0) * ka_ref[...])
    for o_ref, val in ((r_o, r), (d_o, decay), (k_o, k2), (v_o, v), (n_o, -kk), (b_o, kk * a)):
        o_ref[:, 0:W_A] = val
        if o_ref.shape[1] > W_A:
            o_ref[:, W_A:] = jnp.zeros((TM, o_ref.shape[1] - W_A), F32)
    g_o[...] = g
    bonus_o[...] = _dot_select(r * k2 * rk_ref[...], bd) * v


def _rwkv_prep(ca, prev, p, tiles_per_seq):
    n = ca.shape[0]
    vec = _full_spec((1, W_A))
    lora_spec = _full_spec((DECAY_LORA + AAA_LORA, W_A))
    p_spec = _before_spec(COLS_A) if tiles_per_seq else _row_spec(COLS_A)
    w_scan = H_PAD * HEAD_DIM if tiles_per_seq else W_A
    widths = [w_scan] * 6 + [W_A] * 2
    return pl.pallas_call(
        functools.partial(_rwkv_prep_kernel, tiles_per_seq=tiles_per_seq),
        grid=(n // TM,),
        in_specs=[_row_spec(COLS_A), p_spec, _full_spec((1, COLS_A)), vec, vec, vec, vec, vec,
                  lora_spec, lora_spec, _full_spec((GATE_LORA, W_A)), _full_spec((W_A, W_A))],
        out_specs=[_row_spec(w) for w in widths],
        out_shape=[jax.ShapeDtypeStruct((n, w), F32) for w in widths],
        compiler_params=_params(("parallel",)),
        name="rwkv_prep",
    )(ca, ca if tiles_per_seq else prev, p["mu"], p["w0"], p["a0"], p["k_k"], p["k_a"], p["r_k"],
      p["w_up"], p["a_up"], p["g_up"], p["bd_a"])


def _rope_kernel(ang_ref, cos_o, sin_o):
    ang = ang_ref[...]
    lane = lax.broadcasted_iota(jnp.int32, ang.shape, 1)
    cos_o[...] = jnp.cos(ang)
    sin_o[...] = jnp.where((lane % HEAD_DIM) < (HEAD_DIM // 2), -jnp.sin(ang), jnp.sin(ang))


def _rope_tables(t_prompt, t_sample):
    theta = 1.0 / (ROPE_BASE ** jnp.linspace(0.0, 1.0, HEAD_DIM // 2, dtype=F32))
    pos = jnp.concatenate([jnp.arange(t_prompt, dtype=F32), PAST_LEN + (jnp.arange(TM) % t_sample).astype(F32)])
    ang = jnp.tile(pos[:, None] * theta[None, :], (1, W_B // (HEAD_DIM // 2)))
    n = ang.shape[0]
    return pl.pallas_call(
        _rope_kernel,
        grid=(n // TM,),
        in_specs=[_row_spec(W_B)],
        out_specs=[_row_spec(W_B)] * 2,
        out_shape=[jax.ShapeDtypeStruct((n, W_B), F32)] * 2,
        compiler_params=_params(("parallel",)),
        name="rope_tables",
    )(ang)


def _ret_prep_kernel(q_ref, k_ref, cos_ref, sin_ref, q_o, k_o):
    cos = cos_ref[...]
    sin = sin_ref[...]
    lane = lax.broadcasted_iota(jnp.int32, (TM, W_B), 1)
    first_half = (lane % HEAD_DIM) < (HEAD_DIM // 2)

    def rope(x):
        partner = jnp.where(first_half, pltpu.roll(x, W_B - HEAD_DIM // 2, 1), pltpu.roll(x, HEAD_DIM // 2, 1))
        return x * cos + partner * sin

    q_o[...] = rope(q_ref[...])
    k_o[...] = rope(k_ref[...]) * (HEAD_DIM ** -0.5)


def _ret_prep(cb, cos, sin, table_block):
    n = cb.shape[0]
    tab = pl.BlockSpec((TM, W_B), lambda i: (table_block(i), 0))
    return pl.pallas_call(
        _ret_prep_kernel,
        grid=(n // TM,),
        in_specs=[_row_spec(W_B, 0), _row_spec(W_B, 1), tab, tab],
        out_specs=[_row_spec(W_B)] * 2,
        out_shape=[jax.ShapeDtypeStruct((n, W_B), F32)] * 2,
        compiler_params=_params(("parallel",)),
        name="ret_prep",
    )(cb, cb, cos, sin)


def _ssm_prep_kernel(*refs, tiles_per_seq):
    n_shift_refs = 1 if tiles_per_seq else CONV_W - 1
    cc_ref = refs[0]
    shift_refs = refs[1:1 + n_shift_refs]
    cw_ref, cb_ref, dtb_ref, alog_ref, ex_ref, xdt_o, b_o, c_o, dec_o, xs_o, da_o = refs[1 + n_shift_refs:]
    cc = cc_ref[...]
    x0 = cc[:, W_C:W_C + CONV_DIM]
    if tiles_per_seq:
        before = shift_refs[0][:, W_C:W_C + CONV_DIM]
        first = pl.program_id(0) % tiles_per_seq == 0
        xs_prev = [_shift_rows(x0, before, j, first) for j in range(1, CONV_W)]
    else:
        xs_prev = [r[...] for r in shift_refs]
    cw = cw_ref[...]
    conv = x0 * cw[CONV_W - 1:CONV_W, :] + cb_ref[...]
    for j in range(1, CONV_W):
        conv = conv + xs_prev[j - 1] * cw[CONV_W - 1 - j:CONV_W - j, :]
    act = conv * _sigmoid(conv)
    xs = act[:, 0:W_C]
    dt = _softplus(cc[:, W_C + CONV_DIM:] + dtb_ref[...])
    da = dt * (-jnp.exp(alog_ref[...]))
    da_o[...] = da
    dec_o[...] = jnp.exp(da)
    xdt_o[...] = xs * _dot_select(dt, ex_ref[...])
    b_o[...] = act[:, W_C:W_C + N_GROUPS * N_STATE]
    c_o[...] = act[:, W_C + N_GROUPS * N_STATE:]
    xs_o[...] = xs


def _ssm_prep(cc, shifted, p, tiles_per_seq):
    n = cc.shape[0]
    gn = N_GROUPS * N_STATE
    if tiles_per_seq:
        shift_args, shift_specs = [cc], [_before_spec(COLS_C_PAD)]
    else:
        shift_args, shift_specs = list(shifted), [_row_spec(CONV_DIM)] * (CONV_W - 1)
    return pl.pallas_call(
        functools.partial(_ssm_prep_kernel, tiles_per_seq=tiles_per_seq),
        grid=(n // TM,),
        in_specs=[_row_spec(COLS_C_PAD)] + shift_specs + [
            _full_spec((CONV_W, CONV_DIM)), _full_spec((1, CONV_DIM)), _full_spec((1, LANES)),
            _full_spec((1, LANES)), _full_spec((LANES, W_C))],
        out_specs=[_row_spec(W_C), _row_spec(gn), _row_spec(gn), _row_spec(LANES), _row_spec(W_C),
                   _row_spec(LANES)],
        out_shape=[jax.ShapeDtypeStruct((n, W_C), F32), jax.ShapeDtypeStruct((n, gn), F32),
                   jax.ShapeDtypeStruct((n, gn), F32), jax.ShapeDtypeStruct((n, LANES), F32),
                   jax.ShapeDtypeStruct((n, W_C), F32), jax.ShapeDtypeStruct((n, LANES), F32)],
        compiler_params=_params(("parallel",)),
        name="ssm_prep",
    )(cc, *shift_args, p["conv_w"], p["conv_b"], p["dt_bias"], p["a_log"], p["expand"])


def _scan_kernel(*refs, mode, ni, tt_len, n_tt):
    if mode == "rwkv":
        q_ref, d_ref, a_ref, b_ref, n_ref, a2_ref, s0_ref, y_ref, so_ref, st = refs
    else:
        q_ref, d_ref, a_ref, b_ref, s0_ref, y_ref, so_ref, st = refs
    head = pl.program_id(0)
    tt = pl.program_id(1)

    @pl.when(tt == 0)
    def _():
        st[...] = s0_ref[...]

    def row(ref, t, i):
        return ref[t, pl.ds(i, 1), :]

    def step(t, carry):
        bv = b_ref[t]
        if mode == "rwkv":
            sa_parts = [jnp.zeros_like(bv), jnp.zeros_like(bv)]
            for i in range(ni):
                sa_parts[i % 2] = sa_parts[i % 2] + st[i] * row(n_ref, t, i)
            sa = sa_parts[0] + sa_parts[1]
        elif mode == "ssd":
            d = d_ref[t, pl.ds(head, 1), :]
        else:
            d = d_ref[pl.ds(head, 1), :]
        y_parts = [jnp.zeros_like(bv), jnp.zeros_like(bv)]
        for i in range(ni):
            if mode == "rwkv":
                s = st[i] * row(d_ref, t, i) + row(a_ref, t, i) * bv + row(a2_ref, t, i) * sa
            else:
                s = st[i] * d + row(a_ref, t, i) * bv
            st[i] = s
            y_parts[i % 2] = y_parts[i % 2] + s * row(q_ref, t, i)
        y_ref[t] = y_parts[0] + y_parts[1]
        return carry

    lax.fori_loop(0, tt_len, step, 0)

    @pl.when(tt == n_tt - 1)
    def _():
        so_ref[...] = st[...]


def _scan(mode, n_heads, ni, nj, q, d, a, b, s0, n=None, a2=None):
    t_len = b[0].shape[0]

    def rows(op, width):
        arr, block_of = op
        return arr, pl.BlockSpec((t_len, width, LANES), lambda h, t, f=block_of: (0, f(h), 0))

    ops = [rows(q, ni)]
    if mode == "rwkv":
        ops.append(rows(d, ni))
    elif mode == "ssd":
        ops.append((d, pl.BlockSpec((t_len, SUBLANES, LANES), lambda h, t: (0, 0, 0))))
    else:
        ops.append((d, pl.BlockSpec((SUBLANES, LANES), lambda h, t: (0, 0))))
    ops += [rows(a, ni), rows(b, nj)]
    if mode == "rwkv":
        ops += [rows(n, ni), rows(a2, ni)]
    s_spec = pl.BlockSpec((ni, nj, LANES), lambda h, t: (h, 0, 0))
    y_spec = pl.BlockSpec((t_len, nj, LANES), lambda h, t: (0, h, 0))
    return pl.pallas_call(
        functools.partial(_scan_kernel, mode=mode, ni=ni, tt_len=t_len, n_tt=1),
        grid=(n_heads, 1),
        in_specs=[spec for _, spec in ops] + [s_spec],
        out_specs=[y_spec, s_spec],
        out_shape=[jax.ShapeDtypeStruct((t_len, n_heads * nj, LANES), F32),
                   jax.ShapeDtypeStruct((n_heads * ni, nj, LANES), F32)],
        scratch_shapes=[pltpu.VMEM((ni, nj, LANES), F32)],
        compiler_params=_params(("parallel", "arbitrary")),
        name="scan_" + mode,
    )(*[arr for arr, _ in ops], s0)


def _rwkv_long_kernel(q_ref, d_ref, a_ref, b_ref, n1_ref, a2_ref, y_ref, so_ref, st, sa_s, dots_s, rows_s, *,
                      ni, tt_len, n_tt):
    tt = pl.program_id(1)

    @pl.when(tt == 0)
    def _():
        st[...] = jnp.zeros_like(st)
        sa_s[...] = jnp.zeros_like(sa_s)

    refs_i = (q_ref, d_ref, a_ref, n1_ref, a2_ref)
    pitch = tt_len + SUBLANES
    for k, ref in enumerate(refs_i):
        for i in range(ni):
            rows_s[i, k * pitch:k * pitch + tt_len, :] = ref[i]

    op_q, op_d, op_a, op_n1, op_a2 = range(len(refs_i))

    def row(k, t, i):
        return rows_s[i, pl.ds(k * pitch + t, 1), :]

    def fold(x):
        return x + pltpu.roll(x, LANES // 2, 1)

    for k, (u_ref, w_ref) in enumerate(((a_ref, q_ref), (a2_ref, q_ref), (a_ref, n1_ref), (a2_ref, n1_ref))):
        acc = u_ref[0] * w_ref[0]
        for i in range(1, ni):
            acc = acc + u_ref[i] * w_ref[i]
        dots_s[k] = fold(acc)

    def step(t, sa):
        bv = b_ref[t]
        aq, a2q, an, a2n = [dots_s[k, pl.ds(t, 1), :] for k in range(4)]
        y0 = [jnp.zeros_like(bv), jnp.zeros_like(bv)]
        n0 = [jnp.zeros_like(bv), jnp.zeros_like(bv)]
        for i in range(ni):
            sd = st[i] * row(op_d, t, i)
            y0[i % 2] = y0[i % 2] + sd * row(op_q, t, i)
            n0[i % 2] = n0[i % 2] + sd * row(op_n1, t, i)
            st[i] = sd + row(op_a, t, i) * bv + row(op_a2, t, i) * sa
        y_ref[t] = fold(y0[0] + y0[1]) + bv * aq + sa * a2q
        return fold(n0[0] + n0[1]) + bv * an + sa * a2n

    sa_s[...] = lax.fori_loop(0, tt_len, step, sa_s[...])

    @pl.when(tt == n_tt - 1)
    def _():
        so_ref[...] = st[...]


def _rwkv_long_scan(q, d, a, b, n1, a2, tt_len=128):
    ni, t_len, _ = q.shape
    nj = b.shape[1]
    n_tt = t_len // tt_len
    i_spec = pl.BlockSpec((ni, tt_len, LANES), lambda g, t: (0, t, 0))
    j_spec = pl.BlockSpec((tt_len, nj, LANES), lambda g, t: (t, 0, 0))
    s_spec = pl.BlockSpec((ni, nj, LANES), lambda g, t: (0, 0, 0))
    return pl.pallas_call(
        functools.partial(_rwkv_long_kernel, ni=ni, tt_len=tt_len, n_tt=n_tt),
        grid=(1, n_tt),
        in_specs=[i_spec, i_spec, i_spec, j_spec, i_spec, i_spec],
        out_specs=[j_spec, s_spec],
        out_shape=[jax.ShapeDtypeStruct((t_len, nj, LANES), F32), jax.ShapeDtypeStruct((ni, nj, LANES), F32)],
        scratch_shapes=[pltpu.VMEM((ni, nj, LANES), F32), pltpu.VMEM((nj, LANES), F32),
                        pltpu.VMEM((4, tt_len, LANES), F32),
                        pltpu.VMEM((ni, 5 * (tt_len + SUBLANES), LANES), F32)],
        compiler_params=_params(("parallel", "arbitrary")),
        name="scan_rwkv_long",
    )(q, d, a, b, n1, a2)


TR = 128
HALF = LANES // K_SPLIT


Y_PITCH = LANES + SUBLANES


def _to_lanes_kernel(*refs, n_ops, rows, per_half):
    x_refs, o_refs, y2d = refs[:n_ops], refs[n_ops:2 * n_ops], refs[2 * n_ops]
    n_b = x_refs[0].shape[0]
    for x_ref, o_ref in zip(x_refs, o_refs):
        for b in range(n_b):
            xt = x_ref[b].T
            for h in range(H_PAD):
                for s in range(K_SPLIT):
                    lane = s * HALF + b * H_PAD + h
                    f0 = h * HEAD_DIM + (s * rows if per_half else 0)
                    y2d[pl.ds(lane, rows, stride=Y_PITCH), :] = xt[f0:f0 + rows, :]
        for r in range(rows):
            slab = y2d[r * Y_PITCH:r * Y_PITCH + LANES, :].T
            if per_half:
                o_ref[r] = slab
            else:
                o_ref[pl.ds(r, TR, stride=rows), :] = slab


def _long_to_lanes(xs, b, rows, per_half):
    n_ops = len(xs)
    t = xs[0].shape[0] // b
    width = H_PAD * HEAD_DIM
    if per_half:
        o_spec = pl.BlockSpec((rows, TR, LANES), lambda i: (0, i, 0))
        o_shape = jax.ShapeDtypeStruct((rows, t, LANES), F32)
    else:
        o_spec = pl.BlockSpec((TR * rows, LANES), lambda i: (i, 0))
        o_shape = jax.ShapeDtypeStruct((t * rows, LANES), F32)
    outs = pl.pallas_call(
        functools.partial(_to_lanes_kernel, n_ops=n_ops, rows=rows, per_half=per_half),
        grid=(t // TR,),
        in_specs=[pl.BlockSpec((b, TR, width), lambda i: (0, i, 0))] * n_ops,
        out_specs=[o_spec] * n_ops,
        out_shape=[o_shape] * n_ops,
        scratch_shapes=[pltpu.VMEM((rows * Y_PITCH, TR), F32)],
        compiler_params=_params(("parallel",)),
        name="to_lanes",
    )(*[x.reshape(b, t, width) for x in xs])
    return outs if per_half else [o.reshape(t, rows, LANES) for o in outs]


def _from_lanes_kernel(y_ref, o_ref, z2d):
    for v in range(HEAD_DIM):
        z2d[pl.ds(v, LANES, stride=HEAD_DIM), :] = y_ref[pl.ds(v, TR, stride=HEAD_DIM), :].T
    for b in range(o_ref.shape[0]):
        r0 = b * H_PAD * HEAD_DIM
        o_ref[b] = z2d[r0:r0 + W_A, :].T


def _long_from_lanes(y, b):
    t = y.shape[0]
    out = pl.pallas_call(
        _from_lanes_kernel,
        grid=(t // TR,),
        in_specs=[pl.BlockSpec((TR * HEAD_DIM, LANES), lambda i: (i, 0))],
        out_specs=pl.BlockSpec((b, TR, W_A), lambda i: (0, i, 0)),
        out_shape=jax.ShapeDtypeStruct((b, t, W_A), F32),
        scratch_shapes=[pltpu.VMEM((LANES * HEAD_DIM, TR), F32)],
        compiler_params=_params(("parallel",)),
        name="from_lanes",
    )(y.reshape(t * HEAD_DIM, LANES))
    return out.reshape(b * t, W_A)


def _ret_chunk_kernel(q_ref, k_ref, v_ref, y_ref, s_ref, st):
    c = pl.program_id(1)

    @pl.when(c == 0)
    def _():
        st[...] = jnp.zeros_like(st)

    q = q_ref[...]
    k = k_ref[...]
    vb = v_ref[...].astype(BF16)
    kb = k.astype(BF16)
    row = lax.broadcasted_iota(jnp.int32, (LC, W_B), 0)
    head = lax.broadcasted_iota(jnp.int32, (LC, W_B), 1) // HEAD_DIM
    log_g = [math.log(1.0 - 2.0 ** (-5.0 - h)) for h in range(H_B)]
    lg = jnp.full((LC, W_B), log_g[0], F32)
    for h in range(1, H_B):
        lg = jnp.where(head == h, log_g[h], lg)
    rowf = row.astype(F32)
    diff = (lax.broadcasted_iota(jnp.int32, (LC, LC), 0) - lax.broadcasted_iota(jnp.int32, (LC, LC), 1))
    causal = diff >= 0
    difff = jnp.maximum(diff, 0).astype(F32)

    out = jnp.dot((q * jnp.exp(lg * (rowf + 1.0))).astype(BF16), st[...].astype(BF16), preferred_element_type=F32)
    for h in range(H_B):
        qm = jnp.where(head == h, q, 0.0).astype(BF16)
        s = lax.dot_general(qm, kb, (((1,), (1,)), ((), ())), preferred_element_type=F32)
        p = jnp.where(causal, s * jnp.exp(log_g[h] * difff), 0.0).astype(BF16)
        out = out + jnp.where(head == h, jnp.dot(p, vb, preferred_element_type=F32), 0.0)
    y_ref[...] = out

    kt = (k * jnp.exp(lg * (LC - 1.0 - rowf))).T.astype(BF16)
    kv = jnp.dot(kt, vb, preferred_element_type=F32)
    r2 = lax.broadcasted_iota(jnp.int32, (W_B, W_B), 0) // HEAD_DIM
    c2 = lax.broadcasted_iota(jnp.int32, (W_B, W_B), 1) // HEAD_DIM
    cdec = jnp.full((W_B, W_B), math.exp(log_g[0] * LC), F32)
    for h in range(1, H_B):
        cdec = jnp.where(r2 == h, math.exp(log_g[h] * LC), cdec)
    st[...] = st[...] * cdec + jnp.where(r2 == c2, kv, 0.0)
    s_ref[0] = st[...]


def _ret_chunk(q, k, cb, n_seq, t_len):
    n_c = t_len // LC
    rows = lambda col: pl.BlockSpec((LC, W_B), lambda b, c, col=col: (b * n_c + c, col))
    return pl.pallas_call(
        _ret_chunk_kernel,
        grid=(n_seq, n_c),
        in_specs=[rows(0), rows(0), rows(2)],
        out_specs=[rows(0), pl.BlockSpec((1, W_B, W_B), lambda b, c: (b, 0, 0))],
        out_shape=[jax.ShapeDtypeStruct((n_seq * t_len, W_B), F32), jax.ShapeDtypeStruct((n_seq, W_B, W_B), F32)],
        scratch_shapes=[pltpu.VMEM((W_B, W_B), F32)],
        compiler_params=_params(("parallel", "arbitrary")),
        name="ret_chunk",
    )(q, k, cb)


def _ssd_chunk_kernel(x_ref, b_ref, c_ref, da_ref, ex_ref, y_ref, h_ref, st):
    ci = pl.program_id(1)

    @pl.when(ci == 0)
    def _():
        st[...] = jnp.zeros_like(st)

    gn = N_STATE
    half = W_C // N_GROUPS
    ex = ex_ref[...]
    x = x_ref[...]
    xb = x.astype(BF16)
    ri = lax.broadcasted_iota(jnp.int32, (LC, LC), 0)
    cj = lax.broadcasted_iota(jnp.int32, (LC, LC), 1)
    causal = ri >= cj
    cum = _dot_hi(causal.astype(F32), da_ref[...])
    cum_e = _dot_select(cum, ex, 3)
    last_e = cum_e[LC - 1:LC, :]
    cum_t = cum.T
    head = lax.broadcasted_iota(jnp.int32, (LC, W_C), 1) // HEAD_DIM
    bg = [b_ref[:, g * gn:(g + 1) * gn].astype(BF16) for g in range(N_GROUPS)]
    cg = [c_ref[:, g * gn:(g + 1) * gn].astype(BF16) for g in range(N_GROUPS)]
    cb = [lax.dot_general(cg[g], bg[g], (((1,), (1,)), ((), ())), preferred_element_type=F32)
          for g in range(N_GROUPS)]

    hb = st[...].astype(BF16)
    ys = [lax.dot_general(cg[g], hb, (((1,), (1,)), ((), ())), preferred_element_type=F32)
          for g in range(N_GROUPS)]
    y = jnp.where(head < H_C // N_GROUPS, ys[0], ys[1]) * jnp.exp(cum_e)
    for h in range(H_C):
        seg = cum[:, h:h + 1] - cum_t[h:h + 1, :]
        p = (cb[h // (H_C // N_GROUPS)] * jnp.exp(jnp.where(causal, seg, -jnp.inf))).astype(BF16)
        y = y + jnp.where(head == h, jnp.dot(p, xb, preferred_element_type=F32), 0.0)
    y_ref[...] = y

    xt = (x * jnp.exp(last_e - cum_e)).T.astype(BF16)
    upd = [jnp.dot(xt, bg[g], preferred_element_type=F32) for g in range(N_GROUPS)]
    rowi = lax.broadcasted_iota(jnp.int32, (W_C, gn), 0)
    sel = lax.broadcasted_iota(jnp.int32, (W_C, LANES), 1) == lax.broadcasted_iota(jnp.int32, (W_C, LANES), 0) // HEAD_DIM
    tot = jnp.sum(jnp.where(sel, cum[LC - 1:LC, :], 0.0), axis=1, keepdims=True)
    st[...] = st[...] * jnp.exp(tot) + jnp.where(rowi < half, upd[0], upd[1])
    h_ref[0] = st[...]


def _ssd_chunk(xdt, bm, cm, da, ex, n_seq, t_len):
    n_c = t_len // LC
    gn = N_GROUPS * N_STATE
    rows = lambda w: pl.BlockSpec((LC, w), lambda b, c: (b * n_c + c, 0))
    return pl.pallas_call(
        _ssd_chunk_kernel,
        grid=(n_seq, n_c),
        in_specs=[rows(W_C), rows(gn), rows(gn), rows(LANES), pl.BlockSpec((LANES, W_C), lambda b, c: (0, 0))],
        out_specs=[rows(W_C), pl.BlockSpec((1, W_C, N_STATE), lambda b, c: (b, 0, 0))],
        out_shape=[jax.ShapeDtypeStruct((n_seq * t_len, W_C), F32),
                   jax.ShapeDtypeStruct((n_seq, W_C, N_STATE), F32)],
        scratch_shapes=[pltpu.VMEM((W_C, N_STATE), F32)],
        compiler_params=_params(("parallel", "arbitrary")),
        name="ssd_chunk",
    )(xdt, bm, cm, da, ex)


def _post_kernel(x_ref, ya_ref, bonus_ref, ga_ref, ob_ref, gb_ref, yc_ref, xs_ref, z_ref, lnw_ref, lnb_ref,
                 rn_ref, dsk_ref, sn_ref, bda_ref, bdb_ref, wo_ref, out_ref):
    inv_hd = 1.0 / HEAD_DIM
    y = ya_ref[...]
    bda = bda_ref[...]
    mean = _dot_select(y, bda) * inv_hd
    yd = y - mean
    var = _dot_select(yd * yd, bda) * inv_hd
    ya = (yd * lax.rsqrt(var + GN_EPS) * lnw_ref[...] + lnb_ref[...] + bonus_ref[...]) * ga_ref[...]
    o = ob_ref[...]
    ms = _dot_select(o * o, bdb_ref[...]) * inv_hd
    gb = gb_ref[...]
    yb = (gb * _sigmoid(gb)) * (o * lax.rsqrt(ms + RMS_EPS) * rn_ref[...])
    z = z_ref[...]
    yc = (yc_ref[...] + dsk_ref[...] * xs_ref[...]) * (z * _sigmoid(z))
    yc = _rmsnorm(yc, sn_ref[...], GATED_NORM_EPS)
    out_ref[...] = (x_ref[...] + _dot(ya, wo_ref[0:W_A, :]) + _dot(yb, wo_ref[W_A:W_A + W_B, :])
                    + _dot(yc, wo_ref[W_A + W_B:, :]))


def _post(x, ya, bonus, ga, ob, cb, yc, xs, cc, p):
    n = x.shape[0]
    va = _full_spec((1, W_A))
    return pl.pallas_call(
        _post_kernel,
        grid=(n // TM,),
        in_specs=[_row_spec(D_MODEL), _row_spec(W_A), _row_spec(W_A), _row_spec(W_A), _row_spec(W_B),
                  _row_spec(W_B, 3), _row_spec(W_C), _row_spec(W_C), _row_spec(W_C, 0), va, va,
                  _full_spec((1, W_B)), va, va, _full_spec((W_A, W_A)), _full_spec((W_B, W_B)),
                  _full_spec((D_MODEL, D_MODEL))],
        out_specs=_row_spec(D_MODEL),
        out_shape=jax.ShapeDtypeStruct((n, D_MODEL), F32),
        compiler_params=_params(("parallel",)),
        name="post_outproj",
    )(x, ya, bonus, ga, ob, cb, yc, xs, cc, p["ln_w"], p["ln_b"], p["ret_norm"], p["d_skip"], p["ssm_norm"],
      p["bd_a"], p["bd_b"], p["w_out"])


def _ffn_kernel(*refs, gated, final_norm, n_e, n_sub):
    refs = list(refs)
    x_ref, nf_ref = refs[:2]
    pos = 2
    if gated:
        router_ref = refs[pos]
        pos += 1
    wg_ref, wu_ref, wd_ref = refs[pos:pos + 3]
    pos += 3
    if final_norm:
        nfin_ref = refs[pos]
        pos += 1
    out_ref = refs[pos]
    hb_s, acc_s = refs[pos + 1:pos + 3]
    if gated:
        gate_s, asg_s, pos_s, asg_t, pos_t = refs[pos + 3:pos + 8]
    e = pl.program_id(1)
    sr = x_ref.shape[0] // n_sub
    subs = [(slice(s * sr, (s + 1) * sr), slice(s * LANES, (s + 1) * LANES)) for s in range(n_sub)]

    @pl.when(e == 0)
    def _():
        acc_s[...] = jnp.zeros_like(acc_s)
        for rs, ts in subs:
            h = _rmsnorm(x_ref[rs, :], nf_ref[...], RMS_EPS)
            hb_s[rs, :] = h.astype(BF16)
            if gated:
                lane = lax.broadcasted_iota(jnp.int32, (sr, LANES), 1)
                logits = jnp.where(lane < N_EXPERTS, _dot_hi(h, router_ref[...]), -jnp.inf)
                p = jnp.exp(logits - jnp.max(logits, axis=-1, keepdims=True))
                p = p / jnp.sum(p, axis=-1, keepdims=True)
                p1 = jnp.max(p, axis=-1, keepdims=True)
                i1 = jnp.min(jnp.where(p == p1, lane, LANES), axis=-1, keepdims=True)
                rest = jnp.where(lane == i1, -1.0, p)
                p2 = jnp.max(rest, axis=-1, keepdims=True)
                i2 = jnp.min(jnp.where(rest == p2, lane, LANES), axis=-1, keepdims=True)
                gate_s[rs, :] = jnp.where(lane == i1, p1, jnp.where(lane == i2, p2, 0.0)) / (p1 + p2)
                assigned = jnp.where((lane == i1) | (lane == i2), 1.0, 0.0)
                earlier = (lax.broadcasted_iota(jnp.int32, (sr, sr), 0)
                           > lax.broadcasted_iota(jnp.int32, (sr, sr), 1))
                rank = jnp.dot(earlier.astype(BF16), assigned.astype(BF16), preferred_element_type=F32)
                asg_s[rs, :] = assigned
                pos_s[rs, :] = rank
                asg_t[ts, :] = assigned.T
                pos_t[ts, :] = rank.T

    if not gated:
        hb = hb_s[...]
        g = jnp.dot(hb, wg_ref[0], preferred_element_type=F32)
        u = jnp.dot(hb, wu_ref[0], preferred_element_type=F32)
        acc_s[...] += _dot(g * _sigmoid(g) * u, wd_ref[0])
    else:
        lane = lax.broadcasted_iota(jnp.int32, (sr, LANES), 1)
        mine = lane == e
        col = lambda ref, rs: jnp.sum(jnp.where(mine, ref[rs, :], 0.0), axis=-1, keepdims=True)
        cols = [(col(gate_s, rs), col(asg_s, rs), col(pos_s, rs)) for rs, _ in subs]
        rws = [(asg_t[pl.ds(s * LANES + e, 1), :], pos_t[pl.ds(s * LANES + e, 1), :])
               for s in range(n_sub)]
        most = functools.reduce(jnp.maximum, [jnp.sum(asg_r) for asg_r, _ in rws])
        ch = MOE_CHUNK
        n_chunks = (most.astype(jnp.int32) + ch - 1) // ch

        def chunk(c, carry):
            base = (c * ch).astype(F32)
            slot_r = lax.broadcasted_iota(jnp.int32, (ch, sr), 0).astype(F32) + base
            picked = []
            for (rs, _), (asg_r, pos_r) in zip(subs, rws):
                pick = jnp.where((pos_r == slot_r) & (asg_r > 0.0), 1.0, 0.0).astype(BF16)
                picked.append(jnp.dot(pick, hb_s[rs, :], preferred_element_type=F32).astype(BF16))
            rows = jnp.concatenate(picked, axis=0)
            g = jnp.dot(rows, wg_ref[0], preferred_element_type=F32)
            u = jnp.dot(rows, wu_ref[0], preferred_element_type=F32)
            o = _dot(g * _sigmoid(g) * u, wd_ref[0]).astype(BF16)
            slot_c = lax.broadcasted_iota(jnp.int32, (sr, ch), 1).astype(F32) + base
            for s, ((rs, _), (gate_c, asg_c, pos_c)) in enumerate(zip(subs, cols)):
                place = jnp.where((pos_c == slot_c) & (asg_c > 0.0), 1.0, 0.0).astype(BF16)
                back = jnp.dot(place, o[s * ch:(s + 1) * ch, :], preferred_element_type=F32)
                acc_s[rs, :] += gate_c * back
            return carry

        lax.fori_loop(0, n_chunks, chunk, 0)

    @pl.when(e == n_e - 1)
    def _():
        y = x_ref[...] + acc_s[...]
        if final_norm:
            y = _rmsnorm(y, nfin_ref[...], RMS_EPS)
        out_ref[...] = y


def _ffn(x, nf, wg, wu, wd, router=None, nfin=None):
    n = x.shape[0]
    n_e, _, dff = wg.shape
    gated = router is not None
    final_norm = nfin is not None
    n_sub, sub_rows = (MOE_SUB, MOE_ROWS) if gated else (1, TM)
    rows = n_sub * sub_rows
    assert n % rows == 0
    vec = pl.BlockSpec((1, D_MODEL), lambda i, e: (0, 0))
    args = [x, nf]
    specs = [pl.BlockSpec((rows, D_MODEL), lambda i, e: (i, 0)), vec]
    if gated:
        args.append(router)
        specs.append(pl.BlockSpec((D_MODEL, LANES), lambda i, e: (0, 0)))
    args += [wg, wu, wd]
    specs += [pl.BlockSpec((1, D_MODEL, dff), lambda i, e: (e, 0, 0)),
              pl.BlockSpec((1, D_MODEL, dff), lambda i, e: (e, 0, 0)),
              pl.BlockSpec((1, dff, D_MODEL), lambda i, e: (e, 0, 0))]
    if final_norm:
        args.append(nfin)
        specs.append(vec)
    scratch = [pltpu.VMEM((rows, D_MODEL), BF16), pltpu.VMEM((rows, D_MODEL), F32)]
    if gated:
        scratch += [pltpu.VMEM((rows, LANES), F32)] * 3 + [pltpu.VMEM((n_sub * LANES, sub_rows), F32)] * 2
    return pl.pallas_call(
        functools.partial(_ffn_kernel, gated=gated, final_norm=final_norm, n_e=n_e, n_sub=n_sub),
        grid=(n // rows, n_e),
        in_specs=specs,
        out_specs=pl.BlockSpec((rows, D_MODEL), lambda i, e: (i, 0)),
        out_shape=jax.ShapeDtypeStruct((n, D_MODEL), F32),
        scratch_shapes=scratch,
        compiler_params=_params(("parallel", "arbitrary")),
        name="moe" if gated else "ffn",
    )(*args)


def _short_to_lanes_kernel(*refs, n_ops, t_len):
    for x_ref, o_ref in zip(refs[:n_ops], refs[n_ops:]):
        for t in range(t_len):
            o_ref[t] = x_ref[pl.ds(t, LANES, stride=t_len), :].T


def _short_to_lanes(xs, t_len):
    n_chunks = [width // LANES for _, _, width in xs]
    grid = max(n_chunks)
    clamp = lambda c, n: jnp.minimum(c, n - 1)
    outs = pl.pallas_call(
        functools.partial(_short_to_lanes_kernel, n_ops=len(xs), t_len=t_len),
        grid=(grid,),
        in_specs=[pl.BlockSpec((LANES * t_len, LANES), lambda c, c0=col // LANES, n=n: (0, c0 + clamp(c, n)))
                  for (_, col, _), n in zip(xs, n_chunks)],
        out_specs=[pl.BlockSpec((t_len, LANES, LANES), lambda c, n=n: (0, clamp(c, n), 0)) for n in n_chunks],
        out_shape=[jax.ShapeDtypeStruct((t_len, width, LANES), F32) for _, _, width in xs],
        compiler_params=_params(("arbitrary",)),
        name="short_to_lanes",
    )(*[x for x, _, _ in xs])
    return outs


def _short_from_lanes_kernel(y_ref, o_ref, *, t_len):
    for t in range(t_len):
        o_ref[pl.ds(t, LANES, stride=t_len), :] = y_ref[t].T


def _short_from_lanes(y):
    t_len, w, _ = y.shape
    return pl.pallas_call(
        functools.partial(_short_from_lanes_kernel, t_len=t_len),
        grid=(w // LANES,),
        in_specs=[pl.BlockSpec((t_len, LANES, LANES), lambda c: (0, c, 0))],
        out_specs=pl.BlockSpec((LANES * t_len, LANES), lambda c: (0, c)),
        out_shape=jax.ShapeDtypeStruct((LANES * t_len, w), F32),
        compiler_params=_params(("parallel",)),
        name="short_from_lanes",
    )(y)


def _state_to_lanes_kernel(x_ref, o_ref, *, ni, nj, j_first):
    xt = x_ref[...].T
    if j_first:
        for j in range(nj):
            o_ref[pl.ds(j, ni, stride=nj), :] = xt[j * ni:(j + 1) * ni, :]
    else:
        o_ref[...] = xt


def _state_to_lanes(s_all, layer, ni, nj, j_first):
    depth, _, h = s_all.shape[:3]
    out = pl.pallas_call(
        functools.partial(_state_to_lanes_kernel, ni=ni, nj=nj, j_first=j_first),
        grid=(h,),
        in_specs=[pl.BlockSpec((LANES, ni * nj), lambda i: (layer, i))],
        out_specs=pl.BlockSpec((ni * nj, LANES), lambda i: (i, 0)),
        out_shape=jax.ShapeDtypeStruct((h * ni * nj, LANES), F32),
        compiler_params=_params(("parallel",)),
        name="state_to_lanes",
    )(s_all.reshape(depth * LANES, h * ni * nj))
    return out.reshape(h * ni, nj, LANES)


def _state_from_lanes_kernel(s_ref, o_ref, tmp, *, ni, nj, j_first):
    if j_first:
        for j in range(nj):
            tmp[j * ni:(j + 1) * ni, :] = s_ref[pl.ds(j, ni, stride=nj), :]
        o_ref[...] = tmp[...].T
    else:
        o_ref[...] = s_ref[...].T


def _state_from_lanes(s, n_heads, ni, nj, j_first):
    out = pl.pallas_call(
        functools.partial(_state_from_lanes_kernel, ni=ni, nj=nj, j_first=j_first),
        grid=(n_heads,),
        in_specs=[pl.BlockSpec((ni * nj, LANES), lambda i: (i, 0))],
        out_specs=pl.BlockSpec((LANES, ni * nj), lambda i: (0, i)),
        out_shape=jax.ShapeDtypeStruct((LANES, n_heads * ni * nj), F32),
        scratch_shapes=[pltpu.VMEM((ni * nj, LANES), F32)],
        compiler_params=_params(("parallel",)),
        name="state_from_lanes",
    )(s.reshape(n_heads * ni * nj, LANES))
    return out.reshape((LANES, n_heads, nj, ni) if j_first else (LANES, n_heads, ni, nj))


def _rwkv_long(feats, b):
    r, dec, k2, v, nkk, beta = feats
    t = r.shape[0] // b
    kl = HEAD_DIM // K_SPLIT
    assert b * H_PAD == HALF and t % TR == 0
    nkk_next = jnp.concatenate([nkk[1:], jnp.zeros((1, nkk.shape[1]), F32)], axis=0)
    q, d, a, n1, a2 = _long_to_lanes([r, dec, k2, nkk_next, beta], b, kl, True)
    (vv,) = _long_to_lanes([v], b, HEAD_DIM, False)
    y, s = _rwkv_long_scan(q, d, a, vv, n1, a2)
    s = s.reshape(kl, HEAD_DIM, K_SPLIT, b, H_PAD)[..., :H_A]
    return _long_from_lanes(y, b), jnp.transpose(s, (3, 4, 1, 2, 0)).reshape(b, H_A, HEAD_DIM, HEAD_DIM)


def _rwkv_short(feats, t_len, s0):
    r, dec, k2, v, nkk, beta = _short_to_lanes([(f, 0, W_A) for f in feats], t_len)
    hd = HEAD_DIM
    own = lambda h: h
    y, s = _scan("rwkv", H_A, hd, hd, (r, own), (dec, own), (k2, own), (v, own),
                 _state_to_lanes(*s0, hd, hd, True), n=(nkk, own), a2=(beta, own))
    return _short_from_lanes(y), _state_from_lanes(s, H_A, hd, hd, True)


def _ret_short(qh, kh, cb, t_len, s0):
    q, k, v = _short_to_lanes([(qh, 0, W_B), (kh, 0, W_B), (cb, 2 * W_B, W_B)], t_len)
    hd = HEAD_DIM
    gam = np.ones((SUBLANES, LANES), np.float32)
    gam[:H_B] = (1.0 - 2.0 ** (-5.0 - np.arange(H_B, dtype=np.float64)))[:, None]
    own = lambda h: h
    y, s = _scan("ret", H_B, hd, hd, (q, own), jnp.asarray(gam), (k, own), (v, own),
                 _state_to_lanes(*s0, hd, hd, False))
    return _short_from_lanes(y), _state_from_lanes(s, H_B, hd, hd, False)


def _ssd_short(xdt, bm, cm, dssm, t_len, s0):
    gn = N_GROUPS * N_STATE
    x, bl, cl, dl = _short_to_lanes([(xdt, 0, W_C), (bm, 0, gn), (cm, 0, gn), (dssm, 0, LANES)], t_len)
    group = lambda h: h // (H_C // N_GROUPS)
    y, s = _scan("ssd", H_C, N_STATE, HEAD_DIM, (cl, group), dl, (bl, group), (x, lambda h: h),
                 _state_to_lanes(*s0, N_STATE, HEAD_DIM, True))
    return _short_from_lanes(y), _state_from_lanes(s, H_C, N_STATE, HEAD_DIM, True)


def _block_diag_ones(width):
    idx = np.arange(width) // HEAD_DIM
    return jnp.asarray((idx[:, None] == idx[None, :]).astype(np.float32), BF16)


def _shifted(x, prev_rows, shift):
    b, t, c = x.shape
    p = prev_rows.shape[1]
    full = jnp.concatenate([prev_rows, x], axis=1)
    return full[:, p - shift:p - shift + t].reshape(b * t, c)


def _layer(x, b, p, st, rope, table_block):
    n = x.shape[0]
    t = n // b
    fresh = st is None
    tiles_per_seq = t // TM if fresh else 0
    ca, cb, cc = _in_proj(x, p["norm_mix"], p["w_in"])
    ca3 = ca.reshape(b, t, COLS_A)
    xbc_tail = cc.reshape(b, t, COLS_C_PAD)[:, -(CONV_W - 1):, W_C:W_C + CONV_DIM]
    if fresh:
        prev, shifted = None, None
        conv_new = xbc_tail
    else:
        prev = _shifted(ca3, st["shift"][:, None, :], 1)
        xbc = cc[:, W_C:W_C + CONV_DIM].reshape(b, t, CONV_DIM)
        shifted = [_shifted(xbc, st["conv"], j) for j in range(1, CONV_W)]
        conv_new = jnp.concatenate([st["conv"], xbc_tail], axis=1)[:, -(CONV_W - 1):]

    r, dec, k2, v, nkk, beta, ga, bonus = _rwkv_prep(ca, prev, p, tiles_per_seq)
    qh, kh = _ret_prep(cb, rope[0], rope[1], table_block)
    xdt, bm, cm, dssm, xs, da = _ssm_prep(cc, shifted, p, tiles_per_seq)

    feats = (r, dec, k2, v, nkk, beta)
    if fresh:
        ya, s_rwkv = _rwkv_long(feats, b)
        yb, s_full = _ret_chunk(qh, kh, cb, b, t)
        s_ret = jnp.stack([s_full[:, h * HEAD_DIM:(h + 1) * HEAD_DIM, h * HEAD_DIM:(h + 1) * HEAD_DIM]
                           for h in range(H_B)], axis=1)
        yc, s_ssm = _ssd_chunk(xdt, bm, cm, da, p["expand"], b, t)
        s_ssm = s_ssm.reshape(b, H_C, HEAD_DIM, N_STATE)
    else:
        assert b == LANES
        ya, s_rwkv = _rwkv_short(feats, t, st["rwkv"])
        yb, s_ret = _ret_short(qh, kh, cb, t, st["ret"])
        yc, s_ssm = _ssd_short(xdt, bm, cm, dssm, t, st["ssm"])

    x2 = _post(x, ya, bonus, ga, yb, cb, yc, xs, cc, p)
    x = _ffn(x2, p["norm_ffn"], p["wg"], p["wu"], p["wd"], router=p["router"], nfin=p["norm_final"])
    return x, (s_rwkv, ca3[:, -1], s_ret, s_ssm, conv_new)


def kernel(x_prompt, x_sample, state_rwkv, state_shift, state_ret, state_ssm, state_conv, norm_mix, w_in, rwkv_mu, rwkv_w0, rwkv_w_up, rwkv_a0, rwkv_a_up, rwkv_g_up, rwkv_k_k, rwkv_k_a, rwkv_r_k, rwkv_ln_w, rwkv_ln_b, ret_norm, ssm_conv_w, ssm_conv_b, ssm_dt_bias, ssm_a_log, ssm_d, ssm_norm, w_out, norm_ffn, ffn_w_gate, ffn_w_up, ffn_w_down, moe_router, moe_w_gate, moe_w_up, moe_w_down, norm_final):
    bp, tp, _ = x_prompt.shape
    bs, ts, _ = x_sample.shape
    depth = w_in.shape[0]
    assert tp % TM == 0 and tp % LC == 0 and (bs * ts) % TM == 0 and TM % ts == 0

    rope = _rope_tables(tp, ts)
    bd_a, bd_b = _block_diag_ones(W_A), _block_diag_ones(W_B)
    expand = np.zeros((LANES, W_C), np.float32)
    expand[np.arange(W_C) // HEAD_DIM, np.arange(W_C)] = 1.0
    expand = jnp.asarray(expand, BF16)
    row = lambda v: v.reshape(1, -1)
    pad_l = lambda v: jnp.pad(v, (0, LANES - v.shape[0])).reshape(1, LANES)

    xp = x_prompt.reshape(bp * tp, D_MODEL)
    xs = x_sample.reshape(bs * ts, D_MODEL)
    new_p, new_s = [], []
    for i in range(depth):
        j = i // 2
        p = dict(
            norm_mix=row(norm_mix[i]),
            w_in=jnp.pad(w_in[i], ((0, 0), (0, COLS_C_PAD - COLS_C))).astype(BF16),
            mu=row(rwkv_mu[i]), w0=row(rwkv_w0[i]), a0=row(rwkv_a0[i]), k_k=row(rwkv_k_k[i]),
            k_a=row(rwkv_k_a[i]), r_k=row(rwkv_r_k[i]),
            w_up=jnp.pad(rwkv_w_up[i], ((0, AAA_LORA), (0, 0))).astype(BF16),
            a_up=jnp.pad(rwkv_a_up[i], ((DECAY_LORA, 0), (0, 0))).astype(BF16),
            g_up=rwkv_g_up[i].astype(BF16), bd_a=bd_a, bd_b=bd_b,
            conv_w=ssm_conv_w[i], conv_b=row(ssm_conv_b[i]), dt_bias=pad_l(ssm_dt_bias[i]),
            a_log=pad_l(ssm_a_log[i]), expand=expand,
            ln_w=row(rwkv_ln_w[i]), ln_b=row(rwkv_ln_b[i]), ret_norm=row(ret_norm[i]),
            d_skip=row(jnp.repeat(ssm_d[i], HEAD_DIM)), ssm_norm=row(ssm_norm[i]), w_out=w_out[i].astype(BF16),
            norm_ffn=row(norm_ffn[i]), norm_final=row(norm_final) if i == depth - 1 else None)
        if i % 2 == 0:
            dff = ffn_w_gate.shape[-1] // 2
            p.update(router=None,
                     wg=ffn_w_gate[j].reshape(D_MODEL, 2, dff).transpose(1, 0, 2).astype(BF16),
                     wu=ffn_w_up[j].reshape(D_MODEL, 2, dff).transpose(1, 0, 2).astype(BF16),
                     wd=ffn_w_down[j].reshape(2, dff, D_MODEL).astype(BF16))
        else:
            p.update(router=jnp.pad(moe_router[j], ((0, 0), (0, LANES - N_EXPERTS))),
                     wg=moe_w_gate[j].astype(BF16), wu=moe_w_up[j].astype(BF16), wd=moe_w_down[j].astype(BF16))

        xp, st_p = _layer(xp, bp, p, None, rope, lambda t: t % (tp // TM))
        st = dict(rwkv=(state_rwkv, i), shift=state_shift[i], ret=(state_ret, i), ssm=(state_ssm, i),
                  conv=state_conv[i])
        xs, st_s = _layer(xs, bs, p, st, rope, lambda t: tp // TM)
        new_p.append(st_p)
        new_s.append(st_s)

    stack = lambda sts: tuple(jnp.stack(s) for s in zip(*sts))
    return (xp.reshape(bp, tp, D_MODEL), xs.reshape(bs, ts, D_MODEL)) + stack(new_p) + stack(new_s)
```

```python
import functools
import math

import numpy as np
import jax
import jax.numpy as jnp
from jax import lax
from jax.experimental import pallas as pl
from jax.experimental.pallas import tpu as pltpu

F32 = jnp.float32
BF16 = jnp.bfloat16
HIGHEST = lax.Precision.HIGHEST

LANES = 128
SUBLANES = 8
VMEM_LIMIT = 56 * 1024 * 1024

D_MODEL = 1024
HEAD_DIM = 64
H_A, H_B, H_C = 6, 4, 6
W_A, W_B, W_C = H_A * HEAD_DIM, H_B * HEAD_DIM, H_C * HEAD_DIM
DECAY_LORA, AAA_LORA, GATE_LORA = 64, 64, 128
COLS_A = 3 * W_A + DECAY_LORA + AAA_LORA + GATE_LORA
COLS_B = 4 * W_B
N_STATE, N_GROUPS, CONV_W = 128, 2, 4
CONV_DIM = W_C + 2 * N_GROUPS * N_STATE
COLS_C = W_C + CONV_DIM + H_C
COLS_C_PAD = 1408
ROPE_BASE = 10000.0
RMS_EPS = 1e-6
GN_EPS = 64e-5
GATED_NORM_EPS = 1e-5
N_EXPERTS = 8
PAST_LEN = 16384

TM = 512
MOE_ROWS = 512
MOE_CHUNK = 160
MOE_SUB = 2
LC = 256
H_PAD = 8
K_SPLIT = 2


def _dot(a, b):
    return jnp.dot(a.astype(BF16), b.astype(BF16), preferred_element_type=F32)


def _dot_hi(a, b):
    return jnp.dot(a, b, precision=HIGHEST, preferred_element_type=F32)


def _dot_select(a, sel, terms=2):
    out = None
    rest = a
    for _ in range(terms):
        piece = rest.astype(BF16)
        part = jnp.dot(piece, sel, preferred_element_type=F32)
        out = part if out is None else out + part
        rest = rest - piece.astype(F32)
    return out


def _sigmoid(x):
    return 1.0 / (1.0 + jnp.exp(-x))


def _softplus(x):
    return jnp.maximum(x, 0.0) + jnp.log1p(jnp.exp(-jnp.abs(x)))


def _rmsnorm(x, g, eps):
    return x * lax.rsqrt(jnp.mean(x * x, axis=-1, keepdims=True) + eps) * g


def _params(sem):
    return pltpu.CompilerParams(dimension_semantics=sem, vmem_limit_bytes=VMEM_LIMIT)


def _row_spec(width, col=0):
    return pl.BlockSpec((TM, width), lambda i, c=col: (i, c))


def _before_spec(width):
    return pl.BlockSpec((SUBLANES, width), lambda i: (jnp.maximum(i * (TM // SUBLANES) - 1, 0), 0))


def _full_spec(shape):
    nd = len(shape)
    return pl.BlockSpec(shape, lambda i, n=nd: (0,) * n)


def _shift_rows(x, before, j, first):
    rolled = pltpu.roll(x, j, 0)
    prev = jnp.where(first, 0.0, pltpu.roll(before, j, 0))
    row = lax.broadcasted_iota(jnp.int32, prev.shape, 0)
    top = jnp.where(row < j, prev, rolled[0:SUBLANES])
    return jnp.concatenate([top, rolled[SUBLANES:]], axis=0)


def _in_proj_kernel(x_ref, g_ref, w_ref, oa_ref, ob_ref, oc_ref):
    h = _rmsnorm(x_ref[...], g_ref[...], RMS_EPS).astype(BF16)
    oa_ref[...] = jnp.dot(h, w_ref[:, 0:COLS_A], preferred_element_type=F32)
    ob_ref[...] = jnp.dot(h, w_ref[:, COLS_A:COLS_A + COLS_B], preferred_element_type=F32)
    oc_ref[...] = jnp.dot(h, w_ref[:, COLS_A + COLS_B:], preferred_element_type=F32)


def _in_proj(x, g, w):
    n = x.shape[0]
    wtot = w.shape[1]
    return pl.pallas_call(
        _in_proj_kernel,
        grid=(n // TM,),
        in_specs=[_row_spec(D_MODEL), _full_spec((1, D_MODEL)), _full_spec((D_MODEL, wtot))],
        out_specs=[_row_spec(COLS_A), _row_spec(COLS_B), _row_spec(COLS_C_PAD)],
        out_shape=[jax.ShapeDtypeStruct((n, COLS_A), F32), jax.ShapeDtypeStruct((n, COLS_B), F32),
                   jax.ShapeDtypeStruct((n, COLS_C_PAD), F32)],
        compiler_params=_params(("parallel",)),
        name="in_proj",
    )(x, g, w)


def _rwkv_prep_kernel(c_ref, p_ref, mu_ref, w0_ref, a0_ref, kk_ref, ka_ref, rk_ref, wup_ref, aup_ref,
                      gup_ref, bd_ref, r_o, d_o, k_o, v_o, n_o, b_o, g_o, bonus_o, *, tiles_per_seq):
    c = c_ref[...]
    if tiles_per_seq:
        prev = _shift_rows(c, p_ref[...], 1, pl.program_id(0) % tiles_per_seq == 0)
    else:
        prev = p_ref[...]
    xm = c + (prev - c) * mu_ref[...]
    r = xm[:, 0:W_A]
    k = xm[:, W_A:2 * W_A]
    v = xm[:, 2 * W_A:3 * W_A]
    lora = xm[:, 3 * W_A:3 * W_A + DECAY_LORA + AAA_LORA]
    gd = xm[:, 3 * W_A + DECAY_LORA + AAA_LORA:]
    w = w0_ref[...] + _dot(jnp.tanh(lora), wup_ref[...])
    decay = jnp.exp(-math.exp(-0.5) * _sigmoid(w))
    a = _sigmoid(a0_ref[...] + _dot(lora, aup_ref[...]))
    g = _dot(_sigmoid(gd), gup_ref[...])
    bd = bd_ref[...]
    kk = k * kk_ref[...]
    kk = kk / jnp.maximum(jnp.sqrt(_dot_select(kk * kk, bd)), 1e-12)
    k2 = k * (1.0 + (a - 1.0) * ka_ref[...])
    for o_ref, val in ((r_o, r), (d_o, decay), (k_o, k2), (v_o, v), (n_o, -kk), (b_o, kk * a)):
        o_ref[:, 0:W_A] = val
        if o_ref.shape[1] > W_A:
            o_ref[:, W_A:] = jnp.zeros((TM, o_ref.shape[1] - W_A), F32)
    g_o[...] = g
    bonus_o[...] = _dot_select(r * k2 * rk_ref[...], bd) * v


def _rwkv_prep(ca, prev, p, tiles_per_seq):
    n = ca.shape[0]
    vec = _full_spec((1, W_A))
    lora_spec = _full_spec((DECAY_LORA + AAA_LORA, W_A))
    p_spec = _before_spec(COLS_A) if tiles_per_seq else _row_spec(COLS_A)
    w_scan = H_PAD * HEAD_DIM if tiles_per_seq else W_A
    widths = [w_scan] * 6 + [W_A] * 2
    return pl.pallas_call(
        functools.partial(_rwkv_prep_kernel, tiles_per_seq=tiles_per_seq),
        grid=(n // TM,),
        in_specs=[_row_spec(COLS_A), p_spec, _full_spec((1, COLS_A)), vec, vec, vec, vec, vec,
                  lora_spec, lora_spec, _full_spec((GATE_LORA, W_A)), _full_spec((W_A, W_A))],
        out_specs=[_row_spec(w) for w in widths],
        out_shape=[jax.ShapeDtypeStruct((n, w), F32) for w in widths],
        compiler_params=_params(("parallel",)),
        name="rwkv_prep",
    )(ca, ca if tiles_per_seq else prev, p["mu"], p["w0"], p["a0"], p["k_k"], p["k_a"], p["r_k"],
      p["w_up"], p["a_up"], p["g_up"], p["bd_a"])


def _rope_kernel(ang_ref, cos_o, sin_o):
    ang = ang_ref[...]
    lane = lax.broadcasted_iota(jnp.int32, ang.shape, 1)
    cos_o[...] = jnp.cos(ang)
    sin_o[...] = jnp.where((lane % HEAD_DIM) < (HEAD_DIM // 2), -jnp.sin(ang), jnp.sin(ang))


def _rope_tables(t_prompt, t_sample):
    theta = 1.0 / (ROPE_BASE ** jnp.linspace(0.0, 1.0, HEAD_DIM // 2, dtype=F32))
    pos = jnp.concatenate([jnp.arange(t_prompt, dtype=F32), PAST_LEN + (jnp.arange(TM) % t_sample).astype(F32)])
    ang = jnp.tile(pos[:, None] * theta[None, :], (1, W_B // (HEAD_DIM // 2)))
    n = ang.shape[0]
    return pl.pallas_call(
        _rope_kernel,
        grid=(n // TM,),
        in_specs=[_row_spec(W_B)],
        out_specs=[_row_spec(W_B)] * 2,
        out_shape=[jax.ShapeDtypeStruct((n, W_B), F32)] * 2,
        compiler_params=_params(("parallel",)),
        name="rope_tables",
    )(ang)


def _ret_prep_kernel(q_ref, k_ref, cos_ref, sin_ref, q_o, k_o):
    cos = cos_ref[...]
    sin = sin_ref[...]
    lane = lax.broadcasted_iota(jnp.int32, (TM, W_B), 1)
    first_half = (lane % HEAD_DIM) < (HEAD_DIM // 2)

    def rope(x):
        partner = jnp.where(first_half, pltpu.roll(x, W_B - HEAD_DIM // 2, 1), pltpu.roll(x, HEAD_DIM // 2, 1))
        return x * cos + partner * sin

    q_o[...] = rope(q_ref[...])
    k_o[...] = rope(k_ref[...]) * (HEAD_DIM ** -0.5)


def _ret_prep(cb, cos, sin, table_block):
    n = cb.shape[0]
    tab = pl.BlockSpec((TM, W_B), lambda i: (table_block(i), 0))
    return pl.pallas_call(
        _ret_prep_kernel,
        grid=(n // TM,),
        in_specs=[_row_spec(W_B, 0), _row_spec(W_B, 1), tab, tab],
        out_specs=[_row_spec(W_B)] * 2,
        out_shape=[jax.ShapeDtypeStruct((n, W_B), F32)] * 2,
        compiler_params=_params(("parallel",)),
        name="ret_prep",
    )(cb, cb, cos, sin)


def _ssm_prep_kernel(*refs, tiles_per_seq):
    n_shift_refs = 1 if tiles_per_seq else CONV_W - 1
    cc_ref = refs[0]
    shift_refs = refs[1:1 + n_shift_refs]
    cw_ref, cb_ref, dtb_ref, alog_ref, ex_ref, xdt_o, b_o, c_o, dec_o, xs_o, da_o = refs[1 + n_shift_refs:]
    cc = cc_ref[...]
    x0 = cc[:, W_C:W_C + CONV_DIM]
    if tiles_per_seq:
        before = shift_refs[0][:, W_C:W_C + CONV_DIM]
        first = pl.program_id(0) % tiles_per_seq == 0
        xs_prev = [_shift_rows(x0, before, j, first) for j in range(1, CONV_W)]
    else:
        xs_prev = [r[...] for r in shift_refs]
    cw = cw_ref[...]
    conv = x0 * cw[CONV_W - 1:CONV_W, :] + cb_ref[...]
    for j in range(1, CONV_W):
        conv = conv + xs_prev[j - 1] * cw[CONV_W - 1 - j:CONV_W - j, :]
    act = conv * _sigmoid(conv)
    xs = act[:, 0:W_C]
    dt = _softplus(cc[:, W_C + CONV_DIM:] + dtb_ref[...])
    da = dt * (-jnp.exp(alog_ref[...]))
    da_o[...] = da
    dec_o[...] = jnp.exp(da)
    xdt_o[...] = xs * _dot_select(dt, ex_ref[...])
    b_o[...] = act[:, W_C:W_C + N_GROUPS * N_STATE]
    c_o[...] = act[:, W_C + N_GROUPS * N_STATE:]
    xs_o[...] = xs


def _ssm_prep(cc, shifted, p, tiles_per_seq):
    n = cc.shape[0]
    gn = N_GROUPS * N_STATE
    if tiles_per_seq:
        shift_args, shift_specs = [cc], [_before_spec(COLS_C_PAD)]
    else:
        shift_args, shift_specs = list(shifted), [_row_spec(CONV_DIM)] * (CONV_W - 1)
    return pl.pallas_call(
        functools.partial(_ssm_prep_kernel, tiles_per_seq=tiles_per_seq),
        grid=(n // TM,),
        in_specs=[_row_spec(COLS_C_PAD)] + shift_specs + [
            _full_spec((CONV_W, CONV_DIM)), _full_spec((1, CONV_DIM)), _full_spec((1, LANES)),
            _full_spec((1, LANES)), _full_spec((LANES, W_C))],
        out_specs=[_row_spec(W_C), _row_spec(gn), _row_spec(gn), _row_spec(LANES), _row_spec(W_C),
                   _row_spec(LANES)],
        out_shape=[jax.ShapeDtypeStruct((n, W_C), F32), jax.ShapeDtypeStruct((n, gn), F32),
                   jax.ShapeDtypeStruct((n, gn), F32), jax.ShapeDtypeStruct((n, LANES), F32),
                   jax.ShapeDtypeStruct((n, W_C), F32), jax.ShapeDtypeStruct((n, LANES), F32)],
        compiler_params=_params(("parallel",)),
        name="ssm_prep",
    )(cc, *shift_args, p["conv_w"], p["conv_b"], p["dt_bias"], p["a_log"], p["expand"])


def _scan_kernel(*refs, mode, ni, tt_len, n_tt):
    if mode == "rwkv":
        q_ref, d_ref, a_ref, b_ref, n_ref, a2_ref, s0_ref, y_ref, so_ref, st = refs
    else:
        q_ref, d_ref, a_ref, b_ref, s0_ref, y_ref, so_ref, st = refs
    head = pl.program_id(0)
    tt = pl.program_id(1)

    @pl.when(tt == 0)
    def _():
        st[...] = s0_ref[...]

    def row(ref, t, i):
        return ref[t, pl.ds(i, 1), :]

    def step(t, carry):
        bv = b_ref[t]
        if mode == "rwkv":
            sa_parts = [jnp.zeros_like(bv), jnp.zeros_like(bv)]
            for i in range(ni):
                sa_parts[i % 2] = sa_parts[i % 2] + st[i] * row(n_ref, t, i)
            sa = sa_parts[0] + sa_parts[1]
        elif mode == "ssd":
            d = d_ref[t, pl.ds(head, 1), :]
        else:
            d = d_ref[pl.ds(head, 1), :]
        y_parts = [jnp.zeros_like(bv), jnp.zeros_like(bv)]
        for i in range(ni):
            if mode == "rwkv":
                s = st[i] * row(d_ref, t, i) + row(a_ref, t, i) * bv + row(a2_ref, t, i) * sa
            else:
                s = st[i] * d + row(a_ref, t, i) * bv
            st[i] = s
            y_parts[i % 2] = y_parts[i % 2] + s * row(q_ref, t, i)
        y_ref[t] = y_parts[0] + y_parts[1]
        return carry

    lax.fori_loop(0, tt_len, step, 0)

    @pl.when(tt == n_tt - 1)
    def _():
        so_ref[...] = st[...]


def _scan(mode, n_heads, ni, nj, q, d, a, b, s0, n=None, a2=None):
    t_len = b[0].shape[0]

    def rows(op, width):
        arr, block_of = op
        return arr, pl.BlockSpec((t_len, width, LANES), lambda h, t, f=block_of: (0, f(h), 0))

    ops = [rows(q, ni)]
    if mode == "rwkv":
        ops.append(rows(d, ni))
    elif mode == "ssd":
        ops.append((d, pl.BlockSpec((t_len, SUBLANES, LANES), lambda h, t: (0, 0, 0))))
    else:
        ops.append((d, pl.BlockSpec((SUBLANES, LANES), lambda h, t: (0, 0))))
    ops += [rows(a, ni), rows(b, nj)]
    if mode == "rwkv":
        ops += [rows(n, ni), rows(a2, ni)]
    s_spec = pl.BlockSpec((ni, nj, LANES), lambda h, t: (h, 0, 0))
    y_spec = pl.BlockSpec((t_len, nj, LANES), lambda h, t: (0, h, 0))
    return pl.pallas_call(
        functools.partial(_scan_kernel, mode=mode, ni=ni, tt_len=t_len, n_tt=1),
        grid=(n_heads, 1),
        in_specs=[spec for _, spec in ops] + [s_spec],
        out_specs=[y_spec, s_spec],
        out_shape=[jax.ShapeDtypeStruct((t_len, n_heads * nj, LANES), F32),
                   jax.ShapeDtypeStruct((n_heads * ni, nj, LANES), F32)],
        scratch_shapes=[pltpu.VMEM((ni, nj, LANES), F32)],
        compiler_params=_params(("parallel", "arbitrary")),
        name="scan_" + mode,
    )(*[arr for arr, _ in ops], s0)


def _rwkv_long_kernel(q_ref, d_ref, a_ref, b_ref, n1_ref, a2_ref, y_ref, so_ref, st, sa_s, dots_s, rows_s, *,
                      ni, tt_len, n_tt):
    tt = pl.program_id(1)

    @pl.when(tt == 0)
    def _():
        st[...] = jnp.zeros_like(st)
        sa_s[...] = jnp.zeros_like(sa_s)

    refs_i = (q_ref, d_ref, a_ref, n1_ref, a2_ref)
    pitch = tt_len + SUBLANES
    for k, ref in enumerate(refs_i):
        for i in range(ni):
            rows_s[i, k * pitch:k * pitch + tt_len, :] = ref[i]

    op_q, op_d, op_a, op_n1, op_a2 = range(len(refs_i))

    def row(k, t, i):
        return rows_s[i, pl.ds(k * pitch + t, 1), :]

    def fold(x):
        return x + pltpu.roll(x, LANES // 2, 1)

    for k, (u_ref, w_ref) in enumerate(((a_ref, q_ref), (a2_ref, q_ref), (a_ref, n1_ref), (a2_ref, n1_ref))):
        acc = u_ref[0] * w_ref[0]
        for i in range(1, ni):
            acc = acc + u_ref[i] * w_ref[i]
        dots_s[k] = fold(acc)

    def step(t, sa):
        bv = b_ref[t]
        aq, a2q, an, a2n = [dots_s[k, pl.ds(t, 1), :] for k in range(4)]
        y0 = jnp.zeros_like(bv)
        n0 = jnp.zeros_like(bv)
        for i in range(ni):
            sd = st[i] * row(op_d, t, i)
            y0 = y0 + sd * row(op_q, t, i)
            n0 = n0 + sd * row(op_n1, t, i)
            st[i] = sd + row(op_a, t, i) * bv + row(op_a2, t, i) * sa
        y_ref[t] = fold(y0) + bv * aq + sa * a2q
        return fold(n0) + bv * an + sa * a2n

    sa_s[...] = lax.fori_loop(0, tt_len, step, sa_s[...])

    @pl.when(tt == n_tt - 1)
    def _():
        so_ref[...] = st[...]


def _rwkv_long_scan(q, d, a, b, n1, a2, tt_len=128):
    ni, t_len, _ = q.shape
    nj = b.shape[1]
    n_tt = t_len // tt_len
    i_spec = pl.BlockSpec((ni, tt_len, LANES), lambda g, t: (0, t, 0))
    j_spec = pl.BlockSpec((tt_len, nj, LANES), lambda g, t: (t, 0, 0))
    s_spec = pl.BlockSpec((ni, nj, LANES), lambda g, t: (0, 0, 0))
    return pl.pallas_call(
        functools.partial(_rwkv_long_kernel, ni=ni, tt_len=tt_len, n_tt=n_tt),
        grid=(1, n_tt),
        in_specs=[i_spec, i_spec, i_spec, j_spec, i_spec, i_spec],
        out_specs=[j_spec, s_spec],
        out_shape=[jax.ShapeDtypeStruct((t_len, nj, LANES), F32), jax.ShapeDtypeStruct((ni, nj, LANES), F32)],
        scratch_shapes=[pltpu.VMEM((ni, nj, LANES), F32), pltpu.VMEM((nj, LANES), F32),
                        pltpu.VMEM((4, tt_len, LANES), F32),
                        pltpu.VMEM((ni, 5 * (tt_len + SUBLANES), LANES), F32)],
        compiler_params=_params(("parallel", "arbitrary")),
        name="scan_rwkv_long",
    )(q, d, a, b, n1, a2)


TR = 128
HALF = LANES // K_SPLIT


Y_PITCH = LANES + SUBLANES


def _to_lanes_kernel(*refs, n_ops, rows, per_half):
    x_refs, o_refs, y2d = refs[:n_ops], refs[n_ops:2 * n_ops], refs[2 * n_ops]
    n_b = x_refs[0].shape[0]
    for x_ref, o_ref in zip(x_refs, o_refs):
        for b in range(n_b):
            xt = x_ref[b].T
            for h in range(H_PAD):
                for s in range(K_SPLIT):
                    lane = s * HALF + b * H_PAD + h
                    f0 = h * HEAD_DIM + (s * rows if per_half else 0)
                    y2d[pl.ds(lane, rows, stride=Y_PITCH), :] = xt[f0:f0 + rows, :]
        for r in range(rows):
            slab = y2d[r * Y_PITCH:r * Y_PITCH + LANES, :].T
            if per_half:
                o_ref[r] = slab
            else:
                o_ref[pl.ds(r, TR, stride=rows), :] = slab


def _long_to_lanes(xs, b, rows, per_half):
    n_ops = len(xs)
    t = xs[0].shape[0] // b
    width = H_PAD * HEAD_DIM
    if per_half:
        o_spec = pl.BlockSpec((rows, TR, LANES), lambda i: (0, i, 0))
        o_shape = jax.ShapeDtypeStruct((rows, t, LANES), F32)
    else:
        o_spec = pl.BlockSpec((TR * rows, LANES), lambda i: (i, 0))
        o_shape = jax.ShapeDtypeStruct((t * rows, LANES), F32)
    outs = pl.pallas_call(
        functools.partial(_to_lanes_kernel, n_ops=n_ops, rows=rows, per_half=per_half),
        grid=(t // TR,),
        in_specs=[pl.BlockSpec((b, TR, width), lambda i: (0, i, 0))] * n_ops,
        out_specs=[o_spec] * n_ops,
        out_shape=[o_shape] * n_ops,
        scratch_shapes=[pltpu.VMEM((rows * Y_PITCH, TR), F32)],
        compiler_params=_params(("parallel",)),
        name="to_lanes",
    )(*[x.reshape(b, t, width) for x in xs])
    return outs if per_half else [o.reshape(t, rows, LANES) for o in outs]


def _from_lanes_kernel(y_ref, o_ref, z2d):
    for v in range(HEAD_DIM):
        z2d[pl.ds(v, LANES, stride=HEAD_DIM), :] = y_ref[pl.ds(v, TR, stride=HEAD_DIM), :].T
    for b in range(o_ref.shape[0]):
        r0 = b * H_PAD * HEAD_DIM
        o_ref[b] = z2d[r0:r0 + W_A, :].T


def _long_from_lanes(y, b):
    t = y.shape[0]
    out = pl.pallas_call(
        _from_lanes_kernel,
        grid=(t // TR,),
        in_specs=[pl.BlockSpec((TR * HEAD_DIM, LANES), lambda i: (i, 0))],
        out_specs=pl.BlockSpec((b, TR, W_A), lambda i: (0, i, 0)),
        out_shape=jax.ShapeDtypeStruct((b, t, W_A), F32),
        scratch_shapes=[pltpu.VMEM((LANES * HEAD_DIM, TR), F32)],
        compiler_params=_params(("parallel",)),
        name="from_lanes",
    )(y.reshape(t * HEAD_DIM, LANES))
    return out.reshape(b * t, W_A)


def _ret_chunk_kernel(q_ref, k_ref, v_ref, y_ref, s_ref, st):
    c = pl.program_id(1)

    @pl.when(c == 0)
    def _():
        st[...] = jnp.zeros_like(st)

    q = q_ref[...]
    k = k_ref[...]
    vb = v_ref[...].astype(BF16)
    kb = k.astype(BF16)
    row = lax.broadcasted_iota(jnp.int32, (LC, W_B), 0)
    head = lax.broadcasted_iota(jnp.int32, (LC, W_B), 1) // HEAD_DIM
    log_g = [math.log(1.0 - 2.0 ** (-5.0 - h)) for h in range(H_B)]
    lg = jnp.full((LC, W_B), log_g[0], F32)
    for h in range(1, H_B):
        lg = jnp.where(head == h, log_g[h], lg)
    rowf = row.astype(F32)
    diff = (lax.broadcasted_iota(jnp.int32, (LC, LC), 0) - lax.broadcasted_iota(jnp.int32, (LC, LC), 1))
    causal = diff >= 0
    difff = jnp.maximum(diff, 0).astype(F32)

    out = jnp.dot((q * jnp.exp(lg * (rowf + 1.0))).astype(BF16), st[...].astype(BF16), preferred_element_type=F32)
    intra = []
    for pair in range(W_B // LANES):
        cols = slice(pair * LANES, (pair + 1) * LANES)
        acc = jnp.zeros((LC, LANES), F32)
        for h in range(2 * pair, 2 * pair + 2):
            mine = lax.broadcasted_iota(jnp.int32, (LC, LANES), 1) // HEAD_DIM == h - 2 * pair
            qm = jnp.where(mine, q[:, cols], 0.0).astype(BF16)
            s = lax.dot_general(qm, kb[:, cols], (((1,), (1,)), ((), ())), preferred_element_type=F32)
            p = jnp.where(causal, s * jnp.exp(log_g[h] * difff), 0.0).astype(BF16)
            acc = acc + jnp.where(mine, jnp.dot(p, vb[:, cols], preferred_element_type=F32), 0.0)
        intra.append(acc)
    y_ref[...] = out + jnp.concatenate(intra, axis=1)

    kt = (k * jnp.exp(lg * (LC - 1.0 - rowf))).T.astype(BF16)
    kv = jnp.dot(kt, vb, preferred_element_type=F32)
    r2 = lax.broadcasted_iota(jnp.int32, (W_B, W_B), 0) // HEAD_DIM
    c2 = lax.broadcasted_iota(jnp.int32, (W_B, W_B), 1) // HEAD_DIM
    cdec = jnp.full((W_B, W_B), math.exp(log_g[0] * LC), F32)
    for h in range(1, H_B):
        cdec = jnp.where(r2 == h, math.exp(log_g[h] * LC), cdec)
    st[...] = st[...] * cdec + jnp.where(r2 == c2, kv, 0.0)
    s_ref[0] = st[...]


def _ret_chunk(q, k, cb, n_seq, t_len):
    n_c = t_len // LC
    rows = lambda col: pl.BlockSpec((LC, W_B), lambda b, c, col=col: (b * n_c + c, col))
    return pl.pallas_call(
        _ret_chunk_kernel,
        grid=(n_seq, n_c),
        in_specs=[rows(0), rows(0), rows(2)],
        out_specs=[rows(0), pl.BlockSpec((1, W_B, W_B), lambda b, c: (b, 0, 0))],
        out_shape=[jax.ShapeDtypeStruct((n_seq * t_len, W_B), F32), jax.ShapeDtypeStruct((n_seq, W_B, W_B), F32)],
        scratch_shapes=[pltpu.VMEM((W_B, W_B), F32)],
        compiler_params=_params(("parallel", "arbitrary")),
        name="ret_chunk",
    )(q, k, cb)


def _ssd_chunk_kernel(x_ref, b_ref, c_ref, da_ref, ex_ref, y_ref, h_ref, st):
    ci = pl.program_id(1)

    @pl.when(ci == 0)
    def _():
        st[...] = jnp.zeros_like(st)

    gn = N_STATE
    half = W_C // N_GROUPS
    ex = ex_ref[...]
    x = x_ref[...]
    xb = x.astype(BF16)
    ri = lax.broadcasted_iota(jnp.int32, (LC, LC), 0)
    cj = lax.broadcasted_iota(jnp.int32, (LC, LC), 1)
    causal = ri >= cj
    cum = _dot_hi(causal.astype(F32), da_ref[...])
    cum_e = _dot_select(cum, ex, 3)
    last_e = cum_e[LC - 1:LC, :]
    cum_t = cum.T
    head = lax.broadcasted_iota(jnp.int32, (LC, W_C), 1) // HEAD_DIM
    bg = [b_ref[:, g * gn:(g + 1) * gn].astype(BF16) for g in range(N_GROUPS)]
    cg = [c_ref[:, g * gn:(g + 1) * gn].astype(BF16) for g in range(N_GROUPS)]
    cb = [lax.dot_general(cg[g], bg[g], (((1,), (1,)), ((), ())), preferred_element_type=F32)
          for g in range(N_GROUPS)]

    hb = st[...].astype(BF16)
    ys = [lax.dot_general(cg[g], hb, (((1,), (1,)), ((), ())), preferred_element_type=F32)
          for g in range(N_GROUPS)]
    y = jnp.where(head < H_C // N_GROUPS, ys[0], ys[1]) * jnp.exp(cum_e)
    intra = []
    for pair in range(W_C // LANES):
        cols = slice(pair * LANES, (pair + 1) * LANES)
        acc = jnp.zeros((LC, LANES), F32)
        for h in range(2 * pair, 2 * pair + 2):
            seg = cum[:, h:h + 1] - cum_t[h:h + 1, :]
            p = (cb[h // (H_C // N_GROUPS)] * jnp.exp(jnp.where(causal, seg, -jnp.inf))).astype(BF16)
            mine = lax.broadcasted_iota(jnp.int32, (LC, LANES), 1) // HEAD_DIM == h - 2 * pair
            acc = acc + jnp.where(mine, jnp.dot(p, xb[:, cols], preferred_element_type=F32), 0.0)
        intra.append(acc)
    y_ref[...] = y + jnp.concatenate(intra, axis=1)

    xt = (x * jnp.exp(last_e - cum_e)).T.astype(BF16)
    upd = [jnp.dot(xt, bg[g], preferred_element_type=F32) for g in range(N_GROUPS)]
    rowi = lax.broadcasted_iota(jnp.int32, (W_C, gn), 0)
    sel = lax.broadcasted_iota(jnp.int32, (W_C, LANES), 1) == lax.broadcasted_iota(jnp.int32, (W_C, LANES), 0) // HEAD_DIM
    tot = jnp.sum(jnp.where(sel, cum[LC - 1:LC, :], 0.0), axis=1, keepdims=True)
    st[...] = st[...] * jnp.exp(tot) + jnp.where(rowi < half, upd[0], upd[1])
    h_ref[0] = st[...]


def _ssd_chunk(xdt, bm, cm, da, ex, n_seq, t_len):
    n_c = t_len // LC
    gn = N_GROUPS * N_STATE
    rows = lambda w: pl.BlockSpec((LC, w), lambda b, c: (b * n_c + c, 0))
    return pl.pallas_call(
        _ssd_chunk_kernel,
        grid=(n_seq, n_c),
        in_specs=[rows(W_C), rows(gn), rows(gn), rows(LANES), pl.BlockSpec((LANES, W_C), lambda b, c: (0, 0))],
        out_specs=[rows(W_C), pl.BlockSpec((1, W_C, N_STATE), lambda b, c: (b, 0, 0))],
        out_shape=[jax.ShapeDtypeStruct((n_seq * t_len, W_C), F32),
                   jax.ShapeDtypeStruct((n_seq, W_C, N_STATE), F32)],
        scratch_shapes=[pltpu.VMEM((W_C, N_STATE), F32)],
        compiler_params=_params(("parallel", "arbitrary")),
        name="ssd_chunk",
    )(xdt, bm, cm, da, ex)


def _post_kernel(x_ref, ya_ref, bonus_ref, ga_ref, ob_ref, gb_ref, yc_ref, xs_ref, z_ref, lnw_ref, lnb_ref,
                 rn_ref, dsk_ref, sn_ref, bda_ref, bdb_ref, wo_ref, out_ref):
    inv_hd = 1.0 / HEAD_DIM
    y = ya_ref[...]
    bda = bda_ref[...]
    mean = _dot_select(y, bda) * inv_hd
    yd = y - mean
    var = _dot_select(yd * yd, bda) * inv_hd
    ya = (yd * lax.rsqrt(var + GN_EPS) * lnw_ref[...] + lnb_ref[...] + bonus_ref[...]) * ga_ref[...]
    o = ob_ref[...]
    ms = _dot_select(o * o, bdb_ref[...]) * inv_hd
    gb = gb_ref[...]
    yb = (gb * _sigmoid(gb)) * (o * lax.rsqrt(ms + RMS_EPS) * rn_ref[...])
    z = z_ref[...]
    yc = (yc_ref[...] + dsk_ref[...] * xs_ref[...]) * (z * _sigmoid(z))
    yc = _rmsnorm(yc, sn_ref[...], GATED_NORM_EPS)
    out_ref[...] = (x_ref[...] + _dot(ya, wo_ref[0:W_A, :]) + _dot(yb, wo_ref[W_A:W_A + W_B, :])
                    + _dot(yc, wo_ref[W_A + W_B:, :]))


def _post(x, ya, bonus, ga, ob, cb, yc, xs, cc, p):
    n = x.shape[0]
    va = _full_spec((1, W_A))
    return pl.pallas_call(
        _post_kernel,
        grid=(n // TM,),
        in_specs=[_row_spec(D_MODEL), _row_spec(W_A), _row_spec(W_A), _row_spec(W_A), _row_spec(W_B),
                  _row_spec(W_B, 3), _row_spec(W_C), _row_spec(W_C), _row_spec(W_C, 0), va, va,
                  _full_spec((1, W_B)), va, va, _full_spec((W_A, W_A)), _full_spec((W_B, W_B)),
                  _full_spec((D_MODEL, D_MODEL))],
        out_specs=_row_spec(D_MODEL),
        out_shape=jax.ShapeDtypeStruct((n, D_MODEL), F32),
        compiler_params=_params(("parallel",)),
        name="post_outproj",
    )(x, ya, bonus, ga, ob, cb, yc, xs, cc, p["ln_w"], p["ln_b"], p["ret_norm"], p["d_skip"], p["ssm_norm"],
      p["bd_a"], p["bd_b"], p["w_out"])


def _ffn_kernel(*refs, gated, final_norm, n_e, n_sub):
    refs = list(refs)
    x_ref, nf_ref = refs[:2]
    pos = 2
    if gated:
        router_ref = refs[pos]
        pos += 1
    wg_ref, wu_ref, wd_ref = refs[pos:pos + 3]
    pos += 3
    if final_norm:
        nfin_ref = refs[pos]
        pos += 1
    out_ref = refs[pos]
    hb_s, acc_s = refs[pos + 1:pos + 3]
    if gated:
        gate_s, asg_s, pos_s, asg_t, pos_t = refs[pos + 3:pos + 8]
    e = pl.program_id(1)
    sr = x_ref.shape[0] // n_sub
    subs = [(slice(s * sr, (s + 1) * sr), slice(s * LANES, (s + 1) * LANES)) for s in range(n_sub)]

    @pl.when(e == 0)
    def _():
        acc_s[...] = jnp.zeros_like(acc_s)
        for rs, ts in subs:
            h = _rmsnorm(x_ref[rs, :], nf_ref[...], RMS_EPS)
            hb_s[rs, :] = h.astype(BF16)
            if gated:
                lane = lax.broadcasted_iota(jnp.int32, (sr, LANES), 1)
                logits = jnp.where(lane < N_EXPERTS, _dot_hi(h, router_ref[...]), -jnp.inf)
                p = jnp.exp(logits - jnp.max(logits, axis=-1, keepdims=True))
                p = p / jnp.sum(p, axis=-1, keepdims=True)
                p1 = jnp.max(p, axis=-1, keepdims=True)
                i1 = jnp.min(jnp.where(p == p1, lane, LANES), axis=-1, keepdims=True)
                rest = jnp.where(lane == i1, -1.0, p)
                p2 = jnp.max(rest, axis=-1, keepdims=True)
                i2 = jnp.min(jnp.where(rest == p2, lane, LANES), axis=-1, keepdims=True)
                gate_s[rs, :] = jnp.where(lane == i1, p1, jnp.where(lane == i2, p2, 0.0)) / (p1 + p2)
                assigned = jnp.where((lane == i1) | (lane == i2), 1.0, 0.0)
                earlier = (lax.broadcasted_iota(jnp.int32, (sr, sr), 0)
                           > lax.broadcasted_iota(jnp.int32, (sr, sr), 1))
                rank = jnp.dot(earlier.astype(BF16), assigned.astype(BF16), preferred_element_type=F32)
                asg_s[rs, :] = assigned
                pos_s[rs, :] = rank
                asg_t[ts, :] = assigned.T
                pos_t[ts, :] = rank.T

    if not gated:
        hb = hb_s[...]
        g = jnp.dot(hb, wg_ref[0], preferred_element_type=F32)
        u = jnp.dot(hb, wu_ref[0], preferred_element_type=F32)
        acc_s[...] += _dot(g * _sigmoid(g) * u, wd_ref[0])
    else:
        lane = lax.broadcasted_iota(jnp.int32, (sr, LANES), 1)
        mine = lane == e
        col = lambda ref, rs: jnp.sum(jnp.where(mine, ref[rs, :], 0.0), axis=-1, keepdims=True)
        cols = [(col(gate_s, rs), col(asg_s, rs), col(pos_s, rs)) for rs, _ in subs]
        rws = [(asg_t[pl.ds(s * LANES + e, 1), :], pos_t[pl.ds(s * LANES + e, 1), :])
               for s in range(n_sub)]
        most = functools.reduce(jnp.maximum, [jnp.sum(asg_r) for asg_r, _ in rws])
        ch = MOE_CHUNK
        n_chunks = (most.astype(jnp.int32) + ch - 1) // ch

        def chunk(c, carry):
            base = (c * ch).astype(F32)
            slot_r = lax.broadcasted_iota(jnp.int32, (ch, sr), 0).astype(F32) + base
            picked = []
            for (rs, _), (asg_r, pos_r) in zip(subs, rws):
                pick = jnp.where((pos_r == slot_r) & (asg_r > 0.0), 1.0, 0.0).astype(BF16)
                picked.append(jnp.dot(pick, hb_s[rs, :], preferred_element_type=F32).astype(BF16))
            rows = jnp.concatenate(picked, axis=0)
            g = jnp.dot(rows, wg_ref[0], preferred_element_type=F32)
            u = jnp.dot(rows, wu_ref[0], preferred_element_type=F32)
            o = _dot(g * _sigmoid(g) * u, wd_ref[0]).astype(BF16)
            slot_c = lax.broadcasted_iota(jnp.int32, (sr, ch), 1).astype(F32) + base
            for s, ((rs, _), (gate_c, asg_c, pos_c)) in enumerate(zip(subs, cols)):
                place = jnp.where((pos_c == slot_c) & (asg_c > 0.0), 1.0, 0.0).astype(BF16)
                back = jnp.dot(place, o[s * ch:(s + 1) * ch, :], preferred_element_type=F32)
                acc_s[rs, :] += gate_c * back
            return carry

        lax.fori_loop(0, n_chunks, chunk, 0)

    @pl.when(e == n_e - 1)
    def _():
        y = x_ref[...] + acc_s[...]
        if final_norm:
            y = _rmsnorm(y, nfin_ref[...], RMS_EPS)
        out_ref[...] = y


def _ffn(x, nf, wg, wu, wd, router=None, nfin=None):
    n = x.shape[0]
    n_e, _, dff = wg.shape
    gated = router is not None
    final_norm = nfin is not None
    n_sub, sub_rows = (MOE_SUB, MOE_ROWS) if gated else (1, TM)
    rows = n_sub * sub_rows
    assert n % rows == 0
    vec = pl.BlockSpec((1, D_MODEL), lambda i, e: (0, 0))
    args = [x, nf]
    specs = [pl.BlockSpec((rows, D_MODEL), lambda i, e: (i, 0)), vec]
    if gated:
        args.append(router)
        specs.append(pl.BlockSpec((D_MODEL, LANES), lambda i, e: (0, 0)))
    args += [wg, wu, wd]
    specs += [pl.BlockSpec((1, D_MODEL, dff), lambda i, e: (e, 0, 0)),
              pl.BlockSpec((1, D_MODEL, dff), lambda i, e: (e, 0, 0)),
              pl.BlockSpec((1, dff, D_MODEL), lambda i, e: (e, 0, 0))]
    if final_norm:
        args.append(nfin)
        specs.append(vec)
    scratch = [pltpu.VMEM((rows, D_MODEL), BF16), pltpu.VMEM((rows, D_MODEL), F32)]
    if gated:
        scratch += [pltpu.VMEM((rows, LANES), F32)] * 3 + [pltpu.VMEM((n_sub * LANES, sub_rows), F32)] * 2
    return pl.pallas_call(
        functools.partial(_ffn_kernel, gated=gated, final_norm=final_norm, n_e=n_e, n_sub=n_sub),
        grid=(n // rows, n_e),
        in_specs=specs,
        out_specs=pl.BlockSpec((rows, D_MODEL), lambda i, e: (i, 0)),
        out_shape=jax.ShapeDtypeStruct((n, D_MODEL), F32),
        scratch_shapes=scratch,
        compiler_params=_params(("parallel", "arbitrary")),
        name="moe" if gated else "ffn",
    )(*args)


def _short_to_lanes_kernel(*refs, n_ops, t_len):
    for x_ref, o_ref in zip(refs[:n_ops], refs[n_ops:]):
        for t in range(t_len):
            o_ref[t] = x_ref[pl.ds(t, LANES, stride=t_len), :].T


def _short_to_lanes(xs, t_len):
    n_chunks = [width // LANES for _, _, width in xs]
    grid = max(n_chunks)
    clamp = lambda c, n: jnp.minimum(c, n - 1)
    outs = pl.pallas_call(
        functools.partial(_short_to_lanes_kernel, n_ops=len(xs), t_len=t_len),
        grid=(grid,),
        in_specs=[pl.BlockSpec((LANES * t_len, LANES), lambda c, c0=col // LANES, n=n: (0, c0 + clamp(c, n)))
                  for (_, col, _), n in zip(xs, n_chunks)],
        out_specs=[pl.BlockSpec((t_len, LANES, LANES), lambda c, n=n: (0, clamp(c, n), 0)) for n in n_chunks],
        out_shape=[jax.ShapeDtypeStruct((t_len, width, LANES), F32) for _, _, width in xs],
        compiler_params=_params(("arbitrary",)),
        name="short_to_lanes",
    )(*[x for x, _, _ in xs])
    return outs


def _short_from_lanes_kernel(y_ref, o_ref, *, t_len):
    for t in range(t_len):
        o_ref[pl.ds(t, LANES, stride=t_len), :] = y_ref[t].T


def _short_from_lanes(y):
    t_len, w, _ = y.shape
    return pl.pallas_call(
        functools.partial(_short_from_lanes_kernel, t_len=t_len),
        grid=(w // LANES,),
        in_specs=[pl.BlockSpec((t_len, LANES, LANES), lambda c: (0, c, 0))],
        out_specs=pl.BlockSpec((LANES * t_len, LANES), lambda c: (0, c)),
        out_shape=jax.ShapeDtypeStruct((LANES * t_len, w), F32),
        compiler_params=_params(("parallel",)),
        name="short_from_lanes",
    )(y)


def _state_to_lanes_kernel(x_ref, o_ref, *, ni, nj, j_first):
    xt = x_ref[...].T
    if j_first:
        for j in range(nj):
            o_ref[pl.ds(j, ni, stride=nj), :] = xt[j * ni:(j + 1) * ni, :]
    else:
        o_ref[...] = xt


def _state_to_lanes(s_all, layer, ni, nj, j_first):
    depth, _, h = s_all.shape[:3]
    out = pl.pallas_call(
        functools.partial(_state_to_lanes_kernel, ni=ni, nj=nj, j_first=j_first),
        grid=(h,),
        in_specs=[pl.BlockSpec((LANES, ni * nj), lambda i: (layer, i))],
        out_specs=pl.BlockSpec((ni * nj, LANES), lambda i: (i, 0)),
        out_shape=jax.ShapeDtypeStruct((h * ni * nj, LANES), F32),
        compiler_params=_params(("parallel",)),
        name="state_to_lanes",
    )(s_all.reshape(depth * LANES, h * ni * nj))
    return out.reshape(h * ni, nj, LANES)


def _state_from_lanes_kernel(s_ref, o_ref, tmp, *, ni, nj, j_first):
    if j_first:
        for j in range(nj):
            tmp[j * ni:(j + 1) * ni, :] = s_ref[pl.ds(j, ni, stride=nj), :]
        o_ref[...] = tmp[...].T
    else:
        o_ref[...] = s_ref[...].T


def _state_from_lanes(s, n_heads, ni, nj, j_first):
    out = pl.pallas_call(
        functools.partial(_state_from_lanes_kernel, ni=ni, nj=nj, j_first=j_first),
        grid=(n_heads,),
        in_specs=[pl.BlockSpec((ni * nj, LANES), lambda i: (i, 0))],
        out_specs=pl.BlockSpec((LANES, ni * nj), lambda i: (0, i)),
        out_shape=jax.ShapeDtypeStruct((LANES, n_heads * ni * nj), F32),
        scratch_shapes=[pltpu.VMEM((ni * nj, LANES), F32)],
        compiler_params=_params(("parallel",)),
        name="state_from_lanes",
    )(s.reshape(n_heads * ni * nj, LANES))
    return out.reshape((LANES, n_heads, nj, ni) if j_first else (LANES, n_heads, ni, nj))


def _rwkv_long(feats, b):
    r, dec, k2, v, nkk, beta = feats
    t = r.shape[0] // b
    kl = HEAD_DIM // K_SPLIT
    assert b * H_PAD == HALF and t % TR == 0
    nkk_next = jnp.concatenate([nkk[1:], jnp.zeros((1, nkk.shape[1]), F32)], axis=0)
    q, d, a, n1, a2 = _long_to_lanes([r, dec, k2, nkk_next, beta], b, kl, True)
    (vv,) = _long_to_lanes([v], b, HEAD_DIM, False)
    y, s = _rwkv_long_scan(q, d, a, vv, n1, a2)
    s = s.reshape(kl, HEAD_DIM, K_SPLIT, b, H_PAD)[..., :H_A]
    return _long_from_lanes(y, b), jnp.transpose(s, (3, 4, 1, 2, 0)).reshape(b, H_A, HEAD_DIM, HEAD_DIM)


def _rwkv_short(feats, t_len, s0):
    r, dec, k2, v, nkk, beta = _short_to_lanes([(f, 0, W_A) for f in feats], t_len)
    hd = HEAD_DIM
    own = lambda h: h
    y, s = _scan("rwkv", H_A, hd, hd, (r, own), (dec, own), (k2, own), (v, own),
                 _state_to_lanes(*s0, hd, hd, True), n=(nkk, own), a2=(beta, own))
    return _short_from_lanes(y), _state_from_lanes(s, H_A, hd, hd, True)


def _ret_short(qh, kh, cb, t_len, s0):
    q, k, v = _short_to_lanes([(qh, 0, W_B), (kh, 0, W_B), (cb, 2 * W_B, W_B)], t_len)
    hd = HEAD_DIM
    gam = np.ones((SUBLANES, LANES), np.float32)
    gam[:H_B] = (1.0 - 2.0 ** (-5.0 - np.arange(H_B, dtype=np.float64)))[:, None]
    own = lambda h: h
    y, s = _scan("ret", H_B, hd, hd, (q, own), jnp.asarray(gam), (k, own), (v, own),
                 _state_to_lanes(*s0, hd, hd, False))
    return _short_from_lanes(y), _state_from_lanes(s, H_B, hd, hd, False)


def _ssd_short(xdt, bm, cm, dssm, t_len, s0):
    gn = N_GROUPS * N_STATE
    x, bl, cl, dl = _short_to_lanes([(xdt, 0, W_C), (bm, 0, gn), (cm, 0, gn), (dssm, 0, LANES)], t_len)
    group = lambda h: h // (H_C // N_GROUPS)
    y, s = _scan("ssd", H_C, N_STATE, HEAD_DIM, (cl, group), dl, (bl, group), (x, lambda h: h),
                 _state_to_lanes(*s0, N_STATE, HEAD_DIM, True))
    return _short_from_lanes(y), _state_from_lanes(s, H_C, N_STATE, HEAD_DIM, True)


def _block_diag_ones(width):
    idx = np.arange(width) // HEAD_DIM
    return jnp.asarray((idx[:, None] == idx[None, :]).astype(np.float32), BF16)


def _shifted(x, prev_rows, shift):
    b, t, c = x.shape
    p = prev_rows.shape[1]
    full = jnp.concatenate([prev_rows, x], axis=1)
    return full[:, p - shift:p - shift + t].reshape(b * t, c)


def _layer(x, b, p, st, rope, table_block):
    n = x.shape[0]
    t = n // b
    fresh = st is None
    tiles_per_seq = t // TM if fresh else 0
    ca, cb, cc = _in_proj(x, p["norm_mix"], p["w_in"])
    ca3 = ca.reshape(b, t, COLS_A)
    xbc_tail = cc.reshape(b, t, COLS_C_PAD)[:, -(CONV_W - 1):, W_C:W_C + CONV_DIM]
    if fresh:
        prev, shifted = None, None
        conv_new = xbc_tail
    else:
        prev = _shifted(ca3, st["shift"][:, None, :], 1)
        xbc = cc[:, W_C:W_C + CONV_DIM].reshape(b, t, CONV_DIM)
        shifted = [_shifted(xbc, st["conv"], j) for j in range(1, CONV_W)]
        conv_new = jnp.concatenate([st["conv"], xbc_tail], axis=1)[:, -(CONV_W - 1):]

    r, dec, k2, v, nkk, beta, ga, bonus = _rwkv_prep(ca, prev, p, tiles_per_seq)
    qh, kh = _ret_prep(cb, rope[0], rope[1], table_block)
    xdt, bm, cm, dssm, xs, da = _ssm_prep(cc, shifted, p, tiles_per_seq)

    feats = (r, dec, k2, v, nkk, beta)
    if fresh:
        ya, s_rwkv = _rwkv_long(feats, b)
        yb, s_full = _ret_chunk(qh, kh, cb, b, t)
        s_ret = jnp.stack([s_full[:, h * HEAD_DIM:(h + 1) * HEAD_DIM, h * HEAD_DIM:(h + 1) * HEAD_DIM]
                           for h in range(H_B)], axis=1)
        yc, s_ssm = _ssd_chunk(xdt, bm, cm, da, p["expand"], b, t)
        s_ssm = s_ssm.reshape(b, H_C, HEAD_DIM, N_STATE)
    else:
        assert b == LANES
        ya, s_rwkv = _rwkv_short(feats, t, st["rwkv"])
        yb, s_ret = _ret_short(qh, kh, cb, t, st["ret"])
        yc, s_ssm = _ssd_short(xdt, bm, cm, dssm, t, st["ssm"])

    x2 = _post(x, ya, bonus, ga, yb, cb, yc, xs, cc, p)
    x = _ffn(x2, p["norm_ffn"], p["wg"], p["wu"], p["wd"], router=p["router"], nfin=p["norm_final"])
    return x, (s_rwkv, ca3[:, -1], s_ret, s_ssm, conv_new)


def kernel(x_prompt, x_sample, state_rwkv, state_shift, state_ret, state_ssm, state_conv, norm_mix, w_in, rwkv_mu, rwkv_w0, rwkv_w_up, rwkv_a0, rwkv_a_up, rwkv_g_up, rwkv_k_k, rwkv_k_a, rwkv_r_k, rwkv_ln_w, rwkv_ln_b, ret_norm, ssm_conv_w, ssm_conv_b, ssm_dt_bias, ssm_a_log, ssm_d, ssm_norm, w_out, norm_ffn, ffn_w_gate, ffn_w_up, ffn_w_down, moe_router, moe_w_gate, moe_w_up, moe_w_down, norm_final):
    bp, tp, _ = x_prompt.shape
    bs, ts, _ = x_sample.shape
    depth = w_in.shape[0]
    assert tp % TM == 0 and tp % LC == 0 and (bs * ts) % TM == 0 and TM % ts == 0

    rope = _rope_tables(tp, ts)
    bd_a, bd_b = _block_diag_ones(W_A), _block_diag_ones(W_B)
    expand = np.zeros((LANES, W_C), np.float32)
    expand[np.arange(W_C) // HEAD_DIM, np.arange(W_C)] = 1.0
    expand = jnp.asarray(expand, BF16)
    row = lambda v: v.reshape(1, -1)
    pad_l = lambda v: jnp.pad(v, (0, LANES - v.shape[0])).reshape(1, LANES)

    xp = x_prompt.reshape(bp * tp, D_MODEL)
    xs = x_sample.reshape(bs * ts, D_MODEL)
    new_p, new_s = [], []
    for i in range(depth):
        j = i // 2
        p = dict(
            norm_mix=row(norm_mix[i]),
            w_in=jnp.pad(w_in[i], ((0, 0), (0, COLS_C_PAD - COLS_C))).astype(BF16),
            mu=row(rwkv_mu[i]), w0=row(rwkv_w0[i]), a0=row(rwkv_a0[i]), k_k=row(rwkv_k_k[i]),
            k_a=row(rwkv_k_a[i]), r_k=row(rwkv_r_k[i]),
            w_up=jnp.pad(rwkv_w_up[i], ((0, AAA_LORA), (0, 0))).astype(BF16),
            a_up=jnp.pad(rwkv_a_up[i], ((DECAY_LORA, 0), (0, 0))).astype(BF16),
            g_up=rwkv_g_up[i].astype(BF16), bd_a=bd_a, bd_b=bd_b,
            conv_w=ssm_conv_w[i], conv_b=row(ssm_conv_b[i]), dt_bias=pad_l(ssm_dt_bias[i]),
            a_log=pad_l(ssm_a_log[i]), expand=expand,
            ln_w=row(rwkv_ln_w[i]), ln_b=row(rwkv_ln_b[i]), ret_norm=row(ret_norm[i]),
            d_skip=row(jnp.repeat(ssm_d[i], HEAD_DIM)), ssm_norm=row(ssm_norm[i]), w_out=w_out[i].astype(BF16),
            norm_ffn=row(norm_ffn[i]), norm_final=row(norm_final) if i == depth - 1 else None)
        if i % 2 == 0:
            dff = ffn_w_gate.shape[-1] // 2
            p.update(router=None,
                     wg=ffn_w_gate[j].reshape(D_MODEL, 2, dff).transpose(1, 0, 2).astype(BF16),
                     wu=ffn_w_up[j].reshape(D_MODEL, 2, dff).transpose(1, 0, 2).astype(BF16),
                     wd=ffn_w_down[j].reshape(2, dff, D_MODEL).astype(BF16))
        else:
            p.update(router=jnp.pad(moe_router[j], ((0, 0), (0, LANES - N_EXPERTS))),
                     wg=moe_w_gate[j].astype(BF16), wu=moe_w_up[j].astype(BF16), wd=moe_w_down[j].astype(BF16))

        xp, st_p = _layer(xp, bp, p, None, rope, lambda t: t % (tp // TM))
        st = dict(rwkv=(state_rwkv, i), shift=state_shift[i], ret=(state_ret, i), ssm=(state_ssm, i),
                  conv=state_conv[i])
        xs, st_s = _layer(xs, bs, p, st, rope, lambda t: tp // TM)
        new_p.append(st_p)
        new_s.append(st_s)

    stack = lambda sts: tuple(jnp.stack(s) for s in zip(*sts))
    return (xp.reshape(bp, tp, D_MODEL), xs.reshape(bs, ts, D_MODEL)) + stack(new_p) + stack(new_s)
```

```python
import functools
import math

import numpy as np
import jax
import jax.numpy as jnp
from jax import lax
from jax.experimental import pallas as pl
from jax.experimental.pallas import tpu as pltpu

F32 = jnp.float32
BF16 = jnp.bfloat16
HIGHEST = lax.Precision.HIGHEST

LANES = 128
SUBLANES = 8
VMEM_LIMIT = 56 * 1024 * 1024

D_MODEL = 1024
HEAD_DIM = 64
H_A, H_B, H_C = 6, 4, 6
W_A, W_B, W_C = H_A * HEAD_DIM, H_B * HEAD_DIM, H_C * HEAD_DIM
DECAY_LORA, AAA_LORA, GATE_LORA = 64, 64, 128
COLS_A = 3 * W_A + DECAY_LORA + AAA_LORA + GATE_LORA
COLS_B = 4 * W_B
N_STATE, N_GROUPS, CONV_W = 128, 2, 4
CONV_DIM = W_C + 2 * N_GROUPS * N_STATE
COLS_C = W_C + CONV_DIM + H_C
COLS_C_PAD = 1408
ROPE_BASE = 10000.0
RMS_EPS = 1e-6
GN_EPS = 64e-5
GATED_NORM_EPS = 1e-5
N_EXPERTS = 8
PAST_LEN = 16384

TM = 512
MOE_ROWS = 512
MOE_CHUNK = 160
MOE_SUB = 2
LC = 128
H_PAD = 8
K_SPLIT = 2


def _dot(a, b):
    return jnp.dot(a.astype(BF16), b.astype(BF16), preferred_element_type=F32)


def _dot_hi(a, b):
    return jnp.dot(a, b, precision=HIGHEST, preferred_element_type=F32)


def _dot_select(a, sel, terms=2):
    out = None
    rest = a
    for _ in range(terms):
        piece = rest.astype(BF16)
        part = jnp.dot(piece, sel, preferred_element_type=F32)
        out = part if out is None else out + part
        rest = rest - piece.astype(F32)
    return out


def _sigmoid(x):
    return 1.0 / (1.0 + jnp.exp(-x))


def _softplus(x):
    return jnp.maximum(x, 0.0) + jnp.log1p(jnp.exp(-jnp.abs(x)))


def _rmsnorm(x, g, eps):
    return x * lax.rsqrt(jnp.mean(x * x, axis=-1, keepdims=True) + eps) * g


def _params(sem):
    return pltpu.CompilerParams(dimension_semantics=sem, vmem_limit_bytes=VMEM_LIMIT)


def _row_spec(width, col=0):
    return pl.BlockSpec((TM, width), lambda i, c=col: (i, c))


def _before_spec(width):
    return pl.BlockSpec((SUBLANES, width), lambda i: (jnp.maximum(i * (TM // SUBLANES) - 1, 0), 0))


def _full_spec(shape):
    nd = len(shape)
    return pl.BlockSpec(shape, lambda i, n=nd: (0,) * n)


def _shift_rows(x, before, j, first):
    rolled = pltpu.roll(x, j, 0)
    prev = jnp.where(first, 0.0, pltpu.roll(before, j, 0))
    row = lax.broadcasted_iota(jnp.int32, prev.shape, 0)
    top = jnp.where(row < j, prev, rolled[0:SUBLANES])
    return jnp.concatenate([top, rolled[SUBLANES:]], axis=0)


def _in_proj_kernel(x_ref, g_ref, w_ref, oa_ref, ob_ref, oc_ref):
    h = _rmsnorm(x_ref[...], g_ref[...], RMS_EPS).astype(BF16)
    oa_ref[...] = jnp.dot(h, w_ref[:, 0:COLS_A], preferred_element_type=F32)
    ob_ref[...] = jnp.dot(h, w_ref[:, COLS_A:COLS_A + COLS_B], preferred_element_type=F32)
    oc_ref[...] = jnp.dot(h, w_ref[:, COLS_A + COLS_B:], preferred_element_type=F32)


def _in_proj(x, g, w):
    n = x.shape[0]
    wtot = w.shape[1]
    return pl.pallas_call(
        _in_proj_kernel,
        grid=(n // TM,),
        in_specs=[_row_spec(D_MODEL), _full_spec((1, D_MODEL)), _full_spec((D_MODEL, wtot))],
        out_specs=[_row_spec(COLS_A), _row_spec(COLS_B), _row_spec(COLS_C_PAD)],
        out_shape=[jax.ShapeDtypeStruct((n, COLS_A), F32), jax.ShapeDtypeStruct((n, COLS_B), F32),
                   jax.ShapeDtypeStruct((n, COLS_C_PAD), F32)],
        compiler_params=_params(("parallel",)),
        name="in_proj",
    )(x, g, w)


def _rwkv_prep_kernel(c_ref, p_ref, mu_ref, w0_ref, a0_ref, kk_ref, ka_ref, rk_ref, wup_ref, aup_ref,
                      gup_ref, bd_ref, r_o, d_o, k_o, v_o, n_o, b_o, g_o, bonus_o, *, tiles_per_seq):
    c = c_ref[...]
    if tiles_per_seq:
        prev = _shift_rows(c, p_ref[...], 1, pl.program_id(0) % tiles_per_seq == 0)
    else:
        prev = p_ref[...]
    xm = c + (prev - c) * mu_ref[...]
    r = xm[:, 0:W_A]
    k = xm[:, W_A:2 * W_A]
    v = xm[:, 2 * W_A:3 * W_A]
    lora = xm[:, 3 * W_A:3 * W_A + DECAY_LORA + AAA_LORA]
    gd = xm[:, 3 * W_A + DECAY_LORA + AAA_LORA:]
    w = w0_ref[...] + _dot(jnp.tanh(lora), wup_ref[...])
    decay = jnp.exp(-math.exp(-0.5) * _sigmoid(w))
    a = _sigmoid(a0_ref[...] + _dot(lora, aup_ref[...]))
    g = _dot(_sigmoid(gd), gup_ref[...])
    bd = bd_ref[...]
    kk = k * kk_ref[...]
    kk = kk / jnp.maximum(jnp.sqrt(_dot_select(kk * kk, bd)), 1e-12)
    k2 = k * (1.0 + (a - 1.0) * ka_ref[...])
    for o_ref, val in ((r_o, r), (d_o, decay), (k_o, k2), (v_o, v), (n_o, -kk), (b_o, kk * a)):
        o_ref[:, 0:W_A] = val
        if o_ref.shape[1] > W_A:
            o_ref[:, W_A:] = jnp.zeros((TM, o_ref.shape[1] - W_A), F32)
    g_o[...] = g
    bonus_o[...] = _dot_select(r * k2 * rk_ref[...], bd) * v


def _rwkv_prep(ca, prev, p, tiles_per_seq):
    n = ca.shape[0]
    vec = _full_spec((1, W_A))
    lora_spec = _full_spec((DECAY_LORA + AAA_LORA, W_A))
    p_spec = _before_spec(COLS_A) if tiles_per_seq else _row_spec(COLS_A)
    w_scan = H_PAD * HEAD_DIM if tiles_per_seq else W_A
    widths = [w_scan] * 6 + [W_A] * 2
    return pl.pallas_call(
        functools.partial(_rwkv_prep_kernel, tiles_per_seq=tiles_per_seq),
        grid=(n // TM,),
        in_specs=[_row_spec(COLS_A), p_spec, _full_spec((1, COLS_A)), vec, vec, vec, vec, vec,
                  lora_spec, lora_spec, _full_spec((GATE_LORA, W_A)), _full_spec((W_A, W_A))],
        out_specs=[_row_spec(w) for w in widths],
        out_shape=[jax.ShapeDtypeStruct((n, w), F32) for w in widths],
        compiler_params=_params(("parallel",)),
        name="rwkv_prep",
    )(ca, ca if tiles_per_seq else prev, p["mu"], p["w0"], p["a0"], p["k_k"], p["k_a"], p["r_k"],
      p["w_up"], p["a_up"], p["g_up"], p["bd_a"])


def _rope_kernel(ang_ref, cos_o, sin_o):
    ang = ang_ref[...]
    lane = lax.broadcasted_iota(jnp.int32, ang.shape, 1)
    cos_o[...] = jnp.cos(ang)
    sin_o[...] = jnp.where((lane % HEAD_DIM) < (HEAD_DIM // 2), -jnp.sin(ang), jnp.sin(ang))


def _rope_tables(t_prompt, t_sample):
    theta = 1.0 / (ROPE_BASE ** jnp.linspace(0.0, 1.0, HEAD_DIM // 2, dtype=F32))
    pos = jnp.concatenate([jnp.arange(t_prompt, dtype=F32), PAST_LEN + (jnp.arange(TM) % t_sample).astype(F32)])
    ang = jnp.tile(pos[:, None] * theta[None, :], (1, W_B // (HEAD_DIM // 2)))
    n = ang.shape[0]
    return pl.pallas_call(
        _rope_kernel,
        grid=(n // TM,),
        in_specs=[_row_spec(W_B)],
        out_specs=[_row_spec(W_B)] * 2,
        out_shape=[jax.ShapeDtypeStruct((n, W_B), F32)] * 2,
        compiler_params=_params(("parallel",)),
        name="rope_tables",
    )(ang)


def _ret_prep_kernel(q_ref, k_ref, cos_ref, sin_ref, q_o, k_o):
    cos = cos_ref[...]
    sin = sin_ref[...]
    lane = lax.broadcasted_iota(jnp.int32, (TM, W_B), 1)
    first_half = (lane % HEAD_DIM) < (HEAD_DIM // 2)

    def rope(x):
        partner = jnp.where(first_half, pltpu.roll(x, W_B - HEAD_DIM // 2, 1), pltpu.roll(x, HEAD_DIM // 2, 1))
        return x * cos + partner * sin

    q_o[...] = rope(q_ref[...])
    k_o[...] = rope(k_ref[...]) * (HEAD_DIM ** -0.5)


def _ret_prep(cb, cos, sin, table_block):
    n = cb.shape[0]
    tab = pl.BlockSpec((TM, W_B), lambda i: (table_block, 0))
    return pl.pallas_call(
        _ret_prep_kernel,
        grid=(n // TM,),
        in_specs=[_row_spec(W_B, 0), _row_spec(W_B, 1), tab, tab],
        out_specs=[_row_spec(W_B)] * 2,
        out_shape=[jax.ShapeDtypeStruct((n, W_B), F32)] * 2,
        compiler_params=_params(("parallel",)),
        name="ret_prep",
    )(cb, cb, cos, sin)


def _ssm_prep_kernel(*refs, tiles_per_seq):
    n_shift_refs = 1 if tiles_per_seq else CONV_W - 1
    cc_ref = refs[0]
    shift_refs = refs[1:1 + n_shift_refs]
    cw_ref, cb_ref, dtb_ref, alog_ref, ex_ref, xdt_o, b_o, c_o, dec_o, xs_o, da_o = refs[1 + n_shift_refs:]
    cc = cc_ref[...]
    x0 = cc[:, W_C:W_C + CONV_DIM]
    if tiles_per_seq:
        before = shift_refs[0][:, W_C:W_C + CONV_DIM]
        first = pl.program_id(0) % tiles_per_seq == 0
        xs_prev = [_shift_rows(x0, before, j, first) for j in range(1, CONV_W)]
    else:
        xs_prev = [r[...] for r in shift_refs]
    cw = cw_ref[...]
    conv = x0 * cw[CONV_W - 1:CONV_W, :] + cb_ref[...]
    for j in range(1, CONV_W):
        conv = conv + xs_prev[j - 1] * cw[CONV_W - 1 - j:CONV_W - j, :]
    act = conv * _sigmoid(conv)
    xs = act[:, 0:W_C]
    dt = _softplus(cc[:, W_C + CONV_DIM:] + dtb_ref[...])
    da = dt * (-jnp.exp(alog_ref[...]))
    da_o[...] = da
    dec_o[...] = jnp.exp(da)
    xdt_o[...] = xs * _dot_select(dt, ex_ref[...])
    b_o[...] = act[:, W_C:W_C + N_GROUPS * N_STATE]
    c_o[...] = act[:, W_C + N_GROUPS * N_STATE:]
    xs_o[...] = xs


def _ssm_prep(cc, shifted, p, tiles_per_seq):
    n = cc.shape[0]
    gn = N_GROUPS * N_STATE
    if tiles_per_seq:
        shift_args, shift_specs = [cc], [_before_spec(COLS_C_PAD)]
    else:
        shift_args, shift_specs = list(shifted), [_row_spec(CONV_DIM)] * (CONV_W - 1)
    return pl.pallas_call(
        functools.partial(_ssm_prep_kernel, tiles_per_seq=tiles_per_seq),
        grid=(n // TM,),
        in_specs=[_row_spec(COLS_C_PAD)] + shift_specs + [
            _full_spec((CONV_W, CONV_DIM)), _full_spec((1, CONV_DIM)), _full_spec((1, LANES)),
            _full_spec((1, LANES)), _full_spec((LANES, W_C))],
        out_specs=[_row_spec(W_C), _row_spec(gn), _row_spec(gn), _row_spec(LANES), _row_spec(W_C),
                   _row_spec(LANES)],
        out_shape=[jax.ShapeDtypeStruct((n, W_C), F32), jax.ShapeDtypeStruct((n, gn), F32),
                   jax.ShapeDtypeStruct((n, gn), F32), jax.ShapeDtypeStruct((n, LANES), F32),
                   jax.ShapeDtypeStruct((n, W_C), F32), jax.ShapeDtypeStruct((n, LANES), F32)],
        compiler_params=_params(("parallel",)),
        name="ssm_prep",
    )(cc, *shift_args, p["conv_w"], p["conv_b"], p["dt_bias"], p["a_log"], p["expand"])


def _scan_kernel(*refs, mode, ni, tt_len, n_tt):
    if mode == "rwkv":
        q_ref, d_ref, a_ref, b_ref, n_ref, a2_ref, s0_ref, y_ref, so_ref, st = refs
    else:
        q_ref, d_ref, a_ref, b_ref, s0_ref, y_ref, so_ref, st = refs
    head = pl.program_id(0)
    tt = pl.program_id(1)

    @pl.when(tt == 0)
    def _():
        st[...] = s0_ref[...]

    def row(ref, t, i):
        return ref[t, pl.ds(i, 1), :]

    def step(t, carry):
        bv = b_ref[t]
        if mode == "rwkv":
            sa_parts = [jnp.zeros_like(bv), jnp.zeros_like(bv)]
            for i in range(ni):
                sa_parts[i % 2] = sa_parts[i % 2] + st[i] * row(n_ref, t, i)
            sa = sa_parts[0] + sa_parts[1]
        elif mode == "ssd":
            d = d_ref[t, pl.ds(head, 1), :]
        else:
            d = d_ref[pl.ds(head, 1), :]
        y_parts = [jnp.zeros_like(bv), jnp.zeros_like(bv)]
        for i in range(ni):
            if mode == "rwkv":
                s = st[i] * row(d_ref, t, i) + row(a_ref, t, i) * bv + row(a2_ref, t, i) * sa
            else:
                s = st[i] * d + row(a_ref, t, i) * bv
            st[i] = s
            y_parts[i % 2] = y_parts[i % 2] + s * row(q_ref, t, i)
        y_ref[t] = y_parts[0] + y_parts[1]
        return carry

    lax.fori_loop(0, tt_len, step, 0)

    @pl.when(tt == n_tt - 1)
    def _():
        so_ref[...] = st[...]


def _scan(mode, n_heads, ni, nj, q, d, a, b, s0, n=None, a2=None):
    t_len = b[0].shape[0]

    def rows(op, width):
        arr, block_of = op
        return arr, pl.BlockSpec((t_len, width, LANES), lambda h, t, f=block_of: (0, f(h), 0))

    ops = [rows(q, ni)]
    if mode == "rwkv":
        ops.append(rows(d, ni))
    elif mode == "ssd":
        ops.append((d, pl.BlockSpec((t_len, SUBLANES, LANES), lambda h, t: (0, 0, 0))))
    else:
        ops.append((d, pl.BlockSpec((SUBLANES, LANES), lambda h, t: (0, 0))))
    ops += [rows(a, ni), rows(b, nj)]
    if mode == "rwkv":
        ops += [rows(n, ni), rows(a2, ni)]
    s_spec = pl.BlockSpec((ni, nj, LANES), lambda h, t: (h, 0, 0))
    y_spec = pl.BlockSpec((t_len, nj, LANES), lambda h, t: (0, h, 0))
    return pl.pallas_call(
        functools.partial(_scan_kernel, mode=mode, ni=ni, tt_len=t_len, n_tt=1),
        grid=(n_heads, 1),
        in_specs=[spec for _, spec in ops] + [s_spec],
        out_specs=[y_spec, s_spec],
        out_shape=[jax.ShapeDtypeStruct((t_len, n_heads * nj, LANES), F32),
                   jax.ShapeDtypeStruct((n_heads * ni, nj, LANES), F32)],
        scratch_shapes=[pltpu.VMEM((ni, nj, LANES), F32)],
        compiler_params=_params(("parallel", "arbitrary")),
        name="scan_" + mode,
    )(*[arr for arr, _ in ops], s0)


def _rwkv_long_kernel(q_ref, d_ref, a_ref, b_ref, n1_ref, a2_ref, y_ref, so_ref, st, sa_s, dots_s, rows_s, *,
                      ni, tt_len, n_tt):
    tt = pl.program_id(1)

    @pl.when(tt == 0)
    def _():
        st[...] = jnp.zeros_like(st)
        sa_s[...] = jnp.zeros_like(sa_s)

    refs_i = (q_ref, d_ref, a_ref, n1_ref, a2_ref)
    pitch = tt_len + SUBLANES
    for k, ref in enumerate(refs_i):
        for i in range(ni):
            rows_s[i, k * pitch:k * pitch + tt_len, :] = ref[i]

    op_q, op_d, op_a, op_n1, op_a2 = range(len(refs_i))

    def row(k, t, i):
        return rows_s[i, pl.ds(k * pitch + t, 1), :]

    def fold(x):
        return x + pltpu.roll(x, LANES // 2, 1)

    for k, (u_ref, w_ref) in enumerate(((a_ref, q_ref), (a2_ref, q_ref), (a_ref, n1_ref), (a2_ref, n1_ref))):
        acc = u_ref[0] * w_ref[0]
        for i in range(1, ni):
            acc = acc + u_ref[i] * w_ref[i]
        dots_s[k] = fold(acc)

    nj = b_ref.shape[1]
    halves = (slice(0, nj // 2), slice(nj // 2, nj))

    def step(t, sas):
        aq, a2q, an, a2n = [dots_s[k, pl.ds(t, 1), :] for k in range(4)]
        new_sas = []
        for js, sa in zip(halves, sas):
            bv = b_ref[t, js, :]
            y0 = jnp.zeros_like(bv)
            n0 = jnp.zeros_like(bv)
            for i in range(ni):
                sd = st[i, js, :] * row(op_d, t, i)
                y0 = y0 + sd * row(op_q, t, i)
                n0 = n0 + sd * row(op_n1, t, i)
                st[i, js, :] = sd + row(op_a, t, i) * bv + row(op_a2, t, i) * sa
            y_ref[t, js, :] = fold(y0) + bv * aq + sa * a2q
            new_sas.append(fold(n0) + bv * an + sa * a2n)
        return tuple(new_sas)

    sas = lax.fori_loop(0, tt_len, step, tuple(sa_s[js, :] for js in halves))
    for js, sa in zip(halves, sas):
        sa_s[js, :] = sa

    @pl.when(tt == n_tt - 1)
    def _():
        so_ref[...] = st[...]


def _rwkv_long_scan(q, d, a, b, n1, a2, tt_len=128):
    ni, t_len, _ = q.shape
    nj = b.shape[1]
    n_tt = t_len // tt_len
    i_spec = pl.BlockSpec((ni, tt_len, LANES), lambda g, t: (0, t, 0))
    j_spec = pl.BlockSpec((tt_len, nj, LANES), lambda g, t: (t, 0, 0))
    s_spec = pl.BlockSpec((ni, nj, LANES), lambda g, t: (0, 0, 0))
    return pl.pallas_call(
        functools.partial(_rwkv_long_kernel, ni=ni, tt_len=tt_len, n_tt=n_tt),
        grid=(1, n_tt),
        in_specs=[i_spec, i_spec, i_spec, j_spec, i_spec, i_spec],
        out_specs=[j_spec, s_spec],
        out_shape=[jax.ShapeDtypeStruct((t_len, nj, LANES), F32), jax.ShapeDtypeStruct((ni, nj, LANES), F32)],
        scratch_shapes=[pltpu.VMEM((ni, nj, LANES), F32), pltpu.VMEM((nj, LANES), F32),
                        pltpu.VMEM((4, tt_len, LANES), F32),
                        pltpu.VMEM((ni, 5 * (tt_len + SUBLANES), LANES), F32)],
        compiler_params=_params(("parallel", "arbitrary")),
        name="scan_rwkv_long",
    )(q, d, a, b, n1, a2)


TR = 128
HALF = LANES // K_SPLIT


Y_PITCH = LANES + SUBLANES


def _to_lanes_kernel(*refs, n_ops, rows, per_half):
    x_refs, o_refs, y2d = refs[:n_ops], refs[n_ops:2 * n_ops], refs[2 * n_ops]
    n_b = x_refs[0].shape[0]
    for x_ref, o_ref in zip(x_refs, o_refs):
        for b in range(n_b):
            xt = x_ref[b].T
            for h in range(H_PAD):
                for s in range(K_SPLIT):
                    lane = s * HALF + b * H_PAD + h
                    f0 = h * HEAD_DIM + (s * rows if per_half else 0)
                    y2d[pl.ds(lane, rows, stride=Y_PITCH), :] = xt[f0:f0 + rows, :]
        for r in range(rows):
            slab = y2d[r * Y_PITCH:r * Y_PITCH + LANES, :].T
            if per_half:
                o_ref[r] = slab
            else:
                o_ref[pl.ds(r, TR, stride=rows), :] = slab


def _long_to_lanes(xs, b, rows, per_half):
    n_ops = len(xs)
    t = xs[0].shape[0] // b
    width = H_PAD * HEAD_DIM
    if per_half:
        o_spec = pl.BlockSpec((rows, TR, LANES), lambda i: (0, i, 0))
        o_shape = jax.ShapeDtypeStruct((rows, t, LANES), F32)
    else:
        o_spec = pl.BlockSpec((TR * rows, LANES), lambda i: (i, 0))
        o_shape = jax.ShapeDtypeStruct((t * rows, LANES), F32)
    outs = pl.pallas_call(
        functools.partial(_to_lanes_kernel, n_ops=n_ops, rows=rows, per_half=per_half),
        grid=(t // TR,),
        in_specs=[pl.BlockSpec((b, TR, width), lambda i: (0, i, 0))] * n_ops,
        out_specs=[o_spec] * n_ops,
        out_shape=[o_shape] * n_ops,
        scratch_shapes=[pltpu.VMEM((rows * Y_PITCH, TR), F32)],
        compiler_params=_params(("parallel",)),
        name="to_lanes",
    )(*[x.reshape(b, t, width) for x in xs])
    return outs if per_half else [o.reshape(t, rows, LANES) for o in outs]


def _from_lanes_kernel(y_ref, o_ref, z2d):
    for v in range(HEAD_DIM):
        z2d[pl.ds(v, LANES, stride=HEAD_DIM), :] = y_ref[pl.ds(v, TR, stride=HEAD_DIM), :].T
    for b in range(o_ref.shape[0]):
        r0 = b * H_PAD * HEAD_DIM
        o_ref[b] = z2d[r0:r0 + W_A, :].T


def _long_from_lanes(y, b):
    t = y.shape[0]
    out = pl.pallas_call(
        _from_lanes_kernel,
        grid=(t // TR,),
        in_specs=[pl.BlockSpec((TR * HEAD_DIM, LANES), lambda i: (i, 0))],
        out_specs=pl.BlockSpec((b, TR, W_A), lambda i: (0, i, 0)),
        out_shape=jax.ShapeDtypeStruct((b, t, W_A), F32),
        scratch_shapes=[pltpu.VMEM((LANES * HEAD_DIM, TR), F32)],
        compiler_params=_params(("parallel",)),
        name="from_lanes",
    )(y.reshape(t * HEAD_DIM, LANES))
    return out.reshape(b * t, W_A)


def _ret_chunk_kernel(q_ref, k_ref, v_ref, cos_ref, sin_ref, y_ref, s_ref, st):
    c = pl.program_id(1)

    @pl.when(c == 0)
    def _():
        st[...] = jnp.zeros_like(st)

    cos = cos_ref[...]
    sin = sin_ref[...]
    first_half = (lax.broadcasted_iota(jnp.int32, (LC, W_B), 1) % HEAD_DIM) < (HEAD_DIM // 2)

    def rope(x):
        partner = jnp.where(first_half, pltpu.roll(x, W_B - HEAD_DIM // 2, 1), pltpu.roll(x, HEAD_DIM // 2, 1))
        return x * cos + partner * sin

    q = rope(q_ref[...])
    k = rope(k_ref[...]) * (HEAD_DIM ** -0.5)
    vb = v_ref[...].astype(BF16)
    kb = k.astype(BF16)
    row = lax.broadcasted_iota(jnp.int32, (LC, W_B), 0)
    head = lax.broadcasted_iota(jnp.int32, (LC, W_B), 1) // HEAD_DIM
    log_g = [math.log(1.0 - 2.0 ** (-5.0 - h)) for h in range(H_B)]
    lg = jnp.full((LC, W_B), log_g[0], F32)
    for h in range(1, H_B):
        lg = jnp.where(head == h, log_g[h], lg)
    rowf = row.astype(F32)
    diff = (lax.broadcasted_iota(jnp.int32, (LC, LC), 0) - lax.broadcasted_iota(jnp.int32, (LC, LC), 1))
    causal = diff >= 0
    difff = jnp.maximum(diff, 0).astype(F32)

    out = jnp.dot((q * jnp.exp(lg * (rowf + 1.0))).astype(BF16), st[...].astype(BF16), preferred_element_type=F32)
    intra = []
    for pair in range(W_B // LANES):
        cols = slice(pair * LANES, (pair + 1) * LANES)
        acc = jnp.zeros((LC, LANES), F32)
        for h in range(2 * pair, 2 * pair + 2):
            mine = lax.broadcasted_iota(jnp.int32, (LC, LANES), 1) // HEAD_DIM == h - 2 * pair
            qm = jnp.where(mine, q[:, cols], 0.0).astype(BF16)
            s = lax.dot_general(qm, kb[:, cols], (((1,), (1,)), ((), ())), preferred_element_type=F32)
            p = jnp.where(causal, s * jnp.exp(log_g[h] * difff), 0.0).astype(BF16)
            acc = acc + jnp.where(mine, jnp.dot(p, vb[:, cols], preferred_element_type=F32), 0.0)
        intra.append(acc)
    y_ref[...] = out + jnp.concatenate(intra, axis=1)

    kt = (k * jnp.exp(lg * (LC - 1.0 - rowf))).T.astype(BF16)
    kv = jnp.dot(kt, vb, preferred_element_type=F32)
    r2 = lax.broadcasted_iota(jnp.int32, (W_B, W_B), 0) // HEAD_DIM
    c2 = lax.broadcasted_iota(jnp.int32, (W_B, W_B), 1) // HEAD_DIM
    cdec = jnp.full((W_B, W_B), math.exp(log_g[0] * LC), F32)
    for h in range(1, H_B):
        cdec = jnp.where(r2 == h, math.exp(log_g[h] * LC), cdec)
    st[...] = st[...] * cdec + jnp.where(r2 == c2, kv, 0.0)
    s_ref[0] = st[...]


def _ret_chunk(cb, cos, sin, n_seq, t_len):
    n_c = t_len // LC
    rows = lambda col: pl.BlockSpec((LC, W_B), lambda b, c, col=col: (b * n_c + c, col))
    tab = pl.BlockSpec((LC, W_B), lambda b, c: (c, 0))
    return pl.pallas_call(
        _ret_chunk_kernel,
        grid=(n_seq, n_c),
        in_specs=[rows(0), rows(1), rows(2), tab, tab],
        out_specs=[rows(0), pl.BlockSpec((1, W_B, W_B), lambda b, c: (b, 0, 0))],
        out_shape=[jax.ShapeDtypeStruct((n_seq * t_len, W_B), F32), jax.ShapeDtypeStruct((n_seq, W_B, W_B), F32)],
        scratch_shapes=[pltpu.VMEM((W_B, W_B), F32)],
        compiler_params=_params(("parallel", "arbitrary")),
        name="ret_chunk",
    )(cb, cb, cb, cos, sin)


def _ssd_chunk_kernel(x_ref, b_ref, c_ref, da_ref, ex_ref, y_ref, h_ref, st):
    ci = pl.program_id(1)

    @pl.when(ci == 0)
    def _():
        st[...] = jnp.zeros_like(st)

    gn = N_STATE
    half = W_C // N_GROUPS
    ex = ex_ref[...]
    x = x_ref[...]
    xb = x.astype(BF16)
    ri = lax.broadcasted_iota(jnp.int32, (LC, LC), 0)
    cj = lax.broadcasted_iota(jnp.int32, (LC, LC), 1)
    causal = ri >= cj
    cum = _dot_hi(causal.astype(F32), da_ref[...])
    cum_e = _dot_select(cum, ex, 3)
    last_e = cum_e[LC - 1:LC, :]
    cum_t = cum.T
    head = lax.broadcasted_iota(jnp.int32, (LC, W_C), 1) // HEAD_DIM
    bg = [b_ref[:, g * gn:(g + 1) * gn].astype(BF16) for g in range(N_GROUPS)]
    cg = [c_ref[:, g * gn:(g + 1) * gn].astype(BF16) for g in range(N_GROUPS)]
    cb = [lax.dot_general(cg[g], bg[g], (((1,), (1,)), ((), ())), preferred_element_type=F32)
          for g in range(N_GROUPS)]

    hb = st[...].astype(BF16)
    ys = [lax.dot_general(cg[g], hb, (((1,), (1,)), ((), ())), preferred_element_type=F32)
          for g in range(N_GROUPS)]
    y = jnp.where(head < H_C // N_GROUPS, ys[0], ys[1]) * jnp.exp(cum_e)
    intra = []
    for pair in range(W_C // LANES):
        cols = slice(pair * LANES, (pair + 1) * LANES)
        acc = jnp.zeros((LC, LANES), F32)
        for h in range(2 * pair, 2 * pair + 2):
            seg = cum[:, h:h + 1] - cum_t[h:h + 1, :]
            p = (cb[h // (H_C // N_GROUPS)] * jnp.exp(jnp.where(causal, seg, -jnp.inf))).astype(BF16)
            mine = lax.broadcasted_iota(jnp.int32, (LC, LANES), 1) // HEAD_DIM == h - 2 * pair
            acc = acc + jnp.where(mine, jnp.dot(p, xb[:, cols], preferred_element_type=F32), 0.0)
        intra.append(acc)
    y_ref[...] = y + jnp.concatenate(intra, axis=1)

    xt = (x * jnp.exp(last_e - cum_e)).T.astype(BF16)
    upd = [jnp.dot(xt, bg[g], preferred_element_type=F32) for g in range(N_GROUPS)]
    rowi = lax.broadcasted_iota(jnp.int32, (W_C, gn), 0)
    sel = lax.broadcasted_iota(jnp.int32, (W_C, LANES), 1) == lax.broadcasted_iota(jnp.int32, (W_C, LANES), 0) // HEAD_DIM
    tot = jnp.sum(jnp.where(sel, cum[LC - 1:LC, :], 0.0), axis=1, keepdims=True)
    st[...] = st[...] * jnp.exp(tot) + jnp.where(rowi < half, upd[0], upd[1])
    h_ref[0] = st[...]


def _ssd_chunk(xdt, bm, cm, da, ex, n_seq, t_len):
    n_c = t_len // LC
    gn = N_GROUPS * N_STATE
    rows = lambda w: pl.BlockSpec((LC, w), lambda b, c: (b * n_c + c, 0))
    return pl.pallas_call(
        _ssd_chunk_kernel,
        grid=(n_seq, n_c),
        in_specs=[rows(W_C), rows(gn), rows(gn), rows(LANES), pl.BlockSpec((LANES, W_C), lambda b, c: (0, 0))],
        out_specs=[rows(W_C), pl.BlockSpec((1, W_C, N_STATE), lambda b, c: (b, 0, 0))],
        out_shape=[jax.ShapeDtypeStruct((n_seq * t_len, W_C), F32),
                   jax.ShapeDtypeStruct((n_seq, W_C, N_STATE), F32)],
        scratch_shapes=[pltpu.VMEM((W_C, N_STATE), F32)],
        compiler_params=_params(("parallel", "arbitrary")),
        name="ssd_chunk",
    )(xdt, bm, cm, da, ex)


def _post_kernel(x_ref, ya_ref, bonus_ref, ga_ref, ob_ref, gb_ref, yc_ref, xs_ref, z_ref, lnw_ref, lnb_ref,
                 rn_ref, dsk_ref, sn_ref, bda_ref, bdb_ref, wo_ref, out_ref):
    inv_hd = 1.0 / HEAD_DIM
    y = ya_ref[...]
    bda = bda_ref[...]
    mean = _dot_select(y, bda) * inv_hd
    yd = y - mean
    var = _dot_select(yd * yd, bda) * inv_hd
    ya = (yd * lax.rsqrt(var + GN_EPS) * lnw_ref[...] + lnb_ref[...] + bonus_ref[...]) * ga_ref[...]
    o = ob_ref[...]
    ms = _dot_select(o * o, bdb_ref[...]) * inv_hd
    gb = gb_ref[...]
    yb = (gb * _sigmoid(gb)) * (o * lax.rsqrt(ms + RMS_EPS) * rn_ref[...])
    z = z_ref[...]
    yc = (yc_ref[...] + dsk_ref[...] * xs_ref[...]) * (z * _sigmoid(z))
    yc = _rmsnorm(yc, sn_ref[...], GATED_NORM_EPS)
    out_ref[...] = (x_ref[...] + _dot(ya, wo_ref[0:W_A, :]) + _dot(yb, wo_ref[W_A:W_A + W_B, :])
                    + _dot(yc, wo_ref[W_A + W_B:, :]))


def _post(x, ya, bonus, ga, ob, cb, yc, xs, cc, p):
    n = x.shape[0]
    va = _full_spec((1, W_A))
    return pl.pallas_call(
        _post_kernel,
        grid=(n // TM,),
        in_specs=[_row_spec(D_MODEL), _row_spec(W_A), _row_spec(W_A), _row_spec(W_A), _row_spec(W_B),
                  _row_spec(W_B, 3), _row_spec(W_C), _row_spec(W_C), _row_spec(W_C, 0), va, va,
                  _full_spec((1, W_B)), va, va, _full_spec((W_A, W_A)), _full_spec((W_B, W_B)),
                  _full_spec((D_MODEL, D_MODEL))],
        out_specs=_row_spec(D_MODEL),
        out_shape=jax.ShapeDtypeStruct((n, D_MODEL), F32),
        compiler_params=_params(("parallel",)),
        name="post_outproj",
    )(x, ya, bonus, ga, ob, cb, yc, xs, cc, p["ln_w"], p["ln_b"], p["ret_norm"], p["d_skip"], p["ssm_norm"],
      p["bd_a"], p["bd_b"], p["w_out"])


def _ffn_kernel(*refs, gated, final_norm, n_e, n_sub):
    refs = list(refs)
    x_ref, nf_ref = refs[:2]
    pos = 2
    if gated:
        router_ref = refs[pos]
        pos += 1
    wg_ref, wu_ref, wd_ref = refs[pos:pos + 3]
    pos += 3
    if final_norm:
        nfin_ref = refs[pos]
        pos += 1
    out_ref = refs[pos]
    hb_s, acc_s = refs[pos + 1:pos + 3]
    if gated:
        gate_s, asg_s, pos_s, asg_t, pos_t = refs[pos + 3:pos + 8]
    e = pl.program_id(1)
    sr = x_ref.shape[0] // n_sub
    subs = [(slice(s * sr, (s + 1) * sr), slice(s * LANES, (s + 1) * LANES)) for s in range(n_sub)]

    @pl.when(e == 0)
    def _():
        acc_s[...] = jnp.zeros_like(acc_s)
        for rs, ts in subs:
            h = _rmsnorm(x_ref[rs, :], nf_ref[...], RMS_EPS)
            hb_s[rs, :] = h.astype(BF16)
            if gated:
                lane = lax.broadcasted_iota(jnp.int32, (sr, LANES), 1)
                logits = jnp.where(lane < N_EXPERTS, _dot_hi(h, router_ref[...]), -jnp.inf)
                p = jnp.exp(logits - jnp.max(logits, axis=-1, keepdims=True))
                p = p / jnp.sum(p, axis=-1, keepdims=True)
                p1 = jnp.max(p, axis=-1, keepdims=True)
                i1 = jnp.min(jnp.where(p == p1, lane, LANES), axis=-1, keepdims=True)
                rest = jnp.where(lane == i1, -1.0, p)
                p2 = jnp.max(rest, axis=-1, keepdims=True)
                i2 = jnp.min(jnp.where(rest == p2, lane, LANES), axis=-1, keepdims=True)
                gate_s[rs, :] = jnp.where(lane == i1, p1, jnp.where(lane == i2, p2, 0.0)) / (p1 + p2)
                assigned = jnp.where((lane == i1) | (lane == i2), 1.0, 0.0)
                earlier = (lax.broadcasted_iota(jnp.int32, (sr, sr), 0)
                           > lax.broadcasted_iota(jnp.int32, (sr, sr), 1))
                rank = jnp.dot(earlier.astype(BF16), assigned.astype(BF16), preferred_element_type=F32)
                asg_s[rs, :] = assigned
                pos_s[rs, :] = rank
                asg_t[ts, :] = assigned.T
                pos_t[ts, :] = rank.T

    if not gated:
        hb = hb_s[...]
        g = jnp.dot(hb, wg_ref[0], preferred_element_type=F32)
        u = jnp.dot(hb, wu_ref[0], preferred_element_type=F32)
        acc_s[...] += _dot(g * _sigmoid(g) * u, wd_ref[0])
    else:
        lane = lax.broadcasted_iota(jnp.int32, (sr, LANES), 1)
        mine = lane == e
        col = lambda ref, rs: jnp.sum(jnp.where(mine, ref[rs, :], 0.0), axis=-1, keepdims=True)
        cols = [(col(gate_s, rs), col(asg_s, rs), col(pos_s, rs)) for rs, _ in subs]
        rws = [(asg_t[pl.ds(s * LANES + e, 1), :], pos_t[pl.ds(s * LANES + e, 1), :])
               for s in range(n_sub)]
        most = functools.reduce(jnp.maximum, [jnp.sum(asg_r) for asg_r, _ in rws])
        ch = MOE_CHUNK
        n_chunks = (most.astype(jnp.int32) + ch - 1) // ch

        def chunk(c, carry):
            base = (c * ch).astype(F32)
            slot_r = lax.broadcasted_iota(jnp.int32, (ch, sr), 0).astype(F32) + base
            picked = []
            for (rs, _), (asg_r, pos_r) in zip(subs, rws):
                pick = jnp.where((pos_r == slot_r) & (asg_r > 0.0), 1.0, 0.0).astype(BF16)
                picked.append(jnp.dot(pick, hb_s[rs, :], preferred_element_type=F32).astype(BF16))
            rows = jnp.concatenate(picked, axis=0)
            g = jnp.dot(rows, wg_ref[0], preferred_element_type=F32)
            u = jnp.dot(rows, wu_ref[0], preferred_element_type=F32)
            o = _dot(g * _sigmoid(g) * u, wd_ref[0]).astype(BF16)
            slot_c = lax.broadcasted_iota(jnp.int32, (sr, ch), 1).astype(F32) + base
            for s, ((rs, _), (gate_c, asg_c, pos_c)) in enumerate(zip(subs, cols)):
                place = jnp.where((pos_c == slot_c) & (asg_c > 0.0), 1.0, 0.0).astype(BF16)
                back = jnp.dot(place, o[s * ch:(s + 1) * ch, :], preferred_element_type=F32)
                acc_s[rs, :] += gate_c * back
            return carry

        lax.fori_loop(0, n_chunks, chunk, 0)

    @pl.when(e == n_e - 1)
    def _():
        y = x_ref[...] + acc_s[...]
        if final_norm:
            y = _rmsnorm(y, nfin_ref[...], RMS_EPS)
        out_ref[...] = y


def _ffn(x, nf, wg, wu, wd, router=None, nfin=None):
    n = x.shape[0]
    n_e, _, dff = wg.shape
    gated = router is not None
    final_norm = nfin is not None
    n_sub, sub_rows = (MOE_SUB, MOE_ROWS) if gated else (1, TM)
    rows = n_sub * sub_rows
    assert n % rows == 0
    vec = pl.BlockSpec((1, D_MODEL), lambda i, e: (0, 0))
    args = [x, nf]
    specs = [pl.BlockSpec((rows, D_MODEL), lambda i, e: (i, 0)), vec]
    if gated:
        args.append(router)
        specs.append(pl.BlockSpec((D_MODEL, LANES), lambda i, e: (0, 0)))
    args += [wg, wu, wd]
    specs += [pl.BlockSpec((1, D_MODEL, dff), lambda i, e: (e, 0, 0)),
              pl.BlockSpec((1, D_MODEL, dff), lambda i, e: (e, 0, 0)),
              pl.BlockSpec((1, dff, D_MODEL), lambda i, e: (e, 0, 0))]
    if final_norm:
        args.append(nfin)
        specs.append(vec)
    scratch = [pltpu.VMEM((rows, D_MODEL), BF16), pltpu.VMEM((rows, D_MODEL), F32)]
    if gated:
        scratch += [pltpu.VMEM((rows, LANES), F32)] * 3 + [pltpu.VMEM((n_sub * LANES, sub_rows), F32)] * 2
    return pl.pallas_call(
        functools.partial(_ffn_kernel, gated=gated, final_norm=final_norm, n_e=n_e, n_sub=n_sub),
        grid=(n // rows, n_e),
        in_specs=specs,
        out_specs=pl.BlockSpec((rows, D_MODEL), lambda i, e: (i, 0)),
        out_shape=jax.ShapeDtypeStruct((n, D_MODEL), F32),
        scratch_shapes=scratch,
        compiler_params=_params(("parallel", "arbitrary")),
        name="moe" if gated else "ffn",
    )(*args)


def _short_to_lanes_kernel(*refs, n_ops, t_len):
    for x_ref, o_ref in zip(refs[:n_ops], refs[n_ops:]):
        for t in range(t_len):
            o_ref[t] = x_ref[pl.ds(t, LANES, stride=t_len), :].T


def _short_to_lanes(xs, t_len):
    n_chunks = [width // LANES for _, _, width in xs]
    grid = max(n_chunks)
    clamp = lambda c, n: jnp.minimum(c, n - 1)
    outs = pl.pallas_call(
        functools.partial(_short_to_lanes_kernel, n_ops=len(xs), t_len=t_len),
        grid=(grid,),
        in_specs=[pl.BlockSpec((LANES * t_len, LANES), lambda c, c0=col // LANES, n=n: (0, c0 + clamp(c, n)))
                  for (_, col, _), n in zip(xs, n_chunks)],
        out_specs=[pl.BlockSpec((t_len, LANES, LANES), lambda c, n=n: (0, clamp(c, n), 0)) for n in n_chunks],
        out_shape=[jax.ShapeDtypeStruct((t_len, width, LANES), F32) for _, _, width in xs],
        compiler_params=_params(("arbitrary",)),
        name="short_to_lanes",
    )(*[x for x, _, _ in xs])
    return outs


def _short_from_lanes_kernel(y_ref, o_ref, *, t_len):
    for t in range(t_len):
        o_ref[pl.ds(t, LANES, stride=t_len), :] = y_ref[t].T


def _short_from_lanes(y):
    t_len, w, _ = y.shape
    return pl.pallas_call(
        functools.partial(_short_from_lanes_kernel, t_len=t_len),
        grid=(w // LANES,),
        in_specs=[pl.BlockSpec((t_len, LANES, LANES), lambda c: (0, c, 0))],
        out_specs=pl.BlockSpec((LANES * t_len, LANES), lambda c: (0, c)),
        out_shape=jax.ShapeDtypeStruct((LANES * t_len, w), F32),
        compiler_params=_params(("parallel",)),
        name="short_from_lanes",
    )(y)


def _state_to_lanes_kernel(x_ref, o_ref, *, ni, nj, j_first):
    xt = x_ref[...].T
    if j_first:
        for j in range(nj):
            o_ref[pl.ds(j, ni, stride=nj), :] = xt[j * ni:(j + 1) * ni, :]
    else:
        o_ref[...] = xt


def _state_to_lanes(s_all, layer, ni, nj, j_first):
    depth, _, h = s_all.shape[:3]
    out = pl.pallas_call(
        functools.partial(_state_to_lanes_kernel, ni=ni, nj=nj, j_first=j_first),
        grid=(h,),
        in_specs=[pl.BlockSpec((LANES, ni * nj), lambda i: (layer, i))],
        out_specs=pl.BlockSpec((ni * nj, LANES), lambda i: (i, 0)),
        out_shape=jax.ShapeDtypeStruct((h * ni * nj, LANES), F32),
        compiler_params=_params(("parallel",)),
        name="state_to_lanes",
    )(s_all.reshape(depth * LANES, h * ni * nj))
    return out.reshape(h * ni, nj, LANES)


def _state_from_lanes_kernel(s_ref, o_ref, tmp, *, ni, nj, j_first):
    if j_first:
        for j in range(nj):
            tmp[j * ni:(j + 1) * ni, :] = s_ref[pl.ds(j, ni, stride=nj), :]
        o_ref[...] = tmp[...].T
    else:
        o_ref[...] = s_ref[...].T


def _state_from_lanes(s, n_heads, ni, nj, j_first):
    out = pl.pallas_call(
        functools.partial(_state_from_lanes_kernel, ni=ni, nj=nj, j_first=j_first),
        grid=(n_heads,),
        in_specs=[pl.BlockSpec((ni * nj, LANES), lambda i: (i, 0))],
        out_specs=pl.BlockSpec((LANES, ni * nj), lambda i: (0, i)),
        out_shape=jax.ShapeDtypeStruct((LANES, n_heads * ni * nj), F32),
        scratch_shapes=[pltpu.VMEM((ni * nj, LANES), F32)],
        compiler_params=_params(("parallel",)),
        name="state_from_lanes",
    )(s.reshape(n_heads * ni * nj, LANES))
    return out.reshape((LANES, n_heads, nj, ni) if j_first else (LANES, n_heads, ni, nj))


def _rwkv_long(feats, b):
    r, dec, k2, v, nkk, beta = feats
    t = r.shape[0] // b
    kl = HEAD_DIM // K_SPLIT
    assert b * H_PAD == HALF and t % TR == 0
    nkk_next = jnp.concatenate([nkk[1:], jnp.zeros((1, nkk.shape[1]), F32)], axis=0)
    q, d, a, n1, a2 = _long_to_lanes([r, dec, k2, nkk_next, beta], b, kl, True)
    (vv,) = _long_to_lanes([v], b, HEAD_DIM, False)
    y, s = _rwkv_long_scan(q, d, a, vv, n1, a2)
    s = s.reshape(kl, HEAD_DIM, K_SPLIT, b, H_PAD)[..., :H_A]
    return _long_from_lanes(y, b), jnp.transpose(s, (3, 4, 1, 2, 0)).reshape(b, H_A, HEAD_DIM, HEAD_DIM)


def _rwkv_short(feats, t_len, s0):
    r, dec, k2, v, nkk, beta = _short_to_lanes([(f, 0, W_A) for f in feats], t_len)
    hd = HEAD_DIM
    own = lambda h: h
    y, s = _scan("rwkv", H_A, hd, hd, (r, own), (dec, own), (k2, own), (v, own),
                 _state_to_lanes(*s0, hd, hd, True), n=(nkk, own), a2=(beta, own))
    return _short_from_lanes(y), _state_from_lanes(s, H_A, hd, hd, True)


def _ret_short(qh, kh, cb, t_len, s0):
    q, k, v = _short_to_lanes([(qh, 0, W_B), (kh, 0, W_B), (cb, 2 * W_B, W_B)], t_len)
    hd = HEAD_DIM
    gam = np.ones((SUBLANES, LANES), np.float32)
    gam[:H_B] = (1.0 - 2.0 ** (-5.0 - np.arange(H_B, dtype=np.float64)))[:, None]
    own = lambda h: h
    y, s = _scan("ret", H_B, hd, hd, (q, own), jnp.asarray(gam), (k, own), (v, own),
                 _state_to_lanes(*s0, hd, hd, False))
    return _short_from_lanes(y), _state_from_lanes(s, H_B, hd, hd, False)


def _ssd_short(xdt, bm, cm, dssm, t_len, s0):
    gn = N_GROUPS * N_STATE
    x, bl, cl, dl = _short_to_lanes([(xdt, 0, W_C), (bm, 0, gn), (cm, 0, gn), (dssm, 0, LANES)], t_len)
    group = lambda h: h // (H_C // N_GROUPS)
    y, s = _scan("ssd", H_C, N_STATE, HEAD_DIM, (cl, group), dl, (bl, group), (x, lambda h: h),
                 _state_to_lanes(*s0, N_STATE, HEAD_DIM, True))
    return _short_from_lanes(y), _state_from_lanes(s, H_C, N_STATE, HEAD_DIM, True)


def _block_diag_ones(width):
    idx = np.arange(width) // HEAD_DIM
    return jnp.asarray((idx[:, None] == idx[None, :]).astype(np.float32), BF16)


def _shifted(x, prev_rows, shift):
    b, t, c = x.shape
    p = prev_rows.shape[1]
    full = jnp.concatenate([prev_rows, x], axis=1)
    return full[:, p - shift:p - shift + t].reshape(b * t, c)


def _layer(x, b, p, st, rope, table_block):
    n = x.shape[0]
    t = n // b
    fresh = st is None
    tiles_per_seq = t // TM if fresh else 0
    ca, cb, cc = _in_proj(x, p["norm_mix"], p["w_in"])
    ca3 = ca.reshape(b, t, COLS_A)
    xbc_tail = cc.reshape(b, t, COLS_C_PAD)[:, -(CONV_W - 1):, W_C:W_C + CONV_DIM]
    if fresh:
        prev, shifted = None, None
        conv_new = xbc_tail
    else:
        prev = _shifted(ca3, st["shift"][:, None, :], 1)
        xbc = cc[:, W_C:W_C + CONV_DIM].reshape(b, t, CONV_DIM)
        shifted = [_shifted(xbc, st["conv"], j) for j in range(1, CONV_W)]
        conv_new = jnp.concatenate([st["conv"], xbc_tail], axis=1)[:, -(CONV_W - 1):]

    r, dec, k2, v, nkk, beta, ga, bonus = _rwkv_prep(ca, prev, p, tiles_per_seq)
    xdt, bm, cm, dssm, xs, da = _ssm_prep(cc, shifted, p, tiles_per_seq)

    feats = (r, dec, k2, v, nkk, beta)
    if fresh:
        ya, s_rwkv = _rwkv_long(feats, b)
        yb, s_full = _ret_chunk(cb, rope[0], rope[1], b, t)
        s_ret = jnp.stack([s_full[:, h * HEAD_DIM:(h + 1) * HEAD_DIM, h * HEAD_DIM:(h + 1) * HEAD_DIM]
                           for h in range(H_B)], axis=1)
        yc, s_ssm = _ssd_chunk(xdt, bm, cm, da, p["expand"], b, t)
        s_ssm = s_ssm.reshape(b, H_C, HEAD_DIM, N_STATE)
    else:
        assert b == LANES
        qh, kh = _ret_prep(cb, rope[0], rope[1], table_block)
        ya, s_rwkv = _rwkv_short(feats, t, st["rwkv"])
        yb, s_ret = _ret_short(qh, kh, cb, t, st["ret"])
        yc, s_ssm = _ssd_short(xdt, bm, cm, dssm, t, st["ssm"])

    x2 = _post(x, ya, bonus, ga, yb, cb, yc, xs, cc, p)
    x = _ffn(x2, p["norm_ffn"], p["wg"], p["wu"], p["wd"], router=p["router"], nfin=p["norm_final"])
    return x, (s_rwkv, ca3[:, -1], s_ret, s_ssm, conv_new)


def kernel(x_prompt, x_sample, state_rwkv, state_shift, state_ret, state_ssm, state_conv, norm_mix, w_in, rwkv_mu, rwkv_w0, rwkv_w_up, rwkv_a0, rwkv_a_up, rwkv_g_up, rwkv_k_k, rwkv_k_a, rwkv_r_k, rwkv_ln_w, rwkv_ln_b, ret_norm, ssm_conv_w, ssm_conv_b, ssm_dt_bias, ssm_a_log, ssm_d, ssm_norm, w_out, norm_ffn, ffn_w_gate, ffn_w_up, ffn_w_down, moe_router, moe_w_gate, moe_w_up, moe_w_down, norm_final):
    bp, tp, _ = x_prompt.shape
    bs, ts, _ = x_sample.shape
    depth = w_in.shape[0]
    assert tp % TM == 0 and tp % LC == 0 and (bs * ts) % TM == 0 and TM % ts == 0

    rope = _rope_tables(tp, ts)
    bd_a, bd_b = _block_diag_ones(W_A), _block_diag_ones(W_B)
    expand = np.zeros((LANES, W_C), np.float32)
    expand[np.arange(W_C) // HEAD_DIM, np.arange(W_C)] = 1.0
    expand = jnp.asarray(expand, BF16)
    row = lambda v: v.reshape(1, -1)
    pad_l = lambda v: jnp.pad(v, (0, LANES - v.shape[0])).reshape(1, LANES)

    xp = x_prompt.reshape(bp * tp, D_MODEL)
    xs = x_sample.reshape(bs * ts, D_MODEL)
    new_p, new_s = [], []
    for i in range(depth):
        j = i // 2
        p = dict(
            norm_mix=row(norm_mix[i]),
            w_in=jnp.pad(w_in[i], ((0, 0), (0, COLS_C_PAD - COLS_C))).astype(BF16),
            mu=row(rwkv_mu[i]), w0=row(rwkv_w0[i]), a0=row(rwkv_a0[i]), k_k=row(rwkv_k_k[i]),
            k_a=row(rwkv_k_a[i]), r_k=row(rwkv_r_k[i]),
            w_up=jnp.pad(rwkv_w_up[i], ((0, AAA_LORA), (0, 0))).astype(BF16),
            a_up=jnp.pad(rwkv_a_up[i], ((DECAY_LORA, 0), (0, 0))).astype(BF16),
            g_up=rwkv_g_up[i].astype(BF16), bd_a=bd_a, bd_b=bd_b,
            conv_w=ssm_conv_w[i], conv_b=row(ssm_conv_b[i]), dt_bias=pad_l(ssm_dt_bias[i]),
            a_log=pad_l(ssm_a_log[i]), expand=expand,
            ln_w=row(rwkv_ln_w[i]), ln_b=row(rwkv_ln_b[i]), ret_norm=row(ret_norm[i]),
            d_skip=row(jnp.repeat(ssm_d[i], HEAD_DIM)), ssm_norm=row(ssm_norm[i]), w_out=w_out[i].astype(BF16),
            norm_ffn=row(norm_ffn[i]), norm_final=row(norm_final) if i == depth - 1 else None)
        if i % 2 == 0:
            dff = ffn_w_gate.shape[-1] // 2
            p.update(router=None,
                     wg=ffn_w_gate[j].reshape(D_MODEL, 2, dff).transpose(1, 0, 2).astype(BF16),
                     wu=ffn_w_up[j].reshape(D_MODEL, 2, dff).transpose(1, 0, 2).astype(BF16),
                     wd=ffn_w_down[j].reshape(2, dff, D_MODEL).astype(BF16))
        else:
            p.update(router=jnp.pad(moe_router[j], ((0, 0), (0, LANES - N_EXPERTS))),
                     wg=moe_w_gate[j].astype(BF16), wu=moe_w_up[j].astype(BF16), wd=moe_w_down[j].astype(BF16))

        xp, st_p = _layer(xp, bp, p, None, rope, None)
        st = dict(rwkv=(state_rwkv, i), shift=state_shift[i], ret=(state_ret, i), ssm=(state_ssm, i),
                  conv=state_conv[i])
        xs, st_s = _layer(xs, bs, p, st, rope, tp // TM)
        new_p.append(st_p)
        new_s.append(st_s)

    stack = lambda sts: tuple(jnp.stack(s) for s in zip(*sts))
    return (xp.reshape(bp, tp, D_MODEL), xs.reshape(bs, ts, D_MODEL)) + stack(new_p) + stack(new_s)
```
